```python
import math
import jax, jax.numpy as jnp
from jax import lax
import numpy as np

D_MODEL = 2048
BATCH = 32
SEQ = 256
DEPTH = 4
DEC_BATCH = 4
DEC_SEQ = 4096
PAST_LEN = 512

GRID_W = 64
N_MIXERS = 4
N_HEADS = 16
N_KV_HEADS = 4
HEAD_DIM = D_MODEL // N_HEADS
KV_GROUP = N_HEADS // N_KV_HEADS
QKV_DIM = (N_HEADS + 2 * N_KV_HEADS) * HEAD_DIM
ROPE_THETA = 10000.0
Q_BLOCK = 128
WINDOW = 128
POOL_WINDOWS = (2, 4, 8, 16)
POOL_GROUP = D_MODEL // len(POOL_WINDOWS)
HYENA_ORDER = 2
HYENA_EMB_BANDS = 16
HYENA_EMB_DIM = 1 + 2 * HYENA_EMB_BANDS
HYENA_FILTER_HIDDEN = 64
N_EXPERTS = 16
N_EXPERT_GROUPS = 4
EXPERTS_PER_GROUP = N_EXPERTS // N_EXPERT_GROUPS
TOP_K = 2
D_EXPERT = 512
NORM_EPS = 1e-6
NEG_INF = -1e30
F32 = jnp.float32

kernel_name = 'hybrid_prefix_diffusion_step'


def rmsnorm(x, g):
    xf = x.astype(F32)
    y = xf * lax.rsqrt(jnp.mean(xf * xf, axis=-1, keepdims=True) + NORM_EPS)
    return (y * g.astype(F32)).astype(x.dtype)


def adaln_params(cond, w, b):
    m = jax.nn.silu(cond) @ w + b
    return jnp.split(m[:, None, :], 6, axis=-1)


def modulate(h, shift, scale):
    return h * (1 + scale) + shift


def axial_rope(x):
    L = x.shape[1]
    rows = L // GRID_W
    row = jnp.repeat(jnp.arange(rows, dtype=F32), GRID_W)
    col = jnp.tile(jnp.arange(GRID_W, dtype=F32), rows)
    axis_dim = HEAD_DIM // 2
    inv_freq = ROPE_THETA ** (-jnp.arange(0, axis_dim, 2, dtype=F32) / axis_dim)
    xf = x.astype(F32)

    def rot(xa, pos):
        ang = pos[:, None] * inv_freq[None, :]
        cos = jnp.cos(ang)[None, :, None, :]
        sin = jnp.sin(ang)[None, :, None, :]
        x1, x2 = jnp.split(xa, 2, axis=-1)
        return jnp.concatenate([x1 * cos - x2 * sin, x2 * cos + x1 * sin], axis=-1)

    out = jnp.concatenate([rot(xf[..., :axis_dim], row), rot(xf[..., axis_dim:], col)], axis=-1)
    return out.astype(x.dtype)


def _softmax(s, sink):
    if sink is None:
        return jax.nn.softmax(s, axis=-1)
    sk = sink.astype(F32).reshape(1, N_KV_HEADS, KV_GROUP, 1, 1)
    m = jnp.maximum(jnp.max(s, axis=-1, keepdims=True), sk)
    p = jnp.exp(s - m)
    return p / (jnp.sum(p, axis=-1, keepdims=True) + jnp.exp(sk - m))


def _split_q_blocks(q):
    B, L = q.shape[:2]
    nb = L // Q_BLOCK
    return q.reshape(B, nb, Q_BLOCK, N_KV_HEADS, KV_GROUP, HEAD_DIM).transpose(1, 0, 2, 3, 4, 5)


def _merge_q_blocks(o):
    nb, B = o.shape[:2]
    return o.transpose(1, 0, 2, 3, 4, 5).reshape(B, nb * Q_BLOCK, N_HEADS * HEAD_DIM)


def dense_attention(q, k, v, sink):
    scale = HEAD_DIM ** -0.5

    def one_block(qb):
        s = jnp.einsum('bqngd,bknd->bngqk', qb, k, preferred_element_type=F32) * scale
        p = _softmax(s, sink).astype(v.dtype)
        return jnp.einsum('bngqk,bknd->bqngd', p, v)

    return _merge_q_blocks(lax.map(one_block, _split_q_blocks(q)))


def windowed_attention(q, k, v, k_ctx, v_ctx, sink):
    L = q.shape[1]
    span = Q_BLOCK + 2 * WINDOW
    pad = ((0, 0), (WINDOW, WINDOW), (0, 0), (0, 0))
    kp = jnp.pad(k, pad)
    vp = jnp.pad(v, pad)
    scale = HEAD_DIM ** -0.5
    qblocks = _split_q_blocks(q)

    def one_block(args):
        i, qb = args
        start = i * Q_BLOCK
        kw = lax.dynamic_slice_in_dim(kp, start, span, axis=1)
        vw = lax.dynamic_slice_in_dim(vp, start, span, axis=1)
        qpos = start + jnp.arange(Q_BLOCK)
        kpos = start - WINDOW + jnp.arange(span)
        valid = ((jnp.abs(kpos[None, :] - qpos[:, None]) <= WINDOW)
                 & (kpos >= 0)[None, :] & (kpos < L)[None, :])
        s_loc = jnp.einsum('bqngd,bknd->bngqk', qb, kw, preferred_element_type=F32) * scale
        s_loc = jnp.where(valid, s_loc, NEG_INF)
        s_ctx = jnp.einsum('bqngd,bknd->bngqk', qb, k_ctx, preferred_element_type=F32) * scale
        p = _softmax(jnp.concatenate([s_loc, s_ctx], axis=-1), sink).astype(v.dtype)
        return (jnp.einsum('bngqk,bknd->bqngd', p[..., :span], vw)
                + jnp.einsum('bngqk,bknd->bqngd', p[..., span:], v_ctx))

    out = lax.map(one_block, (jnp.arange(qblocks.shape[0]), qblocks))
    return _merge_q_blocks(out)


def _qkv(h, w_qkv):
    B, L, _ = h.shape
    q, k, v = jnp.split(h @ w_qkv, [N_HEADS * HEAD_DIM, (N_HEADS + N_KV_HEADS) * HEAD_DIM], axis=-1)
    return (q.reshape(B, L, N_HEADS, HEAD_DIM), k.reshape(B, L, N_KV_HEADS, HEAD_DIM),
            v.reshape(B, L, N_KV_HEADS, HEAD_DIM))


def full_attn_context(h, w_qkv, q_norm, k_norm, w_o):
    q, k, v = _qkv(h, w_qkv)
    q = rmsnorm(q, q_norm)
    k = rmsnorm(k, k_norm)
    return dense_attention(q, k, v, None) @ w_o, k, v


def full_attn_latent(h, w_qkv, q_norm, k_norm, w_o, k_ctx, v_ctx):
    q, k, v = _qkv(h, w_qkv)
    q = axial_rope(rmsnorm(q, q_norm))
    k = axial_rope(rmsnorm(k, k_norm))
    k_all = jnp.concatenate([k, k_ctx.astype(k.dtype)], axis=1)
    v_all = jnp.concatenate([v, v_ctx.astype(v.dtype)], axis=1)
    return dense_attention(q, k_all, v_all, None) @ w_o


def win_attn_context(h, w_qkv, sink, w_o):
    q, k, v = _qkv(h, w_qkv)
    return dense_attention(q, k, v, sink) @ w_o, k, v


def win_attn_latent(h, w_qkv, sink, w_o, k_ctx, v_ctx):
    q, k, v = _qkv(h, w_qkv)
    q = axial_rope(q)
    k = axial_rope(k)
    return windowed_attention(q, k, v, k_ctx.astype(k.dtype), v_ctx.astype(v.dtype), sink) @ w_o


def centred_mean(h, w):
    L = h.shape[1]
    cs = jnp.pad(jnp.cumsum(h.astype(F32), axis=1), ((0, 0), (1, 0), (0, 0)))
    t = jnp.arange(L)
    lo = jnp.maximum(t - w // 2, 0)
    hi = jnp.minimum(t + w - w // 2, L)
    return (cs[:, hi] - cs[:, lo]) / (hi - lo).astype(F32)[None, :, None]


def pool_mixer(h, w_groups, scale):
    B, L, D = h.shape
    hg = h.reshape(B, L, len(POOL_WINDOWS), POOL_GROUP)
    means = jnp.stack([centred_mean(hg[:, :, g], w) for g, w in enumerate(POOL_WINDOWS)], axis=2)
    d = (means - hg.astype(F32)).astype(h.dtype)
    out = jnp.einsum('blgc,gce->blge', d, w_groups).reshape(B, L, D)
    return out * scale


def centred_conv3(u, w, b):
    up = jnp.pad(u, ((0, 0), (1, 1), (0, 0)))
    return up[:, :-2] * w[0] + up[:, 1:-1] * w[1] + up[:, 2:] * w[2] + b


def hyena_filters(L, f_w1, f_b1, f_w2, f_b2, f_freq, f_w3, decay):
    t = jnp.arange(L, dtype=F32) / L
    bands = jnp.linspace(1e-4, HYENA_EMB_BANDS - 1, HYENA_EMB_BANDS, dtype=F32)
    ang = (2 * math.pi) * t[:, None] * bands[None, :]
    emb = jnp.concatenate([t[:, None], jnp.cos(ang), -jnp.sin(ang)], axis=-1)
    freq = f_freq.astype(F32)
    a = jnp.sin(freq * (emb @ f_w1.astype(F32) + f_b1.astype(F32)))
    a = jnp.sin(freq * (a @ f_w2.astype(F32) + f_b2.astype(F32)))
    filt = (a @ f_w3.astype(F32)) * jnp.exp(-t[:, None] * jnp.abs(decay.astype(F32))[None, :])
    return filt.reshape(L, HYENA_ORDER, 2, D_MODEL)


def bidir_fft_conv(z, h_fwd, h_bwd, skip):
    B, L, C = z.shape
    taps = jnp.concatenate([h_fwd, jnp.zeros((1, C), F32), h_bwd[:0:-1]], axis=0)
    zf = z.astype(F32)
    spec = jnp.fft.rfft(zf, n=2 * L, axis=1) * jnp.fft.rfft(taps, axis=0)[None]
    y = jnp.fft.irfft(spec, n=2 * L, axis=1)[:, :L]
    return (y + zf * skip.astype(F32)).astype(z.dtype)


def hyena_mixer(h, w_in, b_in, conv_w, conv_b, f_w1, f_b1, f_w2, f_b2, f_freq, f_w3, decay, skip, w_out, b_out):
    L = h.shape[1]
    u = centred_conv3(h @ w_in + b_in, conv_w, conv_b)
    v, x1, x2 = jnp.split(u, 3, axis=-1)
    filt = hyena_filters(L, f_w1, f_b1, f_w2, f_b2, f_freq, f_w3, decay)
    z = v
    for n, gate in enumerate((x1, x2)):
        z = gate * bidir_fft_conv(z, filt[:, n, 0], filt[:, n, 1], skip[n])
    return z @ w_out + b_out


def grouped_moe(h, w_router, b_router, w_gate, w_up, w_down):
    B, L, D = h.shape
    x = h.reshape(B * L, D)
    T = x.shape[0]
    scores = jax.nn.sigmoid((x @ w_router).astype(F32))
    grouped = (scores + b_router.astype(F32)).reshape(T, N_EXPERT_GROUPS, EXPERTS_PER_GROUP)
    group_score = jnp.sum(lax.top_k(grouped, TOP_K)[0], axis=-1)
    g_idx = jnp.argmax(group_score, axis=-1)
    in_group = jnp.take_along_axis(grouped, g_idx[:, None, None], axis=1)[:, 0]
    _, local = lax.top_k(in_group, TOP_K)
    e_idx = g_idx[:, None] * EXPERTS_PER_GROUP + local
    w = jnp.take_along_axis(scores, e_idx, axis=1)
    w = w / jnp.sum(w, axis=-1, keepdims=True)
    gates = jnp.sum(jax.nn.one_hot(e_idx, N_EXPERTS, dtype=F32) * w[..., None], axis=1).astype(x.dtype)
    out = jnp.zeros_like(x)
    for e in range(N_EXPERTS):
        he = jax.nn.silu(x @ w_gate[e]) * (x @ w_up[e])
        out = out + gates[:, e:e + 1] * (he @ w_down[e])
    return out.reshape(B, L, D)


def setup_inputs(seed: int = 0) -> dict:
    key = jax.random.key(seed)
    ks = iter(jax.random.split(key, 64))

    def nrm(shape, scale):
        return jax.random.normal(next(ks), shape, F32) * scale

    n_pool, n_hy, n_full, n_win = [len(range(m, DEPTH, N_MIXERS)) for m in range(N_MIXERS)]
    D = D_MODEL
    HD3 = 3 * D
    NF = HYENA_ORDER * 2 * D
    return {
        'x_prompt': nrm((BATCH, SEQ, D), 1.0),
        'x_sample': nrm((DEC_BATCH, DEC_SEQ, D), 1.0),
        'cache_k_full': nrm((DEC_BATCH, n_full, PAST_LEN, N_KV_HEADS, HEAD_DIM), 1.0),
        'cache_v_full': nrm((DEC_BATCH, n_full, PAST_LEN, N_KV_HEADS, HEAD_DIM), 1.0),
        'cache_k_win': nrm((DEC_BATCH, n_win, PAST_LEN, N_KV_HEADS, HEAD_DIM), 1.0),
        'cache_v_win': nrm((DEC_BATCH, n_win, PAST_LEN, N_KV_HEADS, HEAD_DIM), 1.0),
        'c': nrm((DEC_BATCH, D), 1.0),
        'c_ctx': nrm((D,), 1.0),
        'w_mod': nrm((DEPTH, D, 6 * D), 0.5 * D ** -0.5),
        'b_mod': nrm((DEPTH, 6 * D), 0.02),
        'norm_mix': 1.0 + nrm((DEPTH, D), 0.05),
        'norm_ffn': 1.0 + nrm((DEPTH, D), 0.05),
        'final_norm': 1.0 + nrm((D,), 0.05),
        'pool_w': nrm((n_pool, len(POOL_WINDOWS), POOL_GROUP, POOL_GROUP), POOL_GROUP ** -0.5),
        'pool_scale': 1.0 + nrm((n_pool, D), 0.1),
        'hy_w_in': nrm((n_hy, D, HD3), D ** -0.5),
        'hy_b_in': nrm((n_hy, HD3), 0.02),
        'hy_conv_w': nrm((n_hy, 3, HD3), 3 ** -0.5),
        'hy_conv_b': nrm((n_hy, HD3), 0.02),
        'hy_f_w1': nrm((n_hy, HYENA_EMB_DIM, HYENA_FILTER_HIDDEN), HYENA_EMB_DIM ** -0.5),
        'hy_f_b1': nrm((n_hy, HYENA_FILTER_HIDDEN), 0.1),
        'hy_f_w2': nrm((n_hy, HYENA_FILTER_HIDDEN, HYENA_FILTER_HIDDEN), HYENA_FILTER_HIDDEN ** -0.5),
        'hy_f_b2': nrm((n_hy, HYENA_FILTER_HIDDEN), 0.1),
        'hy_f_freq': 1.0 + nrm((n_hy, HYENA_FILTER_HIDDEN), 0.1),
        'hy_f_w3': nrm((n_hy, HYENA_FILTER_HIDDEN, NF), 0.05 * HYENA_FILTER_HIDDEN ** -0.5),
        'hy_decay': jax.random.uniform(next(ks), (n_hy, NF), F32, 3.0, 15.0),
        'hy_skip': nrm((n_hy, HYENA_ORDER, D), 0.5),
        'hy_w_out': nrm((n_hy, D, D), D ** -0.5),
        'hy_b_out': nrm((n_hy, D), 0.02),
        'fa_w_qkv': nrm((n_full, D, QKV_DIM), D ** -0.5),
        'fa_q_norm': 1.0 + nrm((n_full, HEAD_DIM), 0.1),
        'fa_k_norm': 1.0 + nrm((n_full, HEAD_DIM), 0.1),
        'fa_w_o': nrm((n_full, N_HEADS * HEAD_DIM, D), (N_HEADS * HEAD_DIM) ** -0.5),
        'wa_w_qkv': nrm((n_win, D, QKV_DIM), D ** -0.5),
        'wa_sink': nrm((n_win, N_HEADS), 0.5),
        'wa_w_o': nrm((n_win, N_HEADS * HEAD_DIM, D), (N_HEADS * HEAD_DIM) ** -0.5),
        'w_router': nrm((D, N_EXPERTS), D ** -0.5),
        'b_router': nrm((N_EXPERTS,), 0.01),
        'moe_w_gate': nrm((DEPTH, N_EXPERTS, D, D_EXPERT), D ** -0.5),
        'moe_w_up': nrm((DEPTH, N_EXPERTS, D, D_EXPERT), D ** -0.5),
        'moe_w_down': nrm((DEPTH, N_EXPERTS, D_EXPERT, D), D_EXPERT ** -0.5),
    }


def reference(x_prompt, x_sample, cache_k_full, cache_v_full, cache_k_win, cache_v_win, c, c_ctx,
              w_mod, b_mod, norm_mix, norm_ffn, final_norm, pool_w, pool_scale,
              hy_w_in, hy_b_in, hy_conv_w, hy_conv_b, hy_f_w1, hy_f_b1, hy_f_w2, hy_f_b2, hy_f_freq,
              hy_f_w3, hy_decay, hy_skip, hy_w_out, hy_b_out,
              fa_w_qkv, fa_q_norm, fa_k_norm, fa_w_o, wa_w_qkv, wa_sink, wa_w_o,
              w_router, b_router, moe_w_gate, moe_w_up, moe_w_down):
    xp = x_prompt
    xs = x_sample
    new_k_full, new_v_full, new_k_win, new_v_win = [], [], [], []
    for layer in range(DEPTH):
        kind = layer % N_MIXERS
        j = layer // N_MIXERS
        mp = adaln_params(c_ctx[None, :], w_mod[layer], b_mod[layer])
        ms = adaln_params(c, w_mod[layer], b_mod[layer])
        hp = modulate(rmsnorm(xp, norm_mix[layer]), mp[0], mp[1])
        hs = modulate(rmsnorm(xs, norm_mix[layer]), ms[0], ms[1])
        if kind == 0:
            op = pool_mixer(hp, pool_w[j], pool_scale[j])
            osm = pool_mixer(hs, pool_w[j], pool_scale[j])
        elif kind == 1:
            op = hyena_mixer(hp, hy_w_in[j], hy_b_in[j], hy_conv_w[j], hy_conv_b[j], hy_f_w1[j], hy_f_b1[j],
                             hy_f_w2[j], hy_f_b2[j], hy_f_freq[j], hy_f_w3[j], hy_decay[j], hy_skip[j],
                             hy_w_out[j], hy_b_out[j])
            osm = hyena_mixer(hs, hy_w_in[j], hy_b_in[j], hy_conv_w[j], hy_conv_b[j], hy_f_w1[j], hy_f_b1[j],
                              hy_f_w2[j], hy_f_b2[j], hy_f_freq[j], hy_f_w3[j], hy_decay[j], hy_skip[j],
                              hy_w_out[j], hy_b_out[j])
        elif kind == 2:
            op, kc, vc = full_attn_context(hp, fa_w_qkv[j], fa_q_norm[j], fa_k_norm[j], fa_w_o[j])
            new_k_full.append(kc)
            new_v_full.append(vc)
            osm = full_attn_latent(hs, fa_w_qkv[j], fa_q_norm[j], fa_k_norm[j], fa_w_o[j],
                                   cache_k_full[:, j], cache_v_full[:, j])
        else:
            op, kc, vc = win_attn_context(hp, wa_w_qkv[j], wa_sink[j], wa_w_o[j])
            new_k_win.append(kc)
            new_v_win.append(vc)
            osm = win_attn_latent(hs, wa_w_qkv[j], wa_sink[j], wa_w_o[j], cache_k_win[:, j], cache_v_win[:, j])
        xp = xp + mp[2] * op
        xs = xs + ms[2] * osm
        hp = modulate(rmsnorm(xp, norm_ffn[layer]), mp[3], mp[4])
        hs = modulate(rmsnorm(xs, norm_ffn[layer]), ms[3], ms[4])
        xp = xp + mp[5] * grouped_moe(hp, w_router, b_router, moe_w_gate[layer], moe_w_up[layer], moe_w_down[layer])
        xs = xs + ms[5] * grouped_moe(hs, w_router, b_router, moe_w_gate[layer], moe_w_up[layer], moe_w_down[layer])
    y_prompt = rmsnorm(xp, final_norm)
    y_sample = rmsnorm(xs, final_norm)
    return (y_prompt, y_sample, jnp.stack(new_k_full, axis=1), jnp.stack(new_v_full, axis=1),
            jnp.stack(new_k_win, axis=1), jnp.stack(new_v_win, axis=1))
```

```python
import functools
import math

import jax
import jax.numpy as jnp
import numpy as np
from jax import lax
from jax.experimental import pallas as pl
from jax.experimental.pallas import tpu as pltpu

D_MODEL = 2048
BATCH = 32
SEQ = 256
DEPTH = 4
DEC_BATCH = 4
DEC_SEQ = 4096
PAST_LEN = 512
GRID_W = 64
N_HEADS = 16
N_KV_HEADS = 4
HEAD_DIM = D_MODEL // N_HEADS
KV_GROUP = N_HEADS // N_KV_HEADS
KV_DIM = N_KV_HEADS * HEAD_DIM
QKV_DIM = (N_HEADS + 2 * N_KV_HEADS) * HEAD_DIM
ROPE_THETA = 10000.0
WINDOW = 128
POOL_WINDOWS = (2, 4, 8, 16)
POOL_GROUP = D_MODEL // len(POOL_WINDOWS)
HYENA_EMB_BANDS = 16
HYENA_FILTER_HIDDEN = 64
N_EXPERTS = 16
N_EXPERT_GROUPS = 4
EXPERTS_PER_GROUP = 4
D_EXPERT = 512
NORM_EPS = 1e-6
NEG_INF = -1e30

F32 = jnp.float32
BF16 = jnp.bfloat16

TP = BATCH * SEQ
TS = DEC_BATCH * DEC_SEQ
T = TP + TS
N_COND = 8
LANE = 128
MIB = 1024 * 1024

PAIR_LO = (0, 0, 0, 1, 1, 2)
PAIR_HI = (1, 2, 3, 2, 3, 3)
N_BUCKETS = N_EXPERT_GROUPS * len(PAIR_LO)
MOE_TILE = 256
MOE_TILES = T // MOE_TILE + N_BUCKETS
T_PAD = MOE_TILES * MOE_TILE
XH_W = D_MODEL + LANE


def _cp(sem, vmem_mb=48):
    return pltpu.CompilerParams(dimension_semantics=sem, vmem_limit_bytes=vmem_mb * MIB)


def _dot(a, b):
    return jnp.dot(a, b, preferred_element_type=F32)


def _dot3(a, b):
    ah = a.astype(BF16)
    al = (a - ah.astype(F32)).astype(BF16)
    bh = b.astype(BF16)
    bl = (b - bh.astype(F32)).astype(BF16)
    return _dot(ah, bh) + (_dot(al, bh) + _dot(ah, bl))


def _sigmoid(x):
    return 1.0 / (1.0 + jnp.exp(-x))


def _cond_row(r):
    return jnp.where(r < TP, 0, 1 + (r - TP) // DEC_SEQ)


def _mod_spec(tm, chunk, tn=D_MODEL, ncol=False):
    if ncol:
        return pl.BlockSpec((None, None, 1, tn), lambda i, j: (_cond_row(i * tm), chunk, 0, j))
    return pl.BlockSpec((None, None, 1, tn), lambda i, *_: (_cond_row(i * tm), chunk, 0, 0))


def _norm_mod(x, g, shift, scale):
    var = jnp.mean(x * x, axis=-1, keepdims=True)
    y = x * lax.rsqrt(var + NORM_EPS) * g
    return y * (1.0 + scale) + shift


def _adaln_kernel(c_ref, w_ref, b_ref, o_ref):
    c = c_ref[...]
    a = c * _sigmoid(c)
    o_ref[...] = _dot3(a, w_ref[...]) + b_ref[...]


def _adaln(cond, w_mod, b_mod):
    tn = 1024
    n = 6 * D_MODEL
    return pl.pallas_call(
        _adaln_kernel,
        grid=(DEPTH, n // tn),
        in_specs=[
            pl.BlockSpec((N_COND, D_MODEL), lambda l, j: (0, 0)),
            pl.BlockSpec((None, D_MODEL, tn), lambda l, j: (l, 0, j)),
            pl.BlockSpec((None, 1, tn), lambda l, j: (l, 0, j)),
        ],
        out_specs=pl.BlockSpec((None, N_COND, tn), lambda l, j: (l, 0, j)),
        out_shape=jax.ShapeDtypeStruct((DEPTH, N_COND, n), F32),
        compiler_params=_cp(("parallel", "parallel")),
        name="adaln",
    )(cond, w_mod, b_mod.reshape(DEPTH, 1, n))


def _nm_matmul_kernel(x_ref, g_ref, sh_ref, sc_ref, w_ref, b_ref, o_ref, h_scr):
    @pl.when(pl.program_id(1) == 0)
    def _():
        h_scr[...] = _norm_mod(x_ref[...], g_ref[...], sh_ref[...], sc_ref[...]).astype(BF16)

    o_ref[...] = (_dot(h_scr[...], w_ref[...]) + b_ref[...]).astype(o_ref.dtype)


def _nm_matmul(x, g, mod, w, b, out_dtype, name):
    tm, tn = 1024, 1024
    n = w.shape[1]
    return pl.pallas_call(
        _nm_matmul_kernel,
        grid=(T // tm, n // tn),
        in_specs=[
            pl.BlockSpec((tm, D_MODEL), lambda i, j: (i, 0)),
            pl.BlockSpec((1, D_MODEL), lambda i, j: (0, 0)),
            _mod_spec(tm, 0),
            _mod_spec(tm, 1),
            pl.BlockSpec((D_MODEL, tn), lambda i, j: (0, j)),
            pl.BlockSpec((1, tn), lambda i, j: (0, j)),
        ],
        out_specs=pl.BlockSpec((tm, tn), lambda i, j: (i, j)),
        out_shape=jax.ShapeDtypeStruct((T, n), out_dtype),
        scratch_shapes=[pltpu.VMEM((tm, D_MODEL), BF16)],
        compiler_params=_cp(("parallel", "arbitrary")),
        name=name,
    )(x, g, mod, mod, w, b)


def _resid_matmul_kernel(a_ref, w_ref, b_ref, x_ref, gt_ref, o_ref):
    o_ref[...] = x_ref[...] + gt_ref[...] * (_dot(a_ref[...], w_ref[...]) + b_ref[...])


def _resid_matmul(a, w, b, x, mod, name):
    tm, tn = 1024, 1024
    k = a.shape[1]
    return pl.pallas_call(
        _resid_matmul_kernel,
        grid=(T // tm, D_MODEL // tn),
        in_specs=[
            pl.BlockSpec((tm, k), lambda i, j: (i, 0)),
            pl.BlockSpec((k, tn), lambda i, j: (0, j)),
            pl.BlockSpec((1, tn), lambda i, j: (0, j)),
            pl.BlockSpec((tm, tn), lambda i, j: (i, j)),
            _mod_spec(tm, 2, tn, ncol=True),
        ],
        out_specs=pl.BlockSpec((tm, tn), lambda i, j: (i, j)),
        out_shape=jax.ShapeDtypeStruct((T, D_MODEL), F32),
        compiler_params=_cp(("parallel", "parallel")),
        name=name,
    )(a, w, b, x, mod)


POOL_TILE = 256
POOL_HALO = 8


def _seq_pos(r0):
    is_ctx = r0 < TP
    loc0 = jnp.where(is_ctx, r0 % SEQ, (r0 - TP) % DEC_SEQ)
    seq_len = jnp.where(is_ctx, SEQ, DEC_SEQ)
    return loc0, seq_len


def _pool_kernel(x_ref, xp_ref, xn_ref, g_ref, sh_ref, sc_ref, gt_ref, pw_ref, ps_ref, o_ref, hz_scr):
    tm, hl = POOL_TILE, POOL_HALO
    loc0, seq_len = _seq_pos(pl.program_id(0) * tm)
    has_prev = loc0 > 0
    has_next = loc0 + tm < seq_len
    g, sh, sc = g_ref[...], sh_ref[...], sc_ref[...]
    x = x_ref[...]
    h = _norm_mod(x, g, sh, sc)
    hz_scr[0:hl, :] = jnp.where(has_prev, _norm_mod(xp_ref[...], g, sh, sc), 0.0)
    hz_scr[hl:hl + tm, :] = h
    hz_scr[hl + tm:, :] = jnp.where(has_next, _norm_mod(xn_ref[...], g, sh, sc), 0.0)
    tl = loc0 + lax.broadcasted_iota(jnp.int32, (tm, 1), 0)
    outs = []
    for gi, w in enumerate(POOL_WINDOWS):
        cs = slice(gi * POOL_GROUP, (gi + 1) * POOL_GROUP)
        s = jnp.zeros((tm, POOL_GROUP), F32)
        for off in range(-(w // 2), w - w // 2):
            s = s + hz_scr[hl + off:hl + off + tm, cs]
        lo = jnp.maximum(tl - w // 2, 0)
        hi = jnp.minimum(tl + (w - w // 2), seq_len)
        d = s / (hi - lo).astype(F32) - h[:, cs]
        outs.append(_dot(d.astype(BF16), pw_ref[gi]))
    out = jnp.concatenate(outs, axis=1) * ps_ref[...]
    o_ref[...] = x + gt_ref[...] * out


def _pool_mixer(x, g, mod, pool_w, pool_scale):
    tm, hl = POOL_TILE, POOL_HALO
    r = tm // hl
    return pl.pallas_call(
        _pool_kernel,
        grid=(T // tm,),
        in_specs=[
            pl.BlockSpec((tm, D_MODEL), lambda i: (i, 0)),
            pl.BlockSpec((hl, D_MODEL), lambda i: (jnp.maximum(i * r - 1, 0), 0)),
            pl.BlockSpec((hl, D_MODEL), lambda i: (jnp.minimum((i + 1) * r, T // hl - 1), 0)),
            pl.BlockSpec((1, D_MODEL), lambda i: (0, 0)),
            _mod_spec(tm, 0),
            _mod_spec(tm, 1),
            _mod_spec(tm, 2),
            pl.BlockSpec((len(POOL_WINDOWS), POOL_GROUP, POOL_GROUP), lambda i: (0, 0, 0)),
            pl.BlockSpec((1, D_MODEL), lambda i: (0, 0)),
        ],
        out_specs=pl.BlockSpec((tm, D_MODEL), lambda i: (i, 0)),
        out_shape=jax.ShapeDtypeStruct((T, D_MODEL), F32),
        scratch_shapes=[pltpu.VMEM((tm + 2 * hl, D_MODEL), F32)],
        compiler_params=_cp(("parallel",)),
        name="pool_mixer",
    )(x, x, x, g, mod, mod, mod, pool_w.astype(BF16), pool_scale.reshape(1, D_MODEL))


CONV_TILE = 256
CONV_HALO = 16


def _conv3_kernel(u_ref, up_ref, un_ref, cw_ref, cb_ref, o_ref, scr):
    tm, hl = CONV_TILE, CONV_HALO
    loc0, seq_len = _seq_pos(pl.program_id(0) * tm)
    has_prev = loc0 > 0
    has_next = loc0 + tm < seq_len
    scr[0:hl, :] = jnp.where(has_prev, up_ref[...].astype(F32), 0.0)
    scr[hl:hl + tm, :] = u_ref[...].astype(F32)
    scr[hl + tm:, :] = jnp.where(has_next, un_ref[...].astype(F32), 0.0)
    out = (scr[hl - 1:hl - 1 + tm, :] * cw_ref[0:1, :] + scr[hl:hl + tm, :] * cw_ref[1:2, :]
           + scr[hl + 1:hl + 1 + tm, :] * cw_ref[2:3, :] + cb_ref[...])
    o_ref[...] = out.astype(o_ref.dtype)


def _conv3(u0, conv_w, conv_b):
    tm, hl, tc = CONV_TILE, CONV_HALO, D_MODEL
    r = tm // hl
    n = u0.shape[1]
    return pl.pallas_call(
        _conv3_kernel,
        grid=(T // tm, n // tc),
        in_specs=[
            pl.BlockSpec((tm, tc), lambda i, j: (i, j)),
            pl.BlockSpec((hl, tc), lambda i, j: (jnp.maximum(i * r - 1, 0), j)),
            pl.BlockSpec((hl, tc), lambda i, j: (jnp.minimum((i + 1) * r, T // hl - 1), j)),
            pl.BlockSpec((3, tc), lambda i, j: (0, j)),
            pl.BlockSpec((1, tc), lambda i, j: (0, j)),
        ],
        out_specs=pl.BlockSpec((tm, tc), lambda i, j: (i, j)),
        out_shape=jax.ShapeDtypeStruct((T, n), BF16),
        scratch_shapes=[pltpu.VMEM((tm + 2 * hl, tc), F32)],
        compiler_params=_cp(("parallel", "parallel")),
        name="hyena_conv3",
    )(u0, u0, u0, conv_w, conv_b.reshape(1, n))


FILT_TILE = 256


def _filter_kernel(emb_ref, w1_ref, b1_ref, w2_ref, b2_ref, fr_ref, w3_ref, dc_ref, fa_ref, fb_ref):
    emb = emb_ref[...]
    fr = fr_ref[...]
    a = jnp.sin(fr * (_dot3(emb, w1_ref[...]) + b1_ref[...]))
    a = jnp.sin(fr * (_dot3(a, w2_ref[...]) + b2_ref[...]))
    t = emb[:, 0:1]
    filt = _dot3(a, w3_ref[...]) * jnp.exp(-t * jnp.abs(dc_ref[...]))
    hf = filt[:, :D_MODEL]
    row = pl.program_id(0) * FILT_TILE + lax.broadcasted_iota(jnp.int32, (FILT_TILE, 1), 0)
    hb = jnp.where(row == 0, 0.0, filt[:, D_MODEL:])
    fa_ref[...] = (hf + hb).astype(BF16)
    fb_ref[...] = (hb - hf).astype(BF16)


def _pad2(a, rows, cols):
    return jnp.pad(a, ((0, rows - a.shape[0]), (0, cols - a.shape[1])))


def _hyena_filters(L, f_w1, f_b1, f_w2, f_b2, f_freq, f_w3, decay):
    t = jnp.arange(L, dtype=F32) / L
    bands = jnp.linspace(1e-4, HYENA_EMB_BANDS - 1, HYENA_EMB_BANDS, dtype=F32)
    ang = (2 * math.pi) * t[:, None] * bands[None, :]
    emb = _pad2(jnp.concatenate([t[:, None], jnp.cos(ang), -jnp.sin(ang)], axis=-1), L, LANE)
    nf = 2 * D_MODEL
    tl = FILT_TILE
    small = lambda: pl.BlockSpec((LANE, LANE), lambda i, n: (0, 0))
    vec = lambda: pl.BlockSpec((1, LANE), lambda i, n: (0, 0))
    out = pl.BlockSpec((None, tl, D_MODEL), lambda i, n: (n, i, 0))
    return pl.pallas_call(
        _filter_kernel,
        grid=(L // tl, 2),
        in_specs=[
            pl.BlockSpec((tl, LANE), lambda i, n: (i, 0)),
            small(), vec(), small(), vec(), vec(),
            pl.BlockSpec((LANE, nf), lambda i, n: (0, n)),
            pl.BlockSpec((1, nf), lambda i, n: (0, n)),
        ],
        out_specs=[out, out],
        out_shape=[jax.ShapeDtypeStruct((2, L, D_MODEL), BF16)] * 2,
        compiler_params=_cp(("parallel", "parallel")),
        name="hyena_filters",
    )(emb, _pad2(f_w1, LANE, LANE), _pad2(f_b1[None], 1, LANE), _pad2(f_w2, LANE, LANE),
      _pad2(f_b2[None], 1, LANE), _pad2(f_freq[None], 1, LANE), _pad2(f_w3, LANE, 2 * nf), decay[None])


def _dft_mats(L):
    r = int(math.isqrt(L))
    k2 = 2 * jnp.arange(L, dtype=jnp.int32)[:, None] + 1
    n1 = r * jnp.arange(L // r, dtype=jnp.int32)[None, :]
    n2 = jnp.arange(r, dtype=jnp.int32)[None, :]
    sc = math.pi / (2 * L)
    aa = ((k2 * n1) % (4 * L)).astype(F32) * sc
    ab = ((k2 * n2) % (4 * L)).astype(F32) * sc
    ca, sa, cb, sb = jnp.cos(aa)[:, :, None], jnp.sin(aa)[:, :, None], jnp.cos(ab)[:, None, :], jnp.sin(ab)[:, None, :]
    c = (ca * cb - sa * sb).reshape(L, L)
    s = (sa * cb + ca * sb).reshape(L, L)
    return c.astype(BF16), s.astype(BF16), c.T.astype(BF16), s.T.astype(BF16)


def _dft_tiles(L):
    return min(512, L), 512


def _dft_filter_kernel(c_ref, s_ref, a_ref, b_ref, gr_ref, gi_ref):
    gr_ref[...] = _dot(c_ref[...], a_ref[...])
    gi_ref[...] = _dot(s_ref[...], b_ref[...])


def _dft_filter(cm, sm, fa, fb, L):
    tf, tn = _dft_tiles(L)
    mat = lambda: pl.BlockSpec((tf, L), lambda k, c, n: (k, 0))
    rhs = lambda: pl.BlockSpec((None, L, tn), lambda k, c, n: (n, 0, c))
    out = pl.BlockSpec((None, tf, tn), lambda k, c, n: (n, k, c))
    return pl.pallas_call(
        _dft_filter_kernel,
        grid=(L // tf, D_MODEL // tn, 2),
        in_specs=[mat(), mat(), rhs(), rhs()],
        out_specs=[out, out],
        out_shape=[jax.ShapeDtypeStruct((2, L, D_MODEL), F32)] * 2,
        compiler_params=_cp(("parallel", "parallel", "parallel")),
        name="hyena_filter_dft",
    )(cm, sm, fa, fb)


def _dft_fwd_kernel(c_ref, s_ref, z_ref, gr_ref, gi_ref, yr_ref, yi_ref):
    z = z_ref[...]
    zc = _dot(c_ref[...], z)
    zs = _dot(s_ref[...], z)
    gr, gi = gr_ref[...], gi_ref[...]
    yr_ref[...] = (gr * zc + gi * zs).astype(BF16)
    yi_ref[...] = (gi * zc - gr * zs).astype(BF16)


def _dft_fwd(cm, sm, z, z_rowblk, z_colblk, gr, gi, order, nb, L):
    tf, tn = _dft_tiles(L)
    mat = lambda: pl.BlockSpec((tf, L), lambda k, c, b: (k, 0))
    gsp = lambda: pl.BlockSpec((None, tf, tn), lambda k, c, b: (order, k, c))
    out = pl.BlockSpec((None, tf, tn), lambda k, c, b: (b, k, c))
    return pl.pallas_call(
        _dft_fwd_kernel,
        grid=(L // tf, D_MODEL // tn, nb),
        in_specs=[mat(), mat(),
                  pl.BlockSpec((L, tn), lambda k, c, b: (z_rowblk + b, z_colblk + c)),
                  gsp(), gsp()],
        out_specs=[out, out],
        out_shape=[jax.ShapeDtypeStruct((nb, L, D_MODEL), BF16)] * 2,
        compiler_params=_cp(("parallel", "parallel", "parallel")),
        name="hyena_dft_fwd",
    )(cm, sm, z, gr, gi)


def _dft_inv_kernel(ct_ref, st_ref, yr_ref, yi_ref, z_ref, gt_ref, sk_ref, o_ref, *, inv_len):
    y = (_dot(ct_ref[...], yr_ref[...]) - _dot(st_ref[...], yi_ref[...])) * inv_len
    o_ref[...] = (gt_ref[...].astype(F32) * (y + sk_ref[...] * z_ref[...].astype(F32))).astype(BF16)


def _dft_inv(ctm, stm, yr, yi, z, z_rowblk, z_colblk, gate, g_rowblk, g_colblk, skip, nb, L):
    tt, tn = _dft_tiles(L)
    rpb = L // tt
    mat = lambda: pl.BlockSpec((tt, L), lambda t, c, b: (t, 0))
    spec = lambda: pl.BlockSpec((None, L, tn), lambda t, c, b: (b, 0, c))
    return pl.pallas_call(
        functools.partial(_dft_inv_kernel, inv_len=1.0 / L),
        grid=(rpb, D_MODEL // tn, nb),
        in_specs=[mat(), mat(), spec(), spec(),
                  pl.BlockSpec((tt, tn), lambda t, c, b: (z_rowblk + b * rpb + t, z_colblk + c)),
                  pl.BlockSpec((tt, tn), lambda t, c, b: (g_rowblk + b * rpb + t, g_colblk + c)),
                  pl.BlockSpec((1, tn), lambda t, c, b: (0, c))],
        out_specs=pl.BlockSpec((tt, tn), lambda t, c, b: (b * rpb + t, c)),
        out_shape=jax.ShapeDtypeStruct((nb * L, D_MODEL), BF16),
        compiler_params=_cp(("parallel", "parallel", "parallel")),
        name="hyena_dft_inv",
    )(ctm, stm, yr, yi, z, gate, skip)


def _hyena_stream(u, row0, nb, L, fparams, skip):
    cm, sm, ctm, stm = _dft_mats(L)
    fa, fb = _hyena_filters(L, *fparams)
    gr, gi = _dft_filter(cm, sm, fa, fb, L)
    tt, tn = _dft_tiles(L)
    ncb = D_MODEL // tn
    yr, yi = _dft_fwd(cm, sm, u, row0 // L, 0, gr, gi, 0, nb, L)
    z1 = _dft_inv(ctm, stm, yr, yi, u, row0 // tt, 0, u, row0 // tt, ncb, skip[0:1], nb, L)
    yr, yi = _dft_fwd(cm, sm, z1, 0, 0, gr, gi, 1, nb, L)
    return _dft_inv(ctm, stm, yr, yi, z1, 0, 0, u, row0 // tt, 2 * ncb, skip[1:2], nb, L)


def _hyena_mixer(x, g, mod, w_in, b_in, conv_w, conv_b, f_w1, f_b1, f_w2, f_b2, f_freq, f_w3, decay, skip,
                 w_out, b_out):
    u0 = _nm_matmul(x, g, mod, w_in.astype(BF16), b_in.reshape(1, -1), BF16, "hyena_in_proj")
    u = _conv3(u0, conv_w, conv_b)
    fparams = (f_w1, f_b1, f_w2, f_b2, f_freq, f_w3, decay)
    zp = _hyena_stream(u, 0, BATCH, SEQ, fparams, skip)
    zs = _hyena_stream(u, TP, DEC_BATCH, DEC_SEQ, fparams, skip)
    z = jnp.concatenate([zp, zs], axis=0)
    return _resid_matmul(z, w_out.astype(BF16), b_out.reshape(1, -1), x, mod, "hyena_out_proj")


PREP_TILE = 256


def _rope_tables():
    pos = jnp.arange(DEC_SEQ, dtype=jnp.int32)
    row = (pos // GRID_W).astype(F32)
    col = (pos % GRID_W).astype(F32)
    axis_dim = HEAD_DIM // 2
    inv_freq = ROPE_THETA ** (-jnp.arange(0, axis_dim, 2, dtype=F32) / axis_dim)
    ar = row[:, None] * inv_freq[None, :]
    ac = col[:, None] * inv_freq[None, :]
    cos = jnp.concatenate([jnp.cos(ar), jnp.cos(ar), jnp.cos(ac), jnp.cos(ac)], axis=-1)
    sin = jnp.concatenate([-jnp.sin(ar), jnp.sin(ar), -jnp.sin(ac), jnp.sin(ac)], axis=-1)
    return cos, sin


def _prep_kernel(*refs, use_norm, use_rope, emit_kv):
    it = iter(refs)
    qkv_ref, qn_ref, kn_ref = next(it), next(it), next(it)
    cos_ref = sin_ref = None
    if use_rope:
        cos_ref, sin_ref = next(it), next(it)
    q_ref, k_ref, v_ref = next(it), next(it), next(it)
    nk_ref = nv_ref = None
    if emit_kv:
        nk_ref, nv_ref = next(it), next(it)
    quarter = HEAD_DIM // 4
    if use_rope:
        cos, sin = cos_ref[...], sin_ref[...]
        lane = lax.broadcasted_iota(jnp.int32, (PREP_TILE, HEAD_DIM), 1)
        first = (lane % (2 * quarter)) < quarter

    def head(xh, gn):
        if use_norm:
            xh = xh * lax.rsqrt(jnp.mean(xh * xh, axis=-1, keepdims=True) + NORM_EPS) * gn
        return xh

    def rope(xh):
        if not use_rope:
            return xh
        partner = jnp.where(first, pltpu.roll(xh, HEAD_DIM - quarter, 1), pltpu.roll(xh, quarter, 1))
        return xh * cos + partner * sin

    qn, kn = qn_ref[...], kn_ref[...]
    scale = HEAD_DIM ** -0.5
    for h in range(N_HEADS):
        hs = slice(h * HEAD_DIM, (h + 1) * HEAD_DIM)
        q_ref[:, hs] = (rope(head(qkv_ref[:, hs], qn)) * scale).astype(BF16)
    for h in range(N_KV_HEADS):
        hs = slice(h * HEAD_DIM, (h + 1) * HEAD_DIM)
        ks = slice(D_MODEL + h * HEAD_DIM, D_MODEL + (h + 1) * HEAD_DIM)
        vs = slice(D_MODEL + KV_DIM + h * HEAD_DIM, D_MODEL + KV_DIM + (h + 1) * HEAD_DIM)
        kh = head(qkv_ref[:, ks], kn)
        vh = qkv_ref[:, vs]
        if emit_kv:
            nk_ref[:, hs] = kh
            nv_ref[:, hs] = vh
        k_ref[:, hs] = rope(kh).astype(BF16)
        v_ref[:, hs] = vh.astype(BF16)


def _attn_prep(qkv, row0, nrows, q_norm, k_norm, use_norm, rope, emit_kv):
    tm = PREP_TILE
    blk0 = row0 // tm
    in_specs = [pl.BlockSpec((tm, QKV_DIM), lambda i: (blk0 + i, 0)),
                pl.BlockSpec((1, HEAD_DIM), lambda i: (0, 0)),
                pl.BlockSpec((1, HEAD_DIM), lambda i: (0, 0))]
    args = [qkv, q_norm.reshape(1, HEAD_DIM), k_norm.reshape(1, HEAD_DIM)]
    if rope is not None:
        tab = lambda: pl.BlockSpec((tm, HEAD_DIM), lambda i: (i % (DEC_SEQ // tm), 0))
        in_specs += [tab(), tab()]
        args += list(rope)
    row = lambda w: pl.BlockSpec((tm, w), lambda i: (i, 0))
    out_specs = [row(D_MODEL), row(KV_DIM), row(KV_DIM)]
    out_shape = [jax.ShapeDtypeStruct((nrows, D_MODEL), BF16), jax.ShapeDtypeStruct((nrows, KV_DIM), BF16),
                 jax.ShapeDtypeStruct((nrows, KV_DIM), BF16)]
    if emit_kv:
        out_specs += [row(KV_DIM), row(KV_DIM)]
        out_shape += [jax.ShapeDtypeStruct((nrows, KV_DIM), F32)] * 2
    return pl.pallas_call(
        functools.partial(_prep_kernel, use_norm=use_norm, use_rope=rope is not None, emit_kv=emit_kv),
        grid=(nrows // tm,),
        in_specs=in_specs, out_specs=out_specs, out_shape=out_shape,
        compiler_params=_cp(("parallel",)),
        name="attn_prep",
    )(*args)


def _flash_kernel(*refs, n_lat, n_ctx, tq, tk, windowed, has_sink, n_kblk):
    it = iter(refs)
    q_ref, kl_ref, vl_ref = next(it), next(it), next(it)
    kc_ref = vc_ref = sink_ref = None
    if n_ctx:
        kc_ref, vc_ref = next(it), next(it)
    if has_sink:
        sink_ref = next(it)
    o_ref, m_scr, l_scr, acc_scr = next(it), next(it), next(it), next(it)
    i = pl.program_id(1)
    s_id = pl.program_id(2)

    @pl.when(s_id == 0)
    def _():
        if has_sink:
            for h in range(N_HEADS):
                hs = slice(h * HEAD_DIM, (h + 1) * HEAD_DIM)
                m_scr[:, hs] = jnp.broadcast_to(sink_ref[h:h + 1, :], (tq, HEAD_DIM))
            l_scr[...] = jnp.ones_like(l_scr)
        else:
            m_scr[...] = jnp.full_like(m_scr, NEG_INF)
            l_scr[...] = jnp.zeros_like(l_scr)
        acc_scr[...] = jnp.zeros_like(acc_scr)

    def step(k_ref, v_ref, mask):
        for h in range(N_HEADS):
            hs = slice(h * HEAD_DIM, (h + 1) * HEAD_DIM)
            gs = slice((h // KV_GROUP) * HEAD_DIM, (h // KV_GROUP + 1) * HEAD_DIM)
            s = lax.dot_general(q_ref[:, hs], k_ref[:, gs], (((1,), (1,)), ((), ())),
                                preferred_element_type=F32)
            if mask is not None:
                s = jnp.where(mask, s, NEG_INF)
            m_prev = m_scr[:, hs]
            m_new = jnp.maximum(m_prev, jnp.max(s, axis=-1, keepdims=True))
            alpha = jnp.exp(m_prev - m_new)
            p = jnp.exp(s - m_new[:, 0:1])
            l_scr[:, hs] = alpha * l_scr[:, hs] + jnp.sum(p, axis=-1, keepdims=True)
            acc_scr[:, hs] = alpha * acc_scr[:, hs] + _dot(p.astype(BF16), v_ref[:, gs])
            m_scr[:, hs] = m_new

    if windowed:
        kt = i - 1 + s_id
        qpos = i * tq + lax.broadcasted_iota(jnp.int32, (tq, 1), 0)
        kpos = kt * tk + lax.broadcasted_iota(jnp.int32, (1, tk), 1)
        mask = jnp.abs(kpos - qpos) <= WINDOW

        @pl.when((s_id < n_lat) & (kt >= 0) & (kt < n_kblk))
        def _():
            step(kl_ref, vl_ref, mask)
    else:
        @pl.when(s_id < n_lat)
        def _():
            step(kl_ref, vl_ref, None)

    if n_ctx:
        @pl.when(s_id >= n_lat)
        def _():
            step(kc_ref, vc_ref, None)

    @pl.when(s_id == n_lat + n_ctx - 1)
    def _():
        o_ref[...] = (acc_scr[...] / l_scr[...]).astype(BF16)


def _flash(q, k, v, k_ctx, v_ctx, sink, nb, L, tq, tk, windowed):
    n_kblk = L // tk
    n_lat = 3 if windowed else n_kblk
    n_ctx = 0 if k_ctx is None else PAST_LEN // tk
    nq = L // tq
    if windowed:
        lat_map = lambda b, i, s: (b * n_kblk + jnp.clip(i - 1 + jnp.minimum(s, n_lat - 1), 0, n_kblk - 1), 0)
    else:
        lat_map = lambda b, i, s: (b * n_kblk + jnp.minimum(s, n_lat - 1), 0)
    in_specs = [pl.BlockSpec((tq, D_MODEL), lambda b, i, s: (b * nq + i, 0)),
                pl.BlockSpec((tk, KV_DIM), lat_map), pl.BlockSpec((tk, KV_DIM), lat_map)]
    args = [q, k, v]
    if n_ctx:
        ctx_map = lambda b, i, s: (b, jnp.clip(s - n_lat, 0, n_ctx - 1), 0)
        in_specs += [pl.BlockSpec((None, tk, KV_DIM), ctx_map), pl.BlockSpec((None, tk, KV_DIM), ctx_map)]
        args += [k_ctx, v_ctx]
    if sink is not None:
        in_specs.append(pl.BlockSpec((N_HEADS, HEAD_DIM), lambda b, i, s: (0, 0)))
        args.append(jnp.broadcast_to(sink.astype(F32)[:, None], (N_HEADS, HEAD_DIM)))
    return pl.pallas_call(
        functools.partial(_flash_kernel, n_lat=n_lat, n_ctx=n_ctx, tq=tq, tk=tk, windowed=windowed,
                          has_sink=sink is not None, n_kblk=n_kblk),
        grid=(nb, nq, n_lat + n_ctx),
        in_specs=in_specs,
        out_specs=pl.BlockSpec((tq, D_MODEL), lambda b, i, s: (b * nq + i, 0)),
        out_shape=jax.ShapeDtypeStruct((nb * L, D_MODEL), BF16),
        scratch_shapes=[pltpu.VMEM((tq, D_MODEL), F32)] * 3,
        compiler_params=_cp(("parallel", "parallel", "arbitrary")),
        name="flash_attention",
    )(*args)


def _attn_mixer(x, g, mod, w_qkv, q_norm, k_norm, use_norm, sink, w_o, cache_k, cache_v, windowed, rope):
    zero_b = jnp.zeros((1, QKV_DIM), F32)
    qkv = _nm_matmul(x, g, mod, w_qkv.astype(BF16), zero_b, F32, "qkv_proj")
    qp, kp, vp, new_k, new_v = _attn_prep(qkv, 0, TP, q_norm, k_norm, use_norm, None, True)
    qs, ks, vs = _attn_prep(qkv, TP, TS, q_norm, k_norm, use_norm, rope, False)
    op = _flash(qp, kp, vp, None, None, sink, BATCH, SEQ, SEQ, SEQ, False)
    kc = cache_k.reshape(DEC_BATCH, PAST_LEN, KV_DIM).astype(BF16)
    vc = cache_v.reshape(DEC_BATCH, PAST_LEN, KV_DIM).astype(BF16)
    if windowed:
        osm = _flash(qs, ks, vs, kc, vc, sink, DEC_BATCH, DEC_SEQ, 256, 256, True)
    else:
        osm = _flash(qs, ks, vs, kc, vc, sink, DEC_BATCH, DEC_SEQ, 256, 512, False)
    o = jnp.concatenate([op, osm], axis=0)
    x = _resid_matmul(o, w_o.astype(BF16), jnp.zeros((1, D_MODEL), F32), x, mod, "attn_out_proj")
    shape = (BATCH, SEQ, N_KV_HEADS, HEAD_DIM)
    return x, new_k.reshape(shape), new_v.reshape(shape)


ROUTE_TILE = 512
ROUTE_ROWS = 32


def _router_kernel(x_ref, g_ref, sh_ref, sc_ref, wr_ref, br_ref, xh_ref, rt_ref, cnt_ref, carry):
    tm = ROUTE_TILE
    i = pl.program_id(0)

    @pl.when(i == 0)
    def _():
        carry[...] = jnp.zeros_like(carry)

    h = _norm_mod(x_ref[...], g_ref[...], sh_ref[...], sc_ref[...])
    xh_ref[:, :D_MODEL] = h
    logits = _dot(h.astype(BF16), wr_ref[...])
    s = _sigmoid(logits.T[:N_EXPERTS, :])
    sb = s + br_ref[...]
    u = [s[e:e + 1, :] for e in range(N_EXPERTS)]
    v = [sb[e:e + 1, :] for e in range(N_EXPERTS)]

    gscore = []
    for gq in range(N_EXPERT_GROUPS):
        m = v[4 * gq:4 * gq + 4]
        best = m[PAIR_LO[0]] + m[PAIR_HI[0]]
        for a, b in zip(PAIR_LO[1:], PAIR_HI[1:]):
            best = jnp.maximum(best, m[a] + m[b])
        gscore.append(best)
    gidx = jnp.zeros((1, tm), jnp.int32)
    gbest = gscore[0]
    for gq in range(1, N_EXPERT_GROUPS):
        upd = gscore[gq] > gbest
        gidx = jnp.where(upd, gq, gidx)
        gbest = jnp.where(upd, gscore[gq], gbest)

    def pick(rows, j):
        out = rows[j]
        for gq in range(1, N_EXPERT_GROUPS):
            out = jnp.where(gidx == gq, rows[4 * gq + j], out)
        return out

    vin = [pick(v, j) for j in range(EXPERTS_PER_GROUP)]
    uin = [pick(u, j) for j in range(EXPERTS_PER_GROUP)]
    i1 = jnp.zeros((1, tm), jnp.int32)
    m1 = vin[0]
    for j in range(1, EXPERTS_PER_GROUP):
        upd = vin[j] > m1
        i1 = jnp.where(upd, j, i1)
        m1 = jnp.where(upd, vin[j], m1)
    i2 = jnp.full((1, tm), -1, jnp.int32)
    m2 = jnp.full((1, tm), -jnp.inf, F32)
    for j in range(EXPERTS_PER_GROUP):
        upd = (i1 != j) & (vin[j] > m2)
        i2 = jnp.where(upd, j, i2)
        m2 = jnp.where(upd, vin[j], m2)

    def sel(rows, idx):
        out = rows[0]
        for j in range(1, EXPERTS_PER_GROUP):
            out = jnp.where(idx == j, rows[j], out)
        return out

    w1, w2 = sel(uin, i1), sel(uin, i2)
    wsum = w1 + w2
    w1, w2 = w1 / wsum, w2 / wsum
    first_lo = i1 < i2
    lo = jnp.where(first_lo, i1, i2)
    hi = jnp.where(first_lo, i2, i1)
    w_lo = jnp.where(first_lo, w1, w2)
    w_hi = jnp.where(first_lo, w2, w1)
    pair = jnp.where(lo == 0, hi - 1, jnp.where(lo == 1, hi + 1, 5))
    bucket = gidx * len(PAIR_LO) + pair

    onehot = (lax.broadcasted_iota(jnp.int32, (ROUTE_ROWS, tm), 0) == bucket)
    tri = (lax.broadcasted_iota(jnp.int32, (tm, tm), 0) <= lax.broadcasted_iota(jnp.int32, (tm, tm), 1))
    cum = _dot(jnp.where(onehot, 1.0, 0.0).astype(BF16), jnp.where(tri, 1.0, 0.0).astype(BF16))
    rank = jnp.sum(jnp.where(onehot, cum - 1.0 + carry[...], 0.0), axis=0, keepdims=True)
    carry[...] = carry[...] + cum[:, tm - 1:tm]
    cnt_ref[...] = jnp.broadcast_to(carry[...], (ROUTE_ROWS, LANE))

    rt_ref[...] = jnp.zeros_like(rt_ref)
    rt_ref[0:1, :] = bucket.astype(F32)
    rt_ref[1:2, :] = rank
    wt = jnp.concatenate([w_lo, w_hi, jnp.zeros((LANE - 2, tm), F32)], axis=0)
    xh_ref[:, D_MODEL:] = wt.T


def _router(x, g, mod, w_router, b_router):
    tm = ROUTE_TILE
    wr = _pad2(w_router, D_MODEL, LANE).astype(BF16)
    return pl.pallas_call(
        _router_kernel,
        grid=(T // tm,),
        in_specs=[
            pl.BlockSpec((tm, D_MODEL), lambda i: (i, 0)),
            pl.BlockSpec((1, D_MODEL), lambda i: (0, 0)),
            _mod_spec(tm, 3),
            _mod_spec(tm, 4),
            pl.BlockSpec((D_MODEL, LANE), lambda i: (0, 0)),
            pl.BlockSpec((N_EXPERTS, 1), lambda i: (0, 0)),
        ],
        out_specs=[
            pl.BlockSpec((tm, XH_W), lambda i: (i, 0)),
            pl.BlockSpec((8, tm), lambda i: (0, i)),
            pl.BlockSpec((ROUTE_ROWS, LANE), lambda i: (0, 0)),
        ],
        out_shape=[
            jax.ShapeDtypeStruct((T, XH_W), F32),
            jax.ShapeDtypeStruct((8, T), F32),
            jax.ShapeDtypeStruct((ROUTE_ROWS, LANE), F32),
        ],
        scratch_shapes=[pltpu.VMEM((ROUTE_ROWS, 1), F32)],
        compiler_params=_cp(("arbitrary",)),
        name="moe_router",
    )(x, g, mod, mod, wr, b_router.reshape(N_EXPERTS, 1))


DISPATCH_TILE = 256


def _dispatch_kernel(dest_ref, xh_ref, zeros_hbm, xs_hbm, sem):
    del zeros_hbm
    tm = DISPATCH_TILE
    base = pl.program_id(0) * tm

    def row_copy(r):
        return pltpu.make_async_copy(xh_ref.at[pl.ds(r, 1)], xs_hbm.at[pl.ds(dest_ref[base + r], 1)], sem)

    def start(r, c):
        row_copy(r).start()
        return c

    def wait(r, c):
        row_copy(r).wait()
        return c

    lax.fori_loop(0, tm, start, 0)
    lax.fori_loop(0, tm, wait, 0)


def _dispatch(dest, xh):
    tm = DISPATCH_TILE
    return pl.pallas_call(
        _dispatch_kernel,
        grid_spec=pltpu.PrefetchScalarGridSpec(
            num_scalar_prefetch=1,
            grid=(T // tm,),
            in_specs=[pl.BlockSpec((tm, XH_W), lambda i, d: (i, 0)),
                      pl.BlockSpec(memory_space=pl.ANY)],
            out_specs=pl.BlockSpec(memory_space=pl.ANY),
            scratch_shapes=[pltpu.SemaphoreType.DMA(())],
        ),
        out_shape=jax.ShapeDtypeStruct((T_PAD, XH_W), F32),
        input_output_aliases={2: 0},
        compiler_params=_cp(("arbitrary",)),
        name="moe_dispatch",
    )(dest, xh, jnp.zeros((T_PAD, XH_W), F32))


def _expert_kernel(ea_ref, eb_ref, nv_ref, xs_ref, ga_ref, ua_ref, da_ref, gb_ref, ub_ref, db_ref, y_ref):
    tm = MOE_TILE
    nv = nv_ref[pl.program_id(0)]

    @pl.when(nv > 0)
    def _():
        valid = lax.broadcasted_iota(jnp.int32, (tm, 1), 0) < nv
        x = jnp.where(valid, xs_ref[:, :D_MODEL], 0.0).astype(BF16)
        wts = jnp.where(valid, xs_ref[:, D_MODEL:], 0.0)

        def ffn(g_ref, u_ref, d_ref, w):
            a = _dot(x, g_ref[...])
            h = a * _sigmoid(a) * _dot(x, u_ref[...]) * w
            return _dot(h.astype(BF16), d_ref[...])

        y_ref[...] = ffn(ga_ref, ua_ref, da_ref, wts[:, 0:1]) + ffn(gb_ref, ub_ref, db_ref, wts[:, 1:2])

    @pl.when(nv == 0)
    def _():
        y_ref[...] = jnp.zeros_like(y_ref)


def _experts(tile_ea, tile_eb, tile_nv, xs, w_gate, w_up, w_down):
    tm = MOE_TILE
    up = lambda sel: pl.BlockSpec((None, D_MODEL, D_EXPERT), lambda j, ea, eb, nv: ((ea, eb)[sel][j], 0, 0))
    down = lambda sel: pl.BlockSpec((None, D_EXPERT, D_MODEL), lambda j, ea, eb, nv: ((ea, eb)[sel][j], 0, 0))
    return pl.pallas_call(
        _expert_kernel,
        grid_spec=pltpu.PrefetchScalarGridSpec(
            num_scalar_prefetch=3,
            grid=(MOE_TILES,),
            in_specs=[pl.BlockSpec((tm, XH_W), lambda j, ea, eb, nv: (j, 0)),
                      up(0), up(0), down(0), up(1), up(1), down(1)],
            out_specs=pl.BlockSpec((tm, D_MODEL), lambda j, ea, eb, nv: (j, 0)),
        ),
        out_shape=jax.ShapeDtypeStruct((T_PAD, D_MODEL), F32),
        compiler_params=_cp(("arbitrary",), 56),
        name="moe_experts",
    )(tile_ea, tile_eb, tile_nv, xs, w_gate, w_up, w_down, w_gate, w_up, w_down)


def _combine_kernel(dest_ref, x_ref, gt_ref, ys_hbm, o_ref, buf, sem):
    tm = DISPATCH_TILE
    base = pl.program_id(0) * tm

    def row_copy(r):
        return pltpu.make_async_copy(ys_hbm.at[pl.ds(dest_ref[base + r], 1)], buf.at[pl.ds(r, 1)], sem)

    def start(r, c):
        row_copy(r).start()
        return c

    def wait(r, c):
        row_copy(r).wait()
        return c

    lax.fori_loop(0, tm, start, 0)
    lax.fori_loop(0, tm, wait, 0)
    o_ref[...] = x_ref[...] + gt_ref[...] * buf[...]


def _combine(dest, x, mod, ys):
    tm = DISPATCH_TILE
    return pl.pallas_call(
        _combine_kernel,
        grid_spec=pltpu.PrefetchScalarGridSpec(
            num_scalar_prefetch=1,
            grid=(T // tm,),
            in_specs=[pl.BlockSpec((tm, D_MODEL), lambda i, d: (i, 0)),
                      pl.BlockSpec((None, None, 1, D_MODEL), lambda i, d: (_cond_row(i * tm), 5, 0, 0)),
                      pl.BlockSpec(memory_space=pl.ANY)],
            out_specs=pl.BlockSpec((tm, D_MODEL), lambda i, d: (i, 0)),
            scratch_shapes=[pltpu.VMEM((tm, D_MODEL), F32), pltpu.SemaphoreType.DMA(())],
        ),
        out_shape=jax.ShapeDtypeStruct((T, D_MODEL), F32),
        compiler_params=_cp(("arbitrary",)),
        name="moe_combine",
    )(dest, x, mod, ys)


def _moe_plan(rt, cnt):
    bucket = rt[0].astype(jnp.int32)
    rank = rt[1].astype(jnp.int32)
    counts = cnt[:N_BUCKETS, 0].astype(jnp.int32)
    tiles = (counts + MOE_TILE - 1) // MOE_TILE
    tile_end = jnp.cumsum(tiles)
    tile_start = tile_end - tiles
    dest = jnp.take(tile_start * MOE_TILE, bucket) + rank
    j = jnp.arange(MOE_TILES, dtype=jnp.int32)
    used = j < tile_end[-1]
    b = jnp.minimum(jnp.searchsorted(tile_end, jnp.minimum(j, tile_end[-1] - 1), side="right"), N_BUCKETS - 1)
    nv = jnp.clip(jnp.take(counts, b) - (j - jnp.take(tile_start, b)) * MOE_TILE, 0, MOE_TILE)
    nv = jnp.where(used, nv, 0).astype(jnp.int32)
    n_pairs = len(PAIR_LO)
    ea = (b // n_pairs) * EXPERTS_PER_GROUP + jnp.take(jnp.asarray(PAIR_LO, jnp.int32), b % n_pairs)
    eb = (b // n_pairs) * EXPERTS_PER_GROUP + jnp.take(jnp.asarray(PAIR_HI, jnp.int32), b % n_pairs)
    return dest.astype(jnp.int32), ea.astype(jnp.int32), eb.astype(jnp.int32), nv


def _moe(x, g, mod, w_router, b_router, w_gate, w_up, w_down):
    xh, rt, cnt = _router(x, g, mod, w_router, b_router)
    dest, ea, eb, nv = _moe_plan(rt, cnt)
    xs = _dispatch(dest, xh)
    ys = _experts(ea, eb, nv, xs, w_gate.astype(BF16), w_up.astype(BF16), w_down.astype(BF16))
    return _combine(dest, x, mod, ys)


def _final_norm_kernel(x_ref, g_ref, o_ref):
    x = x_ref[...]
    o_ref[...] = x * lax.rsqrt(jnp.mean(x * x, axis=-1, keepdims=True) + NORM_EPS) * g_ref[...]


def _final_norm(x, g):
    tm = 512
    return pl.pallas_call(
        _final_norm_kernel,
        grid=(T // tm,),
        in_specs=[pl.BlockSpec((tm, D_MODEL), lambda i: (i, 0)), pl.BlockSpec((1, D_MODEL), lambda i: (0, 0))],
        out_specs=pl.BlockSpec((tm, D_MODEL), lambda i: (i, 0)),
        out_shape=jax.ShapeDtypeStruct((T, D_MODEL), F32),
        compiler_params=_cp(("parallel",)),
        name="final_norm",
    )(x, g.reshape(1, D_MODEL))


def kernel(x_prompt, x_sample, cache_k_full, cache_v_full, cache_k_win, cache_v_win, c, c_ctx, w_mod, b_mod, norm_mix, norm_ffn, final_norm, pool_w, pool_scale, hy_w_in, hy_b_in, hy_conv_w, hy_conv_b, hy_f_w1, hy_f_b1, hy_f_w2, hy_f_b2, hy_f_freq, hy_f_w3, hy_decay, hy_skip, hy_w_out, hy_b_out, fa_w_qkv, fa_q_norm, fa_k_norm, fa_w_o, wa_w_qkv, wa_sink, wa_w_o, w_router, b_router, moe_w_gate, moe_w_up, moe_w_down):
    x = jnp.concatenate([x_prompt.reshape(TP, D_MODEL), x_sample.reshape(TS, D_MODEL)], axis=0)
    cond = jnp.concatenate([c_ctx[None, :], c, jnp.zeros((N_COND - 1 - DEC_BATCH, D_MODEL), F32)], axis=0)
    mods = _adaln(cond, w_mod, b_mod).reshape(DEPTH, N_COND, 6, 1, D_MODEL)
    rope = _rope_tables()
    ones_hd = jnp.ones((HEAD_DIM,), F32)
    new_kv = {}
    for layer in range(DEPTH):
        kind = layer % 4
        j = layer // 4
        mod = mods[layer]
        g_mix = norm_mix[layer].reshape(1, D_MODEL)
        if kind == 0:
            x = _pool_mixer(x, g_mix, mod, pool_w[j], pool_scale[j])
        elif kind == 1:
            x = _hyena_mixer(x, g_mix, mod, hy_w_in[j], hy_b_in[j], hy_conv_w[j], hy_conv_b[j], hy_f_w1[j],
                             hy_f_b1[j], hy_f_w2[j], hy_f_b2[j], hy_f_freq[j], hy_f_w3[j], hy_decay[j],
                             hy_skip[j], hy_w_out[j], hy_b_out[j])
        elif kind == 2:
            x, nk, nv = _attn_mixer(x, g_mix, mod, fa_w_qkv[j], fa_q_norm[j], fa_k_norm[j], True, None,
                                    fa_w_o[j], cache_k_full[:, j], cache_v_full[:, j], False, rope)
            new_kv.setdefault("kf", []).append(nk)
            new_kv.setdefault("vf", []).append(nv)
        else:
            x, nk, nv = _attn_mixer(x, g_mix, mod, wa_w_qkv[j], ones_hd, ones_hd, False, wa_sink[j],
                                    wa_w_o[j], cache_k_win[:, j], cache_v_win[:, j], True, rope)
            new_kv.setdefault("kw", []).append(nk)
            new_kv.setdefault("vw", []).append(nv)
        x = _moe(x, norm_ffn[layer].reshape(1, D_MODEL), mod, w_router, b_router,
                 moe_w_gate[layer], moe_w_up[layer], moe_w_down[layer])
    y = _final_norm(x, final_norm)
    y_prompt = y[:TP].reshape(BATCH, SEQ, D_MODEL)
    y_sample = y[TP:].reshape(DEC_BATCH, DEC_SEQ, D_MODEL)
    return (y_prompt, y_sample, jnp.stack(new_kv["kf"], axis=1), jnp.stack(new_kv["vf"], axis=1),
            jnp.stack(new_kv["kw"], axis=1), jnp.stack(new_kv["vw"], axis=1))
```

```python
import functools
import math

import jax
import jax.numpy as jnp
import numpy as np
from jax import lax
from jax.experimental import pallas as pl
from jax.experimental.pallas import tpu as pltpu

D_MODEL = 2048
BATCH = 32
SEQ = 256
DEPTH = 4
DEC_BATCH = 4
DEC_SEQ = 4096
PAST_LEN = 512
GRID_W = 64
N_HEADS = 16
N_KV_HEADS = 4
HEAD_DIM = D_MODEL // N_HEADS
KV_GROUP = N_HEADS // N_KV_HEADS
KV_DIM = N_KV_HEADS * HEAD_DIM
QKV_DIM = (N_HEADS + 2 * N_KV_HEADS) * HEAD_DIM
ROPE_THETA = 10000.0
WINDOW = 128
POOL_WINDOWS = (2, 4, 8, 16)
POOL_GROUP = D_MODEL // len(POOL_WINDOWS)
HYENA_EMB_BANDS = 16
HYENA_FILTER_HIDDEN = 64
N_EXPERTS = 16
N_EXPERT_GROUPS = 4
EXPERTS_PER_GROUP = 4
D_EXPERT = 512
NORM_EPS = 1e-6
NEG_INF = -1e30

F32 = jnp.float32
BF16 = jnp.bfloat16

TP = BATCH * SEQ
TS = DEC_BATCH * DEC_SEQ
T = TP + TS
N_COND = 8
LANE = 128
MIB = 1024 * 1024

PAIR_LO = (0, 0, 0, 1, 1, 2)
PAIR_HI = (1, 2, 3, 2, 3, 3)
N_BUCKETS = N_EXPERT_GROUPS * len(PAIR_LO)
MOE_TILE = 256
MOE_TILES = T // MOE_TILE + N_BUCKETS
T_PAD = MOE_TILES * MOE_TILE
XH_W = D_MODEL + LANE


def _cp(sem, vmem_mb=48):
    return pltpu.CompilerParams(dimension_semantics=sem, vmem_limit_bytes=vmem_mb * MIB)


def _dot(a, b):
    return jnp.dot(a, b, preferred_element_type=F32)


def _dot3(a, b):
    ah = a.astype(BF16)
    al = (a - ah.astype(F32)).astype(BF16)
    bh = b.astype(BF16)
    bl = (b - bh.astype(F32)).astype(BF16)
    return _dot(ah, bh) + (_dot(al, bh) + _dot(ah, bl))


def _sigmoid(x):
    return 1.0 / (1.0 + jnp.exp(-x))


def _cond_row(r):
    return jnp.where(r < TP, 0, 1 + (r - TP) // DEC_SEQ)


def _mod_spec(tm, chunk, tn=D_MODEL, ncol=False):
    if ncol:
        return pl.BlockSpec((None, None, 1, tn), lambda i, j: (_cond_row(i * tm), chunk, 0, j))
    return pl.BlockSpec((None, None, 1, tn), lambda i, *_: (_cond_row(i * tm), chunk, 0, 0))


def _norm_mod(x, g, shift, scale):
    var = jnp.mean(x * x, axis=-1, keepdims=True)
    y = x * lax.rsqrt(var + NORM_EPS) * g
    return y * (1.0 + scale) + shift


def _adaln_kernel(c_ref, w_ref, b_ref, o_ref):
    c = c_ref[...]
    a = c * _sigmoid(c)
    o_ref[...] = _dot3(a, w_ref[...]) + b_ref[...]


def _adaln(cond, w_mod, b_mod):
    tn = 1024
    n = 6 * D_MODEL
    return pl.pallas_call(
        _adaln_kernel,
        grid=(DEPTH, n // tn),
        in_specs=[
            pl.BlockSpec((N_COND, D_MODEL), lambda l, j: (0, 0)),
            pl.BlockSpec((None, D_MODEL, tn), lambda l, j: (l, 0, j)),
            pl.BlockSpec((None, 1, tn), lambda l, j: (l, 0, j)),
        ],
        out_specs=pl.BlockSpec((None, N_COND, tn), lambda l, j: (l, 0, j)),
        out_shape=jax.ShapeDtypeStruct((DEPTH, N_COND, n), F32),
        compiler_params=_cp(("parallel", "parallel")),
        name="adaln",
    )(cond, w_mod, b_mod.reshape(DEPTH, 1, n))


def _nm_matmul_kernel(x_ref, g_ref, sh_ref, sc_ref, w_ref, b_ref, o_ref, h_scr):
    @pl.when(pl.program_id(1) == 0)
    def _():
        h_scr[...] = _norm_mod(x_ref[...], g_ref[...], sh_ref[...], sc_ref[...]).astype(BF16)

    o_ref[...] = (_dot(h_scr[...], w_ref[...]) + b_ref[...]).astype(o_ref.dtype)


def _nm_matmul(x, g, mod, w, b, out_dtype, name):
    tm, tn = 1024, 1024
    n = w.shape[1]
    return pl.pallas_call(
        _nm_matmul_kernel,
        grid=(T // tm, n // tn),
        in_specs=[
            pl.BlockSpec((tm, D_MODEL), lambda i, j: (i, 0)),
            pl.BlockSpec((1, D_MODEL), lambda i, j: (0, 0)),
            _mod_spec(tm, 0),
            _mod_spec(tm, 1),
            pl.BlockSpec((D_MODEL, tn), lambda i, j: (0, j)),
            pl.BlockSpec((1, tn), lambda i, j: (0, j)),
        ],
        out_specs=pl.BlockSpec((tm, tn), lambda i, j: (i, j)),
        out_shape=jax.ShapeDtypeStruct((T, n), out_dtype),
        scratch_shapes=[pltpu.VMEM((tm, D_MODEL), BF16)],
        compiler_params=_cp(("parallel", "arbitrary")),
        name=name,
    )(x, g, mod, mod, w, b)


def _resid_matmul_kernel(a_ref, w_ref, b_ref, x_ref, gt_ref, o_ref):
    o_ref[...] = x_ref[...] + gt_ref[...] * (_dot(a_ref[...], w_ref[...]) + b_ref[...])


def _resid_matmul(a, w, b, x, mod, name):
    tm, tn = 1024, 1024
    k = a.shape[1]
    return pl.pallas_call(
        _resid_matmul_kernel,
        grid=(T // tm, D_MODEL // tn),
        in_specs=[
            pl.BlockSpec((tm, k), lambda i, j: (i, 0)),
            pl.BlockSpec((k, tn), lambda i, j: (0, j)),
            pl.BlockSpec((1, tn), lambda i, j: (0, j)),
            pl.BlockSpec((tm, tn), lambda i, j: (i, j)),
            _mod_spec(tm, 2, tn, ncol=True),
        ],
        out_specs=pl.BlockSpec((tm, tn), lambda i, j: (i, j)),
        out_shape=jax.ShapeDtypeStruct((T, D_MODEL), F32),
        compiler_params=_cp(("parallel", "parallel")),
        name=name,
    )(a, w, b, x, mod)


POOL_TILE = 256
POOL_HALO = 8


def _seq_pos(r0):
    is_ctx = r0 < TP
    loc0 = jnp.where(is_ctx, r0 % SEQ, (r0 - TP) % DEC_SEQ)
    seq_len = jnp.where(is_ctx, SEQ, DEC_SEQ)
    return loc0, seq_len


def _pool_kernel(x_ref, xp_ref, xn_ref, g_ref, sh_ref, sc_ref, gt_ref, pw_ref, ps_ref, o_ref, hz_scr):
    tm, hl = POOL_TILE, POOL_HALO
    loc0, seq_len = _seq_pos(pl.program_id(0) * tm)
    has_prev = loc0 > 0
    has_next = loc0 + tm < seq_len
    g, sh, sc = g_ref[...], sh_ref[...], sc_ref[...]
    x = x_ref[...]
    h = _norm_mod(x, g, sh, sc)
    hz_scr[0:hl, :] = jnp.where(has_prev, _norm_mod(xp_ref[...], g, sh, sc), 0.0)
    hz_scr[hl:hl + tm, :] = h
    hz_scr[hl + tm:, :] = jnp.where(has_next, _norm_mod(xn_ref[...], g, sh, sc), 0.0)
    tl = loc0 + lax.broadcasted_iota(jnp.int32, (tm, 1), 0)
    outs = []
    for gi, w in enumerate(POOL_WINDOWS):
        cs = slice(gi * POOL_GROUP, (gi + 1) * POOL_GROUP)
        s = jnp.zeros((tm, POOL_GROUP), F32)
        for off in range(-(w // 2), w - w // 2):
            s = s + hz_scr[hl + off:hl + off + tm, cs]
        lo = jnp.maximum(tl - w // 2, 0)
        hi = jnp.minimum(tl + (w - w // 2), seq_len)
        d = s / (hi - lo).astype(F32) - h[:, cs]
        outs.append(_dot(d.astype(BF16), pw_ref[gi]))
    out = jnp.concatenate(outs, axis=1) * ps_ref[...]
    o_ref[...] = x + gt_ref[...] * out


def _pool_mixer(x, g, mod, pool_w, pool_scale):
    tm, hl = POOL_TILE, POOL_HALO
    r = tm // hl
    return pl.pallas_call(
        _pool_kernel,
        grid=(T // tm,),
        in_specs=[
            pl.BlockSpec((tm, D_MODEL), lambda i: (i, 0)),
            pl.BlockSpec((hl, D_MODEL), lambda i: (jnp.maximum(i * r - 1, 0), 0)),
            pl.BlockSpec((hl, D_MODEL), lambda i: (jnp.minimum((i + 1) * r, T // hl - 1), 0)),
            pl.BlockSpec((1, D_MODEL), lambda i: (0, 0)),
            _mod_spec(tm, 0),
            _mod_spec(tm, 1),
            _mod_spec(tm, 2),
            pl.BlockSpec((len(POOL_WINDOWS), POOL_GROUP, POOL_GROUP), lambda i: (0, 0, 0)),
            pl.BlockSpec((1, D_MODEL), lambda i: (0, 0)),
        ],
        out_specs=pl.BlockSpec((tm, D_MODEL), lambda i: (i, 0)),
        out_shape=jax.ShapeDtypeStruct((T, D_MODEL), F32),
        scratch_shapes=[pltpu.VMEM((tm + 2 * hl, D_MODEL), F32)],
        compiler_params=_cp(("parallel",)),
        name="pool_mixer",
    )(x, x, x, g, mod, mod, mod, pool_w.astype(BF16), pool_scale.reshape(1, D_MODEL))


CONV_TILE = 256
CONV_HALO = 16


def _conv3_kernel(u_ref, up_ref, un_ref, cw_ref, cb_ref, o_ref, scr):
    tm, hl = CONV_TILE, CONV_HALO
    loc0, seq_len = _seq_pos(pl.program_id(0) * tm)
    has_prev = loc0 > 0
    has_next = loc0 + tm < seq_len
    scr[0:hl, :] = jnp.where(has_prev, up_ref[...].astype(F32), 0.0)
    scr[hl:hl + tm, :] = u_ref[...].astype(F32)
    scr[hl + tm:, :] = jnp.where(has_next, un_ref[...].astype(F32), 0.0)
    out = (scr[hl - 1:hl - 1 + tm, :] * cw_ref[0:1, :] + scr[hl:hl + tm, :] * cw_ref[1:2, :]
           + scr[hl + 1:hl + 1 + tm, :] * cw_ref[2:3, :] + cb_ref[...])
    o_ref[...] = out.astype(o_ref.dtype)


def _conv3(u0, conv_w, conv_b):
    tm, hl, tc = CONV_TILE, CONV_HALO, D_MODEL
    r = tm // hl
    n = u0.shape[1]
    return pl.pallas_call(
        _conv3_kernel,
        grid=(T // tm, n // tc),
        in_specs=[
            pl.BlockSpec((tm, tc), lambda i, j: (i, j)),
            pl.BlockSpec((hl, tc), lambda i, j: (jnp.maximum(i * r - 1, 0), j)),
            pl.BlockSpec((hl, tc), lambda i, j: (jnp.minimum((i + 1) * r, T // hl - 1), j)),
            pl.BlockSpec((3, tc), lambda i, j: (0, j)),
            pl.BlockSpec((1, tc), lambda i, j: (0, j)),
        ],
        out_specs=pl.BlockSpec((tm, tc), lambda i, j: (i, j)),
        out_shape=jax.ShapeDtypeStruct((T, n), BF16),
        scratch_shapes=[pltpu.VMEM((tm + 2 * hl, tc), F32)],
        compiler_params=_cp(("parallel", "parallel")),
        name="hyena_conv3",
    )(u0, u0, u0, conv_w, conv_b.reshape(1, n))


FILT_TILE = 256


def _filter_kernel(emb_ref, w1_ref, b1_ref, w2_ref, b2_ref, fr_ref, w3_ref, dc_ref, fa_ref, fb_ref):
    emb = emb_ref[...]
    fr = fr_ref[...]
    a = jnp.sin(fr * (_dot3(emb, w1_ref[...]) + b1_ref[...]))
    a = jnp.sin(fr * (_dot3(a, w2_ref[...]) + b2_ref[...]))
    t = emb[:, 0:1]
    filt = _dot3(a, w3_ref[...]) * jnp.exp(-t * jnp.abs(dc_ref[...]))
    hf = filt[:, :D_MODEL]
    row = pl.program_id(0) * FILT_TILE + lax.broadcasted_iota(jnp.int32, (FILT_TILE, 1), 0)
    hb = jnp.where(row == 0, 0.0, filt[:, D_MODEL:])
    fa_ref[...] = (hf + hb).astype(BF16)
    fb_ref[...] = (hb - hf).astype(BF16)


def _pad2(a, rows, cols):
    return jnp.pad(a, ((0, rows - a.shape[0]), (0, cols - a.shape[1])))


def _hyena_filters(L, f_w1, f_b1, f_w2, f_b2, f_freq, f_w3, decay):
    t = jnp.arange(L, dtype=F32) / L
    bands = jnp.linspace(1e-4, HYENA_EMB_BANDS - 1, HYENA_EMB_BANDS, dtype=F32)
    ang = (2 * math.pi) * t[:, None] * bands[None, :]
    emb = _pad2(jnp.concatenate([t[:, None], jnp.cos(ang), -jnp.sin(ang)], axis=-1), L, LANE)
    nf = 2 * D_MODEL
    tl = FILT_TILE
    small = lambda: pl.BlockSpec((LANE, LANE), lambda i, n: (0, 0))
    vec = lambda: pl.BlockSpec((1, LANE), lambda i, n: (0, 0))
    out = pl.BlockSpec((None, tl, D_MODEL), lambda i, n: (n, i, 0))
    return pl.pallas_call(
        _filter_kernel,
        grid=(L // tl, 2),
        in_specs=[
            pl.BlockSpec((tl, LANE), lambda i, n: (i, 0)),
            small(), vec(), small(), vec(), vec(),
            pl.BlockSpec((LANE, nf), lambda i, n: (0, n)),
            pl.BlockSpec((1, nf), lambda i, n: (0, n)),
        ],
        out_specs=[out, out],
        out_shape=[jax.ShapeDtypeStruct((2, L, D_MODEL), BF16)] * 2,
        compiler_params=_cp(("parallel", "parallel")),
        name="hyena_filters",
    )(emb, _pad2(f_w1, LANE, LANE), _pad2(f_b1[None], 1, LANE), _pad2(f_w2, LANE, LANE),
      _pad2(f_b2[None], 1, LANE), _pad2(f_freq[None], 1, LANE), _pad2(f_w3, LANE, 2 * nf), decay[None])


def _dft_mats(L):
    r = int(math.isqrt(L))
    k2 = 2 * jnp.arange(L, dtype=jnp.int32)[:, None] + 1
    n1 = r * jnp.arange(L // r, dtype=jnp.int32)[None, :]
    n2 = jnp.arange(r, dtype=jnp.int32)[None, :]
    sc = math.pi / (2 * L)
    aa = ((k2 * n1) % (4 * L)).astype(F32) * sc
    ab = ((k2 * n2) % (4 * L)).astype(F32) * sc
    ca, sa, cb, sb = jnp.cos(aa)[:, :, None], jnp.sin(aa)[:, :, None], jnp.cos(ab)[:, None, :], jnp.sin(ab)[:, None, :]
    c = (ca * cb - sa * sb).reshape(L, L)
    s = (sa * cb + ca * sb).reshape(L, L)
    return c.astype(BF16), s.astype(BF16), c.T.astype(BF16), s.T.astype(BF16)


def _dft_tiles(L):
    return min(512, L), 512


def _dft_filter_kernel(c_ref, s_ref, a_ref, b_ref, gr_ref, gi_ref):
    gr_ref[...] = _dot(c_ref[...], a_ref[...])
    gi_ref[...] = _dot(s_ref[...], b_ref[...])


def _dft_filter(cm, sm, fa, fb, L):
    tf, tn = _dft_tiles(L)
    mat = lambda: pl.BlockSpec((tf, L), lambda k, c, n: (k, 0))
    rhs = lambda: pl.BlockSpec((None, L, tn), lambda k, c, n: (n, 0, c))
    out = pl.BlockSpec((None, tf, tn), lambda k, c, n: (n, k, c))
    return pl.pallas_call(
        _dft_filter_kernel,
        grid=(L // tf, D_MODEL // tn, 2),
        in_specs=[mat(), mat(), rhs(), rhs()],
        out_specs=[out, out],
        out_shape=[jax.ShapeDtypeStruct((2, L, D_MODEL), F32)] * 2,
        compiler_params=_cp(("parallel", "parallel", "parallel")),
        name="hyena_filter_dft",
    )(cm, sm, fa, fb)


def _dft_fwd_kernel(c_ref, s_ref, z_ref, gr_ref, gi_ref, yr_ref, yi_ref):
    z = z_ref[...]
    zc = _dot(c_ref[...], z)
    zs = _dot(s_ref[...], z)
    gr, gi = gr_ref[...], gi_ref[...]
    yr_ref[...] = (gr * zc + gi * zs).astype(BF16)
    yi_ref[...] = (gi * zc - gr * zs).astype(BF16)


def _dft_fwd(cm, sm, z, z_rowblk, z_colblk, gr, gi, order, nb, L):
    tf, tn = _dft_tiles(L)
    mat = lambda: pl.BlockSpec((tf, L), lambda k, c, b: (k, 0))
    gsp = lambda: pl.BlockSpec((None, tf, tn), lambda k, c, b: (order, k, c))
    out = pl.BlockSpec((None, tf, tn), lambda k, c, b: (b, k, c))
    return pl.pallas_call(
        _dft_fwd_kernel,
        grid=(L // tf, D_MODEL // tn, nb),
        in_specs=[mat(), mat(),
                  pl.BlockSpec((L, tn), lambda k, c, b: (z_rowblk + b, z_colblk + c)),
                  gsp(), gsp()],
        out_specs=[out, out],
        out_shape=[jax.ShapeDtypeStruct((nb, L, D_MODEL), BF16)] * 2,
        compiler_params=_cp(("parallel", "parallel", "parallel")),
        name="hyena_dft_fwd",
    )(cm, sm, z, gr, gi)


def _dft_inv_kernel(ct_ref, st_ref, yr_ref, yi_ref, z_ref, gt_ref, sk_ref, o_ref, *, inv_len):
    y = (_dot(ct_ref[...], yr_ref[...]) - _dot(st_ref[...], yi_ref[...])) * inv_len
    o_ref[...] = (gt_ref[...].astype(F32) * (y + sk_ref[...] * z_ref[...].astype(F32))).astype(BF16)


def _dft_inv(ctm, stm, yr, yi, z, z_rowblk, z_colblk, gate, g_rowblk, g_colblk, skip, nb, L):
    tt, tn = _dft_tiles(L)
    rpb = L // tt
    mat = lambda: pl.BlockSpec((tt, L), lambda t, c, b: (t, 0))
    spec = lambda: pl.BlockSpec((None, L, tn), lambda t, c, b: (b, 0, c))
    return pl.pallas_call(
        functools.partial(_dft_inv_kernel, inv_len=1.0 / L),
        grid=(rpb, D_MODEL // tn, nb),
        in_specs=[mat(), mat(), spec(), spec(),
                  pl.BlockSpec((tt, tn), lambda t, c, b: (z_rowblk + b * rpb + t, z_colblk + c)),
                  pl.BlockSpec((tt, tn), lambda t, c, b: (g_rowblk + b * rpb + t, g_colblk + c)),
                  pl.BlockSpec((1, tn), lambda t, c, b: (0, c))],
        out_specs=pl.BlockSpec((tt, tn), lambda t, c, b: (b * rpb + t, c)),
        out_shape=jax.ShapeDtypeStruct((nb * L, D_MODEL), BF16),
        compiler_params=_cp(("parallel", "parallel", "parallel")),
        name="hyena_dft_inv",
    )(ctm, stm, yr, yi, z, gate, skip)


def _hyena_stream(u, row0, nb, L, fparams, skip):
    cm, sm, ctm, stm = _dft_mats(L)
    fa, fb = _hyena_filters(L, *fparams)
    gr, gi = _dft_filter(cm, sm, fa, fb, L)
    tt, tn = _dft_tiles(L)
    ncb = D_MODEL // tn
    yr, yi = _dft_fwd(cm, sm, u, row0 // L, 0, gr, gi, 0, nb, L)
    z1 = _dft_inv(ctm, stm, yr, yi, u, row0 // tt, 0, u, row0 // tt, ncb, skip[0:1], nb, L)
    yr, yi = _dft_fwd(cm, sm, z1, 0, 0, gr, gi, 1, nb, L)
    return _dft_inv(ctm, stm, yr, yi, z1, 0, 0, u, row0 // tt, 2 * ncb, skip[1:2], nb, L)


def _hyena_mixer(x, g, mod, w_in, b_in, conv_w, conv_b, f_w1, f_b1, f_w2, f_b2, f_freq, f_w3, decay, skip,
                 w_out, b_out):
    u0 = _nm_matmul(x, g, mod, w_in.astype(BF16), b_in.reshape(1, -1), BF16, "hyena_in_proj")
    u = _conv3(u0, conv_w, conv_b)
    fparams = (f_w1, f_b1, f_w2, f_b2, f_freq, f_w3, decay)
    zp = _hyena_stream(u, 0, BATCH, SEQ, fparams, skip)
    zs = _hyena_stream(u, TP, DEC_BATCH, DEC_SEQ, fparams, skip)
    z = jnp.concatenate([zp, zs], axis=0)
    return _resid_matmul(z, w_out.astype(BF16), b_out.reshape(1, -1), x, mod, "hyena_out_proj")


PREP_TILE = 256


def _rope_tables():
    pos = jnp.arange(DEC_SEQ, dtype=jnp.int32)
    row = (pos // GRID_W).astype(F32)
    col = (pos % GRID_W).astype(F32)
    axis_dim = HEAD_DIM // 2
    inv_freq = ROPE_THETA ** (-jnp.arange(0, axis_dim, 2, dtype=F32) / axis_dim)
    ar = row[:, None] * inv_freq[None, :]
    ac = col[:, None] * inv_freq[None, :]
    cos = jnp.concatenate([jnp.cos(ar), jnp.cos(ar), jnp.cos(ac), jnp.cos(ac)], axis=-1)
    sin = jnp.concatenate([-jnp.sin(ar), jnp.sin(ar), -jnp.sin(ac), jnp.sin(ac)], axis=-1)
    return cos, sin


def _prep_kernel(*refs, use_norm, use_rope, emit_kv):
    it = iter(refs)
    qkv_ref, qn_ref, kn_ref = next(it), next(it), next(it)
    cos_ref = sin_ref = None
    if use_rope:
        cos_ref, sin_ref = next(it), next(it)
    q_ref, k_ref, v_ref = next(it), next(it), next(it)
    nk_ref = nv_ref = None
    if emit_kv:
        nk_ref, nv_ref = next(it), next(it)
    quarter = HEAD_DIM // 4
    if use_rope:
        cos, sin = cos_ref[...], sin_ref[...]
        lane = lax.broadcasted_iota(jnp.int32, (PREP_TILE, HEAD_DIM), 1)
        first = (lane % (2 * quarter)) < quarter

    def head(xh, gn):
        if use_norm:
            xh = xh * lax.rsqrt(jnp.mean(xh * xh, axis=-1, keepdims=True) + NORM_EPS) * gn
        return xh

    def rope(xh):
        if not use_rope:
            return xh
        partner = jnp.where(first, pltpu.roll(xh, HEAD_DIM - quarter, 1), pltpu.roll(xh, quarter, 1))
        return xh * cos + partner * sin

    qn, kn = qn_ref[...], kn_ref[...]
    scale = HEAD_DIM ** -0.5 * LOG2E
    for h in range(N_HEADS):
        hs = slice(h * HEAD_DIM, (h + 1) * HEAD_DIM)
        q_ref[:, hs] = (rope(head(qkv_ref[:, hs], qn)) * scale).astype(BF16)
    for h in range(N_KV_HEADS):
        hs = slice(h * HEAD_DIM, (h + 1) * HEAD_DIM)
        ks = slice(D_MODEL + h * HEAD_DIM, D_MODEL + (h + 1) * HEAD_DIM)
        vs = slice(D_MODEL + KV_DIM + h * HEAD_DIM, D_MODEL + KV_DIM + (h + 1) * HEAD_DIM)
        kh = head(qkv_ref[:, ks], kn)
        vh = qkv_ref[:, vs]
        if emit_kv:
            nk_ref[:, hs] = kh
            nv_ref[:, hs] = vh
        k_ref[:, hs] = rope(kh).astype(BF16)
        v_ref[:, hs] = vh.astype(BF16)


def _attn_prep(qkv, row0, nrows, q_norm, k_norm, use_norm, rope, emit_kv):
    tm = PREP_TILE
    blk0 = row0 // tm
    in_specs = [pl.BlockSpec((tm, QKV_DIM), lambda i: (blk0 + i, 0)),
                pl.BlockSpec((1, HEAD_DIM), lambda i: (0, 0)),
                pl.BlockSpec((1, HEAD_DIM), lambda i: (0, 0))]
    args = [qkv, q_norm.reshape(1, HEAD_DIM), k_norm.reshape(1, HEAD_DIM)]
    if rope is not None:
        tab = lambda: pl.BlockSpec((tm, HEAD_DIM), lambda i: (i % (DEC_SEQ // tm), 0))
        in_specs += [tab(), tab()]
        args += list(rope)
    row = lambda w: pl.BlockSpec((tm, w), lambda i: (i, 0))
    out_specs = [row(D_MODEL), row(KV_DIM), row(KV_DIM)]
    out_shape = [jax.ShapeDtypeStruct((nrows, D_MODEL), BF16), jax.ShapeDtypeStruct((nrows, KV_DIM), BF16),
                 jax.ShapeDtypeStruct((nrows, KV_DIM), BF16)]
    if emit_kv:
        out_specs += [row(KV_DIM), row(KV_DIM)]
        out_shape += [jax.ShapeDtypeStruct((nrows, KV_DIM), F32)] * 2
    return pl.pallas_call(
        functools.partial(_prep_kernel, use_norm=use_norm, use_rope=rope is not None, emit_kv=emit_kv),
        grid=(nrows // tm,),
        in_specs=in_specs, out_specs=out_specs, out_shape=out_shape,
        compiler_params=_cp(("parallel",)),
        name="attn_prep",
    )(*args)


LOG2E = math.log2(math.e)
ATTN_TQ = 256


def _attn_kernel(*refs, tq, seq_len, n_ctx, windowed, has_sink):
    it = iter(refs)
    q_ref, k_ref, v_ref = next(it), next(it), next(it)
    sink_ref = next(it) if has_sink else None
    o_ref = next(it)
    if windowed:
        i = pl.program_id(2)
        span = tq + 2 * WINDOW
        start = pl.multiple_of(jnp.clip(i * tq - WINDOW, 0, seq_len - span), WINDOW)
        qpos = i * tq + lax.broadcasted_iota(jnp.int32, (tq, 1), 0)
        kpos = start + lax.broadcasted_iota(jnp.int32, (1, span), 1)
        segs = [(pl.ds(start, span), jnp.abs(kpos - qpos) <= WINDOW), (pl.ds(seq_len, n_ctx), None)]
    else:
        segs = [(slice(None), None)]
    for h in range(KV_GROUP):
        hs = slice(h * HEAD_DIM, (h + 1) * HEAD_DIM)
        qh = q_ref[:, hs]
        scores = []
        m = None
        for rows, mask in segs:
            s = lax.dot_general(qh, k_ref[rows, :], (((1,), (1,)), ((), ())), preferred_element_type=F32)
            if mask is not None:
                s = jnp.where(mask, s, NEG_INF)
            scores.append(s)
            ms = jnp.max(s, axis=-1, keepdims=True)
            m = ms if m is None else jnp.maximum(m, ms)
        if has_sink:
            sk = sink_ref[pl.program_id(1) * KV_GROUP + h]
            m = jnp.maximum(m, sk)
        l = jnp.exp2(sk - m) if has_sink else jnp.zeros_like(m)
        acc = jnp.zeros((tq, HEAD_DIM), F32)
        for (rows, _), s in zip(segs, scores):
            p = jnp.exp2(s - m)
            l = l + jnp.sum(p, axis=-1, keepdims=True)
            acc = acc + _dot(p.astype(BF16), v_ref[rows, :])
        o_ref[:, hs] = (acc / l).astype(BF16)


def _attention(q, k, v, sink, nb, L, n_ctx, windowed):
    tq = min(ATTN_TQ, L)
    nq = L // tq
    nk = L + n_ctx
    kv = lambda: pl.BlockSpec((None, nk, HEAD_DIM), lambda b, g, i: (b, 0, g))
    in_specs = [pl.BlockSpec((tq, KV_GROUP * HEAD_DIM), lambda b, g, i: (b * nq + i, g)), kv(), kv()]
    args = [q, k, v]
    if sink is not None:
        in_specs.append(pl.BlockSpec(memory_space=pltpu.SMEM))
        args.append(sink.astype(F32) * LOG2E)
    return pl.pallas_call(
        functools.partial(_attn_kernel, tq=tq, seq_len=L, n_ctx=n_ctx, windowed=windowed,
                          has_sink=sink is not None),
        grid=(nb, N_KV_HEADS, nq),
        in_specs=in_specs,
        out_specs=pl.BlockSpec((tq, KV_GROUP * HEAD_DIM), lambda b, g, i: (b * nq + i, g)),
        out_shape=jax.ShapeDtypeStruct((nb * L, D_MODEL), BF16),
        compiler_params=_cp(("parallel", "parallel", "parallel"), 56),
        name="attention",
    )(*args)


def _attn_mixer(x, g, mod, w_qkv, q_norm, k_norm, use_norm, sink, w_o, cache_k, cache_v, windowed, rope):
    zero_b = jnp.zeros((1, QKV_DIM), F32)
    qkv = _nm_matmul(x, g, mod, w_qkv.astype(BF16), zero_b, F32, "qkv_proj")
    qp, kp, vp, new_k, new_v = _attn_prep(qkv, 0, TP, q_norm, k_norm, use_norm, None, True)
    qs, ks, vs = _attn_prep(qkv, TP, TS, q_norm, k_norm, use_norm, rope, False)
    op = _attention(qp, kp.reshape(BATCH, SEQ, KV_DIM), vp.reshape(BATCH, SEQ, KV_DIM), sink, BATCH, SEQ, 0, False)
    kc = cache_k.reshape(DEC_BATCH, PAST_LEN, KV_DIM).astype(BF16)
    vc = cache_v.reshape(DEC_BATCH, PAST_LEN, KV_DIM).astype(BF16)
    k_all = jnp.concatenate([ks.reshape(DEC_BATCH, DEC_SEQ, KV_DIM), kc], axis=1)
    v_all = jnp.concatenate([vs.reshape(DEC_BATCH, DEC_SEQ, KV_DIM), vc], axis=1)
    osm = _attention(qs, k_all, v_all, sink, DEC_BATCH, DEC_SEQ, PAST_LEN, windowed)
    o = jnp.concatenate([op, osm], axis=0)
    x = _resid_matmul(o, w_o.astype(BF16), jnp.zeros((1, D_MODEL), F32), x, mod, "attn_out_proj")
    shape = (BATCH, SEQ, N_KV_HEADS, HEAD_DIM)
    return x, new_k.reshape(shape), new_v.reshape(shape)


ROUTE_TILE = 512
ROUTE_ROWS = 32


def _router_kernel(x_ref, g_ref, sh_ref, sc_ref, wr_ref, br_ref, xh_ref, rt_ref, cnt_ref, carry):
    tm = ROUTE_TILE
    i = pl.program_id(0)

    @pl.when(i == 0)
    def _():
        carry[...] = jnp.zeros_like(carry)

    h = _norm_mod(x_ref[...], g_ref[...], sh_ref[...], sc_ref[...])
    xh_ref[:, :D_MODEL] = h
    logits = _dot(h.astype(BF16), wr_ref[...])
    s = _sigmoid(logits.T[:N_EXPERTS, :])
    sb = s + br_ref[...]
    u = [s[e:e + 1, :] for e in range(N_EXPERTS)]
    v = [sb[e:e + 1, :] for e in range(N_EXPERTS)]

    gscore = []
    for gq in range(N_EXPERT_GROUPS):
        m = v[4 * gq:4 * gq + 4]
        best = m[PAIR_LO[0]] + m[PAIR_HI[0]]
        for a, b in zip(PAIR_LO[1:], PAIR_HI[1:]):
            best = jnp.maximum(best, m[a] + m[b])
        gscore.append(best)
    gidx = jnp.zeros((1, tm), jnp.int32)
    gbest = gscore[0]
    for gq in range(1, N_EXPERT_GROUPS):
        upd = gscore[gq] > gbest
        gidx = jnp.where(upd, gq, gidx)
        gbest = jnp.where(upd, gscore[gq], gbest)

    def pick(rows, j):
        out = rows[j]
        for gq in range(1, N_EXPERT_GROUPS):
            out = jnp.where(gidx == gq, rows[4 * gq + j], out)
        return out

    vin = [pick(v, j) for j in range(EXPERTS_PER_GROUP)]
    uin = [pick(u, j) for j in range(EXPERTS_PER_GROUP)]
    i1 = jnp.zeros((1, tm), jnp.int32)
    m1 = vin[0]
    for j in range(1, EXPERTS_PER_GROUP):
        upd = vin[j] > m1
        i1 = jnp.where(upd, j, i1)
        m1 = jnp.where(upd, vin[j], m1)
    i2 = jnp.full((1, tm), -1, jnp.int32)
    m2 = jnp.full((1, tm), -jnp.inf, F32)
    for j in range(EXPERTS_PER_GROUP):
        upd = (i1 != j) & (vin[j] > m2)
        i2 = jnp.where(upd, j, i2)
        m2 = jnp.where(upd, vin[j], m2)

    def sel(rows, idx):
        out = rows[0]
        for j in range(1, EXPERTS_PER_GROUP):
            out = jnp.where(idx == j, rows[j], out)
        return out

    w1, w2 = sel(uin, i1), sel(uin, i2)
    wsum = w1 + w2
    w1, w2 = w1 / wsum, w2 / wsum
    first_lo = i1 < i2
    lo = jnp.where(first_lo, i1, i2)
    hi = jnp.where(first_lo, i2, i1)
    w_lo = jnp.where(first_lo, w1, w2)
    w_hi = jnp.where(first_lo, w2, w1)
    pair = jnp.where(lo == 0, hi - 1, jnp.where(lo == 1, hi + 1, 5))
    bucket = gidx * len(PAIR_LO) + pair

    onehot = (lax.broadcasted_iota(jnp.int32, (ROUTE_ROWS, tm), 0) == bucket)
    tri = (lax.broadcasted_iota(jnp.int32, (tm, tm), 0) <= lax.broadcasted_iota(jnp.int32, (tm, tm), 1))
    cum = _dot(jnp.where(onehot, 1.0, 0.0).astype(BF16), jnp.where(tri, 1.0, 0.0).astype(BF16))
    rank = jnp.sum(jnp.where(onehot, cum - 1.0 + carry[...], 0.0), axis=0, keepdims=True)
    carry[...] = carry[...] + cum[:, tm - 1:tm]
    cnt_ref[...] = jnp.broadcast_to(carry[...], (ROUTE_ROWS, LANE))

    rt_ref[...] = jnp.zeros_like(rt_ref)
    rt_ref[0:1, :] = bucket.astype(F32)
    rt_ref[1:2, :] = rank
    wt = jnp.concatenate([w_lo, w_hi, jnp.zeros((LANE - 2, tm), F32)], axis=0)
    xh_ref[:, D_MODEL:] = wt.T


def _router(x, g, mod, w_router, b_router):
    tm = ROUTE_TILE
    wr = _pad2(w_router, D_MODEL, LANE).astype(BF16)
    return pl.pallas_call(
        _router_kernel,
        grid=(T // tm,),
        in_specs=[
            pl.BlockSpec((tm, D_MODEL), lambda i: (i, 0)),
            pl.BlockSpec((1, D_MODEL), lambda i: (0, 0)),
            _mod_spec(tm, 3),
            _mod_spec(tm, 4),
            pl.BlockSpec((D_MODEL, LANE), lambda i: (0, 0)),
            pl.BlockSpec((N_EXPERTS, 1), lambda i: (0, 0)),
        ],
        out_specs=[
            pl.BlockSpec((tm, XH_W), lambda i: (i, 0)),
            pl.BlockSpec((8, tm), lambda i: (0, i)),
            pl.BlockSpec((ROUTE_ROWS, LANE), lambda i: (0, 0)),
        ],
        out_shape=[
            jax.ShapeDtypeStruct((T, XH_W), F32),
            jax.ShapeDtypeStruct((8, T), F32),
            jax.ShapeDtypeStruct((ROUTE_ROWS, LANE), F32),
        ],
        scratch_shapes=[pltpu.VMEM((ROUTE_ROWS, 1), F32)],
        compiler_params=_cp(("arbitrary",)),
        name="moe_router",
    )(x, g, mod, mod, wr, b_router.reshape(N_EXPERTS, 1))


DISPATCH_TILE = 256


DMA_UNROLL = 8


def _invert_kernel(dest_ref, src_ref):
    def clear(s, c):
        src_ref[s] = 0
        return c

    def put(t, c):
        src_ref[dest_ref[t]] = t
        return c

    lax.fori_loop(0, T_PAD, clear, 0, unroll=DMA_UNROLL)
    lax.fori_loop(0, T, put, 0, unroll=DMA_UNROLL)


def _invert(dest):
    return pl.pallas_call(
        _invert_kernel,
        in_specs=[pl.BlockSpec(memory_space=pltpu.SMEM)],
        out_specs=pl.BlockSpec(memory_space=pltpu.SMEM),
        out_shape=jax.ShapeDtypeStruct((T_PAD,), jnp.int32),
        name="moe_invert",
    )(dest)


def _gather_rows(idx_ref, base, src_hbm, buf, sem, tm):
    def start(r, c):
        pltpu.make_async_copy(src_hbm.at[pl.ds(idx_ref[base + r], 1)], buf.at[pl.ds(r, 1)], sem).start()
        return c

    lax.fori_loop(0, tm, start, 0, unroll=DMA_UNROLL)


def _wait_rows(src_hbm, buf, sem, tm):
    pltpu.make_async_copy(src_hbm.at[pl.ds(0, tm)], buf, sem).wait()


def _expert_kernel(ea_ref, eb_ref, nv_ref, src_ref, xh_hbm, ga_ref, ua_ref, da_ref, gb_ref, ub_ref, db_ref, y_ref,
                   xbuf, sems):
    tm = MOE_TILE
    j = pl.program_id(0)
    slot = j % 2
    nv = nv_ref[j]

    @pl.when(j == 0)
    def _():
        _gather_rows(src_ref, 0, xh_hbm, xbuf.at[0], sems.at[0], tm)

    @pl.when(jnp.logical_and(j + 1 < MOE_TILES, nv_ref[jnp.minimum(j + 1, MOE_TILES - 1)] > 0))
    def _():
        _gather_rows(src_ref, (j + 1) * tm, xh_hbm, xbuf.at[1 - slot], sems.at[1 - slot], tm)

    @pl.when(nv > 0)
    def _():
        _wait_rows(xh_hbm, xbuf.at[slot], sems.at[slot], tm)
        valid = lax.broadcasted_iota(jnp.int32, (tm, 1), 0) < nv
        x = jnp.where(valid, xbuf[slot, :, :D_MODEL], 0.0).astype(BF16)
        wts = jnp.where(valid, xbuf[slot, :, D_MODEL:], 0.0)

        def ffn(g_ref, u_ref, d_ref, w):
            a = _dot(x, g_ref[...])
            h = a * _sigmoid(a) * _dot(x, u_ref[...]) * w
            return _dot(h.astype(BF16), d_ref[...])

        y_ref[...] = ffn(ga_ref, ua_ref, da_ref, wts[:, 0:1]) + ffn(gb_ref, ub_ref, db_ref, wts[:, 1:2])

    @pl.when(nv == 0)
    def _():
        y_ref[...] = jnp.zeros_like(y_ref)


def _experts(tile_ea, tile_eb, tile_nv, src, xh, w_gate, w_up, w_down):
    tm = MOE_TILE
    up = lambda sel: pl.BlockSpec((None, D_MODEL, D_EXPERT), lambda j, ea, eb, nv, sr: ((ea, eb)[sel][j], 0, 0))
    down = lambda sel: pl.BlockSpec((None, D_EXPERT, D_MODEL), lambda j, ea, eb, nv, sr: ((ea, eb)[sel][j], 0, 0))
    return pl.pallas_call(
        _expert_kernel,
        grid_spec=pltpu.PrefetchScalarGridSpec(
            num_scalar_prefetch=4,
            grid=(MOE_TILES,),
            in_specs=[pl.BlockSpec(memory_space=pl.ANY), up(0), up(0), down(0), up(1), up(1), down(1)],
            out_specs=pl.BlockSpec((tm, D_MODEL), lambda j, ea, eb, nv, sr: (j, 0)),
            scratch_shapes=[pltpu.VMEM((2, tm, XH_W), F32), pltpu.SemaphoreType.DMA((2,))],
        ),
        out_shape=jax.ShapeDtypeStruct((T_PAD, D_MODEL), F32),
        compiler_params=_cp(("arbitrary",), 56),
        name="moe_experts",
    )(tile_ea, tile_eb, tile_nv, src, xh, w_gate, w_up, w_down, w_gate, w_up, w_down)


def _combine_kernel(dest_ref, x_ref, gt_ref, ys_hbm, o_ref, buf, sems):
    tm = DISPATCH_TILE
    i = pl.program_id(0)
    slot = i % 2

    @pl.when(i == 0)
    def _():
        _gather_rows(dest_ref, 0, ys_hbm, buf.at[0], sems.at[0], tm)

    @pl.when(i + 1 < pl.num_programs(0))
    def _():
        _gather_rows(dest_ref, (i + 1) * tm, ys_hbm, buf.at[1 - slot], sems.at[1 - slot], tm)

    _wait_rows(ys_hbm, buf.at[slot], sems.at[slot], tm)
    o_ref[...] = x_ref[...] + gt_ref[...] * buf[slot]


def _combine(dest, x, mod, ys):
    tm = DISPATCH_TILE
    return pl.pallas_call(
        _combine_kernel,
        grid_spec=pltpu.PrefetchScalarGridSpec(
            num_scalar_prefetch=1,
            grid=(T // tm,),
            in_specs=[pl.BlockSpec((tm, D_MODEL), lambda i, d: (i, 0)),
                      pl.BlockSpec((None, None, 1, D_MODEL), lambda i, d: (_cond_row(i * tm), 5, 0, 0)),
                      pl.BlockSpec(memory_space=pl.ANY)],
            out_specs=pl.BlockSpec((tm, D_MODEL), lambda i, d: (i, 0)),
            scratch_shapes=[pltpu.VMEM((2, tm, D_MODEL), F32), pltpu.SemaphoreType.DMA((2,))],
        ),
        out_shape=jax.ShapeDtypeStruct((T, D_MODEL), F32),
        compiler_params=_cp(("arbitrary",)),
        name="moe_combine",
    )(dest, x, mod, ys)


def _lookup(table, idx):
    n = table.shape[0]
    hit = idx[:, None] == jnp.arange(n, dtype=jnp.int32)[None, :]
    return jnp.sum(jnp.where(hit, table[None, :], 0), axis=1)


def _moe_plan(rt, cnt):
    bucket = rt[0].astype(jnp.int32)
    rank = rt[1].astype(jnp.int32)
    counts = cnt[:N_BUCKETS, 0].astype(jnp.int32)
    tiles = (counts + MOE_TILE - 1) // MOE_TILE
    order = jnp.arange(N_BUCKETS, dtype=jnp.int32)
    tile_start = jnp.sum(jnp.where(order[None, :] < order[:, None], tiles[None, :], 0), axis=1)
    tile_end = tile_start + tiles
    n_used = tile_end[N_BUCKETS - 1]
    dest = _lookup(tile_start * MOE_TILE, bucket) + rank
    j = jnp.arange(MOE_TILES, dtype=jnp.int32)
    jc = jnp.minimum(j, n_used - 1)
    b = jnp.minimum(jnp.sum((jc[:, None] >= tile_end[None, :]).astype(jnp.int32), axis=1), N_BUCKETS - 1)
    nv = jnp.clip(_lookup(counts, b) - (j - _lookup(tile_start, b)) * MOE_TILE, 0, MOE_TILE)
    nv = jnp.where(j < n_used, nv, 0)
    n_pairs = len(PAIR_LO)
    ea = (b // n_pairs) * EXPERTS_PER_GROUP + _lookup(jnp.asarray(PAIR_LO, jnp.int32), b % n_pairs)
    eb = (b // n_pairs) * EXPERTS_PER_GROUP + _lookup(jnp.asarray(PAIR_HI, jnp.int32), b % n_pairs)
    return dest, ea, eb, nv


def _moe(x, g, mod, w_router, b_router, w_gate, w_up, w_down):
    xh, rt, cnt = _router(x, g, mod, w_router, b_router)
    dest, ea, eb, nv = _moe_plan(rt, cnt)
    ys = _experts(ea, eb, nv, _invert(dest), xh, w_gate.astype(BF16), w_up.astype(BF16), w_down.astype(BF16))
    return _combine(dest, x, mod, ys)


def _final_norm_kernel(x_ref, g_ref, o_ref):
    x = x_ref[...]
    o_ref[...] = x * lax.rsqrt(jnp.mean(x * x, axis=-1, keepdims=True) + NORM_EPS) * g_ref[...]


def _final_norm(x, g, row0, nrows):
    tm = 512
    blk0 = row0 // tm
    return pl.pallas_call(
        _final_norm_kernel,
        grid=(nrows // tm,),
        in_specs=[pl.BlockSpec((tm, D_MODEL), lambda i: (blk0 + i, 0)), pl.BlockSpec((1, D_MODEL), lambda i: (0, 0))],
        out_specs=pl.BlockSpec((tm, D_MODEL), lambda i: (i, 0)),
        out_shape=jax.ShapeDtypeStruct((nrows, D_MODEL), F32),
        compiler_params=_cp(("parallel",)),
        name="final_norm",
    )(x, g.reshape(1, D_MODEL))


def kernel(x_prompt, x_sample, cache_k_full, cache_v_full, cache_k_win, cache_v_win, c, c_ctx, w_mod, b_mod, norm_mix, norm_ffn, final_norm, pool_w, pool_scale, hy_w_in, hy_b_in, hy_conv_w, hy_conv_b, hy_f_w1, hy_f_b1, hy_f_w2, hy_f_b2, hy_f_freq, hy_f_w3, hy_decay, hy_skip, hy_w_out, hy_b_out, fa_w_qkv, fa_q_norm, fa_k_norm, fa_w_o, wa_w_qkv, wa_sink, wa_w_o, w_router, b_router, moe_w_gate, moe_w_up, moe_w_down):
    x = jnp.concatenate([x_prompt.reshape(TP, D_MODEL), x_sample.reshape(TS, D_MODEL)], axis=0)
    cond = jnp.concatenate([c_ctx[None, :], c, jnp.zeros((N_COND - 1 - DEC_BATCH, D_MODEL), F32)], axis=0)
    mods = _adaln(cond, w_mod, b_mod).reshape(DEPTH, N_COND, 6, 1, D_MODEL)
    rope = _rope_tables()
    ones_hd = jnp.ones((HEAD_DIM,), F32)
    new_kv = {}
    for layer in range(DEPTH):
        kind = layer % 4
        j = layer // 4
        mod = mods[layer]
        g_mix = norm_mix[layer].reshape(1, D_MODEL)
        if kind == 0:
            x = _pool_mixer(x, g_mix, mod, pool_w[j], pool_scale[j])
        elif kind == 1:
            x = _hyena_mixer(x, g_mix, mod, hy_w_in[j], hy_b_in[j], hy_conv_w[j], hy_conv_b[j], hy_f_w1[j],
                             hy_f_b1[j], hy_f_w2[j], hy_f_b2[j], hy_f_freq[j], hy_f_w3[j], hy_decay[j],
                             hy_skip[j], hy_w_out[j], hy_b_out[j])
        elif kind == 2:
            x, nk, nv = _attn_mixer(x, g_mix, mod, fa_w_qkv[j], fa_q_norm[j], fa_k_norm[j], True, None,
                                    fa_w_o[j], cache_k_full[:, j], cache_v_full[:, j], False, rope)
            new_kv.setdefault("kf", []).append(nk)
            new_kv.setdefault("vf", []).append(nv)
        else:
            x, nk, nv = _attn_mixer(x, g_mix, mod, wa_w_qkv[j], ones_hd, ones_hd, False, wa_sink[j],
                                    wa_w_o[j], cache_k_win[:, j], cache_v_win[:, j], True, rope)
            new_kv.setdefault("kw", []).append(nk)
            new_kv.setdefault("vw", []).append(nv)
        x = _moe(x, norm_ffn[layer].reshape(1, D_MODEL), mod, w_router, b_router,
                 moe_w_gate[layer], moe_w_up[layer], moe_w_down[layer])
    y_prompt = _final_norm(x, final_norm, 0, TP).reshape(BATCH, SEQ, D_MODEL)
    y_sample = _final_norm(x, final_norm, TP, TS).reshape(DEC_BATCH, DEC_SEQ, D_MODEL)
    return (y_prompt, y_sample, jnp.stack(new_kv["kf"], axis=1), jnp.stack(new_kv["vf"], axis=1),
            jnp.stack(new_kv["kw"], axis=1), jnp.stack(new_kv["vw"], axis=1))
```

```python
import functools
import math

import jax
import jax.numpy as jnp
import numpy as np
from jax import lax
from jax.experimental import pallas as pl
from jax.experimental.pallas import tpu as pltpu

D_MODEL = 2048
BATCH = 32
SEQ = 256
DEPTH = 4
DEC_BATCH = 4
DEC_SEQ = 4096
PAST_LEN = 512
GRID_W = 64
N_HEADS = 16
N_KV_HEADS = 4
HEAD_DIM = D_MODEL // N_HEADS
KV_GROUP = N_HEADS // N_KV_HEADS
KV_DIM = N_KV_HEADS * HEAD_DIM
QKV_DIM = (N_HEADS + 2 * N_KV_HEADS) * HEAD_DIM
ROPE_THETA = 10000.0
WINDOW = 128
POOL_WINDOWS = (2, 4, 8, 16)
POOL_GROUP = D_MODEL // len(POOL_WINDOWS)
HYENA_EMB_BANDS = 16
HYENA_FILTER_HIDDEN = 64
N_EXPERTS = 16
N_EXPERT_GROUPS = 4
EXPERTS_PER_GROUP = 4
D_EXPERT = 512
NORM_EPS = 1e-6
NEG_INF = -1e30

F32 = jnp.float32
BF16 = jnp.bfloat16

TP = BATCH * SEQ
TS = DEC_BATCH * DEC_SEQ
T = TP + TS
N_COND = 8
LANE = 128
MIB = 1024 * 1024

PAIR_LO = (0, 0, 0, 1, 1, 2)
PAIR_HI = (1, 2, 3, 2, 3, 3)
N_BUCKETS = N_EXPERT_GROUPS * len(PAIR_LO)
MOE_TILE = 256
MOE_TILES = T // MOE_TILE + N_BUCKETS
T_PAD = MOE_TILES * MOE_TILE
XH_W = D_MODEL + LANE


def _cp(sem, vmem_mb=48):
    return pltpu.CompilerParams(dimension_semantics=sem, vmem_limit_bytes=vmem_mb * MIB)


def _dot(a, b):
    return jnp.dot(a, b, preferred_element_type=F32)


def _dot3(a, b):
    ah = a.astype(BF16)
    al = (a - ah.astype(F32)).astype(BF16)
    bh = b.astype(BF16)
    bl = (b - bh.astype(F32)).astype(BF16)
    return _dot(ah, bh) + (_dot(al, bh) + _dot(ah, bl))


def _sigmoid(x):
    return 1.0 / (1.0 + jnp.exp(-x))


def _cond_row(r):
    return jnp.where(r < TP, 0, 1 + (r - TP) // DEC_SEQ)


def _mod_spec(tm, chunk, tn=D_MODEL, ncol=False):
    if ncol:
        return pl.BlockSpec((None, None, 1, tn), lambda i, j: (_cond_row(i * tm), chunk, 0, j))
    return pl.BlockSpec((None, None, 1, tn), lambda i, *_: (_cond_row(i * tm), chunk, 0, 0))


def _norm_mod(x, g, shift, scale):
    var = jnp.mean(x * x, axis=-1, keepdims=True)
    y = x * lax.rsqrt(var + NORM_EPS) * g
    return y * (1.0 + scale) + shift


def _adaln_kernel(c_ref, w_ref, b_ref, o_ref):
    c = c_ref[...]
    a = c * _sigmoid(c)
    o_ref[...] = _dot3(a, w_ref[...]) + b_ref[...]


def _adaln(cond, w_mod, b_mod):
    tn = 1024
    n = 6 * D_MODEL
    return pl.pallas_call(
        _adaln_kernel,
        grid=(DEPTH, n // tn),
        in_specs=[
            pl.BlockSpec((N_COND, D_MODEL), lambda l, j: (0, 0)),
            pl.BlockSpec((None, D_MODEL, tn), lambda l, j: (l, 0, j)),
            pl.BlockSpec((None, 1, tn), lambda l, j: (l, 0, j)),
        ],
        out_specs=pl.BlockSpec((None, N_COND, tn), lambda l, j: (l, 0, j)),
        out_shape=jax.ShapeDtypeStruct((DEPTH, N_COND, n), F32),
        compiler_params=_cp(("parallel", "parallel")),
        name="adaln",
    )(cond, w_mod, b_mod.reshape(DEPTH, 1, n))


def _nm_matmul_kernel(x_ref, g_ref, sh_ref, sc_ref, w_ref, b_ref, o_ref, h_scr):
    @pl.when(pl.program_id(1) == 0)
    def _():
        h_scr[...] = _norm_mod(x_ref[...], g_ref[...], sh_ref[...], sc_ref[...]).astype(BF16)

    o_ref[...] = (_dot(h_scr[...], w_ref[...]) + b_ref[...]).astype(o_ref.dtype)


def _nm_matmul(x, g, mod, w, b, out_dtype, name):
    tm, tn = 1024, 1024
    n = w.shape[1]
    return pl.pallas_call(
        _nm_matmul_kernel,
        grid=(T // tm, n // tn),
        in_specs=[
            pl.BlockSpec((tm, D_MODEL), lambda i, j: (i, 0)),
            pl.BlockSpec((1, D_MODEL), lambda i, j: (0, 0)),
            _mod_spec(tm, 0),
            _mod_spec(tm, 1),
            pl.BlockSpec((D_MODEL, tn), lambda i, j: (0, j)),
            pl.BlockSpec((1, tn), lambda i, j: (0, j)),
        ],
        out_specs=pl.BlockSpec((tm, tn), lambda i, j: (i, j)),
        out_shape=jax.ShapeDtypeStruct((T, n), out_dtype),
        scratch_shapes=[pltpu.VMEM((tm, D_MODEL), BF16)],
        compiler_params=_cp(("parallel", "arbitrary")),
        name=name,
    )(x, g, mod, mod, w, b)


RESID_TM = 1024


def _resid_matmul_kernel(ap_ref, as_ref, w_ref, b_ref, x_ref, gt_ref, o_ref):
    def emit(a_ref):
        o_ref[...] = x_ref[...] + gt_ref[...] * (_dot(a_ref[...], w_ref[...]) + b_ref[...])

    is_ctx = pl.program_id(0) < TP // RESID_TM
    pl.when(is_ctx)(lambda: emit(ap_ref))
    pl.when(jnp.logical_not(is_ctx))(lambda: emit(as_ref))


def _resid_matmul(a_ctx, a_lat, w, b, x, mod, name):
    tm, tn = RESID_TM, 1024
    k = a_ctx.shape[1]
    n_ctx = TP // tm
    return pl.pallas_call(
        _resid_matmul_kernel,
        grid=(T // tm, D_MODEL // tn),
        in_specs=[
            pl.BlockSpec((tm, k), lambda i, j: (jnp.minimum(i, n_ctx - 1), 0)),
            pl.BlockSpec((tm, k), lambda i, j: (jnp.maximum(i - n_ctx, 0), 0)),
            pl.BlockSpec((k, tn), lambda i, j: (0, j)),
            pl.BlockSpec((1, tn), lambda i, j: (0, j)),
            pl.BlockSpec((tm, tn), lambda i, j: (i, j)),
            _mod_spec(tm, 2, tn, ncol=True),
        ],
        out_specs=pl.BlockSpec((tm, tn), lambda i, j: (i, j)),
        out_shape=jax.ShapeDtypeStruct((T, D_MODEL), F32),
        compiler_params=_cp(("parallel", "parallel")),
        name=name,
    )(a_ctx, a_lat, w, b, x, mod)


POOL_TILE = 256
POOL_HALO = 8


def _seq_pos(r0):
    is_ctx = r0 < TP
    loc0 = jnp.where(is_ctx, r0 % SEQ, (r0 - TP) % DEC_SEQ)
    seq_len = jnp.where(is_ctx, SEQ, DEC_SEQ)
    return loc0, seq_len


def _pool_kernel(x_ref, xp_ref, xn_ref, g_ref, sh_ref, sc_ref, gt_ref, pw_ref, ps_ref, o_ref, hz_scr):
    tm, hl = POOL_TILE, POOL_HALO
    loc0, seq_len = _seq_pos(pl.program_id(0) * tm)
    has_prev = loc0 > 0
    has_next = loc0 + tm < seq_len
    g, sh, sc = g_ref[...], sh_ref[...], sc_ref[...]
    x = x_ref[...]
    h = _norm_mod(x, g, sh, sc)
    hz_scr[0:hl, :] = jnp.where(has_prev, _norm_mod(xp_ref[...], g, sh, sc), 0.0)
    hz_scr[hl:hl + tm, :] = h
    hz_scr[hl + tm:, :] = jnp.where(has_next, _norm_mod(xn_ref[...], g, sh, sc), 0.0)
    tl = loc0 + lax.broadcasted_iota(jnp.int32, (tm, 1), 0)
    outs = []
    for gi, w in enumerate(POOL_WINDOWS):
        cs = slice(gi * POOL_GROUP, (gi + 1) * POOL_GROUP)
        s = jnp.zeros((tm, POOL_GROUP), F32)
        for off in range(-(w // 2), w - w // 2):
            s = s + hz_scr[hl + off:hl + off + tm, cs]
        lo = jnp.maximum(tl - w // 2, 0)
        hi = jnp.minimum(tl + (w - w // 2), seq_len)
        d = s / (hi - lo).astype(F32) - h[:, cs]
        outs.append(_dot(d.astype(BF16), pw_ref[gi]))
    out = jnp.concatenate(outs, axis=1) * ps_ref[...]
    o_ref[...] = x + gt_ref[...] * out


def _pool_mixer(x, g, mod, pool_w, pool_scale):
    tm, hl = POOL_TILE, POOL_HALO
    r = tm // hl
    return pl.pallas_call(
        _pool_kernel,
        grid=(T // tm,),
        in_specs=[
            pl.BlockSpec((tm, D_MODEL), lambda i: (i, 0)),
            pl.BlockSpec((hl, D_MODEL), lambda i: (jnp.maximum(i * r - 1, 0), 0)),
            pl.BlockSpec((hl, D_MODEL), lambda i: (jnp.minimum((i + 1) * r, T // hl - 1), 0)),
            pl.BlockSpec((1, D_MODEL), lambda i: (0, 0)),
            _mod_spec(tm, 0),
            _mod_spec(tm, 1),
            _mod_spec(tm, 2),
            pl.BlockSpec((len(POOL_WINDOWS), POOL_GROUP, POOL_GROUP), lambda i: (0, 0, 0)),
            pl.BlockSpec((1, D_MODEL), lambda i: (0, 0)),
        ],
        out_specs=pl.BlockSpec((tm, D_MODEL), lambda i: (i, 0)),
        out_shape=jax.ShapeDtypeStruct((T, D_MODEL), F32),
        scratch_shapes=[pltpu.VMEM((tm + 2 * hl, D_MODEL), F32)],
        compiler_params=_cp(("parallel",)),
        name="pool_mixer",
    )(x, x, x, g, mod, mod, mod, pool_w.astype(BF16), pool_scale.reshape(1, D_MODEL))


CONV_TILE = 256
CONV_HALO = 16


def _conv3_kernel(u_ref, up_ref, un_ref, cw_ref, cb_ref, o_ref, scr):
    tm, hl = CONV_TILE, CONV_HALO
    loc0, seq_len = _seq_pos(pl.program_id(0) * tm)
    has_prev = loc0 > 0
    has_next = loc0 + tm < seq_len
    scr[0:hl, :] = jnp.where(has_prev, up_ref[...].astype(F32), 0.0)
    scr[hl:hl + tm, :] = u_ref[...].astype(F32)
    scr[hl + tm:, :] = jnp.where(has_next, un_ref[...].astype(F32), 0.0)
    out = (scr[hl - 1:hl - 1 + tm, :] * cw_ref[0:1, :] + scr[hl:hl + tm, :] * cw_ref[1:2, :]
           + scr[hl + 1:hl + 1 + tm, :] * cw_ref[2:3, :] + cb_ref[...])
    o_ref[...] = out.astype(o_ref.dtype)


def _conv3(u0, conv_w, conv_b):
    tm, hl, tc = CONV_TILE, CONV_HALO, D_MODEL
    r = tm // hl
    n = u0.shape[1]
    return pl.pallas_call(
        _conv3_kernel,
        grid=(T // tm, n // tc),
        in_specs=[
            pl.BlockSpec((tm, tc), lambda i, j: (i, j)),
            pl.BlockSpec((hl, tc), lambda i, j: (jnp.maximum(i * r - 1, 0), j)),
            pl.BlockSpec((hl, tc), lambda i, j: (jnp.minimum((i + 1) * r, T // hl - 1), j)),
            pl.BlockSpec((3, tc), lambda i, j: (0, j)),
            pl.BlockSpec((1, tc), lambda i, j: (0, j)),
        ],
        out_specs=pl.BlockSpec((tm, tc), lambda i, j: (i, j)),
        out_shape=jax.ShapeDtypeStruct((T, n), BF16),
        scratch_shapes=[pltpu.VMEM((tm + 2 * hl, tc), F32)],
        compiler_params=_cp(("parallel", "parallel")),
        name="hyena_conv3",
    )(u0, u0, u0, conv_w, conv_b.reshape(1, n))


FILT_TILE = 256


def _filter_kernel(emb_ref, w1_ref, b1_ref, w2_ref, b2_ref, fr_ref, w3_ref, dc_ref, fa_ref, fb_ref):
    emb = emb_ref[...]
    fr = fr_ref[...]
    a = jnp.sin(fr * (_dot3(emb, w1_ref[...]) + b1_ref[...]))
    a = jnp.sin(fr * (_dot3(a, w2_ref[...]) + b2_ref[...]))
    t = emb[:, 0:1]
    filt = _dot3(a, w3_ref[...]) * jnp.exp(-t * jnp.abs(dc_ref[...]))
    hf = filt[:, :D_MODEL]
    row = pl.program_id(0) * FILT_TILE + lax.broadcasted_iota(jnp.int32, (FILT_TILE, 1), 0)
    hb = jnp.where(row == 0, 0.0, filt[:, D_MODEL:])
    fa_ref[...] = (hf + hb).astype(BF16)
    fb_ref[...] = (hb - hf).astype(BF16)


def _pad2(a, rows, cols):
    return jnp.pad(a, ((0, rows - a.shape[0]), (0, cols - a.shape[1])))


def _hyena_filters(L, f_w1, f_b1, f_w2, f_b2, f_freq, f_w3, decay):
    t = jnp.arange(L, dtype=F32) / L
    bands = jnp.linspace(1e-4, HYENA_EMB_BANDS - 1, HYENA_EMB_BANDS, dtype=F32)
    ang = (2 * math.pi) * t[:, None] * bands[None, :]
    emb = _pad2(jnp.concatenate([t[:, None], jnp.cos(ang), -jnp.sin(ang)], axis=-1), L, LANE)
    nf = 2 * D_MODEL
    tl = FILT_TILE
    small = lambda: pl.BlockSpec((LANE, LANE), lambda i, n: (0, 0))
    vec = lambda: pl.BlockSpec((1, LANE), lambda i, n: (0, 0))
    out = pl.BlockSpec((None, tl, D_MODEL), lambda i, n: (n, i, 0))
    return pl.pallas_call(
        _filter_kernel,
        grid=(L // tl, 2),
        in_specs=[
            pl.BlockSpec((tl, LANE), lambda i, n: (i, 0)),
            small(), vec(), small(), vec(), vec(),
            pl.BlockSpec((LANE, nf), lambda i, n: (0, n)),
            pl.BlockSpec((1, nf), lambda i, n: (0, n)),
        ],
        out_specs=[out, out],
        out_shape=[jax.ShapeDtypeStruct((2, L, D_MODEL), BF16)] * 2,
        compiler_params=_cp(("parallel", "parallel")),
        name="hyena_filters",
    )(emb, _pad2(f_w1, LANE, LANE), _pad2(f_b1[None], 1, LANE), _pad2(f_w2, LANE, LANE),
      _pad2(f_b2[None], 1, LANE), _pad2(f_freq[None], 1, LANE), _pad2(f_w3, LANE, 2 * nf), decay[None])


def _dft_mats(L):
    r = int(math.isqrt(L))
    k2 = 2 * jnp.arange(L, dtype=jnp.int32)[:, None] + 1
    n1 = r * jnp.arange(L // r, dtype=jnp.int32)[None, :]
    n2 = jnp.arange(r, dtype=jnp.int32)[None, :]
    sc = math.pi / (2 * L)
    aa = ((k2 * n1) % (4 * L)).astype(F32) * sc
    ab = ((k2 * n2) % (4 * L)).astype(F32) * sc
    ca, sa, cb, sb = jnp.cos(aa)[:, :, None], jnp.sin(aa)[:, :, None], jnp.cos(ab)[:, None, :], jnp.sin(ab)[:, None, :]
    c = (ca * cb - sa * sb).reshape(L, L)
    s = (sa * cb + ca * sb).reshape(L, L)
    return c.astype(BF16), s.astype(BF16), c.T.astype(BF16), s.T.astype(BF16)


def _dft_tiles(L):
    return min(512, L), 512


def _dft_filter_kernel(c_ref, s_ref, a_ref, b_ref, gr_ref, gi_ref):
    gr_ref[...] = _dot(c_ref[...], a_ref[...])
    gi_ref[...] = _dot(s_ref[...], b_ref[...])


def _dft_filter(cm, sm, fa, fb, L):
    tf, tn = _dft_tiles(L)
    mat = lambda: pl.BlockSpec((tf, L), lambda k, c, n: (k, 0))
    rhs = lambda: pl.BlockSpec((None, L, tn), lambda k, c, n: (n, 0, c))
    out = pl.BlockSpec((None, tf, tn), lambda k, c, n: (n, k, c))
    return pl.pallas_call(
        _dft_filter_kernel,
        grid=(L // tf, D_MODEL // tn, 2),
        in_specs=[mat(), mat(), rhs(), rhs()],
        out_specs=[out, out],
        out_shape=[jax.ShapeDtypeStruct((2, L, D_MODEL), F32)] * 2,
        compiler_params=_cp(("parallel", "parallel", "parallel")),
        name="hyena_filter_dft",
    )(cm, sm, fa, fb)


def _dft_fwd_kernel(c_ref, s_ref, z_ref, gr_ref, gi_ref, yr_ref, yi_ref):
    z = z_ref[...]
    zc = _dot(c_ref[...], z)
    zs = _dot(s_ref[...], z)
    gr, gi = gr_ref[...], gi_ref[...]
    yr_ref[...] = (gr * zc + gi * zs).astype(BF16)
    yi_ref[...] = (gi * zc - gr * zs).astype(BF16)


def _dft_fwd(cm, sm, z, z_rowblk, z_colblk, gr, gi, order, nb, L):
    tf, tn = _dft_tiles(L)
    mat = lambda: pl.BlockSpec((tf, L), lambda k, c, b: (k, 0))
    gsp = lambda: pl.BlockSpec((None, tf, tn), lambda k, c, b: (order, k, c))
    out = pl.BlockSpec((None, tf, tn), lambda k, c, b: (b, k, c))
    return pl.pallas_call(
        _dft_fwd_kernel,
        grid=(L // tf, D_MODEL // tn, nb),
        in_specs=[mat(), mat(),
                  pl.BlockSpec((L, tn), lambda k, c, b: (z_rowblk + b, z_colblk + c)),
                  gsp(), gsp()],
        out_specs=[out, out],
        out_shape=[jax.ShapeDtypeStruct((nb, L, D_MODEL), BF16)] * 2,
        compiler_params=_cp(("parallel", "parallel", "parallel")),
        name="hyena_dft_fwd",
    )(cm, sm, z, gr, gi)


def _dft_inv_kernel(ct_ref, st_ref, yr_ref, yi_ref, z_ref, gt_ref, sk_ref, o_ref, *, inv_len):
    y = (_dot(ct_ref[...], yr_ref[...]) - _dot(st_ref[...], yi_ref[...])) * inv_len
    o_ref[...] = (gt_ref[...].astype(F32) * (y + sk_ref[...] * z_ref[...].astype(F32))).astype(BF16)


def _dft_inv(ctm, stm, yr, yi, z, z_rowblk, z_colblk, gate, g_rowblk, g_colblk, skip, nb, L):
    tt, tn = _dft_tiles(L)
    rpb = L // tt
    mat = lambda: pl.BlockSpec((tt, L), lambda t, c, b: (t, 0))
    spec = lambda: pl.BlockSpec((None, L, tn), lambda t, c, b: (b, 0, c))
    return pl.pallas_call(
        functools.partial(_dft_inv_kernel, inv_len=1.0 / L),
        grid=(rpb, D_MODEL // tn, nb),
        in_specs=[mat(), mat(), spec(), spec(),
                  pl.BlockSpec((tt, tn), lambda t, c, b: (z_rowblk + b * rpb + t, z_colblk + c)),
                  pl.BlockSpec((tt, tn), lambda t, c, b: (g_rowblk + b * rpb + t, g_colblk + c)),
                  pl.BlockSpec((1, tn), lambda t, c, b: (0, c))],
        out_specs=pl.BlockSpec((tt, tn), lambda t, c, b: (b * rpb + t, c)),
        out_shape=jax.ShapeDtypeStruct((nb * L, D_MODEL), BF16),
        compiler_params=_cp(("parallel", "parallel", "parallel")),
        name="hyena_dft_inv",
    )(ctm, stm, yr, yi, z, gate, skip)


def _hyena_stream(u, row0, nb, L, fparams, skip):
    cm, sm, ctm, stm = _dft_mats(L)
    fa, fb = _hyena_filters(L, *fparams)
    gr, gi = _dft_filter(cm, sm, fa, fb, L)
    tt, tn = _dft_tiles(L)
    ncb = D_MODEL // tn
    yr, yi = _dft_fwd(cm, sm, u, row0 // L, 0, gr, gi, 0, nb, L)
    z1 = _dft_inv(ctm, stm, yr, yi, u, row0 // tt, 0, u, row0 // tt, ncb, skip[0:1], nb, L)
    yr, yi = _dft_fwd(cm, sm, z1, 0, 0, gr, gi, 1, nb, L)
    return _dft_inv(ctm, stm, yr, yi, z1, 0, 0, u, row0 // tt, 2 * ncb, skip[1:2], nb, L)


def _hyena_mixer(x, g, mod, w_in, b_in, conv_w, conv_b, f_w1, f_b1, f_w2, f_b2, f_freq, f_w3, decay, skip,
                 w_out, b_out):
    u0 = _nm_matmul(x, g, mod, w_in.astype(BF16), b_in.reshape(1, -1), BF16, "hyena_in_proj")
    u = _conv3(u0, conv_w, conv_b)
    fparams = (f_w1, f_b1, f_w2, f_b2, f_freq, f_w3, decay)
    zp = _hyena_stream(u, 0, BATCH, SEQ, fparams, skip)
    zs = _hyena_stream(u, TP, DEC_BATCH, DEC_SEQ, fparams, skip)
    return _resid_matmul(zp, zs, w_out.astype(BF16), b_out.reshape(1, -1), x, mod, "hyena_out_proj")


PREP_TILE = 256


def _rope_tables():
    pos = jnp.arange(DEC_SEQ, dtype=jnp.int32)
    row = (pos // GRID_W).astype(F32)
    col = (pos % GRID_W).astype(F32)
    axis_dim = HEAD_DIM // 2
    inv_freq = ROPE_THETA ** (-jnp.arange(0, axis_dim, 2, dtype=F32) / axis_dim)
    ar = row[:, None] * inv_freq[None, :]
    ac = col[:, None] * inv_freq[None, :]
    cos = jnp.concatenate([jnp.cos(ar), jnp.cos(ar), jnp.cos(ac), jnp.cos(ac)], axis=-1)
    sin = jnp.concatenate([-jnp.sin(ar), jnp.sin(ar), -jnp.sin(ac), jnp.sin(ac)], axis=-1)
    return cos, sin


def _prep_kernel(*refs, use_norm, use_rope, emit_kv):
    it = iter(refs)
    qkv_ref, qn_ref, kn_ref = next(it), next(it), next(it)
    cos_ref = sin_ref = None
    if use_rope:
        cos_ref, sin_ref = next(it), next(it)
    q_ref, k_ref, v_ref = next(it), next(it), next(it)
    nk_ref = nv_ref = None
    if emit_kv:
        nk_ref, nv_ref = next(it), next(it)
    quarter = HEAD_DIM // 4
    if use_rope:
        cos, sin = cos_ref[...], sin_ref[...]
        lane = lax.broadcasted_iota(jnp.int32, (PREP_TILE, HEAD_DIM), 1)
        first = (lane % (2 * quarter)) < quarter

    def head(xh, gn):
        if use_norm:
            xh = xh * lax.rsqrt(jnp.mean(xh * xh, axis=-1, keepdims=True) + NORM_EPS) * gn
        return xh

    def rope(xh):
        if not use_rope:
            return xh
        partner = jnp.where(first, pltpu.roll(xh, HEAD_DIM - quarter, 1), pltpu.roll(xh, quarter, 1))
        return xh * cos + partner * sin

    qn, kn = qn_ref[...], kn_ref[...]
    scale = HEAD_DIM ** -0.5 * LOG2E
    for h in range(N_HEADS):
        hs = slice(h * HEAD_DIM, (h + 1) * HEAD_DIM)
        q_ref[:, hs] = (rope(head(qkv_ref[:, hs], qn)) * scale).astype(BF16)
    for h in range(N_KV_HEADS):
        hs = slice(h * HEAD_DIM, (h + 1) * HEAD_DIM)
        ks = slice(D_MODEL + h * HEAD_DIM, D_MODEL + (h + 1) * HEAD_DIM)
        vs = slice(D_MODEL + KV_DIM + h * HEAD_DIM, D_MODEL + KV_DIM + (h + 1) * HEAD_DIM)
        kh = head(qkv_ref[:, ks], kn)
        vh = qkv_ref[:, vs]
        if emit_kv:
            nk_ref[:, hs] = kh
            nv_ref[:, hs] = vh
        k_ref[:, hs] = rope(kh).astype(BF16)
        v_ref[:, hs] = vh.astype(BF16)


def _attn_prep(qkv, row0, nrows, q_norm, k_norm, use_norm, rope, emit_kv):
    tm = PREP_TILE
    blk0 = row0 // tm
    in_specs = [pl.BlockSpec((tm, QKV_DIM), lambda i: (blk0 + i, 0)),
                pl.BlockSpec((1, HEAD_DIM), lambda i: (0, 0)),
                pl.BlockSpec((1, HEAD_DIM), lambda i: (0, 0))]
    args = [qkv, q_norm.reshape(1, HEAD_DIM), k_norm.reshape(1, HEAD_DIM)]
    if rope is not None:
        tab = lambda: pl.BlockSpec((tm, HEAD_DIM), lambda i: (i % (DEC_SEQ // tm), 0))
        in_specs += [tab(), tab()]
        args += list(rope)
    row = lambda w: pl.BlockSpec((tm, w), lambda i: (i, 0))
    out_specs = [row(D_MODEL), row(KV_DIM), row(KV_DIM)]
    out_shape = [jax.ShapeDtypeStruct((nrows, D_MODEL), BF16), jax.ShapeDtypeStruct((nrows, KV_DIM), BF16),
                 jax.ShapeDtypeStruct((nrows, KV_DIM), BF16)]
    if emit_kv:
        out_specs += [row(KV_DIM), row(KV_DIM)]
        out_shape += [jax.ShapeDtypeStruct((nrows, KV_DIM), F32)] * 2
    return pl.pallas_call(
        functools.partial(_prep_kernel, use_norm=use_norm, use_rope=rope is not None, emit_kv=emit_kv),
        grid=(nrows // tm,),
        in_specs=in_specs, out_specs=out_specs, out_shape=out_shape,
        compiler_params=_cp(("parallel",)),
        name="attn_prep",
    )(*args)


LOG2E = math.log2(math.e)
ATTN_TQ = 256


def _attn_kernel(*refs, tq, seq_len, n_ctx, windowed, has_sink):
    it = iter(refs)
    q_ref, k_ref, v_ref = next(it), next(it), next(it)
    sink_ref = next(it) if has_sink else None
    o_ref = next(it)
    if windowed:
        i = pl.program_id(2)
        span = tq + 2 * WINDOW
        start = pl.multiple_of(jnp.clip(i * tq - WINDOW, 0, seq_len - span), WINDOW)
        qpos = i * tq + lax.broadcasted_iota(jnp.int32, (tq, 1), 0)
        kpos = start + lax.broadcasted_iota(jnp.int32, (1, span), 1)
        segs = [(pl.ds(start, span), jnp.abs(kpos - qpos) <= WINDOW), (pl.ds(seq_len, n_ctx), None)]
    else:
        segs = [(slice(None), None)]
    for h in range(KV_GROUP):
        hs = slice(h * HEAD_DIM, (h + 1) * HEAD_DIM)
        qh = q_ref[:, hs]
        scores = []
        m = None
        for rows, mask in segs:
            s = lax.dot_general(qh, k_ref[rows, :], (((1,), (1,)), ((), ())), preferred_element_type=F32)
            if mask is not None:
                s = jnp.where(mask, s, NEG_INF)
            scores.append(s)
            ms = jnp.max(s, axis=-1, keepdims=True)
            m = ms if m is None else jnp.maximum(m, ms)
        if has_sink:
            sk = sink_ref[pl.program_id(1) * KV_GROUP + h]
            m = jnp.maximum(m, sk)
        l = jnp.exp2(sk - m) if has_sink else jnp.zeros_like(m)
        acc = jnp.zeros((tq, HEAD_DIM), F32)
        for (rows, _), s in zip(segs, scores):
            p = jnp.exp2(s - m)
            l = l + jnp.sum(p, axis=-1, keepdims=True)
            acc = acc + _dot(p.astype(BF16), v_ref[rows, :])
        o_ref[:, hs] = (acc / l).astype(BF16)


def _attention(q, k, v, sink, nb, L, n_ctx, windowed):
    tq = min(ATTN_TQ, L)
    nq = L // tq
    nk = L + n_ctx
    kv = lambda: pl.BlockSpec((None, nk, HEAD_DIM), lambda b, g, i: (b, 0, g))
    in_specs = [pl.BlockSpec((tq, KV_GROUP * HEAD_DIM), lambda b, g, i: (b * nq + i, g)), kv(), kv()]
    args = [q, k, v]
    if sink is not None:
        in_specs.append(pl.BlockSpec(memory_space=pltpu.SMEM))
        args.append(sink.astype(F32) * LOG2E)
    return pl.pallas_call(
        functools.partial(_attn_kernel, tq=tq, seq_len=L, n_ctx=n_ctx, windowed=windowed,
                          has_sink=sink is not None),
        grid=(nb, N_KV_HEADS, nq),
        in_specs=in_specs,
        out_specs=pl.BlockSpec((tq, KV_GROUP * HEAD_DIM), lambda b, g, i: (b * nq + i, g)),
        out_shape=jax.ShapeDtypeStruct((nb * L, D_MODEL), BF16),
        compiler_params=_cp(("parallel", "parallel", "parallel"), 56),
        name="attention",
    )(*args)


def _attn_mixer(x, g, mod, w_qkv, q_norm, k_norm, use_norm, sink, w_o, cache_k, cache_v, windowed, rope):
    zero_b = jnp.zeros((1, QKV_DIM), F32)
    qkv = _nm_matmul(x, g, mod, w_qkv.astype(BF16), zero_b, F32, "qkv_proj")
    qp, kp, vp, new_k, new_v = _attn_prep(qkv, 0, TP, q_norm, k_norm, use_norm, None, True)
    qs, ks, vs = _attn_prep(qkv, TP, TS, q_norm, k_norm, use_norm, rope, False)
    op = _attention(qp, kp.reshape(BATCH, SEQ, KV_DIM), vp.reshape(BATCH, SEQ, KV_DIM), sink, BATCH, SEQ, 0, False)
    kc = cache_k.reshape(DEC_BATCH, PAST_LEN, KV_DIM).astype(BF16)
    vc = cache_v.reshape(DEC_BATCH, PAST_LEN, KV_DIM).astype(BF16)
    k_all = jnp.concatenate([ks.reshape(DEC_BATCH, DEC_SEQ, KV_DIM), kc], axis=1)
    v_all = jnp.concatenate([vs.reshape(DEC_BATCH, DEC_SEQ, KV_DIM), vc], axis=1)
    osm = _attention(qs, k_all, v_all, sink, DEC_BATCH, DEC_SEQ, PAST_LEN, windowed)
    x = _resid_matmul(op, osm, w_o.astype(BF16), jnp.zeros((1, D_MODEL), F32), x, mod, "attn_out_proj")
    shape = (BATCH, SEQ, N_KV_HEADS, HEAD_DIM)
    return x, new_k.reshape(shape), new_v.reshape(shape)


ROUTE_TILE = 512
ROUTE_ROWS = 32


def _router_kernel(x_ref, g_ref, sh_ref, sc_ref, wr_ref, br_ref, xh_ref, rt_ref, cnt_ref, carry):
    tm = ROUTE_TILE
    i = pl.program_id(0)

    @pl.when(i == 0)
    def _():
        carry[...] = jnp.zeros_like(carry)

    h = _norm_mod(x_ref[...], g_ref[...], sh_ref[...], sc_ref[...])
    xh_ref[:, :D_MODEL] = h
    logits = _dot(h.astype(BF16), wr_ref[...])
    s = _sigmoid(logits.T[:N_EXPERTS, :])
    sb = s + br_ref[...]
    u = [s[e:e + 1, :] for e in range(N_EXPERTS)]
    v = [sb[e:e + 1, :] for e in range(N_EXPERTS)]

    gscore = []
    for gq in range(N_EXPERT_GROUPS):
        m = v[4 * gq:4 * gq + 4]
        best = m[PAIR_LO[0]] + m[PAIR_HI[0]]
        for a, b in zip(PAIR_LO[1:], PAIR_HI[1:]):
            best = jnp.maximum(best, m[a] + m[b])
        gscore.append(best)
    gidx = jnp.zeros((1, tm), jnp.int32)
    gbest = gscore[0]
    for gq in range(1, N_EXPERT_GROUPS):
        upd = gscore[gq] > gbest
        gidx = jnp.where(upd, gq, gidx)
        gbest = jnp.where(upd, gscore[gq], gbest)

    def pick(rows, j):
        out = rows[j]
        for gq in range(1, N_EXPERT_GROUPS):
            out = jnp.where(gidx == gq, rows[4 * gq + j], out)
        return out

    vin = [pick(v, j) for j in range(EXPERTS_PER_GROUP)]
    uin = [pick(u, j) for j in range(EXPERTS_PER_GROUP)]
    i1 = jnp.zeros((1, tm), jnp.int32)
    m1 = vin[0]
    for j in range(1, EXPERTS_PER_GROUP):
        upd = vin[j] > m1
        i1 = jnp.where(upd, j, i1)
        m1 = jnp.where(upd, vin[j], m1)
    i2 = jnp.full((1, tm), -1, jnp.int32)
    m2 = jnp.full((1, tm), -jnp.inf, F32)
    for j in range(EXPERTS_PER_GROUP):
        upd = (i1 != j) & (vin[j] > m2)
        i2 = jnp.where(upd, j, i2)
        m2 = jnp.where(upd, vin[j], m2)

    def sel(rows, idx):
        out = rows[0]
        for j in range(1, EXPERTS_PER_GROUP):
            out = jnp.where(idx == j, rows[j], out)
        return out

    w1, w2 = sel(uin, i1), sel(uin, i2)
    wsum = w1 + w2
    w1, w2 = w1 / wsum, w2 / wsum
    first_lo = i1 < i2
    lo = jnp.where(first_lo, i1, i2)
    hi = jnp.where(first_lo, i2, i1)
    w_lo = jnp.where(first_lo, w1, w2)
    w_hi = jnp.where(first_lo, w2, w1)
    pair = jnp.where(lo == 0, hi - 1, jnp.where(lo == 1, hi + 1, 5))
    bucket = gidx * len(PAIR_LO) + pair

    onehot = (lax.broadcasted_iota(jnp.int32, (ROUTE_ROWS, tm), 0) == bucket)
    tri = (lax.broadcasted_iota(jnp.int32, (tm, tm), 0) <= lax.broadcasted_iota(jnp.int32, (tm, tm), 1))
    cum = _dot(jnp.where(onehot, 1.0, 0.0).astype(BF16), jnp.where(tri, 1.0, 0.0).astype(BF16))
    rank = jnp.sum(jnp.where(onehot, cum - 1.0 + carry[...], 0.0), axis=0, keepdims=True)
    carry[...] = carry[...] + cum[:, tm - 1:tm]
    cnt_ref[...] = jnp.broadcast_to(carry[...], (ROUTE_ROWS, LANE))

    rt_ref[...] = jnp.zeros_like(rt_ref)
    rt_ref[0:1, :] = bucket.astype(F32)
    rt_ref[1:2, :] = rank
    wt = jnp.concatenate([w_lo, w_hi, jnp.zeros((LANE - 2, tm), F32)], axis=0)
    xh_ref[:, D_MODEL:] = wt.T


def _router(x, g, mod, w_router, b_router):
    tm = ROUTE_TILE
    wr = _pad2(w_router, D_MODEL, LANE).astype(BF16)
    return pl.pallas_call(
        _router_kernel,
        grid=(T // tm,),
        in_specs=[
            pl.BlockSpec((tm, D_MODEL), lambda i: (i, 0)),
            pl.BlockSpec((1, D_MODEL), lambda i: (0, 0)),
            _mod_spec(tm, 3),
            _mod_spec(tm, 4),
            pl.BlockSpec((D_MODEL, LANE), lambda i: (0, 0)),
            pl.BlockSpec((N_EXPERTS, 1), lambda i: (0, 0)),
        ],
        out_specs=[
            pl.BlockSpec((tm, XH_W), lambda i: (i, 0)),
            pl.BlockSpec((8, tm), lambda i: (0, i)),
            pl.BlockSpec((ROUTE_ROWS, LANE), lambda i: (0, 0)),
        ],
        out_shape=[
            jax.ShapeDtypeStruct((T, XH_W), F32),
            jax.ShapeDtypeStruct((8, T), F32),
            jax.ShapeDtypeStruct((ROUTE_ROWS, LANE), F32),
        ],
        scratch_shapes=[pltpu.VMEM((ROUTE_ROWS, 1), F32)],
        compiler_params=_cp(("arbitrary",)),
        name="moe_router",
    )(x, g, mod, mod, wr, b_router.reshape(N_EXPERTS, 1))


DISPATCH_TILE = 256


DMA_UNROLL = 8


def _invert_kernel(dest_ref, src_ref):
    def clear(s, c):
        src_ref[s] = 0
        return c

    def put(t, c):
        src_ref[dest_ref[t]] = t
        return c

    lax.fori_loop(0, T_PAD, clear, 0, unroll=DMA_UNROLL)
    lax.fori_loop(0, T, put, 0, unroll=DMA_UNROLL)


def _invert(dest):
    return pl.pallas_call(
        _invert_kernel,
        in_specs=[pl.BlockSpec(memory_space=pltpu.SMEM)],
        out_specs=pl.BlockSpec(memory_space=pltpu.SMEM),
        out_shape=jax.ShapeDtypeStruct((T_PAD,), jnp.int32),
        name="moe_invert",
    )(dest)


def _gather_rows(idx_ref, base, src_hbm, buf, sem, tm):
    def start(r, c):
        pltpu.make_async_copy(src_hbm.at[pl.ds(idx_ref[base + r], 1)], buf.at[pl.ds(r, 1)], sem).start()
        return c

    lax.fori_loop(0, tm, start, 0, unroll=DMA_UNROLL)


def _wait_rows(src_hbm, buf, sem, tm):
    pltpu.make_async_copy(src_hbm.at[pl.ds(0, tm)], buf, sem).wait()


def _expert_kernel(ea_ref, eb_ref, nv_ref, src_ref, xh_hbm, ga_ref, ua_ref, da_ref, gb_ref, ub_ref, db_ref, y_ref,
                   xbuf0, xbuf1, sems):
    tm = MOE_TILE
    j = pl.program_id(0)
    nv = nv_ref[j]

    @pl.when(j == 0)
    def _():
        _gather_rows(src_ref, 0, xh_hbm, xbuf0, sems.at[0], tm)

    def run(cur, cur_sem, nxt, nxt_sem):
        @pl.when(jnp.logical_or(j == 0, nv_ref[jnp.maximum(j - 1, 0)] > 0))
        def _():
            _wait_rows(xh_hbm, cur, cur_sem, tm)

        @pl.when(nv > 0)
        def _():
            valid = lax.broadcasted_iota(jnp.int32, (tm, 1), 0) < nv
            x = jnp.where(valid, cur[:, :D_MODEL], 0.0).astype(BF16)
            wts = jnp.where(valid, cur[:, D_MODEL:], 0.0)
            for r in range(tm):
                pltpu.make_async_copy(xh_hbm.at[pl.ds(src_ref[(j + 1) * tm + r], 1)], nxt.at[pl.ds(r, 1)],
                                      nxt_sem).start()

            def ffn(g_ref, u_ref, d_ref, w):
                a = _dot(x, g_ref[...])
                h = a * _sigmoid(a) * _dot(x, u_ref[...]) * w
                return _dot(h.astype(BF16), d_ref[...])

            y_ref[...] = ffn(ga_ref, ua_ref, da_ref, wts[:, 0:1]) + ffn(gb_ref, ub_ref, db_ref, wts[:, 1:2])

    even = j % 2 == 0
    pl.when(even)(lambda: run(xbuf0, sems.at[0], xbuf1, sems.at[1]))
    pl.when(jnp.logical_not(even))(lambda: run(xbuf1, sems.at[1], xbuf0, sems.at[0]))

    @pl.when(nv == 0)
    def _():
        y_ref[...] = jnp.zeros_like(y_ref)


def _experts(layer, tile_ea, tile_eb, tile_nv, src, xh, w_gate, w_up, w_down):
    tm = MOE_TILE
    up = lambda sel: pl.BlockSpec((None, None, D_MODEL, D_EXPERT),
                                  lambda j, ea, eb, nv, sr: (layer, (ea, eb)[sel][j], 0, 0))
    down = lambda sel: pl.BlockSpec((None, None, D_EXPERT, D_MODEL),
                                    lambda j, ea, eb, nv, sr: (layer, (ea, eb)[sel][j], 0, 0))
    return pl.pallas_call(
        _expert_kernel,
        grid_spec=pltpu.PrefetchScalarGridSpec(
            num_scalar_prefetch=4,
            grid=(MOE_TILES,),
            in_specs=[pl.BlockSpec(memory_space=pl.ANY), up(0), up(0), down(0), up(1), up(1), down(1)],
            out_specs=pl.BlockSpec((tm, D_MODEL), lambda j, ea, eb, nv, sr: (j, 0)),
            scratch_shapes=[pltpu.VMEM((tm, XH_W), F32), pltpu.VMEM((tm, XH_W), F32),
                            pltpu.SemaphoreType.DMA((2,))],
        ),
        out_shape=jax.ShapeDtypeStruct((T_PAD, D_MODEL), F32),
        compiler_params=_cp(("arbitrary",), 56),
        name="moe_experts",
    )(tile_ea, tile_eb, tile_nv, src, xh, w_gate, w_up, w_down, w_gate, w_up, w_down)


def _combine_kernel(dest_ref, x_ref, gt_ref, ys_hbm, o_ref, buf, sems):
    tm = DISPATCH_TILE
    i = pl.program_id(0)
    slot = i % 2

    @pl.when(i == 0)
    def _():
        _gather_rows(dest_ref, 0, ys_hbm, buf.at[0], sems.at[0], tm)

    @pl.when(i + 1 < pl.num_programs(0))
    def _():
        _gather_rows(dest_ref, (i + 1) * tm, ys_hbm, buf.at[1 - slot], sems.at[1 - slot], tm)

    _wait_rows(ys_hbm, buf.at[slot], sems.at[slot], tm)
    o_ref[...] = x_ref[...] + gt_ref[...] * buf[slot]


def _combine(dest, x, mod, ys):
    tm = DISPATCH_TILE
    return pl.pallas_call(
        _combine_kernel,
        grid_spec=pltpu.PrefetchScalarGridSpec(
            num_scalar_prefetch=1,
            grid=(T // tm,),
            in_specs=[pl.BlockSpec((tm, D_MODEL), lambda i, d: (i, 0)),
                      pl.BlockSpec((None, None, 1, D_MODEL), lambda i, d: (_cond_row(i * tm), 5, 0, 0)),
                      pl.BlockSpec(memory_space=pl.ANY)],
            out_specs=pl.BlockSpec((tm, D_MODEL), lambda i, d: (i, 0)),
            scratch_shapes=[pltpu.VMEM((2, tm, D_MODEL), F32), pltpu.SemaphoreType.DMA((2,))],
        ),
        out_shape=jax.ShapeDtypeStruct((T, D_MODEL), F32),
        compiler_params=_cp(("arbitrary",)),
        name="moe_combine",
    )(dest, x, mod, ys)


def _lookup(table, idx):
    n = table.shape[0]
    hit = idx[:, None] == jnp.arange(n, dtype=jnp.int32)[None, :]
    return jnp.sum(jnp.where(hit, table[None, :], 0), axis=1)


def _moe_plan(rt, cnt):
    bucket = rt[0].astype(jnp.int32)
    rank = rt[1].astype(jnp.int32)
    counts = cnt[:N_BUCKETS, 0].astype(jnp.int32)
    tiles = (counts + MOE_TILE - 1) // MOE_TILE
    order = jnp.arange(N_BUCKETS, dtype=jnp.int32)
    tile_start = jnp.sum(jnp.where(order[None, :] < order[:, None], tiles[None, :], 0), axis=1)
    tile_end = tile_start + tiles
    n_used = tile_end[N_BUCKETS - 1]
    dest = _lookup(tile_start * MOE_TILE, bucket) + rank
    j = jnp.arange(MOE_TILES, dtype=jnp.int32)
    jc = jnp.minimum(j, n_used - 1)
    b = jnp.minimum(jnp.sum((jc[:, None] >= tile_end[None, :]).astype(jnp.int32), axis=1), N_BUCKETS - 1)
    nv = jnp.clip(_lookup(counts, b) - (j - _lookup(tile_start, b)) * MOE_TILE, 0, MOE_TILE)
    nv = jnp.where(j < n_used, nv, 0)
    n_pairs = len(PAIR_LO)
    ea = (b // n_pairs) * EXPERTS_PER_GROUP + _lookup(jnp.asarray(PAIR_LO, jnp.int32), b % n_pairs)
    eb = (b // n_pairs) * EXPERTS_PER_GROUP + _lookup(jnp.asarray(PAIR_HI, jnp.int32), b % n_pairs)
    return dest, ea, eb, nv


def _moe(layer, x, g, mod, w_router, b_router, w_gate, w_up, w_down):
    xh, rt, cnt = _router(x, g, mod, w_router, b_router)
    dest, ea, eb, nv = _moe_plan(rt, cnt)
    ys = _experts(layer, ea, eb, nv, _invert(dest), xh, w_gate, w_up, w_down)
    return _combine(dest, x, mod, ys)


def _final_norm_kernel(x_ref, g_ref, o_ref):
    x = x_ref[...]
    o_ref[...] = x * lax.rsqrt(jnp.mean(x * x, axis=-1, keepdims=True) + NORM_EPS) * g_ref[...]


def _final_norm(x, g, row0, nrows):
    tm = 512
    blk0 = row0 // tm
    return pl.pallas_call(
        _final_norm_kernel,
        grid=(nrows // tm,),
        in_specs=[pl.BlockSpec((tm, D_MODEL), lambda i: (blk0 + i, 0)), pl.BlockSpec((1, D_MODEL), lambda i: (0, 0))],
        out_specs=pl.BlockSpec((tm, D_MODEL), lambda i: (i, 0)),
        out_shape=jax.ShapeDtypeStruct((nrows, D_MODEL), F32),
        compiler_params=_cp(("parallel",)),
        name="final_norm",
    )(x, g.reshape(1, D_MODEL))


def kernel(x_prompt, x_sample, cache_k_full, cache_v_full, cache_k_win, cache_v_win, c, c_ctx, w_mod, b_mod, norm_mix, norm_ffn, final_norm, pool_w, pool_scale, hy_w_in, hy_b_in, hy_conv_w, hy_conv_b, hy_f_w1, hy_f_b1, hy_f_w2, hy_f_b2, hy_f_freq, hy_f_w3, hy_decay, hy_skip, hy_w_out, hy_b_out, fa_w_qkv, fa_q_norm, fa_k_norm, fa_w_o, wa_w_qkv, wa_sink, wa_w_o, w_router, b_router, moe_w_gate, moe_w_up, moe_w_down):
    x = jnp.concatenate([x_prompt.reshape(TP, D_MODEL), x_sample.reshape(TS, D_MODEL)], axis=0)
    cond = jnp.concatenate([c_ctx[None, :], c, jnp.zeros((N_COND - 1 - DEC_BATCH, D_MODEL), F32)], axis=0)
    mods = _adaln(cond, w_mod, b_mod).reshape(DEPTH, N_COND, 6, 1, D_MODEL)
    rope = _rope_tables()
    ones_hd = jnp.ones((HEAD_DIM,), F32)
    wg_bf, wu_bf, wd_bf = moe_w_gate.astype(BF16), moe_w_up.astype(BF16), moe_w_down.astype(BF16)
    new_kv = {}
    for layer in range(DEPTH):
        kind = layer % 4
        j = layer // 4
        mod = mods[layer]
        g_mix = norm_mix[layer].reshape(1, D_MODEL)
        if kind == 0:
            x = _pool_mixer(x, g_mix, mod, pool_w[j], pool_scale[j])
        elif kind == 1:
            x = _hyena_mixer(x, g_mix, mod, hy_w_in[j], hy_b_in[j], hy_conv_w[j], hy_conv_b[j], hy_f_w1[j],
                             hy_f_b1[j], hy_f_w2[j], hy_f_b2[j], hy_f_freq[j], hy_f_w3[j], hy_decay[j],
                             hy_skip[j], hy_w_out[j], hy_b_out[j])
        elif kind == 2:
            x, nk, nv = _attn_mixer(x, g_mix, mod, fa_w_qkv[j], fa_q_norm[j], fa_k_norm[j], True, None,
                                    fa_w_o[j], cache_k_full[:, j], cache_v_full[:, j], False, rope)
            new_kv.setdefault("kf", []).append(nk)
            new_kv.setdefault("vf", []).append(nv)
        else:
            x, nk, nv = _attn_mixer(x, g_mix, mod, wa_w_qkv[j], ones_hd, ones_hd, False, wa_sink[j],
                                    wa_w_o[j], cache_k_win[:, j], cache_v_win[:, j], True, rope)
            new_kv.setdefault("kw", []).append(nk)
            new_kv.setdefault("vw", []).append(nv)
        x = _moe(layer, x, norm_ffn[layer].reshape(1, D_MODEL), mod, w_router, b_router, wg_bf, wu_bf, wd_bf)
    y_prompt = _final_norm(x, final_norm, 0, TP).reshape(BATCH, SEQ, D_MODEL)
    y_sample = _final_norm(x, final_norm, TP, TS).reshape(DEC_BATCH, DEC_SEQ, D_MODEL)
    return (y_prompt, y_sample, jnp.stack(new_kv["kf"], axis=1), jnp.stack(new_kv["vf"], axis=1),
            jnp.stack(new_kv["kw"], axis=1), jnp.stack(new_kv["vw"], axis=1))
```

```python
import functools
import math

import jax
import jax.numpy as jnp
import numpy as np
from jax import lax
from jax.experimental import pallas as pl
from jax.experimental.pallas import tpu as pltpu

D_MODEL = 2048
BATCH = 32
SEQ = 256
DEPTH = 4
DEC_BATCH = 4
DEC_SEQ = 4096
PAST_LEN = 512
GRID_W = 64
N_HEADS = 16
N_KV_HEADS = 4
HEAD_DIM = D_MODEL // N_HEADS
KV_GROUP = N_HEADS // N_KV_HEADS
KV_DIM = N_KV_HEADS * HEAD_DIM
QKV_DIM = (N_HEADS + 2 * N_KV_HEADS) * HEAD_DIM
ROPE_THETA = 10000.0
WINDOW = 128
POOL_WINDOWS = (2, 4, 8, 16)
POOL_GROUP = D_MODEL // len(POOL_WINDOWS)
HYENA_EMB_BANDS = 16
HYENA_FILTER_HIDDEN = 64
N_EXPERTS = 16
N_EXPERT_GROUPS = 4
EXPERTS_PER_GROUP = 4
D_EXPERT = 512
NORM_EPS = 1e-6
NEG_INF = -1e30

F32 = jnp.float32
BF16 = jnp.bfloat16

TP = BATCH * SEQ
TS = DEC_BATCH * DEC_SEQ
T = TP + TS
N_COND = 8
LANE = 128
MIB = 1024 * 1024

PAIR_LO = (0, 0, 0, 1, 1, 2)
PAIR_HI = (1, 2, 3, 2, 3, 3)
N_BUCKETS = N_EXPERT_GROUPS * len(PAIR_LO)
MOE_TILE = 256
MOE_AHEAD = 2
MOE_TILES = T // MOE_TILE + N_BUCKETS + MOE_AHEAD
T_PAD = MOE_TILES * MOE_TILE
HALF_D = D_MODEL // 2
XH_W = HALF_D + LANE


def _cp(sem, vmem_mb=48):
    return pltpu.CompilerParams(dimension_semantics=sem, vmem_limit_bytes=vmem_mb * MIB)


def _dot(a, b):
    return jnp.dot(a, b, preferred_element_type=F32)


def _dot3(a, b):
    ah = a.astype(BF16)
    al = (a - ah.astype(F32)).astype(BF16)
    bh = b.astype(BF16)
    bl = (b - bh.astype(F32)).astype(BF16)
    return _dot(ah, bh) + (_dot(al, bh) + _dot(ah, bl))


def _sigmoid(x):
    return 1.0 / (1.0 + jnp.exp(-x))


def _pack_bf16_pairs(x):
    n = x.shape[1] // 2
    bits = lambda v: lax.bitcast_convert_type(v.astype(BF16).astype(F32), jnp.uint32)
    return (bits(x[:, :n]) >> 16) | bits(x[:, n:])


def _unpack_bf16_pairs(u):
    lo = lax.bitcast_convert_type(u << 16, F32)
    hi = lax.bitcast_convert_type(u & jnp.uint32(0xFFFF0000), F32)
    return jnp.concatenate([lo, hi], axis=1)


def _cond_row(r):
    return jnp.where(r < TP, 0, 1 + (r - TP) // DEC_SEQ)


def _mod_spec(tm, chunk, tn=D_MODEL, ncol=False):
    if ncol:
        return pl.BlockSpec((None, None, 1, tn), lambda i, j: (_cond_row(i * tm), chunk, 0, j))
    return pl.BlockSpec((None, None, 1, tn), lambda i, *_: (_cond_row(i * tm), chunk, 0, 0))


def _norm_mod(x, g, shift, scale):
    var = jnp.mean(x * x, axis=-1, keepdims=True)
    y = x * lax.rsqrt(var + NORM_EPS) * g
    return y * (1.0 + scale) + shift


def _adaln_kernel(c_ref, w_ref, b_ref, o_ref):
    c = c_ref[...]
    a = c * _sigmoid(c)
    o_ref[...] = _dot3(a, w_ref[...]) + b_ref[...]


def _adaln(cond, w_mod, b_mod):
    tn = 1024
    n = 6 * D_MODEL
    return pl.pallas_call(
        _adaln_kernel,
        grid=(DEPTH, n // tn),
        in_specs=[
            pl.BlockSpec((N_COND, D_MODEL), lambda l, j: (0, 0)),
            pl.BlockSpec((None, D_MODEL, tn), lambda l, j: (l, 0, j)),
            pl.BlockSpec((None, 1, tn), lambda l, j: (l, 0, j)),
        ],
        out_specs=pl.BlockSpec((None, N_COND, tn), lambda l, j: (l, 0, j)),
        out_shape=jax.ShapeDtypeStruct((DEPTH, N_COND, n), F32),
        compiler_params=_cp(("parallel", "parallel")),
        name="adaln",
    )(cond, w_mod, b_mod.reshape(DEPTH, 1, n))


def _nm_matmul_kernel(x_ref, g_ref, sh_ref, sc_ref, w_ref, b_ref, o_ref, h_scr):
    @pl.when(pl.program_id(1) == 0)
    def _():
        h_scr[...] = _norm_mod(x_ref[...], g_ref[...], sh_ref[...], sc_ref[...]).astype(BF16)

    o_ref[...] = (_dot(h_scr[...], w_ref[...]) + b_ref[...]).astype(o_ref.dtype)


def _nm_matmul(x, g, mod, w, b, out_dtype, name):
    tm, tn = 1024, 1024
    n = w.shape[1]
    return pl.pallas_call(
        _nm_matmul_kernel,
        grid=(T // tm, n // tn),
        in_specs=[
            pl.BlockSpec((tm, D_MODEL), lambda i, j: (i, 0)),
            pl.BlockSpec((1, D_MODEL), lambda i, j: (0, 0)),
            _mod_spec(tm, 0),
            _mod_spec(tm, 1),
            pl.BlockSpec((D_MODEL, tn), lambda i, j: (0, j)),
            pl.BlockSpec((1, tn), lambda i, j: (0, j)),
        ],
        out_specs=pl.BlockSpec((tm, tn), lambda i, j: (i, j)),
        out_shape=jax.ShapeDtypeStruct((T, n), out_dtype),
        scratch_shapes=[pltpu.VMEM((tm, D_MODEL), BF16)],
        compiler_params=_cp(("parallel", "arbitrary")),
        name=name,
    )(x, g, mod, mod, w, b)


RESID_TM = 1024


def _resid_matmul_kernel(ap_ref, as_ref, w_ref, b_ref, x_ref, gt_ref, o_ref):
    def emit(a_ref):
        o_ref[...] = x_ref[...] + gt_ref[...] * (_dot(a_ref[...], w_ref[...]) + b_ref[...])

    is_ctx = pl.program_id(0) < TP // RESID_TM
    pl.when(is_ctx)(lambda: emit(ap_ref))
    pl.when(jnp.logical_not(is_ctx))(lambda: emit(as_ref))


def _resid_matmul(a_ctx, a_lat, w, b, x, mod, name):
    tm, tn = RESID_TM, 1024
    k = a_ctx.shape[1]
    n_ctx = TP // tm
    return pl.pallas_call(
        _resid_matmul_kernel,
        grid=(T // tm, D_MODEL // tn),
        in_specs=[
            pl.BlockSpec((tm, k), lambda i, j: (jnp.minimum(i, n_ctx - 1), 0)),
            pl.BlockSpec((tm, k), lambda i, j: (jnp.maximum(i - n_ctx, 0), 0)),
            pl.BlockSpec((k, tn), lambda i, j: (0, j)),
            pl.BlockSpec((1, tn), lambda i, j: (0, j)),
            pl.BlockSpec((tm, tn), lambda i, j: (i, j)),
            _mod_spec(tm, 2, tn, ncol=True),
        ],
        out_specs=pl.BlockSpec((tm, tn), lambda i, j: (i, j)),
        out_shape=jax.ShapeDtypeStruct((T, D_MODEL), F32),
        compiler_params=_cp(("parallel", "parallel")),
        name=name,
    )(a_ctx, a_lat, w, b, x, mod)


POOL_TILE = 256
POOL_HALO = 8


def _seq_pos(r0):
    is_ctx = r0 < TP
    loc0 = jnp.where(is_ctx, r0 % SEQ, (r0 - TP) % DEC_SEQ)
    seq_len = jnp.where(is_ctx, SEQ, DEC_SEQ)
    return loc0, seq_len


def _pool_kernel(xc_ref, xcp_ref, xcn_ref, xl_ref, xlp_ref, xln_ref, *rest):
    is_ctx = pl.program_id(0) < TP // POOL_TILE
    pl.when(is_ctx)(lambda: _pool_tile(xc_ref, xcp_ref, xcn_ref, *rest))
    pl.when(jnp.logical_not(is_ctx))(lambda: _pool_tile(xl_ref, xlp_ref, xln_ref, *rest))


def _pool_tile(x_ref, xp_ref, xn_ref, g_ref, sh_ref, sc_ref, gt_ref, pw_ref, ps_ref, o_ref, hz_scr):
    tm, hl = POOL_TILE, POOL_HALO
    loc0, seq_len = _seq_pos(pl.program_id(0) * tm)
    has_prev = loc0 > 0
    has_next = loc0 + tm < seq_len
    g, sh, sc = g_ref[...], sh_ref[...], sc_ref[...]
    x = x_ref[...]
    h = _norm_mod(x, g, sh, sc)
    hz_scr[0:hl, :] = jnp.where(has_prev, _norm_mod(xp_ref[...], g, sh, sc), 0.0)
    hz_scr[hl:hl + tm, :] = h
    hz_scr[hl + tm:, :] = jnp.where(has_next, _norm_mod(xn_ref[...], g, sh, sc), 0.0)
    tl = loc0 + lax.broadcasted_iota(jnp.int32, (tm, 1), 0)
    outs = []
    for gi, w in enumerate(POOL_WINDOWS):
        cs = slice(gi * POOL_GROUP, (gi + 1) * POOL_GROUP)
        s = jnp.zeros((tm, POOL_GROUP), F32)
        for off in range(-(w // 2), w - w // 2):
            s = s + hz_scr[hl + off:hl + off + tm, cs]
        lo = jnp.maximum(tl - w // 2, 0)
        hi = jnp.minimum(tl + (w - w // 2), seq_len)
        d = s / (hi - lo).astype(F32) - h[:, cs]
        outs.append(_dot(d.astype(BF16), pw_ref[gi]))
    out = jnp.concatenate(outs, axis=1) * ps_ref[...]
    o_ref[...] = x + gt_ref[...] * out


def _pool_mixer(x_ctx, x_lat, g, mod, pool_w, pool_scale):
    tm, hl = POOL_TILE, POOL_HALO
    r = tm // hl

    def stream(first_tile, rows):
        tile = lambda i: jnp.clip(i - first_tile, 0, rows // tm - 1)
        return [pl.BlockSpec((tm, D_MODEL), lambda i: (tile(i), 0)),
                pl.BlockSpec((hl, D_MODEL), lambda i: (jnp.maximum(tile(i) * r - 1, 0), 0)),
                pl.BlockSpec((hl, D_MODEL), lambda i: (jnp.minimum((tile(i) + 1) * r, rows // hl - 1), 0))]

    return pl.pallas_call(
        _pool_kernel,
        grid=(T // tm,),
        in_specs=stream(0, TP) + stream(TP // tm, TS) + [
            pl.BlockSpec((1, D_MODEL), lambda i: (0, 0)),
            _mod_spec(tm, 0),
            _mod_spec(tm, 1),
            _mod_spec(tm, 2),
            pl.BlockSpec((len(POOL_WINDOWS), POOL_GROUP, POOL_GROUP), lambda i: (0, 0, 0)),
            pl.BlockSpec((1, D_MODEL), lambda i: (0, 0)),
        ],
        out_specs=pl.BlockSpec((tm, D_MODEL), lambda i: (i, 0)),
        out_shape=jax.ShapeDtypeStruct((T, D_MODEL), F32),
        scratch_shapes=[pltpu.VMEM((tm + 2 * hl, D_MODEL), F32)],
        compiler_params=_cp(("parallel",)),
        name="pool_mixer",
    )(x_ctx, x_ctx, x_ctx, x_lat, x_lat, x_lat, g, mod, mod, mod, pool_w.astype(BF16),
      pool_scale.reshape(1, D_MODEL))


CONV_TILE = 256
CONV_HALO = 16


def _conv3_kernel(u_ref, up_ref, un_ref, cw_ref, cb_ref, o_ref, scr):
    tm, hl = CONV_TILE, CONV_HALO
    loc0, seq_len = _seq_pos(pl.program_id(0) * tm)
    has_prev = loc0 > 0
    has_next = loc0 + tm < seq_len
    scr[0:hl, :] = jnp.where(has_prev, up_ref[...].astype(F32), 0.0)
    scr[hl:hl + tm, :] = u_ref[...].astype(F32)
    scr[hl + tm:, :] = jnp.where(has_next, un_ref[...].astype(F32), 0.0)
    out = (scr[hl - 1:hl - 1 + tm, :] * cw_ref[0:1, :] + scr[hl:hl + tm, :] * cw_ref[1:2, :]
           + scr[hl + 1:hl + 1 + tm, :] * cw_ref[2:3, :] + cb_ref[...])
    o_ref[...] = out.astype(o_ref.dtype)


def _conv3(u0, conv_w, conv_b):
    tm, hl, tc = CONV_TILE, CONV_HALO, D_MODEL
    r = tm // hl
    n = u0.shape[1]
    return pl.pallas_call(
        _conv3_kernel,
        grid=(T // tm, n // tc),
        in_specs=[
            pl.BlockSpec((tm, tc), lambda i, j: (i, j)),
            pl.BlockSpec((hl, tc), lambda i, j: (jnp.maximum(i * r - 1, 0), j)),
            pl.BlockSpec((hl, tc), lambda i, j: (jnp.minimum((i + 1) * r, T // hl - 1), j)),
            pl.BlockSpec((3, tc), lambda i, j: (0, j)),
            pl.BlockSpec((1, tc), lambda i, j: (0, j)),
        ],
        out_specs=pl.BlockSpec((tm, tc), lambda i, j: (i, j)),
        out_shape=jax.ShapeDtypeStruct((T, n), BF16),
        scratch_shapes=[pltpu.VMEM((tm + 2 * hl, tc), F32)],
        compiler_params=_cp(("parallel", "parallel")),
        name="hyena_conv3",
    )(u0, u0, u0, conv_w, conv_b.reshape(1, n))


FILT_TILE = 256


def _filter_kernel(emb_ref, w1_ref, b1_ref, w2_ref, b2_ref, fr_ref, w3_ref, dc_ref, fa_ref, fb_ref):
    emb = emb_ref[...]
    fr = fr_ref[...]
    a = jnp.sin(fr * (_dot3(emb, w1_ref[...]) + b1_ref[...]))
    a = jnp.sin(fr * (_dot3(a, w2_ref[...]) + b2_ref[...]))
    t = emb[:, 0:1]
    filt = _dot3(a, w3_ref[...]) * jnp.exp(-t * jnp.abs(dc_ref[...]))
    hf = filt[:, :D_MODEL]
    row = pl.program_id(0) * FILT_TILE + lax.broadcasted_iota(jnp.int32, (FILT_TILE, 1), 0)
    hb = jnp.where(row == 0, 0.0, filt[:, D_MODEL:])
    fa_ref[...] = (hf + hb).astype(BF16)
    fb_ref[...] = (hb - hf).astype(BF16)


def _pad2(a, rows, cols):
    return jnp.pad(a, ((0, rows - a.shape[0]), (0, cols - a.shape[1])))


def _hyena_filters(L, f_w1, f_b1, f_w2, f_b2, f_freq, f_w3, decay):
    t = jnp.arange(L, dtype=F32) / L
    bands = jnp.linspace(1e-4, HYENA_EMB_BANDS - 1, HYENA_EMB_BANDS, dtype=F32)
    ang = (2 * math.pi) * t[:, None] * bands[None, :]
    emb = _pad2(jnp.concatenate([t[:, None], jnp.cos(ang), -jnp.sin(ang)], axis=-1), L, LANE)
    nf = 2 * D_MODEL
    tl = FILT_TILE
    small = lambda: pl.BlockSpec((LANE, LANE), lambda i, n: (0, 0))
    vec = lambda: pl.BlockSpec((1, LANE), lambda i, n: (0, 0))
    out = pl.BlockSpec((None, tl, D_MODEL), lambda i, n: (n, i, 0))
    return pl.pallas_call(
        _filter_kernel,
        grid=(L // tl, 2),
        in_specs=[
            pl.BlockSpec((tl, LANE), lambda i, n: (i, 0)),
            small(), vec(), small(), vec(), vec(),
            pl.BlockSpec((LANE, nf), lambda i, n: (0, n)),
            pl.BlockSpec((1, nf), lambda i, n: (0, n)),
        ],
        out_specs=[out, out],
        out_shape=[jax.ShapeDtypeStruct((2, L, D_MODEL), BF16)] * 2,
        compiler_params=_cp(("parallel", "parallel")),
        name="hyena_filters",
    )(emb, _pad2(f_w1, LANE, LANE), _pad2(f_b1[None], 1, LANE), _pad2(f_w2, LANE, LANE),
      _pad2(f_b2[None], 1, LANE), _pad2(f_freq[None], 1, LANE), _pad2(f_w3, LANE, 2 * nf), decay[None])


def _dft_mats(L):
    r = int(math.isqrt(L))
    k2 = 2 * jnp.arange(L, dtype=jnp.int32)[:, None] + 1
    n1 = r * jnp.arange(L // r, dtype=jnp.int32)[None, :]
    n2 = jnp.arange(r, dtype=jnp.int32)[None, :]
    sc = math.pi / (2 * L)
    aa = ((k2 * n1) % (4 * L)).astype(F32) * sc
    ab = ((k2 * n2) % (4 * L)).astype(F32) * sc
    ca, sa, cb, sb = jnp.cos(aa)[:, :, None], jnp.sin(aa)[:, :, None], jnp.cos(ab)[:, None, :], jnp.sin(ab)[:, None, :]
    c = (ca * cb - sa * sb).reshape(L, L)
    s = (sa * cb + ca * sb).reshape(L, L)
    return c.astype(BF16), s.astype(BF16), c.T.astype(BF16), s.T.astype(BF16)


def _dft_tiles(L):
    return min(512, L), 512


def _dft_filter_kernel(c_ref, s_ref, a_ref, b_ref, gr_ref, gi_ref):
    gr_ref[...] = _dot(c_ref[...], a_ref[...])
    gi_ref[...] = _dot(s_ref[...], b_ref[...])


def _dft_filter(cm, sm, fa, fb, L):
    tf, tn = _dft_tiles(L)
    mat = lambda: pl.BlockSpec((tf, L), lambda k, c, n: (k, 0))
    rhs = lambda: pl.BlockSpec((None, L, tn), lambda k, c, n: (n, 0, c))
    out = pl.BlockSpec((None, tf, tn), lambda k, c, n: (n, k, c))
    return pl.pallas_call(
        _dft_filter_kernel,
        grid=(L // tf, D_MODEL // tn, 2),
        in_specs=[mat(), mat(), rhs(), rhs()],
        out_specs=[out, out],
        out_shape=[jax.ShapeDtypeStruct((2, L, D_MODEL), F32)] * 2,
        compiler_params=_cp(("parallel", "parallel", "parallel")),
        name="hyena_filter_dft",
    )(cm, sm, fa, fb)


def _dft_fwd_kernel(c_ref, s_ref, z_ref, gr_ref, gi_ref, yr_ref, yi_ref):
    z = z_ref[...]
    zc = _dot(c_ref[...], z)
    zs = _dot(s_ref[...], z)
    gr, gi = gr_ref[...], gi_ref[...]
    yr_ref[...] = (gr * zc + gi * zs).astype(BF16)
    yi_ref[...] = (gi * zc - gr * zs).astype(BF16)


def _dft_fwd(cm, sm, z, z_rowblk, z_colblk, gr, gi, order, nb, L):
    tf, tn = _dft_tiles(L)
    mat = lambda: pl.BlockSpec((tf, L), lambda k, c, b: (k, 0))
    gsp = lambda: pl.BlockSpec((None, tf, tn), lambda k, c, b: (order, k, c))
    out = pl.BlockSpec((None, tf, tn), lambda k, c, b: (b, k, c))
    return pl.pallas_call(
        _dft_fwd_kernel,
        grid=(L // tf, D_MODEL // tn, nb),
        in_specs=[mat(), mat(),
                  pl.BlockSpec((L, tn), lambda k, c, b: (z_rowblk + b, z_colblk + c)),
                  gsp(), gsp()],
        out_specs=[out, out],
        out_shape=[jax.ShapeDtypeStruct((nb, L, D_MODEL), BF16)] * 2,
        compiler_params=_cp(("parallel", "parallel", "parallel")),
        name="hyena_dft_fwd",
    )(cm, sm, z, gr, gi)


def _dft_inv_kernel(ct_ref, st_ref, yr_ref, yi_ref, z_ref, gt_ref, sk_ref, o_ref, *, inv_len):
    y = (_dot(ct_ref[...], yr_ref[...]) - _dot(st_ref[...], yi_ref[...])) * inv_len
    o_ref[...] = (gt_ref[...].astype(F32) * (y + sk_ref[...] * z_ref[...].astype(F32))).astype(BF16)


def _dft_inv(ctm, stm, yr, yi, z, z_rowblk, z_colblk, gate, g_rowblk, g_colblk, skip, nb, L):
    tt, tn = _dft_tiles(L)
    rpb = L // tt
    mat = lambda: pl.BlockSpec((tt, L), lambda t, c, b: (t, 0))
    spec = lambda: pl.BlockSpec((None, L, tn), lambda t, c, b: (b, 0, c))
    return pl.pallas_call(
        functools.partial(_dft_inv_kernel, inv_len=1.0 / L),
        grid=(rpb, D_MODEL // tn, nb),
        in_specs=[mat(), mat(), spec(), spec(),
                  pl.BlockSpec((tt, tn), lambda t, c, b: (z_rowblk + b * rpb + t, z_colblk + c)),
                  pl.BlockSpec((tt, tn), lambda t, c, b: (g_rowblk + b * rpb + t, g_colblk + c)),
                  pl.BlockSpec((1, tn), lambda t, c, b: (0, c))],
        out_specs=pl.BlockSpec((tt, tn), lambda t, c, b: (b * rpb + t, c)),
        out_shape=jax.ShapeDtypeStruct((nb * L, D_MODEL), BF16),
        compiler_params=_cp(("parallel", "parallel", "parallel")),
        name="hyena_dft_inv",
    )(ctm, stm, yr, yi, z, gate, skip)


def _hyena_stream(u, row0, nb, L, fparams, skip):
    cm, sm, ctm, stm = _dft_mats(L)
    fa, fb = _hyena_filters(L, *fparams)
    gr, gi = _dft_filter(cm, sm, fa, fb, L)
    tt, tn = _dft_tiles(L)
    ncb = D_MODEL // tn
    yr, yi = _dft_fwd(cm, sm, u, row0 // L, 0, gr, gi, 0, nb, L)
    z1 = _dft_inv(ctm, stm, yr, yi, u, row0 // tt, 0, u, row0 // tt, ncb, skip[0:1], nb, L)
    yr, yi = _dft_fwd(cm, sm, z1, 0, 0, gr, gi, 1, nb, L)
    return _dft_inv(ctm, stm, yr, yi, z1, 0, 0, u, row0 // tt, 2 * ncb, skip[1:2], nb, L)


def _hyena_mixer(x, g, mod, w_in, b_in, conv_w, conv_b, f_w1, f_b1, f_w2, f_b2, f_freq, f_w3, decay, skip,
                 w_out, b_out):
    u0 = _nm_matmul(x, g, mod, w_in.astype(BF16), b_in.reshape(1, -1), BF16, "hyena_in_proj")
    u = _conv3(u0, conv_w, conv_b)
    fparams = (f_w1, f_b1, f_w2, f_b2, f_freq, f_w3, decay)
    zp = _hyena_stream(u, 0, BATCH, SEQ, fparams, skip)
    zs = _hyena_stream(u, TP, DEC_BATCH, DEC_SEQ, fparams, skip)
    return _resid_matmul(zp, zs, w_out.astype(BF16), b_out.reshape(1, -1), x, mod, "hyena_out_proj")


PREP_TILE = 256


def _rope_tables():
    pos = jnp.arange(DEC_SEQ, dtype=jnp.int32)
    row = (pos // GRID_W).astype(F32)
    col = (pos % GRID_W).astype(F32)
    axis_dim = HEAD_DIM // 2
    inv_freq = ROPE_THETA ** (-jnp.arange(0, axis_dim, 2, dtype=F32) / axis_dim)
    ar = row[:, None] * inv_freq[None, :]
    ac = col[:, None] * inv_freq[None, :]
    cos = jnp.concatenate([jnp.cos(ar), jnp.cos(ar), jnp.cos(ac), jnp.cos(ac)], axis=-1)
    sin = jnp.concatenate([-jnp.sin(ar), jnp.sin(ar), -jnp.sin(ac), jnp.sin(ac)], axis=-1)
    return cos, sin


def _prep_kernel(*refs, use_norm, use_rope, emit_kv):
    it = iter(refs)
    qkv_ref, qn_ref, kn_ref = next(it), next(it), next(it)
    cos_ref = sin_ref = None
    if use_rope:
        cos_ref, sin_ref = next(it), next(it)
    q_ref, k_ref, v_ref = next(it), next(it), next(it)
    nk_ref = nv_ref = None
    if emit_kv:
        nk_ref, nv_ref = next(it), next(it)
    quarter = HEAD_DIM // 4
    if use_rope:
        cos, sin = cos_ref[...], sin_ref[...]
        lane = lax.broadcasted_iota(jnp.int32, (PREP_TILE, HEAD_DIM), 1)
        first = (lane % (2 * quarter)) < quarter

    def head(xh, gn):
        if use_norm:
            xh = xh * lax.rsqrt(jnp.mean(xh * xh, axis=-1, keepdims=True) + NORM_EPS) * gn
        return xh

    def rope(xh):
        if not use_rope:
            return xh
        partner = jnp.where(first, pltpu.roll(xh, HEAD_DIM - quarter, 1), pltpu.roll(xh, quarter, 1))
        return xh * cos + partner * sin

    qn, kn = qn_ref[...], kn_ref[...]
    scale = HEAD_DIM ** -0.5 * LOG2E
    for h in range(N_HEADS):
        hs = slice(h * HEAD_DIM, (h + 1) * HEAD_DIM)
        q_ref[:, hs] = (rope(head(qkv_ref[:, hs], qn)) * scale).astype(BF16)
    for h in range(N_KV_HEADS):
        hs = slice(h * HEAD_DIM, (h + 1) * HEAD_DIM)
        ks = slice(D_MODEL + h * HEAD_DIM, D_MODEL + (h + 1) * HEAD_DIM)
        vs = slice(D_MODEL + KV_DIM + h * HEAD_DIM, D_MODEL + KV_DIM + (h + 1) * HEAD_DIM)
        kh = head(qkv_ref[:, ks], kn)
        vh = qkv_ref[:, vs]
        if emit_kv:
            nk_ref[:, hs] = kh
            nv_ref[:, hs] = vh
        k_ref[:, hs] = rope(kh).astype(BF16)
        v_ref[:, hs] = vh.astype(BF16)


def _attn_prep(qkv, row0, nrows, q_norm, k_norm, use_norm, rope, emit_kv):
    tm = PREP_TILE
    blk0 = row0 // tm
    in_specs = [pl.BlockSpec((tm, QKV_DIM), lambda i: (blk0 + i, 0)),
                pl.BlockSpec((1, HEAD_DIM), lambda i: (0, 0)),
                pl.BlockSpec((1, HEAD_DIM), lambda i: (0, 0))]
    args = [qkv, q_norm.reshape(1, HEAD_DIM), k_norm.reshape(1, HEAD_DIM)]
    if rope is not None:
        tab = lambda: pl.BlockSpec((tm, HEAD_DIM), lambda i: (i % (DEC_SEQ // tm), 0))
        in_specs += [tab(), tab()]
        args += list(rope)
    row = lambda w: pl.BlockSpec((tm, w), lambda i: (i, 0))
    out_specs = [row(D_MODEL), row(KV_DIM), row(KV_DIM)]
    out_shape = [jax.ShapeDtypeStruct((nrows, D_MODEL), BF16), jax.ShapeDtypeStruct((nrows, KV_DIM), BF16),
                 jax.ShapeDtypeStruct((nrows, KV_DIM), BF16)]
    if emit_kv:
        out_specs += [row(KV_DIM), row(KV_DIM)]
        out_shape += [jax.ShapeDtypeStruct((nrows, KV_DIM), F32)] * 2
    return pl.pallas_call(
        functools.partial(_prep_kernel, use_norm=use_norm, use_rope=rope is not None, emit_kv=emit_kv),
        grid=(nrows // tm,),
        in_specs=in_specs, out_specs=out_specs, out_shape=out_shape,
        compiler_params=_cp(("parallel",)),
        name="attn_prep",
    )(*args)


LOG2E = math.log2(math.e)
ATTN_TQ = 256


def _attn_kernel(*refs, tq, seq_len, n_ctx, windowed, has_sink):
    it = iter(refs)
    q_ref, k_ref, v_ref = next(it), next(it), next(it)
    sink_ref = next(it) if has_sink else None
    o_ref = next(it)
    if windowed:
        i = pl.program_id(2)
        span = tq + 2 * WINDOW
        start = pl.multiple_of(jnp.clip(i * tq - WINDOW, 0, seq_len - span), WINDOW)
        qpos = i * tq + lax.broadcasted_iota(jnp.int32, (tq, 1), 0)
        kpos = start + lax.broadcasted_iota(jnp.int32, (1, span), 1)
        segs = [(pl.ds(start, span), jnp.abs(kpos - qpos) <= WINDOW), (pl.ds(seq_len, n_ctx), None)]
    else:
        segs = [(slice(None), None)]
    for h in range(KV_GROUP):
        hs = slice(h * HEAD_DIM, (h + 1) * HEAD_DIM)
        qh = q_ref[:, hs]
        scores = []
        m = None
        for rows, mask in segs:
            s = lax.dot_general(qh, k_ref[rows, :], (((1,), (1,)), ((), ())), preferred_element_type=F32)
            if mask is not None:
                s = jnp.where(mask, s, NEG_INF)
            scores.append(s)
            ms = jnp.max(s, axis=-1, keepdims=True)
            m = ms if m is None else jnp.maximum(m, ms)
        if has_sink:
            sk = sink_ref[pl.program_id(1) * KV_GROUP + h]
            m = jnp.maximum(m, sk)
        l = jnp.exp2(sk - m) if has_sink else jnp.zeros_like(m)
        acc = jnp.zeros((tq, HEAD_DIM), F32)
        for (rows, _), s in zip(segs, scores):
            p = jnp.exp2(s - m)
            l = l + jnp.sum(p, axis=-1, keepdims=True)
            acc = acc + _dot(p.astype(BF16), v_ref[rows, :])
        o_ref[:, hs] = (acc / l).astype(BF16)


def _attention(q, k, v, sink, nb, L, n_ctx, windowed):
    tq = min(ATTN_TQ, L)
    nq = L // tq
    nk = L + n_ctx
    kv = lambda: pl.BlockSpec((None, nk, HEAD_DIM), lambda b, g, i: (b, 0, g))
    in_specs = [pl.BlockSpec((tq, KV_GROUP * HEAD_DIM), lambda b, g, i: (b * nq + i, g)), kv(), kv()]
    args = [q, k, v]
    if sink is not None:
        in_specs.append(pl.BlockSpec(memory_space=pltpu.SMEM))
        args.append(sink.astype(F32) * LOG2E)
    return pl.pallas_call(
        functools.partial(_attn_kernel, tq=tq, seq_len=L, n_ctx=n_ctx, windowed=windowed,
                          has_sink=sink is not None),
        grid=(nb, N_KV_HEADS, nq),
        in_specs=in_specs,
        out_specs=pl.BlockSpec((tq, KV_GROUP * HEAD_DIM), lambda b, g, i: (b * nq + i, g)),
        out_shape=jax.ShapeDtypeStruct((nb * L, D_MODEL), BF16),
        compiler_params=_cp(("parallel", "parallel", "parallel"), 56),
        name="attention",
    )(*args)


def _attn_mixer(x, g, mod, w_qkv, q_norm, k_norm, use_norm, sink, w_o, cache_k, cache_v, windowed, rope):
    zero_b = jnp.zeros((1, QKV_DIM), F32)
    qkv = _nm_matmul(x, g, mod, w_qkv.astype(BF16), zero_b, F32, "qkv_proj")
    qp, kp, vp, new_k, new_v = _attn_prep(qkv, 0, TP, q_norm, k_norm, use_norm, None, True)
    qs, ks, vs = _attn_prep(qkv, TP, TS, q_norm, k_norm, use_norm, rope, False)
    op = _attention(qp, kp.reshape(BATCH, SEQ, KV_DIM), vp.reshape(BATCH, SEQ, KV_DIM), sink, BATCH, SEQ, 0, False)
    kc = cache_k.reshape(DEC_BATCH, PAST_LEN, KV_DIM).astype(BF16)
    vc = cache_v.reshape(DEC_BATCH, PAST_LEN, KV_DIM).astype(BF16)
    k_all = jnp.concatenate([ks.reshape(DEC_BATCH, DEC_SEQ, KV_DIM), kc], axis=1)
    v_all = jnp.concatenate([vs.reshape(DEC_BATCH, DEC_SEQ, KV_DIM), vc], axis=1)
    osm = _attention(qs, k_all, v_all, sink, DEC_BATCH, DEC_SEQ, PAST_LEN, windowed)
    x = _resid_matmul(op, osm, w_o.astype(BF16), jnp.zeros((1, D_MODEL), F32), x, mod, "attn_out_proj")
    shape = (BATCH, SEQ, N_KV_HEADS, HEAD_DIM)
    return x, new_k.reshape(shape), new_v.reshape(shape)


ROUTE_TILE = 512
ROUTE_ROWS = 32


def _router_kernel(x_ref, g_ref, sh_ref, sc_ref, wr_ref, br_ref, xh_ref, rt_ref, cnt_ref, carry):
    tm = ROUTE_TILE
    i = pl.program_id(0)

    @pl.when(i == 0)
    def _():
        carry[...] = jnp.zeros_like(carry)

    h = _norm_mod(x_ref[...], g_ref[...], sh_ref[...], sc_ref[...])
    xh_ref[:, :HALF_D] = _pack_bf16_pairs(h)
    logits = _dot(h.astype(BF16), wr_ref[...])
    s = _sigmoid(logits.T[:N_EXPERTS, :])
    sb = s + br_ref[...]
    u = [s[e:e + 1, :] for e in range(N_EXPERTS)]
    v = [sb[e:e + 1, :] for e in range(N_EXPERTS)]

    gscore = []
    for gq in range(N_EXPERT_GROUPS):
        m = v[4 * gq:4 * gq + 4]
        best = m[PAIR_LO[0]] + m[PAIR_HI[0]]
        for a, b in zip(PAIR_LO[1:], PAIR_HI[1:]):
            best = jnp.maximum(best, m[a] + m[b])
        gscore.append(best)
    gidx = jnp.zeros((1, tm), jnp.int32)
    gbest = gscore[0]
    for gq in range(1, N_EXPERT_GROUPS):
        upd = gscore[gq] > gbest
        gidx = jnp.where(upd, gq, gidx)
        gbest = jnp.where(upd, gscore[gq], gbest)

    def pick(rows, j):
        out = rows[j]
        for gq in range(1, N_EXPERT_GROUPS):
            out = jnp.where(gidx == gq, rows[4 * gq + j], out)
        return out

    vin = [pick(v, j) for j in range(EXPERTS_PER_GROUP)]
    uin = [pick(u, j) for j in range(EXPERTS_PER_GROUP)]
    i1 = jnp.zeros((1, tm), jnp.int32)
    m1 = vin[0]
    for j in range(1, EXPERTS_PER_GROUP):
        upd = vin[j] > m1
        i1 = jnp.where(upd, j, i1)
        m1 = jnp.where(upd, vin[j], m1)
    i2 = jnp.full((1, tm), -1, jnp.int32)
    m2 = jnp.full((1, tm), -jnp.inf, F32)
    for j in range(EXPERTS_PER_GROUP):
        upd = (i1 != j) & (vin[j] > m2)
        i2 = jnp.where(upd, j, i2)
        m2 = jnp.where(upd, vin[j], m2)

    def sel(rows, idx):
        out = rows[0]
        for j in range(1, EXPERTS_PER_GROUP):
            out = jnp.where(idx == j, rows[j], out)
        return out

    w1, w2 = sel(uin, i1), sel(uin, i2)
    wsum = w1 + w2
    w1, w2 = w1 / wsum, w2 / wsum
    first_lo = i1 < i2
    lo = jnp.where(first_lo, i1, i2)
    hi = jnp.where(first_lo, i2, i1)
    w_lo = jnp.where(first_lo, w1, w2)
    w_hi = jnp.where(first_lo, w2, w1)
    pair = jnp.where(lo == 0, hi - 1, jnp.where(lo == 1, hi + 1, 5))
    bucket = gidx * len(PAIR_LO) + pair

    onehot = (lax.broadcasted_iota(jnp.int32, (ROUTE_ROWS, tm), 0) == bucket)
    tri = (lax.broadcasted_iota(jnp.int32, (tm, tm), 0) <= lax.broadcasted_iota(jnp.int32, (tm, tm), 1))
    cum = _dot(jnp.where(onehot, 1.0, 0.0).astype(BF16), jnp.where(tri, 1.0, 0.0).astype(BF16))
    rank = jnp.sum(jnp.where(onehot, cum - 1.0 + carry[...], 0.0), axis=0, keepdims=True)
    carry[...] = carry[...] + cum[:, tm - 1:tm]
    cnt_ref[...] = jnp.broadcast_to(carry[...], (ROUTE_ROWS, LANE))

    rt_ref[...] = jnp.zeros_like(rt_ref)
    rt_ref[0:1, :] = bucket.astype(F32)
    rt_ref[1:2, :] = rank
    wt = jnp.concatenate([w_lo, w_hi, jnp.zeros((LANE - 2, tm), F32)], axis=0)
    xh_ref[:, HALF_D:] = lax.bitcast_convert_type(wt.T, jnp.uint32)


def _router(x, g, mod, w_router, b_router):
    tm = ROUTE_TILE
    wr = _pad2(w_router, D_MODEL, LANE).astype(BF16)
    return pl.pallas_call(
        _router_kernel,
        grid=(T // tm,),
        in_specs=[
            pl.BlockSpec((tm, D_MODEL), lambda i: (i, 0)),
            pl.BlockSpec((1, D_MODEL), lambda i: (0, 0)),
            _mod_spec(tm, 3),
            _mod_spec(tm, 4),
            pl.BlockSpec((D_MODEL, LANE), lambda i: (0, 0)),
            pl.BlockSpec((N_EXPERTS, 1), lambda i: (0, 0)),
        ],
        out_specs=[
            pl.BlockSpec((tm, XH_W), lambda i: (i, 0)),
            pl.BlockSpec((8, tm), lambda i: (0, i)),
            pl.BlockSpec((ROUTE_ROWS, LANE), lambda i: (0, 0)),
        ],
        out_shape=[
            jax.ShapeDtypeStruct((T, XH_W), jnp.uint32),
            jax.ShapeDtypeStruct((8, T), F32),
            jax.ShapeDtypeStruct((ROUTE_ROWS, LANE), F32),
        ],
        scratch_shapes=[pltpu.VMEM((ROUTE_ROWS, 1), F32)],
        compiler_params=_cp(("arbitrary",)),
        name="moe_router",
    )(x, g, mod, mod, wr, b_router.reshape(N_EXPERTS, 1))


DISPATCH_TILE = 256


DMA_UNROLL = 8


def _invert_kernel(dest_ref, src_ref):
    def clear(s, c):
        src_ref[s] = 0
        return c

    def put(t, c):
        src_ref[dest_ref[t]] = t
        return c

    lax.fori_loop(0, T_PAD, clear, 0, unroll=DMA_UNROLL)
    lax.fori_loop(0, T, put, 0, unroll=DMA_UNROLL)


def _invert(dest):
    return pl.pallas_call(
        _invert_kernel,
        in_specs=[pl.BlockSpec(memory_space=pltpu.SMEM)],
        out_specs=pl.BlockSpec(memory_space=pltpu.SMEM),
        out_shape=jax.ShapeDtypeStruct((T_PAD,), jnp.int32),
        name="moe_invert",
    )(dest)


def _gather_rows(idx_ref, base, src_hbm, buf, sem, tm):
    def start(r, c):
        pltpu.make_async_copy(src_hbm.at[pl.ds(idx_ref[base + r], 1)], buf.at[pl.ds(r, 1)], sem).start()
        return c

    lax.fori_loop(0, tm, start, 0, unroll=DMA_UNROLL)


def _wait_rows(src_hbm, buf, sem, tm):
    pltpu.make_async_copy(src_hbm.at[pl.ds(0, tm)], buf, sem).wait()


def _expert_kernel(ea_ref, eb_ref, nv_ref, src_ref, xh_hbm, ga_ref, ua_ref, da_ref, gb_ref, ub_ref, db_ref, y_ref,
                   xbuf0, xbuf1, xbuf2, sems):
    tm = MOE_TILE
    j = pl.program_id(0)
    nv = nv_ref[j]
    bufs = (xbuf0, xbuf1, xbuf2)
    n_buf = len(bufs)

    @pl.when(j == 0)
    def _():
        for t in range(MOE_AHEAD):
            _gather_rows(src_ref, t * tm, xh_hbm, bufs[t], sems.at[t], tm)

    def run(p):
        cur, cur_sem = bufs[p], sems.at[p]
        q = (p + MOE_AHEAD) % n_buf
        ahead, ahead_sem = bufs[q], sems.at[q]

        @pl.when(jnp.logical_or(j < MOE_AHEAD, nv_ref[jnp.maximum(j - MOE_AHEAD, 0)] > 0))
        def _():
            _wait_rows(xh_hbm, cur, cur_sem, tm)

        @pl.when(nv > 0)
        def _():
            valid = lax.broadcasted_iota(jnp.int32, (tm, 1), 0) < nv
            x = jnp.where(valid, _unpack_bf16_pairs(cur[:, :HALF_D]), 0.0).astype(BF16)
            wts = jnp.where(valid, lax.bitcast_convert_type(cur[:, HALF_D:], F32), 0.0)
            for r in range(tm):
                pltpu.make_async_copy(xh_hbm.at[pl.ds(src_ref[(j + MOE_AHEAD) * tm + r], 1)],
                                      ahead.at[pl.ds(r, 1)], ahead_sem).start()

            def ffn(g_ref, u_ref, d_ref, w):
                a = _dot(x, g_ref[...])
                h = a * _sigmoid(a) * _dot(x, u_ref[...]) * w
                return _dot(h.astype(BF16), d_ref[...])

            y = ffn(ga_ref, ua_ref, da_ref, wts[:, 0:1]) + ffn(gb_ref, ub_ref, db_ref, wts[:, 1:2])
            y_ref[...] = _pack_bf16_pairs(y)

    for p in range(n_buf):
        pl.when(j % n_buf == p)(functools.partial(run, p))

    @pl.when(nv == 0)
    def _():
        y_ref[...] = jnp.zeros_like(y_ref)


def _experts(layer, tile_ea, tile_eb, tile_nv, src, xh, w_gate, w_up, w_down):
    tm = MOE_TILE
    up = lambda sel: pl.BlockSpec((None, None, D_MODEL, D_EXPERT),
                                  lambda j, ea, eb, nv, sr: (layer, (ea, eb)[sel][j], 0, 0))
    down = lambda sel: pl.BlockSpec((None, None, D_EXPERT, D_MODEL),
                                    lambda j, ea, eb, nv, sr: (layer, (ea, eb)[sel][j], 0, 0))
    return pl.pallas_call(
        _expert_kernel,
        grid_spec=pltpu.PrefetchScalarGridSpec(
            num_scalar_prefetch=4,
            grid=(MOE_TILES,),
            in_specs=[pl.BlockSpec(memory_space=pl.ANY), up(0), up(0), down(0), up(1), up(1), down(1)],
            out_specs=pl.BlockSpec((tm, HALF_D), lambda j, ea, eb, nv, sr: (j, 0)),
            scratch_shapes=[pltpu.VMEM((tm, XH_W), jnp.uint32)] * (MOE_AHEAD + 1)
            + [pltpu.SemaphoreType.DMA((MOE_AHEAD + 1,))],
        ),
        out_shape=jax.ShapeDtypeStruct((T_PAD, HALF_D), jnp.uint32),
        compiler_params=_cp(("arbitrary",), 56),
        name="moe_experts",
    )(tile_ea, tile_eb, tile_nv, src, xh, w_gate, w_up, w_down, w_gate, w_up, w_down)


def _combine_kernel(dest_ref, x_ref, gt_ref, ys_hbm, o_ref, buf, sems):
    tm = DISPATCH_TILE
    i = pl.program_id(0)
    slot = i % 2

    @pl.when(i == 0)
    def _():
        _gather_rows(dest_ref, 0, ys_hbm, buf.at[0], sems.at[0], tm)

    @pl.when(i + 1 < pl.num_programs(0))
    def _():
        _gather_rows(dest_ref, (i + 1) * tm, ys_hbm, buf.at[1 - slot], sems.at[1 - slot], tm)

    _wait_rows(ys_hbm, buf.at[slot], sems.at[slot], tm)
    o_ref[...] = x_ref[...] + gt_ref[...] * _unpack_bf16_pairs(buf[slot])


def _combine(dest, x, mod, ys):
    tm = DISPATCH_TILE
    return pl.pallas_call(
        _combine_kernel,
        grid_spec=pltpu.PrefetchScalarGridSpec(
            num_scalar_prefetch=1,
            grid=(T // tm,),
            in_specs=[pl.BlockSpec((tm, D_MODEL), lambda i, d: (i, 0)),
                      pl.BlockSpec((None, None, 1, D_MODEL), lambda i, d: (_cond_row(i * tm), 5, 0, 0)),
                      pl.BlockSpec(memory_space=pl.ANY)],
            out_specs=pl.BlockSpec((tm, D_MODEL), lambda i, d: (i, 0)),
            scratch_shapes=[pltpu.VMEM((2, tm, HALF_D), jnp.uint32), pltpu.SemaphoreType.DMA((2,))],
        ),
        out_shape=jax.ShapeDtypeStruct((T, D_MODEL), F32),
        compiler_params=_cp(("arbitrary",)),
        name="moe_combine",
    )(dest, x, mod, ys)


def _lookup(table, idx):
    n = table.shape[0]
    hit = idx[:, None] == jnp.arange(n, dtype=jnp.int32)[None, :]
    return jnp.sum(jnp.where(hit, table[None, :], 0), axis=1)


def _moe_plan(rt, cnt):
    bucket = rt[0].astype(jnp.int32)
    rank = rt[1].astype(jnp.int32)
    counts = cnt[:N_BUCKETS, 0].astype(jnp.int32)
    tiles = (counts + MOE_TILE - 1) // MOE_TILE
    order = jnp.arange(N_BUCKETS, dtype=jnp.int32)
    tile_start = jnp.sum(jnp.where(order[None, :] < order[:, None], tiles[None, :], 0), axis=1)
    tile_end = tile_start + tiles
    n_used = tile_end[N_BUCKETS - 1]
    dest = _lookup(tile_start * MOE_TILE, bucket) + rank
    j = jnp.arange(MOE_TILES, dtype=jnp.int32)
    jc = jnp.minimum(j, n_used - 1)
    b = jnp.minimum(jnp.sum((jc[:, None] >= tile_end[None, :]).astype(jnp.int32), axis=1), N_BUCKETS - 1)
    nv = jnp.clip(_lookup(counts, b) - (j - _lookup(tile_start, b)) * MOE_TILE, 0, MOE_TILE)
    nv = jnp.where(j < n_used, nv, 0)
    n_pairs = len(PAIR_LO)
    ea = (b // n_pairs) * EXPERTS_PER_GROUP + _lookup(jnp.asarray(PAIR_LO, jnp.int32), b % n_pairs)
    eb = (b // n_pairs) * EXPERTS_PER_GROUP + _lookup(jnp.asarray(PAIR_HI, jnp.int32), b % n_pairs)
    return dest, ea, eb, nv


def _moe(layer, x, g, mod, w_router, b_router, w_gate, w_up, w_down):
    xh, rt, cnt = _router(x, g, mod, w_router, b_router)
    dest, ea, eb, nv = _moe_plan(rt, cnt)
    ys = _experts(layer, ea, eb, nv, _invert(dest), xh, w_gate, w_up, w_down)
    return _combine(dest, x, mod, ys)


def _final_norm_kernel(x_ref, g_ref, o_ref):
    x = x_ref[...]
    o_ref[...] = x * lax.rsqrt(jnp.mean(x * x, axis=-1, keepdims=True) + NORM_EPS) * g_ref[...]


def _final_norm(x, g, row0, nrows):
    tm = 512
    blk0 = row0 // tm
    return pl.pallas_call(
        _final_norm_kernel,
        grid=(nrows // tm,),
        in_specs=[pl.BlockSpec((tm, D_MODEL), lambda i: (blk0 + i, 0)), pl.BlockSpec((1, D_MODEL), lambda i: (0, 0))],
        out_specs=pl.BlockSpec((tm, D_MODEL), lambda i: (i, 0)),
        out_shape=jax.ShapeDtypeStruct((nrows, D_MODEL), F32),
        compiler_params=_cp(("parallel",)),
        name="final_norm",
    )(x, g.reshape(1, D_MODEL))


def kernel(x_prompt, x_sample, cache_k_full, cache_v_full, cache_k_win, cache_v_win, c, c_ctx, w_mod, b_mod, norm_mix, norm_ffn, final_norm, pool_w, pool_scale, hy_w_in, hy_b_in, hy_conv_w, hy_conv_b, hy_f_w1, hy_f_b1, hy_f_w2, hy_f_b2, hy_f_freq, hy_f_w3, hy_decay, hy_skip, hy_w_out, hy_b_out, fa_w_qkv, fa_q_norm, fa_k_norm, fa_w_o, wa_w_qkv, wa_sink, wa_w_o, w_router, b_router, moe_w_gate, moe_w_up, moe_w_down):
    x = None
    cond =jnp.concatenate([c_ctx[None, :], c, jnp.zeros((N_COND - 1 - DEC_BATCH, D_MODEL), F32)], axis=0)
    mods = _adaln(cond, w_mod, b_mod).reshape(DEPTH, N_COND, 6, 1, D_MODEL)
    rope = _rope_tables()
    ones_hd = jnp.ones((HEAD_DIM,), F32)
    wg_bf, wu_bf, wd_bf = moe_w_gate.astype(BF16), moe_w_up.astype(BF16), moe_w_down.astype(BF16)
    new_kv = {}
    for layer in range(DEPTH):
        kind = layer % 4
        j = layer // 4
        mod = mods[layer]
        g_mix = norm_mix[layer].reshape(1, D_MODEL)
        if kind == 0:
            assert layer == 0, "the pooling mixer reads the two input streams, so it must be the first layer"
            x = _pool_mixer(x_prompt.reshape(TP, D_MODEL), x_sample.reshape(TS, D_MODEL), g_mix, mod,
                            pool_w[j], pool_scale[j])
        elif kind == 1:
            x = _hyena_mixer(x, g_mix, mod, hy_w_in[j], hy_b_in[j], hy_conv_w[j], hy_conv_b[j], hy_f_w1[j],
                             hy_f_b1[j], hy_f_w2[j], hy_f_b2[j], hy_f_freq[j], hy_f_w3[j], hy_decay[j],
                             hy_skip[j], hy_w_out[j], hy_b_out[j])
        elif kind == 2:
            x, nk, nv = _attn_mixer(x, g_mix, mod, fa_w_qkv[j], fa_q_norm[j], fa_k_norm[j], True, None,
                                    fa_w_o[j], cache_k_full[:, j], cache_v_full[:, j], False, rope)
            new_kv.setdefault("kf", []).append(nk)
            new_kv.setdefault("vf", []).append(nv)
        else:
            x, nk, nv = _attn_mixer(x, g_mix, mod, wa_w_qkv[j], ones_hd, ones_hd, False, wa_sink[j],
                                    wa_w_o[j], cache_k_win[:, j], cache_v_win[:, j], True, rope)
            new_kv.setdefault("kw", []).append(nk)
            new_kv.setdefault("vw", []).append(nv)
        x = _moe(layer, x, norm_ffn[layer].reshape(1, D_MODEL), mod, w_router, b_router, wg_bf, wu_bf, wd_bf)
    y_prompt = _final_norm(x, final_norm, 0, TP).reshape(BATCH, SEQ, D_MODEL)
    y_sample = _final_norm(x, final_norm, TP, TS).reshape(DEC_BATCH, DEC_SEQ, D_MODEL)
    return (y_prompt, y_sample, jnp.stack(new_kv["kf"], axis=1), jnp.stack(new_kv["vf"], axis=1),
            jnp.stack(new_kv["kw"], axis=1), jnp.stack(new_kv["vw"], axis=1))
```

```python
import functools
import math

import jax
import jax.numpy as jnp
import numpy as np
from jax import lax
from jax.experimental import pallas as pl
from jax.experimental.pallas import tpu as pltpu

D_MODEL = 2048
BATCH = 32
SEQ = 256
DEPTH = 4
DEC_BATCH = 4
DEC_SEQ = 4096
PAST_LEN = 512
GRID_W = 64
N_HEADS = 16
N_KV_HEADS = 4
HEAD_DIM = D_MODEL // N_HEADS
KV_GROUP = N_HEADS // N_KV_HEADS
KV_DIM = N_KV_HEADS * HEAD_DIM
QKV_DIM = (N_HEADS + 2 * N_KV_HEADS) * HEAD_DIM
ROPE_THETA = 10000.0
WINDOW = 128
POOL_WINDOWS = (2, 4, 8, 16)
POOL_GROUP = D_MODEL // len(POOL_WINDOWS)
HYENA_EMB_BANDS = 16
HYENA_FILTER_HIDDEN = 64
N_EXPERTS = 16
N_EXPERT_GROUPS = 4
EXPERTS_PER_GROUP = 4
D_EXPERT = 512
NORM_EPS = 1e-6
NEG_INF = -1e30

F32 = jnp.float32
BF16 = jnp.bfloat16

TP = BATCH * SEQ
TS = DEC_BATCH * DEC_SEQ
T = TP + TS
N_COND = 8
LANE = 128
MIB = 1024 * 1024

PAIR_LO = (0, 0, 0, 1, 1, 2)
PAIR_HI = (1, 2, 3, 2, 3, 3)
N_BUCKETS = N_EXPERT_GROUPS * len(PAIR_LO)
MOE_TILE = 256
MOE_AHEAD = 2
MOE_TILES = T // MOE_TILE + N_BUCKETS + MOE_AHEAD
T_PAD = MOE_TILES * MOE_TILE
HALF_D = D_MODEL // 2
XH_W = HALF_D + LANE


def _cp(sem, vmem_mb=48):
    return pltpu.CompilerParams(dimension_semantics=sem, vmem_limit_bytes=vmem_mb * MIB)


def _dot(a, b):
    return jnp.dot(a, b, preferred_element_type=F32)


def _dot3(a, b):
    ah = a.astype(BF16)
    al = (a - ah.astype(F32)).astype(BF16)
    bh = b.astype(BF16)
    bl = (b - bh.astype(F32)).astype(BF16)
    return _dot(ah, bh) + (_dot(al, bh) + _dot(ah, bl))


def _sigmoid(x):
    return 1.0 / (1.0 + jnp.exp(-x))


def _pack_bf16_pairs(x):
    n = x.shape[1] // 2
    bits = lambda v: lax.bitcast_convert_type(v.astype(BF16).astype(F32), jnp.uint32)
    return (bits(x[:, :n]) >> 16) | bits(x[:, n:])


def _unpack_bf16_pairs(u):
    lo = lax.bitcast_convert_type(u << 16, F32)
    hi = lax.bitcast_convert_type(u & jnp.uint32(0xFFFF0000), F32)
    return jnp.concatenate([lo, hi], axis=1)


def _cond_row(r):
    return jnp.where(r < TP, 0, 1 + (r - TP) // DEC_SEQ)


def _mod_spec(tm, chunk, tn=D_MODEL, ncol=False):
    if ncol:
        return pl.BlockSpec((None, None, 1, tn), lambda i, j: (_cond_row(i * tm), chunk, 0, j))
    return pl.BlockSpec((None, None, 1, tn), lambda i, *_: (_cond_row(i * tm), chunk, 0, 0))


def _norm_mod(x, g, shift, scale):
    var = jnp.mean(x * x, axis=-1, keepdims=True)
    y = x * lax.rsqrt(var + NORM_EPS) * g
    return y * (1.0 + scale) + shift


def _adaln_kernel(c_ref, w_ref, b_ref, o_ref):
    c = c_ref[...]
    a = c * _sigmoid(c)
    o_ref[...] = _dot3(a, w_ref[...]) + b_ref[...]


def _adaln(cond, w_mod, b_mod):
    tn = 1024
    n = 6 * D_MODEL
    return pl.pallas_call(
        _adaln_kernel,
        grid=(DEPTH, n // tn),
        in_specs=[
            pl.BlockSpec((N_COND, D_MODEL), lambda l, j: (0, 0)),
            pl.BlockSpec((None, D_MODEL, tn), lambda l, j: (l, 0, j)),
            pl.BlockSpec((None, 1, tn), lambda l, j: (l, 0, j)),
        ],
        out_specs=pl.BlockSpec((None, N_COND, tn), lambda l, j: (l, 0, j)),
        out_shape=jax.ShapeDtypeStruct((DEPTH, N_COND, n), F32),
        compiler_params=_cp(("parallel", "parallel")),
        name="adaln",
    )(cond, w_mod, b_mod.reshape(DEPTH, 1, n))


def _nm_matmul_kernel(x_ref, g_ref, sh_ref, sc_ref, w_ref, b_ref, o_ref, h_scr):
    @pl.when(pl.program_id(1) == 0)
    def _():
        h_scr[...] = _norm_mod(x_ref[...], g_ref[...], sh_ref[...], sc_ref[...]).astype(BF16)

    o_ref[...] = (_dot(h_scr[...], w_ref[...]) + b_ref[...]).astype(o_ref.dtype)


def _nm_matmul(x, g, mod, w, b, out_dtype, name):
    tm, tn = 1024, 1024
    n = w.shape[1]
    return pl.pallas_call(
        _nm_matmul_kernel,
        grid=(T // tm, n // tn),
        in_specs=[
            pl.BlockSpec((tm, D_MODEL), lambda i, j: (i, 0)),
            pl.BlockSpec((1, D_MODEL), lambda i, j: (0, 0)),
            _mod_spec(tm, 0),
            _mod_spec(tm, 1),
            pl.BlockSpec((D_MODEL, tn), lambda i, j: (0, j)),
            pl.BlockSpec((1, tn), lambda i, j: (0, j)),
        ],
        out_specs=pl.BlockSpec((tm, tn), lambda i, j: (i, j)),
        out_shape=jax.ShapeDtypeStruct((T, n), out_dtype),
        scratch_shapes=[pltpu.VMEM((tm, D_MODEL), BF16)],
        compiler_params=_cp(("parallel", "arbitrary")),
        name=name,
    )(x, g, mod, mod, w, b)


RESID_TM = 1024


def _resid_matmul_kernel(ap_ref, as_ref, w_ref, b_ref, x_ref, gt_ref, o_ref):
    def emit(a_ref):
        o_ref[...] = x_ref[...] + gt_ref[...] * (_dot(a_ref[...], w_ref[...]) + b_ref[...])

    is_ctx = pl.program_id(0) < TP // RESID_TM
    pl.when(is_ctx)(lambda: emit(ap_ref))
    pl.when(jnp.logical_not(is_ctx))(lambda: emit(as_ref))


def _resid_matmul(a_ctx, a_lat, w, b, x, mod, name):
    tm, tn = RESID_TM, 1024
    k = a_ctx.shape[1]
    n_ctx = TP // tm
    return pl.pallas_call(
        _resid_matmul_kernel,
        grid=(T // tm, D_MODEL // tn),
        in_specs=[
            pl.BlockSpec((tm, k), lambda i, j: (jnp.minimum(i, n_ctx - 1), 0)),
            pl.BlockSpec((tm, k), lambda i, j: (jnp.maximum(i - n_ctx, 0), 0)),
            pl.BlockSpec((k, tn), lambda i, j: (0, j)),
            pl.BlockSpec((1, tn), lambda i, j: (0, j)),
            pl.BlockSpec((tm, tn), lambda i, j: (i, j)),
            _mod_spec(tm, 2, tn, ncol=True),
        ],
        out_specs=pl.BlockSpec((tm, tn), lambda i, j: (i, j)),
        out_shape=jax.ShapeDtypeStruct((T, D_MODEL), F32),
        compiler_params=_cp(("parallel", "parallel")),
        name=name,
    )(a_ctx, a_lat, w, b, x, mod)


POOL_TILE = 256
POOL_HALO = 8


def _seq_pos(r0):
    is_ctx = r0 < TP
    loc0 = jnp.where(is_ctx, r0 % SEQ, (r0 - TP) % DEC_SEQ)
    seq_len = jnp.where(is_ctx, SEQ, DEC_SEQ)
    return loc0, seq_len


def _pool_kernel(xc_ref, xcp_ref, xcn_ref, xl_ref, xlp_ref, xln_ref, *rest):
    is_ctx = pl.program_id(0) < TP // POOL_TILE
    pl.when(is_ctx)(lambda: _pool_tile(xc_ref, xcp_ref, xcn_ref, *rest))
    pl.when(jnp.logical_not(is_ctx))(lambda: _pool_tile(xl_ref, xlp_ref, xln_ref, *rest))


def _pool_tile(x_ref, xp_ref, xn_ref, g_ref, sh_ref, sc_ref, gt_ref, pw_ref, ps_ref, o_ref, hz_scr):
    tm, hl = POOL_TILE, POOL_HALO
    loc0, seq_len = _seq_pos(pl.program_id(0) * tm)
    has_prev = loc0 > 0
    has_next = loc0 + tm < seq_len
    g, sh, sc = g_ref[...], sh_ref[...], sc_ref[...]
    x = x_ref[...]
    h = _norm_mod(x, g, sh, sc)
    hz_scr[0:hl, :] = jnp.where(has_prev, _norm_mod(xp_ref[...], g, sh, sc), 0.0)
    hz_scr[hl:hl + tm, :] = h
    hz_scr[hl + tm:, :] = jnp.where(has_next, _norm_mod(xn_ref[...], g, sh, sc), 0.0)
    tl = loc0 + lax.broadcasted_iota(jnp.int32, (tm, 1), 0)
    outs = []
    for gi, w in enumerate(POOL_WINDOWS):
        cs = slice(gi * POOL_GROUP, (gi + 1) * POOL_GROUP)
        s = jnp.zeros((tm, POOL_GROUP), F32)
        for off in range(-(w // 2), w - w // 2):
            s = s + hz_scr[hl + off:hl + off + tm, cs]
        lo = jnp.maximum(tl - w // 2, 0)
        hi = jnp.minimum(tl + (w - w // 2), seq_len)
        d = s / (hi - lo).astype(F32) - h[:, cs]
        outs.append(_dot(d.astype(BF16), pw_ref[gi]))
    out = jnp.concatenate(outs, axis=1) * ps_ref[...]
    o_ref[...] = x + gt_ref[...] * out


def _pool_mixer(x_ctx, x_lat, g, mod, pool_w, pool_scale):
    tm, hl = POOL_TILE, POOL_HALO
    r = tm // hl

    def stream(first_tile, rows):
        tile = lambda i: jnp.clip(i - first_tile, 0, rows // tm - 1)
        return [pl.BlockSpec((tm, D_MODEL), lambda i: (tile(i), 0)),
                pl.BlockSpec((hl, D_MODEL), lambda i: (jnp.maximum(tile(i) * r - 1, 0), 0)),
                pl.BlockSpec((hl, D_MODEL), lambda i: (jnp.minimum((tile(i) + 1) * r, rows // hl - 1), 0))]

    return pl.pallas_call(
        _pool_kernel,
        grid=(T // tm,),
        in_specs=stream(0, TP) + stream(TP // tm, TS) + [
            pl.BlockSpec((1, D_MODEL), lambda i: (0, 0)),
            _mod_spec(tm, 0),
            _mod_spec(tm, 1),
            _mod_spec(tm, 2),
            pl.BlockSpec((len(POOL_WINDOWS), POOL_GROUP, POOL_GROUP), lambda i: (0, 0, 0)),
            pl.BlockSpec((1, D_MODEL), lambda i: (0, 0)),
        ],
        out_specs=pl.BlockSpec((tm, D_MODEL), lambda i: (i, 0)),
        out_shape=jax.ShapeDtypeStruct((T, D_MODEL), F32),
        scratch_shapes=[pltpu.VMEM((tm + 2 * hl, D_MODEL), F32)],
        compiler_params=_cp(("parallel",)),
        name="pool_mixer",
    )(x_ctx, x_ctx, x_ctx, x_lat, x_lat, x_lat, g, mod, mod, mod, pool_w.astype(BF16),
      pool_scale.reshape(1, D_MODEL))


CONV_TILE = 256
CONV_HALO = 16


def _conv3_kernel(u_ref, up_ref, un_ref, cw_ref, cb_ref, o_ref, scr):
    tm, hl = CONV_TILE, CONV_HALO
    loc0, seq_len = _seq_pos(pl.program_id(0) * tm)
    has_prev = loc0 > 0
    has_next = loc0 + tm < seq_len
    scr[0:hl, :] = jnp.where(has_prev, up_ref[...].astype(F32), 0.0)
    scr[hl:hl + tm, :] = u_ref[...].astype(F32)
    scr[hl + tm:, :] = jnp.where(has_next, un_ref[...].astype(F32), 0.0)
    out = (scr[hl - 1:hl - 1 + tm, :] * cw_ref[0:1, :] + scr[hl:hl + tm, :] * cw_ref[1:2, :]
           + scr[hl + 1:hl + 1 + tm, :] * cw_ref[2:3, :] + cb_ref[...])
    o_ref[...] = out.astype(o_ref.dtype)


def _conv3(u0, conv_w, conv_b):
    tm, hl, tc = CONV_TILE, CONV_HALO, D_MODEL
    r = tm // hl
    n = u0.shape[1]
    return pl.pallas_call(
        _conv3_kernel,
        grid=(T // tm, n // tc),
        in_specs=[
            pl.BlockSpec((tm, tc), lambda i, j: (i, j)),
            pl.BlockSpec((hl, tc), lambda i, j: (jnp.maximum(i * r - 1, 0), j)),
            pl.BlockSpec((hl, tc), lambda i, j: (jnp.minimum((i + 1) * r, T // hl - 1), j)),
            pl.BlockSpec((3, tc), lambda i, j: (0, j)),
            pl.BlockSpec((1, tc), lambda i, j: (0, j)),
        ],
        out_specs=pl.BlockSpec((tm, tc), lambda i, j: (i, j)),
        out_shape=jax.ShapeDtypeStruct((T, n), BF16),
        scratch_shapes=[pltpu.VMEM((tm + 2 * hl, tc), F32)],
        compiler_params=_cp(("parallel", "parallel")),
        name="hyena_conv3",
    )(u0, u0, u0, conv_w, conv_b.reshape(1, n))


FILT_TILE = 256


HYENA_BLOCK = 1024


def _filter_kernel(e1_ref, e2_ref, w1_ref, b1_ref, w2_ref, b2_ref, fr_ref, w3a_ref, dca_ref, w3b_ref, dcb_ref,
                   fa_ref, fb_ref, *, blk):
    fr = fr_ref[...]

    def taps(emb, w3_ref, dc_ref):
        a = jnp.sin(fr * (_dot3(emb, w1_ref[...]) + b1_ref[...]))
        a = jnp.sin(fr * (_dot3(a, w2_ref[...]) + b2_ref[...]))
        return _dot3(a, w3_ref[...]) * jnp.exp(-emb[:, 0:1] * jnp.abs(dc_ref[...]))

    pos = taps(e1_ref[...], w3a_ref, dca_ref)
    neg = taps(e2_ref[...], w3b_ref, dcb_ref)
    m = (pl.program_id(0) * FILT_TILE + lax.broadcasted_iota(jnp.int32, (FILT_TILE, 1), 0)) % blk
    fa_ref[...] = jnp.where(m == 0, pos, pos + neg).astype(BF16)
    fb_ref[...] = jnp.where(m == 0, 0.0, neg - pos).astype(BF16)


def _pad2(a, rows, cols):
    return jnp.pad(a, ((0, rows - a.shape[0]), (0, cols - a.shape[1])))


def _filter_positions(L, blk):
    n_blk = L // blk
    m = np.arange(blk)
    p1, p2 = [], []
    for d in range(-(n_blk - 1), n_blk):
        if d >= 1:
            p1.append(d * blk + m), p2.append(d * blk - m)
        elif d == 0:
            p1.append(m), p2.append(m)
        else:
            p1.append(-d * blk - m), p2.append(-d * blk + m)
    return np.concatenate(p1), np.concatenate(p2)


def _filter_embedding(pos, L):
    t = jnp.asarray(pos, F32) / L
    bands = jnp.linspace(1e-4, HYENA_EMB_BANDS - 1, HYENA_EMB_BANDS, dtype=F32)
    ang = (2 * math.pi) * t[:, None] * bands[None, :]
    return _pad2(jnp.concatenate([t[:, None], jnp.cos(ang), -jnp.sin(ang)], axis=-1), pos.shape[0], LANE)


def _hyena_filters(L, blk, f_w1, f_b1, f_w2, f_b2, f_freq, f_w3, decay):
    n_blk = L // blk
    p1, p2 = _filter_positions(L, blk)
    rows = p1.shape[0]
    tl = FILT_TILE
    tiles_per_lag = blk // tl
    lag = lambda i: i // tiles_per_lag - (n_blk - 1)
    col1 = lambda i, n: 2 * n + jnp.where(lag(i) >= 0, 0, 1)
    col2 = lambda i, n: 2 * n + jnp.where(lag(i) >= 1, 0, 1)
    small = lambda: pl.BlockSpec((LANE, LANE), lambda i, n: (0, 0))
    vec = lambda: pl.BlockSpec((1, LANE), lambda i, n: (0, 0))
    emb = lambda: pl.BlockSpec((tl, LANE), lambda i, n: (i, 0))
    out = pl.BlockSpec((None, tl, D_MODEL), lambda i, n: (n, i, 0))
    w3 = _pad2(f_w3, LANE, f_w3.shape[1])
    return pl.pallas_call(
        functools.partial(_filter_kernel, blk=blk),
        grid=(rows // tl, 2),
        in_specs=[
            emb(), emb(), small(), vec(), small(), vec(), vec(),
            pl.BlockSpec((LANE, D_MODEL), lambda i, n: (0, col1(i, n))),
            pl.BlockSpec((1, D_MODEL), lambda i, n: (0, col1(i, n))),
            pl.BlockSpec((LANE, D_MODEL), lambda i, n: (0, col2(i, n))),
            pl.BlockSpec((1, D_MODEL), lambda i, n: (0, col2(i, n))),
        ],
        out_specs=[out, out],
        out_shape=[jax.ShapeDtypeStruct((2, rows, D_MODEL), BF16)] * 2,
        compiler_params=_cp(("parallel", "parallel")),
        name="hyena_filters",
    )(_filter_embedding(p1, L), _filter_embedding(p2, L), _pad2(f_w1, LANE, LANE), _pad2(f_b1[None], 1, LANE),
      _pad2(f_w2, LANE, LANE), _pad2(f_b2[None], 1, LANE), _pad2(f_freq[None], 1, LANE),
      w3, decay[None], w3, decay[None])


def _dft_mats(L):
    r = int(math.isqrt(L))
    k2 = 2 * jnp.arange(L, dtype=jnp.int32)[:, None] + 1
    n1 = r * jnp.arange(L // r, dtype=jnp.int32)[None, :]
    n2 = jnp.arange(r, dtype=jnp.int32)[None, :]
    sc = math.pi / (2 * L)
    aa = ((k2 * n1) % (4 * L)).astype(F32) * sc
    ab = ((k2 * n2) % (4 * L)).astype(F32) * sc
    ca, sa, cb, sb = jnp.cos(aa)[:, :, None], jnp.sin(aa)[:, :, None], jnp.cos(ab)[:, None, :], jnp.sin(ab)[:, None, :]
    c = (ca * cb - sa * sb).reshape(L, L)
    s = (sa * cb + ca * sb).reshape(L, L)
    return c.astype(BF16), s.astype(BF16), c.T.astype(BF16), s.T.astype(BF16)


def _dft_tiles(L):
    return min(512, L), 512


def _dft_filter_kernel(c_ref, s_ref, a_ref, b_ref, gr_ref, gi_ref):
    gr_ref[...] = _dot(c_ref[...], a_ref[...])
    gi_ref[...] = _dot(s_ref[...], b_ref[...])


def _dft_filter(cm, sm, fa, fb, L):
    tf, tn = _dft_tiles(L)
    n = fa.shape[0]
    mat = lambda: pl.BlockSpec((tf, L), lambda k, c, s: (k, 0))
    rhs = lambda: pl.BlockSpec((None, L, tn), lambda k, c, s: (s, 0, c))
    out = pl.BlockSpec((None, tf, tn), lambda k, c, s: (s, k, c))
    return pl.pallas_call(
        _dft_filter_kernel,
        grid=(L // tf, D_MODEL // tn, n),
        in_specs=[mat(), mat(), rhs(), rhs()],
        out_specs=[out, out],
        out_shape=[jax.ShapeDtypeStruct((n, L, D_MODEL), F32)] * 2,
        compiler_params=_cp(("parallel", "parallel", "parallel")),
        name="hyena_filter_dft",
    )(cm, sm, fa, fb)


FWD_TF = 256


def _dft_fwd_kernel(c_ref, s_ref, z_ref, gr_ref, gi_ref, yr_ref, yi_ref, *, n_blk, blk):
    c, s = c_ref[...], s_ref[...]
    zc, zs = [], []
    for j in range(n_blk):
        zj = z_ref[j * blk:(j + 1) * blk, :]
        zc.append(_dot(c, zj))
        zs.append(_dot(s, zj))
    for i in range(n_blk):
        yr = yi = None
        for j in range(n_blk):
            lag = i - j + n_blk - 1
            gr, gi = gr_ref[lag], gi_ref[lag]
            tr = gr * zc[j] + gi * zs[j]
            ti = gi * zc[j] - gr * zs[j]
            yr = tr if yr is None else yr + tr
            yi = ti if yi is None else yi + ti
        yr_ref[i] = yr.astype(BF16)
        yi_ref[i] = yi.astype(BF16)


def _dft_fwd(cm, sm, z, z_rowblk, z_colblk, gr, gi, order, nb, L, blk):
    n_blk = L // blk
    tf, tn = min(FWD_TF, blk), 512
    mat = lambda: pl.BlockSpec((tf, blk), lambda k, c, b: (k, 0))
    gsp = lambda: pl.BlockSpec((None, 2 * n_blk - 1, tf, tn), lambda k, c, b: (order, 0, k, c))
    out = pl.BlockSpec((None, n_blk, tf, tn), lambda k, c, b: (b, 0, k, c))
    return pl.pallas_call(
        functools.partial(_dft_fwd_kernel, n_blk=n_blk, blk=blk),
        grid=(blk // tf, D_MODEL // tn, nb),
        in_specs=[mat(), mat(),
                  pl.BlockSpec((L, tn), lambda k, c, b: (z_rowblk + b, z_colblk + c)),
                  gsp(), gsp()],
        out_specs=[out, out],
        out_shape=[jax.ShapeDtypeStruct((nb, n_blk, blk, D_MODEL), BF16)] * 2,
        compiler_params=_cp(("parallel", "parallel", "parallel")),
        name="hyena_dft_fwd",
    )(cm, sm, z, gr, gi)


def _dft_inv_kernel(ct_ref, st_ref, yr_ref, yi_ref, z_ref, gt_ref, sk_ref, o_ref, *, inv_len):
    y = (_dot(ct_ref[...], yr_ref[...]) - _dot(st_ref[...], yi_ref[...])) * inv_len
    o_ref[...] = (gt_ref[...].astype(F32) * (y + sk_ref[...] * z_ref[...].astype(F32))).astype(BF16)


def _dft_inv(ctm, stm, yr, yi, z, z_rowblk, z_colblk, gate, g_rowblk, g_colblk, skip, nb, L):
    tt, tn = _dft_tiles(L)
    rpb = L // tt
    mat = lambda: pl.BlockSpec((tt, L), lambda t, c, b: (t, 0))
    spec = lambda: pl.BlockSpec((None, L, tn), lambda t, c, b: (b, 0, c))
    return pl.pallas_call(
        functools.partial(_dft_inv_kernel, inv_len=1.0 / L),
        grid=(rpb, D_MODEL // tn, nb),
        in_specs=[mat(), mat(), spec(), spec(),
                  pl.BlockSpec((tt, tn), lambda t, c, b: (z_rowblk + b * rpb + t, z_colblk + c)),
                  pl.BlockSpec((tt, tn), lambda t, c, b: (g_rowblk + b * rpb + t, g_colblk + c)),
                  pl.BlockSpec((1, tn), lambda t, c, b: (0, c))],
        out_specs=pl.BlockSpec((tt, tn), lambda t, c, b: (b * rpb + t, c)),
        out_shape=jax.ShapeDtypeStruct((nb * L, D_MODEL), BF16),
        compiler_params=_cp(("parallel", "parallel", "parallel")),
        name="hyena_dft_inv",
    )(ctm, stm, yr, yi, z, gate, skip)


def _hyena_stream(u, row0, nb, L, fparams, skip):
    blk = min(HYENA_BLOCK, L)
    n_blk = L // blk
    n_lag = 2 * n_blk - 1
    cm, sm, ctm, stm = _dft_mats(blk)
    fa, fb = _hyena_filters(L, blk, *fparams)
    seg = lambda a: a.reshape(2 * n_lag, blk, D_MODEL)
    gr, gi = _dft_filter(cm, sm, seg(fa), seg(fb), blk)
    gr, gi = (a.reshape(2, n_lag, blk, D_MODEL) for a in (gr, gi))
    tt, tn = _dft_tiles(blk)
    ncb = D_MODEL // tn
    blocks = lambda a: a.reshape(nb * n_blk, blk, D_MODEL)

    yr, yi = _dft_fwd(cm, sm, u, row0 // L, 0, gr, gi, 0, nb, L, blk)
    z1 = _dft_inv(ctm, stm, blocks(yr), blocks(yi), u, row0 // tt, 0, u, row0 // tt, ncb, skip[0:1],
                  nb * n_blk, blk)
    yr, yi = _dft_fwd(cm, sm, z1, 0, 0, gr, gi, 1, nb, L, blk)
    return _dft_inv(ctm, stm, blocks(yr), blocks(yi), z1, 0, 0, u, row0 // tt, 2 * ncb, skip[1:2],
                    nb * n_blk, blk)


def _hyena_mixer(x, g, mod, w_in, b_in, conv_w, conv_b, f_w1, f_b1, f_w2, f_b2, f_freq, f_w3, decay, skip,
                 w_out, b_out):
    u0 = _nm_matmul(x, g, mod, w_in.astype(BF16), b_in.reshape(1, -1), BF16, "hyena_in_proj")
    u = _conv3(u0, conv_w, conv_b)
    fparams = (f_w1, f_b1, f_w2, f_b2, f_freq, f_w3, decay)
    zp = _hyena_stream(u, 0, BATCH, SEQ, fparams, skip)
    zs = _hyena_stream(u, TP, DEC_BATCH, DEC_SEQ, fparams, skip)
    return _resid_matmul(zp, zs, w_out.astype(BF16), b_out.reshape(1, -1), x, mod, "hyena_out_proj")


PREP_TILE = 256


def _rope_tables():
    pos = jnp.arange(DEC_SEQ, dtype=jnp.int32)
    row = (pos // GRID_W).astype(F32)
    col = (pos % GRID_W).astype(F32)
    axis_dim = HEAD_DIM // 2
    inv_freq = ROPE_THETA ** (-jnp.arange(0, axis_dim, 2, dtype=F32) / axis_dim)
    ar = row[:, None] * inv_freq[None, :]
    ac = col[:, None] * inv_freq[None, :]
    cos = jnp.concatenate([jnp.cos(ar), jnp.cos(ar), jnp.cos(ac), jnp.cos(ac)], axis=-1)
    sin = jnp.concatenate([-jnp.sin(ar), jnp.sin(ar), -jnp.sin(ac), jnp.sin(ac)], axis=-1)
    return cos, sin


def _prep_kernel(*refs, use_norm, use_rope, emit_kv):
    it = iter(refs)
    qkv_ref, qn_ref, kn_ref = next(it), next(it), next(it)
    cos_ref = sin_ref = None
    if use_rope:
        cos_ref, sin_ref = next(it), next(it)
    q_ref, k_ref, v_ref = next(it), next(it), next(it)
    nk_ref = nv_ref = None
    if emit_kv:
        nk_ref, nv_ref = next(it), next(it)
    quarter = HEAD_DIM // 4
    if use_rope:
        cos, sin = cos_ref[...], sin_ref[...]
        lane = lax.broadcasted_iota(jnp.int32, (PREP_TILE, HEAD_DIM), 1)
        first = (lane % (2 * quarter)) < quarter

    def head(xh, gn):
        if use_norm:
            xh = xh * lax.rsqrt(jnp.mean(xh * xh, axis=-1, keepdims=True) + NORM_EPS) * gn
        return xh

    def rope(xh):
        if not use_rope:
            return xh
        partner = jnp.where(first, pltpu.roll(xh, HEAD_DIM - quarter, 1), pltpu.roll(xh, quarter, 1))
        return xh * cos + partner * sin

    qn, kn = qn_ref[...], kn_ref[...]
    scale = HEAD_DIM ** -0.5 * LOG2E
    for h in range(N_HEADS):
        hs = slice(h * HEAD_DIM, (h + 1) * HEAD_DIM)
        q_ref[:, hs] = (rope(head(qkv_ref[:, hs], qn)) * scale).astype(BF16)
    for h in range(N_KV_HEADS):
        hs = slice(h * HEAD_DIM, (h + 1) * HEAD_DIM)
        ks = slice(D_MODEL + h * HEAD_DIM, D_MODEL + (h + 1) * HEAD_DIM)
        vs = slice(D_MODEL + KV_DIM + h * HEAD_DIM, D_MODEL + KV_DIM + (h + 1) * HEAD_DIM)
        kh = head(qkv_ref[:, ks], kn)
        vh = qkv_ref[:, vs]
        if emit_kv:
            nk_ref[:, hs] = kh
            nv_ref[:, hs] = vh
        k_ref[:, hs] = rope(kh).astype(BF16)
        v_ref[:, hs] = vh.astype(BF16)


def _attn_prep(qkv, row0, nrows, q_norm, k_norm, use_norm, rope, emit_kv):
    tm = PREP_TILE
    blk0 = row0 // tm
    in_specs = [pl.BlockSpec((tm, QKV_DIM), lambda i: (blk0 + i, 0)),
                pl.BlockSpec((1, HEAD_DIM), lambda i: (0, 0)),
                pl.BlockSpec((1, HEAD_DIM), lambda i: (0, 0))]
    args = [qkv, q_norm.reshape(1, HEAD_DIM), k_norm.reshape(1, HEAD_DIM)]
    if rope is not None:
        tab = lambda: pl.BlockSpec((tm, HEAD_DIM), lambda i: (i % (DEC_SEQ // tm), 0))
        in_specs += [tab(), tab()]
        args += list(rope)
    row = lambda w: pl.BlockSpec((tm, w), lambda i: (i, 0))
    out_specs = [row(D_MODEL), row(KV_DIM), row(KV_DIM)]
    out_shape = [jax.ShapeDtypeStruct((nrows, D_MODEL), BF16), jax.ShapeDtypeStruct((nrows, KV_DIM), BF16),
                 jax.ShapeDtypeStruct((nrows, KV_DIM), BF16)]
    if emit_kv:
        out_specs += [row(KV_DIM), row(KV_DIM)]
        out_shape += [jax.ShapeDtypeStruct((nrows, KV_DIM), F32)] * 2
    return pl.pallas_call(
        functools.partial(_prep_kernel, use_norm=use_norm, use_rope=rope is not None, emit_kv=emit_kv),
        grid=(nrows // tm,),
        in_specs=in_specs, out_specs=out_specs, out_shape=out_shape,
        compiler_params=_cp(("parallel",)),
        name="attn_prep",
    )(*args)


LOG2E = math.log2(math.e)
ATTN_TQ = 256


def _attn_kernel(*refs, tq, seq_len, n_ctx, windowed, has_sink):
    it = iter(refs)
    q_ref, k_ref, v_ref = next(it), next(it), next(it)
    sink_ref = next(it) if has_sink else None
    o_ref = next(it)
    if windowed:
        i = pl.program_id(2)
        span = tq + 2 * WINDOW
        start = pl.multiple_of(jnp.clip(i * tq - WINDOW, 0, seq_len - span), WINDOW)
        qpos = i * tq + lax.broadcasted_iota(jnp.int32, (tq, 1), 0)
        kpos = start + lax.broadcasted_iota(jnp.int32, (1, span), 1)
        segs = [(pl.ds(start, span), jnp.abs(kpos - qpos) <= WINDOW), (pl.ds(seq_len, n_ctx), None)]
    else:
        segs = [(slice(None), None)]
    for h in range(KV_GROUP):
        hs = slice(h * HEAD_DIM, (h + 1) * HEAD_DIM)
        qh = q_ref[:, hs]
        scores = []
        m = None
        for rows, mask in segs:
            s = lax.dot_general(qh, k_ref[rows, :], (((1,), (1,)), ((), ())), preferred_element_type=F32)
            if mask is not None:
                s = jnp.where(mask, s, NEG_INF)
            scores.append(s)
            ms = jnp.max(s, axis=-1, keepdims=True)
            m = ms if m is None else jnp.maximum(m, ms)
        if has_sink:
            sk = sink_ref[pl.program_id(1) * KV_GROUP + h]
            m = jnp.maximum(m, sk)
        l = jnp.exp2(sk - m) if has_sink else jnp.zeros_like(m)
        acc = jnp.zeros((tq, HEAD_DIM), F32)
        for (rows, _), s in zip(segs, scores):
            p = jnp.exp2(s - m)
            l = l + jnp.sum(p, axis=-1, keepdims=True)
            acc = acc + _dot(p.astype(BF16), v_ref[rows, :])
        o_ref[:, hs] = (acc / l).astype(BF16)


def _attention(q, k, v, sink, nb, L, n_ctx, windowed):
    tq = min(ATTN_TQ, L)
    nq = L // tq
    nk = L + n_ctx
    kv = lambda: pl.BlockSpec((None, nk, HEAD_DIM), lambda b, g, i: (b, 0, g))
    in_specs = [pl.BlockSpec((tq, KV_GROUP * HEAD_DIM), lambda b, g, i: (b * nq + i, g)), kv(), kv()]
    args = [q, k, v]
    if sink is not None:
        in_specs.append(pl.BlockSpec(memory_space=pltpu.SMEM))
        args.append(sink.astype(F32) * LOG2E)
    return pl.pallas_call(
        functools.partial(_attn_kernel, tq=tq, seq_len=L, n_ctx=n_ctx, windowed=windowed,
                          has_sink=sink is not None),
        grid=(nb, N_KV_HEADS, nq),
        in_specs=in_specs,
        out_specs=pl.BlockSpec((tq, KV_GROUP * HEAD_DIM), lambda b, g, i: (b * nq + i, g)),
        out_shape=jax.ShapeDtypeStruct((nb * L, D_MODEL), BF16),
        compiler_params=_cp(("parallel", "parallel", "parallel"), 56),
        name="attention",
    )(*args)


def _attn_mixer(x, g, mod, w_qkv, q_norm, k_norm, use_norm, sink, w_o, cache_k, cache_v, windowed, rope):
    zero_b = jnp.zeros((1, QKV_DIM), F32)
    qkv = _nm_matmul(x, g, mod, w_qkv.astype(BF16), zero_b, F32, "qkv_proj")
    qp, kp, vp, new_k, new_v = _attn_prep(qkv, 0, TP, q_norm, k_norm, use_norm, None, True)
    qs, ks, vs = _attn_prep(qkv, TP, TS, q_norm, k_norm, use_norm, rope, False)
    op = _attention(qp, kp.reshape(BATCH, SEQ, KV_DIM), vp.reshape(BATCH, SEQ, KV_DIM), sink, BATCH, SEQ, 0, False)
    kc = cache_k.reshape(DEC_BATCH, PAST_LEN, KV_DIM).astype(BF16)
    vc = cache_v.reshape(DEC_BATCH, PAST_LEN, KV_DIM).astype(BF16)
    k_all = jnp.concatenate([ks.reshape(DEC_BATCH, DEC_SEQ, KV_DIM), kc], axis=1)
    v_all = jnp.concatenate([vs.reshape(DEC_BATCH, DEC_SEQ, KV_DIM), vc], axis=1)
    osm = _attention(qs, k_all, v_all, sink, DEC_BATCH, DEC_SEQ, PAST_LEN, windowed)
    x = _resid_matmul(op, osm, w_o.astype(BF16), jnp.zeros((1, D_MODEL), F32), x, mod, "attn_out_proj")
    shape = (BATCH, SEQ, N_KV_HEADS, HEAD_DIM)
    return x, new_k.reshape(shape), new_v.reshape(shape)


ROUTE_TILE = 512
ROUTE_ROWS = 32


def _router_kernel(x_ref, g_ref, sh_ref, sc_ref, wr_ref, br_ref, xh_ref, rt_ref, cnt_ref, carry):
    tm = ROUTE_TILE
    i = pl.program_id(0)

    @pl.when(i == 0)
    def _():
        carry[...] = jnp.zeros_like(carry)

    h = _norm_mod(x_ref[...], g_ref[...], sh_ref[...], sc_ref[...])
    xh_ref[:, :HALF_D] = _pack_bf16_pairs(h)
    logits = _dot(h.astype(BF16), wr_ref[...])
    s = _sigmoid(logits.T[:N_EXPERTS, :])
    sb = s + br_ref[...]
    u = [s[e:e + 1, :] for e in range(N_EXPERTS)]
    v = [sb[e:e + 1, :] for e in range(N_EXPERTS)]

    gscore = []
    for gq in range(N_EXPERT_GROUPS):
        m = v[4 * gq:4 * gq + 4]
        best = m[PAIR_LO[0]] + m[PAIR_HI[0]]
        for a, b in zip(PAIR_LO[1:], PAIR_HI[1:]):
            best = jnp.maximum(best, m[a] + m[b])
        gscore.append(best)
    gidx = jnp.zeros((1, tm), jnp.int32)
    gbest = gscore[0]
    for gq in range(1, N_EXPERT_GROUPS):
        upd = gscore[gq] > gbest
        gidx = jnp.where(upd, gq, gidx)
        gbest = jnp.where(upd, gscore[gq], gbest)

    def pick(rows, j):
        out = rows[j]
        for gq in range(1, N_EXPERT_GROUPS):
            out = jnp.where(gidx == gq, rows[4 * gq + j], out)
        return out

    vin = [pick(v, j) for j in range(EXPERTS_PER_GROUP)]
    uin = [pick(u, j) for j in range(EXPERTS_PER_GROUP)]
    i1 = jnp.zeros((1, tm), jnp.int32)
    m1 = vin[0]
    for j in range(1, EXPERTS_PER_GROUP):
        upd = vin[j] > m1
        i1 = jnp.where(upd, j, i1)
        m1 = jnp.where(upd, vin[j], m1)
    i2 = jnp.full((1, tm), -1, jnp.int32)
    m2 = jnp.full((1, tm), -jnp.inf, F32)
    for j in range(EXPERTS_PER_GROUP):
        upd = (i1 != j) & (vin[j] > m2)
        i2 = jnp.where(upd, j, i2)
        m2 = jnp.where(upd, vin[j], m2)

    def sel(rows, idx):
        out = rows[0]
        for j in range(1, EXPERTS_PER_GROUP):
            out = jnp.where(idx == j, rows[j], out)
        return out

    w1, w2 = sel(uin, i1), sel(uin, i2)
    wsum = w1 + w2
    w1, w2 = w1 / wsum, w2 / wsum
    first_lo = i1 < i2
    lo = jnp.where(first_lo, i1, i2)
    hi = jnp.where(first_lo, i2, i1)
    w_lo = jnp.where(first_lo, w1, w2)
    w_hi = jnp.where(first_lo, w2, w1)
    pair = jnp.where(lo == 0, hi - 1, jnp.where(lo == 1, hi + 1, 5))
    bucket = gidx * len(PAIR_LO) + pair

    onehot = (lax.broadcasted_iota(jnp.int32, (ROUTE_ROWS, tm), 0) == bucket)
    tri = (lax.broadcasted_iota(jnp.int32, (tm, tm), 0) <= lax.broadcasted_iota(jnp.int32, (tm, tm), 1))
    cum = _dot(jnp.where(onehot, 1.0, 0.0).astype(BF16), jnp.where(tri, 1.0, 0.0).astype(BF16))
    rank = jnp.sum(jnp.where(onehot, cum - 1.0 + carry[...], 0.0), axis=0, keepdims=True)
    carry[...] = carry[...] + cum[:, tm - 1:tm]
    cnt_ref[...] = jnp.broadcast_to(carry[...], (ROUTE_ROWS, LANE))

    rt_ref[...] = jnp.zeros_like(rt_ref)
    rt_ref[0:1, :] = bucket.astype(F32)
    rt_ref[1:2, :] = rank
    wt = jnp.concatenate([w_lo, w_hi, jnp.zeros((LANE - 2, tm), F32)], axis=0)
    xh_ref[:, HALF_D:] = lax.bitcast_convert_type(wt.T, jnp.uint32)


def _router(x, g, mod, w_router, b_router):
    tm = ROUTE_TILE
    wr = _pad2(w_router, D_MODEL, LANE).astype(BF16)
    return pl.pallas_call(
        _router_kernel,
        grid=(T // tm,),
        in_specs=[
            pl.BlockSpec((tm, D_MODEL), lambda i: (i, 0)),
            pl.BlockSpec((1, D_MODEL), lambda i: (0, 0)),
            _mod_spec(tm, 3),
            _mod_spec(tm, 4),
            pl.BlockSpec((D_MODEL, LANE), lambda i: (0, 0)),
            pl.BlockSpec((N_EXPERTS, 1), lambda i: (0, 0)),
        ],
        out_specs=[
            pl.BlockSpec((tm, XH_W), lambda i: (i, 0)),
            pl.BlockSpec((8, tm), lambda i: (0, i)),
            pl.BlockSpec((ROUTE_ROWS, LANE), lambda i: (0, 0)),
        ],
        out_shape=[
            jax.ShapeDtypeStruct((T, XH_W), jnp.uint32),
            jax.ShapeDtypeStruct((8, T), F32),
            jax.ShapeDtypeStruct((ROUTE_ROWS, LANE), F32),
        ],
        scratch_shapes=[pltpu.VMEM((ROUTE_ROWS, 1), F32)],
        compiler_params=_cp(("arbitrary",)),
        name="moe_router",
    )(x, g, mod, mod, wr, b_router.reshape(N_EXPERTS, 1))


DISPATCH_TILE = 256


DMA_UNROLL = 8


def _invert_kernel(dest_ref, src_ref):
    def clear(s, c):
        src_ref[s] = 0
        return c

    def put(t, c):
        src_ref[dest_ref[t]] = t
        return c

    lax.fori_loop(0, T_PAD, clear, 0, unroll=DMA_UNROLL)
    lax.fori_loop(0, T, put, 0, unroll=DMA_UNROLL)


def _invert(dest):
    return pl.pallas_call(
        _invert_kernel,
        in_specs=[pl.BlockSpec(memory_space=pltpu.SMEM)],
        out_specs=pl.BlockSpec(memory_space=pltpu.SMEM),
        out_shape=jax.ShapeDtypeStruct((T_PAD,), jnp.int32),
        name="moe_invert",
    )(dest)


def _gather_rows(idx_ref, base, src_hbm, buf, sem, tm):
    def start(r, c):
        pltpu.make_async_copy(src_hbm.at[pl.ds(idx_ref[base + r], 1)], buf.at[pl.ds(r, 1)], sem).start()
        return c

    lax.fori_loop(0, tm, start, 0, unroll=DMA_UNROLL)


def _wait_rows(src_hbm, buf, sem, tm):
    pltpu.make_async_copy(src_hbm.at[pl.ds(0, tm)], buf, sem).wait()


def _expert_kernel(ea_ref, eb_ref, nv_ref, src_ref, xh_hbm, ga_ref, ua_ref, da_ref, gb_ref, ub_ref, db_ref, y_ref,
                   xbuf0, xbuf1, xbuf2, sems):
    tm = MOE_TILE
    j = pl.program_id(0)
    nv = nv_ref[j]
    bufs = (xbuf0, xbuf1, xbuf2)
    n_buf = len(bufs)

    @pl.when(j == 0)
    def _():
        for t in range(MOE_AHEAD):
            _gather_rows(src_ref, t * tm, xh_hbm, bufs[t], sems.at[t], tm)

    def run(p):
        cur, cur_sem = bufs[p], sems.at[p]
        q = (p + MOE_AHEAD) % n_buf
        ahead, ahead_sem = bufs[q], sems.at[q]

        @pl.when(jnp.logical_or(j < MOE_AHEAD, nv_ref[jnp.maximum(j - MOE_AHEAD, 0)] > 0))
        def _():
            _wait_rows(xh_hbm, cur, cur_sem, tm)

        @pl.when(nv > 0)
        def _():
            valid = lax.broadcasted_iota(jnp.int32, (tm, 1), 0) < nv
            x = jnp.where(valid, _unpack_bf16_pairs(cur[:, :HALF_D]), 0.0).astype(BF16)
            wts = jnp.where(valid, lax.bitcast_convert_type(cur[:, HALF_D:], F32), 0.0)
            for r in range(tm):
                pltpu.make_async_copy(xh_hbm.at[pl.ds(src_ref[(j + MOE_AHEAD) * tm + r], 1)],
                                      ahead.at[pl.ds(r, 1)], ahead_sem).start()

            def ffn(g_ref, u_ref, d_ref, w):
                a = _dot(x, g_ref[...])
                h = a * _sigmoid(a) * _dot(x, u_ref[...]) * w
                return _dot(h.astype(BF16), d_ref[...])

            y = ffn(ga_ref, ua_ref, da_ref, wts[:, 0:1]) + ffn(gb_ref, ub_ref, db_ref, wts[:, 1:2])
            y_ref[...] = _pack_bf16_pairs(y)

    for p in range(n_buf):
        pl.when(j % n_buf == p)(functools.partial(run, p))

    @pl.when(nv == 0)
    def _():
        y_ref[...] = jnp.zeros_like(y_ref)


def _experts(layer, tile_ea, tile_eb, tile_nv, src, xh, w_gate, w_up, w_down):
    tm = MOE_TILE
    up = lambda sel: pl.BlockSpec((None, None, D_MODEL, D_EXPERT),
                                  lambda j, ea, eb, nv, sr: (layer, (ea, eb)[sel][j], 0, 0))
    down = lambda sel: pl.BlockSpec((None, None, D_EXPERT, D_MODEL),
                                    lambda j, ea, eb, nv, sr: (layer, (ea, eb)[sel][j], 0, 0))
    return pl.pallas_call(
        _expert_kernel,
        grid_spec=pltpu.PrefetchScalarGridSpec(
            num_scalar_prefetch=4,
            grid=(MOE_TILES,),
            in_specs=[pl.BlockSpec(memory_space=pl.ANY), up(0), up(0), down(0), up(1), up(1), down(1)],
            out_specs=pl.BlockSpec((tm, HALF_D), lambda j, ea, eb, nv, sr: (j, 0)),
            scratch_shapes=[pltpu.VMEM((tm, XH_W), jnp.uint32)] * (MOE_AHEAD + 1)
            + [pltpu.SemaphoreType.DMA((MOE_AHEAD + 1,))],
        ),
        out_shape=jax.ShapeDtypeStruct((T_PAD, HALF_D), jnp.uint32),
        compiler_params=_cp(("arbitrary",), 56),
        name="moe_experts",
    )(tile_ea, tile_eb, tile_nv, src, xh, w_gate, w_up, w_down, w_gate, w_up, w_down)


def _combine_kernel(dest_ref, x_ref, gt_ref, ys_hbm, o_ref, buf, sems):
    tm = DISPATCH_TILE
    i = pl.program_id(0)
    slot = i % 2

    @pl.when(i == 0)
    def _():
        _gather_rows(dest_ref, 0, ys_hbm, buf.at[0], sems.at[0], tm)

    @pl.when(i + 1 < pl.num_programs(0))
    def _():
        _gather_rows(dest_ref, (i + 1) * tm, ys_hbm, buf.at[1 - slot], sems.at[1 - slot], tm)

    _wait_rows(ys_hbm, buf.at[slot], sems.at[slot], tm)
    o_ref[...] = x_ref[...] + gt_ref[...] * _unpack_bf16_pairs(buf[slot])


def _combine(dest, x, mod, ys):
    tm = DISPATCH_TILE
    return pl.pallas_call(
        _combine_kernel,
        grid_spec=pltpu.PrefetchScalarGridSpec(
            num_scalar_prefetch=1,
            grid=(T // tm,),
            in_specs=[pl.BlockSpec((tm, D_MODEL), lambda i, d: (i, 0)),
                      pl.BlockSpec((None, None, 1, D_MODEL), lambda i, d: (_cond_row(i * tm), 5, 0, 0)),
                      pl.BlockSpec(memory_space=pl.ANY)],
            out_specs=pl.BlockSpec((tm, D_MODEL), lambda i, d: (i, 0)),
            scratch_shapes=[pltpu.VMEM((2, tm, HALF_D), jnp.uint32), pltpu.SemaphoreType.DMA((2,))],
        ),
        out_shape=jax.ShapeDtypeStruct((T, D_MODEL), F32),
        compiler_params=_cp(("arbitrary",)),
        name="moe_combine",
    )(dest, x, mod, ys)


def _lookup(table, idx):
    n = table.shape[0]
    hit = idx[:, None] == jnp.arange(n, dtype=jnp.int32)[None, :]
    return jnp.sum(jnp.where(hit, table[None, :], 0), axis=1)


def _moe_plan(rt, cnt):
    bucket = rt[0].astype(jnp.int32)
    rank = rt[1].astype(jnp.int32)
    counts = cnt[:N_BUCKETS, 0].astype(jnp.int32)
    tiles = (counts + MOE_TILE - 1) // MOE_TILE
    order = jnp.arange(N_BUCKETS, dtype=jnp.int32)
    tile_start = jnp.sum(jnp.where(order[None, :] < order[:, None], tiles[None, :], 0), axis=1)
    tile_end = tile_start + tiles
    n_used = tile_end[N_BUCKETS - 1]
    dest = _lookup(tile_start * MOE_TILE, bucket) + rank
    j = jnp.arange(MOE_TILES, dtype=jnp.int32)
    jc = jnp.minimum(j, n_used - 1)
    b = jnp.minimum(jnp.sum((jc[:, None] >= tile_end[None, :]).astype(jnp.int32), axis=1), N_BUCKETS - 1)
    nv = jnp.clip(_lookup(counts, b) - (j - _lookup(tile_start, b)) * MOE_TILE, 0, MOE_TILE)
    nv = jnp.where(j < n_used, nv, 0)
    n_pairs = len(PAIR_LO)
    ea = (b // n_pairs) * EXPERTS_PER_GROUP + _lookup(jnp.asarray(PAIR_LO, jnp.int32), b % n_pairs)
    eb = (b // n_pairs) * EXPERTS_PER_GROUP + _lookup(jnp.asarray(PAIR_HI, jnp.int32), b % n_pairs)
    return dest, ea, eb, nv


def _moe(layer, x, g, mod, w_router, b_router, w_gate, w_up, w_down):
    xh, rt, cnt = _router(x, g, mod, w_router, b_router)
    dest, ea, eb, nv = _moe_plan(rt, cnt)
    ys = _experts(layer, ea, eb, nv, _invert(dest), xh, w_gate, w_up, w_down)
    return _combine(dest, x, mod, ys)


def _final_norm_kernel(x_ref, g_ref, o_ref):
    x = x_ref[...]
    o_ref[...] = x * lax.rsqrt(jnp.mean(x * x, axis=-1, keepdims=True) + NORM_EPS) * g_ref[...]


def _final_norm(x, g, row0, nrows):
    tm = 512
    blk0 = row0 // tm
    return pl.pallas_call(
        _final_norm_kernel,
        grid=(nrows // tm,),
        in_specs=[pl.BlockSpec((tm, D_MODEL), lambda i: (blk0 + i, 0)), pl.BlockSpec((1, D_MODEL), lambda i: (0, 0))],
        out_specs=pl.BlockSpec((tm, D_MODEL), lambda i: (i, 0)),
        out_shape=jax.ShapeDtypeStruct((nrows, D_MODEL), F32),
        compiler_params=_cp(("parallel",)),
        name="final_norm",
    )(x, g.reshape(1, D_MODEL))


def kernel(x_prompt, x_sample, cache_k_full, cache_v_full, cache_k_win, cache_v_win, c, c_ctx, w_mod, b_mod, norm_mix, norm_ffn, final_norm, pool_w, pool_scale, hy_w_in, hy_b_in, hy_conv_w, hy_conv_b, hy_f_w1, hy_f_b1, hy_f_w2, hy_f_b2, hy_f_freq, hy_f_w3, hy_decay, hy_skip, hy_w_out, hy_b_out, fa_w_qkv, fa_q_norm, fa_k_norm, fa_w_o, wa_w_qkv, wa_sink, wa_w_o, w_router, b_router, moe_w_gate, moe_w_up, moe_w_down):
    x = None
    cond =jnp.concatenate([c_ctx[None, :], c, jnp.zeros((N_COND - 1 - DEC_BATCH, D_MODEL), F32)], axis=0)
    mods = _adaln(cond, w_mod, b_mod).reshape(DEPTH, N_COND, 6, 1, D_MODEL)
    rope = _rope_tables()
    ones_hd = jnp.ones((HEAD_DIM,), F32)
    wg_bf, wu_bf, wd_bf = moe_w_gate.astype(BF16), moe_w_up.astype(BF16), moe_w_down.astype(BF16)
    new_kv = {}
    for layer in range(DEPTH):
        kind = layer % 4
        j = layer // 4
        mod = mods[layer]
        g_mix = norm_mix[layer].reshape(1, D_MODEL)
        if kind == 0:
            assert layer == 0, "the pooling mixer reads the two input streams, so it must be the first layer"
            x = _pool_mixer(x_prompt.reshape(TP, D_MODEL), x_sample.reshape(TS, D_MODEL), g_mix, mod,
                            pool_w[j], pool_scale[j])
        elif kind == 1:
            x = _hyena_mixer(x, g_mix, mod, hy_w_in[j], hy_b_in[j], hy_conv_w[j], hy_conv_b[j], hy_f_w1[j],
                             hy_f_b1[j], hy_f_w2[j], hy_f_b2[j], hy_f_freq[j], hy_f_w3[j], hy_decay[j],
                             hy_skip[j], hy_w_out[j], hy_b_out[j])
        elif kind == 2:
            x, nk, nv = _attn_mixer(x, g_mix, mod, fa_w_qkv[j], fa_q_norm[j], fa_k_norm[j], True, None,
                                    fa_w_o[j], cache_k_full[:, j], cache_v_full[:, j], False, rope)
            new_kv.setdefault("kf", []).append(nk)
            new_kv.setdefault("vf", []).append(nv)
        else:
            x, nk, nv = _attn_mixer(x, g_mix, mod, wa_w_qkv[j], ones_hd, ones_hd, False, wa_sink[j],
                                    wa_w_o[j], cache_k_win[:, j], cache_v_win[:, j], True, rope)
            new_kv.setdefault("kw", []).append(nk)
            new_kv.setdefault("vw", []).append(nv)
        x = _moe(layer, x, norm_ffn[layer].reshape(1, D_MODEL), mod, w_router, b_router, wg_bf, wu_bf, wd_bf)
    y_prompt = _final_norm(x, final_norm, 0, TP).reshape(BATCH, SEQ, D_MODEL)
    y_sample = _final_norm(x, final_norm, TP, TS).reshape(DEC_BATCH, DEC_SEQ, D_MODEL)
    return (y_prompt, y_sample, jnp.stack(new_kv["kf"], axis=1), jnp.stack(new_kv["vf"], axis=1),
            jnp.stack(new_kv["kw"], axis=1), jnp.stack(new_kv["vw"], axis=1))
```

```python
import functools
import math

import jax
import jax.numpy as jnp
import numpy as np
from jax import lax
from jax.experimental import pallas as pl
from jax.experimental.pallas import tpu as pltpu

D_MODEL = 2048
BATCH = 32
SEQ = 256
DEPTH = 4
DEC_BATCH = 4
DEC_SEQ = 4096
PAST_LEN = 512
GRID_W = 64
N_HEADS = 16
N_KV_HEADS = 4
HEAD_DIM = D_MODEL // N_HEADS
KV_GROUP = N_HEADS // N_KV_HEADS
KV_DIM = N_KV_HEADS * HEAD_DIM
QKV_DIM = (N_HEADS + 2 * N_KV_HEADS) * HEAD_DIM
ROPE_THETA = 10000.0
WINDOW = 128
POOL_WINDOWS = (2, 4, 8, 16)
POOL_GROUP = D_MODEL // len(POOL_WINDOWS)
HYENA_EMB_BANDS = 16
HYENA_FILTER_HIDDEN = 64
N_EXPERTS = 16
N_EXPERT_GROUPS = 4
EXPERTS_PER_GROUP = 4
D_EXPERT = 512
NORM_EPS = 1e-6
NEG_INF = -1e30

F32 = jnp.float32
BF16 = jnp.bfloat16

TP = BATCH * SEQ
TS = DEC_BATCH * DEC_SEQ
T = TP + TS
N_COND = 8
LANE = 128
MIB = 1024 * 1024

PAIR_LO = (0, 0, 0, 1, 1, 2)
PAIR_HI = (1, 2, 3, 2, 3, 3)
N_BUCKETS = N_EXPERT_GROUPS * len(PAIR_LO)
MOE_TILE = 256
MOE_AHEAD = 2
MOE_TILES = T // MOE_TILE + N_BUCKETS + MOE_AHEAD
T_PAD = MOE_TILES * MOE_TILE
HALF_D = D_MODEL // 2
XH_W = HALF_D + LANE


def _cp(sem, vmem_mb=48):
    return pltpu.CompilerParams(dimension_semantics=sem, vmem_limit_bytes=vmem_mb * MIB)


def _dot(a, b):
    return jnp.dot(a, b, preferred_element_type=F32)


def _dot3(a, b):
    ah = a.astype(BF16)
    al = (a - ah.astype(F32)).astype(BF16)
    bh = b.astype(BF16)
    bl = (b - bh.astype(F32)).astype(BF16)
    return _dot(ah, bh) + (_dot(al, bh) + _dot(ah, bl))


def _sigmoid(x):
    return 1.0 / (1.0 + jnp.exp(-x))


def _pack_bf16_pairs(x):
    n = x.shape[1] // 2
    bits = lambda v: lax.bitcast_convert_type(v.astype(BF16).astype(F32), jnp.uint32)
    return (bits(x[:, :n]) >> 16) | bits(x[:, n:])


def _unpack_bf16_pairs(u):
    lo = lax.bitcast_convert_type(u << 16, F32)
    hi = lax.bitcast_convert_type(u & jnp.uint32(0xFFFF0000), F32)
    return jnp.concatenate([lo, hi], axis=1)


def _cond_row(r):
    return jnp.where(r < TP, 0, 1 + (r - TP) // DEC_SEQ)


def _mod_spec(tm, chunk, tn=D_MODEL, ncol=False):
    if ncol:
        return pl.BlockSpec((None, None, 1, tn), lambda i, j: (_cond_row(i * tm), chunk, 0, j))
    return pl.BlockSpec((None, None, 1, tn), lambda i, *_: (_cond_row(i * tm), chunk, 0, 0))


def _norm_mod(x, g, shift, scale):
    var = jnp.mean(x * x, axis=-1, keepdims=True)
    y = x * lax.rsqrt(var + NORM_EPS) * g
    return y * (1.0 + scale) + shift


def _adaln_kernel(c_ref, w_ref, b_ref, o_ref):
    c = c_ref[...]
    a = c * _sigmoid(c)
    o_ref[...] = _dot3(a, w_ref[...]) + b_ref[...]


def _adaln(cond, w_mod, b_mod):
    tn = 1024
    n = 6 * D_MODEL
    return pl.pallas_call(
        _adaln_kernel,
        grid=(DEPTH, n // tn),
        in_specs=[
            pl.BlockSpec((N_COND, D_MODEL), lambda l, j: (0, 0)),
            pl.BlockSpec((None, D_MODEL, tn), lambda l, j: (l, 0, j)),
            pl.BlockSpec((None, 1, tn), lambda l, j: (l, 0, j)),
        ],
        out_specs=pl.BlockSpec((None, N_COND, tn), lambda l, j: (l, 0, j)),
        out_shape=jax.ShapeDtypeStruct((DEPTH, N_COND, n), F32),
        compiler_params=_cp(("parallel", "parallel")),
        name="adaln",
    )(cond, w_mod, b_mod.reshape(DEPTH, 1, n))


def _nm_matmul_kernel(x_ref, g_ref, sh_ref, sc_ref, w_ref, b_ref, o_ref, h_scr):
    @pl.when(pl.program_id(1) == 0)
    def _():
        h_scr[...] = _norm_mod(x_ref[...], g_ref[...], sh_ref[...], sc_ref[...]).astype(BF16)

    o_ref[...] = (_dot(h_scr[...], w_ref[...]) + b_ref[...]).astype(o_ref.dtype)


def _nm_matmul(x, g, mod, w, b, out_dtype, name):
    tm, tn = 1024, 1024
    n = w.shape[1]
    return pl.pallas_call(
        _nm_matmul_kernel,
        grid=(T // tm, n // tn),
        in_specs=[
            pl.BlockSpec((tm, D_MODEL), lambda i, j: (i, 0)),
            pl.BlockSpec((1, D_MODEL), lambda i, j: (0, 0)),
            _mod_spec(tm, 0),
            _mod_spec(tm, 1),
            pl.BlockSpec((D_MODEL, tn), lambda i, j: (0, j)),
            pl.BlockSpec((1, tn), lambda i, j: (0, j)),
        ],
        out_specs=pl.BlockSpec((tm, tn), lambda i, j: (i, j)),
        out_shape=jax.ShapeDtypeStruct((T, n), out_dtype),
        scratch_shapes=[pltpu.VMEM((tm, D_MODEL), BF16)],
        compiler_params=_cp(("parallel", "arbitrary")),
        name=name,
    )(x, g, mod, mod, w, b)


RESID_TM = 1024


def _resid_matmul_kernel(ap_ref, as_ref, w_ref, b_ref, x_ref, gt_ref, o_ref):
    def emit(a_ref):
        o_ref[...] = x_ref[...] + gt_ref[...] * (_dot(a_ref[...], w_ref[...]) + b_ref[...])

    is_ctx = pl.program_id(0) < TP // RESID_TM
    pl.when(is_ctx)(lambda: emit(ap_ref))
    pl.when(jnp.logical_not(is_ctx))(lambda: emit(as_ref))


def _resid_matmul(a_ctx, a_lat, w, b, x, mod, name):
    tm, tn = RESID_TM, 1024
    k = a_ctx.shape[1]
    n_ctx = TP // tm
    return pl.pallas_call(
        _resid_matmul_kernel,
        grid=(T // tm, D_MODEL // tn),
        in_specs=[
            pl.BlockSpec((tm, k), lambda i, j: (jnp.minimum(i, n_ctx - 1), 0)),
            pl.BlockSpec((tm, k), lambda i, j: (jnp.maximum(i - n_ctx, 0), 0)),
            pl.BlockSpec((k, tn), lambda i, j: (0, j)),
            pl.BlockSpec((1, tn), lambda i, j: (0, j)),
            pl.BlockSpec((tm, tn), lambda i, j: (i, j)),
            _mod_spec(tm, 2, tn, ncol=True),
        ],
        out_specs=pl.BlockSpec((tm, tn), lambda i, j: (i, j)),
        out_shape=jax.ShapeDtypeStruct((T, D_MODEL), F32),
        compiler_params=_cp(("parallel", "parallel")),
        name=name,
    )(a_ctx, a_lat, w, b, x, mod)


POOL_TILE = 256
POOL_HALO = 8


def _seq_pos(r0):
    is_ctx = r0 < TP
    loc0 = jnp.where(is_ctx, r0 % SEQ, (r0 - TP) % DEC_SEQ)
    seq_len = jnp.where(is_ctx, SEQ, DEC_SEQ)
    return loc0, seq_len


def _pool_kernel(xc_ref, xcp_ref, xcn_ref, xl_ref, xlp_ref, xln_ref, *rest):
    is_ctx = pl.program_id(0) < TP // POOL_TILE
    pl.when(is_ctx)(lambda: _pool_tile(xc_ref, xcp_ref, xcn_ref, *rest))
    pl.when(jnp.logical_not(is_ctx))(lambda: _pool_tile(xl_ref, xlp_ref, xln_ref, *rest))


def _pool_tile(x_ref, xp_ref, xn_ref, g_ref, sh_ref, sc_ref, gt_ref, pw_ref, ps_ref, o_ref, hz_scr):
    tm, hl = POOL_TILE, POOL_HALO
    loc0, seq_len = _seq_pos(pl.program_id(0) * tm)
    has_prev = loc0 > 0
    has_next = loc0 + tm < seq_len
    g, sh, sc = g_ref[...], sh_ref[...], sc_ref[...]
    x = x_ref[...]
    h = _norm_mod(x, g, sh, sc)
    hz_scr[0:hl, :] = jnp.where(has_prev, _norm_mod(xp_ref[...], g, sh, sc), 0.0)
    hz_scr[hl:hl + tm, :] = h
    hz_scr[hl + tm:, :] = jnp.where(has_next, _norm_mod(xn_ref[...], g, sh, sc), 0.0)
    tl = loc0 + lax.broadcasted_iota(jnp.int32, (tm, 1), 0)
    outs = []
    for gi, w in enumerate(POOL_WINDOWS):
        cs = slice(gi * POOL_GROUP, (gi + 1) * POOL_GROUP)
        s = jnp.zeros((tm, POOL_GROUP), F32)
        for off in range(-(w // 2), w - w // 2):
            s = s + hz_scr[hl + off:hl + off + tm, cs]
        lo = jnp.maximum(tl - w // 2, 0)
        hi = jnp.minimum(tl + (w - w // 2), seq_len)
        d = s / (hi - lo).astype(F32) - h[:, cs]
        outs.append(_dot(d.astype(BF16), pw_ref[gi]))
    out = jnp.concatenate(outs, axis=1) * ps_ref[...]
    o_ref[...] = x + gt_ref[...] * out


def _pool_mixer(x_ctx, x_lat, g, mod, pool_w, pool_scale):
    tm, hl = POOL_TILE, POOL_HALO
    r = tm // hl

    def stream(first_tile, rows):
        tile = lambda i: jnp.clip(i - first_tile, 0, rows // tm - 1)
        return [pl.BlockSpec((tm, D_MODEL), lambda i: (tile(i), 0)),
                pl.BlockSpec((hl, D_MODEL), lambda i: (jnp.maximum(tile(i) * r - 1, 0), 0)),
                pl.BlockSpec((hl, D_MODEL), lambda i: (jnp.minimum((tile(i) + 1) * r, rows // hl - 1), 0))]

    return pl.pallas_call(
        _pool_kernel,
        grid=(T // tm,),
        in_specs=stream(0, TP) + stream(TP // tm, TS) + [
            pl.BlockSpec((1, D_MODEL), lambda i: (0, 0)),
            _mod_spec(tm, 0),
            _mod_spec(tm, 1),
            _mod_spec(tm, 2),
            pl.BlockSpec((len(POOL_WINDOWS), POOL_GROUP, POOL_GROUP), lambda i: (0, 0, 0)),
            pl.BlockSpec((1, D_MODEL), lambda i: (0, 0)),
        ],
        out_specs=pl.BlockSpec((tm, D_MODEL), lambda i: (i, 0)),
        out_shape=jax.ShapeDtypeStruct((T, D_MODEL), F32),
        scratch_shapes=[pltpu.VMEM((tm + 2 * hl, D_MODEL), F32)],
        compiler_params=_cp(("parallel",)),
        name="pool_mixer",
    )(x_ctx, x_ctx, x_ctx, x_lat, x_lat, x_lat, g, mod, mod, mod, pool_w.astype(BF16),
      pool_scale.reshape(1, D_MODEL))


CONV_TILE = 256
CONV_HALO = 16


def _conv3_kernel(u_ref, up_ref, un_ref, cw_ref, cb_ref, o_ref, scr):
    tm, hl = CONV_TILE, CONV_HALO
    loc0, seq_len = _seq_pos(pl.program_id(0) * tm)
    has_prev = loc0 > 0
    has_next = loc0 + tm < seq_len
    scr[0:hl, :] = jnp.where(has_prev, up_ref[...].astype(F32), 0.0)
    scr[hl:hl + tm, :] = u_ref[...].astype(F32)
    scr[hl + tm:, :] = jnp.where(has_next, un_ref[...].astype(F32), 0.0)
    out = (scr[hl - 1:hl - 1 + tm, :] * cw_ref[0:1, :] + scr[hl:hl + tm, :] * cw_ref[1:2, :]
           + scr[hl + 1:hl + 1 + tm, :] * cw_ref[2:3, :] + cb_ref[...])
    o_ref[...] = out.astype(o_ref.dtype)


def _conv3(u0, conv_w, conv_b):
    tm, hl, tc = CONV_TILE, CONV_HALO, D_MODEL
    r = tm // hl
    n = u0.shape[1]
    return pl.pallas_call(
        _conv3_kernel,
        grid=(T // tm, n // tc),
        in_specs=[
            pl.BlockSpec((tm, tc), lambda i, j: (i, j)),
            pl.BlockSpec((hl, tc), lambda i, j: (jnp.maximum(i * r - 1, 0), j)),
            pl.BlockSpec((hl, tc), lambda i, j: (jnp.minimum((i + 1) * r, T // hl - 1), j)),
            pl.BlockSpec((3, tc), lambda i, j: (0, j)),
            pl.BlockSpec((1, tc), lambda i, j: (0, j)),
        ],
        out_specs=pl.BlockSpec((tm, tc), lambda i, j: (i, j)),
        out_shape=jax.ShapeDtypeStruct((T, n), BF16),
        scratch_shapes=[pltpu.VMEM((tm + 2 * hl, tc), F32)],
        compiler_params=_cp(("parallel", "parallel")),
        name="hyena_conv3",
    )(u0, u0, u0, conv_w, conv_b.reshape(1, n))


FILT_TILE = 256


HYENA_BLOCK = 1024


T_LANE = LANE - 1


def _filter_mlp_kernel(emb_ref, w1_ref, b1_ref, w2_ref, b2_ref, fr_ref, o_ref):
    emb = emb_ref[...]
    fr = fr_ref[...]
    a = jnp.sin(fr * (_dot3(emb, w1_ref[...]) + b1_ref[...]))
    a = jnp.sin(fr * (_dot3(a, w2_ref[...]) + b2_ref[...]))
    lane = lax.broadcasted_iota(jnp.int32, a.shape, 1)
    o_ref[...] = jnp.where(lane == T_LANE, emb[:, 0:1], a)


def _filter_mlp(pos, L, f_w1, f_b1, f_w2, f_b2, f_freq):
    assert HYENA_FILTER_HIDDEN <= T_LANE
    tl = FILT_TILE
    rows = pos.shape[0]
    small = lambda: pl.BlockSpec((LANE, LANE), lambda i: (0, 0))
    vec = lambda: pl.BlockSpec((1, LANE), lambda i: (0, 0))
    return pl.pallas_call(
        _filter_mlp_kernel,
        grid=(rows // tl,),
        in_specs=[pl.BlockSpec((tl, LANE), lambda i: (i, 0)), small(), vec(), small(), vec(), vec()],
        out_specs=pl.BlockSpec((tl, LANE), lambda i: (i, 0)),
        out_shape=jax.ShapeDtypeStruct((rows, LANE), F32),
        compiler_params=_cp(("parallel",)),
        name="hyena_filter_mlp",
    )(_filter_embedding(pos, L), _pad2(f_w1, LANE, LANE), _pad2(f_b1[None], 1, LANE),
      _pad2(f_w2, LANE, LANE), _pad2(f_b2[None], 1, LANE), _pad2(f_freq[None], 1, LANE))


def _filter_kernel(h1_ref, h2_ref, w3a_ref, dca_ref, w3b_ref, dcb_ref, fa_ref, fb_ref, *, blk):
    def taps(h, w3_ref, dc_ref):
        return _dot3(h, w3_ref[...]) * jnp.exp(-h[:, T_LANE:] * jnp.abs(dc_ref[...]))

    pos = taps(h1_ref[...], w3a_ref, dca_ref)
    neg = taps(h2_ref[...], w3b_ref, dcb_ref)
    m = (pl.program_id(0) * FILT_TILE + lax.broadcasted_iota(jnp.int32, (FILT_TILE, 1), 0)) % blk
    fa_ref[...] = jnp.where(m == 0, pos, pos + neg).astype(BF16)
    fb_ref[...] = jnp.where(m == 0, 0.0, neg - pos).astype(BF16)


def _pad2(a, rows, cols):
    return jnp.pad(a, ((0, rows - a.shape[0]), (0, cols - a.shape[1])))


def _filter_positions(L, blk):
    n_blk = L // blk
    m = np.arange(blk)
    p1, p2 = [], []
    for d in range(-(n_blk - 1), n_blk):
        if d >= 1:
            p1.append(d * blk + m), p2.append(d * blk - m)
        elif d == 0:
            p1.append(m), p2.append(m)
        else:
            p1.append(-d * blk - m), p2.append(-d * blk + m)
    return np.concatenate(p1), np.concatenate(p2)


def _filter_embedding(pos, L):
    t = jnp.asarray(pos, F32) / L
    bands = jnp.linspace(1e-4, HYENA_EMB_BANDS - 1, HYENA_EMB_BANDS, dtype=F32)
    ang = (2 * math.pi) * t[:, None] * bands[None, :]
    return _pad2(jnp.concatenate([t[:, None], jnp.cos(ang), -jnp.sin(ang)], axis=-1), pos.shape[0], LANE)


def _hyena_filters(L, blk, f_w1, f_b1, f_w2, f_b2, f_freq, f_w3, decay):
    n_blk = L // blk
    p1, p2 = _filter_positions(L, blk)
    rows = p1.shape[0]
    tl = FILT_TILE
    tiles_per_lag = blk // tl
    lag = lambda i: i // tiles_per_lag - (n_blk - 1)
    col1 = lambda i, n: 2 * n + jnp.where(lag(i) >= 0, 0, 1)
    col2 = lambda i, n: 2 * n + jnp.where(lag(i) >= 1, 0, 1)
    n_tiles = rows // tl
    out = pl.BlockSpec((None, tl, D_MODEL), lambda i, n: (n, i, 0))
    w3 = _pad2(f_w3, LANE, f_w3.shape[1])
    hidden = _filter_mlp(np.concatenate([p1, p2]), L, f_w1, f_b1, f_w2, f_b2, f_freq)
    return pl.pallas_call(
        functools.partial(_filter_kernel, blk=blk),
        grid=(n_tiles, 2),
        in_specs=[
            pl.BlockSpec((tl, LANE), lambda i, n: (i, 0)),
            pl.BlockSpec((tl, LANE), lambda i, n: (n_tiles + i, 0)),
            pl.BlockSpec((LANE, D_MODEL), lambda i, n: (0, col1(i, n))),
            pl.BlockSpec((1, D_MODEL), lambda i, n: (0, col1(i, n))),
            pl.BlockSpec((LANE, D_MODEL), lambda i, n: (0, col2(i, n))),
            pl.BlockSpec((1, D_MODEL), lambda i, n: (0, col2(i, n))),
        ],
        out_specs=[out, out],
        out_shape=[jax.ShapeDtypeStruct((2, rows, D_MODEL), BF16)] * 2,
        compiler_params=_cp(("parallel", "parallel")),
        name="hyena_filters",
    )(hidden, hidden, w3, decay[None], w3, decay[None])


def _dft_mats(L):
    r = int(math.isqrt(L))
    k2 = 2 * jnp.arange(L, dtype=jnp.int32)[:, None] + 1
    n1 = r * jnp.arange(L // r, dtype=jnp.int32)[None, :]
    n2 = jnp.arange(r, dtype=jnp.int32)[None, :]
    sc = math.pi / (2 * L)
    aa = ((k2 * n1) % (4 * L)).astype(F32) * sc
    ab = ((k2 * n2) % (4 * L)).astype(F32) * sc
    ca, sa, cb, sb = jnp.cos(aa)[:, :, None], jnp.sin(aa)[:, :, None], jnp.cos(ab)[:, None, :], jnp.sin(ab)[:, None, :]
    c = (ca * cb - sa * sb).reshape(L, L)
    s = (sa * cb + ca * sb).reshape(L, L)
    return c.astype(BF16), s.astype(BF16), c.T.astype(BF16), s.T.astype(BF16)


def _dft_tiles(L):
    return min(512, L), 512


def _dft_filter_kernel(c_ref, s_ref, a_ref, b_ref, gr_ref, gi_ref):
    gr_ref[...] = _dot(c_ref[...], a_ref[...])
    gi_ref[...] = _dot(s_ref[...], b_ref[...])


def _dft_filter(cm, sm, fa, fb, L):
    tf, tn = _dft_tiles(L)
    n = fa.shape[0]
    mat = lambda: pl.BlockSpec((tf, L), lambda k, c, s: (k, 0))
    rhs = lambda: pl.BlockSpec((None, L, tn), lambda k, c, s: (s, 0, c))
    out = pl.BlockSpec((None, tf, tn), lambda k, c, s: (s, k, c))
    return pl.pallas_call(
        _dft_filter_kernel,
        grid=(L // tf, D_MODEL // tn, n),
        in_specs=[mat(), mat(), rhs(), rhs()],
        out_specs=[out, out],
        out_shape=[jax.ShapeDtypeStruct((n, L, D_MODEL), F32)] * 2,
        compiler_params=_cp(("parallel", "parallel", "parallel")),
        name="hyena_filter_dft",
    )(cm, sm, fa, fb)


FWD_TF = 256


def _dft_fwd_kernel(c_ref, s_ref, z_ref, gr_ref, gi_ref, yr_ref, yi_ref, *, n_blk, blk):
    c, s = c_ref[...], s_ref[...]
    zc, zs = [], []
    for j in range(n_blk):
        zj = z_ref[j * blk:(j + 1) * blk, :]
        zc.append(_dot(c, zj))
        zs.append(_dot(s, zj))
    for i in range(n_blk):
        yr = yi = None
        for j in range(n_blk):
            lag = i - j + n_blk - 1
            gr, gi = gr_ref[lag], gi_ref[lag]
            tr = gr * zc[j] + gi * zs[j]
            ti = gi * zc[j] - gr * zs[j]
            yr = tr if yr is None else yr + tr
            yi = ti if yi is None else yi + ti
        yr_ref[i] = yr.astype(BF16)
        yi_ref[i] = yi.astype(BF16)


def _dft_fwd(cm, sm, z, z_rowblk, z_colblk, gr, gi, order, nb, L, blk):
    n_blk = L // blk
    tf, tn = min(FWD_TF, blk), 512
    mat = lambda: pl.BlockSpec((tf, blk), lambda k, c, b: (k, 0))
    gsp = lambda: pl.BlockSpec((None, 2 * n_blk - 1, tf, tn), lambda k, c, b: (order, 0, k, c))
    out = pl.BlockSpec((None, n_blk, tf, tn), lambda k, c, b: (b, 0, k, c))
    return pl.pallas_call(
        functools.partial(_dft_fwd_kernel, n_blk=n_blk, blk=blk),
        grid=(blk // tf, D_MODEL // tn, nb),
        in_specs=[mat(), mat(),
                  pl.BlockSpec((L, tn), lambda k, c, b: (z_rowblk + b, z_colblk + c)),
                  gsp(), gsp()],
        out_specs=[out, out],
        out_shape=[jax.ShapeDtypeStruct((nb, n_blk, blk, D_MODEL), BF16)] * 2,
        compiler_params=_cp(("parallel", "parallel", "parallel")),
        name="hyena_dft_fwd",
    )(cm, sm, z, gr, gi)


def _dft_inv_kernel(ct_ref, st_ref, yr_ref, yi_ref, z_ref, gt_ref, sk_ref, o_ref, *, inv_len):
    y = (_dot(ct_ref[...], yr_ref[...]) - _dot(st_ref[...], yi_ref[...])) * inv_len
    o_ref[...] = (gt_ref[...].astype(F32) * (y + sk_ref[...] * z_ref[...].astype(F32))).astype(BF16)


def _dft_inv(ctm, stm, yr, yi, z, z_rowblk, z_colblk, gate, g_rowblk, g_colblk, skip, nb, L):
    tt, tn = _dft_tiles(L)
    rpb = L // tt
    mat = lambda: pl.BlockSpec((tt, L), lambda t, c, b: (t, 0))
    spec = lambda: pl.BlockSpec((None, L, tn), lambda t, c, b: (b, 0, c))
    return pl.pallas_call(
        functools.partial(_dft_inv_kernel, inv_len=1.0 / L),
        grid=(rpb, D_MODEL // tn, nb),
        in_specs=[mat(), mat(), spec(), spec(),
                  pl.BlockSpec((tt, tn), lambda t, c, b: (z_rowblk + b * rpb + t, z_colblk + c)),
                  pl.BlockSpec((tt, tn), lambda t, c, b: (g_rowblk + b * rpb + t, g_colblk + c)),
                  pl.BlockSpec((1, tn), lambda t, c, b: (0, c))],
        out_specs=pl.BlockSpec((tt, tn), lambda t, c, b: (b * rpb + t, c)),
        out_shape=jax.ShapeDtypeStruct((nb * L, D_MODEL), BF16),
        compiler_params=_cp(("parallel", "parallel", "parallel")),
        name="hyena_dft_inv",
    )(ctm, stm, yr, yi, z, gate, skip)


def _hyena_stream(u, row0, nb, L, fparams, skip):
    blk = min(HYENA_BLOCK, L)
    n_blk = L // blk
    n_lag = 2 * n_blk - 1
    cm, sm, ctm, stm = _dft_mats(blk)
    fa, fb = _hyena_filters(L, blk, *fparams)
    seg = lambda a: a.reshape(2 * n_lag, blk, D_MODEL)
    gr, gi = _dft_filter(cm, sm, seg(fa), seg(fb), blk)
    gr, gi = (a.reshape(2, n_lag, blk, D_MODEL) for a in (gr, gi))
    tt, tn = _dft_tiles(blk)
    ncb = D_MODEL // tn
    blocks = lambda a: a.reshape(nb * n_blk, blk, D_MODEL)

    yr, yi = _dft_fwd(cm, sm, u, row0 // L, 0, gr, gi, 0, nb, L, blk)
    z1 = _dft_inv(ctm, stm, blocks(yr), blocks(yi), u, row0 // tt, 0, u, row0 // tt, ncb, skip[0:1],
                  nb * n_blk, blk)
    yr, yi = _dft_fwd(cm, sm, z1, 0, 0, gr, gi, 1, nb, L, blk)
    return _dft_inv(ctm, stm, blocks(yr), blocks(yi), z1, 0, 0, u, row0 // tt, 2 * ncb, skip[1:2],
                    nb * n_blk, blk)


def _hyena_mixer(x, g, mod, w_in, b_in, conv_w, conv_b, f_w1, f_b1, f_w2, f_b2, f_freq, f_w3, decay, skip,
                 w_out, b_out):
    u0 = _nm_matmul(x, g, mod, w_in.astype(BF16), b_in.reshape(1, -1), BF16, "hyena_in_proj")
    u = _conv3(u0, conv_w, conv_b)
    fparams = (f_w1, f_b1, f_w2, f_b2, f_freq, f_w3, decay)
    zp = _hyena_stream(u, 0, BATCH, SEQ, fparams, skip)
    zs = _hyena_stream(u, TP, DEC_BATCH, DEC_SEQ, fparams, skip)
    return _resid_matmul(zp, zs, w_out.astype(BF16), b_out.reshape(1, -1), x, mod, "hyena_out_proj")


def _rope_tables():
    pos = jnp.arange(DEC_SEQ, dtype=jnp.int32)
    row = (pos // GRID_W).astype(F32)
    col = (pos % GRID_W).astype(F32)
    axis_dim = HEAD_DIM // 2
    inv_freq = ROPE_THETA ** (-jnp.arange(0, axis_dim, 2, dtype=F32) / axis_dim)
    ar = row[:, None] * inv_freq[None, :]
    ac = col[:, None] * inv_freq[None, :]
    cos = jnp.concatenate([jnp.cos(ar), jnp.cos(ar), jnp.cos(ac), jnp.cos(ac)], axis=-1)
    sin = jnp.concatenate([-jnp.sin(ar), jnp.sin(ar), -jnp.sin(ac), jnp.sin(ac)], axis=-1)
    return cos, sin


QKV_TM = 512
QKV_TN = 1024
PAIR = 2 * HEAD_DIM


def _qkv_kernel(x_ref, g_ref, sh_ref, sc_ref, w_ref, qn_ref, kn_ref, cos_ref, sin_ref,
                q_ref, k_ref, v_ref, nk_ref, nv_ref, h_scr, *, use_norm):
    tm = QKV_TM
    i, j = pl.program_id(0), pl.program_id(1)

    @pl.when(j == 0)
    def _():
        h_scr[...] = _norm_mod(x_ref[...], g_ref[...], sh_ref[...], sc_ref[...]).astype(BF16)

    quarter = HEAD_DIM // 4
    scale = HEAD_DIM ** -0.5 * LOG2E

    def head(xh, gn):
        if use_norm:
            xh = xh * lax.rsqrt(jnp.mean(xh * xh, axis=-1, keepdims=True) + NORM_EPS) * gn
        return xh

    def rope(xh):
        lane = lax.broadcasted_iota(jnp.int32, (tm, HEAD_DIM), 1)
        first = (lane % (2 * quarter)) < quarter
        partner = jnp.where(first, pltpu.roll(xh, HEAD_DIM - quarter, 1), pltpu.roll(xh, quarter, 1))
        return xh * cos_ref[...] + partner * sin_ref[...]

    def proj(c0):
        return _dot(h_scr[...], w_ref[:, c0:c0 + PAIR])

    def q_tile(latent):
        for p in range(QKV_TN // PAIR):
            acc = proj(p * PAIR)
            for t in range(2):
                xh = head(acc[:, t * HEAD_DIM:(t + 1) * HEAD_DIM], qn_ref[...])
                xh = rope(xh) if latent else xh
                c0 = p * PAIR + t * HEAD_DIM
                q_ref[:, c0:c0 + HEAD_DIM] = (xh * scale).astype(BF16)

    def kv_tile(latent):
        for p in range(KV_DIM // PAIR):
            acc = proj(p * PAIR)
            for t in range(2):
                c0 = p * PAIR + t * HEAD_DIM
                kh = head(acc[:, t * HEAD_DIM:(t + 1) * HEAD_DIM], kn_ref[...])
                if not latent:
                    nk_ref[:, c0:c0 + HEAD_DIM] = kh
                k_ref[:, c0:c0 + HEAD_DIM] = (rope(kh) if latent else kh).astype(BF16)
        for p in range(KV_DIM // PAIR):
            acc = proj(KV_DIM + p * PAIR)
            if not latent:
                nv_ref[:, p * PAIR:(p + 1) * PAIR] = acc
            v_ref[:, p * PAIR:(p + 1) * PAIR] = acc.astype(BF16)

    is_ctx = i < TP // tm
    is_q = j < D_MODEL // QKV_TN
    for latent in (False, True):
        stream = jnp.logical_not(is_ctx) if latent else is_ctx
        pl.when(jnp.logical_and(stream, is_q))(functools.partial(q_tile, latent))
        pl.when(jnp.logical_and(stream, jnp.logical_not(is_q)))(functools.partial(kv_tile, latent))


def _qkv_proj(x, g, mod, w_qkv, q_norm, k_norm, use_norm, rope):
    tm, tn = QKV_TM, QKV_TN
    n_ctx = TP // tm
    n_q = D_MODEL // tn
    tab = lambda: pl.BlockSpec((tm, HEAD_DIM), lambda i, j: (jnp.maximum(i - n_ctx, 0) % (DEC_SEQ // tm), 0))
    kv = lambda: pl.BlockSpec((tm, KV_DIM), lambda i, j: (i, 0))
    new = lambda: pl.BlockSpec((tm, KV_DIM), lambda i, j: (jnp.minimum(i, n_ctx - 1), 0))
    return pl.pallas_call(
        functools.partial(_qkv_kernel, use_norm=use_norm),
        grid=(T // tm, QKV_DIM // tn),
        in_specs=[
            pl.BlockSpec((tm, D_MODEL), lambda i, j: (i, 0)),
            pl.BlockSpec((1, D_MODEL), lambda i, j: (0, 0)),
            _mod_spec(tm, 0),
            _mod_spec(tm, 1),
            pl.BlockSpec((D_MODEL, tn), lambda i, j: (0, j)),
            pl.BlockSpec((1, HEAD_DIM), lambda i, j: (0, 0)),
            pl.BlockSpec((1, HEAD_DIM), lambda i, j: (0, 0)),
            tab(), tab(),
        ],
        out_specs=[pl.BlockSpec((tm, tn), lambda i, j: (i, jnp.minimum(j, n_q - 1))), kv(), kv(), new(), new()],
        out_shape=[jax.ShapeDtypeStruct((T, D_MODEL), BF16), jax.ShapeDtypeStruct((T, KV_DIM), BF16),
                   jax.ShapeDtypeStruct((T, KV_DIM), BF16), jax.ShapeDtypeStruct((TP, KV_DIM), F32),
                   jax.ShapeDtypeStruct((TP, KV_DIM), F32)],
        scratch_shapes=[pltpu.VMEM((tm, D_MODEL), BF16)],
        compiler_params=_cp(("arbitrary", "arbitrary")),
        name="qkv_proj",
    )(x, g, mod, mod, w_qkv, q_norm.reshape(1, HEAD_DIM), k_norm.reshape(1, HEAD_DIM), *rope)


LOG2E = math.log2(math.e)
ATTN_TQ = 256


def _attn_kernel(*refs, tq, seq_len, n_ctx, windowed, has_sink):
    it = iter(refs)
    q_ref, k_ref, v_ref = next(it), next(it), next(it)
    sink_ref = next(it) if has_sink else None
    o_ref = next(it)
    if windowed:
        i = pl.program_id(2)
        span = tq + 2 * WINDOW
        start = pl.multiple_of(jnp.clip(i * tq - WINDOW, 0, seq_len - span), WINDOW)
        qpos = i * tq + lax.broadcasted_iota(jnp.int32, (tq, 1), 0)
        kpos = start + lax.broadcasted_iota(jnp.int32, (1, span), 1)
        segs = [(pl.ds(start, span), jnp.abs(kpos - qpos) <= WINDOW), (pl.ds(seq_len, n_ctx), None)]
    else:
        segs = [(slice(None), None)]
    for h in range(KV_GROUP):
        hs = slice(h * HEAD_DIM, (h + 1) * HEAD_DIM)
        qh = q_ref[:, hs]
        scores = []
        m = None
        for rows, mask in segs:
            s = lax.dot_general(qh, k_ref[rows, :], (((1,), (1,)), ((), ())), preferred_element_type=F32)
            if mask is not None:
                s = jnp.where(mask, s, NEG_INF)
            scores.append(s)
            ms = jnp.max(s, axis=-1, keepdims=True)
            m = ms if m is None else jnp.maximum(m, ms)
        if has_sink:
            sk = sink_ref[pl.program_id(1) * KV_GROUP + h]
            m = jnp.maximum(m, sk)
        l = jnp.exp2(sk - m) if has_sink else jnp.zeros_like(m)
        acc = jnp.zeros((tq, HEAD_DIM), F32)
        for (rows, _), s in zip(segs, scores):
            p = jnp.exp2(s - m)
            l = l + jnp.sum(p, axis=-1, keepdims=True)
            acc = acc + _dot(p.astype(BF16), v_ref[rows, :])
        o_ref[:, hs] = (acc / l).astype(BF16)


def _attention(q, q_row0, k, v, sink, nb, L, n_ctx, windowed):
    tq = min(ATTN_TQ, L)
    nq = L // tq
    nk = L + n_ctx
    q_blk0 = q_row0 // tq
    kv = lambda: pl.BlockSpec((None, nk, HEAD_DIM), lambda b, g, i: (b, 0, g))
    in_specs = [pl.BlockSpec((tq, KV_GROUP * HEAD_DIM), lambda b, g, i: (q_blk0 + b * nq + i, g)), kv(), kv()]
    args = [q, k, v]
    if sink is not None:
        in_specs.append(pl.BlockSpec(memory_space=pltpu.SMEM))
        args.append(sink.astype(F32) * LOG2E)
    return pl.pallas_call(
        functools.partial(_attn_kernel, tq=tq, seq_len=L, n_ctx=n_ctx, windowed=windowed,
                          has_sink=sink is not None),
        grid=(nb, N_KV_HEADS, nq),
        in_specs=in_specs,
        out_specs=pl.BlockSpec((tq, KV_GROUP * HEAD_DIM), lambda b, g, i: (b * nq + i, g)),
        out_shape=jax.ShapeDtypeStruct((nb * L, D_MODEL), BF16),
        compiler_params=_cp(("parallel", "parallel", "parallel"), 56),
        name="attention",
    )(*args)


def _attn_mixer(x, g, mod, w_qkv, q_norm, k_norm, use_norm, sink, w_o, cache_k, cache_v, windowed, rope):
    q, k, v, new_k, new_v = _qkv_proj(x, g, mod, w_qkv.astype(BF16), q_norm, k_norm, use_norm, rope)
    op = _attention(q, 0, k[:TP].reshape(BATCH, SEQ, KV_DIM), v[:TP].reshape(BATCH, SEQ, KV_DIM), sink,
                    BATCH, SEQ, 0, False)
    kc = cache_k.reshape(DEC_BATCH, PAST_LEN, KV_DIM).astype(BF16)
    vc = cache_v.reshape(DEC_BATCH, PAST_LEN, KV_DIM).astype(BF16)
    k_all = jnp.concatenate([k[TP:].reshape(DEC_BATCH, DEC_SEQ, KV_DIM), kc], axis=1)
    v_all = jnp.concatenate([v[TP:].reshape(DEC_BATCH, DEC_SEQ, KV_DIM), vc], axis=1)
    osm = _attention(q, TP, k_all, v_all, sink, DEC_BATCH, DEC_SEQ, PAST_LEN, windowed)
    x = _resid_matmul(op, osm, w_o.astype(BF16), jnp.zeros((1, D_MODEL), F32), x, mod, "attn_out_proj")
    shape = (BATCH, SEQ, N_KV_HEADS, HEAD_DIM)
    return x, new_k.reshape(shape), new_v.reshape(shape)


ROUTE_TILE = 512
ROUTE_ROWS = 32


def _router_kernel(x_ref, g_ref, sh_ref, sc_ref, wr_ref, br_ref, xh_ref, rt_ref, cnt_ref, carry):
    tm = ROUTE_TILE
    i = pl.program_id(0)

    @pl.when(i == 0)
    def _():
        carry[...] = jnp.zeros_like(carry)

    h = _norm_mod(x_ref[...], g_ref[...], sh_ref[...], sc_ref[...])
    xh_ref[:, :HALF_D] = _pack_bf16_pairs(h)
    logits = _dot(h.astype(BF16), wr_ref[...])
    s = _sigmoid(logits.T[:N_EXPERTS, :])
    sb = s + br_ref[...]
    u = [s[e:e + 1, :] for e in range(N_EXPERTS)]
    v = [sb[e:e + 1, :] for e in range(N_EXPERTS)]

    gscore = []
    for gq in range(N_EXPERT_GROUPS):
        m = v[4 * gq:4 * gq + 4]
        best = m[PAIR_LO[0]] + m[PAIR_HI[0]]
        for a, b in zip(PAIR_LO[1:], PAIR_HI[1:]):
            best = jnp.maximum(best, m[a] + m[b])
        gscore.append(best)
    gidx = jnp.zeros((1, tm), jnp.int32)
    gbest = gscore[0]
    for gq in range(1, N_EXPERT_GROUPS):
        upd = gscore[gq] > gbest
        gidx = jnp.where(upd, gq, gidx)
        gbest = jnp.where(upd, gscore[gq], gbest)

    def pick(rows, j):
        out = rows[j]
        for gq in range(1, N_EXPERT_GROUPS):
            out = jnp.where(gidx == gq, rows[4 * gq + j], out)
        return out

    vin = [pick(v, j) for j in range(EXPERTS_PER_GROUP)]
    uin = [pick(u, j) for j in range(EXPERTS_PER_GROUP)]
    i1 = jnp.zeros((1, tm), jnp.int32)
    m1 = vin[0]
    for j in range(1, EXPERTS_PER_GROUP):
        upd = vin[j] > m1
        i1 = jnp.where(upd, j, i1)
        m1 = jnp.where(upd, vin[j], m1)
    i2 = jnp.full((1, tm), -1, jnp.int32)
    m2 = jnp.full((1, tm), -jnp.inf, F32)
    for j in range(EXPERTS_PER_GROUP):
        upd = (i1 != j) & (vin[j] > m2)
        i2 = jnp.where(upd, j, i2)
        m2 = jnp.where(upd, vin[j], m2)

    def sel(rows, idx):
        out = rows[0]
        for j in range(1, EXPERTS_PER_GROUP):
            out = jnp.where(idx == j, rows[j], out)
        return out

    w1, w2 = sel(uin, i1), sel(uin, i2)
    wsum = w1 + w2
    w1, w2 = w1 / wsum, w2 / wsum
    first_lo = i1 < i2
    lo = jnp.where(first_lo, i1, i2)
    hi = jnp.where(first_lo, i2, i1)
    w_lo = jnp.where(first_lo, w1, w2)
    w_hi = jnp.where(first_lo, w2, w1)
    pair = jnp.where(lo == 0, hi - 1, jnp.where(lo == 1, hi + 1, 5))
    bucket = gidx * len(PAIR_LO) + pair

    onehot = (lax.broadcasted_iota(jnp.int32, (ROUTE_ROWS, tm), 0) == bucket)
    tri = (lax.broadcasted_iota(jnp.int32, (tm, tm), 0) <= lax.broadcasted_iota(jnp.int32, (tm, tm), 1))
    cum = _dot(jnp.where(onehot, 1.0, 0.0).astype(BF16), jnp.where(tri, 1.0, 0.0).astype(BF16))
    rank = jnp.sum(jnp.where(onehot, cum - 1.0 + carry[...], 0.0), axis=0, keepdims=True)
    carry[...] = carry[...] + cum[:, tm - 1:tm]
    cnt_ref[...] = jnp.broadcast_to(carry[...], (ROUTE_ROWS, LANE))

    rt_ref[...] = jnp.zeros_like(rt_ref)
    rt_ref[0:1, :] = bucket.astype(F32)
    rt_ref[1:2, :] = rank
    wt = jnp.concatenate([w_lo, w_hi, jnp.zeros((LANE - 2, tm), F32)], axis=0)
    xh_ref[:, HALF_D:] = lax.bitcast_convert_type(wt.T, jnp.uint32)


def _router(x, g, mod, w_router, b_router):
    tm = ROUTE_TILE
    wr = _pad2(w_router, D_MODEL, LANE).astype(BF16)
    return pl.pallas_call(
        _router_kernel,
        grid=(T // tm,),
        in_specs=[
            pl.BlockSpec((tm, D_MODEL), lambda i: (i, 0)),
            pl.BlockSpec((1, D_MODEL), lambda i: (0, 0)),
            _mod_spec(tm, 3),
            _mod_spec(tm, 4),
            pl.BlockSpec((D_MODEL, LANE), lambda i: (0, 0)),
            pl.BlockSpec((N_EXPERTS, 1), lambda i: (0, 0)),
        ],
        out_specs=[
            pl.BlockSpec((tm, XH_W), lambda i: (i, 0)),
            pl.BlockSpec((8, tm), lambda i: (0, i)),
            pl.BlockSpec((ROUTE_ROWS, LANE), lambda i: (0, 0)),
        ],
        out_shape=[
            jax.ShapeDtypeStruct((T, XH_W), jnp.uint32),
            jax.ShapeDtypeStruct((8, T), F32),
            jax.ShapeDtypeStruct((ROUTE_ROWS, LANE), F32),
        ],
        scratch_shapes=[pltpu.VMEM((ROUTE_ROWS, 1), F32)],
        compiler_params=_cp(("arbitrary",)),
        name="moe_router",
    )(x, g, mod, mod, wr, b_router.reshape(N_EXPERTS, 1))


DISPATCH_TILE = 256


DMA_UNROLL = 8


def _invert_kernel(dest_ref, src_ref):
    def clear(s, c):
        src_ref[s] = 0
        return c

    def put(t, c):
        src_ref[dest_ref[t]] = t
        return c

    lax.fori_loop(0, T_PAD, clear, 0, unroll=DMA_UNROLL)
    lax.fori_loop(0, T, put, 0, unroll=DMA_UNROLL)


def _invert(dest):
    return pl.pallas_call(
        _invert_kernel,
        in_specs=[pl.BlockSpec(memory_space=pltpu.SMEM)],
        out_specs=pl.BlockSpec(memory_space=pltpu.SMEM),
        out_shape=jax.ShapeDtypeStruct((T_PAD,), jnp.int32),
        name="moe_invert",
    )(dest)


def _gather_rows(idx_ref, base, src_hbm, buf, sem, tm):
    def start(r, c):
        pltpu.make_async_copy(src_hbm.at[pl.ds(idx_ref[base + r], 1)], buf.at[pl.ds(r, 1)], sem).start()
        return c

    lax.fori_loop(0, tm, start, 0, unroll=DMA_UNROLL)


def _wait_rows(src_hbm, buf, sem, tm):
    pltpu.make_async_copy(src_hbm.at[pl.ds(0, tm)], buf, sem).wait()


def _expert_kernel(ea_ref, eb_ref, nv_ref, src_ref, xh_hbm, ga_ref, ua_ref, da_ref, gb_ref, ub_ref, db_ref, y_ref,
                   xbuf0, xbuf1, xbuf2, sems):
    tm = MOE_TILE
    j = pl.program_id(0)
    nv = nv_ref[j]
    bufs = (xbuf0, xbuf1, xbuf2)
    n_buf = len(bufs)

    @pl.when(j == 0)
    def _():
        for t in range(MOE_AHEAD):
            _gather_rows(src_ref, t * tm, xh_hbm, bufs[t], sems.at[t], tm)

    def run(p):
        cur, cur_sem = bufs[p], sems.at[p]
        q = (p + MOE_AHEAD) % n_buf
        ahead, ahead_sem = bufs[q], sems.at[q]

        @pl.when(jnp.logical_or(j < MOE_AHEAD, nv_ref[jnp.maximum(j - MOE_AHEAD, 0)] > 0))
        def _():
            _wait_rows(xh_hbm, cur, cur_sem, tm)

        @pl.when(nv > 0)
        def _():
            valid = lax.broadcasted_iota(jnp.int32, (tm, 1), 0) < nv
            x = jnp.where(valid, _unpack_bf16_pairs(cur[:, :HALF_D]), 0.0).astype(BF16)
            wts = jnp.where(valid, lax.bitcast_convert_type(cur[:, HALF_D:], F32), 0.0)
            for r in range(tm):
                pltpu.make_async_copy(xh_hbm.at[pl.ds(src_ref[(j + MOE_AHEAD) * tm + r], 1)],
                                      ahead.at[pl.ds(r, 1)], ahead_sem).start()

            def ffn(g_ref, u_ref, d_ref, w):
                a = _dot(x, g_ref[...])
                h = a * _sigmoid(a) * _dot(x, u_ref[...]) * w
                return _dot(h.astype(BF16), d_ref[...])

            y = ffn(ga_ref, ua_ref, da_ref, wts[:, 0:1]) + ffn(gb_ref, ub_ref, db_ref, wts[:, 1:2])
            y_ref[...] = _pack_bf16_pairs(y)

    for p in range(n_buf):
        pl.when(j % n_buf == p)(functools.partial(run, p))

    @pl.when(nv == 0)
    def _():
        y_ref[...] = jnp.zeros_like(y_ref)


def _experts(layer, tile_ea, tile_eb, tile_nv, src, xh, w_gate, w_up, w_down):
    tm = MOE_TILE
    up = lambda sel: pl.BlockSpec((None, None, D_MODEL, D_EXPERT),
                                  lambda j, ea, eb, nv, sr: (layer, (ea, eb)[sel][j], 0, 0))
    down = lambda sel: pl.BlockSpec((None, None, D_EXPERT, D_MODEL),
                                    lambda j, ea, eb, nv, sr: (layer, (ea, eb)[sel][j], 0, 0))
    return pl.pallas_call(
        _expert_kernel,
        grid_spec=pltpu.PrefetchScalarGridSpec(
            num_scalar_prefetch=4,
            grid=(MOE_TILES,),
            in_specs=[pl.BlockSpec(memory_space=pl.ANY), up(0), up(0), down(0), up(1), up(1), down(1)],
            out_specs=pl.BlockSpec((tm, HALF_D), lambda j, ea, eb, nv, sr: (j, 0)),
            scratch_shapes=[pltpu.VMEM((tm, XH_W), jnp.uint32)] * (MOE_AHEAD + 1)
            + [pltpu.SemaphoreType.DMA((MOE_AHEAD + 1,))],
        ),
        out_shape=jax.ShapeDtypeStruct((T_PAD, HALF_D), jnp.uint32),
        compiler_params=_cp(("arbitrary",), 56),
        name="moe_experts",
    )(tile_ea, tile_eb, tile_nv, src, xh, w_gate, w_up, w_down, w_gate, w_up, w_down)


def _combine_kernel(dest_ref, x_ref, gt_ref, ys_hbm, o_ref, buf, sems):
    tm = DISPATCH_TILE
    i = pl.program_id(0)
    slot = i % 2

    @pl.when(i == 0)
    def _():
        _gather_rows(dest_ref, 0, ys_hbm, buf.at[0], sems.at[0], tm)

    @pl.when(i + 1 < pl.num_programs(0))
    def _():
        _gather_rows(dest_ref, (i + 1) * tm, ys_hbm, buf.at[1 - slot], sems.at[1 - slot], tm)

    _wait_rows(ys_hbm, buf.at[slot], sems.at[slot], tm)
    o_ref[...] = x_ref[...] + gt_ref[...] * _unpack_bf16_pairs(buf[slot])


def _combine(dest, x, mod, ys):
    tm = DISPATCH_TILE
    return pl.pallas_call(
        _combine_kernel,
        grid_spec=pltpu.PrefetchScalarGridSpec(
            num_scalar_prefetch=1,
            grid=(T // tm,),
            in_specs=[pl.BlockSpec((tm, D_MODEL), lambda i, d: (i, 0)),
                      pl.BlockSpec((None, None, 1, D_MODEL), lambda i, d: (_cond_row(i * tm), 5, 0, 0)),
                      pl.BlockSpec(memory_space=pl.ANY)],
            out_specs=pl.BlockSpec((tm, D_MODEL), lambda i, d: (i, 0)),
            scratch_shapes=[pltpu.VMEM((2, tm, HALF_D), jnp.uint32), pltpu.SemaphoreType.DMA((2,))],
        ),
        out_shape=jax.ShapeDtypeStruct((T, D_MODEL), F32),
        compiler_params=_cp(("arbitrary",)),
        name="moe_combine",
    )(dest, x, mod, ys)


def _lookup(table, idx):
    n = table.shape[0]
    hit = idx[:, None] == jnp.arange(n, dtype=jnp.int32)[None, :]
    return jnp.sum(jnp.where(hit, table[None, :], 0), axis=1)


def _moe_plan(rt, cnt):
    bucket = rt[0].astype(jnp.int32)
    rank = rt[1].astype(jnp.int32)
    counts = cnt[:N_BUCKETS, 0].astype(jnp.int32)
    tiles = (counts + MOE_TILE - 1) // MOE_TILE
    order = jnp.arange(N_BUCKETS, dtype=jnp.int32)
    tile_start = jnp.sum(jnp.where(order[None, :] < order[:, None], tiles[None, :], 0), axis=1)
    tile_end = tile_start + tiles
    n_used = tile_end[N_BUCKETS - 1]
    dest = _lookup(tile_start * MOE_TILE, bucket) + rank
    j = jnp.arange(MOE_TILES, dtype=jnp.int32)
    jc = jnp.minimum(j, n_used - 1)
    b = jnp.minimum(jnp.sum((jc[:, None] >= tile_end[None, :]).astype(jnp.int32), axis=1), N_BUCKETS - 1)
    nv = jnp.clip(_lookup(counts, b) - (j - _lookup(tile_start, b)) * MOE_TILE, 0, MOE_TILE)
    nv = jnp.where(j < n_used, nv, 0)
    n_pairs = len(PAIR_LO)
    ea = (b // n_pairs) * EXPERTS_PER_GROUP + _lookup(jnp.asarray(PAIR_LO, jnp.int32), b % n_pairs)
    eb = (b // n_pairs) * EXPERTS_PER_GROUP + _lookup(jnp.asarray(PAIR_HI, jnp.int32), b % n_pairs)
    return dest, ea, eb, nv


def _moe(layer, x, g, mod, w_router, b_router, w_gate, w_up, w_down):
    xh, rt, cnt = _router(x, g, mod, w_router, b_router)
    dest, ea, eb, nv = _moe_plan(rt, cnt)
    ys = _experts(layer, ea, eb, nv, _invert(dest), xh, w_gate, w_up, w_down)
    return _combine(dest, x, mod, ys)


def _final_norm_kernel(x_ref, g_ref, o_ref):
    x = x_ref[...]
    o_ref[...] = x * lax.rsqrt(jnp.mean(x * x, axis=-1, keepdims=True) + NORM_EPS) * g_ref[...]


def _final_norm(x, g, row0, nrows):
    tm = 512
    blk0 = row0 // tm
    return pl.pallas_call(
        _final_norm_kernel,
        grid=(nrows // tm,),
        in_specs=[pl.BlockSpec((tm, D_MODEL), lambda i: (blk0 + i, 0)), pl.BlockSpec((1, D_MODEL), lambda i: (0, 0))],
        out_specs=pl.BlockSpec((tm, D_MODEL), lambda i: (i, 0)),
        out_shape=jax.ShapeDtypeStruct((nrows, D_MODEL), F32),
        compiler_params=_cp(("parallel",)),
        name="final_norm",
    )(x, g.reshape(1, D_MODEL))


def kernel(x_prompt, x_sample, cache_k_full, cache_v_full, cache_k_win, cache_v_win, c, c_ctx, w_mod, b_mod, norm_mix, norm_ffn, final_norm, pool_w, pool_scale, hy_w_in, hy_b_in, hy_conv_w, hy_conv_b, hy_f_w1, hy_f_b1, hy_f_w2, hy_f_b2, hy_f_freq, hy_f_w3, hy_decay, hy_skip, hy_w_out, hy_b_out, fa_w_qkv, fa_q_norm, fa_k_norm, fa_w_o, wa_w_qkv, wa_sink, wa_w_o, w_router, b_router, moe_w_gate, moe_w_up, moe_w_down):
    x = None
    cond =jnp.concatenate([c_ctx[None, :], c, jnp.zeros((N_COND - 1 - DEC_BATCH, D_MODEL), F32)], axis=0)
    mods = _adaln(cond, w_mod, b_mod).reshape(DEPTH, N_COND, 6, 1, D_MODEL)
    rope = _rope_tables()
    ones_hd = jnp.ones((HEAD_DIM,), F32)
    wg_bf, wu_bf, wd_bf = moe_w_gate.astype(BF16), moe_w_up.astype(BF16), moe_w_down.astype(BF16)
    new_kv = {}
    for layer in range(DEPTH):
        kind = layer % 4
        j = layer // 4
        mod = mods[layer]
        g_mix = norm_mix[layer].reshape(1, D_MODEL)
        if kind == 0:
            assert layer == 0, "the pooling mixer reads the two input streams, so it must be the first layer"
            x = _pool_mixer(x_prompt.reshape(TP, D_MODEL), x_sample.reshape(TS, D_MODEL), g_mix, mod,
                            pool_w[j], pool_scale[j])
        elif kind == 1:
            x = _hyena_mixer(x, g_mix, mod, hy_w_in[j], hy_b_in[j], hy_conv_w[j], hy_conv_b[j], hy_f_w1[j],
                             hy_f_b1[j], hy_f_w2[j], hy_f_b2[j], hy_f_freq[j], hy_f_w3[j], hy_decay[j],
                             hy_skip[j], hy_w_out[j], hy_b_out[j])
        elif kind == 2:
            x, nk, nv = _attn_mixer(x, g_mix, mod, fa_w_qkv[j], fa_q_norm[j], fa_k_norm[j], True, None,
                                    fa_w_o[j], cache_k_full[:, j], cache_v_full[:, j], False, rope)
            new_kv.setdefault("kf", []).append(nk)
            new_kv.setdefault("vf", []).append(nv)
        else:
            x, nk, nv = _attn_mixer(x, g_mix, mod, wa_w_qkv[j], ones_hd, ones_hd, False, wa_sink[j],
                                    wa_w_o[j], cache_k_win[:, j], cache_v_win[:, j], True, rope)
            new_kv.setdefault("kw", []).append(nk)
            new_kv.setdefault("vw", []).append(nv)
        x = _moe(layer, x, norm_ffn[layer].reshape(1, D_MODEL), mod, w_router, b_router, wg_bf, wu_bf, wd_bf)
    y_prompt = _final_norm(x, final_norm, 0, TP).reshape(BATCH, SEQ, D_MODEL)
    y_sample = _final_norm(x, final_norm, TP, TS).reshape(DEC_BATCH, DEC_SEQ, D_MODEL)
    return (y_prompt, y_sample, jnp.stack(new_kv["kf"], axis=1), jnp.stack(new_kv["vf"], axis=1),
            jnp.stack(new_kv["kw"], axis=1), jnp.stack(new_kv["vw"], axis=1))
```

```python
import functools
import math

import jax
import jax.numpy as jnp
import numpy as np
from jax import lax
from jax.experimental import pallas as pl
from jax.experimental.pallas import tpu as pltpu

D_MODEL = 2048
BATCH = 32
SEQ = 256
DEPTH = 4
DEC_BATCH = 4
DEC_SEQ = 4096
PAST_LEN = 512
GRID_W = 64
N_HEADS = 16
N_KV_HEADS = 4
HEAD_DIM = D_MODEL // N_HEADS
KV_GROUP = N_HEADS // N_KV_HEADS
KV_DIM = N_KV_HEADS * HEAD_DIM
QKV_DIM = (N_HEADS + 2 * N_KV_HEADS) * HEAD_DIM
ROPE_THETA = 10000.0
WINDOW = 128
POOL_WINDOWS = (2, 4, 8, 16)
POOL_GROUP = D_MODEL // len(POOL_WINDOWS)
HYENA_EMB_BANDS = 16
HYENA_FILTER_HIDDEN = 64
N_EXPERTS = 16
N_EXPERT_GROUPS = 4
EXPERTS_PER_GROUP = 4
D_EXPERT = 512
NORM_EPS = 1e-6
NEG_INF = -1e30

F32 = jnp.float32
BF16 = jnp.bfloat16

TP = BATCH * SEQ
TS = DEC_BATCH * DEC_SEQ
T = TP + TS
N_COND = 8
LANE = 128
MIB = 1024 * 1024

PAIR_LO = (0, 0, 0, 1, 1, 2)
PAIR_HI = (1, 2, 3, 2, 3, 3)
N_BUCKETS = N_EXPERT_GROUPS * len(PAIR_LO)
MOE_TILE = 256
MOE_AHEAD = 2
MOE_TILES = T // MOE_TILE + N_BUCKETS + MOE_AHEAD
T_PAD = MOE_TILES * MOE_TILE
HALF_D = D_MODEL // 2
XH_W = HALF_D + LANE


def _cp(sem, vmem_mb=48):
    return pltpu.CompilerParams(dimension_semantics=sem, vmem_limit_bytes=vmem_mb * MIB)


def _dot(a, b):
    return jnp.dot(a, b, preferred_element_type=F32)


def _dot3(a, b):
    ah = a.astype(BF16)
    al = (a - ah.astype(F32)).astype(BF16)
    bh = b.astype(BF16)
    bl = (b - bh.astype(F32)).astype(BF16)
    return _dot(ah, bh) + (_dot(al, bh) + _dot(ah, bl))


def _sigmoid(x):
    return 1.0 / (1.0 + jnp.exp(-x))


def _pack_bf16_pairs(x):
    n = x.shape[1] // 2
    bits = lambda v: lax.bitcast_convert_type(v.astype(BF16).astype(F32), jnp.uint32)
    return (bits(x[:, :n]) >> 16) | bits(x[:, n:])


def _unpack_bf16_pairs(u):
    lo = lax.bitcast_convert_type(u << 16, F32)
    hi = lax.bitcast_convert_type(u & jnp.uint32(0xFFFF0000), F32)
    return jnp.concatenate([lo, hi], axis=1)


def _cond_row(r):
    return jnp.where(r < TP, 0, 1 + (r - TP) // DEC_SEQ)


def _mod_spec(tm, chunk, tn=D_MODEL, ncol=False):
    if ncol:
        return pl.BlockSpec((None, None, 1, tn), lambda i, j: (_cond_row(i * tm), chunk, 0, j))
    return pl.BlockSpec((None, None, 1, tn), lambda i, *_: (_cond_row(i * tm), chunk, 0, 0))


def _norm_mod(x, g, shift, scale):
    var = jnp.mean(x * x, axis=-1, keepdims=True)
    y = x * lax.rsqrt(var + NORM_EPS) * g
    return y * (1.0 + scale) + shift


def _adaln_kernel(c_ref, w_ref, b_ref, o_ref):
    c = c_ref[...]
    a = c * _sigmoid(c)
    o_ref[...] = _dot3(a, w_ref[...]) + b_ref[...]


def _adaln(cond, w_mod, b_mod):
    tn = 1024
    n = 6 * D_MODEL
    return pl.pallas_call(
        _adaln_kernel,
        grid=(DEPTH, n // tn),
        in_specs=[
            pl.BlockSpec((N_COND, D_MODEL), lambda l, j: (0, 0)),
            pl.BlockSpec((None, D_MODEL, tn), lambda l, j: (l, 0, j)),
            pl.BlockSpec((None, 1, tn), lambda l, j: (l, 0, j)),
        ],
        out_specs=pl.BlockSpec((None, N_COND, tn), lambda l, j: (l, 0, j)),
        out_shape=jax.ShapeDtypeStruct((DEPTH, N_COND, n), F32),
        compiler_params=_cp(("parallel", "parallel")),
        name="adaln",
    )(cond, w_mod, b_mod.reshape(DEPTH, 1, n))


def _nm_matmul_kernel(x_ref, g_ref, sh_ref, sc_ref, w_ref, b_ref, o_ref, h_scr):
    @pl.when(pl.program_id(1) == 0)
    def _():
        h_scr[...] = _norm_mod(x_ref[...], g_ref[...], sh_ref[...], sc_ref[...]).astype(BF16)

    o_ref[...] = (_dot(h_scr[...], w_ref[...]) + b_ref[...]).astype(o_ref.dtype)


def _nm_matmul(x, g, mod, w, b, out_dtype, name):
    tm, tn = 1024, 1024
    n = w.shape[1]
    return pl.pallas_call(
        _nm_matmul_kernel,
        grid=(T // tm, n // tn),
        in_specs=[
            pl.BlockSpec((tm, D_MODEL), lambda i, j: (i, 0)),
            pl.BlockSpec((1, D_MODEL), lambda i, j: (0, 0)),
            _mod_spec(tm, 0),
            _mod_spec(tm, 1),
            pl.BlockSpec((D_MODEL, tn), lambda i, j: (0, j)),
            pl.BlockSpec((1, tn), lambda i, j: (0, j)),
        ],
        out_specs=pl.BlockSpec((tm, tn), lambda i, j: (i, j)),
        out_shape=jax.ShapeDtypeStruct((T, n), out_dtype),
        scratch_shapes=[pltpu.VMEM((tm, D_MODEL), BF16)],
        compiler_params=_cp(("parallel", "arbitrary")),
        name=name,
    )(x, g, mod, mod, w, b)


RESID_TM = 1024


def _resid_matmul_kernel(ap_ref, as_ref, w_ref, b_ref, x_ref, gt_ref, o_ref):
    def emit(a_ref):
        o_ref[...] = x_ref[...] + gt_ref[...] * (_dot(a_ref[...], w_ref[...]) + b_ref[...])

    is_ctx = pl.program_id(0) < TP // RESID_TM
    pl.when(is_ctx)(lambda: emit(ap_ref))
    pl.when(jnp.logical_not(is_ctx))(lambda: emit(as_ref))


def _resid_matmul(a_ctx, a_lat, w, b, x, mod, name):
    tm, tn = RESID_TM, 1024
    k = a_ctx.shape[1]
    n_ctx = TP // tm
    return pl.pallas_call(
        _resid_matmul_kernel,
        grid=(T // tm, D_MODEL // tn),
        in_specs=[
            pl.BlockSpec((tm, k), lambda i, j: (jnp.minimum(i, n_ctx - 1), 0)),
            pl.BlockSpec((tm, k), lambda i, j: (jnp.maximum(i - n_ctx, 0), 0)),
            pl.BlockSpec((k, tn), lambda i, j: (0, j)),
            pl.BlockSpec((1, tn), lambda i, j: (0, j)),
            pl.BlockSpec((tm, tn), lambda i, j: (i, j)),
            _mod_spec(tm, 2, tn, ncol=True),
        ],
        out_specs=pl.BlockSpec((tm, tn), lambda i, j: (i, j)),
        out_shape=jax.ShapeDtypeStruct((T, D_MODEL), F32),
        compiler_params=_cp(("parallel", "parallel")),
        name=name,
    )(a_ctx, a_lat, w, b, x, mod)


POOL_TILE = 256
POOL_HALO = 8


def _seq_pos(r0):
    is_ctx = r0 < TP
    loc0 = jnp.where(is_ctx, r0 % SEQ, (r0 - TP) % DEC_SEQ)
    seq_len = jnp.where(is_ctx, SEQ, DEC_SEQ)
    return loc0, seq_len


def _pool_kernel(xc_ref, xcp_ref, xcn_ref, xl_ref, xlp_ref, xln_ref, *rest):
    is_ctx = pl.program_id(0) < TP // POOL_TILE
    pl.when(is_ctx)(lambda: _pool_tile(xc_ref, xcp_ref, xcn_ref, *rest))
    pl.when(jnp.logical_not(is_ctx))(lambda: _pool_tile(xl_ref, xlp_ref, xln_ref, *rest))


def _pool_tile(x_ref, xp_ref, xn_ref, g_ref, sh_ref, sc_ref, gt_ref, pw_ref, ps_ref, o_ref, hz_scr):
    tm, hl = POOL_TILE, POOL_HALO
    loc0, seq_len = _seq_pos(pl.program_id(0) * tm)
    has_prev = loc0 > 0
    has_next = loc0 + tm < seq_len
    g, sh, sc = g_ref[...], sh_ref[...], sc_ref[...]
    x = x_ref[...]
    h = _norm_mod(x, g, sh, sc)
    hz_scr[0:hl, :] = jnp.where(has_prev, _norm_mod(xp_ref[...], g, sh, sc), 0.0)
    hz_scr[hl:hl + tm, :] = h
    hz_scr[hl + tm:, :] = jnp.where(has_next, _norm_mod(xn_ref[...], g, sh, sc), 0.0)
    tl = loc0 + lax.broadcasted_iota(jnp.int32, (tm, 1), 0)
    outs = []
    for gi, w in enumerate(POOL_WINDOWS):
        cs = slice(gi * POOL_GROUP, (gi + 1) * POOL_GROUP)
        s = jnp.zeros((tm, POOL_GROUP), F32)
        for off in range(-(w // 2), w - w // 2):
            s = s + hz_scr[hl + off:hl + off + tm, cs]
        lo = jnp.maximum(tl - w // 2, 0)
        hi = jnp.minimum(tl + (w - w // 2), seq_len)
        d = s / (hi - lo).astype(F32) - h[:, cs]
        outs.append(_dot(d.astype(BF16), pw_ref[gi]))
    out = jnp.concatenate(outs, axis=1) * ps_ref[...]
    o_ref[...] = x + gt_ref[...] * out


def _pool_mixer(x_ctx, x_lat, g, mod, pool_w, pool_scale):
    tm, hl = POOL_TILE, POOL_HALO
    r = tm // hl

    def stream(first_tile, rows):
        tile = lambda i: jnp.clip(i - first_tile, 0, rows // tm - 1)
        return [pl.BlockSpec((tm, D_MODEL), lambda i: (tile(i), 0)),
                pl.BlockSpec((hl, D_MODEL), lambda i: (jnp.maximum(tile(i) * r - 1, 0), 0)),
                pl.BlockSpec((hl, D_MODEL), lambda i: (jnp.minimum((tile(i) + 1) * r, rows // hl - 1), 0))]

    return pl.pallas_call(
        _pool_kernel,
        grid=(T // tm,),
        in_specs=stream(0, TP) + stream(TP // tm, TS) + [
            pl.BlockSpec((1, D_MODEL), lambda i: (0, 0)),
            _mod_spec(tm, 0),
            _mod_spec(tm, 1),
            _mod_spec(tm, 2),
            pl.BlockSpec((len(POOL_WINDOWS), POOL_GROUP, POOL_GROUP), lambda i: (0, 0, 0)),
            pl.BlockSpec((1, D_MODEL), lambda i: (0, 0)),
        ],
        out_specs=pl.BlockSpec((tm, D_MODEL), lambda i: (i, 0)),
        out_shape=jax.ShapeDtypeStruct((T, D_MODEL), F32),
        scratch_shapes=[pltpu.VMEM((tm + 2 * hl, D_MODEL), F32)],
        compiler_params=_cp(("parallel",)),
        name="pool_mixer",
    )(x_ctx, x_ctx, x_ctx, x_lat, x_lat, x_lat, g, mod, mod, mod, pool_w.astype(BF16),
      pool_scale.reshape(1, D_MODEL))


CONV_TILE = 256
CONV_HALO = 16


def _conv3_kernel(u_ref, up_ref, un_ref, cw_ref, cb_ref, o_ref, scr):
    tm, hl = CONV_TILE, CONV_HALO
    loc0, seq_len = _seq_pos(pl.program_id(0) * tm)
    has_prev = loc0 > 0
    has_next = loc0 + tm < seq_len
    scr[0:hl, :] = jnp.where(has_prev, up_ref[...].astype(F32), 0.0)
    scr[hl:hl + tm, :] = u_ref[...].astype(F32)
    scr[hl + tm:, :] = jnp.where(has_next, un_ref[...].astype(F32), 0.0)
    out = (scr[hl - 1:hl - 1 + tm, :] * cw_ref[0:1, :] + scr[hl:hl + tm, :] * cw_ref[1:2, :]
           + scr[hl + 1:hl + 1 + tm, :] * cw_ref[2:3, :] + cb_ref[...])
    o_ref[...] = out.astype(o_ref.dtype)


def _conv3(u0, conv_w, conv_b):
    tm, hl, tc = CONV_TILE, CONV_HALO, D_MODEL
    r = tm // hl
    n = u0.shape[1]
    return pl.pallas_call(
        _conv3_kernel,
        grid=(T // tm, n // tc),
        in_specs=[
            pl.BlockSpec((tm, tc), lambda i, j: (i, j)),
            pl.BlockSpec((hl, tc), lambda i, j: (jnp.maximum(i * r - 1, 0), j)),
            pl.BlockSpec((hl, tc), lambda i, j: (jnp.minimum((i + 1) * r, T // hl - 1), j)),
            pl.BlockSpec((3, tc), lambda i, j: (0, j)),
            pl.BlockSpec((1, tc), lambda i, j: (0, j)),
        ],
        out_specs=pl.BlockSpec((tm, tc), lambda i, j: (i, j)),
        out_shape=jax.ShapeDtypeStruct((T, n), BF16),
        scratch_shapes=[pltpu.VMEM((tm + 2 * hl, tc), F32)],
        compiler_params=_cp(("parallel", "parallel")),
        name="hyena_conv3",
    )(u0, u0, u0, conv_w, conv_b.reshape(1, n))


FILT_TILE = 256


HYENA_BLOCK = 1024


T_LANE = LANE - 1


def _filter_mlp_kernel(emb_ref, w1_ref, b1_ref, w2_ref, b2_ref, fr_ref, o_ref):
    emb = emb_ref[...]
    fr = fr_ref[...]
    a = jnp.sin(fr * (_dot3(emb, w1_ref[...]) + b1_ref[...]))
    a = jnp.sin(fr * (_dot3(a, w2_ref[...]) + b2_ref[...]))
    lane = lax.broadcasted_iota(jnp.int32, a.shape, 1)
    o_ref[...] = jnp.where(lane == T_LANE, emb[:, 0:1], a)


def _filter_mlp(pos, L, f_w1, f_b1, f_w2, f_b2, f_freq):
    assert HYENA_FILTER_HIDDEN <= T_LANE
    tl = FILT_TILE
    rows = pos.shape[0]
    small = lambda: pl.BlockSpec((LANE, LANE), lambda i: (0, 0))
    vec = lambda: pl.BlockSpec((1, LANE), lambda i: (0, 0))
    return pl.pallas_call(
        _filter_mlp_kernel,
        grid=(rows // tl,),
        in_specs=[pl.BlockSpec((tl, LANE), lambda i: (i, 0)), small(), vec(), small(), vec(), vec()],
        out_specs=pl.BlockSpec((tl, LANE), lambda i: (i, 0)),
        out_shape=jax.ShapeDtypeStruct((rows, LANE), F32),
        compiler_params=_cp(("parallel",)),
        name="hyena_filter_mlp",
    )(_filter_embedding(pos, L), _pad2(f_w1, LANE, LANE), _pad2(f_b1[None], 1, LANE),
      _pad2(f_w2, LANE, LANE), _pad2(f_b2[None], 1, LANE), _pad2(f_freq[None], 1, LANE))


def _filter_kernel(h1_ref, h2_ref, w3a_ref, dca_ref, w3b_ref, dcb_ref, fa_ref, fb_ref, *, blk):
    def taps(h, w3_ref, dc_ref):
        return _dot3(h, w3_ref[...]) * jnp.exp(-h[:, T_LANE:] * jnp.abs(dc_ref[...]))

    pos = taps(h1_ref[...], w3a_ref, dca_ref)
    neg = taps(h2_ref[...], w3b_ref, dcb_ref)
    m = (pl.program_id(0) * FILT_TILE + lax.broadcasted_iota(jnp.int32, (FILT_TILE, 1), 0)) % blk
    fa_ref[...] = jnp.where(m == 0, pos, pos + neg).astype(BF16)
    fb_ref[...] = jnp.where(m == 0, 0.0, neg - pos).astype(BF16)


def _pad2(a, rows, cols):
    return jnp.pad(a, ((0, rows - a.shape[0]), (0, cols - a.shape[1])))


def _filter_positions(L, blk):
    n_blk = L // blk
    m = np.arange(blk)
    p1, p2 = [], []
    for d in range(-(n_blk - 1), n_blk):
        if d >= 1:
            p1.append(d * blk + m), p2.append(d * blk - m)
        elif d == 0:
            p1.append(m), p2.append(m)
        else:
            p1.append(-d * blk - m), p2.append(-d * blk + m)
    return np.concatenate(p1), np.concatenate(p2)


def _filter_embedding(pos, L):
    t = jnp.asarray(pos, F32) / L
    bands = jnp.linspace(1e-4, HYENA_EMB_BANDS - 1, HYENA_EMB_BANDS, dtype=F32)
    ang = (2 * math.pi) * t[:, None] * bands[None, :]
    return _pad2(jnp.concatenate([t[:, None], jnp.cos(ang), -jnp.sin(ang)], axis=-1), pos.shape[0], LANE)


def _hyena_filters(L, blk, f_w1, f_b1, f_w2, f_b2, f_freq, f_w3, decay):
    n_blk = L // blk
    p1, p2 = _filter_positions(L, blk)
    rows = p1.shape[0]
    tl = FILT_TILE
    tiles_per_lag = blk // tl
    lag = lambda i: i // tiles_per_lag - (n_blk - 1)
    col1 = lambda i, n: 2 * n + jnp.where(lag(i) >= 0, 0, 1)
    col2 = lambda i, n: 2 * n + jnp.where(lag(i) >= 1, 0, 1)
    n_tiles = rows // tl
    out = pl.BlockSpec((None, tl, D_MODEL), lambda i, n: (n, i, 0))
    w3 = _pad2(f_w3, LANE, f_w3.shape[1])
    hidden = _filter_mlp(np.concatenate([p1, p2]), L, f_w1, f_b1, f_w2, f_b2, f_freq)
    return pl.pallas_call(
        functools.partial(_filter_kernel, blk=blk),
        grid=(n_tiles, 2),
        in_specs=[
            pl.BlockSpec((tl, LANE), lambda i, n: (i, 0)),
            pl.BlockSpec((tl, LANE), lambda i, n: (n_tiles + i, 0)),
            pl.BlockSpec((LANE, D_MODEL), lambda i, n: (0, col1(i, n))),
            pl.BlockSpec((1, D_MODEL), lambda i, n: (0, col1(i, n))),
            pl.BlockSpec((LANE, D_MODEL), lambda i, n: (0, col2(i, n))),
            pl.BlockSpec((1, D_MODEL), lambda i, n: (0, col2(i, n))),
        ],
        out_specs=[out, out],
        out_shape=[jax.ShapeDtypeStruct((2, rows, D_MODEL), BF16)] * 2,
        compiler_params=_cp(("parallel", "parallel")),
        name="hyena_filters",
    )(hidden, hidden, w3, decay[None], w3, decay[None])


def _dft_mats(L):
    r = int(math.isqrt(L))
    k2 = 2 * jnp.arange(L, dtype=jnp.int32)[:, None] + 1
    n1 = r * jnp.arange(L // r, dtype=jnp.int32)[None, :]
    n2 = jnp.arange(r, dtype=jnp.int32)[None, :]
    sc = math.pi / (2 * L)
    aa = ((k2 * n1) % (4 * L)).astype(F32) * sc
    ab = ((k2 * n2) % (4 * L)).astype(F32) * sc
    ca, sa, cb, sb = jnp.cos(aa)[:, :, None], jnp.sin(aa)[:, :, None], jnp.cos(ab)[:, None, :], jnp.sin(ab)[:, None, :]
    c = (ca * cb - sa * sb).reshape(L, L)
    s = (sa * cb + ca * sb).reshape(L, L)
    return c.astype(BF16), s.astype(BF16), c.T.astype(BF16), s.T.astype(BF16)


def _dft_tiles(L):
    return min(512, L), 512


def _dft_filter_kernel(c_ref, s_ref, a_ref, b_ref, gr_ref, gi_ref):
    gr_ref[...] = _dot(c_ref[...], a_ref[...]).astype(gr_ref.dtype)
    gi_ref[...] = _dot(s_ref[...], b_ref[...]).astype(gi_ref.dtype)


def _dft_filter(cm, sm, fa, fb, L):
    tf, tn = _dft_tiles(L)
    n = fa.shape[0]
    mat = lambda: pl.BlockSpec((tf, L), lambda k, c, s: (k, 0))
    rhs = lambda: pl.BlockSpec((None, L, tn), lambda k, c, s: (s, 0, c))
    out = pl.BlockSpec((None, tf, tn), lambda k, c, s: (s, k, c))
    return pl.pallas_call(
        _dft_filter_kernel,
        grid=(L // tf, D_MODEL // tn, n),
        in_specs=[mat(), mat(), rhs(), rhs()],
        out_specs=[out, out],
        out_shape=[jax.ShapeDtypeStruct((n, L, D_MODEL), BF16)] * 2,
        compiler_params=_cp(("parallel", "parallel", "parallel")),
        name="hyena_filter_dft",
    )(cm, sm, fa, fb)


FWD_TF = 256


def _dft_fwd_kernel(c_ref, s_ref, z_ref, gr_ref, gi_ref, yr_ref, yi_ref, *, n_blk, blk, bpb):
    c, s = c_ref[...], s_ref[...]
    for bb in range(bpb):
        zc, zs = [], []
        for j in range(n_blk):
            r = (bb * n_blk + j) * blk
            zj = z_ref[r:r + blk, :]
            zc.append(_dot(c, zj).astype(BF16))
            zs.append(_dot(s, zj).astype(BF16))
        for i in range(n_blk):
            yr = yi = None
            for j in range(n_blk):
                lag = i - j + n_blk - 1
                gr, gi = gr_ref[lag], gi_ref[lag]
                tr = gr * zc[j] + gi * zs[j]
                ti = gi * zc[j] - gr * zs[j]
                yr = tr if yr is None else yr + tr
                yi = ti if yi is None else yi + ti
            yr_ref[bb, i] = yr.astype(BF16)
            yi_ref[bb, i] = yi.astype(BF16)


def _seqs_per_step(L):
    return max(1, 2048 // L)


def _dft_fwd(cm, sm, z, z_rowblk, z_colblk, gr, gi, order, nb, L, blk):
    n_blk = L // blk
    bpb = _seqs_per_step(L)
    assert nb % bpb == 0 and z_rowblk % bpb == 0
    tf, tn = min(FWD_TF, blk), 512
    mat = lambda: pl.BlockSpec((tf, blk), lambda k, c, b: (k, 0))
    gsp = lambda: pl.BlockSpec((None, 2 * n_blk - 1, tf, tn), lambda k, c, b: (order, 0, k, c))
    out = pl.BlockSpec((bpb, n_blk, tf, tn), lambda k, c, b: (b, 0, k, c))
    return pl.pallas_call(
        functools.partial(_dft_fwd_kernel, n_blk=n_blk, blk=blk, bpb=bpb),
        grid=(blk // tf, D_MODEL // tn, nb // bpb),
        in_specs=[mat(), mat(),
                  pl.BlockSpec((bpb * L, tn), lambda k, c, b: (z_rowblk // bpb + b, z_colblk + c)),
                  gsp(), gsp()],
        out_specs=[out, out],
        out_shape=[jax.ShapeDtypeStruct((nb, n_blk, blk, D_MODEL), BF16)] * 2,
        compiler_params=_cp(("parallel", "parallel", "parallel")),
        name="hyena_dft_fwd",
    )(cm, sm, z, gr, gi)


def _dft_inv_kernel(ct_ref, st_ref, yr_ref, yi_ref, z_ref, gt_ref, sk_ref, o_ref, *, inv_len, bpb, tt):
    for bb in range(bpb):
        rows = slice(bb * tt, (bb + 1) * tt)
        y = (_dot(ct_ref[...], yr_ref[bb]) - _dot(st_ref[...], yi_ref[bb])) * inv_len
        o_ref[rows, :] = (gt_ref[rows, :].astype(F32) * (y + sk_ref[...] * z_ref[rows, :].astype(F32))).astype(BF16)


def _dft_inv(ctm, stm, yr, yi, z, z_rowblk, z_colblk, gate, g_rowblk, g_colblk, skip, nb, L):
    tt, tn = _dft_tiles(L)
    rpb = L // tt
    bpb = _seqs_per_step(L) if rpb == 1 else 1
    assert nb % bpb == 0 and z_rowblk % bpb == 0 and g_rowblk % bpb == 0
    mat = lambda: pl.BlockSpec((tt, L), lambda t, c, b: (t, 0))
    spec = lambda: pl.BlockSpec((bpb, L, tn), lambda t, c, b: (b, 0, c))
    rows = lambda blk0: (lambda t, c, b: ((blk0 + b * bpb * rpb + t) // bpb))
    return pl.pallas_call(
        functools.partial(_dft_inv_kernel, inv_len=1.0 / L, bpb=bpb, tt=tt),
        grid=(rpb, D_MODEL // tn, nb // bpb),
        in_specs=[mat(), mat(), spec(), spec(),
                  pl.BlockSpec((bpb * tt, tn), lambda t, c, b: (rows(z_rowblk)(t, c, b), z_colblk + c)),
                  pl.BlockSpec((bpb * tt, tn), lambda t, c, b: (rows(g_rowblk)(t, c, b), g_colblk + c)),
                  pl.BlockSpec((1, tn), lambda t, c, b: (0, c))],
        out_specs=pl.BlockSpec((bpb * tt, tn), lambda t, c, b: (rows(0)(t, c, b), c)),
        out_shape=jax.ShapeDtypeStruct((nb * L, D_MODEL), BF16),
        compiler_params=_cp(("parallel", "parallel", "parallel")),
        name="hyena_dft_inv",
    )(ctm, stm, yr, yi, z, gate, skip)


def _hyena_stream(u, row0, nb, L, fparams, skip):
    blk = min(HYENA_BLOCK, L)
    n_blk = L // blk
    n_lag = 2 * n_blk - 1
    cm, sm, ctm, stm = _dft_mats(blk)
    fa, fb = _hyena_filters(L, blk, *fparams)
    seg = lambda a: a.reshape(2 * n_lag, blk, D_MODEL)
    gr, gi = _dft_filter(cm, sm, seg(fa), seg(fb), blk)
    gr, gi = (a.reshape(2, n_lag, blk, D_MODEL) for a in (gr, gi))
    tt, tn = _dft_tiles(blk)
    ncb = D_MODEL // tn
    blocks = lambda a: a.reshape(nb * n_blk, blk, D_MODEL)

    yr, yi = _dft_fwd(cm, sm, u, row0 // L, 0, gr, gi, 0, nb, L, blk)
    z1 = _dft_inv(ctm, stm, blocks(yr), blocks(yi), u, row0 // tt, 0, u, row0 // tt, ncb, skip[0:1],
                  nb * n_blk, blk)
    yr, yi = _dft_fwd(cm, sm, z1, 0, 0, gr, gi, 1, nb, L, blk)
    return _dft_inv(ctm, stm, blocks(yr), blocks(yi), z1, 0, 0, u, row0 // tt, 2 * ncb, skip[1:2],
                    nb * n_blk, blk)


def _hyena_mixer(x, g, mod, w_in, b_in, conv_w, conv_b, f_w1, f_b1, f_w2, f_b2, f_freq, f_w3, decay, skip,
                 w_out, b_out):
    u0 = _nm_matmul(x, g, mod, w_in.astype(BF16), b_in.reshape(1, -1), BF16, "hyena_in_proj")
    u = _conv3(u0, conv_w, conv_b)
    fparams = (f_w1, f_b1, f_w2, f_b2, f_freq, f_w3, decay)
    zp = _hyena_stream(u, 0, BATCH, SEQ, fparams, skip)
    zs = _hyena_stream(u, TP, DEC_BATCH, DEC_SEQ, fparams, skip)
    return _resid_matmul(zp, zs, w_out.astype(BF16), b_out.reshape(1, -1), x, mod, "hyena_out_proj")


def _rope_tables():
    pos = jnp.arange(DEC_SEQ, dtype=jnp.int32)
    row = (pos // GRID_W).astype(F32)
    col = (pos % GRID_W).astype(F32)
    axis_dim = HEAD_DIM // 2
    inv_freq = ROPE_THETA ** (-jnp.arange(0, axis_dim, 2, dtype=F32) / axis_dim)
    ar = row[:, None] * inv_freq[None, :]
    ac = col[:, None] * inv_freq[None, :]
    cos = jnp.concatenate([jnp.cos(ar), jnp.cos(ar), jnp.cos(ac), jnp.cos(ac)], axis=-1)
    sin = jnp.concatenate([-jnp.sin(ar), jnp.sin(ar), -jnp.sin(ac), jnp.sin(ac)], axis=-1)
    return cos, sin


QKV_TM = 512
QKV_TN = 1024
PAIR = 2 * HEAD_DIM


def _qkv_kernel(x_ref, g_ref, sh_ref, sc_ref, w_ref, qn_ref, kn_ref, cos_ref, sin_ref,
                q_ref, k_ref, v_ref, nk_ref, nv_ref, h_scr, *, use_norm):
    tm = QKV_TM
    i, j = pl.program_id(0), pl.program_id(1)

    @pl.when(j == 0)
    def _():
        h_scr[...] = _norm_mod(x_ref[...], g_ref[...], sh_ref[...], sc_ref[...]).astype(BF16)

    quarter = HEAD_DIM // 4
    scale = HEAD_DIM ** -0.5 * LOG2E

    def head(xh, gn):
        if use_norm:
            xh = xh * lax.rsqrt(jnp.mean(xh * xh, axis=-1, keepdims=True) + NORM_EPS) * gn
        return xh

    def rope(xh):
        lane = lax.broadcasted_iota(jnp.int32, (tm, HEAD_DIM), 1)
        first = (lane % (2 * quarter)) < quarter
        partner = jnp.where(first, pltpu.roll(xh, HEAD_DIM - quarter, 1), pltpu.roll(xh, quarter, 1))
        return xh * cos_ref[...] + partner * sin_ref[...]

    def proj(c0):
        return _dot(h_scr[...], w_ref[:, c0:c0 + PAIR])

    def q_tile(latent):
        for p in range(QKV_TN // PAIR):
            acc = proj(p * PAIR)
            for t in range(2):
                xh = head(acc[:, t * HEAD_DIM:(t + 1) * HEAD_DIM], qn_ref[...])
                xh = rope(xh) if latent else xh
                c0 = p * PAIR + t * HEAD_DIM
                q_ref[:, c0:c0 + HEAD_DIM] = (xh * scale).astype(BF16)

    def kv_tile(latent):
        for p in range(KV_DIM // PAIR):
            acc = proj(p * PAIR)
            for t in range(2):
                c0 = p * PAIR + t * HEAD_DIM
                kh = head(acc[:, t * HEAD_DIM:(t + 1) * HEAD_DIM], kn_ref[...])
                if not latent:
                    nk_ref[:, c0:c0 + HEAD_DIM] = kh
                k_ref[:, c0:c0 + HEAD_DIM] = (rope(kh) if latent else kh).astype(BF16)
        for p in range(KV_DIM // PAIR):
            acc = proj(KV_DIM + p * PAIR)
            if not latent:
                nv_ref[:, p * PAIR:(p + 1) * PAIR] = acc
            v_ref[:, p * PAIR:(p + 1) * PAIR] = acc.astype(BF16)

    is_ctx = i < TP // tm
    is_q = j < D_MODEL // QKV_TN
    for latent in (False, True):
        stream = jnp.logical_not(is_ctx) if latent else is_ctx
        pl.when(jnp.logical_and(stream, is_q))(functools.partial(q_tile, latent))
        pl.when(jnp.logical_and(stream, jnp.logical_not(is_q)))(functools.partial(kv_tile, latent))


def _qkv_proj(x, g, mod, w_qkv, q_norm, k_norm, use_norm, rope):
    tm, tn = QKV_TM, QKV_TN
    n_ctx = TP // tm
    n_q = D_MODEL // tn
    tab = lambda: pl.BlockSpec((tm, HEAD_DIM), lambda i, j: (jnp.maximum(i - n_ctx, 0) % (DEC_SEQ // tm), 0))
    kv = lambda: pl.BlockSpec((tm, KV_DIM), lambda i, j: (i, 0))
    new = lambda: pl.BlockSpec((tm, KV_DIM), lambda i, j: (jnp.minimum(i, n_ctx - 1), 0))
    return pl.pallas_call(
        functools.partial(_qkv_kernel, use_norm=use_norm),
        grid=(T // tm, QKV_DIM // tn),
        in_specs=[
            pl.BlockSpec((tm, D_MODEL), lambda i, j: (i, 0)),
            pl.BlockSpec((1, D_MODEL), lambda i, j: (0, 0)),
            _mod_spec(tm, 0),
            _mod_spec(tm, 1),
            pl.BlockSpec((D_MODEL, tn), lambda i, j: (0, j)),
            pl.BlockSpec((1, HEAD_DIM), lambda i, j: (0, 0)),
            pl.BlockSpec((1, HEAD_DIM), lambda i, j: (0, 0)),
            tab(), tab(),
        ],
        out_specs=[pl.BlockSpec((tm, tn), lambda i, j: (i, jnp.minimum(j, n_q - 1))), kv(), kv(), new(), new()],
        out_shape=[jax.ShapeDtypeStruct((T, D_MODEL), BF16), jax.ShapeDtypeStruct((T, KV_DIM), BF16),
                   jax.ShapeDtypeStruct((T, KV_DIM), BF16), jax.ShapeDtypeStruct((TP, KV_DIM), F32),
                   jax.ShapeDtypeStruct((TP, KV_DIM), F32)],
        scratch_shapes=[pltpu.VMEM((tm, D_MODEL), BF16)],
        compiler_params=_cp(("arbitrary", "arbitrary")),
        name="qkv_proj",
    )(x, g, mod, mod, w_qkv, q_norm.reshape(1, HEAD_DIM), k_norm.reshape(1, HEAD_DIM), *rope)


LOG2E = math.log2(math.e)
ATTN_TQ = 256


def _attn_kernel(*refs, tq, seq_len, n_ctx, windowed, has_sink):
    it = iter(refs)
    q_ref, k_ref, v_ref = next(it), next(it), next(it)
    sink_ref = next(it) if has_sink else None
    o_ref = next(it)
    if windowed:
        i = pl.program_id(2)
        span = tq + 2 * WINDOW
        start = pl.multiple_of(jnp.clip(i * tq - WINDOW, 0, seq_len - span), WINDOW)
        qpos = i * tq + lax.broadcasted_iota(jnp.int32, (tq, 1), 0)
        kpos = start + lax.broadcasted_iota(jnp.int32, (1, span), 1)
        segs = [(pl.ds(start, span), jnp.abs(kpos - qpos) <= WINDOW), (pl.ds(seq_len, n_ctx), None)]
    else:
        segs = [(slice(None), None)]
    for h in range(KV_GROUP):
        hs = slice(h * HEAD_DIM, (h + 1) * HEAD_DIM)
        qh = q_ref[:, hs]
        scores = []
        m = None
        for rows, mask in segs:
            s = lax.dot_general(qh, k_ref[rows, :], (((1,), (1,)), ((), ())), preferred_element_type=F32)
            if mask is not None:
                s = jnp.where(mask, s, NEG_INF)
            scores.append(s)
            ms = jnp.max(s, axis=-1, keepdims=True)
            m = ms if m is None else jnp.maximum(m, ms)
        if has_sink:
            sk = sink_ref[pl.program_id(1) * KV_GROUP + h]
            m = jnp.maximum(m, sk)
        l = jnp.exp2(sk - m) if has_sink else jnp.zeros_like(m)
        acc = jnp.zeros((tq, HEAD_DIM), F32)
        for (rows, _), s in zip(segs, scores):
            p = jnp.exp2(s - m)
            l = l + jnp.sum(p, axis=-1, keepdims=True)
            acc = acc + _dot(p.astype(BF16), v_ref[rows, :])
        o_ref[:, hs] = (acc / l).astype(BF16)


def _attention(q, q_row0, k, v, sink, nb, L, n_ctx, windowed):
    tq = min(ATTN_TQ, L)
    nq = L // tq
    nk = L + n_ctx
    q_blk0 = q_row0 // tq
    kv = lambda: pl.BlockSpec((None, nk, HEAD_DIM), lambda b, g, i: (b, 0, g))
    in_specs = [pl.BlockSpec((tq, KV_GROUP * HEAD_DIM), lambda b, g, i: (q_blk0 + b * nq + i, g)), kv(), kv()]
    args = [q, k, v]
    if sink is not None:
        in_specs.append(pl.BlockSpec(memory_space=pltpu.SMEM))
        args.append(sink.astype(F32) * LOG2E)
    return pl.pallas_call(
        functools.partial(_attn_kernel, tq=tq, seq_len=L, n_ctx=n_ctx, windowed=windowed,
                          has_sink=sink is not None),
        grid=(nb, N_KV_HEADS, nq),
        in_specs=in_specs,
        out_specs=pl.BlockSpec((tq, KV_GROUP * HEAD_DIM), lambda b, g, i: (b * nq + i, g)),
        out_shape=jax.ShapeDtypeStruct((nb * L, D_MODEL), BF16),
        compiler_params=_cp(("parallel", "parallel", "parallel"), 56),
        name="attention",
    )(*args)


def _attn_mixer(x, g, mod, w_qkv, q_norm, k_norm, use_norm, sink, w_o, cache_k, cache_v, windowed, rope):
    q, k, v, new_k, new_v = _qkv_proj(x, g, mod, w_qkv.astype(BF16), q_norm, k_norm, use_norm, rope)
    op = _attention(q, 0, k[:TP].reshape(BATCH, SEQ, KV_DIM), v[:TP].reshape(BATCH, SEQ, KV_DIM), sink,
                    BATCH, SEQ, 0, False)
    kc = cache_k.reshape(DEC_BATCH, PAST_LEN, KV_DIM).astype(BF16)
    vc = cache_v.reshape(DEC_BATCH, PAST_LEN, KV_DIM).astype(BF16)
    k_all = jnp.concatenate([k[TP:].reshape(DEC_BATCH, DEC_SEQ, KV_DIM), kc], axis=1)
    v_all = jnp.concatenate([v[TP:].reshape(DEC_BATCH, DEC_SEQ, KV_DIM), vc], axis=1)
    osm = _attention(q, TP, k_all, v_all, sink, DEC_BATCH, DEC_SEQ, PAST_LEN, windowed)
    x = _resid_matmul(op, osm, w_o.astype(BF16), jnp.zeros((1, D_MODEL), F32), x, mod, "attn_out_proj")
    shape = (BATCH, SEQ, N_KV_HEADS, HEAD_DIM)
    return x, new_k.reshape(shape), new_v.reshape(shape)


ROUTE_TILE = 512
ROUTE_ROWS = 32


def _router_kernel(x_ref, g_ref, sh_ref, sc_ref, wr_ref, br_ref, xh_ref, rt_ref, cnt_ref, carry):
    tm = ROUTE_TILE
    i = pl.program_id(0)

    @pl.when(i == 0)
    def _():
        carry[...] = jnp.zeros_like(carry)

    h = _norm_mod(x_ref[...], g_ref[...], sh_ref[...], sc_ref[...])
    xh_ref[:, :HALF_D] = _pack_bf16_pairs(h)
    logits = _dot(h.astype(BF16), wr_ref[...])
    s = _sigmoid(logits.T[:N_EXPERTS, :])
    sb = s + br_ref[...]
    u = [s[e:e + 1, :] for e in range(N_EXPERTS)]
    v = [sb[e:e + 1, :] for e in range(N_EXPERTS)]

    gscore = []
    for gq in range(N_EXPERT_GROUPS):
        m = v[4 * gq:4 * gq + 4]
        best = m[PAIR_LO[0]] + m[PAIR_HI[0]]
        for a, b in zip(PAIR_LO[1:], PAIR_HI[1:]):
            best = jnp.maximum(best, m[a] + m[b])
        gscore.append(best)
    gidx = jnp.zeros((1, tm), jnp.int32)
    gbest = gscore[0]
    for gq in range(1, N_EXPERT_GROUPS):
        upd = gscore[gq] > gbest
        gidx = jnp.where(upd, gq, gidx)
        gbest = jnp.where(upd, gscore[gq], gbest)

    def pick(rows, j):
        out = rows[j]
        for gq in range(1, N_EXPERT_GROUPS):
            out = jnp.where(gidx == gq, rows[4 * gq + j], out)
        return out

    vin = [pick(v, j) for j in range(EXPERTS_PER_GROUP)]
    uin = [pick(u, j) for j in range(EXPERTS_PER_GROUP)]
    i1 = jnp.zeros((1, tm), jnp.int32)
    m1 = vin[0]
    for j in range(1, EXPERTS_PER_GROUP):
        upd = vin[j] > m1
        i1 = jnp.where(upd, j, i1)
        m1 = jnp.where(upd, vin[j], m1)
    i2 = jnp.full((1, tm), -1, jnp.int32)
    m2 = jnp.full((1, tm), -jnp.inf, F32)
    for j in range(EXPERTS_PER_GROUP):
        upd = (i1 != j) & (vin[j] > m2)
        i2 = jnp.where(upd, j, i2)
        m2 = jnp.where(upd, vin[j], m2)

    def sel(rows, idx):
        out = rows[0]
        for j in range(1, EXPERTS_PER_GROUP):
            out = jnp.where(idx == j, rows[j], out)
        return out

    w1, w2 = sel(uin, i1), sel(uin, i2)
    wsum = w1 + w2
    w1, w2 = w1 / wsum, w2 / wsum
    first_lo = i1 < i2
    lo = jnp.where(first_lo, i1, i2)
    hi = jnp.where(first_lo, i2, i1)
    w_lo = jnp.where(first_lo, w1, w2)
    w_hi = jnp.where(first_lo, w2, w1)
    pair = jnp.where(lo == 0, hi - 1, jnp.where(lo == 1, hi + 1, 5))
    bucket = gidx * len(PAIR_LO) + pair

    onehot = (lax.broadcasted_iota(jnp.int32, (ROUTE_ROWS, tm), 0) == bucket)
    tri = (lax.broadcasted_iota(jnp.int32, (tm, tm), 0) <= lax.broadcasted_iota(jnp.int32, (tm, tm), 1))
    cum = _dot(jnp.where(onehot, 1.0, 0.0).astype(BF16), jnp.where(tri, 1.0, 0.0).astype(BF16))
    rank = jnp.sum(jnp.where(onehot, cum - 1.0 + carry[...], 0.0), axis=0, keepdims=True)
    carry[...] = carry[...] + cum[:, tm - 1:tm]
    cnt_ref[...] = jnp.broadcast_to(carry[...], (ROUTE_ROWS, LANE))

    rt_ref[...] = jnp.zeros_like(rt_ref)
    rt_ref[0:1, :] = bucket.astype(F32)
    rt_ref[1:2, :] = rank
    wt = jnp.concatenate([w_lo, w_hi, jnp.zeros((LANE - 2, tm), F32)], axis=0)
    xh_ref[:, HALF_D:] = lax.bitcast_convert_type(wt.T, jnp.uint32)


def _router(x, g, mod, w_router, b_router):
    tm = ROUTE_TILE
    wr = _pad2(w_router, D_MODEL, LANE).astype(BF16)
    return pl.pallas_call(
        _router_kernel,
        grid=(T // tm,),
        in_specs=[
            pl.BlockSpec((tm, D_MODEL), lambda i: (i, 0)),
            pl.BlockSpec((1, D_MODEL), lambda i: (0, 0)),
            _mod_spec(tm, 3),
            _mod_spec(tm, 4),
            pl.BlockSpec((D_MODEL, LANE), lambda i: (0, 0)),
            pl.BlockSpec((N_EXPERTS, 1), lambda i: (0, 0)),
        ],
        out_specs=[
            pl.BlockSpec((tm, XH_W), lambda i: (i, 0)),
            pl.BlockSpec((8, tm), lambda i: (0, i)),
            pl.BlockSpec((ROUTE_ROWS, LANE), lambda i: (0, 0)),
        ],
        out_shape=[
            jax.ShapeDtypeStruct((T, XH_W), jnp.uint32),
            jax.ShapeDtypeStruct((8, T), F32),
            jax.ShapeDtypeStruct((ROUTE_ROWS, LANE), F32),
        ],
        scratch_shapes=[pltpu.VMEM((ROUTE_ROWS, 1), F32)],
        compiler_params=_cp(("arbitrary",)),
        name="moe_router",
    )(x, g, mod, mod, wr, b_router.reshape(N_EXPERTS, 1))


DISPATCH_TILE = 256


DMA_UNROLL = 8


def _invert_kernel(dest_ref, src_ref):
    def clear(s, c):
        src_ref[s] = 0
        return c

    def put(t, c):
        src_ref[dest_ref[t]] = t
        return c

    lax.fori_loop(0, T_PAD, clear, 0, unroll=DMA_UNROLL)
    lax.fori_loop(0, T, put, 0, unroll=DMA_UNROLL)


def _invert(dest):
    return pl.pallas_call(
        _invert_kernel,
        in_specs=[pl.BlockSpec(memory_space=pltpu.SMEM)],
        out_specs=pl.BlockSpec(memory_space=pltpu.SMEM),
        out_shape=jax.ShapeDtypeStruct((T_PAD,), jnp.int32),
        name="moe_invert",
    )(dest)


def _gather_rows(idx_ref, base, src_hbm, buf, sem, tm):
    def start(r, c):
        pltpu.make_async_copy(src_hbm.at[pl.ds(idx_ref[base + r], 1)], buf.at[pl.ds(r, 1)], sem).start()
        return c

    lax.fori_loop(0, tm, start, 0, unroll=DMA_UNROLL)


def _wait_rows(src_hbm, buf, sem, tm):
    pltpu.make_async_copy(src_hbm.at[pl.ds(0, tm)], buf, sem).wait()


def _expert_kernel(ea_ref, eb_ref, nv_ref, src_ref, xh_hbm, ga_ref, ua_ref, da_ref, gb_ref, ub_ref, db_ref, y_ref,
                   xbuf0, xbuf1, xbuf2, sems):
    tm = MOE_TILE
    j = pl.program_id(0)
    nv = nv_ref[j]
    bufs = (xbuf0, xbuf1, xbuf2)
    n_buf = len(bufs)

    @pl.when(j == 0)
    def _():
        for t in range(MOE_AHEAD):
            _gather_rows(src_ref, t * tm, xh_hbm, bufs[t], sems.at[t], tm)

    def run(p):
        cur, cur_sem = bufs[p], sems.at[p]
        q = (p + MOE_AHEAD) % n_buf
        ahead, ahead_sem = bufs[q], sems.at[q]

        @pl.when(jnp.logical_or(j < MOE_AHEAD, nv_ref[jnp.maximum(j - MOE_AHEAD, 0)] > 0))
        def _():
            _wait_rows(xh_hbm, cur, cur_sem, tm)

        @pl.when(nv > 0)
        def _():
            valid = lax.broadcasted_iota(jnp.int32, (tm, 1), 0) < nv
            x = jnp.where(valid, _unpack_bf16_pairs(cur[:, :HALF_D]), 0.0).astype(BF16)
            wts = jnp.where(valid, lax.bitcast_convert_type(cur[:, HALF_D:], F32), 0.0)
            for r in range(tm):
                pltpu.make_async_copy(xh_hbm.at[pl.ds(src_ref[(j + MOE_AHEAD) * tm + r], 1)],
                                      ahead.at[pl.ds(r, 1)], ahead_sem).start()

            def ffn(g_ref, u_ref, d_ref, w):
                a = _dot(x, g_ref[...])
                h = a * _sigmoid(a) * _dot(x, u_ref[...]) * w
                return _dot(h.astype(BF16), d_ref[...])

            y = ffn(ga_ref, ua_ref, da_ref, wts[:, 0:1]) + ffn(gb_ref, ub_ref, db_ref, wts[:, 1:2])
            y_ref[...] = _pack_bf16_pairs(y)

    for p in range(n_buf):
        pl.when(j % n_buf == p)(functools.partial(run, p))

    @pl.when(nv == 0)
    def _():
        y_ref[...] = jnp.zeros_like(y_ref)


def _experts(layer, tile_ea, tile_eb, tile_nv, src, xh, w_gate, w_up, w_down):
    tm = MOE_TILE
    up = lambda sel: pl.BlockSpec((None, None, D_MODEL, D_EXPERT),
                                  lambda j, ea, eb, nv, sr: (layer, (ea, eb)[sel][j], 0, 0))
    down = lambda sel: pl.BlockSpec((None, None, D_EXPERT, D_MODEL),
                                    lambda j, ea, eb, nv, sr: (layer, (ea, eb)[sel][j], 0, 0))
    return pl.pallas_call(
        _expert_kernel,
        grid_spec=pltpu.PrefetchScalarGridSpec(
            num_scalar_prefetch=4,
            grid=(MOE_TILES,),
            in_specs=[pl.BlockSpec(memory_space=pl.ANY), up(0), up(0), down(0), up(1), up(1), down(1)],
            out_specs=pl.BlockSpec((tm, HALF_D), lambda j, ea, eb, nv, sr: (j, 0)),
            scratch_shapes=[pltpu.VMEM((tm, XH_W), jnp.uint32)] * (MOE_AHEAD + 1)
            + [pltpu.SemaphoreType.DMA((MOE_AHEAD + 1,))],
        ),
        out_shape=jax.ShapeDtypeStruct((T_PAD, HALF_D), jnp.uint32),
        compiler_params=_cp(("arbitrary",), 56),
        name="moe_experts",
    )(tile_ea, tile_eb, tile_nv, src, xh, w_gate, w_up, w_down, w_gate, w_up, w_down)


def _combine_kernel(dest_ref, x_ref, gt_ref, ys_hbm, o_ref, buf, sems):
    tm = DISPATCH_TILE
    i = pl.program_id(0)
    slot = i % 2

    @pl.when(i == 0)
    def _():
        _gather_rows(dest_ref, 0, ys_hbm, buf.at[0], sems.at[0], tm)

    @pl.when(i + 1 < pl.num_programs(0))
    def _():
        _gather_rows(dest_ref, (i + 1) * tm, ys_hbm, buf.at[1 - slot], sems.at[1 - slot], tm)

    _wait_rows(ys_hbm, buf.at[slot], sems.at[slot], tm)
    o_ref[...] = x_ref[...] + gt_ref[...] * _unpack_bf16_pairs(buf[slot])


def _combine(dest, x, mod, ys):
    tm = DISPATCH_TILE
    return pl.pallas_call(
        _combine_kernel,
        grid_spec=pltpu.PrefetchScalarGridSpec(
            num_scalar_prefetch=1,
            grid=(T // tm,),
            in_specs=[pl.BlockSpec((tm, D_MODEL), lambda i, d: (i, 0)),
                      pl.BlockSpec((None, None, 1, D_MODEL), lambda i, d: (_cond_row(i * tm), 5, 0, 0)),
                      pl.BlockSpec(memory_space=pl.ANY)],
            out_specs=pl.BlockSpec((tm, D_MODEL), lambda i, d: (i, 0)),
            scratch_shapes=[pltpu.VMEM((2, tm, HALF_D), jnp.uint32), pltpu.SemaphoreType.DMA((2,))],
        ),
        out_shape=jax.ShapeDtypeStruct((T, D_MODEL), F32),
        compiler_params=_cp(("arbitrary",)),
        name="moe_combine",
    )(dest, x, mod, ys)


def _lookup(table, idx):
    n = table.shape[0]
    hit = idx[:, None] == jnp.arange(n, dtype=jnp.int32)[None, :]
    return jnp.sum(jnp.where(hit, table[None, :], 0), axis=1)


def _moe_plan(rt, cnt):
    bucket = rt[0].astype(jnp.int32)
    rank = rt[1].astype(jnp.int32)
    counts = cnt[:N_BUCKETS, 0].astype(jnp.int32)
    tiles = (counts + MOE_TILE - 1) // MOE_TILE
    order = jnp.arange(N_BUCKETS, dtype=jnp.int32)
    tile_start = jnp.sum(jnp.where(order[None, :] < order[:, None], tiles[None, :], 0), axis=1)
    tile_end = tile_start + tiles
    n_used = tile_end[N_BUCKETS - 1]
    dest = _lookup(tile_start * MOE_TILE, bucket) + rank
    j = jnp.arange(MOE_TILES, dtype=jnp.int32)
    jc = jnp.minimum(j, n_used - 1)
    b = jnp.minimum(jnp.sum((jc[:, None] >= tile_end[None, :]).astype(jnp.int32), axis=1), N_BUCKETS - 1)
    nv = jnp.clip(_lookup(counts, b) - (j - _lookup(tile_start, b)) * MOE_TILE, 0, MOE_TILE)
    nv = jnp.where(j < n_used, nv, 0)
    n_pairs = len(PAIR_LO)
    ea = (b // n_pairs) * EXPERTS_PER_GROUP + _lookup(jnp.asarray(PAIR_LO, jnp.int32), b % n_pairs)
    eb = (b // n_pairs) * EXPERTS_PER_GROUP + _lookup(jnp.asarray(PAIR_HI, jnp.int32), b % n_pairs)
    return dest, ea, eb, nv


def _moe(layer, x, g, mod, w_router, b_router, w_gate, w_up, w_down):
    xh, rt, cnt = _router(x, g, mod, w_router, b_router)
    dest, ea, eb, nv = _moe_plan(rt, cnt)
    ys = _experts(layer, ea, eb, nv, _invert(dest), xh, w_gate, w_up, w_down)
    return _combine(dest, x, mod, ys)


def _final_norm_kernel(x_ref, g_ref, o_ref):
    x = x_ref[...]
    o_ref[...] = x * lax.rsqrt(jnp.mean(x * x, axis=-1, keepdims=True) + NORM_EPS) * g_ref[...]


def _final_norm(x, g, row0, nrows):
    tm = 512
    blk0 = row0 // tm
    return pl.pallas_call(
        _final_norm_kernel,
        grid=(nrows // tm,),
        in_specs=[pl.BlockSpec((tm, D_MODEL), lambda i: (blk0 + i, 0)), pl.BlockSpec((1, D_MODEL), lambda i: (0, 0))],
        out_specs=pl.BlockSpec((tm, D_MODEL), lambda i: (i, 0)),
        out_shape=jax.ShapeDtypeStruct((nrows, D_MODEL), F32),
        compiler_params=_cp(("parallel",)),
        name="final_norm",
    )(x, g.reshape(1, D_MODEL))


def kernel(x_prompt, x_sample, cache_k_full, cache_v_full, cache_k_win, cache_v_win, c, c_ctx, w_mod, b_mod, norm_mix, norm_ffn, final_norm, pool_w, pool_scale, hy_w_in, hy_b_in, hy_conv_w, hy_conv_b, hy_f_w1, hy_f_b1, hy_f_w2, hy_f_b2, hy_f_freq, hy_f_w3, hy_decay, hy_skip, hy_w_out, hy_b_out, fa_w_qkv, fa_q_norm, fa_k_norm, fa_w_o, wa_w_qkv, wa_sink, wa_w_o, w_router, b_router, moe_w_gate, moe_w_up, moe_w_down):
    x = None
    cond =jnp.concatenate([c_ctx[None, :], c, jnp.zeros((N_COND - 1 - DEC_BATCH, D_MODEL), F32)], axis=0)
    mods = _adaln(cond, w_mod, b_mod).reshape(DEPTH, N_COND, 6, 1, D_MODEL)
    rope = _rope_tables()
    ones_hd = jnp.ones((HEAD_DIM,), F32)
    wg_bf, wu_bf, wd_bf = moe_w_gate.astype(BF16), moe_w_up.astype(BF16), moe_w_down.astype(BF16)
    new_kv = {}
    for layer in range(DEPTH):
        kind = layer % 4
        j = layer // 4
        mod = mods[layer]
        g_mix = norm_mix[layer].reshape(1, D_MODEL)
        if kind == 0:
            assert layer == 0, "the pooling mixer reads the two input streams, so it must be the first layer"
            x = _pool_mixer(x_prompt.reshape(TP, D_MODEL), x_sample.reshape(TS, D_MODEL), g_mix, mod,
                            pool_w[j], pool_scale[j])
        elif kind == 1:
            x = _hyena_mixer(x, g_mix, mod, hy_w_in[j], hy_b_in[j], hy_conv_w[j], hy_conv_b[j], hy_f_w1[j],
                             hy_f_b1[j], hy_f_w2[j], hy_f_b2[j], hy_f_freq[j], hy_f_w3[j], hy_decay[j],
                             hy_skip[j], hy_w_out[j], hy_b_out[j])
        elif kind == 2:
            x, nk, nv = _attn_mixer(x, g_mix, mod, fa_w_qkv[j], fa_q_norm[j], fa_k_norm[j], True, None,
                                    fa_w_o[j], cache_k_full[:, j], cache_v_full[:, j], False, rope)
            new_kv.setdefault("kf", []).append(nk)
            new_kv.setdefault("vf", []).append(nv)
        else:
            x, nk, nv = _attn_mixer(x, g_mix, mod, wa_w_qkv[j], ones_hd, ones_hd, False, wa_sink[j],
                                    wa_w_o[j], cache_k_win[:, j], cache_v_win[:, j], True, rope)
            new_kv.setdefault("kw", []).append(nk)
            new_kv.setdefault("vw", []).append(nv)
        x = _moe(layer, x, norm_ffn[layer].reshape(1, D_MODEL), mod, w_router, b_router, wg_bf, wu_bf, wd_bf)
    y_prompt = _final_norm(x, final_norm, 0, TP).reshape(BATCH, SEQ, D_MODEL)
    y_sample = _final_norm(x, final_norm, TP, TS).reshape(DEC_BATCH, DEC_SEQ, D_MODEL)
    return (y_prompt, y_sample, jnp.stack(new_kv["kf"], axis=1), jnp.stack(new_kv["vf"], axis=1),
            jnp.stack(new_kv["kw"], axis=1), jnp.stack(new_kv["vw"], axis=1))
```

```python
import functools
import math

import jax
import jax.numpy as jnp
import numpy as np
from jax import lax
from jax.experimental import pallas as pl
from jax.experimental.pallas import tpu as pltpu

D_MODEL = 2048
BATCH = 32
SEQ = 256
DEPTH = 4
DEC_BATCH = 4
DEC_SEQ = 4096
PAST_LEN = 512
GRID_W = 64
N_HEADS = 16
N_KV_HEADS = 4
HEAD_DIM = D_MODEL // N_HEADS
KV_GROUP = N_HEADS // N_KV_HEADS
KV_DIM = N_KV_HEADS * HEAD_DIM
QKV_DIM = (N_HEADS + 2 * N_KV_HEADS) * HEAD_DIM
ROPE_THETA = 10000.0
WINDOW = 128
POOL_WINDOWS = (2, 4, 8, 16)
POOL_GROUP = D_MODEL // len(POOL_WINDOWS)
HYENA_EMB_BANDS = 16
HYENA_FILTER_HIDDEN = 64
N_EXPERTS = 16
N_EXPERT_GROUPS = 4
EXPERTS_PER_GROUP = 4
D_EXPERT = 512
NORM_EPS = 1e-6
NEG_INF = -1e30

F32 = jnp.float32
BF16 = jnp.bfloat16

TP = BATCH * SEQ
TS = DEC_BATCH * DEC_SEQ
T = TP + TS
N_COND = 8
LANE = 128
MIB = 1024 * 1024

PAIR_LO = (0, 0, 0, 1, 1, 2)
PAIR_HI = (1, 2, 3, 2, 3, 3)
N_BUCKETS = N_EXPERT_GROUPS * len(PAIR_LO)
MOE_TILE = 256
MOE_AHEAD = 2
MOE_TILES = T // MOE_TILE + N_BUCKETS + MOE_AHEAD
T_PAD = MOE_TILES * MOE_TILE
HALF_D = D_MODEL // 2
XH_W = HALF_D + LANE


def _cp(sem, vmem_mb=48):
    return pltpu.CompilerParams(dimension_semantics=sem, vmem_limit_bytes=vmem_mb * MIB)


def _dot(a, b):
    return jnp.dot(a, b, preferred_element_type=F32)


def _dot3(a, b):
    ah = a.astype(BF16)
    al = (a - ah.astype(F32)).astype(BF16)
    bh = b.astype(BF16)
    bl = (b - bh.astype(F32)).astype(BF16)
    return _dot(ah, bh) + (_dot(al, bh) + _dot(ah, bl))


def _sigmoid(x):
    return 1.0 / (1.0 + jnp.exp(-x))


def _pack_bf16_pairs(x):
    n = x.shape[1] // 2
    bits = lambda v: lax.bitcast_convert_type(v.astype(BF16).astype(F32), jnp.uint32)
    return (bits(x[:, :n]) >> 16) | bits(x[:, n:])


def _unpack_bf16_pairs(u):
    lo = lax.bitcast_convert_type(u << 16, F32)
    hi = lax.bitcast_convert_type(u & jnp.uint32(0xFFFF0000), F32)
    return jnp.concatenate([lo, hi], axis=1)


def _cond_row(r):
    return jnp.where(r < TP, 0, 1 + (r - TP) // DEC_SEQ)


def _mod_spec(tm, chunk, tn=D_MODEL, ncol=False):
    if ncol:
        return pl.BlockSpec((None, None, 1, tn), lambda i, j: (_cond_row(i * tm), chunk, 0, j))
    return pl.BlockSpec((None, None, 1, tn), lambda i, *_: (_cond_row(i * tm), chunk, 0, 0))


def _norm_mod(x, g, shift, scale):
    var = jnp.mean(x * x, axis=-1, keepdims=True)
    y = x * lax.rsqrt(var + NORM_EPS) * g
    return y * (1.0 + scale) + shift


def _adaln_kernel(c_ref, w_ref, b_ref, o_ref):
    c = c_ref[...]
    a = c * _sigmoid(c)
    o_ref[...] = _dot3(a, w_ref[...]) + b_ref[...]


def _adaln(cond, w_mod, b_mod):
    tn = 1024
    n = 6 * D_MODEL
    return pl.pallas_call(
        _adaln_kernel,
        grid=(DEPTH, n // tn),
        in_specs=[
            pl.BlockSpec((N_COND, D_MODEL), lambda l, j: (0, 0)),
            pl.BlockSpec((None, D_MODEL, tn), lambda l, j: (l, 0, j)),
            pl.BlockSpec((None, 1, tn), lambda l, j: (l, 0, j)),
        ],
        out_specs=pl.BlockSpec((None, N_COND, tn), lambda l, j: (l, 0, j)),
        out_shape=jax.ShapeDtypeStruct((DEPTH, N_COND, n), F32),
        compiler_params=_cp(("parallel", "parallel")),
        name="adaln",
    )(cond, w_mod, b_mod.reshape(DEPTH, 1, n))


def _nm_matmul_kernel(x_ref, g_ref, sh_ref, sc_ref, w_ref, b_ref, o_ref, h_scr):
    @pl.when(pl.program_id(1) == 0)
    def _():
        h_scr[...] = _norm_mod(x_ref[...], g_ref[...], sh_ref[...], sc_ref[...]).astype(BF16)

    o_ref[...] = (_dot(h_scr[...], w_ref[...]) + b_ref[...]).astype(o_ref.dtype)


def _nm_matmul(x, g, mod, w, b, out_dtype, name):
    tm, tn = 1024, 1024
    n = w.shape[1]
    return pl.pallas_call(
        _nm_matmul_kernel,
        grid=(T // tm, n // tn),
        in_specs=[
            pl.BlockSpec((tm, D_MODEL), lambda i, j: (i, 0)),
            pl.BlockSpec((1, D_MODEL), lambda i, j: (0, 0)),
            _mod_spec(tm, 0),
            _mod_spec(tm, 1),
            pl.BlockSpec((D_MODEL, tn), lambda i, j: (0, j)),
            pl.BlockSpec((1, tn), lambda i, j: (0, j)),
        ],
        out_specs=pl.BlockSpec((tm, tn), lambda i, j: (i, j)),
        out_shape=jax.ShapeDtypeStruct((T, n), out_dtype),
        scratch_shapes=[pltpu.VMEM((tm, D_MODEL), BF16)],
        compiler_params=_cp(("parallel", "arbitrary")),
        name=name,
    )(x, g, mod, mod, w, b)


RESID_TM = 1024


def _resid_matmul_kernel(ap_ref, as_ref, w_ref, b_ref, x_ref, gt_ref, o_ref):
    def emit(a_ref):
        o_ref[...] = x_ref[...] + gt_ref[...] * (_dot(a_ref[...], w_ref[...]) + b_ref[...])

    is_ctx = pl.program_id(0) < TP // RESID_TM
    pl.when(is_ctx)(lambda: emit(ap_ref))
    pl.when(jnp.logical_not(is_ctx))(lambda: emit(as_ref))


def _resid_matmul(a_ctx, a_lat, w, b, x, mod, name):
    tm, tn = RESID_TM, 1024
    k = a_ctx.shape[1]
    n_ctx = TP // tm
    return pl.pallas_call(
        _resid_matmul_kernel,
        grid=(T // tm, D_MODEL // tn),
        in_specs=[
            pl.BlockSpec((tm, k), lambda i, j: (jnp.minimum(i, n_ctx - 1), 0)),
            pl.BlockSpec((tm, k), lambda i, j: (jnp.maximum(i - n_ctx, 0), 0)),
            pl.BlockSpec((k, tn), lambda i, j: (0, j)),
            pl.BlockSpec((1, tn), lambda i, j: (0, j)),
            pl.BlockSpec((tm, tn), lambda i, j: (i, j)),
            _mod_spec(tm, 2, tn, ncol=True),
        ],
        out_specs=pl.BlockSpec((tm, tn), lambda i, j: (i, j)),
        out_shape=jax.ShapeDtypeStruct((T, D_MODEL), F32),
        compiler_params=_cp(("parallel", "parallel")),
        name=name,
    )(a_ctx, a_lat, w, b, x, mod)


POOL_TILE = 256
POOL_HALO = 8


def _seq_pos(r0):
    is_ctx = r0 < TP
    loc0 = jnp.where(is_ctx, r0 % SEQ, (r0 - TP) % DEC_SEQ)
    seq_len = jnp.where(is_ctx, SEQ, DEC_SEQ)
    return loc0, seq_len


def _pool_kernel(xc_ref, xcp_ref, xcn_ref, xl_ref, xlp_ref, xln_ref, *rest):
    is_ctx = pl.program_id(0) < TP // POOL_TILE
    pl.when(is_ctx)(lambda: _pool_tile(xc_ref, xcp_ref, xcn_ref, *rest))
    pl.when(jnp.logical_not(is_ctx))(lambda: _pool_tile(xl_ref, xlp_ref, xln_ref, *rest))


def _pool_tile(x_ref, xp_ref, xn_ref, g_ref, sh_ref, sc_ref, gt_ref, pw_ref, ps_ref, o_ref, hz_scr):
    tm, hl = POOL_TILE, POOL_HALO
    loc0, seq_len = _seq_pos(pl.program_id(0) * tm)
    has_prev = loc0 > 0
    has_next = loc0 + tm < seq_len
    g, sh, sc = g_ref[...], sh_ref[...], sc_ref[...]
    x = x_ref[...]
    h = _norm_mod(x, g, sh, sc)
    hz_scr[0:hl, :] = jnp.where(has_prev, _norm_mod(xp_ref[...], g, sh, sc), 0.0)
    hz_scr[hl:hl + tm, :] = h
    hz_scr[hl + tm:, :] = jnp.where(has_next, _norm_mod(xn_ref[...], g, sh, sc), 0.0)
    tl = loc0 + lax.broadcasted_iota(jnp.int32, (tm, 1), 0)
    outs = []
    for gi, w in enumerate(POOL_WINDOWS):
        cs = slice(gi * POOL_GROUP, (gi + 1) * POOL_GROUP)
        s = jnp.zeros((tm, POOL_GROUP), F32)
        for off in range(-(w // 2), w - w // 2):
            s = s + hz_scr[hl + off:hl + off + tm, cs]
        lo = jnp.maximum(tl - w // 2, 0)
        hi = jnp.minimum(tl + (w - w // 2), seq_len)
        d = s / (hi - lo).astype(F32) - h[:, cs]
        outs.append(_dot(d.astype(BF16), pw_ref[gi]))
    out = jnp.concatenate(outs, axis=1) * ps_ref[...]
    o_ref[...] = x + gt_ref[...] * out


def _pool_mixer(x_ctx, x_lat, g, mod, pool_w, pool_scale):
    tm, hl = POOL_TILE, POOL_HALO
    r = tm // hl

    def stream(first_tile, rows):
        tile = lambda i: jnp.clip(i - first_tile, 0, rows // tm - 1)
        return [pl.BlockSpec((tm, D_MODEL), lambda i: (tile(i), 0)),
                pl.BlockSpec((hl, D_MODEL), lambda i: (jnp.maximum(tile(i) * r - 1, 0), 0)),
                pl.BlockSpec((hl, D_MODEL), lambda i: (jnp.minimum((tile(i) + 1) * r, rows // hl - 1), 0))]

    return pl.pallas_call(
        _pool_kernel,
        grid=(T // tm,),
        in_specs=stream(0, TP) + stream(TP // tm, TS) + [
            pl.BlockSpec((1, D_MODEL), lambda i: (0, 0)),
            _mod_spec(tm, 0),
            _mod_spec(tm, 1),
            _mod_spec(tm, 2),
            pl.BlockSpec((len(POOL_WINDOWS), POOL_GROUP, POOL_GROUP), lambda i: (0, 0, 0)),
            pl.BlockSpec((1, D_MODEL), lambda i: (0, 0)),
        ],
        out_specs=pl.BlockSpec((tm, D_MODEL), lambda i: (i, 0)),
        out_shape=jax.ShapeDtypeStruct((T, D_MODEL), F32),
        scratch_shapes=[pltpu.VMEM((tm + 2 * hl, D_MODEL), F32)],
        compiler_params=_cp(("parallel",)),
        name="pool_mixer",
    )(x_ctx, x_ctx, x_ctx, x_lat, x_lat, x_lat, g, mod, mod, mod, pool_w.astype(BF16),
      pool_scale.reshape(1, D_MODEL))


CONV_TILE = 256
CONV_HALO = 16


def _conv3_kernel(u_ref, up_ref, un_ref, cw_ref, cb_ref, o_ref, scr):
    tm, hl = CONV_TILE, CONV_HALO
    loc0, seq_len = _seq_pos(pl.program_id(0) * tm)
    has_prev = loc0 > 0
    has_next = loc0 + tm < seq_len
    scr[0:hl, :] = jnp.where(has_prev, up_ref[...].astype(F32), 0.0)
    scr[hl:hl + tm, :] = u_ref[...].astype(F32)
    scr[hl + tm:, :] = jnp.where(has_next, un_ref[...].astype(F32), 0.0)
    out = (scr[hl - 1:hl - 1 + tm, :] * cw_ref[0:1, :] + scr[hl:hl + tm, :] * cw_ref[1:2, :]
           + scr[hl + 1:hl + 1 + tm, :] * cw_ref[2:3, :] + cb_ref[...])
    o_ref[...] = out.astype(o_ref.dtype)


def _conv3(u0, conv_w, conv_b):
    tm, hl, tc = CONV_TILE, CONV_HALO, D_MODEL
    r = tm // hl
    n = u0.shape[1]
    return pl.pallas_call(
        _conv3_kernel,
        grid=(T // tm, n // tc),
        in_specs=[
            pl.BlockSpec((tm, tc), lambda i, j: (i, j)),
            pl.BlockSpec((hl, tc), lambda i, j: (jnp.maximum(i * r - 1, 0), j)),
            pl.BlockSpec((hl, tc), lambda i, j: (jnp.minimum((i + 1) * r, T // hl - 1), j)),
            pl.BlockSpec((3, tc), lambda i, j: (0, j)),
            pl.BlockSpec((1, tc), lambda i, j: (0, j)),
        ],
        out_specs=pl.BlockSpec((tm, tc), lambda i, j: (i, j)),
        out_shape=jax.ShapeDtypeStruct((T, n), BF16),
        scratch_shapes=[pltpu.VMEM((tm + 2 * hl, tc), F32)],
        compiler_params=_cp(("parallel", "parallel")),
        name="hyena_conv3",
    )(u0, u0, u0, conv_w, conv_b.reshape(1, n))


FILT_TILE = 256


HYENA_BLOCK = 1024


T_LANE = LANE - 1


def _filter_mlp_kernel(emb_ref, w1_ref, b1_ref, w2_ref, b2_ref, fr_ref, o_ref):
    emb = emb_ref[...]
    fr = fr_ref[...]
    a = jnp.sin(fr * (_dot3(emb, w1_ref[...]) + b1_ref[...]))
    a = jnp.sin(fr * (_dot3(a, w2_ref[...]) + b2_ref[...]))
    lane = lax.broadcasted_iota(jnp.int32, a.shape, 1)
    o_ref[...] = jnp.where(lane == T_LANE, emb[:, 0:1], a)


def _filter_mlp(pos, L, f_w1, f_b1, f_w2, f_b2, f_freq):
    assert HYENA_FILTER_HIDDEN <= T_LANE
    tl = FILT_TILE
    rows = pos.shape[0]
    small = lambda: pl.BlockSpec((LANE, LANE), lambda i: (0, 0))
    vec = lambda: pl.BlockSpec((1, LANE), lambda i: (0, 0))
    return pl.pallas_call(
        _filter_mlp_kernel,
        grid=(rows // tl,),
        in_specs=[pl.BlockSpec((tl, LANE), lambda i: (i, 0)), small(), vec(), small(), vec(), vec()],
        out_specs=pl.BlockSpec((tl, LANE), lambda i: (i, 0)),
        out_shape=jax.ShapeDtypeStruct((rows, LANE), F32),
        compiler_params=_cp(("parallel",)),
        name="hyena_filter_mlp",
    )(_filter_embedding(pos, L), _pad2(f_w1, LANE, LANE), _pad2(f_b1[None], 1, LANE),
      _pad2(f_w2, LANE, LANE), _pad2(f_b2[None], 1, LANE), _pad2(f_freq[None], 1, LANE))


def _filter_kernel(h1_ref, h2_ref, w3a_ref, dca_ref, w3b_ref, dcb_ref, fa_ref, fb_ref, *, blk):
    def taps(h, w3_ref, dc_ref):
        return _dot3(h, w3_ref[...]) * jnp.exp(-h[:, T_LANE:] * jnp.abs(dc_ref[...]))

    pos = taps(h1_ref[...], w3a_ref, dca_ref)
    neg = taps(h2_ref[...], w3b_ref, dcb_ref)
    m = (pl.program_id(0) * FILT_TILE + lax.broadcasted_iota(jnp.int32, (FILT_TILE, 1), 0)) % blk
    fa_ref[...] = jnp.where(m == 0, pos, pos + neg).astype(BF16)
    fb_ref[...] = jnp.where(m == 0, 0.0, neg - pos).astype(BF16)


def _pad2(a, rows, cols):
    return jnp.pad(a, ((0, rows - a.shape[0]), (0, cols - a.shape[1])))


def _filter_positions(L, blk):
    n_blk = L // blk
    m = np.arange(blk)
    p1, p2 = [], []
    for d in range(-(n_blk - 1), n_blk):
        if d >= 1:
            p1.append(d * blk + m), p2.append(d * blk - m)
        elif d == 0:
            p1.append(m), p2.append(m)
        else:
            p1.append(-d * blk - m), p2.append(-d * blk + m)
    return np.concatenate(p1), np.concatenate(p2)


def _filter_embedding(pos, L):
    t = jnp.asarray(pos, F32) / L
    bands = jnp.linspace(1e-4, HYENA_EMB_BANDS - 1, HYENA_EMB_BANDS, dtype=F32)
    ang = (2 * math.pi) * t[:, None] * bands[None, :]
    return _pad2(jnp.concatenate([t[:, None], jnp.cos(ang), -jnp.sin(ang)], axis=-1), pos.shape[0], LANE)


def _hyena_filters(L, blk, f_w1, f_b1, f_w2, f_b2, f_freq, f_w3, decay):
    n_blk = L // blk
    p1, p2 = _filter_positions(L, blk)
    rows = p1.shape[0]
    tl = FILT_TILE
    tiles_per_lag = blk // tl
    lag = lambda i: i // tiles_per_lag - (n_blk - 1)
    col1 = lambda i, n: 2 * n + jnp.where(lag(i) >= 0, 0, 1)
    col2 = lambda i, n: 2 * n + jnp.where(lag(i) >= 1, 0, 1)
    n_tiles = rows // tl
    out = pl.BlockSpec((None, tl, D_MODEL), lambda i, n: (n, i, 0))
    w3 = _pad2(f_w3, LANE, f_w3.shape[1])
    hidden = _filter_mlp(np.concatenate([p1, p2]), L, f_w1, f_b1, f_w2, f_b2, f_freq)
    return pl.pallas_call(
        functools.partial(_filter_kernel, blk=blk),
        grid=(n_tiles, 2),
        in_specs=[
            pl.BlockSpec((tl, LANE), lambda i, n: (i, 0)),
            pl.BlockSpec((tl, LANE), lambda i, n: (n_tiles + i, 0)),
            pl.BlockSpec((LANE, D_MODEL), lambda i, n: (0, col1(i, n))),
            pl.BlockSpec((1, D_MODEL), lambda i, n: (0, col1(i, n))),
            pl.BlockSpec((LANE, D_MODEL), lambda i, n: (0, col2(i, n))),
            pl.BlockSpec((1, D_MODEL), lambda i, n: (0, col2(i, n))),
        ],
        out_specs=[out, out],
        out_shape=[jax.ShapeDtypeStruct((2, rows, D_MODEL), BF16)] * 2,
        compiler_params=_cp(("parallel", "parallel")),
        name="hyena_filters",
    )(hidden, hidden, w3, decay[None], w3, decay[None])


def _dft_mats(L):
    r = int(math.isqrt(L))
    k2 = 2 * jnp.arange(L, dtype=jnp.int32)[:, None] + 1
    n1 = r * jnp.arange(L // r, dtype=jnp.int32)[None, :]
    n2 = jnp.arange(r, dtype=jnp.int32)[None, :]
    sc = math.pi / (2 * L)
    aa = ((k2 * n1) % (4 * L)).astype(F32) * sc
    ab = ((k2 * n2) % (4 * L)).astype(F32) * sc
    ca, sa, cb, sb = jnp.cos(aa)[:, :, None], jnp.sin(aa)[:, :, None], jnp.cos(ab)[:, None, :], jnp.sin(ab)[:, None, :]
    c = (ca * cb - sa * sb).reshape(L, L)
    s = (sa * cb + ca * sb).reshape(L, L)
    return c.astype(BF16), s.astype(BF16), c.T.astype(BF16), s.T.astype(BF16)


def _dft_tiles(L):
    return min(512, L), 512


def _dft_filter_kernel(c_ref, s_ref, a_ref, b_ref, gr_ref, gi_ref):
    gr_ref[...] = _dot(c_ref[...], a_ref[...]).astype(gr_ref.dtype)
    gi_ref[...] = _dot(s_ref[...], b_ref[...]).astype(gi_ref.dtype)


def _dft_filter(cm, sm, fa, fb, L):
    tf, tn = _dft_tiles(L)
    n = fa.shape[0]
    mat = lambda: pl.BlockSpec((tf, L), lambda k, c, s: (k, 0))
    rhs = lambda: pl.BlockSpec((None, L, tn), lambda k, c, s: (s, 0, c))
    out = pl.BlockSpec((None, tf, tn), lambda k, c, s: (s, k, c))
    return pl.pallas_call(
        _dft_filter_kernel,
        grid=(L // tf, D_MODEL // tn, n),
        in_specs=[mat(), mat(), rhs(), rhs()],
        out_specs=[out, out],
        out_shape=[jax.ShapeDtypeStruct((n, L, D_MODEL), BF16)] * 2,
        compiler_params=_cp(("parallel", "parallel", "parallel")),
        name="hyena_filter_dft",
    )(cm, sm, fa, fb)


FWD_TF = 256


def _dft_fwd_kernel(c_ref, s_ref, z_ref, gr_ref, gi_ref, yr_ref, yi_ref, *, n_blk, blk, bpb):
    c, s = c_ref[...], s_ref[...]
    for bb in range(bpb):
        zc, zs = [], []
        for j in range(n_blk):
            r = (bb * n_blk + j) * blk
            zj = z_ref[r:r + blk, :]
            zc.append(_dot(c, zj).astype(BF16))
            zs.append(_dot(s, zj).astype(BF16))
        for i in range(n_blk):
            yr = yi = None
            for j in range(n_blk):
                lag = i - j + n_blk - 1
                gr, gi = gr_ref[lag], gi_ref[lag]
                tr = gr * zc[j] + gi * zs[j]
                ti = gi * zc[j] - gr * zs[j]
                yr = tr if yr is None else yr + tr
                yi = ti if yi is None else yi + ti
            yr_ref[bb, i] = yr.astype(BF16)
            yi_ref[bb, i] = yi.astype(BF16)


def _seqs_per_step(L):
    return max(1, 2048 // L)


def _dft_fwd(cm, sm, z, z_rowblk, z_colblk, gr, gi, order, nb, L, blk):
    n_blk = L // blk
    bpb = _seqs_per_step(L)
    assert nb % bpb == 0 and z_rowblk % bpb == 0
    tf, tn = min(FWD_TF, blk), 512
    mat = lambda: pl.BlockSpec((tf, blk), lambda k, c, b: (k, 0))
    gsp = lambda: pl.BlockSpec((None, 2 * n_blk - 1, tf, tn), lambda k, c, b: (order, 0, k, c))
    out = pl.BlockSpec((bpb, n_blk, tf, tn), lambda k, c, b: (b, 0, k, c))
    return pl.pallas_call(
        functools.partial(_dft_fwd_kernel, n_blk=n_blk, blk=blk, bpb=bpb),
        grid=(blk // tf, D_MODEL // tn, nb // bpb),
        in_specs=[mat(), mat(),
                  pl.BlockSpec((bpb * L, tn), lambda k, c, b: (z_rowblk // bpb + b, z_colblk + c)),
                  gsp(), gsp()],
        out_specs=[out, out],
        out_shape=[jax.ShapeDtypeStruct((nb, n_blk, blk, D_MODEL), BF16)] * 2,
        compiler_params=_cp(("parallel", "parallel", "parallel")),
        name="hyena_dft_fwd",
    )(cm, sm, z, gr, gi)


def _dft_inv_kernel(ct_ref, st_ref, yr_ref, yi_ref, z_ref, gt_ref, sk_ref, o_ref, *, inv_len, bpb, tt):
    for bb in range(bpb):
        rows = slice(bb * tt, (bb + 1) * tt)
        y = (_dot(ct_ref[...], yr_ref[bb]) - _dot(st_ref[...], yi_ref[bb])) * inv_len
        o_ref[rows, :] = (gt_ref[rows, :].astype(F32) * (y + sk_ref[...] * z_ref[rows, :].astype(F32))).astype(BF16)


def _dft_inv(ctm, stm, yr, yi, z, z_rowblk, z_colblk, gate, g_rowblk, g_colblk, skip, nb, L):
    tt, tn = _dft_tiles(L)
    rpb = L // tt
    bpb = _seqs_per_step(L) if rpb == 1 else 1
    assert nb % bpb == 0 and z_rowblk % bpb == 0 and g_rowblk % bpb == 0
    mat = lambda: pl.BlockSpec((tt, L), lambda t, c, b: (t, 0))
    spec = lambda: pl.BlockSpec((bpb, L, tn), lambda t, c, b: (b, 0, c))
    rows = lambda blk0: (lambda t, c, b: ((blk0 + b * bpb * rpb + t) // bpb))
    return pl.pallas_call(
        functools.partial(_dft_inv_kernel, inv_len=1.0 / L, bpb=bpb, tt=tt),
        grid=(rpb, D_MODEL // tn, nb // bpb),
        in_specs=[mat(), mat(), spec(), spec(),
                  pl.BlockSpec((bpb * tt, tn), lambda t, c, b: (rows(z_rowblk)(t, c, b), z_colblk + c)),
                  pl.BlockSpec((bpb * tt, tn), lambda t, c, b: (rows(g_rowblk)(t, c, b), g_colblk + c)),
                  pl.BlockSpec((1, tn), lambda t, c, b: (0, c))],
        out_specs=pl.BlockSpec((bpb * tt, tn), lambda t, c, b: (rows(0)(t, c, b), c)),
        out_shape=jax.ShapeDtypeStruct((nb * L, D_MODEL), BF16),
        compiler_params=_cp(("parallel", "parallel", "parallel")),
        name="hyena_dft_inv",
    )(ctm, stm, yr, yi, z, gate, skip)


def _hyena_stream(u, row0, nb, L, fparams, skip):
    blk = min(HYENA_BLOCK, L)
    n_blk = L // blk
    n_lag = 2 * n_blk - 1
    cm, sm, ctm, stm = _dft_mats(blk)
    fa, fb = _hyena_filters(L, blk, *fparams)
    seg = lambda a: a.reshape(2 * n_lag, blk, D_MODEL)
    gr, gi = _dft_filter(cm, sm, seg(fa), seg(fb), blk)
    gr, gi = (a.reshape(2, n_lag, blk, D_MODEL) for a in (gr, gi))
    tt, tn = _dft_tiles(blk)
    ncb = D_MODEL // tn
    blocks = lambda a: a.reshape(nb * n_blk, blk, D_MODEL)

    yr, yi = _dft_fwd(cm, sm, u, row0 // L, 0, gr, gi, 0, nb, L, blk)
    z1 = _dft_inv(ctm, stm, blocks(yr), blocks(yi), u, row0 // tt, 0, u, row0 // tt, ncb, skip[0:1],
                  nb * n_blk, blk)
    yr, yi = _dft_fwd(cm, sm, z1, 0, 0, gr, gi, 1, nb, L, blk)
    return _dft_inv(ctm, stm, blocks(yr), blocks(yi), z1, 0, 0, u, row0 // tt, 2 * ncb, skip[1:2],
                    nb * n_blk, blk)


def _hyena_mixer(x, g, mod, w_in, b_in, conv_w, conv_b, f_w1, f_b1, f_w2, f_b2, f_freq, f_w3, decay, skip,
                 w_out, b_out):
    u0 = _nm_matmul(x, g, mod, w_in.astype(BF16), b_in.reshape(1, -1), BF16, "hyena_in_proj")
    u = _conv3(u0, conv_w, conv_b)
    fparams = (f_w1, f_b1, f_w2, f_b2, f_freq, f_w3, decay)
    zp = _hyena_stream(u, 0, BATCH, SEQ, fparams, skip)
    zs = _hyena_stream(u, TP, DEC_BATCH, DEC_SEQ, fparams, skip)
    return _resid_matmul(zp, zs, w_out.astype(BF16), b_out.reshape(1, -1), x, mod, "hyena_out_proj")


def _rope_tables():
    pos = jnp.arange(DEC_SEQ, dtype=jnp.int32)
    row = (pos // GRID_W).astype(F32)
    col = (pos % GRID_W).astype(F32)
    axis_dim = HEAD_DIM // 2
    inv_freq = ROPE_THETA ** (-jnp.arange(0, axis_dim, 2, dtype=F32) / axis_dim)
    ar = row[:, None] * inv_freq[None, :]
    ac = col[:, None] * inv_freq[None, :]
    cos = jnp.concatenate([jnp.cos(ar), jnp.cos(ar), jnp.cos(ac), jnp.cos(ac)], axis=-1)
    sin = jnp.concatenate([-jnp.sin(ar), jnp.sin(ar), -jnp.sin(ac), jnp.sin(ac)], axis=-1)
    return cos, sin


QKV_TM = 512
QKV_TN = 1024
PAIR = 2 * HEAD_DIM


def _qkv_kernel(x_ref, g_ref, sh_ref, sc_ref, w_ref, qn_ref, kn_ref, cos_ref, sin_ref,
                q_ref, k_ref, v_ref, nk_ref, nv_ref, h_scr, *, use_norm):
    tm = QKV_TM
    i, j = pl.program_id(0), pl.program_id(1)

    @pl.when(j == 0)
    def _():
        h_scr[...] = _norm_mod(x_ref[...], g_ref[...], sh_ref[...], sc_ref[...]).astype(BF16)

    quarter = HEAD_DIM // 4
    scale = HEAD_DIM ** -0.5 * LOG2E

    def head(xh, gn):
        if use_norm:
            xh = xh * lax.rsqrt(jnp.mean(xh * xh, axis=-1, keepdims=True) + NORM_EPS) * gn
        return xh

    def rope(xh):
        lane = lax.broadcasted_iota(jnp.int32, (tm, HEAD_DIM), 1)
        first = (lane % (2 * quarter)) < quarter
        partner = jnp.where(first, pltpu.roll(xh, HEAD_DIM - quarter, 1), pltpu.roll(xh, quarter, 1))
        return xh * cos_ref[...] + partner * sin_ref[...]

    def proj(c0):
        return _dot(h_scr[...], w_ref[:, c0:c0 + PAIR])

    def q_tile(latent):
        for p in range(QKV_TN // PAIR):
            acc = proj(p * PAIR)
            for t in range(2):
                xh = head(acc[:, t * HEAD_DIM:(t + 1) * HEAD_DIM], qn_ref[...])
                xh = rope(xh) if latent else xh
                c0 = p * PAIR + t * HEAD_DIM
                q_ref[:, c0:c0 + HEAD_DIM] = (xh * scale).astype(BF16)

    def kv_tile(latent):
        for p in range(KV_DIM // PAIR):
            acc = proj(p * PAIR)
            for t in range(2):
                c0 = p * PAIR + t * HEAD_DIM
                kh = head(acc[:, t * HEAD_DIM:(t + 1) * HEAD_DIM], kn_ref[...])
                if not latent:
                    nk_ref[:, c0:c0 + HEAD_DIM] = kh
                k_ref[:, c0:c0 + HEAD_DIM] = (rope(kh) if latent else kh).astype(BF16)
        for p in range(KV_DIM // PAIR):
            acc = proj(KV_DIM + p * PAIR)
            if not latent:
                nv_ref[:, p * PAIR:(p + 1) * PAIR] = acc
            v_ref[:, p * PAIR:(p + 1) * PAIR] = acc.astype(BF16)

    is_ctx = i < TP // tm
    is_q = j < D_MODEL // QKV_TN
    for latent in (False, True):
        stream = jnp.logical_not(is_ctx) if latent else is_ctx
        pl.when(jnp.logical_and(stream, is_q))(functools.partial(q_tile, latent))
        pl.when(jnp.logical_and(stream, jnp.logical_not(is_q)))(functools.partial(kv_tile, latent))


def _qkv_proj(x, g, mod, w_qkv, q_norm, k_norm, use_norm, rope):
    tm, tn = QKV_TM, QKV_TN
    n_ctx = TP // tm
    n_q = D_MODEL // tn
    tab = lambda: pl.BlockSpec((tm, HEAD_DIM), lambda i, j: (jnp.maximum(i - n_ctx, 0) % (DEC_SEQ // tm), 0))
    kv = lambda: pl.BlockSpec((tm, KV_DIM), lambda i, j: (i, 0))
    new = lambda: pl.BlockSpec((tm, KV_DIM), lambda i, j: (jnp.minimum(i, n_ctx - 1), 0))
    return pl.pallas_call(
        functools.partial(_qkv_kernel, use_norm=use_norm),
        grid=(T // tm, QKV_DIM // tn),
        in_specs=[
            pl.BlockSpec((tm, D_MODEL), lambda i, j: (i, 0)),
            pl.BlockSpec((1, D_MODEL), lambda i, j: (0, 0)),
            _mod_spec(tm, 0),
            _mod_spec(tm, 1),
            pl.BlockSpec((D_MODEL, tn), lambda i, j: (0, j)),
            pl.BlockSpec((1, HEAD_DIM), lambda i, j: (0, 0)),
            pl.BlockSpec((1, HEAD_DIM), lambda i, j: (0, 0)),
            tab(), tab(),
        ],
        out_specs=[pl.BlockSpec((tm, tn), lambda i, j: (i, jnp.minimum(j, n_q - 1))), kv(), kv(), new(), new()],
        out_shape=[jax.ShapeDtypeStruct((T, D_MODEL), BF16), jax.ShapeDtypeStruct((T, KV_DIM), BF16),
                   jax.ShapeDtypeStruct((T, KV_DIM), BF16), jax.ShapeDtypeStruct((TP, KV_DIM), F32),
                   jax.ShapeDtypeStruct((TP, KV_DIM), F32)],
        scratch_shapes=[pltpu.VMEM((tm, D_MODEL), BF16)],
        compiler_params=_cp(("arbitrary", "arbitrary")),
        name="qkv_proj",
    )(x, g, mod, mod, w_qkv, q_norm.reshape(1, HEAD_DIM), k_norm.reshape(1, HEAD_DIM), *rope)


LOG2E = math.log2(math.e)
ATTN_TQ = 256


def _attn_kernel(*refs, tq, seq_len, n_ctx, windowed, has_sink):
    it = iter(refs)
    q_ref, k_ref, v_ref = next(it), next(it), next(it)
    sink_ref = next(it) if has_sink else None
    o_ref = next(it)
    if windowed:
        i = pl.program_id(2)
        span = tq + 2 * WINDOW
        start = pl.multiple_of(jnp.clip(i * tq - WINDOW, 0, seq_len - span), WINDOW)
        qpos = i * tq + lax.broadcasted_iota(jnp.int32, (tq, 1), 0)
        kpos = start + lax.broadcasted_iota(jnp.int32, (1, span), 1)
        segs = [(pl.ds(start, span), jnp.abs(kpos - qpos) <= WINDOW), (pl.ds(seq_len, n_ctx), None)]
    else:
        segs = [(slice(None), None)]
    for h in range(KV_GROUP):
        hs = slice(h * HEAD_DIM, (h + 1) * HEAD_DIM)
        qh = q_ref[:, hs]
        scores = []
        m = None
        for rows, mask in segs:
            s = lax.dot_general(qh, k_ref[rows, :], (((1,), (1,)), ((), ())), preferred_element_type=F32)
            if mask is not None:
                s = jnp.where(mask, s, NEG_INF)
            scores.append(s)
            ms = jnp.max(s, axis=-1, keepdims=True)
            m = ms if m is None else jnp.maximum(m, ms)
        if has_sink:
            sk = sink_ref[pl.program_id(1) * KV_GROUP + h]
            m = jnp.maximum(m, sk)
        l = jnp.exp2(sk - m) if has_sink else jnp.zeros_like(m)
        acc = jnp.zeros((tq, HEAD_DIM), F32)
        for (rows, _), s in zip(segs, scores):
            p = jnp.exp2(s - m)
            l = l + jnp.sum(p, axis=-1, keepdims=True)
            acc = acc + _dot(p.astype(BF16), v_ref[rows, :])
        o_ref[:, hs] = (acc / l).astype(BF16)


def _attention(q, q_row0, k, v, sink, nb, L, n_ctx, windowed):
    tq = min(ATTN_TQ, L)
    nq = L // tq
    nk = L + n_ctx
    q_blk0 = q_row0 // tq
    kv = lambda: pl.BlockSpec((None, nk, HEAD_DIM), lambda b, g, i: (b, 0, g))
    in_specs = [pl.BlockSpec((tq, KV_GROUP * HEAD_DIM), lambda b, g, i: (q_blk0 + b * nq + i, g)), kv(), kv()]
    args = [q, k, v]
    if sink is not None:
        in_specs.append(pl.BlockSpec(memory_space=pltpu.SMEM))
        args.append(sink.astype(F32) * LOG2E)
    return pl.pallas_call(
        functools.partial(_attn_kernel, tq=tq, seq_len=L, n_ctx=n_ctx, windowed=windowed,
                          has_sink=sink is not None),
        grid=(nb, N_KV_HEADS, nq),
        in_specs=in_specs,
        out_specs=pl.BlockSpec((tq, KV_GROUP * HEAD_DIM), lambda b, g, i: (b * nq + i, g)),
        out_shape=jax.ShapeDtypeStruct((nb * L, D_MODEL), BF16),
        compiler_params=_cp(("parallel", "parallel", "parallel"), 56),
        name="attention",
    )(*args)


def _attn_mixer(x, g, mod, w_qkv, q_norm, k_norm, use_norm, sink, w_o, cache_k, cache_v, windowed, rope):
    q, k, v, new_k, new_v = _qkv_proj(x, g, mod, w_qkv.astype(BF16), q_norm, k_norm, use_norm, rope)
    op = _attention(q, 0, k[:TP].reshape(BATCH, SEQ, KV_DIM), v[:TP].reshape(BATCH, SEQ, KV_DIM), sink,
                    BATCH, SEQ, 0, False)
    kc = cache_k.reshape(DEC_BATCH, PAST_LEN, KV_DIM).astype(BF16)
    vc = cache_v.reshape(DEC_BATCH, PAST_LEN, KV_DIM).astype(BF16)
    k_all = jnp.concatenate([k[TP:].reshape(DEC_BATCH, DEC_SEQ, KV_DIM), kc], axis=1)
    v_all = jnp.concatenate([v[TP:].reshape(DEC_BATCH, DEC_SEQ, KV_DIM), vc], axis=1)
    osm = _attention(q, TP, k_all, v_all, sink, DEC_BATCH, DEC_SEQ, PAST_LEN, windowed)
    x = _resid_matmul(op, osm, w_o.astype(BF16), jnp.zeros((1, D_MODEL), F32), x, mod, "attn_out_proj")
    shape = (BATCH, SEQ, N_KV_HEADS, HEAD_DIM)
    return x, new_k.reshape(shape), new_v.reshape(shape)


ROUTE_TILE = 512
ROUTE_ROWS = 32


def _router_kernel(x_ref, g_ref, sh_ref, sc_ref, wr_ref, br_ref, xh_ref, rt_ref, cnt_ref, carry):
    tm = ROUTE_TILE
    i = pl.program_id(0)

    @pl.when(i == 0)
    def _():
        carry[...] = jnp.zeros_like(carry)

    h = _norm_mod(x_ref[...], g_ref[...], sh_ref[...], sc_ref[...])
    xh_ref[:, :HALF_D] = _pack_bf16_pairs(h)
    logits = _dot(h.astype(BF16), wr_ref[...])
    s = _sigmoid(logits.T[:N_EXPERTS, :])
    sb = s + br_ref[...]
    u = [s[e:e + 1, :] for e in range(N_EXPERTS)]
    v = [sb[e:e + 1, :] for e in range(N_EXPERTS)]

    gscore = []
    for gq in range(N_EXPERT_GROUPS):
        m = v[4 * gq:4 * gq + 4]
        best = m[PAIR_LO[0]] + m[PAIR_HI[0]]
        for a, b in zip(PAIR_LO[1:], PAIR_HI[1:]):
            best = jnp.maximum(best, m[a] + m[b])
        gscore.append(best)
    gidx = jnp.zeros((1, tm), jnp.int32)
    gbest = gscore[0]
    for gq in range(1, N_EXPERT_GROUPS):
        upd = gscore[gq] > gbest
        gidx = jnp.where(upd, gq, gidx)
        gbest = jnp.where(upd, gscore[gq], gbest)

    def pick(rows, j):
        out = rows[j]
        for gq in range(1, N_EXPERT_GROUPS):
            out = jnp.where(gidx == gq, rows[4 * gq + j], out)
        return out

    vin = [pick(v, j) for j in range(EXPERTS_PER_GROUP)]
    uin = [pick(u, j) for j in range(EXPERTS_PER_GROUP)]
    i1 = jnp.zeros((1, tm), jnp.int32)
    m1 = vin[0]
    for j in range(1, EXPERTS_PER_GROUP):
        upd = vin[j] > m1
        i1 = jnp.where(upd, j, i1)
        m1 = jnp.where(upd, vin[j], m1)
    i2 = jnp.full((1, tm), -1, jnp.int32)
    m2 = jnp.full((1, tm), -jnp.inf, F32)
    for j in range(EXPERTS_PER_GROUP):
        upd = (i1 != j) & (vin[j] > m2)
        i2 = jnp.where(upd, j, i2)
        m2 = jnp.where(upd, vin[j], m2)

    def sel(rows, idx):
        out = rows[0]
        for j in range(1, EXPERTS_PER_GROUP):
            out = jnp.where(idx == j, rows[j], out)
        return out

    w1, w2 = sel(uin, i1), sel(uin, i2)
    wsum = w1 + w2
    w1, w2 = w1 / wsum, w2 / wsum
    first_lo = i1 < i2
    lo = jnp.where(first_lo, i1, i2)
    hi = jnp.where(first_lo, i2, i1)
    w_lo = jnp.where(first_lo, w1, w2)
    w_hi = jnp.where(first_lo, w2, w1)
    pair = jnp.where(lo == 0, hi - 1, jnp.where(lo == 1, hi + 1, 5))
    bucket = gidx * len(PAIR_LO) + pair

    onehot = (lax.broadcasted_iota(jnp.int32, (ROUTE_ROWS, tm), 0) == bucket)
    tri = (lax.broadcasted_iota(jnp.int32, (tm, tm), 0) <= lax.broadcasted_iota(jnp.int32, (tm, tm), 1))
    cum = _dot(jnp.where(onehot, 1.0, 0.0).astype(BF16), jnp.where(tri, 1.0, 0.0).astype(BF16))
    rank = jnp.sum(jnp.where(onehot, cum - 1.0 + carry[...], 0.0), axis=0, keepdims=True)
    carry[...] = carry[...] + cum[:, tm - 1:tm]
    cnt_ref[...] = jnp.broadcast_to(carry[...], (ROUTE_ROWS, LANE))

    rt_ref[...] = jnp.zeros_like(rt_ref)
    rt_ref[0:1, :] = bucket.astype(F32)
    rt_ref[1:2, :] = rank
    wt = jnp.concatenate([w_lo, w_hi, jnp.zeros((LANE - 2, tm), F32)], axis=0)
    xh_ref[:, HALF_D:] = lax.bitcast_convert_type(wt.T, jnp.uint32)


def _router(x, g, mod, w_router, b_router):
    tm = ROUTE_TILE
    wr = _pad2(w_router, D_MODEL, LANE).astype(BF16)
    return pl.pallas_call(
        _router_kernel,
        grid=(T // tm,),
        in_specs=[
            pl.BlockSpec((tm, D_MODEL), lambda i: (i, 0)),
            pl.BlockSpec((1, D_MODEL), lambda i: (0, 0)),
            _mod_spec(tm, 3),
            _mod_spec(tm, 4),
            pl.BlockSpec((D_MODEL, LANE), lambda i: (0, 0)),
            pl.BlockSpec((N_EXPERTS, 1), lambda i: (0, 0)),
        ],
        out_specs=[
            pl.BlockSpec((tm, XH_W), lambda i: (i, 0)),
            pl.BlockSpec((8, tm), lambda i: (0, i)),
            pl.BlockSpec((ROUTE_ROWS, LANE), lambda i: (0, 0)),
        ],
        out_shape=[
            jax.ShapeDtypeStruct((T, XH_W), jnp.uint32),
            jax.ShapeDtypeStruct((8, T), F32),
            jax.ShapeDtypeStruct((ROUTE_ROWS, LANE), F32),
        ],
        scratch_shapes=[pltpu.VMEM((ROUTE_ROWS, 1), F32)],
        compiler_params=_cp(("arbitrary",)),
        name="moe_router",
    )(x, g, mod, mod, wr, b_router.reshape(N_EXPERTS, 1))


DISPATCH_TILE = 256


DMA_UNROLL = 32


def _invert_kernel(dest_ref, src_ref):
    def clear(s, c):
        src_ref[s] = 0
        return c

    def put(t, c):
        src_ref[dest_ref[t]] = t
        return c

    lax.fori_loop(0, T_PAD, clear, 0, unroll=DMA_UNROLL)
    lax.fori_loop(0, T, put, 0, unroll=DMA_UNROLL)


def _invert(dest):
    return pl.pallas_call(
        _invert_kernel,
        in_specs=[pl.BlockSpec(memory_space=pltpu.SMEM)],
        out_specs=pl.BlockSpec(memory_space=pltpu.SMEM),
        out_shape=jax.ShapeDtypeStruct((T_PAD,), jnp.int32),
        name="moe_invert",
    )(dest)


def _gather_rows(idx_ref, base, src_hbm, buf, sem, tm, static=False):
    def start(r, c):
        pltpu.make_async_copy(src_hbm.at[pl.ds(idx_ref[base + r], 1)], buf.at[pl.ds(r, 1)], sem).start()
        return c

    if static:
        for r in range(tm):
            start(r, 0)
    else:
        lax.fori_loop(0, tm, start, 0, unroll=DMA_UNROLL)


def _wait_rows(src_hbm, buf, sem, tm):
    pltpu.make_async_copy(src_hbm.at[pl.ds(0, tm)], buf, sem).wait()


def _expert_kernel(ea_ref, eb_ref, nv_ref, src_ref, xh_hbm, ga_ref, ua_ref, da_ref, gb_ref, ub_ref, db_ref, y_ref,
                   xbuf0, xbuf1, xbuf2, sems):
    tm = MOE_TILE
    j = pl.program_id(0)
    nv = nv_ref[j]
    bufs = (xbuf0, xbuf1, xbuf2)
    n_buf = len(bufs)

    @pl.when(j == 0)
    def _():
        for t in range(MOE_AHEAD):
            _gather_rows(src_ref, t * tm, xh_hbm, bufs[t], sems.at[t], tm)

    def run(p):
        cur, cur_sem = bufs[p], sems.at[p]
        q = (p + MOE_AHEAD) % n_buf
        ahead, ahead_sem = bufs[q], sems.at[q]

        @pl.when(jnp.logical_or(j < MOE_AHEAD, nv_ref[jnp.maximum(j - MOE_AHEAD, 0)] > 0))
        def _():
            _wait_rows(xh_hbm, cur, cur_sem, tm)

        @pl.when(nv > 0)
        def _():
            valid = lax.broadcasted_iota(jnp.int32, (tm, 1), 0) < nv
            x = jnp.where(valid, _unpack_bf16_pairs(cur[:, :HALF_D]), 0.0).astype(BF16)
            wts = jnp.where(valid, lax.bitcast_convert_type(cur[:, HALF_D:], F32), 0.0)
            for r in range(tm):
                pltpu.make_async_copy(xh_hbm.at[pl.ds(src_ref[(j + MOE_AHEAD) * tm + r], 1)],
                                      ahead.at[pl.ds(r, 1)], ahead_sem).start()

            def ffn(g_ref, u_ref, d_ref, w):
                a = _dot(x, g_ref[...])
                h = a * _sigmoid(a) * _dot(x, u_ref[...]) * w
                return _dot(h.astype(BF16), d_ref[...])

            y = ffn(ga_ref, ua_ref, da_ref, wts[:, 0:1]) + ffn(gb_ref, ub_ref, db_ref, wts[:, 1:2])
            y_ref[...] = _pack_bf16_pairs(y)

    for p in range(n_buf):
        pl.when(j % n_buf == p)(functools.partial(run, p))

    @pl.when(nv == 0)
    def _():
        y_ref[...] = jnp.zeros_like(y_ref)


def _experts(layer, tile_ea, tile_eb, tile_nv, src, xh, w_gate, w_up, w_down):
    tm = MOE_TILE
    up = lambda sel: pl.BlockSpec((None, None, D_MODEL, D_EXPERT),
                                  lambda j, ea, eb, nv, sr: (layer, (ea, eb)[sel][j], 0, 0))
    down = lambda sel: pl.BlockSpec((None, None, D_EXPERT, D_MODEL),
                                    lambda j, ea, eb, nv, sr: (layer, (ea, eb)[sel][j], 0, 0))
    return pl.pallas_call(
        _expert_kernel,
        grid_spec=pltpu.PrefetchScalarGridSpec(
            num_scalar_prefetch=4,
            grid=(MOE_TILES,),
            in_specs=[pl.BlockSpec(memory_space=pl.ANY), up(0), up(0), down(0), up(1), up(1), down(1)],
            out_specs=pl.BlockSpec((tm, HALF_D), lambda j, ea, eb, nv, sr: (j, 0)),
            scratch_shapes=[pltpu.VMEM((tm, XH_W), jnp.uint32)] * (MOE_AHEAD + 1)
            + [pltpu.SemaphoreType.DMA((MOE_AHEAD + 1,))],
        ),
        out_shape=jax.ShapeDtypeStruct((T_PAD, HALF_D), jnp.uint32),
        compiler_params=_cp(("arbitrary",), 56),
        name="moe_experts",
    )(tile_ea, tile_eb, tile_nv, src, xh, w_gate, w_up, w_down, w_gate, w_up, w_down)


def _combine_kernel(dest_ref, x_ref, gt_ref, ys_hbm, o_ref, buf, sems):
    tm = DISPATCH_TILE
    i = pl.program_id(0)
    slot = i % 2

    last = pl.num_programs(0) - 1

    @pl.when(i == 0)
    def _():
        _gather_rows(dest_ref, 0, ys_hbm, buf.at[0], sems.at[0], tm)

    def finish():
        _wait_rows(ys_hbm, buf.at[slot], sems.at[slot], tm)
        o_ref[...] = x_ref[...] + gt_ref[...] * _unpack_bf16_pairs(buf[slot])

    @pl.when(i < last)
    def _():
        _gather_rows(dest_ref, (i + 1) * tm, ys_hbm, buf.at[1 - slot], sems.at[1 - slot], tm, static=True)
        finish()

    pl.when(i == last)(finish)


def _combine(dest, x, mod, ys):
    tm = DISPATCH_TILE
    return pl.pallas_call(
        _combine_kernel,
        grid_spec=pltpu.PrefetchScalarGridSpec(
            num_scalar_prefetch=1,
            grid=(T // tm,),
            in_specs=[pl.BlockSpec((tm, D_MODEL), lambda i, d: (i, 0)),
                      pl.BlockSpec((None, None, 1, D_MODEL), lambda i, d: (_cond_row(i * tm), 5, 0, 0)),
                      pl.BlockSpec(memory_space=pl.ANY)],
            out_specs=pl.BlockSpec((tm, D_MODEL), lambda i, d: (i, 0)),
            scratch_shapes=[pltpu.VMEM((2, tm, HALF_D), jnp.uint32), pltpu.SemaphoreType.DMA((2,))],
        ),
        out_shape=jax.ShapeDtypeStruct((T, D_MODEL), F32),
        compiler_params=_cp(("arbitrary",)),
        name="moe_combine",
    )(dest, x, mod, ys)


def _lookup(table, idx):
    n = table.shape[0]
    hit = idx[:, None] == jnp.arange(n, dtype=jnp.int32)[None, :]
    return jnp.sum(jnp.where(hit, table[None, :], 0), axis=1)


def _moe_plan(rt, cnt):
    bucket = rt[0].astype(jnp.int32)
    rank = rt[1].astype(jnp.int32)
    counts = cnt[:N_BUCKETS, 0].astype(jnp.int32)
    tiles = (counts + MOE_TILE - 1) // MOE_TILE
    order = jnp.arange(N_BUCKETS, dtype=jnp.int32)
    tile_start = jnp.sum(jnp.where(order[None, :] < order[:, None], tiles[None, :], 0), axis=1)
    tile_end = tile_start + tiles
    n_used = tile_end[N_BUCKETS - 1]
    dest = _lookup(tile_start * MOE_TILE, bucket) + rank
    j = jnp.arange(MOE_TILES, dtype=jnp.int32)
    jc = jnp.minimum(j, n_used - 1)
    b = jnp.minimum(jnp.sum((jc[:, None] >= tile_end[None, :]).astype(jnp.int32), axis=1), N_BUCKETS - 1)
    nv = jnp.clip(_lookup(counts, b) - (j - _lookup(tile_start, b)) * MOE_TILE, 0, MOE_TILE)
    nv = jnp.where(j < n_used, nv, 0)
    n_pairs = len(PAIR_LO)
    ea = (b // n_pairs) * EXPERTS_PER_GROUP + _lookup(jnp.asarray(PAIR_LO, jnp.int32), b % n_pairs)
    eb = (b // n_pairs) * EXPERTS_PER_GROUP + _lookup(jnp.asarray(PAIR_HI, jnp.int32), b % n_pairs)
    return dest, ea, eb, nv


def _moe(layer, x, g, mod, w_router, b_router, w_gate, w_up, w_down):
    xh, rt, cnt = _router(x, g, mod, w_router, b_router)
    dest, ea, eb, nv = _moe_plan(rt, cnt)
    ys = _experts(layer, ea, eb, nv, _invert(dest), xh, w_gate, w_up, w_down)
    return _combine(dest, x, mod, ys)


def _final_norm_kernel(x_ref, g_ref, o_ref):
    x = x_ref[...]
    o_ref[...] = x * lax.rsqrt(jnp.mean(x * x, axis=-1, keepdims=True) + NORM_EPS) * g_ref[...]


def _final_norm(x, g, row0, nrows):
    tm = 512
    blk0 = row0 // tm
    return pl.pallas_call(
        _final_norm_kernel,
        grid=(nrows // tm,),
        in_specs=[pl.BlockSpec((tm, D_MODEL), lambda i: (blk0 + i, 0)), pl.BlockSpec((1, D_MODEL), lambda i: (0, 0))],
        out_specs=pl.BlockSpec((tm, D_MODEL), lambda i: (i, 0)),
        out_shape=jax.ShapeDtypeStruct((nrows, D_MODEL), F32),
        compiler_params=_cp(("parallel",)),
        name="final_norm",
    )(x, g.reshape(1, D_MODEL))


def kernel(x_prompt, x_sample, cache_k_full, cache_v_full, cache_k_win, cache_v_win, c, c_ctx, w_mod, b_mod, norm_mix, norm_ffn, final_norm, pool_w, pool_scale, hy_w_in, hy_b_in, hy_conv_w, hy_conv_b, hy_f_w1, hy_f_b1, hy_f_w2, hy_f_b2, hy_f_freq, hy_f_w3, hy_decay, hy_skip, hy_w_out, hy_b_out, fa_w_qkv, fa_q_norm, fa_k_norm, fa_w_o, wa_w_qkv, wa_sink, wa_w_o, w_router, b_router, moe_w_gate, moe_w_up, moe_w_down):
    x = None
    cond =jnp.concatenate([c_ctx[None, :], c, jnp.zeros((N_COND - 1 - DEC_BATCH, D_MODEL), F32)], axis=0)
    mods = _adaln(cond, w_mod, b_mod).reshape(DEPTH, N_COND, 6, 1, D_MODEL)
    rope = _rope_tables()
    ones_hd = jnp.ones((HEAD_DIM,), F32)
    wg_bf, wu_bf, wd_bf = moe_w_gate.astype(BF16), moe_w_up.astype(BF16), moe_w_down.astype(BF16)
    new_kv = {}
    for layer in range(DEPTH):
        kind = layer % 4
        j = layer // 4
        mod = mods[layer]
        g_mix = norm_mix[layer].reshape(1, D_MODEL)
        if kind == 0:
            assert layer == 0, "the pooling mixer reads the two input streams, so it must be the first layer"
            x = _pool_mixer(x_prompt.reshape(TP, D_MODEL), x_sample.reshape(TS, D_MODEL), g_mix, mod,
                            pool_w[j], pool_scale[j])
        elif kind == 1:
            x = _hyena_mixer(x, g_mix, mod, hy_w_in[j], hy_b_in[j], hy_conv_w[j], hy_conv_b[j], hy_f_w1[j],
                             hy_f_b1[j], hy_f_w2[j], hy_f_b2[j], hy_f_freq[j], hy_f_w3[j], hy_decay[j],
                             hy_skip[j], hy_w_out[j], hy_b_out[j])
        elif kind == 2:
            x, nk, nv = _attn_mixer(x, g_mix, mod, fa_w_qkv[j], fa_q_norm[j], fa_k_norm[j], True, None,
                                    fa_w_o[j], cache_k_full[:, j], cache_v_full[:, j], False, rope)
            new_kv.setdefault("kf", []).append(nk)
            new_kv.setdefault("vf", []).append(nv)
        else:
            x, nk, nv = _attn_mixer(x, g_mix, mod, wa_w_qkv[j], ones_hd, ones_hd, False, wa_sink[j],
                                    wa_w_o[j], cache_k_win[:, j], cache_v_win[:, j], True, rope)
            new_kv.setdefault("kw", []).append(nk)
            new_kv.setdefault("vw", []).append(nv)
        x = _moe(layer, x, norm_ffn[layer].reshape(1, D_MODEL), mod, w_router, b_router, wg_bf, wu_bf, wd_bf)
    y_prompt = _final_norm(x, final_norm, 0, TP).reshape(BATCH, SEQ, D_MODEL)
    y_sample = _final_norm(x, final_norm, TP, TS).reshape(DEC_BATCH, DEC_SEQ, D_MODEL)
    return (y_prompt, y_sample, jnp.stack(new_kv["kf"], axis=1), jnp.stack(new_kv["vf"], axis=1),
            jnp.stack(new_kv["kw"], axis=1), jnp.stack(new_kv["vw"], axis=1))
```

```python
import functools
import math

import jax
import jax.numpy as jnp
import numpy as np
from jax import lax
from jax.experimental import pallas as pl
from jax.experimental.pallas import tpu as pltpu

D_MODEL = 2048
BATCH = 32
SEQ = 256
DEPTH = 4
DEC_BATCH = 4
DEC_SEQ = 4096
PAST_LEN = 512
GRID_W = 64
N_HEADS = 16
N_KV_HEADS = 4
HEAD_DIM = D_MODEL // N_HEADS
KV_GROUP = N_HEADS // N_KV_HEADS
KV_DIM = N_KV_HEADS * HEAD_DIM
QKV_DIM = (N_HEADS + 2 * N_KV_HEADS) * HEAD_DIM
ROPE_THETA = 10000.0
WINDOW = 128
POOL_WINDOWS = (2, 4, 8, 16)
POOL_GROUP = D_MODEL // len(POOL_WINDOWS)
HYENA_EMB_BANDS = 16
HYENA_FILTER_HIDDEN = 64
N_EXPERTS = 16
N_EXPERT_GROUPS = 4
EXPERTS_PER_GROUP = 4
D_EXPERT = 512
NORM_EPS = 1e-6
NEG_INF = -1e30

F32 = jnp.float32
BF16 = jnp.bfloat16

TP = BATCH * SEQ
TS = DEC_BATCH * DEC_SEQ
T = TP + TS
N_COND = 8
LANE = 128
MIB = 1024 * 1024

PAIR_LO = (0, 0, 0, 1, 1, 2)
PAIR_HI = (1, 2, 3, 2, 3, 3)
N_BUCKETS = N_EXPERT_GROUPS * len(PAIR_LO)
MOE_TILE = 256
MOE_AHEAD = 2
MOE_TILES = T // MOE_TILE + N_BUCKETS + MOE_AHEAD
T_PAD = MOE_TILES * MOE_TILE
HALF_D = D_MODEL // 2
XH_W = HALF_D + LANE


def _cp(sem, vmem_mb=48):
    return pltpu.CompilerParams(dimension_semantics=sem, vmem_limit_bytes=vmem_mb * MIB)


def _dot(a, b):
    return jnp.dot(a, b, preferred_element_type=F32)


def _dot3(a, b):
    ah = a.astype(BF16)
    al = (a - ah.astype(F32)).astype(BF16)
    bh = b.astype(BF16)
    bl = (b - bh.astype(F32)).astype(BF16)
    return _dot(ah, bh) + (_dot(al, bh) + _dot(ah, bl))


def _sigmoid(x):
    return 1.0 / (1.0 + jnp.exp(-x))


def _pack_bf16_pairs(x):
    n = x.shape[1] // 2
    bits = lambda v: lax.bitcast_convert_type(v.astype(BF16).astype(F32), jnp.uint32)
    return (bits(x[:, :n]) >> 16) | bits(x[:, n:])


def _unpack_bf16_pairs(u):
    lo = lax.bitcast_convert_type(u << 16, F32)
    hi = lax.bitcast_convert_type(u & jnp.uint32(0xFFFF0000), F32)
    return jnp.concatenate([lo, hi], axis=1)


def _cond_row(r):
    return jnp.where(r < TP, 0, 1 + (r - TP) // DEC_SEQ)


def _mod_spec(tm, chunk, tn=D_MODEL, ncol=False):
    if ncol:
        return pl.BlockSpec((None, None, 1, tn), lambda i, j: (_cond_row(i * tm), chunk, 0, j))
    return pl.BlockSpec((None, None, 1, tn), lambda i, *_: (_cond_row(i * tm), chunk, 0, 0))


def _norm_mod(x, g, shift, scale):
    var = jnp.mean(x * x, axis=-1, keepdims=True)
    y = x * lax.rsqrt(var + NORM_EPS) * g
    return y * (1.0 + scale) + shift


def _adaln_kernel(c_ref, w_ref, b_ref, o_ref):
    c = c_ref[...]
    a = c * _sigmoid(c)
    o_ref[...] = _dot3(a, w_ref[...]) + b_ref[...]


def _adaln(cond, w_mod, b_mod):
    tn = 1024
    n = 6 * D_MODEL
    return pl.pallas_call(
        _adaln_kernel,
        grid=(DEPTH, n // tn),
        in_specs=[
            pl.BlockSpec((N_COND, D_MODEL), lambda l, j: (0, 0)),
            pl.BlockSpec((None, D_MODEL, tn), lambda l, j: (l, 0, j)),
            pl.BlockSpec((None, 1, tn), lambda l, j: (l, 0, j)),
        ],
        out_specs=pl.BlockSpec((None, N_COND, tn), lambda l, j: (l, 0, j)),
        out_shape=jax.ShapeDtypeStruct((DEPTH, N_COND, n), F32),
        compiler_params=_cp(("parallel", "parallel")),
        name="adaln",
    )(cond, w_mod, b_mod.reshape(DEPTH, 1, n))


def _nm_matmul_kernel(x_ref, g_ref, sh_ref, sc_ref, w_ref, b_ref, o_ref, h_scr):
    @pl.when(pl.program_id(1) == 0)
    def _():
        h_scr[...] = _norm_mod(x_ref[...], g_ref[...], sh_ref[...], sc_ref[...]).astype(BF16)

    o_ref[...] = (_dot(h_scr[...], w_ref[...]) + b_ref[...]).astype(o_ref.dtype)


def _nm_matmul(x, g, mod, w, b, out_dtype, name):
    tm, tn = 1024, 1024
    n = w.shape[1]
    return pl.pallas_call(
        _nm_matmul_kernel,
        grid=(T // tm, n // tn),
        in_specs=[
            pl.BlockSpec((tm, D_MODEL), lambda i, j: (i, 0)),
            pl.BlockSpec((1, D_MODEL), lambda i, j: (0, 0)),
            _mod_spec(tm, 0),
            _mod_spec(tm, 1),
            pl.BlockSpec((D_MODEL, tn), lambda i, j: (0, j)),
            pl.BlockSpec((1, tn), lambda i, j: (0, j)),
        ],
        out_specs=pl.BlockSpec((tm, tn), lambda i, j: (i, j)),
        out_shape=jax.ShapeDtypeStruct((T, n), out_dtype),
        scratch_shapes=[pltpu.VMEM((tm, D_MODEL), BF16)],
        compiler_params=_cp(("parallel", "arbitrary")),
        name=name,
    )(x, g, mod, mod, w, b)


RESID_TM = 1024


def _resid_matmul_kernel(ap_ref, as_ref, w_ref, b_ref, x_ref, gt_ref, o_ref):
    def emit(a_ref):
        o_ref[...] = x_ref[...] + gt_ref[...] * (_dot(a_ref[...], w_ref[...]) + b_ref[...])

    is_ctx = pl.program_id(0) < TP // RESID_TM
    pl.when(is_ctx)(lambda: emit(ap_ref))
    pl.when(jnp.logical_not(is_ctx))(lambda: emit(as_ref))


def _resid_matmul(a_ctx, a_lat, w, b, x, mod, name):
    tm, tn = RESID_TM, 1024
    k = a_ctx.shape[1]
    n_ctx = TP // tm
    return pl.pallas_call(
        _resid_matmul_kernel,
        grid=(T // tm, D_MODEL // tn),
        in_specs=[
            pl.BlockSpec((tm, k), lambda i, j: (jnp.minimum(i, n_ctx - 1), 0)),
            pl.BlockSpec((tm, k), lambda i, j: (jnp.maximum(i - n_ctx, 0), 0)),
            pl.BlockSpec((k, tn), lambda i, j: (0, j)),
            pl.BlockSpec((1, tn), lambda i, j: (0, j)),
            pl.BlockSpec((tm, tn), lambda i, j: (i, j)),
            _mod_spec(tm, 2, tn, ncol=True),
        ],
        out_specs=pl.BlockSpec((tm, tn), lambda i, j: (i, j)),
        out_shape=jax.ShapeDtypeStruct((T, D_MODEL), F32),
        compiler_params=_cp(("parallel", "parallel")),
        name=name,
    )(a_ctx, a_lat, w, b, x, mod)


POOL_TILE = 256
POOL_HALO = 8


def _seq_pos(r0):
    is_ctx = r0 < TP
    loc0 = jnp.where(is_ctx, r0 % SEQ, (r0 - TP) % DEC_SEQ)
    seq_len = jnp.where(is_ctx, SEQ, DEC_SEQ)
    return loc0, seq_len


def _pool_kernel(xc_ref, xcp_ref, xcn_ref, xl_ref, xlp_ref, xln_ref, *rest):
    is_ctx = pl.program_id(0) < TP // POOL_TILE
    pl.when(is_ctx)(lambda: _pool_tile(xc_ref, xcp_ref, xcn_ref, *rest))
    pl.when(jnp.logical_not(is_ctx))(lambda: _pool_tile(xl_ref, xlp_ref, xln_ref, *rest))


def _pool_tile(x_ref, xp_ref, xn_ref, g_ref, sh_ref, sc_ref, gt_ref, pw_ref, ps_ref, o_ref, hz_scr):
    tm, hl = POOL_TILE, POOL_HALO
    loc0, seq_len = _seq_pos(pl.program_id(0) * tm)
    has_prev = loc0 > 0
    has_next = loc0 + tm < seq_len
    g, sh, sc = g_ref[...], sh_ref[...], sc_ref[...]
    x = x_ref[...]
    h = _norm_mod(x, g, sh, sc)
    hz_scr[0:hl, :] = jnp.where(has_prev, _norm_mod(xp_ref[...], g, sh, sc), 0.0)
    hz_scr[hl:hl + tm, :] = h
    hz_scr[hl + tm:, :] = jnp.where(has_next, _norm_mod(xn_ref[...], g, sh, sc), 0.0)
    tl = loc0 + lax.broadcasted_iota(jnp.int32, (tm, 1), 0)
    outs = []
    for gi, w in enumerate(POOL_WINDOWS):
        cs = slice(gi * POOL_GROUP, (gi + 1) * POOL_GROUP)
        s = jnp.zeros((tm, POOL_GROUP), F32)
        for off in range(-(w // 2), w - w // 2):
            s = s + hz_scr[hl + off:hl + off + tm, cs]
        lo = jnp.maximum(tl - w // 2, 0)
        hi = jnp.minimum(tl + (w - w // 2), seq_len)
        d = s / (hi - lo).astype(F32) - h[:, cs]
        outs.append(_dot(d.astype(BF16), pw_ref[gi]))
    out = jnp.concatenate(outs, axis=1) * ps_ref[...]
    o_ref[...] = x + gt_ref[...] * out


def _pool_mixer(x_ctx, x_lat, g, mod, pool_w, pool_scale):
    tm, hl = POOL_TILE, POOL_HALO
    r = tm // hl

    def stream(first_tile, rows):
        tile = lambda i: jnp.clip(i - first_tile, 0, rows // tm - 1)
        return [pl.BlockSpec((tm, D_MODEL), lambda i: (tile(i), 0)),
                pl.BlockSpec((hl, D_MODEL), lambda i: (jnp.maximum(tile(i) * r - 1, 0), 0)),
                pl.BlockSpec((hl, D_MODEL), lambda i: (jnp.minimum((tile(i) + 1) * r, rows // hl - 1), 0))]

    return pl.pallas_call(
        _pool_kernel,
        grid=(T // tm,),
        in_specs=stream(0, TP) + stream(TP // tm, TS) + [
            pl.BlockSpec((1, D_MODEL), lambda i: (0, 0)),
            _mod_spec(tm, 0),
            _mod_spec(tm, 1),
            _mod_spec(tm, 2),
            pl.BlockSpec((len(POOL_WINDOWS), POOL_GROUP, POOL_GROUP), lambda i: (0, 0, 0)),
            pl.BlockSpec((1, D_MODEL), lambda i: (0, 0)),
        ],
        out_specs=pl.BlockSpec((tm, D_MODEL), lambda i: (i, 0)),
        out_shape=jax.ShapeDtypeStruct((T, D_MODEL), F32),
        scratch_shapes=[pltpu.VMEM((tm + 2 * hl, D_MODEL), F32)],
        compiler_params=_cp(("parallel",)),
        name="pool_mixer",
    )(x_ctx, x_ctx, x_ctx, x_lat, x_lat, x_lat, g, mod, mod, mod, pool_w.astype(BF16),
      pool_scale.reshape(1, D_MODEL))


CONV_TILE = 256
CONV_HALO = 16


def _conv3_kernel(u_ref, up_ref, un_ref, cw_ref, cb_ref, o_ref, scr):
    tm, hl = CONV_TILE, CONV_HALO
    loc0, seq_len = _seq_pos(pl.program_id(0) * tm)
    has_prev = loc0 > 0
    has_next = loc0 + tm < seq_len
    scr[0:hl, :] = jnp.where(has_prev, up_ref[...].astype(F32), 0.0)
    scr[hl:hl + tm, :] = u_ref[...].astype(F32)
    scr[hl + tm:, :] = jnp.where(has_next, un_ref[...].astype(F32), 0.0)
    out = (scr[hl - 1:hl - 1 + tm, :] * cw_ref[0:1, :] + scr[hl:hl + tm, :] * cw_ref[1:2, :]
           + scr[hl + 1:hl + 1 + tm, :] * cw_ref[2:3, :] + cb_ref[...])
    o_ref[...] = out.astype(o_ref.dtype)


def _conv3(u0, conv_w, conv_b):
    tm, hl, tc = CONV_TILE, CONV_HALO, D_MODEL
    r = tm // hl
    n = u0.shape[1]
    return pl.pallas_call(
        _conv3_kernel,
        grid=(T // tm, n // tc),
        in_specs=[
            pl.BlockSpec((tm, tc), lambda i, j: (i, j)),
            pl.BlockSpec((hl, tc), lambda i, j: (jnp.maximum(i * r - 1, 0), j)),
            pl.BlockSpec((hl, tc), lambda i, j: (jnp.minimum((i + 1) * r, T // hl - 1), j)),
            pl.BlockSpec((3, tc), lambda i, j: (0, j)),
            pl.BlockSpec((1, tc), lambda i, j: (0, j)),
        ],
        out_specs=pl.BlockSpec((tm, tc), lambda i, j: (i, j)),
        out_shape=jax.ShapeDtypeStruct((T, n), BF16),
        scratch_shapes=[pltpu.VMEM((tm + 2 * hl, tc), F32)],
        compiler_params=_cp(("parallel", "parallel")),
        name="hyena_conv3",
    )(u0, u0, u0, conv_w, conv_b.reshape(1, n))


FILT_TILE = 256


HYENA_BLOCK = 1024


T_LANE = LANE - 1


def _filter_mlp_kernel(emb_ref, w1_ref, b1_ref, w2_ref, b2_ref, fr_ref, o_ref):
    emb = emb_ref[...]
    fr = fr_ref[...]
    a = jnp.sin(fr * (_dot3(emb, w1_ref[...]) + b1_ref[...]))
    a = jnp.sin(fr * (_dot3(a, w2_ref[...]) + b2_ref[...]))
    lane = lax.broadcasted_iota(jnp.int32, a.shape, 1)
    o_ref[...] = jnp.where(lane == T_LANE, emb[:, 0:1], a)


def _filter_mlp(pos, L, f_w1, f_b1, f_w2, f_b2, f_freq):
    assert HYENA_FILTER_HIDDEN <= T_LANE
    tl = FILT_TILE
    rows = pos.shape[0]
    small = lambda: pl.BlockSpec((LANE, LANE), lambda i: (0, 0))
    vec = lambda: pl.BlockSpec((1, LANE), lambda i: (0, 0))
    return pl.pallas_call(
        _filter_mlp_kernel,
        grid=(rows // tl,),
        in_specs=[pl.BlockSpec((tl, LANE), lambda i: (i, 0)), small(), vec(), small(), vec(), vec()],
        out_specs=pl.BlockSpec((tl, LANE), lambda i: (i, 0)),
        out_shape=jax.ShapeDtypeStruct((rows, LANE), F32),
        compiler_params=_cp(("parallel",)),
        name="hyena_filter_mlp",
    )(_filter_embedding(pos, L), _pad2(f_w1, LANE, LANE), _pad2(f_b1[None], 1, LANE),
      _pad2(f_w2, LANE, LANE), _pad2(f_b2[None], 1, LANE), _pad2(f_freq[None], 1, LANE))


def _filter_kernel(h1_ref, h2_ref, w3a_ref, dca_ref, w3b_ref, dcb_ref, fa_ref, fb_ref, *, blk):
    def taps(h, w3_ref, dc_ref):
        return _dot3(h, w3_ref[...]) * jnp.exp(-h[:, T_LANE:] * jnp.abs(dc_ref[...]))

    pos = taps(h1_ref[...], w3a_ref, dca_ref)
    neg = taps(h2_ref[...], w3b_ref, dcb_ref)
    m = (pl.program_id(0) * FILT_TILE + lax.broadcasted_iota(jnp.int32, (FILT_TILE, 1), 0)) % blk
    fa_ref[...] = jnp.where(m == 0, pos, pos + neg).astype(BF16)
    fb_ref[...] = jnp.where(m == 0, 0.0, neg - pos).astype(BF16)


def _pad2(a, rows, cols):
    return jnp.pad(a, ((0, rows - a.shape[0]), (0, cols - a.shape[1])))


def _filter_positions(L, blk):
    n_blk = L // blk
    m = np.arange(blk)
    p1, p2 = [], []
    for d in range(-(n_blk - 1), n_blk):
        if d >= 1:
            p1.append(d * blk + m), p2.append(d * blk - m)
        elif d == 0:
            p1.append(m), p2.append(m)
        else:
            p1.append(-d * blk - m), p2.append(-d * blk + m)
    return np.concatenate(p1), np.concatenate(p2)


def _filter_embedding(pos, L):
    t = jnp.asarray(pos, F32) / L
    bands = jnp.linspace(1e-4, HYENA_EMB_BANDS - 1, HYENA_EMB_BANDS, dtype=F32)
    ang = (2 * math.pi) * t[:, None] * bands[None, :]
    return _pad2(jnp.concatenate([t[:, None], jnp.cos(ang), -jnp.sin(ang)], axis=-1), pos.shape[0], LANE)


def _hyena_filters(L, blk, f_w1, f_b1, f_w2, f_b2, f_freq, f_w3, decay):
    n_blk = L // blk
    p1, p2 = _filter_positions(L, blk)
    rows = p1.shape[0]
    tl = FILT_TILE
    tiles_per_lag = blk // tl
    lag = lambda i: i // tiles_per_lag - (n_blk - 1)
    col1 = lambda i, n: 2 * n + jnp.where(lag(i) >= 0, 0, 1)
    col2 = lambda i, n: 2 * n + jnp.where(lag(i) >= 1, 0, 1)
    n_tiles = rows // tl
    out = pl.BlockSpec((None, tl, D_MODEL), lambda i, n: (n, i, 0))
    w3 = _pad2(f_w3, LANE, f_w3.shape[1])
    hidden = _filter_mlp(np.concatenate([p1, p2]), L, f_w1, f_b1, f_w2, f_b2, f_freq)
    return pl.pallas_call(
        functools.partial(_filter_kernel, blk=blk),
        grid=(n_tiles, 2),
        in_specs=[
            pl.BlockSpec((tl, LANE), lambda i, n: (i, 0)),
            pl.BlockSpec((tl, LANE), lambda i, n: (n_tiles + i, 0)),
            pl.BlockSpec((LANE, D_MODEL), lambda i, n: (0, col1(i, n))),
            pl.BlockSpec((1, D_MODEL), lambda i, n: (0, col1(i, n))),
            pl.BlockSpec((LANE, D_MODEL), lambda i, n: (0, col2(i, n))),
            pl.BlockSpec((1, D_MODEL), lambda i, n: (0, col2(i, n))),
        ],
        out_specs=[out, out],
        out_shape=[jax.ShapeDtypeStruct((2, rows, D_MODEL), BF16)] * 2,
        compiler_params=_cp(("parallel", "parallel")),
        name="hyena_filters",
    )(hidden, hidden, w3, decay[None], w3, decay[None])


def _dft_mats(L):
    r = int(math.isqrt(L))
    k2 = 2 * jnp.arange(L, dtype=jnp.int32)[:, None] + 1
    n1 = r * jnp.arange(L // r, dtype=jnp.int32)[None, :]
    n2 = jnp.arange(r, dtype=jnp.int32)[None, :]
    sc = math.pi / (2 * L)
    aa = ((k2 * n1) % (4 * L)).astype(F32) * sc
    ab = ((k2 * n2) % (4 * L)).astype(F32) * sc
    ca, sa, cb, sb = jnp.cos(aa)[:, :, None], jnp.sin(aa)[:, :, None], jnp.cos(ab)[:, None, :], jnp.sin(ab)[:, None, :]
    c = (ca * cb - sa * sb).reshape(L, L)
    s = (sa * cb + ca * sb).reshape(L, L)
    return c.astype(BF16), s.astype(BF16), c.T.astype(BF16), s.T.astype(BF16)


def _dft_tiles(L):
    return min(512, L), 512


def _dft_filter_kernel(c_ref, s_ref, a_ref, b_ref, gr_ref, gi_ref):
    gr_ref[...] = _dot(c_ref[...], a_ref[...]).astype(gr_ref.dtype)
    gi_ref[...] = _dot(s_ref[...], b_ref[...]).astype(gi_ref.dtype)


def _dft_filter(cm, sm, fa, fb, L):
    tf, tn = _dft_tiles(L)
    n = fa.shape[0]
    mat = lambda: pl.BlockSpec((tf, L), lambda k, c, s: (k, 0))
    rhs = lambda: pl.BlockSpec((None, L, tn), lambda k, c, s: (s, 0, c))
    out = pl.BlockSpec((None, tf, tn), lambda k, c, s: (s, k, c))
    return pl.pallas_call(
        _dft_filter_kernel,
        grid=(L // tf, D_MODEL // tn, n),
        in_specs=[mat(), mat(), rhs(), rhs()],
        out_specs=[out, out],
        out_shape=[jax.ShapeDtypeStruct((n, L, D_MODEL), BF16)] * 2,
        compiler_params=_cp(("parallel", "parallel", "parallel")),
        name="hyena_filter_dft",
    )(cm, sm, fa, fb)


FWD_TF = 256


def _dft_fwd_kernel(c_ref, s_ref, z_ref, gr_ref, gi_ref, yr_ref, yi_ref, *, n_blk, blk, bpb):
    c, s = c_ref[...], s_ref[...]
    for bb in range(bpb):
        zc, zs = [], []
        for j in range(n_blk):
            r = (bb * n_blk + j) * blk
            zj = z_ref[r:r + blk, :]
            zc.append(_dot(c, zj).astype(BF16))
            zs.append(_dot(s, zj).astype(BF16))
        for i in range(n_blk):
            yr = yi = None
            for j in range(n_blk):
                lag = i - j + n_blk - 1
                gr, gi = gr_ref[lag], gi_ref[lag]
                tr = gr * zc[j] + gi * zs[j]
                ti = gi * zc[j] - gr * zs[j]
                yr = tr if yr is None else yr + tr
                yi = ti if yi is None else yi + ti
            yr_ref[bb, i] = yr.astype(BF16)
            yi_ref[bb, i] = yi.astype(BF16)


def _seqs_per_step(L):
    return max(1, 2048 // L)


def _dft_fwd(cm, sm, z, z_rowblk, z_colblk, gr, gi, order, nb, L, blk):
    n_blk = L // blk
    bpb = _seqs_per_step(L)
    assert nb % bpb == 0 and z_rowblk % bpb == 0
    tf, tn = min(FWD_TF, blk), 512
    mat = lambda: pl.BlockSpec((tf, blk), lambda k, c, b: (k, 0))
    gsp = lambda: pl.BlockSpec((None, 2 * n_blk - 1, tf, tn), lambda k, c, b: (order, 0, k, c))
    out = pl.BlockSpec((bpb, n_blk, tf, tn), lambda k, c, b: (b, 0, k, c))
    return pl.pallas_call(
        functools.partial(_dft_fwd_kernel, n_blk=n_blk, blk=blk, bpb=bpb),
        grid=(blk // tf, D_MODEL // tn, nb // bpb),
        in_specs=[mat(), mat(),
                  pl.BlockSpec((bpb * L, tn), lambda k, c, b: (z_rowblk // bpb + b, z_colblk + c)),
                  gsp(), gsp()],
        out_specs=[out, out],
        out_shape=[jax.ShapeDtypeStruct((nb, n_blk, blk, D_MODEL), BF16)] * 2,
        compiler_params=_cp(("parallel", "parallel", "parallel")),
        name="hyena_dft_fwd",
    )(cm, sm, z, gr, gi)


def _dft_inv_kernel(ct_ref, st_ref, yr_ref, yi_ref, z_ref, gt_ref, sk_ref, o_ref, *, inv_len, bpb, tt):
    for bb in range(bpb):
        rows = slice(bb * tt, (bb + 1) * tt)
        y = (_dot(ct_ref[...], yr_ref[bb]) - _dot(st_ref[...], yi_ref[bb])) * inv_len
        o_ref[rows, :] = (gt_ref[rows, :].astype(F32) * (y + sk_ref[...] * z_ref[rows, :].astype(F32))).astype(BF16)


def _dft_inv(ctm, stm, yr, yi, z, z_rowblk, z_colblk, gate, g_rowblk, g_colblk, skip, nb, L):
    tt, tn = _dft_tiles(L)
    rpb = L // tt
    bpb = _seqs_per_step(L) if rpb == 1 else 1
    assert nb % bpb == 0 and z_rowblk % bpb == 0 and g_rowblk % bpb == 0
    mat = lambda: pl.BlockSpec((tt, L), lambda t, c, b: (t, 0))
    spec = lambda: pl.BlockSpec((bpb, L, tn), lambda t, c, b: (b, 0, c))
    rows = lambda blk0: (lambda t, c, b: ((blk0 + b * bpb * rpb + t) // bpb))
    return pl.pallas_call(
        functools.partial(_dft_inv_kernel, inv_len=1.0 / L, bpb=bpb, tt=tt),
        grid=(rpb, D_MODEL // tn, nb // bpb),
        in_specs=[mat(), mat(), spec(), spec(),
                  pl.BlockSpec((bpb * tt, tn), lambda t, c, b: (rows(z_rowblk)(t, c, b), z_colblk + c)),
                  pl.BlockSpec((bpb * tt, tn), lambda t, c, b: (rows(g_rowblk)(t, c, b), g_colblk + c)),
                  pl.BlockSpec((1, tn), lambda t, c, b: (0, c))],
        out_specs=pl.BlockSpec((bpb * tt, tn), lambda t, c, b: (rows(0)(t, c, b), c)),
        out_shape=jax.ShapeDtypeStruct((nb * L, D_MODEL), BF16),
        compiler_params=_cp(("parallel", "parallel", "parallel")),
        name="hyena_dft_inv",
    )(ctm, stm, yr, yi, z, gate, skip)


def _hyena_stream(u, row0, nb, L, fparams, skip):
    blk = min(HYENA_BLOCK, L)
    n_blk = L // blk
    n_lag = 2 * n_blk - 1
    cm, sm, ctm, stm = _dft_mats(blk)
    fa, fb = _hyena_filters(L, blk, *fparams)
    seg = lambda a: a.reshape(2 * n_lag, blk, D_MODEL)
    gr, gi = _dft_filter(cm, sm, seg(fa), seg(fb), blk)
    gr, gi = (a.reshape(2, n_lag, blk, D_MODEL) for a in (gr, gi))
    tt, tn = _dft_tiles(blk)
    ncb = D_MODEL // tn
    blocks = lambda a: a.reshape(nb * n_blk, blk, D_MODEL)

    yr, yi = _dft_fwd(cm, sm, u, row0 // L, 0, gr, gi, 0, nb, L, blk)
    z1 = _dft_inv(ctm, stm, blocks(yr), blocks(yi), u, row0 // tt, 0, u, row0 // tt, ncb, skip[0:1],
                  nb * n_blk, blk)
    yr, yi = _dft_fwd(cm, sm, z1, 0, 0, gr, gi, 1, nb, L, blk)
    return _dft_inv(ctm, stm, blocks(yr), blocks(yi), z1, 0, 0, u, row0 // tt, 2 * ncb, skip[1:2],
                    nb * n_blk, blk)


def _hyena_mixer(x, g, mod, w_in, b_in, conv_w, conv_b, f_w1, f_b1, f_w2, f_b2, f_freq, f_w3, decay, skip,
                 w_out, b_out):
    u0 = _nm_matmul(x, g, mod, w_in.astype(BF16), b_in.reshape(1, -1), BF16, "hyena_in_proj")
    u = _conv3(u0, conv_w, conv_b)
    fparams = (f_w1, f_b1, f_w2, f_b2, f_freq, f_w3, decay)
    zp = _hyena_stream(u, 0, BATCH, SEQ, fparams, skip)
    zs = _hyena_stream(u, TP, DEC_BATCH, DEC_SEQ, fparams, skip)
    return _resid_matmul(zp, zs, w_out.astype(BF16), b_out.reshape(1, -1), x, mod, "hyena_out_proj")


def _rope_tables():
    pos = jnp.arange(DEC_SEQ, dtype=jnp.int32)
    row = (pos // GRID_W).astype(F32)
    col = (pos % GRID_W).astype(F32)
    axis_dim = HEAD_DIM // 2
    inv_freq = ROPE_THETA ** (-jnp.arange(0, axis_dim, 2, dtype=F32) / axis_dim)
    ar = row[:, None] * inv_freq[None, :]
    ac = col[:, None] * inv_freq[None, :]
    cos = jnp.concatenate([jnp.cos(ar), jnp.cos(ar), jnp.cos(ac), jnp.cos(ac)], axis=-1)
    sin = jnp.concatenate([-jnp.sin(ar), jnp.sin(ar), -jnp.sin(ac), jnp.sin(ac)], axis=-1)
    return cos, sin


QKV_TM = 512
QKV_TN = 1024
PAIR = 2 * HEAD_DIM


def _qkv_kernel(x_ref, g_ref, sh_ref, sc_ref, w_ref, qn_ref, kn_ref, cos_ref, sin_ref,
                q_ref, k_ref, v_ref, nk_ref, nv_ref, h_scr, *, use_norm):
    tm = QKV_TM
    i, j = pl.program_id(0), pl.program_id(1)

    @pl.when(j == 0)
    def _():
        h_scr[...] = _norm_mod(x_ref[...], g_ref[...], sh_ref[...], sc_ref[...]).astype(BF16)

    quarter = HEAD_DIM // 4
    scale = HEAD_DIM ** -0.5 * LOG2E

    def head(xh, gn):
        if use_norm:
            xh = xh * lax.rsqrt(jnp.mean(xh * xh, axis=-1, keepdims=True) + NORM_EPS) * gn
        return xh

    def rope(xh):
        lane = lax.broadcasted_iota(jnp.int32, (tm, HEAD_DIM), 1)
        first = (lane % (2 * quarter)) < quarter
        partner = jnp.where(first, pltpu.roll(xh, HEAD_DIM - quarter, 1), pltpu.roll(xh, quarter, 1))
        return xh * cos_ref[...] + partner * sin_ref[...]

    def proj(c0):
        return _dot(h_scr[...], w_ref[:, c0:c0 + PAIR])

    def q_tile(latent):
        for p in range(QKV_TN // PAIR):
            acc = proj(p * PAIR)
            for t in range(2):
                xh = head(acc[:, t * HEAD_DIM:(t + 1) * HEAD_DIM], qn_ref[...])
                xh = rope(xh) if latent else xh
                c0 = p * PAIR + t * HEAD_DIM
                q_ref[:, c0:c0 + HEAD_DIM] = (xh * scale).astype(BF16)

    def kv_tile(latent):
        for p in range(KV_DIM // PAIR):
            acc = proj(p * PAIR)
            for t in range(2):
                c0 = p * PAIR + t * HEAD_DIM
                kh = head(acc[:, t * HEAD_DIM:(t + 1) * HEAD_DIM], kn_ref[...])
                if not latent:
                    nk_ref[:, c0:c0 + HEAD_DIM] = kh
                k_ref[:, c0:c0 + HEAD_DIM] = (rope(kh) if latent else kh).astype(BF16)
        for p in range(KV_DIM // PAIR):
            acc = proj(KV_DIM + p * PAIR)
            if not latent:
                nv_ref[:, p * PAIR:(p + 1) * PAIR] = acc
            v_ref[:, p * PAIR:(p + 1) * PAIR] = acc.astype(BF16)

    is_ctx = i < TP // tm
    is_q = j < D_MODEL // QKV_TN
    for latent in (False, True):
        stream = jnp.logical_not(is_ctx) if latent else is_ctx
        pl.when(jnp.logical_and(stream, is_q))(functools.partial(q_tile, latent))
        pl.when(jnp.logical_and(stream, jnp.logical_not(is_q)))(functools.partial(kv_tile, latent))


def _qkv_proj(x, g, mod, w_qkv, q_norm, k_norm, use_norm, rope):
    tm, tn = QKV_TM, QKV_TN
    n_ctx = TP // tm
    n_q = D_MODEL // tn
    tab = lambda: pl.BlockSpec((tm, HEAD_DIM), lambda i, j: (jnp.maximum(i - n_ctx, 0) % (DEC_SEQ // tm), 0))
    kv = lambda: pl.BlockSpec((tm, KV_DIM), lambda i, j: (i, 0))
    new = lambda: pl.BlockSpec((tm, KV_DIM), lambda i, j: (jnp.minimum(i, n_ctx - 1), 0))
    return pl.pallas_call(
        functools.partial(_qkv_kernel, use_norm=use_norm),
        grid=(T // tm, QKV_DIM // tn),
        in_specs=[
            pl.BlockSpec((tm, D_MODEL), lambda i, j: (i, 0)),
            pl.BlockSpec((1, D_MODEL), lambda i, j: (0, 0)),
            _mod_spec(tm, 0),
            _mod_spec(tm, 1),
            pl.BlockSpec((D_MODEL, tn), lambda i, j: (0, j)),
            pl.BlockSpec((1, HEAD_DIM), lambda i, j: (0, 0)),
            pl.BlockSpec((1, HEAD_DIM), lambda i, j: (0, 0)),
            tab(), tab(),
        ],
        out_specs=[pl.BlockSpec((tm, tn), lambda i, j: (i, jnp.minimum(j, n_q - 1))), kv(), kv(), new(), new()],
        out_shape=[jax.ShapeDtypeStruct((T, D_MODEL), BF16), jax.ShapeDtypeStruct((T, KV_DIM), BF16),
                   jax.ShapeDtypeStruct((T, KV_DIM), BF16), jax.ShapeDtypeStruct((TP, KV_DIM), F32),
                   jax.ShapeDtypeStruct((TP, KV_DIM), F32)],
        scratch_shapes=[pltpu.VMEM((tm, D_MODEL), BF16)],
        compiler_params=_cp(("arbitrary", "arbitrary")),
        name="qkv_proj",
    )(x, g, mod, mod, w_qkv, q_norm.reshape(1, HEAD_DIM), k_norm.reshape(1, HEAD_DIM), *rope)


LOG2E = math.log2(math.e)
ATTN_TQ = 256


def _attn_kernel(*refs, tq, seq_len, has_ctx, windowed, has_sink):
    it = iter(refs)
    q_ref, k_ref, v_ref = next(it), next(it), next(it)
    kc_ref, vc_ref = (next(it), next(it)) if has_ctx else (None, None)
    sink_ref = next(it) if has_sink else None
    o_ref = next(it)
    if windowed:
        i = pl.program_id(2)
        span = tq + 2 * WINDOW
        start = pl.multiple_of(jnp.clip(i * tq - WINDOW, 0, seq_len - span), WINDOW)
        qpos = i * tq + lax.broadcasted_iota(jnp.int32, (tq, 1), 0)
        kpos = start + lax.broadcasted_iota(jnp.int32, (1, span), 1)
        segs = [(k_ref, v_ref, pl.ds(start, span), jnp.abs(kpos - qpos) <= WINDOW)]
    else:
        segs = [(k_ref, v_ref, slice(None), None)]
    if has_ctx:
        segs.append((kc_ref, vc_ref, slice(None), None))
    for h in range(KV_GROUP):
        hs = slice(h * HEAD_DIM, (h + 1) * HEAD_DIM)
        qh = q_ref[:, hs]
        scores = []
        m = None
        for kr, _, rows, mask in segs:
            s = lax.dot_general(qh, kr[rows, :], (((1,), (1,)), ((), ())), preferred_element_type=F32)
            if mask is not None:
                s = jnp.where(mask, s, NEG_INF)
            scores.append(s)
            ms = jnp.max(s, axis=-1, keepdims=True)
            m = ms if m is None else jnp.maximum(m, ms)
        if has_sink:
            sk = sink_ref[pl.program_id(1) * KV_GROUP + h]
            m = jnp.maximum(m, sk)
        l = jnp.exp2(sk - m) if has_sink else jnp.zeros_like(m)
        acc = jnp.zeros((tq, HEAD_DIM), F32)
        for (_, vr, rows, _), s in zip(segs, scores):
            p = jnp.exp2(s - m)
            l = l + jnp.sum(p, axis=-1, keepdims=True)
            acc = acc + _dot(p.astype(BF16), vr[rows, :])
        o_ref[:, hs] = (acc / l).astype(BF16)


def _attention(q, row0, k, v, k_row0, n_keys, k_ctx, v_ctx, sink, nb, L, windowed):
    tq = min(ATTN_TQ, L)
    nq = L // tq
    q_blk0 = row0 // tq
    seq0 = k_row0 // n_keys
    own = lambda: pl.BlockSpec((n_keys, HEAD_DIM), lambda b, g, i: (seq0 + b, g))
    in_specs = [pl.BlockSpec((tq, KV_GROUP * HEAD_DIM), lambda b, g, i: (q_blk0 + b * nq + i, g)), own(), own()]
    args = [q, k, v]
    if k_ctx is not None:
        ctx = lambda: pl.BlockSpec((None, PAST_LEN, HEAD_DIM), lambda b, g, i: (b, 0, g))
        in_specs += [ctx(), ctx()]
        args += [k_ctx, v_ctx]
    if sink is not None:
        in_specs.append(pl.BlockSpec(memory_space=pltpu.SMEM))
        args.append(sink.astype(F32) * LOG2E)
    return pl.pallas_call(
        functools.partial(_attn_kernel, tq=tq, seq_len=L, has_ctx=k_ctx is not None, windowed=windowed,
                          has_sink=sink is not None),
        grid=(nb, N_KV_HEADS, nq),
        in_specs=in_specs,
        out_specs=pl.BlockSpec((tq, KV_GROUP * HEAD_DIM), lambda b, g, i: (b * nq + i, g)),
        out_shape=jax.ShapeDtypeStruct((nb * L, D_MODEL), BF16),
        compiler_params=_cp(("parallel", "parallel", "parallel"), 56),
        name="attention",
    )(*args)


def _attn_mixer(x, g, mod, w_qkv, q_norm, k_norm, use_norm, sink, w_o, cache_k, cache_v, windowed, rope):
    q, k, v, new_k, new_v = _qkv_proj(x, g, mod, w_qkv.astype(BF16), q_norm, k_norm, use_norm, rope)
    op = _attention(q, 0, k, v, 0, SEQ, None, None, sink, BATCH, SEQ, False)
    kc = cache_k.reshape(DEC_BATCH, PAST_LEN, KV_DIM).astype(BF16)
    vc = cache_v.reshape(DEC_BATCH, PAST_LEN, KV_DIM).astype(BF16)
    if windowed:
        osm = _attention(q, TP, k, v, TP, DEC_SEQ, kc, vc, sink, DEC_BATCH, DEC_SEQ, True)
    else:
        n_keys = DEC_SEQ + PAST_LEN
        both = lambda a, c: jnp.concatenate([a[TP:].reshape(DEC_BATCH, DEC_SEQ, KV_DIM), c], axis=1).reshape(
            DEC_BATCH * n_keys, KV_DIM)
        osm = _attention(q, TP, both(k, kc), both(v, vc), 0, n_keys, None, None, sink, DEC_BATCH, DEC_SEQ, False)
    x = _resid_matmul(op, osm, w_o.astype(BF16), jnp.zeros((1, D_MODEL), F32), x, mod, "attn_out_proj")
    shape = (BATCH, SEQ, N_KV_HEADS, HEAD_DIM)
    return x, new_k.reshape(shape), new_v.reshape(shape)


ROUTE_TILE = 512
ROUTE_ROWS = 32


def _router_kernel(x_ref, g_ref, sh_ref, sc_ref, wr_ref, br_ref, xh_ref, rt_ref, cnt_ref, carry):
    tm = ROUTE_TILE
    i = pl.program_id(0)

    @pl.when(i == 0)
    def _():
        carry[...] = jnp.zeros_like(carry)

    h = _norm_mod(x_ref[...], g_ref[...], sh_ref[...], sc_ref[...])
    xh_ref[:, :HALF_D] = _pack_bf16_pairs(h)
    logits = _dot(h.astype(BF16), wr_ref[...])
    s = _sigmoid(logits.T[:N_EXPERTS, :])
    sb = s + br_ref[...]
    u = [s[e:e + 1, :] for e in range(N_EXPERTS)]
    v = [sb[e:e + 1, :] for e in range(N_EXPERTS)]

    gscore = []
    for gq in range(N_EXPERT_GROUPS):
        m = v[4 * gq:4 * gq + 4]
        best = m[PAIR_LO[0]] + m[PAIR_HI[0]]
        for a, b in zip(PAIR_LO[1:], PAIR_HI[1:]):
            best = jnp.maximum(best, m[a] + m[b])
        gscore.append(best)
    gidx = jnp.zeros((1, tm), jnp.int32)
    gbest = gscore[0]
    for gq in range(1, N_EXPERT_GROUPS):
        upd = gscore[gq] > gbest
        gidx = jnp.where(upd, gq, gidx)
        gbest = jnp.where(upd, gscore[gq], gbest)

    def pick(rows, j):
        out = rows[j]
        for gq in range(1, N_EXPERT_GROUPS):
            out = jnp.where(gidx == gq, rows[4 * gq + j], out)
        return out

    vin = [pick(v, j) for j in range(EXPERTS_PER_GROUP)]
    uin = [pick(u, j) for j in range(EXPERTS_PER_GROUP)]
    i1 = jnp.zeros((1, tm), jnp.int32)
    m1 = vin[0]
    for j in range(1, EXPERTS_PER_GROUP):
        upd = vin[j] > m1
        i1 = jnp.where(upd, j, i1)
        m1 = jnp.where(upd, vin[j], m1)
    i2 = jnp.full((1, tm), -1, jnp.int32)
    m2 = jnp.full((1, tm), -jnp.inf, F32)
    for j in range(EXPERTS_PER_GROUP):
        upd = (i1 != j) & (vin[j] > m2)
        i2 = jnp.where(upd, j, i2)
        m2 = jnp.where(upd, vin[j], m2)

    def sel(rows, idx):
        out = rows[0]
        for j in range(1, EXPERTS_PER_GROUP):
            out = jnp.where(idx == j, rows[j], out)
        return out

    w1, w2 = sel(uin, i1), sel(uin, i2)
    wsum = w1 + w2
    w1, w2 = w1 / wsum, w2 / wsum
    first_lo = i1 < i2
    lo = jnp.where(first_lo, i1, i2)
    hi = jnp.where(first_lo, i2, i1)
    w_lo = jnp.where(first_lo, w1, w2)
    w_hi = jnp.where(first_lo, w2, w1)
    pair = jnp.where(lo == 0, hi - 1, jnp.where(lo == 1, hi + 1, 5))
    bucket = gidx * len(PAIR_LO) + pair

    onehot = (lax.broadcasted_iota(jnp.int32, (ROUTE_ROWS, tm), 0) == bucket)
    tri = (lax.broadcasted_iota(jnp.int32, (tm, tm), 0) <= lax.broadcasted_iota(jnp.int32, (tm, tm), 1))
    cum = _dot(jnp.where(onehot, 1.0, 0.0).astype(BF16), jnp.where(tri, 1.0, 0.0).astype(BF16))
    rank = jnp.sum(jnp.where(onehot, cum - 1.0 + carry[...], 0.0), axis=0, keepdims=True)
    carry[...] = carry[...] + cum[:, tm - 1:tm]
    cnt_ref[...] = jnp.broadcast_to(carry[...], (ROUTE_ROWS, LANE))

    rt_ref[...] = jnp.zeros_like(rt_ref)
    rt_ref[0:1, :] = bucket.astype(F32)
    rt_ref[1:2, :] = rank
    wt = jnp.concatenate([w_lo, w_hi, jnp.zeros((LANE - 2, tm), F32)], axis=0)
    xh_ref[:, HALF_D:] = lax.bitcast_convert_type(wt.T, jnp.uint32)


def _router(x, g, mod, w_router, b_router):
    tm = ROUTE_TILE
    wr = _pad2(w_router, D_MODEL, LANE).astype(BF16)
    return pl.pallas_call(
        _router_kernel,
        grid=(T // tm,),
        in_specs=[
            pl.BlockSpec((tm, D_MODEL), lambda i: (i, 0)),
            pl.BlockSpec((1, D_MODEL), lambda i: (0, 0)),
            _mod_spec(tm, 3),
            _mod_spec(tm, 4),
            pl.BlockSpec((D_MODEL, LANE), lambda i: (0, 0)),
            pl.BlockSpec((N_EXPERTS, 1), lambda i: (0, 0)),
        ],
        out_specs=[
            pl.BlockSpec((tm, XH_W), lambda i: (i, 0)),
            pl.BlockSpec((8, tm), lambda i: (0, i)),
            pl.BlockSpec((ROUTE_ROWS, LANE), lambda i: (0, 0)),
        ],
        out_shape=[
            jax.ShapeDtypeStruct((T, XH_W), jnp.uint32),
            jax.ShapeDtypeStruct((8, T), F32),
            jax.ShapeDtypeStruct((ROUTE_ROWS, LANE), F32),
        ],
        scratch_shapes=[pltpu.VMEM((ROUTE_ROWS, 1), F32)],
        compiler_params=_cp(("arbitrary",)),
        name="moe_router",
    )(x, g, mod, mod, wr, b_router.reshape(N_EXPERTS, 1))


DISPATCH_TILE = 256


DMA_UNROLL = 32


def _invert_kernel(dest_ref, src_ref):
    def clear(s, c):
        src_ref[s] = 0
        return c

    def put(t, c):
        src_ref[dest_ref[t]] = t
        return c

    lax.fori_loop(0, T_PAD, clear, 0, unroll=DMA_UNROLL)
    lax.fori_loop(0, T, put, 0, unroll=DMA_UNROLL)


def _invert(dest):
    return pl.pallas_call(
        _invert_kernel,
        in_specs=[pl.BlockSpec(memory_space=pltpu.SMEM)],
        out_specs=pl.BlockSpec(memory_space=pltpu.SMEM),
        out_shape=jax.ShapeDtypeStruct((T_PAD,), jnp.int32),
        name="moe_invert",
    )(dest)


def _gather_rows(idx_ref, base, src_hbm, buf, sem, tm, static=False):
    def start(r, c):
        pltpu.make_async_copy(src_hbm.at[pl.ds(idx_ref[base + r], 1)], buf.at[pl.ds(r, 1)], sem).start()
        return c

    if static:
        for r in range(tm):
            start(r, 0)
    else:
        lax.fori_loop(0, tm, start, 0, unroll=DMA_UNROLL)


def _wait_rows(src_hbm, buf, sem, tm):
    pltpu.make_async_copy(src_hbm.at[pl.ds(0, tm)], buf, sem).wait()


def _expert_kernel(ea_ref, eb_ref, nv_ref, src_ref, xh_hbm, ga_ref, ua_ref, da_ref, gb_ref, ub_ref, db_ref, y_ref,
                   xbuf0, xbuf1, xbuf2, sems):
    tm = MOE_TILE
    j = pl.program_id(0)
    nv = nv_ref[j]
    bufs = (xbuf0, xbuf1, xbuf2)
    n_buf = len(bufs)

    @pl.when(j == 0)
    def _():
        for t in range(MOE_AHEAD):
            _gather_rows(src_ref, t * tm, xh_hbm, bufs[t], sems.at[t], tm)

    def run(p):
        cur, cur_sem = bufs[p], sems.at[p]
        q = (p + MOE_AHEAD) % n_buf
        ahead, ahead_sem = bufs[q], sems.at[q]

        @pl.when(jnp.logical_or(j < MOE_AHEAD, nv_ref[jnp.maximum(j - MOE_AHEAD, 0)] > 0))
        def _():
            _wait_rows(xh_hbm, cur, cur_sem, tm)

        @pl.when(nv > 0)
        def _():
            valid = lax.broadcasted_iota(jnp.int32, (tm, 1), 0) < nv
            x = jnp.where(valid, _unpack_bf16_pairs(cur[:, :HALF_D]), 0.0).astype(BF16)
            wts = jnp.where(valid, lax.bitcast_convert_type(cur[:, HALF_D:], F32), 0.0)
            for r in range(tm):
                pltpu.make_async_copy(xh_hbm.at[pl.ds(src_ref[(j + MOE_AHEAD) * tm + r], 1)],
                                      ahead.at[pl.ds(r, 1)], ahead_sem).start()

            def ffn(g_ref, u_ref, d_ref, w):
                a = _dot(x, g_ref[...])
                h = a * _sigmoid(a) * _dot(x, u_ref[...]) * w
                return _dot(h.astype(BF16), d_ref[...])

            y = ffn(ga_ref, ua_ref, da_ref, wts[:, 0:1]) + ffn(gb_ref, ub_ref, db_ref, wts[:, 1:2])
            y_ref[...] = _pack_bf16_pairs(y)

    for p in range(n_buf):
        pl.when(j % n_buf == p)(functools.partial(run, p))

    @pl.when(nv == 0)
    def _():
        y_ref[...] = jnp.zeros_like(y_ref)


def _experts(layer, tile_ea, tile_eb, tile_nv, src, xh, w_gate, w_up, w_down):
    tm = MOE_TILE
    up = lambda sel: pl.BlockSpec((None, None, D_MODEL, D_EXPERT),
                                  lambda j, ea, eb, nv, sr: (layer, (ea, eb)[sel][j], 0, 0))
    down = lambda sel: pl.BlockSpec((None, None, D_EXPERT, D_MODEL),
                                    lambda j, ea, eb, nv, sr: (layer, (ea, eb)[sel][j], 0, 0))
    return pl.pallas_call(
        _expert_kernel,
        grid_spec=pltpu.PrefetchScalarGridSpec(
            num_scalar_prefetch=4,
            grid=(MOE_TILES,),
            in_specs=[pl.BlockSpec(memory_space=pl.ANY), up(0), up(0), down(0), up(1), up(1), down(1)],
            out_specs=pl.BlockSpec((tm, HALF_D), lambda j, ea, eb, nv, sr: (j, 0)),
            scratch_shapes=[pltpu.VMEM((tm, XH_W), jnp.uint32)] * (MOE_AHEAD + 1)
            + [pltpu.SemaphoreType.DMA((MOE_AHEAD + 1,))],
        ),
        out_shape=jax.ShapeDtypeStruct((T_PAD, HALF_D), jnp.uint32),
        compiler_params=_cp(("arbitrary",), 56),
        name="moe_experts",
    )(tile_ea, tile_eb, tile_nv, src, xh, w_gate, w_up, w_down, w_gate, w_up, w_down)


def _combine_kernel(*refs, final):
    tm = DISPATCH_TILE
    if final:
        dest_ref, x_ref, gt_ref, ys_hbm, fg_ref, oc_ref, ol_ref, buf, sems = refs
    else:
        dest_ref, x_ref, gt_ref, ys_hbm, o_ref, buf, sems = refs
    i = pl.program_id(0)
    slot = i % 2

    last = pl.num_programs(0) - 1

    @pl.when(i == 0)
    def _():
        _gather_rows(dest_ref, 0, ys_hbm, buf.at[0], sems.at[0], tm)

    def finish():
        _wait_rows(ys_hbm, buf.at[slot], sems.at[slot], tm)
        x = x_ref[...] + gt_ref[...] * _unpack_bf16_pairs(buf[slot])
        if not final:
            o_ref[...] = x
            return
        y = x * lax.rsqrt(jnp.mean(x * x, axis=-1, keepdims=True) + NORM_EPS) * fg_ref[...]
        is_ctx = i < TP // tm

        @pl.when(is_ctx)
        def _():
            oc_ref[...] = y

        @pl.when(jnp.logical_not(is_ctx))
        def _():
            ol_ref[...] = y

    @pl.when(i < last)
    def _():
        _gather_rows(dest_ref, (i + 1) * tm, ys_hbm, buf.at[1 - slot], sems.at[1 - slot], tm, static=True)
        finish()

    pl.when(i == last)(finish)


def _combine(dest, x, mod, ys, final_gain=None):
    tm = DISPATCH_TILE
    final = final_gain is not None
    n_ctx = TP // tm
    in_specs = [pl.BlockSpec((tm, D_MODEL), lambda i, d: (i, 0)),
                pl.BlockSpec((None, None, 1, D_MODEL), lambda i, d: (_cond_row(i * tm), 5, 0, 0)),
                pl.BlockSpec(memory_space=pl.ANY)]
    args = [dest, x, mod, ys]
    if final:
        in_specs.append(pl.BlockSpec((1, D_MODEL), lambda i, d: (0, 0)))
        args.append(final_gain.reshape(1, D_MODEL))
        out_specs = [pl.BlockSpec((tm, D_MODEL), lambda i, d: (jnp.minimum(i, n_ctx - 1), 0)),
                     pl.BlockSpec((tm, D_MODEL), lambda i, d: (jnp.maximum(i - n_ctx, 0), 0))]
        out_shape = [jax.ShapeDtypeStruct((TP, D_MODEL), F32), jax.ShapeDtypeStruct((TS, D_MODEL), F32)]
    else:
        out_specs = pl.BlockSpec((tm, D_MODEL), lambda i, d: (i, 0))
        out_shape = jax.ShapeDtypeStruct((T, D_MODEL), F32)
    return pl.pallas_call(
        functools.partial(_combine_kernel, final=final),
        grid_spec=pltpu.PrefetchScalarGridSpec(
            num_scalar_prefetch=1,
            grid=(T // tm,),
            in_specs=in_specs,
            out_specs=out_specs,
            scratch_shapes=[pltpu.VMEM((2, tm, HALF_D), jnp.uint32), pltpu.SemaphoreType.DMA((2,))],
        ),
        out_shape=out_shape,
        compiler_params=_cp(("arbitrary",)),
        name="moe_combine",
    )(*args)


def _lookup(table, idx):
    n = table.shape[0]
    hit = idx[:, None] == jnp.arange(n, dtype=jnp.int32)[None, :]
    return jnp.sum(jnp.where(hit, table[None, :], 0), axis=1)


def _moe_plan(rt, cnt):
    bucket = rt[0].astype(jnp.int32)
    rank = rt[1].astype(jnp.int32)
    counts = cnt[:N_BUCKETS, 0].astype(jnp.int32)
    tiles = (counts + MOE_TILE - 1) // MOE_TILE
    order = jnp.arange(N_BUCKETS, dtype=jnp.int32)
    tile_start = jnp.sum(jnp.where(order[None, :] < order[:, None], tiles[None, :], 0), axis=1)
    tile_end = tile_start + tiles
    n_used = tile_end[N_BUCKETS - 1]
    dest = _lookup(tile_start * MOE_TILE, bucket) + rank
    j = jnp.arange(MOE_TILES, dtype=jnp.int32)
    jc = jnp.minimum(j, n_used - 1)
    b = jnp.minimum(jnp.sum((jc[:, None] >= tile_end[None, :]).astype(jnp.int32), axis=1), N_BUCKETS - 1)
    nv = jnp.clip(_lookup(counts, b) - (j - _lookup(tile_start, b)) * MOE_TILE, 0, MOE_TILE)
    nv = jnp.where(j < n_used, nv, 0)
    n_pairs = len(PAIR_LO)
    ea = (b // n_pairs) * EXPERTS_PER_GROUP + _lookup(jnp.asarray(PAIR_LO, jnp.int32), b % n_pairs)
    eb = (b // n_pairs) * EXPERTS_PER_GROUP + _lookup(jnp.asarray(PAIR_HI, jnp.int32), b % n_pairs)
    return dest, ea, eb, nv


def _moe(layer, x, g, mod, w_router, b_router, w_gate, w_up, w_down, final_gain=None):
    xh, rt, cnt = _router(x, g, mod, w_router, b_router)
    dest, ea, eb, nv = _moe_plan(rt, cnt)
    ys = _experts(layer, ea, eb, nv, _invert(dest), xh, w_gate, w_up, w_down)
    return _combine(dest, x, mod, ys, final_gain)


def kernel(x_prompt, x_sample, cache_k_full, cache_v_full, cache_k_win, cache_v_win, c, c_ctx, w_mod, b_mod, norm_mix, norm_ffn, final_norm, pool_w, pool_scale, hy_w_in, hy_b_in, hy_conv_w, hy_conv_b, hy_f_w1, hy_f_b1, hy_f_w2, hy_f_b2, hy_f_freq, hy_f_w3, hy_decay, hy_skip, hy_w_out, hy_b_out, fa_w_qkv, fa_q_norm, fa_k_norm, fa_w_o, wa_w_qkv, wa_sink, wa_w_o, w_router, b_router, moe_w_gate, moe_w_up, moe_w_down):
    x = None
    cond =jnp.concatenate([c_ctx[None, :], c, jnp.zeros((N_COND - 1 - DEC_BATCH, D_MODEL), F32)], axis=0)
    mods = _adaln(cond, w_mod, b_mod).reshape(DEPTH, N_COND, 6, 1, D_MODEL)
    rope = _rope_tables()
    ones_hd = jnp.ones((HEAD_DIM,), F32)
    wg_bf, wu_bf, wd_bf = moe_w_gate.astype(BF16), moe_w_up.astype(BF16), moe_w_down.astype(BF16)
    new_kv = {}
    for layer in range(DEPTH):
        kind = layer % 4
        j = layer // 4
        mod = mods[layer]
        g_mix = norm_mix[layer].reshape(1, D_MODEL)
        if kind == 0:
            assert layer == 0, "the pooling mixer reads the two input streams, so it must be the first layer"
            x = _pool_mixer(x_prompt.reshape(TP, D_MODEL), x_sample.reshape(TS, D_MODEL), g_mix, mod,
                            pool_w[j], pool_scale[j])
        elif kind == 1:
            x = _hyena_mixer(x, g_mix, mod, hy_w_in[j], hy_b_in[j], hy_conv_w[j], hy_conv_b[j], hy_f_w1[j],
                             hy_f_b1[j], hy_f_w2[j], hy_f_b2[j], hy_f_freq[j], hy_f_w3[j], hy_decay[j],
                             hy_skip[j], hy_w_out[j], hy_b_out[j])
        elif kind == 2:
            x, nk, nv = _attn_mixer(x, g_mix, mod, fa_w_qkv[j], fa_q_norm[j], fa_k_norm[j], True, None,
                                    fa_w_o[j], cache_k_full[:, j], cache_v_full[:, j], False, rope)
            new_kv.setdefault("kf", []).append(nk)
            new_kv.setdefault("vf", []).append(nv)
        else:
            x, nk, nv = _attn_mixer(x, g_mix, mod, wa_w_qkv[j], ones_hd, ones_hd, False, wa_sink[j],
                                    wa_w_o[j], cache_k_win[:, j], cache_v_win[:, j], True, rope)
            new_kv.setdefault("kw", []).append(nk)
            new_kv.setdefault("vw", []).append(nv)
        x = _moe(layer, x, norm_ffn[layer].reshape(1, D_MODEL), mod, w_router, b_router, wg_bf, wu_bf, wd_bf,
                 final_norm if layer == DEPTH - 1 else None)
    y_prompt, y_sample = x
    y_prompt = y_prompt.reshape(BATCH, SEQ, D_MODEL)
    y_sample = y_sample.reshape(DEC_BATCH, DEC_SEQ, D_MODEL)
    return (y_prompt, y_sample, jnp.stack(new_kv["kf"], axis=1), jnp.stack(new_kv["vf"], axis=1),
            jnp.stack(new_kv["kw"], axis=1), jnp.stack(new_kv["vw"], axis=1))
```

```python
import functools
import math

import jax
import jax.numpy as jnp
import numpy as np
from jax import lax
from jax.experimental import pallas as pl
from jax.experimental.pallas import tpu as pltpu

D_MODEL = 2048
BATCH = 32
SEQ = 256
DEPTH = 4
DEC_BATCH = 4
DEC_SEQ = 4096
PAST_LEN = 512
GRID_W = 64
N_HEADS = 16
N_KV_HEADS = 4
HEAD_DIM = D_MODEL // N_HEADS
KV_GROUP = N_HEADS // N_KV_HEADS
KV_DIM = N_KV_HEADS * HEAD_DIM
QKV_DIM = (N_HEADS + 2 * N_KV_HEADS) * HEAD_DIM
ROPE_THETA = 10000.0
WINDOW = 128
POOL_WINDOWS = (2, 4, 8, 16)
POOL_GROUP = D_MODEL // len(POOL_WINDOWS)
HYENA_EMB_BANDS = 16
HYENA_FILTER_HIDDEN = 64
N_EXPERTS = 16
N_EXPERT_GROUPS = 4
EXPERTS_PER_GROUP = 4
D_EXPERT = 512
NORM_EPS = 1e-6
NEG_INF = -1e30

F32 = jnp.float32
BF16 = jnp.bfloat16

TP = BATCH * SEQ
TS = DEC_BATCH * DEC_SEQ
T = TP + TS
N_COND = 8
LANE = 128
MIB = 1024 * 1024

PAIR_LO = (0, 0, 0, 1, 1, 2)
PAIR_HI = (1, 2, 3, 2, 3, 3)
N_BUCKETS = N_EXPERT_GROUPS * len(PAIR_LO)
MOE_TILE = 256
MOE_AHEAD = 3
MOE_TILES = T // MOE_TILE + N_BUCKETS + MOE_AHEAD
T_PAD = MOE_TILES * MOE_TILE
HALF_D = D_MODEL // 2
XH_W = HALF_D + LANE


def _cp(sem, vmem_mb=48):
    return pltpu.CompilerParams(dimension_semantics=sem, vmem_limit_bytes=vmem_mb * MIB)


def _dot(a, b):
    return jnp.dot(a, b, preferred_element_type=F32)


def _dot3(a, b):
    ah = a.astype(BF16)
    al = (a - ah.astype(F32)).astype(BF16)
    bh = b.astype(BF16)
    bl = (b - bh.astype(F32)).astype(BF16)
    return _dot(ah, bh) + (_dot(al, bh) + _dot(ah, bl))


def _sigmoid(x):
    return 1.0 / (1.0 + jnp.exp(-x))


def _pack_bf16_pairs(x):
    n = x.shape[1] // 2
    bits = lambda v: lax.bitcast_convert_type(v.astype(BF16).astype(F32), jnp.uint32)
    return (bits(x[:, :n]) >> 16) | bits(x[:, n:])


def _unpack_bf16_pairs(u):
    lo = lax.bitcast_convert_type(u << 16, F32)
    hi = lax.bitcast_convert_type(u & jnp.uint32(0xFFFF0000), F32)
    return jnp.concatenate([lo, hi], axis=1)


def _cond_row(r):
    return jnp.where(r < TP, 0, 1 + (r - TP) // DEC_SEQ)


def _mod_spec(tm, chunk, tn=D_MODEL, ncol=False):
    if ncol:
        return pl.BlockSpec((None, None, 1, tn), lambda i, j: (_cond_row(i * tm), chunk, 0, j))
    return pl.BlockSpec((None, None, 1, tn), lambda i, *_: (_cond_row(i * tm), chunk, 0, 0))


def _norm_mod(x, g, shift, scale):
    var = jnp.mean(x * x, axis=-1, keepdims=True)
    y = x * lax.rsqrt(var + NORM_EPS) * g
    return y * (1.0 + scale) + shift


def _adaln_kernel(c_ref, w_ref, b_ref, o_ref):
    c = c_ref[...]
    a = c * _sigmoid(c)
    o_ref[...] = _dot3(a, w_ref[...]) + b_ref[...]


def _adaln(cond, w_mod, b_mod):
    tn = 1024
    n = 6 * D_MODEL
    return pl.pallas_call(
        _adaln_kernel,
        grid=(DEPTH, n // tn),
        in_specs=[
            pl.BlockSpec((N_COND, D_MODEL), lambda l, j: (0, 0)),
            pl.BlockSpec((None, D_MODEL, tn), lambda l, j: (l, 0, j)),
            pl.BlockSpec((None, 1, tn), lambda l, j: (l, 0, j)),
        ],
        out_specs=pl.BlockSpec((None, N_COND, tn), lambda l, j: (l, 0, j)),
        out_shape=jax.ShapeDtypeStruct((DEPTH, N_COND, n), F32),
        compiler_params=_cp(("parallel", "parallel")),
        name="adaln",
    )(cond, w_mod, b_mod.reshape(DEPTH, 1, n))


def _nm_matmul_kernel(x_ref, g_ref, sh_ref, sc_ref, w_ref, b_ref, o_ref, h_scr):
    @pl.when(pl.program_id(1) == 0)
    def _():
        h_scr[...] = _norm_mod(x_ref[...], g_ref[...], sh_ref[...], sc_ref[...]).astype(BF16)

    o_ref[...] = (_dot(h_scr[...], w_ref[...]) + b_ref[...]).astype(o_ref.dtype)


def _nm_matmul(x, g, mod, w, b, out_dtype, name):
    tm, tn = 1024, 1024
    n = w.shape[1]
    return pl.pallas_call(
        _nm_matmul_kernel,
        grid=(T // tm, n // tn),
        in_specs=[
            pl.BlockSpec((tm, D_MODEL), lambda i, j: (i, 0)),
            pl.BlockSpec((1, D_MODEL), lambda i, j: (0, 0)),
            _mod_spec(tm, 0),
            _mod_spec(tm, 1),
            pl.BlockSpec((D_MODEL, tn), lambda i, j: (0, j)),
            pl.BlockSpec((1, tn), lambda i, j: (0, j)),
        ],
        out_specs=pl.BlockSpec((tm, tn), lambda i, j: (i, j)),
        out_shape=jax.ShapeDtypeStruct((T, n), out_dtype),
        scratch_shapes=[pltpu.VMEM((tm, D_MODEL), BF16)],
        compiler_params=_cp(("parallel", "arbitrary")),
        name=name,
    )(x, g, mod, mod, w, b)


RESID_TM = 1024


def _resid_matmul_kernel(ap_ref, as_ref, w_ref, b_ref, x_ref, gt_ref, o_ref):
    def emit(a_ref):
        o_ref[...] = x_ref[...] + gt_ref[...] * (_dot(a_ref[...], w_ref[...]) + b_ref[...])

    is_ctx = pl.program_id(0) < TP // RESID_TM
    pl.when(is_ctx)(lambda: emit(ap_ref))
    pl.when(jnp.logical_not(is_ctx))(lambda: emit(as_ref))


def _resid_matmul(a_ctx, a_lat, w, b, x, mod, name):
    tm, tn = RESID_TM, 1024
    k = a_ctx.shape[1]
    n_ctx = TP // tm
    return pl.pallas_call(
        _resid_matmul_kernel,
        grid=(T // tm, D_MODEL // tn),
        in_specs=[
            pl.BlockSpec((tm, k), lambda i, j: (jnp.minimum(i, n_ctx - 1), 0)),
            pl.BlockSpec((tm, k), lambda i, j: (jnp.maximum(i - n_ctx, 0), 0)),
            pl.BlockSpec((k, tn), lambda i, j: (0, j)),
            pl.BlockSpec((1, tn), lambda i, j: (0, j)),
            pl.BlockSpec((tm, tn), lambda i, j: (i, j)),
            _mod_spec(tm, 2, tn, ncol=True),
        ],
        out_specs=pl.BlockSpec((tm, tn), lambda i, j: (i, j)),
        out_shape=jax.ShapeDtypeStruct((T, D_MODEL), F32),
        compiler_params=_cp(("parallel", "parallel")),
        name=name,
    )(a_ctx, a_lat, w, b, x, mod)


POOL_TILE = 256
POOL_HALO = 8


def _seq_pos(r0):
    is_ctx = r0 < TP
    loc0 = jnp.where(is_ctx, r0 % SEQ, (r0 - TP) % DEC_SEQ)
    seq_len = jnp.where(is_ctx, SEQ, DEC_SEQ)
    return loc0, seq_len


def _pool_kernel(xc_ref, xcp_ref, xcn_ref, xl_ref, xlp_ref, xln_ref, *rest):
    is_ctx = pl.program_id(0) < TP // POOL_TILE
    pl.when(is_ctx)(lambda: _pool_tile(xc_ref, xcp_ref, xcn_ref, *rest))
    pl.when(jnp.logical_not(is_ctx))(lambda: _pool_tile(xl_ref, xlp_ref, xln_ref, *rest))


def _pool_tile(x_ref, xp_ref, xn_ref, g_ref, sh_ref, sc_ref, gt_ref, pw_ref, ps_ref, o_ref, hz_scr):
    tm, hl = POOL_TILE, POOL_HALO
    loc0, seq_len = _seq_pos(pl.program_id(0) * tm)
    has_prev = loc0 > 0
    has_next = loc0 + tm < seq_len
    g, sh, sc = g_ref[...], sh_ref[...], sc_ref[...]
    x = x_ref[...]
    h = _norm_mod(x, g, sh, sc)
    hz_scr[0:hl, :] = jnp.where(has_prev, _norm_mod(xp_ref[...], g, sh, sc), 0.0)
    hz_scr[hl:hl + tm, :] = h
    hz_scr[hl + tm:, :] = jnp.where(has_next, _norm_mod(xn_ref[...], g, sh, sc), 0.0)
    tl = loc0 + lax.broadcasted_iota(jnp.int32, (tm, 1), 0)
    outs = []
    for gi, w in enumerate(POOL_WINDOWS):
        cs = slice(gi * POOL_GROUP, (gi + 1) * POOL_GROUP)
        s = jnp.zeros((tm, POOL_GROUP), F32)
        for off in range(-(w // 2), w - w // 2):
            s = s + hz_scr[hl + off:hl + off + tm, cs]
        lo = jnp.maximum(tl - w // 2, 0)
        hi = jnp.minimum(tl + (w - w // 2), seq_len)
        d = s / (hi - lo).astype(F32) - h[:, cs]
        outs.append(_dot(d.astype(BF16), pw_ref[gi]))
    out = jnp.concatenate(outs, axis=1) * ps_ref[...]
    o_ref[...] = x + gt_ref[...] * out


def _pool_mixer(x_ctx, x_lat, g, mod, pool_w, pool_scale):
    tm, hl = POOL_TILE, POOL_HALO
    r = tm // hl

    def stream(first_tile, rows):
        tile = lambda i: jnp.clip(i - first_tile, 0, rows // tm - 1)
        return [pl.BlockSpec((tm, D_MODEL), lambda i: (tile(i), 0)),
                pl.BlockSpec((hl, D_MODEL), lambda i: (jnp.maximum(tile(i) * r - 1, 0), 0)),
                pl.BlockSpec((hl, D_MODEL), lambda i: (jnp.minimum((tile(i) + 1) * r, rows // hl - 1), 0))]

    return pl.pallas_call(
        _pool_kernel,
        grid=(T // tm,),
        in_specs=stream(0, TP) + stream(TP // tm, TS) + [
            pl.BlockSpec((1, D_MODEL), lambda i: (0, 0)),
            _mod_spec(tm, 0),
            _mod_spec(tm, 1),
            _mod_spec(tm, 2),
            pl.BlockSpec((len(POOL_WINDOWS), POOL_GROUP, POOL_GROUP), lambda i: (0, 0, 0)),
            pl.BlockSpec((1, D_MODEL), lambda i: (0, 0)),
        ],
        out_specs=pl.BlockSpec((tm, D_MODEL), lambda i: (i, 0)),
        out_shape=jax.ShapeDtypeStruct((T, D_MODEL), F32),
        scratch_shapes=[pltpu.VMEM((tm + 2 * hl, D_MODEL), F32)],
        compiler_params=_cp(("parallel",)),
        name="pool_mixer",
    )(x_ctx, x_ctx, x_ctx, x_lat, x_lat, x_lat, g, mod, mod, mod, pool_w.astype(BF16),
      pool_scale.reshape(1, D_MODEL))


CONV_TILE = 256
CONV_HALO = 16


def _conv3_kernel(u_ref, up_ref, un_ref, cw_ref, cb_ref, o_ref, scr):
    tm, hl = CONV_TILE, CONV_HALO
    loc0, seq_len = _seq_pos(pl.program_id(0) * tm)
    has_prev = loc0 > 0
    has_next = loc0 + tm < seq_len
    scr[0:hl, :] = jnp.where(has_prev, up_ref[...].astype(F32), 0.0)
    scr[hl:hl + tm, :] = u_ref[...].astype(F32)
    scr[hl + tm:, :] = jnp.where(has_next, un_ref[...].astype(F32), 0.0)
    out = (scr[hl - 1:hl - 1 + tm, :] * cw_ref[0:1, :] + scr[hl:hl + tm, :] * cw_ref[1:2, :]
           + scr[hl + 1:hl + 1 + tm, :] * cw_ref[2:3, :] + cb_ref[...])
    o_ref[...] = out.astype(o_ref.dtype)


def _conv3(u0, conv_w, conv_b):
    tm, hl, tc = CONV_TILE, CONV_HALO, D_MODEL
    r = tm // hl
    n = u0.shape[1]
    return pl.pallas_call(
        _conv3_kernel,
        grid=(T // tm, n // tc),
        in_specs=[
            pl.BlockSpec((tm, tc), lambda i, j: (i, j)),
            pl.BlockSpec((hl, tc), lambda i, j: (jnp.maximum(i * r - 1, 0), j)),
            pl.BlockSpec((hl, tc), lambda i, j: (jnp.minimum((i + 1) * r, T // hl - 1), j)),
            pl.BlockSpec((3, tc), lambda i, j: (0, j)),
            pl.BlockSpec((1, tc), lambda i, j: (0, j)),
        ],
        out_specs=pl.BlockSpec((tm, tc), lambda i, j: (i, j)),
        out_shape=jax.ShapeDtypeStruct((T, n), BF16),
        scratch_shapes=[pltpu.VMEM((tm + 2 * hl, tc), F32)],
        compiler_params=_cp(("parallel", "parallel")),
        name="hyena_conv3",
    )(u0, u0, u0, conv_w, conv_b.reshape(1, n))


FILT_TILE = 256


HYENA_BLOCK = 1024


T_LANE = LANE - 1


def _filter_mlp_kernel(emb_ref, w1_ref, b1_ref, w2_ref, b2_ref, fr_ref, o_ref):
    emb = emb_ref[...]
    fr = fr_ref[...]
    a = jnp.sin(fr * (_dot3(emb, w1_ref[...]) + b1_ref[...]))
    a = jnp.sin(fr * (_dot3(a, w2_ref[...]) + b2_ref[...]))
    lane = lax.broadcasted_iota(jnp.int32, a.shape, 1)
    o_ref[...] = jnp.where(lane == T_LANE, emb[:, 0:1], a)


def _filter_mlp(pos, L, f_w1, f_b1, f_w2, f_b2, f_freq):
    assert HYENA_FILTER_HIDDEN <= T_LANE
    tl = FILT_TILE
    rows = pos.shape[0]
    small = lambda: pl.BlockSpec((LANE, LANE), lambda i: (0, 0))
    vec = lambda: pl.BlockSpec((1, LANE), lambda i: (0, 0))
    return pl.pallas_call(
        _filter_mlp_kernel,
        grid=(rows // tl,),
        in_specs=[pl.BlockSpec((tl, LANE), lambda i: (i, 0)), small(), vec(), small(), vec(), vec()],
        out_specs=pl.BlockSpec((tl, LANE), lambda i: (i, 0)),
        out_shape=jax.ShapeDtypeStruct((rows, LANE), F32),
        compiler_params=_cp(("parallel",)),
        name="hyena_filter_mlp",
    )(_filter_embedding(pos, L), _pad2(f_w1, LANE, LANE), _pad2(f_b1[None], 1, LANE),
      _pad2(f_w2, LANE, LANE), _pad2(f_b2[None], 1, LANE), _pad2(f_freq[None], 1, LANE))


def _filter_kernel(h1_ref, h2_ref, w3a_ref, dca_ref, w3b_ref, dcb_ref, fa_ref, fb_ref, *, blk):
    def taps(h, w3_ref, dc_ref):
        return _dot3(h, w3_ref[...]) * jnp.exp(-h[:, T_LANE:] * jnp.abs(dc_ref[...]))

    pos = taps(h1_ref[...], w3a_ref, dca_ref)
    neg = taps(h2_ref[...], w3b_ref, dcb_ref)
    m = (pl.program_id(0) * FILT_TILE + lax.broadcasted_iota(jnp.int32, (FILT_TILE, 1), 0)) % blk
    fa_ref[...] = jnp.where(m == 0, pos, pos + neg).astype(BF16)
    fb_ref[...] = jnp.where(m == 0, 0.0, neg - pos).astype(BF16)


def _pad2(a, rows, cols):
    return jnp.pad(a, ((0, rows - a.shape[0]), (0, cols - a.shape[1])))


def _filter_positions(L, blk):
    n_blk = L // blk
    m = np.arange(blk)
    p1, p2 = [], []
    for d in range(-(n_blk - 1), n_blk):
        if d >= 1:
            p1.append(d * blk + m), p2.append(d * blk - m)
        elif d == 0:
            p1.append(m), p2.append(m)
        else:
            p1.append(-d * blk - m), p2.append(-d * blk + m)
    return np.concatenate(p1), np.concatenate(p2)


def _filter_embedding(pos, L):
    t = jnp.asarray(pos, F32) / L
    bands = jnp.linspace(1e-4, HYENA_EMB_BANDS - 1, HYENA_EMB_BANDS, dtype=F32)
    ang = (2 * math.pi) * t[:, None] * bands[None, :]
    return _pad2(jnp.concatenate([t[:, None], jnp.cos(ang), -jnp.sin(ang)], axis=-1), pos.shape[0], LANE)


def _hyena_filters(L, blk, f_w1, f_b1, f_w2, f_b2, f_freq, f_w3, decay):
    n_blk = L // blk
    p1, p2 = _filter_positions(L, blk)
    rows = p1.shape[0]
    tl = FILT_TILE
    tiles_per_lag = blk // tl
    lag = lambda i: i // tiles_per_lag - (n_blk - 1)
    col1 = lambda i, n: 2 * n + jnp.where(lag(i) >= 0, 0, 1)
    col2 = lambda i, n: 2 * n + jnp.where(lag(i) >= 1, 0, 1)
    n_tiles = rows // tl
    out = pl.BlockSpec((None, tl, D_MODEL), lambda i, n: (n, i, 0))
    w3 = _pad2(f_w3, LANE, f_w3.shape[1])
    hidden = _filter_mlp(np.concatenate([p1, p2]), L, f_w1, f_b1, f_w2, f_b2, f_freq)
    return pl.pallas_call(
        functools.partial(_filter_kernel, blk=blk),
        grid=(n_tiles, 2),
        in_specs=[
            pl.BlockSpec((tl, LANE), lambda i, n: (i, 0)),
            pl.BlockSpec((tl, LANE), lambda i, n: (n_tiles + i, 0)),
            pl.BlockSpec((LANE, D_MODEL), lambda i, n: (0, col1(i, n))),
            pl.BlockSpec((1, D_MODEL), lambda i, n: (0, col1(i, n))),
            pl.BlockSpec((LANE, D_MODEL), lambda i, n: (0, col2(i, n))),
            pl.BlockSpec((1, D_MODEL), lambda i, n: (0, col2(i, n))),
        ],
        out_specs=[out, out],
        out_shape=[jax.ShapeDtypeStruct((2, rows, D_MODEL), BF16)] * 2,
        compiler_params=_cp(("parallel", "parallel")),
        name="hyena_filters",
    )(hidden, hidden, w3, decay[None], w3, decay[None])


def _dft_mats(L):
    r = int(math.isqrt(L))
    k2 = 2 * jnp.arange(L, dtype=jnp.int32)[:, None] + 1
    n1 = r * jnp.arange(L // r, dtype=jnp.int32)[None, :]
    n2 = jnp.arange(r, dtype=jnp.int32)[None, :]
    sc = math.pi / (2 * L)
    aa = ((k2 * n1) % (4 * L)).astype(F32) * sc
    ab = ((k2 * n2) % (4 * L)).astype(F32) * sc
    ca, sa, cb, sb = jnp.cos(aa)[:, :, None], jnp.sin(aa)[:, :, None], jnp.cos(ab)[:, None, :], jnp.sin(ab)[:, None, :]
    c = (ca * cb - sa * sb).reshape(L, L)
    s = (sa * cb + ca * sb).reshape(L, L)
    return c.astype(BF16), s.astype(BF16), c.T.astype(BF16), s.T.astype(BF16)


def _dft_tiles(L):
    return min(512, L), 512


def _dft_filter_kernel(c_ref, s_ref, a_ref, b_ref, gr_ref, gi_ref):
    gr_ref[...] = _dot(c_ref[...], a_ref[...]).astype(gr_ref.dtype)
    gi_ref[...] = _dot(s_ref[...], b_ref[...]).astype(gi_ref.dtype)


def _dft_filter(cm, sm, fa, fb, L):
    tf, tn = _dft_tiles(L)
    n = fa.shape[0]
    mat = lambda: pl.BlockSpec((tf, L), lambda k, c, s: (k, 0))
    rhs = lambda: pl.BlockSpec((None, L, tn), lambda k, c, s: (s, 0, c))
    out = pl.BlockSpec((None, tf, tn), lambda k, c, s: (s, k, c))
    return pl.pallas_call(
        _dft_filter_kernel,
        grid=(L // tf, D_MODEL // tn, n),
        in_specs=[mat(), mat(), rhs(), rhs()],
        out_specs=[out, out],
        out_shape=[jax.ShapeDtypeStruct((n, L, D_MODEL), BF16)] * 2,
        compiler_params=_cp(("parallel", "parallel", "parallel")),
        name="hyena_filter_dft",
    )(cm, sm, fa, fb)


FWD_TF = 256


def _dft_fwd_kernel(c_ref, s_ref, z_ref, gr_ref, gi_ref, yr_ref, yi_ref, *, n_blk, blk, bpb):
    c, s = c_ref[...], s_ref[...]
    for bb in range(bpb):
        zc, zs = [], []
        for j in range(n_blk):
            r = (bb * n_blk + j) * blk
            zj = z_ref[r:r + blk, :]
            zc.append(_dot(c, zj).astype(BF16))
            zs.append(_dot(s, zj).astype(BF16))
        for i in range(n_blk):
            yr = yi = None
            for j in range(n_blk):
                lag = i - j + n_blk - 1
                gr, gi = gr_ref[lag], gi_ref[lag]
                tr = gr * zc[j] + gi * zs[j]
                ti = gi * zc[j] - gr * zs[j]
                yr = tr if yr is None else yr + tr
                yi = ti if yi is None else yi + ti
            yr_ref[bb, i] = yr.astype(BF16)
            yi_ref[bb, i] = yi.astype(BF16)


def _seqs_per_step(L):
    return max(1, 2048 // L)


def _dft_fwd(cm, sm, z, z_rowblk, z_colblk, gr, gi, order, nb, L, blk):
    n_blk = L // blk
    bpb = _seqs_per_step(L)
    assert nb % bpb == 0 and z_rowblk % bpb == 0
    tf, tn = min(FWD_TF, blk), 512
    mat = lambda: pl.BlockSpec((tf, blk), lambda k, c, b: (k, 0))
    gsp = lambda: pl.BlockSpec((None, 2 * n_blk - 1, tf, tn), lambda k, c, b: (order, 0, k, c))
    out = pl.BlockSpec((bpb, n_blk, tf, tn), lambda k, c, b: (b, 0, k, c))
    return pl.pallas_call(
        functools.partial(_dft_fwd_kernel, n_blk=n_blk, blk=blk, bpb=bpb),
        grid=(blk // tf, D_MODEL // tn, nb // bpb),
        in_specs=[mat(), mat(),
                  pl.BlockSpec((bpb * L, tn), lambda k, c, b: (z_rowblk // bpb + b, z_colblk + c)),
                  gsp(), gsp()],
        out_specs=[out, out],
        out_shape=[jax.ShapeDtypeStruct((nb, n_blk, blk, D_MODEL), BF16)] * 2,
        compiler_params=_cp(("parallel", "parallel", "parallel")),
        name="hyena_dft_fwd",
    )(cm, sm, z, gr, gi)


def _dft_inv_kernel(ct_ref, st_ref, yr_ref, yi_ref, z_ref, gt_ref, sk_ref, o_ref, *, inv_len, bpb, tt):
    for bb in range(bpb):
        rows = slice(bb * tt, (bb + 1) * tt)
        y = (_dot(ct_ref[...], yr_ref[bb]) - _dot(st_ref[...], yi_ref[bb])) * inv_len
        o_ref[rows, :] = (gt_ref[rows, :].astype(F32) * (y + sk_ref[...] * z_ref[rows, :].astype(F32))).astype(BF16)


def _dft_inv(ctm, stm, yr, yi, z, z_rowblk, z_colblk, gate, g_rowblk, g_colblk, skip, nb, L):
    tt, tn = _dft_tiles(L)
    rpb = L // tt
    bpb = _seqs_per_step(L) if rpb == 1 else 1
    assert nb % bpb == 0 and z_rowblk % bpb == 0 and g_rowblk % bpb == 0
    mat = lambda: pl.BlockSpec((tt, L), lambda t, c, b: (t, 0))
    spec = lambda: pl.BlockSpec((bpb, L, tn), lambda t, c, b: (b, 0, c))
    rows = lambda blk0: (lambda t, c, b: ((blk0 + b * bpb * rpb + t) // bpb))
    return pl.pallas_call(
        functools.partial(_dft_inv_kernel, inv_len=1.0 / L, bpb=bpb, tt=tt),
        grid=(rpb, D_MODEL // tn, nb // bpb),
        in_specs=[mat(), mat(), spec(), spec(),
                  pl.BlockSpec((bpb * tt, tn), lambda t, c, b: (rows(z_rowblk)(t, c, b), z_colblk + c)),
                  pl.BlockSpec((bpb * tt, tn), lambda t, c, b: (rows(g_rowblk)(t, c, b), g_colblk + c)),
                  pl.BlockSpec((1, tn), lambda t, c, b: (0, c))],
        out_specs=pl.BlockSpec((bpb * tt, tn), lambda t, c, b: (rows(0)(t, c, b), c)),
        out_shape=jax.ShapeDtypeStruct((nb * L, D_MODEL), BF16),
        compiler_params=_cp(("parallel", "parallel", "parallel")),
        name="hyena_dft_inv",
    )(ctm, stm, yr, yi, z, gate, skip)


def _hyena_stream(u, row0, nb, L, fparams, skip):
    blk = min(HYENA_BLOCK, L)
    n_blk = L // blk
    n_lag = 2 * n_blk - 1
    cm, sm, ctm, stm = _dft_mats(blk)
    fa, fb = _hyena_filters(L, blk, *fparams)
    seg = lambda a: a.reshape(2 * n_lag, blk, D_MODEL)
    gr, gi = _dft_filter(cm, sm, seg(fa), seg(fb), blk)
    gr, gi = (a.reshape(2, n_lag, blk, D_MODEL) for a in (gr, gi))
    tt, tn = _dft_tiles(blk)
    ncb = D_MODEL // tn
    blocks = lambda a: a.reshape(nb * n_blk, blk, D_MODEL)

    yr, yi = _dft_fwd(cm, sm, u, row0 // L, 0, gr, gi, 0, nb, L, blk)
    z1 = _dft_inv(ctm, stm, blocks(yr), blocks(yi), u, row0 // tt, 0, u, row0 // tt, ncb, skip[0:1],
                  nb * n_blk, blk)
    yr, yi = _dft_fwd(cm, sm, z1, 0, 0, gr, gi, 1, nb, L, blk)
    return _dft_inv(ctm, stm, blocks(yr), blocks(yi), z1, 0, 0, u, row0 // tt, 2 * ncb, skip[1:2],
                    nb * n_blk, blk)


def _hyena_mixer(x, g, mod, w_in, b_in, conv_w, conv_b, f_w1, f_b1, f_w2, f_b2, f_freq, f_w3, decay, skip,
                 w_out, b_out):
    u0 = _nm_matmul(x, g, mod, w_in.astype(BF16), b_in.reshape(1, -1), BF16, "hyena_in_proj")
    u = _conv3(u0, conv_w, conv_b)
    fparams = (f_w1, f_b1, f_w2, f_b2, f_freq, f_w3, decay)
    zp = _hyena_stream(u, 0, BATCH, SEQ, fparams, skip)
    zs = _hyena_stream(u, TP, DEC_BATCH, DEC_SEQ, fparams, skip)
    return _resid_matmul(zp, zs, w_out.astype(BF16), b_out.reshape(1, -1), x, mod, "hyena_out_proj")


def _rope_tables():
    pos = jnp.arange(DEC_SEQ, dtype=jnp.int32)
    row = (pos // GRID_W).astype(F32)
    col = (pos % GRID_W).astype(F32)
    axis_dim = HEAD_DIM // 2
    inv_freq = ROPE_THETA ** (-jnp.arange(0, axis_dim, 2, dtype=F32) / axis_dim)
    ar = row[:, None] * inv_freq[None, :]
    ac = col[:, None] * inv_freq[None, :]
    cos = jnp.concatenate([jnp.cos(ar), jnp.cos(ar), jnp.cos(ac), jnp.cos(ac)], axis=-1)
    sin = jnp.concatenate([-jnp.sin(ar), jnp.sin(ar), -jnp.sin(ac), jnp.sin(ac)], axis=-1)
    return cos, sin


QKV_TM = 512
QKV_TN = 1024
PAIR = 2 * HEAD_DIM


def _qkv_kernel(x_ref, g_ref, sh_ref, sc_ref, w_ref, qn_ref, kn_ref, cos_ref, sin_ref,
                q_ref, k_ref, v_ref, nk_ref, nv_ref, h_scr, *, use_norm):
    tm = QKV_TM
    i, j = pl.program_id(0), pl.program_id(1)

    @pl.when(j == 0)
    def _():
        h_scr[...] = _norm_mod(x_ref[...], g_ref[...], sh_ref[...], sc_ref[...]).astype(BF16)

    quarter = HEAD_DIM // 4
    scale = HEAD_DIM ** -0.5 * LOG2E

    def head(xh, gn):
        if use_norm:
            xh = xh * lax.rsqrt(jnp.mean(xh * xh, axis=-1, keepdims=True) + NORM_EPS) * gn
        return xh

    def rope(xh):
        lane = lax.broadcasted_iota(jnp.int32, (tm, HEAD_DIM), 1)
        first = (lane % (2 * quarter)) < quarter
        partner = jnp.where(first, pltpu.roll(xh, HEAD_DIM - quarter, 1), pltpu.roll(xh, quarter, 1))
        return xh * cos_ref[...] + partner * sin_ref[...]

    def proj(c0):
        return _dot(h_scr[...], w_ref[:, c0:c0 + PAIR])

    def q_tile(latent):
        for p in range(QKV_TN // PAIR):
            acc = proj(p * PAIR)
            for t in range(2):
                xh = head(acc[:, t * HEAD_DIM:(t + 1) * HEAD_DIM], qn_ref[...])
                xh = rope(xh) if latent else xh
                c0 = p * PAIR + t * HEAD_DIM
                q_ref[:, c0:c0 + HEAD_DIM] = (xh * scale).astype(BF16)

    def kv_tile(latent):
        for p in range(KV_DIM // PAIR):
            acc = proj(p * PAIR)
            for t in range(2):
                c0 = p * PAIR + t * HEAD_DIM
                kh = head(acc[:, t * HEAD_DIM:(t + 1) * HEAD_DIM], kn_ref[...])
                if not latent:
                    nk_ref[:, c0:c0 + HEAD_DIM] = kh
                k_ref[:, c0:c0 + HEAD_DIM] = (rope(kh) if latent else kh).astype(BF16)
        for p in range(KV_DIM // PAIR):
            acc = proj(KV_DIM + p * PAIR)
            if not latent:
                nv_ref[:, p * PAIR:(p + 1) * PAIR] = acc
            v_ref[:, p * PAIR:(p + 1) * PAIR] = acc.astype(BF16)

    is_ctx = i < TP // tm
    is_q = j < D_MODEL // QKV_TN
    for latent in (False, True):
        stream = jnp.logical_not(is_ctx) if latent else is_ctx
        pl.when(jnp.logical_and(stream, is_q))(functools.partial(q_tile, latent))
        pl.when(jnp.logical_and(stream, jnp.logical_not(is_q)))(functools.partial(kv_tile, latent))


def _qkv_proj(x, g, mod, w_qkv, q_norm, k_norm, use_norm, rope):
    tm, tn = QKV_TM, QKV_TN
    n_ctx = TP // tm
    n_q = D_MODEL // tn
    tab = lambda: pl.BlockSpec((tm, HEAD_DIM), lambda i, j: (jnp.maximum(i - n_ctx, 0) % (DEC_SEQ // tm), 0))
    kv = lambda: pl.BlockSpec((tm, KV_DIM), lambda i, j: (i, 0))
    new = lambda: pl.BlockSpec((tm, KV_DIM), lambda i, j: (jnp.minimum(i, n_ctx - 1), 0))
    return pl.pallas_call(
        functools.partial(_qkv_kernel, use_norm=use_norm),
        grid=(T // tm, QKV_DIM // tn),
        in_specs=[
            pl.BlockSpec((tm, D_MODEL), lambda i, j: (i, 0)),
            pl.BlockSpec((1, D_MODEL), lambda i, j: (0, 0)),
            _mod_spec(tm, 0),
            _mod_spec(tm, 1),
            pl.BlockSpec((D_MODEL, tn), lambda i, j: (0, j)),
            pl.BlockSpec((1, HEAD_DIM), lambda i, j: (0, 0)),
            pl.BlockSpec((1, HEAD_DIM), lambda i, j: (0, 0)),
            tab(), tab(),
        ],
        out_specs=[pl.BlockSpec((tm, tn), lambda i, j: (i, jnp.minimum(j, n_q - 1))), kv(), kv(), new(), new()],
        out_shape=[jax.ShapeDtypeStruct((T, D_MODEL), BF16), jax.ShapeDtypeStruct((T, KV_DIM), BF16),
                   jax.ShapeDtypeStruct((T, KV_DIM), BF16), jax.ShapeDtypeStruct((TP, KV_DIM), F32),
                   jax.ShapeDtypeStruct((TP, KV_DIM), F32)],
        scratch_shapes=[pltpu.VMEM((tm, D_MODEL), BF16)],
        compiler_params=_cp(("arbitrary", "arbitrary")),
        name="qkv_proj",
    )(x, g, mod, mod, w_qkv, q_norm.reshape(1, HEAD_DIM), k_norm.reshape(1, HEAD_DIM), *rope)


LOG2E = math.log2(math.e)
ATTN_TQ = 256


def _attn_kernel(*refs, tq, seq_len, has_ctx, windowed, has_sink):
    it = iter(refs)
    q_ref, k_ref, v_ref = next(it), next(it), next(it)
    kc_ref, vc_ref = (next(it), next(it)) if has_ctx else (None, None)
    sink_ref = next(it) if has_sink else None
    o_ref = next(it)
    if windowed:
        i = pl.program_id(2)
        span = tq + 2 * WINDOW
        start = pl.multiple_of(jnp.clip(i * tq - WINDOW, 0, seq_len - span), WINDOW)
        qpos = i * tq + lax.broadcasted_iota(jnp.int32, (tq, 1), 0)
        kpos = start + lax.broadcasted_iota(jnp.int32, (1, span), 1)
        segs = [(k_ref, v_ref, pl.ds(start, span), jnp.abs(kpos - qpos) <= WINDOW)]
    else:
        segs = [(k_ref, v_ref, slice(None), None)]
    if has_ctx:
        segs.append((kc_ref, vc_ref, slice(None), None))
    for h in range(KV_GROUP):
        hs = slice(h * HEAD_DIM, (h + 1) * HEAD_DIM)
        qh = q_ref[:, hs]
        scores = []
        m = None
        for kr, _, rows, mask in segs:
            s = lax.dot_general(qh, kr[rows, :], (((1,), (1,)), ((), ())), preferred_element_type=F32)
            if mask is not None:
                s = jnp.where(mask, s, NEG_INF)
            scores.append(s)
            ms = jnp.max(s, axis=-1, keepdims=True)
            m = ms if m is None else jnp.maximum(m, ms)
        if has_sink:
            sk = sink_ref[pl.program_id(1) * KV_GROUP + h]
            m = jnp.maximum(m, sk)
        l = jnp.exp2(sk - m) if has_sink else jnp.zeros_like(m)
        acc = jnp.zeros((tq, HEAD_DIM), F32)
        for (_, vr, rows, _), s in zip(segs, scores):
            p = jnp.exp2(s - m)
            l = l + jnp.sum(p, axis=-1, keepdims=True)
            acc = acc + _dot(p.astype(BF16), vr[rows, :])
        o_ref[:, hs] = (acc / l).astype(BF16)


def _attention(q, row0, k, v, k_row0, n_keys, k_ctx, v_ctx, sink, nb, L, windowed):
    tq = min(ATTN_TQ, L)
    nq = L // tq
    q_blk0 = row0 // tq
    seq0 = k_row0 // n_keys
    own = lambda: pl.BlockSpec((n_keys, HEAD_DIM), lambda b, g, i: (seq0 + b, g))
    in_specs = [pl.BlockSpec((tq, KV_GROUP * HEAD_DIM), lambda b, g, i: (q_blk0 + b * nq + i, g)), own(), own()]
    args = [q, k, v]
    if k_ctx is not None:
        ctx = lambda: pl.BlockSpec((None, PAST_LEN, HEAD_DIM), lambda b, g, i: (b, 0, g))
        in_specs += [ctx(), ctx()]
        args += [k_ctx, v_ctx]
    if sink is not None:
        in_specs.append(pl.BlockSpec(memory_space=pltpu.SMEM))
        args.append(sink.astype(F32) * LOG2E)
    return pl.pallas_call(
        functools.partial(_attn_kernel, tq=tq, seq_len=L, has_ctx=k_ctx is not None, windowed=windowed,
                          has_sink=sink is not None),
        grid=(nb, N_KV_HEADS, nq),
        in_specs=in_specs,
        out_specs=pl.BlockSpec((tq, KV_GROUP * HEAD_DIM), lambda b, g, i: (b * nq + i, g)),
        out_shape=jax.ShapeDtypeStruct((nb * L, D_MODEL), BF16),
        compiler_params=_cp(("parallel", "parallel", "parallel"), 56),
        name="attention",
    )(*args)


def _attn_mixer(x, g, mod, w_qkv, q_norm, k_norm, use_norm, sink, w_o, cache_k, cache_v, windowed, rope):
    q, k, v, new_k, new_v = _qkv_proj(x, g, mod, w_qkv.astype(BF16), q_norm, k_norm, use_norm, rope)
    op = _attention(q, 0, k, v, 0, SEQ, None, None, sink, BATCH, SEQ, False)
    kc = cache_k.reshape(DEC_BATCH, PAST_LEN, KV_DIM).astype(BF16)
    vc = cache_v.reshape(DEC_BATCH, PAST_LEN, KV_DIM).astype(BF16)
    if windowed:
        osm = _attention(q, TP, k, v, TP, DEC_SEQ, kc, vc, sink, DEC_BATCH, DEC_SEQ, True)
    else:
        n_keys = DEC_SEQ + PAST_LEN
        both = lambda a, c: jnp.concatenate([a[TP:].reshape(DEC_BATCH, DEC_SEQ, KV_DIM), c], axis=1).reshape(
            DEC_BATCH * n_keys, KV_DIM)
        osm = _attention(q, TP, both(k, kc), both(v, vc), 0, n_keys, None, None, sink, DEC_BATCH, DEC_SEQ, False)
    x = _resid_matmul(op, osm, w_o.astype(BF16), jnp.zeros((1, D_MODEL), F32), x, mod, "attn_out_proj")
    shape = (BATCH, SEQ, N_KV_HEADS, HEAD_DIM)
    return x, new_k.reshape(shape), new_v.reshape(shape)


ROUTE_TILE = 512
ROUTE_ROWS = 32


def _router_kernel(x_ref, g_ref, sh_ref, sc_ref, wr_ref, br_ref, xh_ref, rt_ref, cnt_ref, carry):
    tm = ROUTE_TILE
    i = pl.program_id(0)

    @pl.when(i == 0)
    def _():
        carry[...] = jnp.zeros_like(carry)

    h = _norm_mod(x_ref[...], g_ref[...], sh_ref[...], sc_ref[...])
    xh_ref[:, :HALF_D] = _pack_bf16_pairs(h)
    logits = _dot(h.astype(BF16), wr_ref[...])
    s = _sigmoid(logits.T[:N_EXPERTS, :])
    sb = s + br_ref[...]
    u = [s[e:e + 1, :] for e in range(N_EXPERTS)]
    v = [sb[e:e + 1, :] for e in range(N_EXPERTS)]

    gscore = []
    for gq in range(N_EXPERT_GROUPS):
        m = v[4 * gq:4 * gq + 4]
        best = m[PAIR_LO[0]] + m[PAIR_HI[0]]
        for a, b in zip(PAIR_LO[1:], PAIR_HI[1:]):
            best = jnp.maximum(best, m[a] + m[b])
        gscore.append(best)
    gidx = jnp.zeros((1, tm), jnp.int32)
    gbest = gscore[0]
    for gq in range(1, N_EXPERT_GROUPS):
        upd = gscore[gq] > gbest
        gidx = jnp.where(upd, gq, gidx)
        gbest = jnp.where(upd, gscore[gq], gbest)

    def pick(rows, j):
        out = rows[j]
        for gq in range(1, N_EXPERT_GROUPS):
            out = jnp.where(gidx == gq, rows[4 * gq + j], out)
        return out

    vin = [pick(v, j) for j in range(EXPERTS_PER_GROUP)]
    uin = [pick(u, j) for j in range(EXPERTS_PER_GROUP)]
    i1 = jnp.zeros((1, tm), jnp.int32)
    m1 = vin[0]
    for j in range(1, EXPERTS_PER_GROUP):
        upd = vin[j] > m1
        i1 = jnp.where(upd, j, i1)
        m1 = jnp.where(upd, vin[j], m1)
    i2 = jnp.full((1, tm), -1, jnp.int32)
    m2 = jnp.full((1, tm), -jnp.inf, F32)
    for j in range(EXPERTS_PER_GROUP):
        upd = (i1 != j) & (vin[j] > m2)
        i2 = jnp.where(upd, j, i2)
        m2 = jnp.where(upd, vin[j], m2)

    def sel(rows, idx):
        out = rows[0]
        for j in range(1, EXPERTS_PER_GROUP):
            out = jnp.where(idx == j, rows[j], out)
        return out

    w1, w2 = sel(uin, i1), sel(uin, i2)
    wsum = w1 + w2
    w1, w2 = w1 / wsum, w2 / wsum
    first_lo = i1 < i2
    lo = jnp.where(first_lo, i1, i2)
    hi = jnp.where(first_lo, i2, i1)
    w_lo = jnp.where(first_lo, w1, w2)
    w_hi = jnp.where(first_lo, w2, w1)
    pair = jnp.where(lo == 0, hi - 1, jnp.where(lo == 1, hi + 1, 5))
    bucket = gidx * len(PAIR_LO) + pair

    onehot = (lax.broadcasted_iota(jnp.int32, (ROUTE_ROWS, tm), 0) == bucket)
    tri = (lax.broadcasted_iota(jnp.int32, (tm, tm), 0) <= lax.broadcasted_iota(jnp.int32, (tm, tm), 1))
    cum = _dot(jnp.where(onehot, 1.0, 0.0).astype(BF16), jnp.where(tri, 1.0, 0.0).astype(BF16))
    rank = jnp.sum(jnp.where(onehot, cum - 1.0 + carry[...], 0.0), axis=0, keepdims=True)
    carry[...] = carry[...] + cum[:, tm - 1:tm]
    cnt_ref[...] = jnp.broadcast_to(carry[...], (ROUTE_ROWS, LANE))

    rt_ref[...] = jnp.zeros_like(rt_ref)
    rt_ref[0:1, :] = bucket.astype(F32)
    rt_ref[1:2, :] = rank
    wt = jnp.concatenate([w_lo, w_hi, jnp.zeros((LANE - 2, tm), F32)], axis=0)
    xh_ref[:, HALF_D:] = lax.bitcast_convert_type(wt.T, jnp.uint32)


def _router(x, g, mod, w_router, b_router):
    tm = ROUTE_TILE
    wr = _pad2(w_router, D_MODEL, LANE).astype(BF16)
    return pl.pallas_call(
        _router_kernel,
        grid=(T // tm,),
        in_specs=[
            pl.BlockSpec((tm, D_MODEL), lambda i: (i, 0)),
            pl.BlockSpec((1, D_MODEL), lambda i: (0, 0)),
            _mod_spec(tm, 3),
            _mod_spec(tm, 4),
            pl.BlockSpec((D_MODEL, LANE), lambda i: (0, 0)),
            pl.BlockSpec((N_EXPERTS, 1), lambda i: (0, 0)),
        ],
        out_specs=[
            pl.BlockSpec((tm, XH_W), lambda i: (i, 0)),
            pl.BlockSpec((8, tm), lambda i: (0, i)),
            pl.BlockSpec((ROUTE_ROWS, LANE), lambda i: (0, 0)),
        ],
        out_shape=[
            jax.ShapeDtypeStruct((T, XH_W), jnp.uint32),
            jax.ShapeDtypeStruct((8, T), F32),
            jax.ShapeDtypeStruct((ROUTE_ROWS, LANE), F32),
        ],
        scratch_shapes=[pltpu.VMEM((ROUTE_ROWS, 1), F32)],
        compiler_params=_cp(("arbitrary",)),
        name="moe_router",
    )(x, g, mod, mod, wr, b_router.reshape(N_EXPERTS, 1))


DISPATCH_TILE = 256


DMA_UNROLL = 32


def _invert_kernel(dest_ref, src_ref):
    def clear(s, c):
        src_ref[s] = 0
        return c

    def put(t, c):
        src_ref[dest_ref[t]] = t
        return c

    lax.fori_loop(0, T_PAD, clear, 0, unroll=DMA_UNROLL)
    lax.fori_loop(0, T, put, 0, unroll=DMA_UNROLL)


def _invert(dest):
    return pl.pallas_call(
        _invert_kernel,
        in_specs=[pl.BlockSpec(memory_space=pltpu.SMEM)],
        out_specs=pl.BlockSpec(memory_space=pltpu.SMEM),
        out_shape=jax.ShapeDtypeStruct((T_PAD,), jnp.int32),
        name="moe_invert",
    )(dest)


def _gather_rows(idx_ref, base, src_hbm, buf, sem, tm, static=False):
    def start(r, c):
        pltpu.make_async_copy(src_hbm.at[pl.ds(idx_ref[base + r], 1)], buf.at[pl.ds(r, 1)], sem).start()
        return c

    if static:
        for r in range(tm):
            start(r, 0)
    else:
        lax.fori_loop(0, tm, start, 0, unroll=DMA_UNROLL)


def _wait_rows(src_hbm, buf, sem, tm):
    pltpu.make_async_copy(src_hbm.at[pl.ds(0, tm)], buf, sem).wait()


def _expert_kernel(ea_ref, eb_ref, nv_ref, src_ref, xh_hbm, ga_ref, ua_ref, da_ref, gb_ref, ub_ref, db_ref, y_ref,
                   *scratch):
    tm = MOE_TILE
    j = pl.program_id(0)
    nv = nv_ref[j]
    *bufs, sems = scratch
    n_buf = len(bufs)

    @pl.when(j == 0)
    def _():
        for t in range(MOE_AHEAD):
            _gather_rows(src_ref, t * tm, xh_hbm, bufs[t], sems.at[t], tm)

    def run(p):
        cur, cur_sem = bufs[p], sems.at[p]
        q = (p + MOE_AHEAD) % n_buf
        ahead, ahead_sem = bufs[q], sems.at[q]

        @pl.when(jnp.logical_or(j < MOE_AHEAD, nv_ref[jnp.maximum(j - MOE_AHEAD, 0)] > 0))
        def _():
            _wait_rows(xh_hbm, cur, cur_sem, tm)

        @pl.when(nv > 0)
        def _():
            valid = lax.broadcasted_iota(jnp.int32, (tm, 1), 0) < nv
            x = jnp.where(valid, _unpack_bf16_pairs(cur[:, :HALF_D]), 0.0).astype(BF16)
            wts = jnp.where(valid, lax.bitcast_convert_type(cur[:, HALF_D:], F32), 0.0)
            for r in range(tm):
                pltpu.make_async_copy(xh_hbm.at[pl.ds(src_ref[(j + MOE_AHEAD) * tm + r], 1)],
                                      ahead.at[pl.ds(r, 1)], ahead_sem).start()

            def ffn(g_ref, u_ref, d_ref, w):
                a = _dot(x, g_ref[...])
                h = a * _sigmoid(a) * _dot(x, u_ref[...]) * w
                return _dot(h.astype(BF16), d_ref[...])

            y = ffn(ga_ref, ua_ref, da_ref, wts[:, 0:1]) + ffn(gb_ref, ub_ref, db_ref, wts[:, 1:2])
            y_ref[...] = _pack_bf16_pairs(y)

    for p in range(n_buf):
        pl.when(j % n_buf == p)(functools.partial(run, p))

    @pl.when(nv == 0)
    def _():
        y_ref[...] = jnp.zeros_like(y_ref)


def _experts(layer, tile_ea, tile_eb, tile_nv, src, xh, w_gate, w_up, w_down):
    tm = MOE_TILE
    up = lambda sel: pl.BlockSpec((None, None, D_MODEL, D_EXPERT),
                                  lambda j, ea, eb, nv, sr: (layer, (ea, eb)[sel][j], 0, 0))
    down = lambda sel: pl.BlockSpec((None, None, D_EXPERT, D_MODEL),
                                    lambda j, ea, eb, nv, sr: (layer, (ea, eb)[sel][j], 0, 0))
    return pl.pallas_call(
        _expert_kernel,
        grid_spec=pltpu.PrefetchScalarGridSpec(
            num_scalar_prefetch=4,
            grid=(MOE_TILES,),
            in_specs=[pl.BlockSpec(memory_space=pl.ANY), up(0), up(0), down(0), up(1), up(1), down(1)],
            out_specs=pl.BlockSpec((tm, HALF_D), lambda j, ea, eb, nv, sr: (j, 0)),
            scratch_shapes=[pltpu.VMEM((tm, XH_W), jnp.uint32)] * (MOE_AHEAD + 1)
            + [pltpu.SemaphoreType.DMA((MOE_AHEAD + 1,))],
        ),
        out_shape=jax.ShapeDtypeStruct((T_PAD, HALF_D), jnp.uint32),
        compiler_params=_cp(("arbitrary",), 56),
        name="moe_experts",
    )(tile_ea, tile_eb, tile_nv, src, xh, w_gate, w_up, w_down, w_gate, w_up, w_down)


def _combine_kernel(*refs, final):
    tm = DISPATCH_TILE
    if final:
        dest_ref, x_ref, gt_ref, ys_hbm, fg_ref, oc_ref, ol_ref, buf, sems = refs
    else:
        dest_ref, x_ref, gt_ref, ys_hbm, o_ref, buf, sems = refs
    i = pl.program_id(0)
    slot = i % 2

    last = pl.num_programs(0) - 1

    @pl.when(i == 0)
    def _():
        _gather_rows(dest_ref, 0, ys_hbm, buf.at[0], sems.at[0], tm)

    def finish():
        _wait_rows(ys_hbm, buf.at[slot], sems.at[slot], tm)
        x = x_ref[...] + gt_ref[...] * _unpack_bf16_pairs(buf[slot])
        if not final:
            o_ref[...] = x
            return
        y = x * lax.rsqrt(jnp.mean(x * x, axis=-1, keepdims=True) + NORM_EPS) * fg_ref[...]
        is_ctx = i < TP // tm

        @pl.when(is_ctx)
        def _():
            oc_ref[...] = y

        @pl.when(jnp.logical_not(is_ctx))
        def _():
            ol_ref[...] = y

    @pl.when(i < last)
    def _():
        _gather_rows(dest_ref, (i + 1) * tm, ys_hbm, buf.at[1 - slot], sems.at[1 - slot], tm, static=True)
        finish()

    pl.when(i == last)(finish)


def _combine(dest, x, mod, ys, final_gain=None):
    tm = DISPATCH_TILE
    final = final_gain is not None
    n_ctx = TP // tm
    in_specs = [pl.BlockSpec((tm, D_MODEL), lambda i, d: (i, 0)),
                pl.BlockSpec((None, None, 1, D_MODEL), lambda i, d: (_cond_row(i * tm), 5, 0, 0)),
                pl.BlockSpec(memory_space=pl.ANY)]
    args = [dest, x, mod, ys]
    if final:
        in_specs.append(pl.BlockSpec((1, D_MODEL), lambda i, d: (0, 0)))
        args.append(final_gain.reshape(1, D_MODEL))
        out_specs = [pl.BlockSpec((tm, D_MODEL), lambda i, d: (jnp.minimum(i, n_ctx - 1), 0)),
                     pl.BlockSpec((tm, D_MODEL), lambda i, d: (jnp.maximum(i - n_ctx, 0), 0))]
        out_shape = [jax.ShapeDtypeStruct((TP, D_MODEL), F32), jax.ShapeDtypeStruct((TS, D_MODEL), F32)]
    else:
        out_specs = pl.BlockSpec((tm, D_MODEL), lambda i, d: (i, 0))
        out_shape = jax.ShapeDtypeStruct((T, D_MODEL), F32)
    return pl.pallas_call(
        functools.partial(_combine_kernel, final=final),
        grid_spec=pltpu.PrefetchScalarGridSpec(
            num_scalar_prefetch=1,
            grid=(T // tm,),
            in_specs=in_specs,
            out_specs=out_specs,
            scratch_shapes=[pltpu.VMEM((2, tm, HALF_D), jnp.uint32), pltpu.SemaphoreType.DMA((2,))],
        ),
        out_shape=out_shape,
        compiler_params=_cp(("arbitrary",)),
        name="moe_combine",
    )(*args)


def _lookup(table, idx):
    n = table.shape[0]
    hit = idx[:, None] == jnp.arange(n, dtype=jnp.int32)[None, :]
    return jnp.sum(jnp.where(hit, table[None, :], 0), axis=1)


def _moe_plan(rt, cnt):
    bucket = rt[0].astype(jnp.int32)
    rank = rt[1].astype(jnp.int32)
    counts = cnt[:N_BUCKETS, 0].astype(jnp.int32)
    tiles = (counts + MOE_TILE - 1) // MOE_TILE
    order = jnp.arange(N_BUCKETS, dtype=jnp.int32)
    tile_start = jnp.sum(jnp.where(order[None, :] < order[:, None], tiles[None, :], 0), axis=1)
    tile_end = tile_start + tiles
    n_used = tile_end[N_BUCKETS - 1]
    dest = _lookup(tile_start * MOE_TILE, bucket) + rank
    j = jnp.arange(MOE_TILES, dtype=jnp.int32)
    jc = jnp.minimum(j, n_used - 1)
    b = jnp.minimum(jnp.sum((jc[:, None] >= tile_end[None, :]).astype(jnp.int32), axis=1), N_BUCKETS - 1)
    nv = jnp.clip(_lookup(counts, b) - (j - _lookup(tile_start, b)) * MOE_TILE, 0, MOE_TILE)
    nv = jnp.where(j < n_used, nv, 0)
    n_pairs = len(PAIR_LO)
    ea = (b // n_pairs) * EXPERTS_PER_GROUP + _lookup(jnp.asarray(PAIR_LO, jnp.int32), b % n_pairs)
    eb = (b // n_pairs) * EXPERTS_PER_GROUP + _lookup(jnp.asarray(PAIR_HI, jnp.int32), b % n_pairs)
    return dest, ea, eb, nv


def _moe(layer, x, g, mod, w_router, b_router, w_gate, w_up, w_down, final_gain=None):
    xh, rt, cnt = _router(x, g, mod, w_router, b_router)
    dest, ea, eb, nv = _moe_plan(rt, cnt)
    ys = _experts(layer, ea, eb, nv, _invert(dest), xh, w_gate, w_up, w_down)
    return _combine(dest, x, mod, ys, final_gain)


def kernel(x_prompt, x_sample, cache_k_full, cache_v_full, cache_k_win, cache_v_win, c, c_ctx, w_mod, b_mod, norm_mix, norm_ffn, final_norm, pool_w, pool_scale, hy_w_in, hy_b_in, hy_conv_w, hy_conv_b, hy_f_w1, hy_f_b1, hy_f_w2, hy_f_b2, hy_f_freq, hy_f_w3, hy_decay, hy_skip, hy_w_out, hy_b_out, fa_w_qkv, fa_q_norm, fa_k_norm, fa_w_o, wa_w_qkv, wa_sink, wa_w_o, w_router, b_router, moe_w_gate, moe_w_up, moe_w_down):
    x = None
    cond =jnp.concatenate([c_ctx[None, :], c, jnp.zeros((N_COND - 1 - DEC_BATCH, D_MODEL), F32)], axis=0)
    mods = _adaln(cond, w_mod, b_mod).reshape(DEPTH, N_COND, 6, 1, D_MODEL)
    rope = _rope_tables()
    ones_hd = jnp.ones((HEAD_DIM,), F32)
    wg_bf, wu_bf, wd_bf = moe_w_gate.astype(BF16), moe_w_up.astype(BF16), moe_w_down.astype(BF16)
    new_kv = {}
    for layer in range(DEPTH):
        kind = layer % 4
        j = layer // 4
        mod = mods[layer]
        g_mix = norm_mix[layer].reshape(1, D_MODEL)
        if kind == 0:
            assert layer == 0, "the pooling mixer reads the two input streams, so it must be the first layer"
            x = _pool_mixer(x_prompt.reshape(TP, D_MODEL), x_sample.reshape(TS, D_MODEL), g_mix, mod,
                            pool_w[j], pool_scale[j])
        elif kind == 1:
            x = _hyena_mixer(x, g_mix, mod, hy_w_in[j], hy_b_in[j], hy_conv_w[j], hy_conv_b[j], hy_f_w1[j],
                             hy_f_b1[j], hy_f_w2[j], hy_f_b2[j], hy_f_freq[j], hy_f_w3[j], hy_decay[j],
                             hy_skip[j], hy_w_out[j], hy_b_out[j])
        elif kind == 2:
            x, nk, nv = _attn_mixer(x, g_mix, mod, fa_w_qkv[j], fa_q_norm[j], fa_k_norm[j], True, None,
                                    fa_w_o[j], cache_k_full[:, j], cache_v_full[:, j], False, rope)
            new_kv.setdefault("kf", []).append(nk)
            new_kv.setdefault("vf", []).append(nv)
        else:
            x, nk, nv = _attn_mixer(x, g_mix, mod, wa_w_qkv[j], ones_hd, ones_hd, False, wa_sink[j],
                                    wa_w_o[j], cache_k_win[:, j], cache_v_win[:, j], True, rope)
            new_kv.setdefault("kw", []).append(nk)
            new_kv.setdefault("vw", []).append(nv)
        x = _moe(layer, x, norm_ffn[layer].reshape(1, D_MODEL), mod, w_router, b_router, wg_bf, wu_bf, wd_bf,
                 final_norm if layer == DEPTH - 1 else None)
    y_prompt, y_sample = x
    y_prompt = y_prompt.reshape(BATCH, SEQ, D_MODEL)
    y_sample = y_sample.reshape(DEC_BATCH, DEC_SEQ, D_MODEL)
    return (y_prompt, y_sample, jnp.stack(new_kv["kf"], axis=1), jnp.stack(new_kv["vf"], axis=1),
            jnp.stack(new_kv["kw"], axis=1), jnp.stack(new_kv["vw"], axis=1))
```

```python
import functools
import math

import jax
import jax.numpy as jnp
import numpy as np
from jax import lax
from jax.experimental import pallas as pl
from jax.experimental.pallas import tpu as pltpu

D_MODEL = 2048
BATCH = 32
SEQ = 256
DEPTH = 4
DEC_BATCH = 4
DEC_SEQ = 4096
PAST_LEN = 512
GRID_W = 64
N_HEADS = 16
N_KV_HEADS = 4
HEAD_DIM = D_MODEL // N_HEADS
KV_GROUP = N_HEADS // N_KV_HEADS
KV_DIM = N_KV_HEADS * HEAD_DIM
QKV_DIM = (N_HEADS + 2 * N_KV_HEADS) * HEAD_DIM
ROPE_THETA = 10000.0
WINDOW = 128
POOL_WINDOWS = (2, 4, 8, 16)
POOL_GROUP = D_MODEL // len(POOL_WINDOWS)
HYENA_EMB_BANDS = 16
HYENA_FILTER_HIDDEN = 64
N_EXPERTS = 16
N_EXPERT_GROUPS = 4
EXPERTS_PER_GROUP = 4
D_EXPERT = 512
NORM_EPS = 1e-6
NEG_INF = -1e30

F32 = jnp.float32
BF16 = jnp.bfloat16

TP = BATCH * SEQ
TS = DEC_BATCH * DEC_SEQ
T = TP + TS
N_COND = 8
LANE = 128
MIB = 1024 * 1024

PAIR_LO = (0, 0, 0, 1, 1, 2)
PAIR_HI = (1, 2, 3, 2, 3, 3)
N_BUCKETS = N_EXPERT_GROUPS * len(PAIR_LO)
MOE_TILE = 256
MOE_AHEAD = 5
MOE_TILES = T // MOE_TILE + N_BUCKETS + MOE_AHEAD
T_PAD = MOE_TILES * MOE_TILE
HALF_D = D_MODEL // 2
XH_W = HALF_D + LANE


def _cp(sem, vmem_mb=48):
    return pltpu.CompilerParams(dimension_semantics=sem, vmem_limit_bytes=vmem_mb * MIB)


def _dot(a, b):
    return jnp.dot(a, b, preferred_element_type=F32)


def _dot3(a, b):
    ah = a.astype(BF16)
    al = (a - ah.astype(F32)).astype(BF16)
    bh = b.astype(BF16)
    bl = (b - bh.astype(F32)).astype(BF16)
    return _dot(ah, bh) + (_dot(al, bh) + _dot(ah, bl))


def _sigmoid(x):
    return 1.0 / (1.0 + jnp.exp(-x))


def _pack_bf16_pairs(x):
    n = x.shape[1] // 2
    bits = lambda v: lax.bitcast_convert_type(v.astype(BF16).astype(F32), jnp.uint32)
    return (bits(x[:, :n]) >> 16) | bits(x[:, n:])


def _unpack_bf16_pairs(u):
    lo = lax.bitcast_convert_type(u << 16, F32)
    hi = lax.bitcast_convert_type(u & jnp.uint32(0xFFFF0000), F32)
    return jnp.concatenate([lo, hi], axis=1)


def _cond_row(r):
    return jnp.where(r < TP, 0, 1 + (r - TP) // DEC_SEQ)


def _mod_spec(tm, chunk, tn=D_MODEL, ncol=False):
    if ncol:
        return pl.BlockSpec((None, None, 1, tn), lambda i, j: (_cond_row(i * tm), chunk, 0, j))
    return pl.BlockSpec((None, None, 1, tn), lambda i, *_: (_cond_row(i * tm), chunk, 0, 0))


def _norm_mod(x, g, shift, scale):
    var = jnp.mean(x * x, axis=-1, keepdims=True)
    y = x * lax.rsqrt(var + NORM_EPS) * g
    return y * (1.0 + scale) + shift


def _adaln_kernel(c_ref, w_ref, b_ref, o_ref):
    c = c_ref[...]
    a = c * _sigmoid(c)
    o_ref[...] = _dot3(a, w_ref[...]) + b_ref[...]


def _adaln(cond, w_mod, b_mod):
    tn = 1024
    n = 6 * D_MODEL
    return pl.pallas_call(
        _adaln_kernel,
        grid=(DEPTH, n // tn),
        in_specs=[
            pl.BlockSpec((N_COND, D_MODEL), lambda l, j: (0, 0)),
            pl.BlockSpec((None, D_MODEL, tn), lambda l, j: (l, 0, j)),
            pl.BlockSpec((None, 1, tn), lambda l, j: (l, 0, j)),
        ],
        out_specs=pl.BlockSpec((None, N_COND, tn), lambda l, j: (l, 0, j)),
        out_shape=jax.ShapeDtypeStruct((DEPTH, N_COND, n), F32),
        compiler_params=_cp(("parallel", "parallel")),
        name="adaln",
    )(cond, w_mod, b_mod.reshape(DEPTH, 1, n))


def _nm_matmul_kernel(x_ref, g_ref, sh_ref, sc_ref, w_ref, b_ref, o_ref, h_scr):
    @pl.when(pl.program_id(1) == 0)
    def _():
        h_scr[...] = _norm_mod(x_ref[...], g_ref[...], sh_ref[...], sc_ref[...]).astype(BF16)

    o_ref[...] = (_dot(h_scr[...], w_ref[...]) + b_ref[...]).astype(o_ref.dtype)


def _nm_matmul(x, g, mod, w, b, out_dtype, name):
    tm, tn = 1024, 1024
    n = w.shape[1]
    return pl.pallas_call(
        _nm_matmul_kernel,
        grid=(T // tm, n // tn),
        in_specs=[
            pl.BlockSpec((tm, D_MODEL), lambda i, j: (i, 0)),
            pl.BlockSpec((1, D_MODEL), lambda i, j: (0, 0)),
            _mod_spec(tm, 0),
            _mod_spec(tm, 1),
            pl.BlockSpec((D_MODEL, tn), lambda i, j: (0, j)),
            pl.BlockSpec((1, tn), lambda i, j: (0, j)),
        ],
        out_specs=pl.BlockSpec((tm, tn), lambda i, j: (i, j)),
        out_shape=jax.ShapeDtypeStruct((T, n), out_dtype),
        scratch_shapes=[pltpu.VMEM((tm, D_MODEL), BF16)],
        compiler_params=_cp(("parallel", "arbitrary")),
        name=name,
    )(x, g, mod, mod, w, b)


RESID_TM = 1024


def _resid_matmul_kernel(ap_ref, as_ref, w_ref, b_ref, x_ref, gt_ref, o_ref):
    def emit(a_ref):
        o_ref[...] = x_ref[...] + gt_ref[...] * (_dot(a_ref[...], w_ref[...]) + b_ref[...])

    is_ctx = pl.program_id(0) < TP // RESID_TM
    pl.when(is_ctx)(lambda: emit(ap_ref))
    pl.when(jnp.logical_not(is_ctx))(lambda: emit(as_ref))


def _resid_matmul(a_ctx, a_lat, w, b, x, mod, name):
    tm, tn = RESID_TM, 1024
    k = a_ctx.shape[1]
    n_ctx = TP // tm
    return pl.pallas_call(
        _resid_matmul_kernel,
        grid=(T // tm, D_MODEL // tn),
        in_specs=[
            pl.BlockSpec((tm, k), lambda i, j: (jnp.minimum(i, n_ctx - 1), 0)),
            pl.BlockSpec((tm, k), lambda i, j: (jnp.maximum(i - n_ctx, 0), 0)),
            pl.BlockSpec((k, tn), lambda i, j: (0, j)),
            pl.BlockSpec((1, tn), lambda i, j: (0, j)),
            pl.BlockSpec((tm, tn), lambda i, j: (i, j)),
            _mod_spec(tm, 2, tn, ncol=True),
        ],
        out_specs=pl.BlockSpec((tm, tn), lambda i, j: (i, j)),
        out_shape=jax.ShapeDtypeStruct((T, D_MODEL), F32),
        compiler_params=_cp(("parallel", "parallel")),
        name=name,
    )(a_ctx, a_lat, w, b, x, mod)


POOL_TILE = 256
POOL_HALO = 8


def _seq_pos(r0):
    is_ctx = r0 < TP
    loc0 = jnp.where(is_ctx, r0 % SEQ, (r0 - TP) % DEC_SEQ)
    seq_len = jnp.where(is_ctx, SEQ, DEC_SEQ)
    return loc0, seq_len


def _pool_kernel(xc_ref, xcp_ref, xcn_ref, xl_ref, xlp_ref, xln_ref, *rest):
    is_ctx = pl.program_id(0) < TP // POOL_TILE
    pl.when(is_ctx)(lambda: _pool_tile(xc_ref, xcp_ref, xcn_ref, *rest))
    pl.when(jnp.logical_not(is_ctx))(lambda: _pool_tile(xl_ref, xlp_ref, xln_ref, *rest))


def _pool_tile(x_ref, xp_ref, xn_ref, g_ref, sh_ref, sc_ref, gt_ref, pw_ref, ps_ref, o_ref, hz_scr):
    tm, hl = POOL_TILE, POOL_HALO
    loc0, seq_len = _seq_pos(pl.program_id(0) * tm)
    has_prev = loc0 > 0
    has_next = loc0 + tm < seq_len
    g, sh, sc = g_ref[...], sh_ref[...], sc_ref[...]
    x = x_ref[...]
    h = _norm_mod(x, g, sh, sc)
    hz_scr[0:hl, :] = jnp.where(has_prev, _norm_mod(xp_ref[...], g, sh, sc), 0.0)
    hz_scr[hl:hl + tm, :] = h
    hz_scr[hl + tm:, :] = jnp.where(has_next, _norm_mod(xn_ref[...], g, sh, sc), 0.0)
    tl = loc0 + lax.broadcasted_iota(jnp.int32, (tm, 1), 0)
    outs = []
    for gi, w in enumerate(POOL_WINDOWS):
        cs = slice(gi * POOL_GROUP, (gi + 1) * POOL_GROUP)
        s = jnp.zeros((tm, POOL_GROUP), F32)
        for off in range(-(w // 2), w - w // 2):
            s = s + hz_scr[hl + off:hl + off + tm, cs]
        lo = jnp.maximum(tl - w // 2, 0)
        hi = jnp.minimum(tl + (w - w // 2), seq_len)
        d = s / (hi - lo).astype(F32) - h[:, cs]
        outs.append(_dot(d.astype(BF16), pw_ref[gi]))
    out = jnp.concatenate(outs, axis=1) * ps_ref[...]
    o_ref[...] = x + gt_ref[...] * out


def _pool_mixer(x_ctx, x_lat, g, mod, pool_w, pool_scale):
    tm, hl = POOL_TILE, POOL_HALO
    r = tm // hl

    def stream(first_tile, rows):
        tile = lambda i: jnp.clip(i - first_tile, 0, rows // tm - 1)
        return [pl.BlockSpec((tm, D_MODEL), lambda i: (tile(i), 0)),
                pl.BlockSpec((hl, D_MODEL), lambda i: (jnp.maximum(tile(i) * r - 1, 0), 0)),
                pl.BlockSpec((hl, D_MODEL), lambda i: (jnp.minimum((tile(i) + 1) * r, rows // hl - 1), 0))]

    return pl.pallas_call(
        _pool_kernel,
        grid=(T // tm,),
        in_specs=stream(0, TP) + stream(TP // tm, TS) + [
            pl.BlockSpec((1, D_MODEL), lambda i: (0, 0)),
            _mod_spec(tm, 0),
            _mod_spec(tm, 1),
            _mod_spec(tm, 2),
            pl.BlockSpec((len(POOL_WINDOWS), POOL_GROUP, POOL_GROUP), lambda i: (0, 0, 0)),
            pl.BlockSpec((1, D_MODEL), lambda i: (0, 0)),
        ],
        out_specs=pl.BlockSpec((tm, D_MODEL), lambda i: (i, 0)),
        out_shape=jax.ShapeDtypeStruct((T, D_MODEL), F32),
        scratch_shapes=[pltpu.VMEM((tm + 2 * hl, D_MODEL), F32)],
        compiler_params=_cp(("parallel",)),
        name="pool_mixer",
    )(x_ctx, x_ctx, x_ctx, x_lat, x_lat, x_lat, g, mod, mod, mod, pool_w.astype(BF16),
      pool_scale.reshape(1, D_MODEL))


CONV_TILE = 256
CONV_HALO = 16


def _conv3_kernel(u_ref, up_ref, un_ref, cw_ref, cb_ref, o_ref, scr):
    tm, hl = CONV_TILE, CONV_HALO
    loc0, seq_len = _seq_pos(pl.program_id(0) * tm)
    has_prev = loc0 > 0
    has_next = loc0 + tm < seq_len
    scr[0:hl, :] = jnp.where(has_prev, up_ref[...].astype(F32), 0.0)
    scr[hl:hl + tm, :] = u_ref[...].astype(F32)
    scr[hl + tm:, :] = jnp.where(has_next, un_ref[...].astype(F32), 0.0)
    out = (scr[hl - 1:hl - 1 + tm, :] * cw_ref[0:1, :] + scr[hl:hl + tm, :] * cw_ref[1:2, :]
           + scr[hl + 1:hl + 1 + tm, :] * cw_ref[2:3, :] + cb_ref[...])
    o_ref[...] = out.astype(o_ref.dtype)


def _conv3(u0, conv_w, conv_b):
    tm, hl, tc = CONV_TILE, CONV_HALO, D_MODEL
    r = tm // hl
    n = u0.shape[1]
    return pl.pallas_call(
        _conv3_kernel,
        grid=(T // tm, n // tc),
        in_specs=[
            pl.BlockSpec((tm, tc), lambda i, j: (i, j)),
            pl.BlockSpec((hl, tc), lambda i, j: (jnp.maximum(i * r - 1, 0), j)),
            pl.BlockSpec((hl, tc), lambda i, j: (jnp.minimum((i + 1) * r, T // hl - 1), j)),
            pl.BlockSpec((3, tc), lambda i, j: (0, j)),
            pl.BlockSpec((1, tc), lambda i, j: (0, j)),
        ],
        out_specs=pl.BlockSpec((tm, tc), lambda i, j: (i, j)),
        out_shape=jax.ShapeDtypeStruct((T, n), BF16),
        scratch_shapes=[pltpu.VMEM((tm + 2 * hl, tc), F32)],
        compiler_params=_cp(("parallel", "parallel")),
        name="hyena_conv3",
    )(u0, u0, u0, conv_w, conv_b.reshape(1, n))


FILT_TILE = 256


HYENA_BLOCK = 1024


T_LANE = LANE - 1


def _filter_mlp_kernel(emb_ref, w1_ref, b1_ref, w2_ref, b2_ref, fr_ref, o_ref):
    emb = emb_ref[...]
    fr = fr_ref[...]
    a = jnp.sin(fr * (_dot3(emb, w1_ref[...]) + b1_ref[...]))
    a = jnp.sin(fr * (_dot3(a, w2_ref[...]) + b2_ref[...]))
    lane = lax.broadcasted_iota(jnp.int32, a.shape, 1)
    o_ref[...] = jnp.where(lane == T_LANE, emb[:, 0:1], a)


def _filter_mlp(pos, L, f_w1, f_b1, f_w2, f_b2, f_freq):
    assert HYENA_FILTER_HIDDEN <= T_LANE
    tl = FILT_TILE
    rows = pos.shape[0]
    small = lambda: pl.BlockSpec((LANE, LANE), lambda i: (0, 0))
    vec = lambda: pl.BlockSpec((1, LANE), lambda i: (0, 0))
    return pl.pallas_call(
        _filter_mlp_kernel,
        grid=(rows // tl,),
        in_specs=[pl.BlockSpec((tl, LANE), lambda i: (i, 0)), small(), vec(), small(), vec(), vec()],
        out_specs=pl.BlockSpec((tl, LANE), lambda i: (i, 0)),
        out_shape=jax.ShapeDtypeStruct((rows, LANE), F32),
        compiler_params=_cp(("parallel",)),
        name="hyena_filter_mlp",
    )(_filter_embedding(pos, L), _pad2(f_w1, LANE, LANE), _pad2(f_b1[None], 1, LANE),
      _pad2(f_w2, LANE, LANE), _pad2(f_b2[None], 1, LANE), _pad2(f_freq[None], 1, LANE))


def _filter_kernel(h1_ref, h2_ref, w3a_ref, dca_ref, w3b_ref, dcb_ref, fa_ref, fb_ref, *, blk):
    def taps(h, w3_ref, dc_ref):
        return _dot3(h, w3_ref[...]) * jnp.exp(-h[:, T_LANE:] * jnp.abs(dc_ref[...]))

    pos = taps(h1_ref[...], w3a_ref, dca_ref)
    neg = taps(h2_ref[...], w3b_ref, dcb_ref)
    m = (pl.program_id(0) * FILT_TILE + lax.broadcasted_iota(jnp.int32, (FILT_TILE, 1), 0)) % blk
    fa_ref[...] = jnp.where(m == 0, pos, pos + neg).astype(BF16)
    fb_ref[...] = jnp.where(m == 0, 0.0, neg - pos).astype(BF16)


def _pad2(a, rows, cols):
    return jnp.pad(a, ((0, rows - a.shape[0]), (0, cols - a.shape[1])))


def _filter_positions(L, blk):
    n_blk = L // blk
    m = np.arange(blk)
    p1, p2 = [], []
    for d in range(-(n_blk - 1), n_blk):
        if d >= 1:
            p1.append(d * blk + m), p2.append(d * blk - m)
        elif d == 0:
            p1.append(m), p2.append(m)
        else:
            p1.append(-d * blk - m), p2.append(-d * blk + m)
    return np.concatenate(p1), np.concatenate(p2)


def _filter_embedding(pos, L):
    t = jnp.asarray(pos, F32) / L
    bands = jnp.linspace(1e-4, HYENA_EMB_BANDS - 1, HYENA_EMB_BANDS, dtype=F32)
    ang = (2 * math.pi) * t[:, None] * bands[None, :]
    return _pad2(jnp.concatenate([t[:, None], jnp.cos(ang), -jnp.sin(ang)], axis=-1), pos.shape[0], LANE)


def _hyena_filters(L, blk, f_w1, f_b1, f_w2, f_b2, f_freq, f_w3, decay):
    n_blk = L // blk
    p1, p2 = _filter_positions(L, blk)
    rows = p1.shape[0]
    tl = FILT_TILE
    tiles_per_lag = blk // tl
    lag = lambda i: i // tiles_per_lag - (n_blk - 1)
    col1 = lambda i, n: 2 * n + jnp.where(lag(i) >= 0, 0, 1)
    col2 = lambda i, n: 2 * n + jnp.where(lag(i) >= 1, 0, 1)
    n_tiles = rows // tl
    out = pl.BlockSpec((None, tl, D_MODEL), lambda i, n: (n, i, 0))
    w3 = _pad2(f_w3, LANE, f_w3.shape[1])
    hidden = _filter_mlp(np.concatenate([p1, p2]), L, f_w1, f_b1, f_w2, f_b2, f_freq)
    return pl.pallas_call(
        functools.partial(_filter_kernel, blk=blk),
        grid=(n_tiles, 2),
        in_specs=[
            pl.BlockSpec((tl, LANE), lambda i, n: (i, 0)),
            pl.BlockSpec((tl, LANE), lambda i, n: (n_tiles + i, 0)),
            pl.BlockSpec((LANE, D_MODEL), lambda i, n: (0, col1(i, n))),
            pl.BlockSpec((1, D_MODEL), lambda i, n: (0, col1(i, n))),
            pl.BlockSpec((LANE, D_MODEL), lambda i, n: (0, col2(i, n))),
            pl.BlockSpec((1, D_MODEL), lambda i, n: (0, col2(i, n))),
        ],
        out_specs=[out, out],
        out_shape=[jax.ShapeDtypeStruct((2, rows, D_MODEL), BF16)] * 2,
        compiler_params=_cp(("parallel", "parallel")),
        name="hyena_filters",
    )(hidden, hidden, w3, decay[None], w3, decay[None])


def _dft_mats(L):
    r = int(math.isqrt(L))
    k2 = 2 * jnp.arange(L, dtype=jnp.int32)[:, None] + 1
    n1 = r * jnp.arange(L // r, dtype=jnp.int32)[None, :]
    n2 = jnp.arange(r, dtype=jnp.int32)[None, :]
    sc = math.pi / (2 * L)
    aa = ((k2 * n1) % (4 * L)).astype(F32) * sc
    ab = ((k2 * n2) % (4 * L)).astype(F32) * sc
    ca, sa, cb, sb = jnp.cos(aa)[:, :, None], jnp.sin(aa)[:, :, None], jnp.cos(ab)[:, None, :], jnp.sin(ab)[:, None, :]
    c = (ca * cb - sa * sb).reshape(L, L)
    s = (sa * cb + ca * sb).reshape(L, L)
    return c.astype(BF16), s.astype(BF16), c.T.astype(BF16), s.T.astype(BF16)


def _dft_tiles(L):
    return min(512, L), 512


def _dft_filter_kernel(c_ref, s_ref, a_ref, b_ref, gr_ref, gi_ref):
    gr_ref[...] = _dot(c_ref[...], a_ref[...]).astype(gr_ref.dtype)
    gi_ref[...] = _dot(s_ref[...], b_ref[...]).astype(gi_ref.dtype)


def _dft_filter(cm, sm, fa, fb, L):
    tf, tn = _dft_tiles(L)
    n = fa.shape[0]
    mat = lambda: pl.BlockSpec((tf, L), lambda k, c, s: (k, 0))
    rhs = lambda: pl.BlockSpec((None, L, tn), lambda k, c, s: (s, 0, c))
    out = pl.BlockSpec((None, tf, tn), lambda k, c, s: (s, k, c))
    return pl.pallas_call(
        _dft_filter_kernel,
        grid=(L // tf, D_MODEL // tn, n),
        in_specs=[mat(), mat(), rhs(), rhs()],
        out_specs=[out, out],
        out_shape=[jax.ShapeDtypeStruct((n, L, D_MODEL), BF16)] * 2,
        compiler_params=_cp(("parallel", "parallel", "parallel")),
        name="hyena_filter_dft",
    )(cm, sm, fa, fb)


FWD_TF = 256


def _dft_fwd_kernel(c_ref, s_ref, z_ref, gr_ref, gi_ref, yr_ref, yi_ref, *, n_blk, blk, bpb):
    c, s = c_ref[...], s_ref[...]
    for bb in range(bpb):
        zc, zs = [], []
        for j in range(n_blk):
            r = (bb * n_blk + j) * blk
            zj = z_ref[r:r + blk, :]
            zc.append(_dot(c, zj).astype(BF16))
            zs.append(_dot(s, zj).astype(BF16))
        for i in range(n_blk):
            yr = yi = None
            for j in range(n_blk):
                lag = i - j + n_blk - 1
                gr, gi = gr_ref[lag], gi_ref[lag]
                tr = gr * zc[j] + gi * zs[j]
                ti = gi * zc[j] - gr * zs[j]
                yr = tr if yr is None else yr + tr
                yi = ti if yi is None else yi + ti
            yr_ref[bb, i] = yr.astype(BF16)
            yi_ref[bb, i] = yi.astype(BF16)


def _seqs_per_step(L):
    return max(1, 2048 // L)


def _dft_fwd(cm, sm, z, z_rowblk, z_colblk, gr, gi, order, nb, L, blk):
    n_blk = L // blk
    bpb = _seqs_per_step(L)
    assert nb % bpb == 0 and z_rowblk % bpb == 0
    tf, tn = min(FWD_TF, blk), 512
    mat = lambda: pl.BlockSpec((tf, blk), lambda k, c, b: (k, 0))
    gsp = lambda: pl.BlockSpec((None, 2 * n_blk - 1, tf, tn), lambda k, c, b: (order, 0, k, c))
    out = pl.BlockSpec((bpb, n_blk, tf, tn), lambda k, c, b: (b, 0, k, c))
    return pl.pallas_call(
        functools.partial(_dft_fwd_kernel, n_blk=n_blk, blk=blk, bpb=bpb),
        grid=(blk // tf, D_MODEL // tn, nb // bpb),
        in_specs=[mat(), mat(),
                  pl.BlockSpec((bpb * L, tn), lambda k, c, b: (z_rowblk // bpb + b, z_colblk + c)),
                  gsp(), gsp()],
        out_specs=[out, out],
        out_shape=[jax.ShapeDtypeStruct((nb, n_blk, blk, D_MODEL), BF16)] * 2,
        compiler_params=_cp(("parallel", "parallel", "parallel")),
        name="hyena_dft_fwd",
    )(cm, sm, z, gr, gi)


def _dft_inv_kernel(ct_ref, st_ref, yr_ref, yi_ref, z_ref, gt_ref, sk_ref, o_ref, *, inv_len, bpb, tt):
    for bb in range(bpb):
        rows = slice(bb * tt, (bb + 1) * tt)
        y = (_dot(ct_ref[...], yr_ref[bb]) - _dot(st_ref[...], yi_ref[bb])) * inv_len
        o_ref[rows, :] = (gt_ref[rows, :].astype(F32) * (y + sk_ref[...] * z_ref[rows, :].astype(F32))).astype(BF16)


def _dft_inv(ctm, stm, yr, yi, z, z_rowblk, z_colblk, gate, g_rowblk, g_colblk, skip, nb, L):
    tt, tn = _dft_tiles(L)
    rpb = L // tt
    bpb = _seqs_per_step(L) if rpb == 1 else 1
    assert nb % bpb == 0 and z_rowblk % bpb == 0 and g_rowblk % bpb == 0
    mat = lambda: pl.BlockSpec((tt, L), lambda t, c, b: (t, 0))
    spec = lambda: pl.BlockSpec((bpb, L, tn), lambda t, c, b: (b, 0, c))
    rows = lambda blk0: (lambda t, c, b: ((blk0 + b * bpb * rpb + t) // bpb))
    return pl.pallas_call(
        functools.partial(_dft_inv_kernel, inv_len=1.0 / L, bpb=bpb, tt=tt),
        grid=(rpb, D_MODEL // tn, nb // bpb),
        in_specs=[mat(), mat(), spec(), spec(),
                  pl.BlockSpec((bpb * tt, tn), lambda t, c, b: (rows(z_rowblk)(t, c, b), z_colblk + c)),
                  pl.BlockSpec((bpb * tt, tn), lambda t, c, b: (rows(g_rowblk)(t, c, b), g_colblk + c)),
                  pl.BlockSpec((1, tn), lambda t, c, b: (0, c))],
        out_specs=pl.BlockSpec((bpb * tt, tn), lambda t, c, b: (rows(0)(t, c, b), c)),
        out_shape=jax.ShapeDtypeStruct((nb * L, D_MODEL), BF16),
        compiler_params=_cp(("parallel", "parallel", "parallel")),
        name="hyena_dft_inv",
    )(ctm, stm, yr, yi, z, gate, skip)


def _hyena_stream(u, row0, nb, L, fparams, skip):
    blk = min(HYENA_BLOCK, L)
    n_blk = L // blk
    n_lag = 2 * n_blk - 1
    cm, sm, ctm, stm = _dft_mats(blk)
    fa, fb = _hyena_filters(L, blk, *fparams)
    seg = lambda a: a.reshape(2 * n_lag, blk, D_MODEL)
    gr, gi = _dft_filter(cm, sm, seg(fa), seg(fb), blk)
    gr, gi = (a.reshape(2, n_lag, blk, D_MODEL) for a in (gr, gi))
    tt, tn = _dft_tiles(blk)
    ncb = D_MODEL // tn
    blocks = lambda a: a.reshape(nb * n_blk, blk, D_MODEL)

    yr, yi = _dft_fwd(cm, sm, u, row0 // L, 0, gr, gi, 0, nb, L, blk)
    z1 = _dft_inv(ctm, stm, blocks(yr), blocks(yi), u, row0 // tt, 0, u, row0 // tt, ncb, skip[0:1],
                  nb * n_blk, blk)
    yr, yi = _dft_fwd(cm, sm, z1, 0, 0, gr, gi, 1, nb, L, blk)
    return _dft_inv(ctm, stm, blocks(yr), blocks(yi), z1, 0, 0, u, row0 // tt, 2 * ncb, skip[1:2],
                    nb * n_blk, blk)


def _hyena_mixer(x, g, mod, w_in, b_in, conv_w, conv_b, f_w1, f_b1, f_w2, f_b2, f_freq, f_w3, decay, skip,
                 w_out, b_out):
    u0 = _nm_matmul(x, g, mod, w_in.astype(BF16), b_in.reshape(1, -1), BF16, "hyena_in_proj")
    u = _conv3(u0, conv_w, conv_b)
    fparams = (f_w1, f_b1, f_w2, f_b2, f_freq, f_w3, decay)
    zp = _hyena_stream(u, 0, BATCH, SEQ, fparams, skip)
    zs = _hyena_stream(u, TP, DEC_BATCH, DEC_SEQ, fparams, skip)
    return _resid_matmul(zp, zs, w_out.astype(BF16), b_out.reshape(1, -1), x, mod, "hyena_out_proj")


def _rope_tables():
    pos = jnp.arange(DEC_SEQ, dtype=jnp.int32)
    row = (pos // GRID_W).astype(F32)
    col = (pos % GRID_W).astype(F32)
    axis_dim = HEAD_DIM // 2
    inv_freq = ROPE_THETA ** (-jnp.arange(0, axis_dim, 2, dtype=F32) / axis_dim)
    ar = row[:, None] * inv_freq[None, :]
    ac = col[:, None] * inv_freq[None, :]
    cos = jnp.concatenate([jnp.cos(ar), jnp.cos(ar), jnp.cos(ac), jnp.cos(ac)], axis=-1)
    sin = jnp.concatenate([-jnp.sin(ar), jnp.sin(ar), -jnp.sin(ac), jnp.sin(ac)], axis=-1)
    return cos, sin


QKV_TM = 512
QKV_TN = 1024
PAIR = 2 * HEAD_DIM


def _qkv_kernel(x_ref, g_ref, sh_ref, sc_ref, w_ref, qn_ref, kn_ref, cos_ref, sin_ref,
                q_ref, k_ref, v_ref, nk_ref, nv_ref, h_scr, *, use_norm):
    tm = QKV_TM
    i, j = pl.program_id(0), pl.program_id(1)

    @pl.when(j == 0)
    def _():
        h_scr[...] = _norm_mod(x_ref[...], g_ref[...], sh_ref[...], sc_ref[...]).astype(BF16)

    quarter = HEAD_DIM // 4
    scale = HEAD_DIM ** -0.5 * LOG2E

    def head(xh, gn):
        if use_norm:
            xh = xh * lax.rsqrt(jnp.mean(xh * xh, axis=-1, keepdims=True) + NORM_EPS) * gn
        return xh

    def rope(xh):
        lane = lax.broadcasted_iota(jnp.int32, (tm, HEAD_DIM), 1)
        first = (lane % (2 * quarter)) < quarter
        partner = jnp.where(first, pltpu.roll(xh, HEAD_DIM - quarter, 1), pltpu.roll(xh, quarter, 1))
        return xh * cos_ref[...] + partner * sin_ref[...]

    def proj(c0):
        return _dot(h_scr[...], w_ref[:, c0:c0 + PAIR])

    def q_tile(latent):
        for p in range(QKV_TN // PAIR):
            acc = proj(p * PAIR)
            for t in range(2):
                xh = head(acc[:, t * HEAD_DIM:(t + 1) * HEAD_DIM], qn_ref[...])
                xh = rope(xh) if latent else xh
                c0 = p * PAIR + t * HEAD_DIM
                q_ref[:, c0:c0 + HEAD_DIM] = (xh * scale).astype(BF16)

    def kv_tile(latent):
        for p in range(KV_DIM // PAIR):
            acc = proj(p * PAIR)
            for t in range(2):
                c0 = p * PAIR + t * HEAD_DIM
                kh = head(acc[:, t * HEAD_DIM:(t + 1) * HEAD_DIM], kn_ref[...])
                if not latent:
                    nk_ref[:, c0:c0 + HEAD_DIM] = kh
                k_ref[:, c0:c0 + HEAD_DIM] = (rope(kh) if latent else kh).astype(BF16)
        for p in range(KV_DIM // PAIR):
            acc = proj(KV_DIM + p * PAIR)
            if not latent:
                nv_ref[:, p * PAIR:(p + 1) * PAIR] = acc
            v_ref[:, p * PAIR:(p + 1) * PAIR] = acc.astype(BF16)

    is_ctx = i < TP // tm
    is_q = j < D_MODEL // QKV_TN
    for latent in (False, True):
        stream = jnp.logical_not(is_ctx) if latent else is_ctx
        pl.when(jnp.logical_and(stream, is_q))(functools.partial(q_tile, latent))
        pl.when(jnp.logical_and(stream, jnp.logical_not(is_q)))(functools.partial(kv_tile, latent))


def _qkv_proj(x, g, mod, w_qkv, q_norm, k_norm, use_norm, rope):
    tm, tn = QKV_TM, QKV_TN
    n_ctx = TP // tm
    n_q = D_MODEL // tn
    tab = lambda: pl.BlockSpec((tm, HEAD_DIM), lambda i, j: (jnp.maximum(i - n_ctx, 0) % (DEC_SEQ // tm), 0))
    kv = lambda: pl.BlockSpec((tm, KV_DIM), lambda i, j: (i, 0))
    new = lambda: pl.BlockSpec((tm, KV_DIM), lambda i, j: (jnp.minimum(i, n_ctx - 1), 0))
    return pl.pallas_call(
        functools.partial(_qkv_kernel, use_norm=use_norm),
        grid=(T // tm, QKV_DIM // tn),
        in_specs=[
            pl.BlockSpec((tm, D_MODEL), lambda i, j: (i, 0)),
            pl.BlockSpec((1, D_MODEL), lambda i, j: (0, 0)),
            _mod_spec(tm, 0),
            _mod_spec(tm, 1),
            pl.BlockSpec((D_MODEL, tn), lambda i, j: (0, j)),
            pl.BlockSpec((1, HEAD_DIM), lambda i, j: (0, 0)),
            pl.BlockSpec((1, HEAD_DIM), lambda i, j: (0, 0)),
            tab(), tab(),
        ],
        out_specs=[pl.BlockSpec((tm, tn), lambda i, j: (i, jnp.minimum(j, n_q - 1))), kv(), kv(), new(), new()],
        out_shape=[jax.ShapeDtypeStruct((T, D_MODEL), BF16), jax.ShapeDtypeStruct((T, KV_DIM), BF16),
                   jax.ShapeDtypeStruct((T, KV_DIM), BF16), jax.ShapeDtypeStruct((TP, KV_DIM), F32),
                   jax.ShapeDtypeStruct((TP, KV_DIM), F32)],
        scratch_shapes=[pltpu.VMEM((tm, D_MODEL), BF16)],
        compiler_params=_cp(("arbitrary", "arbitrary")),
        name="qkv_proj",
    )(x, g, mod, mod, w_qkv, q_norm.reshape(1, HEAD_DIM), k_norm.reshape(1, HEAD_DIM), *rope)


LOG2E = math.log2(math.e)
ATTN_TQ = 256


def _attn_kernel(*refs, tq, seq_len, has_ctx, windowed, has_sink):
    it = iter(refs)
    q_ref, k_ref, v_ref = next(it), next(it), next(it)
    kc_ref, vc_ref = (next(it), next(it)) if has_ctx else (None, None)
    sink_ref = next(it) if has_sink else None
    o_ref = next(it)
    if windowed:
        i = pl.program_id(2)
        span = tq + 2 * WINDOW
        start = pl.multiple_of(jnp.clip(i * tq - WINDOW, 0, seq_len - span), WINDOW)
        qpos = i * tq + lax.broadcasted_iota(jnp.int32, (tq, 1), 0)
        kpos = start + lax.broadcasted_iota(jnp.int32, (1, span), 1)
        segs = [(k_ref, v_ref, pl.ds(start, span), jnp.abs(kpos - qpos) <= WINDOW)]
    else:
        segs = [(k_ref, v_ref, slice(None), None)]
    if has_ctx:
        segs.append((kc_ref, vc_ref, slice(None), None))
    for h in range(KV_GROUP):
        hs = slice(h * HEAD_DIM, (h + 1) * HEAD_DIM)
        qh = q_ref[:, hs]
        scores = []
        m = None
        for kr, _, rows, mask in segs:
            s = lax.dot_general(qh, kr[rows, :], (((1,), (1,)), ((), ())), preferred_element_type=F32)
            if mask is not None:
                s = jnp.where(mask, s, NEG_INF)
            scores.append(s)
            ms = jnp.max(s, axis=-1, keepdims=True)
            m = ms if m is None else jnp.maximum(m, ms)
        if has_sink:
            sk = sink_ref[pl.program_id(1) * KV_GROUP + h]
            m = jnp.maximum(m, sk)
        l = jnp.exp2(sk - m) if has_sink else jnp.zeros_like(m)
        acc = jnp.zeros((tq, HEAD_DIM), F32)
        for (_, vr, rows, _), s in zip(segs, scores):
            p = jnp.exp2(s - m)
            l = l + jnp.sum(p, axis=-1, keepdims=True)
            acc = acc + _dot(p.astype(BF16), vr[rows, :])
        o_ref[:, hs] = (acc / l).astype(BF16)


def _attention(q, row0, k, v, k_row0, n_keys, k_ctx, v_ctx, sink, nb, L, windowed):
    tq = min(ATTN_TQ, L)
    nq = L // tq
    q_blk0 = row0 // tq
    seq0 = k_row0 // n_keys
    own = lambda: pl.BlockSpec((n_keys, HEAD_DIM), lambda b, g, i: (seq0 + b, g))
    in_specs = [pl.BlockSpec((tq, KV_GROUP * HEAD_DIM), lambda b, g, i: (q_blk0 + b * nq + i, g)), own(), own()]
    args = [q, k, v]
    if k_ctx is not None:
        ctx = lambda: pl.BlockSpec((None, PAST_LEN, HEAD_DIM), lambda b, g, i: (b, 0, g))
        in_specs += [ctx(), ctx()]
        args += [k_ctx, v_ctx]
    if sink is not None:
        in_specs.append(pl.BlockSpec(memory_space=pltpu.SMEM))
        args.append(sink.astype(F32) * LOG2E)
    return pl.pallas_call(
        functools.partial(_attn_kernel, tq=tq, seq_len=L, has_ctx=k_ctx is not None, windowed=windowed,
                          has_sink=sink is not None),
        grid=(nb, N_KV_HEADS, nq),
        in_specs=in_specs,
        out_specs=pl.BlockSpec((tq, KV_GROUP * HEAD_DIM), lambda b, g, i: (b * nq + i, g)),
        out_shape=jax.ShapeDtypeStruct((nb * L, D_MODEL), BF16),
        compiler_params=_cp(("parallel", "parallel", "parallel"), 56),
        name="attention",
    )(*args)


def _attn_mixer(x, g, mod, w_qkv, q_norm, k_norm, use_norm, sink, w_o, cache_k, cache_v, windowed, rope):
    q, k, v, new_k, new_v = _qkv_proj(x, g, mod, w_qkv.astype(BF16), q_norm, k_norm, use_norm, rope)
    op = _attention(q, 0, k, v, 0, SEQ, None, None, sink, BATCH, SEQ, False)
    kc = cache_k.reshape(DEC_BATCH, PAST_LEN, KV_DIM).astype(BF16)
    vc = cache_v.reshape(DEC_BATCH, PAST_LEN, KV_DIM).astype(BF16)
    if windowed:
        osm = _attention(q, TP, k, v, TP, DEC_SEQ, kc, vc, sink, DEC_BATCH, DEC_SEQ, True)
    else:
        n_keys = DEC_SEQ + PAST_LEN
        both = lambda a, c: jnp.concatenate([a[TP:].reshape(DEC_BATCH, DEC_SEQ, KV_DIM), c], axis=1).reshape(
            DEC_BATCH * n_keys, KV_DIM)
        osm = _attention(q, TP, both(k, kc), both(v, vc), 0, n_keys, None, None, sink, DEC_BATCH, DEC_SEQ, False)
    x = _resid_matmul(op, osm, w_o.astype(BF16), jnp.zeros((1, D_MODEL), F32), x, mod, "attn_out_proj")
    shape = (BATCH, SEQ, N_KV_HEADS, HEAD_DIM)
    return x, new_k.reshape(shape), new_v.reshape(shape)


ROUTE_TILE = 512
ROUTE_ROWS = 32


def _router_kernel(x_ref, g_ref, sh_ref, sc_ref, wr_ref, br_ref, xh_ref, rt_ref, cnt_ref, carry):
    tm = ROUTE_TILE
    i = pl.program_id(0)

    @pl.when(i == 0)
    def _():
        carry[...] = jnp.zeros_like(carry)

    h = _norm_mod(x_ref[...], g_ref[...], sh_ref[...], sc_ref[...])
    xh_ref[:, :HALF_D] = _pack_bf16_pairs(h)
    logits = _dot(h.astype(BF16), wr_ref[...])
    s = _sigmoid(logits.T[:N_EXPERTS, :])
    sb = s + br_ref[...]
    u = [s[e:e + 1, :] for e in range(N_EXPERTS)]
    v = [sb[e:e + 1, :] for e in range(N_EXPERTS)]

    gscore = []
    for gq in range(N_EXPERT_GROUPS):
        m = v[4 * gq:4 * gq + 4]
        best = m[PAIR_LO[0]] + m[PAIR_HI[0]]
        for a, b in zip(PAIR_LO[1:], PAIR_HI[1:]):
            best = jnp.maximum(best, m[a] + m[b])
        gscore.append(best)
    gidx = jnp.zeros((1, tm), jnp.int32)
    gbest = gscore[0]
    for gq in range(1, N_EXPERT_GROUPS):
        upd = gscore[gq] > gbest
        gidx = jnp.where(upd, gq, gidx)
        gbest = jnp.where(upd, gscore[gq], gbest)

    def pick(rows, j):
        out = rows[j]
        for gq in range(1, N_EXPERT_GROUPS):
            out = jnp.where(gidx == gq, rows[4 * gq + j], out)
        return out

    vin = [pick(v, j) for j in range(EXPERTS_PER_GROUP)]
    uin = [pick(u, j) for j in range(EXPERTS_PER_GROUP)]
    i1 = jnp.zeros((1, tm), jnp.int32)
    m1 = vin[0]
    for j in range(1, EXPERTS_PER_GROUP):
        upd = vin[j] > m1
        i1 = jnp.where(upd, j, i1)
        m1 = jnp.where(upd, vin[j], m1)
    i2 = jnp.full((1, tm), -1, jnp.int32)
    m2 = jnp.full((1, tm), -jnp.inf, F32)
    for j in range(EXPERTS_PER_GROUP):
        upd = (i1 != j) & (vin[j] > m2)
        i2 = jnp.where(upd, j, i2)
        m2 = jnp.where(upd, vin[j], m2)

    def sel(rows, idx):
        out = rows[0]
        for j in range(1, EXPERTS_PER_GROUP):
            out = jnp.where(idx == j, rows[j], out)
        return out

    w1, w2 = sel(uin, i1), sel(uin, i2)
    wsum = w1 + w2
    w1, w2 = w1 / wsum, w2 / wsum
    first_lo = i1 < i2
    lo = jnp.where(first_lo, i1, i2)
    hi = jnp.where(first_lo, i2, i1)
    w_lo = jnp.where(first_lo, w1, w2)
    w_hi = jnp.where(first_lo, w2, w1)
    pair = jnp.where(lo == 0, hi - 1, jnp.where(lo == 1, hi + 1, 5))
    bucket = gidx * len(PAIR_LO) + pair

    onehot = (lax.broadcasted_iota(jnp.int32, (ROUTE_ROWS, tm), 0) == bucket)
    tri = (lax.broadcasted_iota(jnp.int32, (tm, tm), 0) <= lax.broadcasted_iota(jnp.int32, (tm, tm), 1))
    cum = _dot(jnp.where(onehot, 1.0, 0.0).astype(BF16), jnp.where(tri, 1.0, 0.0).astype(BF16))
    rank = jnp.sum(jnp.where(onehot, cum - 1.0 + carry[...], 0.0), axis=0, keepdims=True)
    carry[...] = carry[...] + cum[:, tm - 1:tm]
    cnt_ref[...] = jnp.broadcast_to(carry[...], (ROUTE_ROWS, LANE))

    rt_ref[...] = jnp.zeros_like(rt_ref)
    rt_ref[0:1, :] = bucket.astype(F32)
    rt_ref[1:2, :] = rank
    wt = jnp.concatenate([w_lo, w_hi, jnp.zeros((LANE - 2, tm), F32)], axis=0)
    xh_ref[:, HALF_D:] = lax.bitcast_convert_type(wt.T, jnp.uint32)


def _router(x, g, mod, w_router, b_router):
    tm = ROUTE_TILE
    wr = _pad2(w_router, D_MODEL, LANE).astype(BF16)
    return pl.pallas_call(
        _router_kernel,
        grid=(T // tm,),
        in_specs=[
            pl.BlockSpec((tm, D_MODEL), lambda i: (i, 0)),
            pl.BlockSpec((1, D_MODEL), lambda i: (0, 0)),
            _mod_spec(tm, 3),
            _mod_spec(tm, 4),
            pl.BlockSpec((D_MODEL, LANE), lambda i: (0, 0)),
            pl.BlockSpec((N_EXPERTS, 1), lambda i: (0, 0)),
        ],
        out_specs=[
            pl.BlockSpec((tm, XH_W), lambda i: (i, 0)),
            pl.BlockSpec((8, tm), lambda i: (0, i)),
            pl.BlockSpec((ROUTE_ROWS, LANE), lambda i: (0, 0)),
        ],
        out_shape=[
            jax.ShapeDtypeStruct((T, XH_W), jnp.uint32),
            jax.ShapeDtypeStruct((8, T), F32),
            jax.ShapeDtypeStruct((ROUTE_ROWS, LANE), F32),
        ],
        scratch_shapes=[pltpu.VMEM((ROUTE_ROWS, 1), F32)],
        compiler_params=_cp(("arbitrary",)),
        name="moe_router",
    )(x, g, mod, mod, wr, b_router.reshape(N_EXPERTS, 1))


DISPATCH_TILE = 256


DMA_UNROLL = 32


def _invert_kernel(dest_ref, src_ref):
    def clear(s, c):
        src_ref[s] = 0
        return c

    def put(t, c):
        src_ref[dest_ref[t]] = t
        return c

    lax.fori_loop(0, T_PAD, clear, 0, unroll=DMA_UNROLL)
    lax.fori_loop(0, T, put, 0, unroll=DMA_UNROLL)


def _invert(dest):
    return pl.pallas_call(
        _invert_kernel,
        in_specs=[pl.BlockSpec(memory_space=pltpu.SMEM)],
        out_specs=pl.BlockSpec(memory_space=pltpu.SMEM),
        out_shape=jax.ShapeDtypeStruct((T_PAD,), jnp.int32),
        name="moe_invert",
    )(dest)


def _gather_rows(idx_ref, base, src_hbm, buf, sem, tm, static=False):
    def start(r, c):
        pltpu.make_async_copy(src_hbm.at[pl.ds(idx_ref[base + r], 1)], buf.at[pl.ds(r, 1)], sem).start()
        return c

    if static:
        for r in range(tm):
            start(r, 0)
    else:
        lax.fori_loop(0, tm, start, 0, unroll=DMA_UNROLL)


def _wait_rows(src_hbm, buf, sem, tm):
    pltpu.make_async_copy(src_hbm.at[pl.ds(0, tm)], buf, sem).wait()


def _expert_kernel(ea_ref, eb_ref, nv_ref, src_ref, xh_hbm, ga_ref, ua_ref, da_ref, gb_ref, ub_ref, db_ref, y_ref,
                   *scratch):
    tm = MOE_TILE
    j = pl.program_id(0)
    nv = nv_ref[j]
    *bufs, sems = scratch
    n_buf = len(bufs)

    @pl.when(j == 0)
    def _():
        for t in range(MOE_AHEAD):
            _gather_rows(src_ref, t * tm, xh_hbm, bufs[t], sems.at[t], tm)

    def run(p):
        cur, cur_sem = bufs[p], sems.at[p]
        q = (p + MOE_AHEAD) % n_buf
        ahead, ahead_sem = bufs[q], sems.at[q]

        @pl.when(jnp.logical_or(j < MOE_AHEAD, nv_ref[jnp.maximum(j - MOE_AHEAD, 0)] > 0))
        def _():
            _wait_rows(xh_hbm, cur, cur_sem, tm)

        @pl.when(nv > 0)
        def _():
            valid = lax.broadcasted_iota(jnp.int32, (tm, 1), 0) < nv
            x = jnp.where(valid, _unpack_bf16_pairs(cur[:, :HALF_D]), 0.0).astype(BF16)
            wts = jnp.where(valid, lax.bitcast_convert_type(cur[:, HALF_D:], F32), 0.0)
            for r in range(tm):
                pltpu.make_async_copy(xh_hbm.at[pl.ds(src_ref[(j + MOE_AHEAD) * tm + r], 1)],
                                      ahead.at[pl.ds(r, 1)], ahead_sem).start()

            def ffn(g_ref, u_ref, d_ref, w):
                a = _dot(x, g_ref[...])
                h = a * _sigmoid(a) * _dot(x, u_ref[...]) * w
                return _dot(h.astype(BF16), d_ref[...])

            y = ffn(ga_ref, ua_ref, da_ref, wts[:, 0:1]) + ffn(gb_ref, ub_ref, db_ref, wts[:, 1:2])
            y_ref[...] = _pack_bf16_pairs(y)

    for p in range(n_buf):
        pl.when(j % n_buf == p)(functools.partial(run, p))

    @pl.when(nv == 0)
    def _():
        y_ref[...] = jnp.zeros_like(y_ref)


def _experts(layer, tile_ea, tile_eb, tile_nv, src, xh, w_gate, w_up, w_down):
    tm = MOE_TILE
    up = lambda sel: pl.BlockSpec((None, None, D_MODEL, D_EXPERT),
                                  lambda j, ea, eb, nv, sr: (layer, (ea, eb)[sel][j], 0, 0))
    down = lambda sel: pl.BlockSpec((None, None, D_EXPERT, D_MODEL),
                                    lambda j, ea, eb, nv, sr: (layer, (ea, eb)[sel][j], 0, 0))
    return pl.pallas_call(
        _expert_kernel,
        grid_spec=pltpu.PrefetchScalarGridSpec(
            num_scalar_prefetch=4,
            grid=(MOE_TILES,),
            in_specs=[pl.BlockSpec(memory_space=pl.ANY), up(0), up(0), down(0), up(1), up(1), down(1)],
            out_specs=pl.BlockSpec((tm, HALF_D), lambda j, ea, eb, nv, sr: (j, 0)),
            scratch_shapes=[pltpu.VMEM((tm, XH_W), jnp.uint32)] * (MOE_AHEAD + 1)
            + [pltpu.SemaphoreType.DMA((MOE_AHEAD + 1,))],
        ),
        out_shape=jax.ShapeDtypeStruct((T_PAD, HALF_D), jnp.uint32),
        compiler_params=_cp(("arbitrary",), 56),
        name="moe_experts",
    )(tile_ea, tile_eb, tile_nv, src, xh, w_gate, w_up, w_down, w_gate, w_up, w_down)


def _combine_kernel(*refs, final):
    tm = DISPATCH_TILE
    if final:
        dest_ref, x_ref, gt_ref, ys_hbm, fg_ref, oc_ref, ol_ref, buf, sems = refs
    else:
        dest_ref, x_ref, gt_ref, ys_hbm, o_ref, buf, sems = refs
    i = pl.program_id(0)
    slot = i % 2

    last = pl.num_programs(0) - 1

    @pl.when(i == 0)
    def _():
        _gather_rows(dest_ref, 0, ys_hbm, buf.at[0], sems.at[0], tm)

    def finish():
        _wait_rows(ys_hbm, buf.at[slot], sems.at[slot], tm)
        x = x_ref[...] + gt_ref[...] * _unpack_bf16_pairs(buf[slot])
        if not final:
            o_ref[...] = x
            return
        y = x * lax.rsqrt(jnp.mean(x * x, axis=-1, keepdims=True) + NORM_EPS) * fg_ref[...]
        is_ctx = i < TP // tm

        @pl.when(is_ctx)
        def _():
            oc_ref[...] = y

        @pl.when(jnp.logical_not(is_ctx))
        def _():
            ol_ref[...] = y

    @pl.when(i < last)
    def _():
        _gather_rows(dest_ref, (i + 1) * tm, ys_hbm, buf.at[1 - slot], sems.at[1 - slot], tm, static=True)
        finish()

    pl.when(i == last)(finish)


def _combine(dest, x, mod, ys, final_gain=None):
    tm = DISPATCH_TILE
    final = final_gain is not None
    n_ctx = TP // tm
    in_specs = [pl.BlockSpec((tm, D_MODEL), lambda i, d: (i, 0)),
                pl.BlockSpec((None, None, 1, D_MODEL), lambda i, d: (_cond_row(i * tm), 5, 0, 0)),
                pl.BlockSpec(memory_space=pl.ANY)]
    args = [dest, x, mod, ys]
    if final:
        in_specs.append(pl.BlockSpec((1, D_MODEL), lambda i, d: (0, 0)))
        args.append(final_gain.reshape(1, D_MODEL))
        out_specs = [pl.BlockSpec((tm, D_MODEL), lambda i, d: (jnp.minimum(i, n_ctx - 1), 0)),
                     pl.BlockSpec((tm, D_MODEL), lambda i, d: (jnp.maximum(i - n_ctx, 0), 0))]
        out_shape = [jax.ShapeDtypeStruct((TP, D_MODEL), F32), jax.ShapeDtypeStruct((TS, D_MODEL), F32)]
    else:
        out_specs = pl.BlockSpec((tm, D_MODEL), lambda i, d: (i, 0))
        out_shape = jax.ShapeDtypeStruct((T, D_MODEL), F32)
    return pl.pallas_call(
        functools.partial(_combine_kernel, final=final),
        grid_spec=pltpu.PrefetchScalarGridSpec(
            num_scalar_prefetch=1,
            grid=(T // tm,),
            in_specs=in_specs,
            out_specs=out_specs,
            scratch_shapes=[pltpu.VMEM((2, tm, HALF_D), jnp.uint32), pltpu.SemaphoreType.DMA((2,))],
        ),
        out_shape=out_shape,
        compiler_params=_cp(("arbitrary",)),
        name="moe_combine",
    )(*args)


def _lookup(table, idx):
    n = table.shape[0]
    hit = idx[:, None] == jnp.arange(n, dtype=jnp.int32)[None, :]
    return jnp.sum(jnp.where(hit, table[None, :], 0), axis=1)


def _moe_plan(rt, cnt):
    bucket = rt[0].astype(jnp.int32)
    rank = rt[1].astype(jnp.int32)
    counts = cnt[:N_BUCKETS, 0].astype(jnp.int32)
    tiles = (counts + MOE_TILE - 1) // MOE_TILE
    order = jnp.arange(N_BUCKETS, dtype=jnp.int32)
    tile_start = jnp.sum(jnp.where(order[None, :] < order[:, None], tiles[None, :], 0), axis=1)
    tile_end = tile_start + tiles
    n_used = tile_end[N_BUCKETS - 1]
    dest = _lookup(tile_start * MOE_TILE, bucket) + rank
    j = jnp.arange(MOE_TILES, dtype=jnp.int32)
    jc = jnp.minimum(j, n_used - 1)
    b = jnp.minimum(jnp.sum((jc[:, None] >= tile_end[None, :]).astype(jnp.int32), axis=1), N_BUCKETS - 1)
    nv = jnp.clip(_lookup(counts, b) - (j - _lookup(tile_start, b)) * MOE_TILE, 0, MOE_TILE)
    nv = jnp.where(j < n_used, nv, 0)
    n_pairs = len(PAIR_LO)
    ea = (b // n_pairs) * EXPERTS_PER_GROUP + _lookup(jnp.asarray(PAIR_LO, jnp.int32), b % n_pairs)
    eb = (b // n_pairs) * EXPERTS_PER_GROUP + _lookup(jnp.asarray(PAIR_HI, jnp.int32), b % n_pairs)
    return dest, ea, eb, nv


def _moe(layer, x, g, mod, w_router, b_router, w_gate, w_up, w_down, final_gain=None):
    xh, rt, cnt = _router(x, g, mod, w_router, b_router)
    dest, ea, eb, nv = _moe_plan(rt, cnt)
    ys = _experts(layer, ea, eb, nv, _invert(dest), xh, w_gate, w_up, w_down)
    return _combine(dest, x, mod, ys, final_gain)


def kernel(x_prompt, x_sample, cache_k_full, cache_v_full, cache_k_win, cache_v_win, c, c_ctx, w_mod, b_mod, norm_mix, norm_ffn, final_norm, pool_w, pool_scale, hy_w_in, hy_b_in, hy_conv_w, hy_conv_b, hy_f_w1, hy_f_b1, hy_f_w2, hy_f_b2, hy_f_freq, hy_f_w3, hy_decay, hy_skip, hy_w_out, hy_b_out, fa_w_qkv, fa_q_norm, fa_k_norm, fa_w_o, wa_w_qkv, wa_sink, wa_w_o, w_router, b_router, moe_w_gate, moe_w_up, moe_w_down):
    x = None
    cond =jnp.concatenate([c_ctx[None, :], c, jnp.zeros((N_COND - 1 - DEC_BATCH, D_MODEL), F32)], axis=0)
    mods = _adaln(cond, w_mod, b_mod).reshape(DEPTH, N_COND, 6, 1, D_MODEL)
    rope = _rope_tables()
    ones_hd = jnp.ones((HEAD_DIM,), F32)
    wg_bf, wu_bf, wd_bf = moe_w_gate.astype(BF16), moe_w_up.astype(BF16), moe_w_down.astype(BF16)
    new_kv = {}
    for layer in range(DEPTH):
        kind = layer % 4
        j = layer // 4
        mod = mods[layer]
        g_mix = norm_mix[layer].reshape(1, D_MODEL)
        if kind == 0:
            assert layer == 0, "the pooling mixer reads the two input streams, so it must be the first layer"
            x = _pool_mixer(x_prompt.reshape(TP, D_MODEL), x_sample.reshape(TS, D_MODEL), g_mix, mod,
                            pool_w[j], pool_scale[j])
        elif kind == 1:
            x = _hyena_mixer(x, g_mix, mod, hy_w_in[j], hy_b_in[j], hy_conv_w[j], hy_conv_b[j], hy_f_w1[j],
                             hy_f_b1[j], hy_f_w2[j], hy_f_b2[j], hy_f_freq[j], hy_f_w3[j], hy_decay[j],
                             hy_skip[j], hy_w_out[j], hy_b_out[j])
        elif kind == 2:
            x, nk, nv = _attn_mixer(x, g_mix, mod, fa_w_qkv[j], fa_q_norm[j], fa_k_norm[j], True, None,
                                    fa_w_o[j], cache_k_full[:, j], cache_v_full[:, j], False, rope)
            new_kv.setdefault("kf", []).append(nk)
            new_kv.setdefault("vf", []).append(nv)
        else:
            x, nk, nv = _attn_mixer(x, g_mix, mod, wa_w_qkv[j], ones_hd, ones_hd, False, wa_sink[j],
                                    wa_w_o[j], cache_k_win[:, j], cache_v_win[:, j], True, rope)
            new_kv.setdefault("kw", []).append(nk)
            new_kv.setdefault("vw", []).append(nv)
        x = _moe(layer, x, norm_ffn[layer].reshape(1, D_MODEL), mod, w_router, b_router, wg_bf, wu_bf, wd_bf,
                 final_norm if layer == DEPTH - 1 else None)
    y_prompt, y_sample = x
    y_prompt = y_prompt.reshape(BATCH, SEQ, D_MODEL)
    y_sample = y_sample.reshape(DEC_BATCH, DEC_SEQ, D_MODEL)
    return (y_prompt, y_sample, jnp.stack(new_kv["kf"], axis=1), jnp.stack(new_kv["vf"], axis=1),
            jnp.stack(new_kv["kw"], axis=1), jnp.stack(new_kv["vw"], axis=1))
```

```python
import functools
import math

import jax
import jax.numpy as jnp
import numpy as np
from jax import lax
from jax.experimental import pallas as pl
from jax.experimental.pallas import tpu as pltpu

D_MODEL = 2048
BATCH = 32
SEQ = 256
DEPTH = 4
DEC_BATCH = 4
DEC_SEQ = 4096
PAST_LEN = 512
GRID_W = 64
N_HEADS = 16
N_KV_HEADS = 4
HEAD_DIM = D_MODEL // N_HEADS
KV_GROUP = N_HEADS // N_KV_HEADS
KV_DIM = N_KV_HEADS * HEAD_DIM
QKV_DIM = (N_HEADS + 2 * N_KV_HEADS) * HEAD_DIM
ROPE_THETA = 10000.0
WINDOW = 128
POOL_WINDOWS = (2, 4, 8, 16)
POOL_GROUP = D_MODEL // len(POOL_WINDOWS)
HYENA_EMB_BANDS = 16
HYENA_FILTER_HIDDEN = 64
N_EXPERTS = 16
N_EXPERT_GROUPS = 4
EXPERTS_PER_GROUP = 4
D_EXPERT = 512
NORM_EPS = 1e-6
NEG_INF = -1e30

F32 = jnp.float32
BF16 = jnp.bfloat16

TP = BATCH * SEQ
TS = DEC_BATCH * DEC_SEQ
T = TP + TS
N_COND = 8
LANE = 128
MIB = 1024 * 1024

PAIR_LO = (0, 0, 0, 1, 1, 2)
PAIR_HI = (1, 2, 3, 2, 3, 3)
N_BUCKETS = N_EXPERT_GROUPS * len(PAIR_LO)
MOE_TILE = 256
MOE_AHEAD = 4
MOE_TILES = T // MOE_TILE + N_BUCKETS + MOE_AHEAD
T_PAD = MOE_TILES * MOE_TILE
HALF_D = D_MODEL // 2
XH_W = HALF_D + LANE


def _cp(sem, vmem_mb=48):
    return pltpu.CompilerParams(dimension_semantics=sem, vmem_limit_bytes=vmem_mb * MIB)


def _dot(a, b):
    return jnp.dot(a, b, preferred_element_type=F32)


def _dot3(a, b):
    ah = a.astype(BF16)
    al = (a - ah.astype(F32)).astype(BF16)
    bh = b.astype(BF16)
    bl = (b - bh.astype(F32)).astype(BF16)
    return _dot(ah, bh) + (_dot(al, bh) + _dot(ah, bl))


def _sigmoid(x):
    return 1.0 / (1.0 + jnp.exp(-x))


def _pack_bf16_pairs(x):
    n = x.shape[1] // 2
    bits = lambda v: lax.bitcast_convert_type(v.astype(BF16).astype(F32), jnp.uint32)
    return (bits(x[:, :n]) >> 16) | bits(x[:, n:])


def _unpack_bf16_pairs(u):
    lo = lax.bitcast_convert_type(u << 16, F32)
    hi = lax.bitcast_convert_type(u & jnp.uint32(0xFFFF0000), F32)
    return jnp.concatenate([lo, hi], axis=1)


def _cond_row(r):
    return jnp.where(r < TP, 0, 1 + (r - TP) // DEC_SEQ)


def _mod_spec(tm, chunk, tn=D_MODEL, ncol=False):
    if ncol:
        return pl.BlockSpec((None, None, 1, tn), lambda i, j: (_cond_row(i * tm), chunk, 0, j))
    return pl.BlockSpec((None, None, 1, tn), lambda i, *_: (_cond_row(i * tm), chunk, 0, 0))


def _norm_mod(x, g, shift, scale):
    var = jnp.mean(x * x, axis=-1, keepdims=True)
    y = x * lax.rsqrt(var + NORM_EPS) * g
    return y * (1.0 + scale) + shift


def _adaln_kernel(c_ref, w_ref, b_ref, o_ref):
    c = c_ref[...]
    a = c * _sigmoid(c)
    o_ref[...] = _dot3(a, w_ref[...]) + b_ref[...]


def _adaln(cond, w_mod, b_mod):
    tn = 1024
    n = 6 * D_MODEL
    return pl.pallas_call(
        _adaln_kernel,
        grid=(DEPTH, n // tn),
        in_specs=[
            pl.BlockSpec((N_COND, D_MODEL), lambda l, j: (0, 0)),
            pl.BlockSpec((None, D_MODEL, tn), lambda l, j: (l, 0, j)),
            pl.BlockSpec((None, 1, tn), lambda l, j: (l, 0, j)),
        ],
        out_specs=pl.BlockSpec((None, N_COND, tn), lambda l, j: (l, 0, j)),
        out_shape=jax.ShapeDtypeStruct((DEPTH, N_COND, n), F32),
        compiler_params=_cp(("parallel", "parallel")),
        name="adaln",
    )(cond, w_mod, b_mod.reshape(DEPTH, 1, n))


def _nm_matmul_kernel(x_ref, g_ref, sh_ref, sc_ref, w_ref, b_ref, o_ref, h_scr):
    @pl.when(pl.program_id(1) == 0)
    def _():
        h_scr[...] = _norm_mod(x_ref[...], g_ref[...], sh_ref[...], sc_ref[...]).astype(BF16)

    o_ref[...] = (_dot(h_scr[...], w_ref[...]) + b_ref[...]).astype(o_ref.dtype)


def _nm_matmul(x, g, mod, w, b, out_dtype, name):
    tm, tn = 1024, 1024
    n = w.shape[1]
    return pl.pallas_call(
        _nm_matmul_kernel,
        grid=(T // tm, n // tn),
        in_specs=[
            pl.BlockSpec((tm, D_MODEL), lambda i, j: (i, 0)),
            pl.BlockSpec((1, D_MODEL), lambda i, j: (0, 0)),
            _mod_spec(tm, 0),
            _mod_spec(tm, 1),
            pl.BlockSpec((D_MODEL, tn), lambda i, j: (0, j)),
            pl.BlockSpec((1, tn), lambda i, j: (0, j)),
        ],
        out_specs=pl.BlockSpec((tm, tn), lambda i, j: (i, j)),
        out_shape=jax.ShapeDtypeStruct((T, n), out_dtype),
        scratch_shapes=[pltpu.VMEM((tm, D_MODEL), BF16)],
        compiler_params=_cp(("parallel", "arbitrary")),
        name=name,
    )(x, g, mod, mod, w, b)


RESID_TM = 1024


def _resid_matmul_kernel(ap_ref, as_ref, w_ref, b_ref, x_ref, gt_ref, o_ref):
    def emit(a_ref):
        o_ref[...] = x_ref[...] + gt_ref[...] * (_dot(a_ref[...], w_ref[...]) + b_ref[...])

    is_ctx = pl.program_id(0) < TP // RESID_TM
    pl.when(is_ctx)(lambda: emit(ap_ref))
    pl.when(jnp.logical_not(is_ctx))(lambda: emit(as_ref))


def _resid_matmul(a_ctx, a_lat, w, b, x, mod, name):
    tm, tn = RESID_TM, 1024
    k = a_ctx.shape[1]
    n_ctx = TP // tm
    return pl.pallas_call(
        _resid_matmul_kernel,
        grid=(T // tm, D_MODEL // tn),
        in_specs=[
            pl.BlockSpec((tm, k), lambda i, j: (jnp.minimum(i, n_ctx - 1), 0)),
            pl.BlockSpec((tm, k), lambda i, j: (jnp.maximum(i - n_ctx, 0), 0)),
            pl.BlockSpec((k, tn), lambda i, j: (0, j)),
            pl.BlockSpec((1, tn), lambda i, j: (0, j)),
            pl.BlockSpec((tm, tn), lambda i, j: (i, j)),
            _mod_spec(tm, 2, tn, ncol=True),
        ],
        out_specs=pl.BlockSpec((tm, tn), lambda i, j: (i, j)),
        out_shape=jax.ShapeDtypeStruct((T, D_MODEL), F32),
        compiler_params=_cp(("parallel", "parallel")),
        name=name,
    )(a_ctx, a_lat, w, b, x, mod)


POOL_TILE = 256
POOL_HALO = 8


def _seq_pos(r0):
    is_ctx = r0 < TP
    loc0 = jnp.where(is_ctx, r0 % SEQ, (r0 - TP) % DEC_SEQ)
    seq_len = jnp.where(is_ctx, SEQ, DEC_SEQ)
    return loc0, seq_len


def _pool_kernel(xc_ref, xcp_ref, xcn_ref, xl_ref, xlp_ref, xln_ref, *rest):
    is_ctx = pl.program_id(0) < TP // POOL_TILE
    pl.when(is_ctx)(lambda: _pool_tile(xc_ref, xcp_ref, xcn_ref, *rest))
    pl.when(jnp.logical_not(is_ctx))(lambda: _pool_tile(xl_ref, xlp_ref, xln_ref, *rest))


def _pool_tile(x_ref, xp_ref, xn_ref, g_ref, sh_ref, sc_ref, gt_ref, pw_ref, ps_ref, o_ref, hz_scr):
    tm, hl = POOL_TILE, POOL_HALO
    loc0, seq_len = _seq_pos(pl.program_id(0) * tm)
    has_prev = loc0 > 0
    has_next = loc0 + tm < seq_len
    g, sh, sc = g_ref[...], sh_ref[...], sc_ref[...]
    x = x_ref[...]
    h = _norm_mod(x, g, sh, sc)
    hz_scr[0:hl, :] = jnp.where(has_prev, _norm_mod(xp_ref[...], g, sh, sc), 0.0)
    hz_scr[hl:hl + tm, :] = h
    hz_scr[hl + tm:, :] = jnp.where(has_next, _norm_mod(xn_ref[...], g, sh, sc), 0.0)
    tl = loc0 + lax.broadcasted_iota(jnp.int32, (tm, 1), 0)
    outs = []
    for gi, w in enumerate(POOL_WINDOWS):
        cs = slice(gi * POOL_GROUP, (gi + 1) * POOL_GROUP)
        s = jnp.zeros((tm, POOL_GROUP), F32)
        for off in range(-(w // 2), w - w // 2):
            s = s + hz_scr[hl + off:hl + off + tm, cs]
        lo = jnp.maximum(tl - w // 2, 0)
        hi = jnp.minimum(tl + (w - w // 2), seq_len)
        d = s / (hi - lo).astype(F32) - h[:, cs]
        outs.append(_dot(d.astype(BF16), pw_ref[gi]))
    out = jnp.concatenate(outs, axis=1) * ps_ref[...]
    o_ref[...] = x + gt_ref[...] * out


def _pool_mixer(x_ctx, x_lat, g, mod, pool_w, pool_scale):
    tm, hl = POOL_TILE, POOL_HALO
    r = tm // hl

    def stream(first_tile, rows):
        tile = lambda i: jnp.clip(i - first_tile, 0, rows // tm - 1)
        return [pl.BlockSpec((tm, D_MODEL), lambda i: (tile(i), 0)),
                pl.BlockSpec((hl, D_MODEL), lambda i: (jnp.maximum(tile(i) * r - 1, 0), 0)),
                pl.BlockSpec((hl, D_MODEL), lambda i: (jnp.minimum((tile(i) + 1) * r, rows // hl - 1), 0))]

    return pl.pallas_call(
        _pool_kernel,
        grid=(T // tm,),
        in_specs=stream(0, TP) + stream(TP // tm, TS) + [
            pl.BlockSpec((1, D_MODEL), lambda i: (0, 0)),
            _mod_spec(tm, 0),
            _mod_spec(tm, 1),
            _mod_spec(tm, 2),
            pl.BlockSpec((len(POOL_WINDOWS), POOL_GROUP, POOL_GROUP), lambda i: (0, 0, 0)),
            pl.BlockSpec((1, D_MODEL), lambda i: (0, 0)),
        ],
        out_specs=pl.BlockSpec((tm, D_MODEL), lambda i: (i, 0)),
        out_shape=jax.ShapeDtypeStruct((T, D_MODEL), F32),
        scratch_shapes=[pltpu.VMEM((tm + 2 * hl, D_MODEL), F32)],
        compiler_params=_cp(("parallel",)),
        name="pool_mixer",
    )(x_ctx, x_ctx, x_ctx, x_lat, x_lat, x_lat, g, mod, mod, mod, pool_w.astype(BF16),
      pool_scale.reshape(1, D_MODEL))


CONV_TILE = 256
CONV_HALO = 16


def _conv3_kernel(u_ref, up_ref, un_ref, cw_ref, cb_ref, o_ref, scr):
    tm, hl = CONV_TILE, CONV_HALO
    loc0, seq_len = _seq_pos(pl.program_id(0) * tm)
    has_prev = loc0 > 0
    has_next = loc0 + tm < seq_len
    scr[0:hl, :] = jnp.where(has_prev, up_ref[...].astype(F32), 0.0)
    scr[hl:hl + tm, :] = u_ref[...].astype(F32)
    scr[hl + tm:, :] = jnp.where(has_next, un_ref[...].astype(F32), 0.0)
    out = (scr[hl - 1:hl - 1 + tm, :] * cw_ref[0:1, :] + scr[hl:hl + tm, :] * cw_ref[1:2, :]
           + scr[hl + 1:hl + 1 + tm, :] * cw_ref[2:3, :] + cb_ref[...])
    o_ref[...] = out.astype(o_ref.dtype)


def _conv3(u0, conv_w, conv_b):
    tm, hl, tc = CONV_TILE, CONV_HALO, D_MODEL
    r = tm // hl
    n = u0.shape[1]
    return pl.pallas_call(
        _conv3_kernel,
        grid=(T // tm, n // tc),
        in_specs=[
            pl.BlockSpec((tm, tc), lambda i, j: (i, j)),
            pl.BlockSpec((hl, tc), lambda i, j: (jnp.maximum(i * r - 1, 0), j)),
            pl.BlockSpec((hl, tc), lambda i, j: (jnp.minimum((i + 1) * r, T // hl - 1), j)),
            pl.BlockSpec((3, tc), lambda i, j: (0, j)),
            pl.BlockSpec((1, tc), lambda i, j: (0, j)),
        ],
        out_specs=pl.BlockSpec((tm, tc), lambda i, j: (i, j)),
        out_shape=jax.ShapeDtypeStruct((T, n), BF16),
        scratch_shapes=[pltpu.VMEM((tm + 2 * hl, tc), F32)],
        compiler_params=_cp(("parallel", "parallel")),
        name="hyena_conv3",
    )(u0, u0, u0, conv_w, conv_b.reshape(1, n))


FILT_TILE = 256


HYENA_BLOCK = 1024


T_LANE = LANE - 1


def _filter_mlp_kernel(emb_ref, w1_ref, b1_ref, w2_ref, b2_ref, fr_ref, o_ref):
    emb = emb_ref[...]
    fr = fr_ref[...]
    a = jnp.sin(fr * (_dot3(emb, w1_ref[...]) + b1_ref[...]))
    a = jnp.sin(fr * (_dot3(a, w2_ref[...]) + b2_ref[...]))
    lane = lax.broadcasted_iota(jnp.int32, a.shape, 1)
    o_ref[...] = jnp.where(lane == T_LANE, emb[:, 0:1], a)


def _filter_mlp(pos, L, f_w1, f_b1, f_w2, f_b2, f_freq):
    assert HYENA_FILTER_HIDDEN <= T_LANE
    tl = FILT_TILE
    rows = pos.shape[0]
    small = lambda: pl.BlockSpec((LANE, LANE), lambda i: (0, 0))
    vec = lambda: pl.BlockSpec((1, LANE), lambda i: (0, 0))
    return pl.pallas_call(
        _filter_mlp_kernel,
        grid=(rows // tl,),
        in_specs=[pl.BlockSpec((tl, LANE), lambda i: (i, 0)), small(), vec(), small(), vec(), vec()],
        out_specs=pl.BlockSpec((tl, LANE), lambda i: (i, 0)),
        out_shape=jax.ShapeDtypeStruct((rows, LANE), F32),
        compiler_params=_cp(("parallel",)),
        name="hyena_filter_mlp",
    )(_filter_embedding(pos, L), _pad2(f_w1, LANE, LANE), _pad2(f_b1[None], 1, LANE),
      _pad2(f_w2, LANE, LANE), _pad2(f_b2[None], 1, LANE), _pad2(f_freq[None], 1, LANE))


def _filter_kernel(h1_ref, h2_ref, w3a_ref, dca_ref, w3b_ref, dcb_ref, fa_ref, fb_ref, *, blk):
    def taps(h, w3_ref, dc_ref):
        return _dot3(h, w3_ref[...]) * jnp.exp(-h[:, T_LANE:] * jnp.abs(dc_ref[...]))

    pos = taps(h1_ref[...], w3a_ref, dca_ref)
    neg = taps(h2_ref[...], w3b_ref, dcb_ref)
    m = (pl.program_id(0) * FILT_TILE + lax.broadcasted_iota(jnp.int32, (FILT_TILE, 1), 0)) % blk
    fa_ref[...] = jnp.where(m == 0, pos, pos + neg).astype(BF16)
    fb_ref[...] = jnp.where(m == 0, 0.0, neg - pos).astype(BF16)


def _pad2(a, rows, cols):
    return jnp.pad(a, ((0, rows - a.shape[0]), (0, cols - a.shape[1])))


def _filter_positions(L, blk):
    n_blk = L // blk
    m = np.arange(blk)
    p1, p2 = [], []
    for d in range(-(n_blk - 1), n_blk):
        if d >= 1:
            p1.append(d * blk + m), p2.append(d * blk - m)
        elif d == 0:
            p1.append(m), p2.append(m)
        else:
            p1.append(-d * blk - m), p2.append(-d * blk + m)
    return np.concatenate(p1), np.concatenate(p2)


def _filter_embedding(pos, L):
    t = jnp.asarray(pos, F32) / L
    bands = jnp.linspace(1e-4, HYENA_EMB_BANDS - 1, HYENA_EMB_BANDS, dtype=F32)
    ang = (2 * math.pi) * t[:, None] * bands[None, :]
    return _pad2(jnp.concatenate([t[:, None], jnp.cos(ang), -jnp.sin(ang)], axis=-1), pos.shape[0], LANE)


def _hyena_filters(L, blk, f_w1, f_b1, f_w2, f_b2, f_freq, f_w3, decay):
    n_blk = L // blk
    p1, p2 = _filter_positions(L, blk)
    rows = p1.shape[0]
    tl = FILT_TILE
    tiles_per_lag = blk // tl
    lag = lambda i: i // tiles_per_lag - (n_blk - 1)
    col1 = lambda i, n: 2 * n + jnp.where(lag(i) >= 0, 0, 1)
    col2 = lambda i, n: 2 * n + jnp.where(lag(i) >= 1, 0, 1)
    n_tiles = rows // tl
    out = pl.BlockSpec((None, tl, D_MODEL), lambda i, n: (n, i, 0))
    w3 = _pad2(f_w3, LANE, f_w3.shape[1])
    hidden = _filter_mlp(np.concatenate([p1, p2]), L, f_w1, f_b1, f_w2, f_b2, f_freq)
    return pl.pallas_call(
        functools.partial(_filter_kernel, blk=blk),
        grid=(n_tiles, 2),
        in_specs=[
            pl.BlockSpec((tl, LANE), lambda i, n: (i, 0)),
            pl.BlockSpec((tl, LANE), lambda i, n: (n_tiles + i, 0)),
            pl.BlockSpec((LANE, D_MODEL), lambda i, n: (0, col1(i, n))),
            pl.BlockSpec((1, D_MODEL), lambda i, n: (0, col1(i, n))),
            pl.BlockSpec((LANE, D_MODEL), lambda i, n: (0, col2(i, n))),
            pl.BlockSpec((1, D_MODEL), lambda i, n: (0, col2(i, n))),
        ],
        out_specs=[out, out],
        out_shape=[jax.ShapeDtypeStruct((2, rows, D_MODEL), BF16)] * 2,
        compiler_params=_cp(("parallel", "parallel")),
        name="hyena_filters",
    )(hidden, hidden, w3, decay[None], w3, decay[None])


def _dft_mats(L):
    r = int(math.isqrt(L))
    k2 = 2 * jnp.arange(L, dtype=jnp.int32)[:, None] + 1
    n1 = r * jnp.arange(L // r, dtype=jnp.int32)[None, :]
    n2 = jnp.arange(r, dtype=jnp.int32)[None, :]
    sc = math.pi / (2 * L)
    aa = ((k2 * n1) % (4 * L)).astype(F32) * sc
    ab = ((k2 * n2) % (4 * L)).astype(F32) * sc
    ca, sa, cb, sb = jnp.cos(aa)[:, :, None], jnp.sin(aa)[:, :, None], jnp.cos(ab)[:, None, :], jnp.sin(ab)[:, None, :]
    c = (ca * cb - sa * sb).reshape(L, L)
    s = (sa * cb + ca * sb).reshape(L, L)
    return c.astype(BF16), s.astype(BF16), c.T.astype(BF16), s.T.astype(BF16)


def _dft_tiles(L):
    return min(512, L), 512


def _dft_filter_kernel(c_ref, s_ref, a_ref, b_ref, gr_ref, gi_ref):
    gr_ref[...] = _dot(c_ref[...], a_ref[...]).astype(gr_ref.dtype)
    gi_ref[...] = _dot(s_ref[...], b_ref[...]).astype(gi_ref.dtype)


def _dft_filter(cm, sm, fa, fb, L):
    tf, tn = _dft_tiles(L)
    n = fa.shape[0]
    mat = lambda: pl.BlockSpec((tf, L), lambda k, c, s: (k, 0))
    rhs = lambda: pl.BlockSpec((None, L, tn), lambda k, c, s: (s, 0, c))
    out = pl.BlockSpec((None, tf, tn), lambda k, c, s: (s, k, c))
    return pl.pallas_call(
        _dft_filter_kernel,
        grid=(L // tf, D_MODEL // tn, n),
        in_specs=[mat(), mat(), rhs(), rhs()],
        out_specs=[out, out],
        out_shape=[jax.ShapeDtypeStruct((n, L, D_MODEL), BF16)] * 2,
        compiler_params=_cp(("parallel", "parallel", "parallel")),
        name="hyena_filter_dft",
    )(cm, sm, fa, fb)


FWD_TF = 256


def _dft_fwd_kernel(c_ref, s_ref, z_ref, gr_ref, gi_ref, yr_ref, yi_ref, *, n_blk, blk, bpb):
    c, s = c_ref[...], s_ref[...]
    for bb in range(bpb):
        zc, zs = [], []
        for j in range(n_blk):
            r = (bb * n_blk + j) * blk
            zj = z_ref[r:r + blk, :]
            zc.append(_dot(c, zj).astype(BF16))
            zs.append(_dot(s, zj).astype(BF16))
        for i in range(n_blk):
            yr = yi = None
            for j in range(n_blk):
                lag = i - j + n_blk - 1
                gr, gi = gr_ref[lag], gi_ref[lag]
                tr = gr * zc[j] + gi * zs[j]
                ti = gi * zc[j] - gr * zs[j]
                yr = tr if yr is None else yr + tr
                yi = ti if yi is None else yi + ti
            yr_ref[bb, i] = yr.astype(BF16)
            yi_ref[bb, i] = yi.astype(BF16)


def _seqs_per_step(L):
    return max(1, 2048 // L)


def _dft_fwd(cm, sm, z, z_rowblk, z_colblk, gr, gi, order, nb, L, blk):
    n_blk = L // blk
    bpb = _seqs_per_step(L)
    assert nb % bpb == 0 and z_rowblk % bpb == 0
    tf, tn = min(FWD_TF, blk), 512
    mat = lambda: pl.BlockSpec((tf, blk), lambda k, c, b: (k, 0))
    gsp = lambda: pl.BlockSpec((None, 2 * n_blk - 1, tf, tn), lambda k, c, b: (order, 0, k, c))
    out = pl.BlockSpec((bpb, n_blk, tf, tn), lambda k, c, b: (b, 0, k, c))
    return pl.pallas_call(
        functools.partial(_dft_fwd_kernel, n_blk=n_blk, blk=blk, bpb=bpb),
        grid=(blk // tf, D_MODEL // tn, nb // bpb),
        in_specs=[mat(), mat(),
                  pl.BlockSpec((bpb * L, tn), lambda k, c, b: (z_rowblk // bpb + b, z_colblk + c)),
                  gsp(), gsp()],
        out_specs=[out, out],
        out_shape=[jax.ShapeDtypeStruct((nb, n_blk, blk, D_MODEL), BF16)] * 2,
        compiler_params=_cp(("parallel", "parallel", "parallel")),
        name="hyena_dft_fwd",
    )(cm, sm, z, gr, gi)


def _dft_inv_kernel(ct_ref, st_ref, yr_ref, yi_ref, z_ref, gt_ref, sk_ref, o_ref, *, inv_len, bpb, tt):
    for bb in range(bpb):
        rows = slice(bb * tt, (bb + 1) * tt)
        y = (_dot(ct_ref[...], yr_ref[bb]) - _dot(st_ref[...], yi_ref[bb])) * inv_len
        o_ref[rows, :] = (gt_ref[rows, :].astype(F32) * (y + sk_ref[...] * z_ref[rows, :].astype(F32))).astype(BF16)


def _dft_inv(ctm, stm, yr, yi, z, z_rowblk, z_colblk, gate, g_rowblk, g_colblk, skip, nb, L):
    tt, tn = _dft_tiles(L)
    rpb = L // tt
    bpb = _seqs_per_step(L) if rpb == 1 else 1
    assert nb % bpb == 0 and z_rowblk % bpb == 0 and g_rowblk % bpb == 0
    mat = lambda: pl.BlockSpec((tt, L), lambda t, c, b: (t, 0))
    spec = lambda: pl.BlockSpec((bpb, L, tn), lambda t, c, b: (b, 0, c))
    rows = lambda blk0: (lambda t, c, b: ((blk0 + b * bpb * rpb + t) // bpb))
    return pl.pallas_call(
        functools.partial(_dft_inv_kernel, inv_len=1.0 / L, bpb=bpb, tt=tt),
        grid=(rpb, D_MODEL // tn, nb // bpb),
        in_specs=[mat(), mat(), spec(), spec(),
                  pl.BlockSpec((bpb * tt, tn), lambda t, c, b: (rows(z_rowblk)(t, c, b), z_colblk + c)),
                  pl.BlockSpec((bpb * tt, tn), lambda t, c, b: (rows(g_rowblk)(t, c, b), g_colblk + c)),
                  pl.BlockSpec((1, tn), lambda t, c, b: (0, c))],
        out_specs=pl.BlockSpec((bpb * tt, tn), lambda t, c, b: (rows(0)(t, c, b), c)),
        out_shape=jax.ShapeDtypeStruct((nb * L, D_MODEL), BF16),
        compiler_params=_cp(("parallel", "parallel", "parallel")),
        name="hyena_dft_inv",
    )(ctm, stm, yr, yi, z, gate, skip)


def _hyena_stream(u, row0, nb, L, fparams, skip):
    blk = min(HYENA_BLOCK, L)
    n_blk = L // blk
    n_lag = 2 * n_blk - 1
    cm, sm, ctm, stm = _dft_mats(blk)
    fa, fb = _hyena_filters(L, blk, *fparams)
    seg = lambda a: a.reshape(2 * n_lag, blk, D_MODEL)
    gr, gi = _dft_filter(cm, sm, seg(fa), seg(fb), blk)
    gr, gi = (a.reshape(2, n_lag, blk, D_MODEL) for a in (gr, gi))
    tt, tn = _dft_tiles(blk)
    ncb = D_MODEL // tn
    blocks = lambda a: a.reshape(nb * n_blk, blk, D_MODEL)

    yr, yi = _dft_fwd(cm, sm, u, row0 // L, 0, gr, gi, 0, nb, L, blk)
    z1 = _dft_inv(ctm, stm, blocks(yr), blocks(yi), u, row0 // tt, 0, u, row0 // tt, ncb, skip[0:1],
                  nb * n_blk, blk)
    yr, yi = _dft_fwd(cm, sm, z1, 0, 0, gr, gi, 1, nb, L, blk)
    return _dft_inv(ctm, stm, blocks(yr), blocks(yi), z1, 0, 0, u, row0 // tt, 2 * ncb, skip[1:2],
                    nb * n_blk, blk)


def _hyena_mixer(x, g, mod, w_in, b_in, conv_w, conv_b, f_w1, f_b1, f_w2, f_b2, f_freq, f_w3, decay, skip,
                 w_out, b_out):
    u0 = _nm_matmul(x, g, mod, w_in.astype(BF16), b_in.reshape(1, -1), BF16, "hyena_in_proj")
    u = _conv3(u0, conv_w, conv_b)
    fparams = (f_w1, f_b1, f_w2, f_b2, f_freq, f_w3, decay)
    zp = _hyena_stream(u, 0, BATCH, SEQ, fparams, skip)
    zs = _hyena_stream(u, TP, DEC_BATCH, DEC_SEQ, fparams, skip)
    return _resid_matmul(zp, zs, w_out.astype(BF16), b_out.reshape(1, -1), x, mod, "hyena_out_proj")


def _rope_tables():
    pos = jnp.arange(DEC_SEQ, dtype=jnp.int32)
    row = (pos // GRID_W).astype(F32)
    col = (pos % GRID_W).astype(F32)
    axis_dim = HEAD_DIM // 2
    inv_freq = ROPE_THETA ** (-jnp.arange(0, axis_dim, 2, dtype=F32) / axis_dim)
    ar = row[:, None] * inv_freq[None, :]
    ac = col[:, None] * inv_freq[None, :]
    cos = jnp.concatenate([jnp.cos(ar), jnp.cos(ar), jnp.cos(ac), jnp.cos(ac)], axis=-1)
    sin = jnp.concatenate([-jnp.sin(ar), jnp.sin(ar), -jnp.sin(ac), jnp.sin(ac)], axis=-1)
    return cos, sin


QKV_TM = 512
QKV_TN = 1024
PAIR = 2 * HEAD_DIM


def _qkv_kernel(x_ref, g_ref, sh_ref, sc_ref, w_ref, qn_ref, kn_ref, cos_ref, sin_ref,
                q_ref, k_ref, v_ref, nk_ref, nv_ref, h_scr, *, use_norm):
    tm = QKV_TM
    i, j = pl.program_id(0), pl.program_id(1)

    @pl.when(j == 0)
    def _():
        h_scr[...] = _norm_mod(x_ref[...], g_ref[...], sh_ref[...], sc_ref[...]).astype(BF16)

    quarter = HEAD_DIM // 4
    scale = HEAD_DIM ** -0.5 * LOG2E

    def head(xh, gn):
        if use_norm:
            xh = xh * lax.rsqrt(jnp.mean(xh * xh, axis=-1, keepdims=True) + NORM_EPS) * gn
        return xh

    def rope(xh):
        lane = lax.broadcasted_iota(jnp.int32, (tm, HEAD_DIM), 1)
        first = (lane % (2 * quarter)) < quarter
        partner = jnp.where(first, pltpu.roll(xh, HEAD_DIM - quarter, 1), pltpu.roll(xh, quarter, 1))
        return xh * cos_ref[...] + partner * sin_ref[...]

    def proj(c0):
        return _dot(h_scr[...], w_ref[:, c0:c0 + PAIR])

    def q_tile(latent):
        for p in range(QKV_TN // PAIR):
            acc = proj(p * PAIR)
            for t in range(2):
                xh = head(acc[:, t * HEAD_DIM:(t + 1) * HEAD_DIM], qn_ref[...])
                xh = rope(xh) if latent else xh
                c0 = p * PAIR + t * HEAD_DIM
                q_ref[:, c0:c0 + HEAD_DIM] = (xh * scale).astype(BF16)

    def kv_tile(latent):
        for p in range(KV_DIM // PAIR):
            acc = proj(p * PAIR)
            for t in range(2):
                c0 = p * PAIR + t * HEAD_DIM
                kh = head(acc[:, t * HEAD_DIM:(t + 1) * HEAD_DIM], kn_ref[...])
                if not latent:
                    nk_ref[:, c0:c0 + HEAD_DIM] = kh
                k_ref[:, c0:c0 + HEAD_DIM] = (rope(kh) if latent else kh).astype(BF16)
        for p in range(KV_DIM // PAIR):
            acc = proj(KV_DIM + p * PAIR)
            if not latent:
                nv_ref[:, p * PAIR:(p + 1) * PAIR] = acc
            v_ref[:, p * PAIR:(p + 1) * PAIR] = acc.astype(BF16)

    is_ctx = i < TP // tm
    is_q = j < D_MODEL // QKV_TN
    for latent in (False, True):
        stream = jnp.logical_not(is_ctx) if latent else is_ctx
        pl.when(jnp.logical_and(stream, is_q))(functools.partial(q_tile, latent))
        pl.when(jnp.logical_and(stream, jnp.logical_not(is_q)))(functools.partial(kv_tile, latent))


def _qkv_proj(x, g, mod, w_qkv, q_norm, k_norm, use_norm, rope):
    tm, tn = QKV_TM, QKV_TN
    n_ctx = TP // tm
    n_q = D_MODEL // tn
    tab = lambda: pl.BlockSpec((tm, HEAD_DIM), lambda i, j: (jnp.maximum(i - n_ctx, 0) % (DEC_SEQ // tm), 0))
    kv = lambda: pl.BlockSpec((tm, KV_DIM), lambda i, j: (i, 0))
    new = lambda: pl.BlockSpec((tm, KV_DIM), lambda i, j: (jnp.minimum(i, n_ctx - 1), 0))
    return pl.pallas_call(
        functools.partial(_qkv_kernel, use_norm=use_norm),
        grid=(T // tm, QKV_DIM // tn),
        in_specs=[
            pl.BlockSpec((tm, D_MODEL), lambda i, j: (i, 0)),
            pl.BlockSpec((1, D_MODEL), lambda i, j: (0, 0)),
            _mod_spec(tm, 0),
            _mod_spec(tm, 1),
            pl.BlockSpec((D_MODEL, tn), lambda i, j: (0, j)),
            pl.BlockSpec((1, HEAD_DIM), lambda i, j: (0, 0)),
            pl.BlockSpec((1, HEAD_DIM), lambda i, j: (0, 0)),
            tab(), tab(),
        ],
        out_specs=[pl.BlockSpec((tm, tn), lambda i, j: (i, jnp.minimum(j, n_q - 1))), kv(), kv(), new(), new()],
        out_shape=[jax.ShapeDtypeStruct((T, D_MODEL), BF16), jax.ShapeDtypeStruct((T, KV_DIM), BF16),
                   jax.ShapeDtypeStruct((T, KV_DIM), BF16), jax.ShapeDtypeStruct((TP, KV_DIM), F32),
                   jax.ShapeDtypeStruct((TP, KV_DIM), F32)],
        scratch_shapes=[pltpu.VMEM((tm, D_MODEL), BF16)],
        compiler_params=_cp(("arbitrary", "arbitrary")),
        name="qkv_proj",
    )(x, g, mod, mod, w_qkv, q_norm.reshape(1, HEAD_DIM), k_norm.reshape(1, HEAD_DIM), *rope)


LOG2E = math.log2(math.e)
ATTN_TQ = 256


def _attn_kernel(*refs, tq, seq_len, has_ctx, windowed, has_sink):
    it = iter(refs)
    q_ref, k_ref, v_ref = next(it), next(it), next(it)
    kc_ref, vc_ref = (next(it), next(it)) if has_ctx else (None, None)
    sink_ref = next(it) if has_sink else None
    o_ref = next(it)
    if windowed:
        i = pl.program_id(2)
        span = tq + 2 * WINDOW
        start = pl.multiple_of(jnp.clip(i * tq - WINDOW, 0, seq_len - span), WINDOW)
        qpos = i * tq + lax.broadcasted_iota(jnp.int32, (tq, 1), 0)
        kpos = start + lax.broadcasted_iota(jnp.int32, (1, span), 1)
        segs = [(k_ref, v_ref, pl.ds(start, span), jnp.abs(kpos - qpos) <= WINDOW)]
    else:
        segs = [(k_ref, v_ref, slice(None), None)]
    if has_ctx:
        segs.append((kc_ref, vc_ref, slice(None), None))
    for h in range(KV_GROUP):
        hs = slice(h * HEAD_DIM, (h + 1) * HEAD_DIM)
        qh = q_ref[:, hs]
        scores = []
        m = None
        for kr, _, rows, mask in segs:
            s = lax.dot_general(qh, kr[rows, :], (((1,), (1,)), ((), ())), preferred_element_type=F32)
            if mask is not None:
                s = jnp.where(mask, s, NEG_INF)
            scores.append(s)
            ms = jnp.max(s, axis=-1, keepdims=True)
            m = ms if m is None else jnp.maximum(m, ms)
        if has_sink:
            sk = sink_ref[pl.program_id(1) * KV_GROUP + h]
            m = jnp.maximum(m, sk)
        l = jnp.exp2(sk - m) if has_sink else jnp.zeros_like(m)
        acc = jnp.zeros((tq, HEAD_DIM), F32)
        for (_, vr, rows, _), s in zip(segs, scores):
            p = jnp.exp2(s - m)
            l = l + jnp.sum(p, axis=-1, keepdims=True)
            acc = acc + _dot(p.astype(BF16), vr[rows, :])
        o_ref[:, hs] = (acc / l).astype(BF16)


def _attention(q, row0, k, v, k_row0, n_keys, k_ctx, v_ctx, sink, nb, L, windowed):
    tq = min(ATTN_TQ, L)
    nq = L // tq
    q_blk0 = row0 // tq
    seq0 = k_row0 // n_keys
    own = lambda: pl.BlockSpec((n_keys, HEAD_DIM), lambda b, g, i: (seq0 + b, g))
    in_specs = [pl.BlockSpec((tq, KV_GROUP * HEAD_DIM), lambda b, g, i: (q_blk0 + b * nq + i, g)), own(), own()]
    args = [q, k, v]
    if k_ctx is not None:
        ctx = lambda: pl.BlockSpec((None, PAST_LEN, HEAD_DIM), lambda b, g, i: (b, 0, g))
        in_specs += [ctx(), ctx()]
        args += [k_ctx, v_ctx]
    if sink is not None:
        in_specs.append(pl.BlockSpec(memory_space=pltpu.SMEM))
        args.append(sink.astype(F32) * LOG2E)
    return pl.pallas_call(
        functools.partial(_attn_kernel, tq=tq, seq_len=L, has_ctx=k_ctx is not None, windowed=windowed,
                          has_sink=sink is not None),
        grid=(nb, N_KV_HEADS, nq),
        in_specs=in_specs,
        out_specs=pl.BlockSpec((tq, KV_GROUP * HEAD_DIM), lambda b, g, i: (b * nq + i, g)),
        out_shape=jax.ShapeDtypeStruct((nb * L, D_MODEL), BF16),
        compiler_params=_cp(("parallel", "parallel", "parallel"), 56),
        name="attention",
    )(*args)


def _attn_mixer(x, g, mod, w_qkv, q_norm, k_norm, use_norm, sink, w_o, cache_k, cache_v, windowed, rope):
    q, k, v, new_k, new_v = _qkv_proj(x, g, mod, w_qkv.astype(BF16), q_norm, k_norm, use_norm, rope)
    op = _attention(q, 0, k, v, 0, SEQ, None, None, sink, BATCH, SEQ, False)
    kc = cache_k.reshape(DEC_BATCH, PAST_LEN, KV_DIM).astype(BF16)
    vc = cache_v.reshape(DEC_BATCH, PAST_LEN, KV_DIM).astype(BF16)
    if windowed:
        osm = _attention(q, TP, k, v, TP, DEC_SEQ, kc, vc, sink, DEC_BATCH, DEC_SEQ, True)
    else:
        n_keys = DEC_SEQ + PAST_LEN
        both = lambda a, c: jnp.concatenate([a[TP:].reshape(DEC_BATCH, DEC_SEQ, KV_DIM), c], axis=1).reshape(
            DEC_BATCH * n_keys, KV_DIM)
        osm = _attention(q, TP, both(k, kc), both(v, vc), 0, n_keys, None, None, sink, DEC_BATCH, DEC_SEQ, False)
    x = _resid_matmul(op, osm, w_o.astype(BF16), jnp.zeros((1, D_MODEL), F32), x, mod, "attn_out_proj")
    shape = (BATCH, SEQ, N_KV_HEADS, HEAD_DIM)
    return x, new_k.reshape(shape), new_v.reshape(shape)


ROUTE_TILE = 512
ROUTE_ROWS = 32


def _router_kernel(x_ref, g_ref, sh_ref, sc_ref, wr_ref, br_ref, xh_ref, rt_ref, cnt_ref, carry):
    tm = ROUTE_TILE
    i = pl.program_id(0)

    @pl.when(i == 0)
    def _():
        carry[...] = jnp.zeros_like(carry)

    h = _norm_mod(x_ref[...], g_ref[...], sh_ref[...], sc_ref[...])
    xh_ref[:, :HALF_D] = _pack_bf16_pairs(h)
    logits = _dot(h.astype(BF16), wr_ref[...])
    s = _sigmoid(logits.T[:N_EXPERTS, :])
    sb = s + br_ref[...]
    u = [s[e:e + 1, :] for e in range(N_EXPERTS)]
    v = [sb[e:e + 1, :] for e in range(N_EXPERTS)]

    gscore = []
    for gq in range(N_EXPERT_GROUPS):
        m = v[4 * gq:4 * gq + 4]
        best = m[PAIR_LO[0]] + m[PAIR_HI[0]]
        for a, b in zip(PAIR_LO[1:], PAIR_HI[1:]):
            best = jnp.maximum(best, m[a] + m[b])
        gscore.append(best)
    gidx = jnp.zeros((1, tm), jnp.int32)
    gbest = gscore[0]
    for gq in range(1, N_EXPERT_GROUPS):
        upd = gscore[gq] > gbest
        gidx = jnp.where(upd, gq, gidx)
        gbest = jnp.where(upd, gscore[gq], gbest)

    def pick(rows, j):
        out = rows[j]
        for gq in range(1, N_EXPERT_GROUPS):
            out = jnp.where(gidx == gq, rows[4 * gq + j], out)
        return out

    vin = [pick(v, j) for j in range(EXPERTS_PER_GROUP)]
    uin = [pick(u, j) for j in range(EXPERTS_PER_GROUP)]
    i1 = jnp.zeros((1, tm), jnp.int32)
    m1 = vin[0]
    for j in range(1, EXPERTS_PER_GROUP):
        upd = vin[j] > m1
        i1 = jnp.where(upd, j, i1)
        m1 = jnp.where(upd, vin[j], m1)
    i2 = jnp.full((1, tm), -1, jnp.int32)
    m2 = jnp.full((1, tm), -jnp.inf, F32)
    for j in range(EXPERTS_PER_GROUP):
        upd = (i1 != j) & (vin[j] > m2)
        i2 = jnp.where(upd, j, i2)
        m2 = jnp.where(upd, vin[j], m2)

    def sel(rows, idx):
        out = rows[0]
        for j in range(1, EXPERTS_PER_GROUP):
            out = jnp.where(idx == j, rows[j], out)
        return out

    w1, w2 = sel(uin, i1), sel(uin, i2)
    wsum = w1 + w2
    w1, w2 = w1 / wsum, w2 / wsum
    first_lo = i1 < i2
    lo = jnp.where(first_lo, i1, i2)
    hi = jnp.where(first_lo, i2, i1)
    w_lo = jnp.where(first_lo, w1, w2)
    w_hi = jnp.where(first_lo, w2, w1)
    pair = jnp.where(lo == 0, hi - 1, jnp.where(lo == 1, hi + 1, 5))
    bucket = gidx * len(PAIR_LO) + pair

    onehot = (lax.broadcasted_iota(jnp.int32, (ROUTE_ROWS, tm), 0) == bucket)
    tri = (lax.broadcasted_iota(jnp.int32, (tm, tm), 0) <= lax.broadcasted_iota(jnp.int32, (tm, tm), 1))
    cum = _dot(jnp.where(onehot, 1.0, 0.0).astype(BF16), jnp.where(tri, 1.0, 0.0).astype(BF16))
    rank = jnp.sum(jnp.where(onehot, cum - 1.0 + carry[...], 0.0), axis=0, keepdims=True)
    carry[...] = carry[...] + cum[:, tm - 1:tm]
    cnt_ref[...] = jnp.broadcast_to(carry[...], (ROUTE_ROWS, LANE))

    rt_ref[...] = jnp.zeros_like(rt_ref)
    rt_ref[0:1, :] = bucket.astype(F32)
    rt_ref[1:2, :] = rank
    wt = jnp.concatenate([w_lo, w_hi, jnp.zeros((LANE - 2, tm), F32)], axis=0)
    xh_ref[:, HALF_D:] = lax.bitcast_convert_type(wt.T, jnp.uint32)


def _router(x, g, mod, w_router, b_router):
    tm = ROUTE_TILE
    wr = _pad2(w_router, D_MODEL, LANE).astype(BF16)
    return pl.pallas_call(
        _router_kernel,
        grid=(T // tm,),
        in_specs=[
            pl.BlockSpec((tm, D_MODEL), lambda i: (i, 0)),
            pl.BlockSpec((1, D_MODEL), lambda i: (0, 0)),
            _mod_spec(tm, 3),
            _mod_spec(tm, 4),
            pl.BlockSpec((D_MODEL, LANE), lambda i: (0, 0)),
            pl.BlockSpec((N_EXPERTS, 1), lambda i: (0, 0)),
        ],
        out_specs=[
            pl.BlockSpec((tm, XH_W), lambda i: (i, 0)),
            pl.BlockSpec((8, tm), lambda i: (0, i)),
            pl.BlockSpec((ROUTE_ROWS, LANE), lambda i: (0, 0)),
        ],
        out_shape=[
            jax.ShapeDtypeStruct((T, XH_W), jnp.uint32),
            jax.ShapeDtypeStruct((8, T), F32),
            jax.ShapeDtypeStruct((ROUTE_ROWS, LANE), F32),
        ],
        scratch_shapes=[pltpu.VMEM((ROUTE_ROWS, 1), F32)],
        compiler_params=_cp(("arbitrary",)),
        name="moe_router",
    )(x, g, mod, mod, wr, b_router.reshape(N_EXPERTS, 1))


DISPATCH_TILE = 256


DMA_UNROLL = 32


def _invert_kernel(dest_ref, src_ref):
    def clear(s, c):
        src_ref[s] = 0
        return c

    def put(t, c):
        src_ref[dest_ref[t]] = t
        return c

    lax.fori_loop(0, T_PAD, clear, 0, unroll=DMA_UNROLL)
    lax.fori_loop(0, T, put, 0, unroll=DMA_UNROLL)


def _invert(dest):
    return pl.pallas_call(
        _invert_kernel,
        in_specs=[pl.BlockSpec(memory_space=pltpu.SMEM)],
        out_specs=pl.BlockSpec(memory_space=pltpu.SMEM),
        out_shape=jax.ShapeDtypeStruct((T_PAD,), jnp.int32),
        name="moe_invert",
    )(dest)


def _gather_rows(idx_ref, base, src_hbm, buf, sem, tm, static=False):
    def start(r, c):
        pltpu.make_async_copy(src_hbm.at[pl.ds(idx_ref[base + r], 1)], buf.at[pl.ds(r, 1)], sem).start()
        return c

    if static:
        for r in range(tm):
            start(r, 0)
    else:
        lax.fori_loop(0, tm, start, 0, unroll=DMA_UNROLL)


def _wait_rows(src_hbm, buf, sem, tm):
    pltpu.make_async_copy(src_hbm.at[pl.ds(0, tm)], buf, sem).wait()


def _expert_kernel(ea_ref, eb_ref, nv_ref, src_ref, xh_hbm, ga_ref, ua_ref, da_ref, gb_ref, ub_ref, db_ref, y_ref,
                   *scratch):
    tm = MOE_TILE
    j = pl.program_id(0)
    nv = nv_ref[j]
    *bufs, sems = scratch
    n_buf = len(bufs)

    @pl.when(j == 0)
    def _():
        for t in range(MOE_AHEAD):
            _gather_rows(src_ref, t * tm, xh_hbm, bufs[t], sems.at[t], tm)

    def run(p):
        cur, cur_sem = bufs[p], sems.at[p]
        q = (p + MOE_AHEAD) % n_buf
        ahead, ahead_sem = bufs[q], sems.at[q]

        @pl.when(jnp.logical_or(j < MOE_AHEAD, nv_ref[jnp.maximum(j - MOE_AHEAD, 0)] > 0))
        def _():
            _wait_rows(xh_hbm, cur, cur_sem, tm)

        @pl.when(nv > 0)
        def _():
            valid = lax.broadcasted_iota(jnp.int32, (tm, 1), 0) < nv
            x = jnp.where(valid, _unpack_bf16_pairs(cur[:, :HALF_D]), 0.0).astype(BF16)
            wts = jnp.where(valid, lax.bitcast_convert_type(cur[:, HALF_D:], F32), 0.0)
            for r in range(tm):
                pltpu.make_async_copy(xh_hbm.at[pl.ds(src_ref[(j + MOE_AHEAD) * tm + r], 1)],
                                      ahead.at[pl.ds(r, 1)], ahead_sem).start()

            def ffn(g_ref, u_ref, d_ref, w):
                a = _dot(x, g_ref[...])
                h = a * _sigmoid(a) * _dot(x, u_ref[...]) * w
                return _dot(h.astype(BF16), d_ref[...])

            y = ffn(ga_ref, ua_ref, da_ref, wts[:, 0:1]) + ffn(gb_ref, ub_ref, db_ref, wts[:, 1:2])
            y_ref[...] = _pack_bf16_pairs(y)

    for p in range(n_buf):
        pl.when(j % n_buf == p)(functools.partial(run, p))

    @pl.when(nv == 0)
    def _():
        y_ref[...] = jnp.zeros_like(y_ref)


def _experts(layer, tile_ea, tile_eb, tile_nv, src, xh, w_gate, w_up, w_down):
    tm = MOE_TILE
    up = lambda sel: pl.BlockSpec((None, None, D_MODEL, D_EXPERT),
                                  lambda j, ea, eb, nv, sr: (layer, (ea, eb)[sel][j], 0, 0))
    down = lambda sel: pl.BlockSpec((None, None, D_EXPERT, D_MODEL),
                                    lambda j, ea, eb, nv, sr: (layer, (ea, eb)[sel][j], 0, 0))
    return pl.pallas_call(
        _expert_kernel,
        grid_spec=pltpu.PrefetchScalarGridSpec(
            num_scalar_prefetch=4,
            grid=(MOE_TILES,),
            in_specs=[pl.BlockSpec(memory_space=pl.ANY), up(0), up(0), down(0), up(1), up(1), down(1)],
            out_specs=pl.BlockSpec((tm, HALF_D), lambda j, ea, eb, nv, sr: (j, 0)),
            scratch_shapes=[pltpu.VMEM((tm, XH_W), jnp.uint32)] * (MOE_AHEAD + 1)
            + [pltpu.SemaphoreType.DMA((MOE_AHEAD + 1,))],
        ),
        out_shape=jax.ShapeDtypeStruct((T_PAD, HALF_D), jnp.uint32),
        compiler_params=_cp(("arbitrary",), 56),
        name="moe_experts",
    )(tile_ea, tile_eb, tile_nv, src, xh, w_gate, w_up, w_down, w_gate, w_up, w_down)


def _combine_kernel(*refs, final):
    tm = DISPATCH_TILE
    if final:
        dest_ref, x_ref, gt_ref, ys_hbm, fg_ref, oc_ref, ol_ref, buf, sems = refs
    else:
        dest_ref, x_ref, gt_ref, ys_hbm, o_ref, buf, sems = refs
    i = pl.program_id(0)
    slot = i % 2

    last = pl.num_programs(0) - 1

    @pl.when(i == 0)
    def _():
        _gather_rows(dest_ref, 0, ys_hbm, buf.at[0], sems.at[0], tm)

    def finish():
        _wait_rows(ys_hbm, buf.at[slot], sems.at[slot], tm)
        x = x_ref[...] + gt_ref[...] * _unpack_bf16_pairs(buf[slot])
        if not final:
            o_ref[...] = x
            return
        y = x * lax.rsqrt(jnp.mean(x * x, axis=-1, keepdims=True) + NORM_EPS) * fg_ref[...]
        is_ctx = i < TP // tm

        @pl.when(is_ctx)
        def _():
            oc_ref[...] = y

        @pl.when(jnp.logical_not(is_ctx))
        def _():
            ol_ref[...] = y

    @pl.when(i < last)
    def _():
        _gather_rows(dest_ref, (i + 1) * tm, ys_hbm, buf.at[1 - slot], sems.at[1 - slot], tm, static=True)
        finish()

    pl.when(i == last)(finish)


def _combine(dest, x, mod, ys, final_gain=None):
    tm = DISPATCH_TILE
    final = final_gain is not None
    n_ctx = TP // tm
    in_specs = [pl.BlockSpec((tm, D_MODEL), lambda i, d: (i, 0)),
                pl.BlockSpec((None, None, 1, D_MODEL), lambda i, d: (_cond_row(i * tm), 5, 0, 0)),
                pl.BlockSpec(memory_space=pl.ANY)]
    args = [dest, x, mod, ys]
    if final:
        in_specs.append(pl.BlockSpec((1, D_MODEL), lambda i, d: (0, 0)))
        args.append(final_gain.reshape(1, D_MODEL))
        out_specs = [pl.BlockSpec((tm, D_MODEL), lambda i, d: (jnp.minimum(i, n_ctx - 1), 0)),
                     pl.BlockSpec((tm, D_MODEL), lambda i, d: (jnp.maximum(i - n_ctx, 0), 0))]
        out_shape = [jax.ShapeDtypeStruct((TP, D_MODEL), F32), jax.ShapeDtypeStruct((TS, D_MODEL), F32)]
    else:
        out_specs = pl.BlockSpec((tm, D_MODEL), lambda i, d: (i, 0))
        out_shape = jax.ShapeDtypeStruct((T, D_MODEL), F32)
    return pl.pallas_call(
        functools.partial(_combine_kernel, final=final),
        grid_spec=pltpu.PrefetchScalarGridSpec(
            num_scalar_prefetch=1,
            grid=(T // tm,),
            in_specs=in_specs,
            out_specs=out_specs,
            scratch_shapes=[pltpu.VMEM((2, tm, HALF_D), jnp.uint32), pltpu.SemaphoreType.DMA((2,))],
        ),
        out_shape=out_shape,
        compiler_params=_cp(("arbitrary",)),
        name="moe_combine",
    )(*args)


def _lookup(table, idx):
    n = table.shape[0]
    hit = idx[:, None] == jnp.arange(n, dtype=jnp.int32)[None, :]
    return jnp.sum(jnp.where(hit, table[None, :], 0), axis=1)


def _moe_plan(rt, cnt):
    bucket = rt[0].astype(jnp.int32)
    rank = rt[1].astype(jnp.int32)
    counts = cnt[:N_BUCKETS, 0].astype(jnp.int32)
    tiles = (counts + MOE_TILE - 1) // MOE_TILE
    order = jnp.arange(N_BUCKETS, dtype=jnp.int32)
    tile_start = jnp.sum(jnp.where(order[None, :] < order[:, None], tiles[None, :], 0), axis=1)
    tile_end = tile_start + tiles
    n_used = tile_end[N_BUCKETS - 1]
    dest = _lookup(tile_start * MOE_TILE, bucket) + rank
    j = jnp.arange(MOE_TILES, dtype=jnp.int32)
    jc = jnp.minimum(j, n_used - 1)
    b = jnp.minimum(jnp.sum((jc[:, None] >= tile_end[None, :]).astype(jnp.int32), axis=1), N_BUCKETS - 1)
    nv = jnp.clip(_lookup(counts, b) - (j - _lookup(tile_start, b)) * MOE_TILE, 0, MOE_TILE)
    nv = jnp.where(j < n_used, nv, 0)
    n_pairs = len(PAIR_LO)
    ea = (b // n_pairs) * EXPERTS_PER_GROUP + _lookup(jnp.asarray(PAIR_LO, jnp.int32), b % n_pairs)
    eb = (b // n_pairs) * EXPERTS_PER_GROUP + _lookup(jnp.asarray(PAIR_HI, jnp.int32), b % n_pairs)
    return dest, ea, eb, nv


def _moe(layer, x, g, mod, w_router, b_router, w_gate, w_up, w_down, final_gain=None):
    xh, rt, cnt = _router(x, g, mod, w_router, b_router)
    dest, ea, eb, nv = _moe_plan(rt, cnt)
    ys = _experts(layer, ea, eb, nv, _invert(dest), xh, w_gate, w_up, w_down)
    return _combine(dest, x, mod, ys, final_gain)


def kernel(x_prompt, x_sample, cache_k_full, cache_v_full, cache_k_win, cache_v_win, c, c_ctx, w_mod, b_mod, norm_mix, norm_ffn, final_norm, pool_w, pool_scale, hy_w_in, hy_b_in, hy_conv_w, hy_conv_b, hy_f_w1, hy_f_b1, hy_f_w2, hy_f_b2, hy_f_freq, hy_f_w3, hy_decay, hy_skip, hy_w_out, hy_b_out, fa_w_qkv, fa_q_norm, fa_k_norm, fa_w_o, wa_w_qkv, wa_sink, wa_w_o, w_router, b_router, moe_w_gate, moe_w_up, moe_w_down):
    x = None
    cond =jnp.concatenate([c_ctx[None, :], c, jnp.zeros((N_COND - 1 - DEC_BATCH, D_MODEL), F32)], axis=0)
    mods = _adaln(cond, w_mod, b_mod).reshape(DEPTH, N_COND, 6, 1, D_MODEL)
    rope = _rope_tables()
    ones_hd = jnp.ones((HEAD_DIM,), F32)
    wg_bf, wu_bf, wd_bf = moe_w_gate.astype(BF16), moe_w_up.astype(BF16), moe_w_down.astype(BF16)
    new_kv = {}
    for layer in range(DEPTH):
        kind = layer % 4
        j = layer // 4
        mod = mods[layer]
        g_mix = norm_mix[layer].reshape(1, D_MODEL)
        if kind == 0:
            assert layer == 0, "the pooling mixer reads the two input streams, so it must be the first layer"
            x = _pool_mixer(x_prompt.reshape(TP, D_MODEL), x_sample.reshape(TS, D_MODEL), g_mix, mod,
                            pool_w[j], pool_scale[j])
        elif kind == 1:
            x = _hyena_mixer(x, g_mix, mod, hy_w_in[j], hy_b_in[j], hy_conv_w[j], hy_conv_b[j], hy_f_w1[j],
                             hy_f_b1[j], hy_f_w2[j], hy_f_b2[j], hy_f_freq[j], hy_f_w3[j], hy_decay[j],
                             hy_skip[j], hy_w_out[j], hy_b_out[j])
        elif kind == 2:
            x, nk, nv = _attn_mixer(x, g_mix, mod, fa_w_qkv[j], fa_q_norm[j], fa_k_norm[j], True, None,
                                    fa_w_o[j], cache_k_full[:, j], cache_v_full[:, j], False, rope)
            new_kv.setdefault("kf", []).append(nk)
            new_kv.setdefault("vf", []).append(nv)
        else:
            x, nk, nv = _attn_mixer(x, g_mix, mod, wa_w_qkv[j], ones_hd, ones_hd, False, wa_sink[j],
                                    wa_w_o[j], cache_k_win[:, j], cache_v_win[:, j], True, rope)
            new_kv.setdefault("kw", []).append(nk)
            new_kv.setdefault("vw", []).append(nv)
        x = _moe(layer, x, norm_ffn[layer].reshape(1, D_MODEL), mod, w_router, b_router, wg_bf, wu_bf, wd_bf,
                 final_norm if layer == DEPTH - 1 else None)
    y_prompt, y_sample = x
    y_prompt = y_prompt.reshape(BATCH, SEQ, D_MODEL)
    y_sample = y_sample.reshape(DEC_BATCH, DEC_SEQ, D_MODEL)
    return (y_prompt, y_sample, jnp.stack(new_kv["kf"], axis=1), jnp.stack(new_kv["vf"], axis=1),
            jnp.stack(new_kv["kw"], axis=1), jnp.stack(new_kv["vw"], axis=1))
```

```python
import functools
import math

import jax
import jax.numpy as jnp
import numpy as np
from jax import lax
from jax.experimental import pallas as pl
from jax.experimental.pallas import tpu as pltpu

D_MODEL = 2048
BATCH = 32
SEQ = 256
DEPTH = 4
DEC_BATCH = 4
DEC_SEQ = 4096
PAST_LEN = 512
GRID_W = 64
N_HEADS = 16
N_KV_HEADS = 4
HEAD_DIM = D_MODEL // N_HEADS
KV_GROUP = N_HEADS // N_KV_HEADS
KV_DIM = N_KV_HEADS * HEAD_DIM
QKV_DIM = (N_HEADS + 2 * N_KV_HEADS) * HEAD_DIM
ROPE_THETA = 10000.0
WINDOW = 128
POOL_WINDOWS = (2, 4, 8, 16)
POOL_GROUP = D_MODEL // len(POOL_WINDOWS)
HYENA_EMB_BANDS = 16
HYENA_FILTER_HIDDEN = 64
N_EXPERTS = 16
N_EXPERT_GROUPS = 4
EXPERTS_PER_GROUP = 4
D_EXPERT = 512
NORM_EPS = 1e-6
NEG_INF = -1e30

F32 = jnp.float32
BF16 = jnp.bfloat16

TP = BATCH * SEQ
TS = DEC_BATCH * DEC_SEQ
T = TP + TS
N_COND = 8
LANE = 128
MIB = 1024 * 1024

PAIR_LO = (0, 0, 0, 1, 1, 2)
PAIR_HI = (1, 2, 3, 2, 3, 3)
N_BUCKETS = N_EXPERT_GROUPS * len(PAIR_LO)
MOE_TILE = 256
MOE_AHEAD = 3
MOE_TILES = T // MOE_TILE + N_BUCKETS + MOE_AHEAD
T_PAD = MOE_TILES * MOE_TILE
HALF_D = D_MODEL // 2
XH_W = HALF_D + LANE


def _cp(sem, vmem_mb=48):
    return pltpu.CompilerParams(dimension_semantics=sem, vmem_limit_bytes=vmem_mb * MIB)


def _dot(a, b):
    return jnp.dot(a, b, preferred_element_type=F32)


def _dot3(a, b):
    ah = a.astype(BF16)
    al = (a - ah.astype(F32)).astype(BF16)
    bh = b.astype(BF16)
    bl = (b - bh.astype(F32)).astype(BF16)
    return _dot(ah, bh) + (_dot(al, bh) + _dot(ah, bl))


def _sigmoid(x):
    return 1.0 / (1.0 + jnp.exp(-x))


def _pack_bf16_pairs(x):
    n = x.shape[1] // 2
    bits = lambda v: lax.bitcast_convert_type(v.astype(BF16).astype(F32), jnp.uint32)
    return (bits(x[:, :n]) >> 16) | bits(x[:, n:])


def _unpack_bf16_pairs(u):
    lo = lax.bitcast_convert_type(u << 16, F32)
    hi = lax.bitcast_convert_type(u & jnp.uint32(0xFFFF0000), F32)
    return jnp.concatenate([lo, hi], axis=1)


def _cond_row(r):
    return jnp.where(r < TP, 0, 1 + (r - TP) // DEC_SEQ)


def _mod_spec(tm, chunk, tn=D_MODEL, ncol=False):
    if ncol:
        return pl.BlockSpec((None, None, 1, tn), lambda i, j: (_cond_row(i * tm), chunk, 0, j))
    return pl.BlockSpec((None, None, 1, tn), lambda i, *_: (_cond_row(i * tm), chunk, 0, 0))


def _norm_mod(x, g, shift, scale):
    var = jnp.mean(x * x, axis=-1, keepdims=True)
    y = x * lax.rsqrt(var + NORM_EPS) * g
    return y * (1.0 + scale) + shift


def _adaln_kernel(c_ref, w_ref, b_ref, o_ref):
    c = c_ref[...]
    a = c * _sigmoid(c)
    o_ref[...] = _dot3(a, w_ref[...]) + b_ref[...]


def _adaln(cond, w_mod, b_mod):
    tn = 1024
    n = 6 * D_MODEL
    return pl.pallas_call(
        _adaln_kernel,
        grid=(DEPTH, n // tn),
        in_specs=[
            pl.BlockSpec((N_COND, D_MODEL), lambda l, j: (0, 0)),
            pl.BlockSpec((None, D_MODEL, tn), lambda l, j: (l, 0, j)),
            pl.BlockSpec((None, 1, tn), lambda l, j: (l, 0, j)),
        ],
        out_specs=pl.BlockSpec((None, N_COND, tn), lambda l, j: (l, 0, j)),
        out_shape=jax.ShapeDtypeStruct((DEPTH, N_COND, n), F32),
        compiler_params=_cp(("parallel", "parallel")),
        name="adaln",
    )(cond, w_mod, b_mod.reshape(DEPTH, 1, n))


def _nm_matmul_kernel(x_ref, g_ref, sh_ref, sc_ref, w_ref, b_ref, o_ref, h_scr):
    @pl.when(pl.program_id(1) == 0)
    def _():
        h_scr[...] = _norm_mod(x_ref[...], g_ref[...], sh_ref[...], sc_ref[...]).astype(BF16)

    o_ref[...] = (_dot(h_scr[...], w_ref[...]) + b_ref[...]).astype(o_ref.dtype)


def _nm_matmul(x, g, mod, w, b, out_dtype, name):
    tm, tn = 1024, 1024
    n = w.shape[1]
    return pl.pallas_call(
        _nm_matmul_kernel,
        grid=(T // tm, n // tn),
        in_specs=[
            pl.BlockSpec((tm, D_MODEL), lambda i, j: (i, 0)),
            pl.BlockSpec((1, D_MODEL), lambda i, j: (0, 0)),
            _mod_spec(tm, 0),
            _mod_spec(tm, 1),
            pl.BlockSpec((D_MODEL, tn), lambda i, j: (0, j)),
            pl.BlockSpec((1, tn), lambda i, j: (0, j)),
        ],
        out_specs=pl.BlockSpec((tm, tn), lambda i, j: (i, j)),
        out_shape=jax.ShapeDtypeStruct((T, n), out_dtype),
        scratch_shapes=[pltpu.VMEM((tm, D_MODEL), BF16)],
        compiler_params=_cp(("parallel", "arbitrary")),
        name=name,
    )(x, g, mod, mod, w, b)


RESID_TM = 1024


def _resid_matmul_kernel(ap_ref, as_ref, w_ref, b_ref, x_ref, gt_ref, o_ref):
    def emit(a_ref):
        o_ref[...] = x_ref[...] + gt_ref[...] * (_dot(a_ref[...], w_ref[...]) + b_ref[...])

    is_ctx = pl.program_id(0) < TP // RESID_TM
    pl.when(is_ctx)(lambda: emit(ap_ref))
    pl.when(jnp.logical_not(is_ctx))(lambda: emit(as_ref))


def _resid_matmul(a_ctx, a_lat, w, b, x, mod, name):
    tm, tn = RESID_TM, 1024
    k = a_ctx.shape[1]
    n_ctx = TP // tm
    return pl.pallas_call(
        _resid_matmul_kernel,
        grid=(T // tm, D_MODEL // tn),
        in_specs=[
            pl.BlockSpec((tm, k), lambda i, j: (jnp.minimum(i, n_ctx - 1), 0)),
            pl.BlockSpec((tm, k), lambda i, j: (jnp.maximum(i - n_ctx, 0), 0)),
            pl.BlockSpec((k, tn), lambda i, j: (0, j)),
            pl.BlockSpec((1, tn), lambda i, j: (0, j)),
            pl.BlockSpec((tm, tn), lambda i, j: (i, j)),
            _mod_spec(tm, 2, tn, ncol=True),
        ],
        out_specs=pl.BlockSpec((tm, tn), lambda i, j: (i, j)),
        out_shape=jax.ShapeDtypeStruct((T, D_MODEL), F32),
        compiler_params=_cp(("parallel", "parallel")),
        name=name,
    )(a_ctx, a_lat, w, b, x, mod)


POOL_TILE = 256
POOL_HALO = 8


def _seq_pos(r0):
    is_ctx = r0 < TP
    loc0 = jnp.where(is_ctx, r0 % SEQ, (r0 - TP) % DEC_SEQ)
    seq_len = jnp.where(is_ctx, SEQ, DEC_SEQ)
    return loc0, seq_len


def _pool_kernel(xc_ref, xcp_ref, xcn_ref, xl_ref, xlp_ref, xln_ref, *rest):
    is_ctx = pl.program_id(0) < TP // POOL_TILE
    pl.when(is_ctx)(lambda: _pool_tile(xc_ref, xcp_ref, xcn_ref, *rest))
    pl.when(jnp.logical_not(is_ctx))(lambda: _pool_tile(xl_ref, xlp_ref, xln_ref, *rest))


def _pool_tile(x_ref, xp_ref, xn_ref, g_ref, sh_ref, sc_ref, gt_ref, pw_ref, ps_ref, o_ref, hz_scr):
    tm, hl = POOL_TILE, POOL_HALO
    loc0, seq_len = _seq_pos(pl.program_id(0) * tm)
    has_prev = loc0 > 0
    has_next = loc0 + tm < seq_len
    g, sh, sc = g_ref[...], sh_ref[...], sc_ref[...]
    x = x_ref[...]
    h = _norm_mod(x, g, sh, sc)
    hz_scr[0:hl, :] = jnp.where(has_prev, _norm_mod(xp_ref[...], g, sh, sc), 0.0)
    hz_scr[hl:hl + tm, :] = h
    hz_scr[hl + tm:, :] = jnp.where(has_next, _norm_mod(xn_ref[...], g, sh, sc), 0.0)
    tl = loc0 + lax.broadcasted_iota(jnp.int32, (tm, 1), 0)
    outs = []
    for gi, w in enumerate(POOL_WINDOWS):
        cs = slice(gi * POOL_GROUP, (gi + 1) * POOL_GROUP)
        s = jnp.zeros((tm, POOL_GROUP), F32)
        for off in range(-(w // 2), w - w // 2):
            s = s + hz_scr[hl + off:hl + off + tm, cs]
        lo = jnp.maximum(tl - w // 2, 0)
        hi = jnp.minimum(tl + (w - w // 2), seq_len)
        d = s / (hi - lo).astype(F32) - h[:, cs]
        outs.append(_dot(d.astype(BF16), pw_ref[gi]))
    out = jnp.concatenate(outs, axis=1) * ps_ref[...]
    o_ref[...] = x + gt_ref[...] * out


def _pool_mixer(x_ctx, x_lat, g, mod, pool_w, pool_scale):
    tm, hl = POOL_TILE, POOL_HALO
    r = tm // hl

    def stream(first_tile, rows):
        tile = lambda i: jnp.clip(i - first_tile, 0, rows // tm - 1)
        return [pl.BlockSpec((tm, D_MODEL), lambda i: (tile(i), 0)),
                pl.BlockSpec((hl, D_MODEL), lambda i: (jnp.maximum(tile(i) * r - 1, 0), 0)),
                pl.BlockSpec((hl, D_MODEL), lambda i: (jnp.minimum((tile(i) + 1) * r, rows // hl - 1), 0))]

    return pl.pallas_call(
        _pool_kernel,
        grid=(T // tm,),
        in_specs=stream(0, TP) + stream(TP // tm, TS) + [
            pl.BlockSpec((1, D_MODEL), lambda i: (0, 0)),
            _mod_spec(tm, 0),
            _mod_spec(tm, 1),
            _mod_spec(tm, 2),
            pl.BlockSpec((len(POOL_WINDOWS), POOL_GROUP, POOL_GROUP), lambda i: (0, 0, 0)),
            pl.BlockSpec((1, D_MODEL), lambda i: (0, 0)),
        ],
        out_specs=pl.BlockSpec((tm, D_MODEL), lambda i: (i, 0)),
        out_shape=jax.ShapeDtypeStruct((T, D_MODEL), F32),
        scratch_shapes=[pltpu.VMEM((tm + 2 * hl, D_MODEL), F32)],
        compiler_params=_cp(("parallel",)),
        name="pool_mixer",
    )(x_ctx, x_ctx, x_ctx, x_lat, x_lat, x_lat, g, mod, mod, mod, pool_w.astype(BF16),
      pool_scale.reshape(1, D_MODEL))


CONV_TILE = 256
CONV_HALO = 16


def _conv3_kernel(u_ref, up_ref, un_ref, cw_ref, cb_ref, o_ref, scr):
    tm, hl = CONV_TILE, CONV_HALO
    loc0, seq_len = _seq_pos(pl.program_id(0) * tm)
    has_prev = loc0 > 0
    has_next = loc0 + tm < seq_len
    scr[0:hl, :] = jnp.where(has_prev, up_ref[...].astype(F32), 0.0)
    scr[hl:hl + tm, :] = u_ref[...].astype(F32)
    scr[hl + tm:, :] = jnp.where(has_next, un_ref[...].astype(F32), 0.0)
    out = (scr[hl - 1:hl - 1 + tm, :] * cw_ref[0:1, :] + scr[hl:hl + tm, :] * cw_ref[1:2, :]
           + scr[hl + 1:hl + 1 + tm, :] * cw_ref[2:3, :] + cb_ref[...])
    o_ref[...] = out.astype(o_ref.dtype)


def _conv3(u0, conv_w, conv_b):
    tm, hl, tc = CONV_TILE, CONV_HALO, D_MODEL
    r = tm // hl
    n = u0.shape[1]
    return pl.pallas_call(
        _conv3_kernel,
        grid=(T // tm, n // tc),
        in_specs=[
            pl.BlockSpec((tm, tc), lambda i, j: (i, j)),
            pl.BlockSpec((hl, tc), lambda i, j: (jnp.maximum(i * r - 1, 0), j)),
            pl.BlockSpec((hl, tc), lambda i, j: (jnp.minimum((i + 1) * r, T // hl - 1), j)),
            pl.BlockSpec((3, tc), lambda i, j: (0, j)),
            pl.BlockSpec((1, tc), lambda i, j: (0, j)),
        ],
        out_specs=pl.BlockSpec((tm, tc), lambda i, j: (i, j)),
        out_shape=jax.ShapeDtypeStruct((T, n), BF16),
        scratch_shapes=[pltpu.VMEM((tm + 2 * hl, tc), F32)],
        compiler_params=_cp(("parallel", "parallel")),
        name="hyena_conv3",
    )(u0, u0, u0, conv_w, conv_b.reshape(1, n))


FILT_TILE = 256


HYENA_BLOCK = 1024


T_LANE = LANE - 1


def _filter_mlp_kernel(emb_ref, w1_ref, b1_ref, w2_ref, b2_ref, fr_ref, o_ref):
    emb = emb_ref[...]
    fr = fr_ref[...]
    a = jnp.sin(fr * (_dot3(emb, w1_ref[...]) + b1_ref[...]))
    a = jnp.sin(fr * (_dot3(a, w2_ref[...]) + b2_ref[...]))
    lane = lax.broadcasted_iota(jnp.int32, a.shape, 1)
    o_ref[...] = jnp.where(lane == T_LANE, emb[:, 0:1], a)


def _filter_mlp(pos, L, f_w1, f_b1, f_w2, f_b2, f_freq):
    assert HYENA_FILTER_HIDDEN <= T_LANE
    tl = FILT_TILE
    rows = pos.shape[0]
    small = lambda: pl.BlockSpec((LANE, LANE), lambda i: (0, 0))
    vec = lambda: pl.BlockSpec((1, LANE), lambda i: (0, 0))
    return pl.pallas_call(
        _filter_mlp_kernel,
        grid=(rows // tl,),
        in_specs=[pl.BlockSpec((tl, LANE), lambda i: (i, 0)), small(), vec(), small(), vec(), vec()],
        out_specs=pl.BlockSpec((tl, LANE), lambda i: (i, 0)),
        out_shape=jax.ShapeDtypeStruct((rows, LANE), F32),
        compiler_params=_cp(("parallel",)),
        name="hyena_filter_mlp",
    )(_filter_embedding(pos, L), _pad2(f_w1, LANE, LANE), _pad2(f_b1[None], 1, LANE),
      _pad2(f_w2, LANE, LANE), _pad2(f_b2[None], 1, LANE), _pad2(f_freq[None], 1, LANE))


def _filter_kernel(h1_ref, h2_ref, w3a_ref, dca_ref, w3b_ref, dcb_ref, fa_ref, fb_ref, *, blk):
    def taps(h, w3_ref, dc_ref):
        return _dot3(h, w3_ref[...]) * jnp.exp(-h[:, T_LANE:] * jnp.abs(dc_ref[...]))

    pos = taps(h1_ref[...], w3a_ref, dca_ref)
    neg = taps(h2_ref[...], w3b_ref, dcb_ref)
    m = (pl.program_id(0) * FILT_TILE + lax.broadcasted_iota(jnp.int32, (FILT_TILE, 1), 0)) % blk
    fa_ref[...] = jnp.where(m == 0, pos, pos + neg).astype(BF16)
    fb_ref[...] = jnp.where(m == 0, 0.0, neg - pos).astype(BF16)


def _pad2(a, rows, cols):
    return jnp.pad(a, ((0, rows - a.shape[0]), (0, cols - a.shape[1])))


def _filter_positions(L, blk):
    n_blk = L // blk
    m = np.arange(blk)
    p1, p2 = [], []
    for d in range(-(n_blk - 1), n_blk):
        if d >= 1:
            p1.append(d * blk + m), p2.append(d * blk - m)
        elif d == 0:
            p1.append(m), p2.append(m)
        else:
            p1.append(-d * blk - m), p2.append(-d * blk + m)
    return np.concatenate(p1), np.concatenate(p2)


def _filter_embedding(pos, L):
    t = jnp.asarray(pos, F32) / L
    bands = jnp.linspace(1e-4, HYENA_EMB_BANDS - 1, HYENA_EMB_BANDS, dtype=F32)
    ang = (2 * math.pi) * t[:, None] * bands[None, :]
    return _pad2(jnp.concatenate([t[:, None], jnp.cos(ang), -jnp.sin(ang)], axis=-1), pos.shape[0], LANE)


def _hyena_filters(L, blk, f_w1, f_b1, f_w2, f_b2, f_freq, f_w3, decay):
    n_blk = L // blk
    p1, p2 = _filter_positions(L, blk)
    rows = p1.shape[0]
    tl = FILT_TILE
    tiles_per_lag = blk // tl
    lag = lambda i: i // tiles_per_lag - (n_blk - 1)
    col1 = lambda i, n: 2 * n + jnp.where(lag(i) >= 0, 0, 1)
    col2 = lambda i, n: 2 * n + jnp.where(lag(i) >= 1, 0, 1)
    n_tiles = rows // tl
    out = pl.BlockSpec((None, tl, D_MODEL), lambda i, n: (n, i, 0))
    w3 = _pad2(f_w3, LANE, f_w3.shape[1])
    hidden = _filter_mlp(np.concatenate([p1, p2]), L, f_w1, f_b1, f_w2, f_b2, f_freq)
    return pl.pallas_call(
        functools.partial(_filter_kernel, blk=blk),
        grid=(n_tiles, 2),
        in_specs=[
            pl.BlockSpec((tl, LANE), lambda i, n: (i, 0)),
            pl.BlockSpec((tl, LANE), lambda i, n: (n_tiles + i, 0)),
            pl.BlockSpec((LANE, D_MODEL), lambda i, n: (0, col1(i, n))),
            pl.BlockSpec((1, D_MODEL), lambda i, n: (0, col1(i, n))),
            pl.BlockSpec((LANE, D_MODEL), lambda i, n: (0, col2(i, n))),
            pl.BlockSpec((1, D_MODEL), lambda i, n: (0, col2(i, n))),
        ],
        out_specs=[out, out],
        out_shape=[jax.ShapeDtypeStruct((2, rows, D_MODEL), BF16)] * 2,
        compiler_params=_cp(("parallel", "parallel")),
        name="hyena_filters",
    )(hidden, hidden, w3, decay[None], w3, decay[None])


def _dft_mats(L):
    r = int(math.isqrt(L))
    k2 = 2 * jnp.arange(L, dtype=jnp.int32)[:, None] + 1
    n1 = r * jnp.arange(L // r, dtype=jnp.int32)[None, :]
    n2 = jnp.arange(r, dtype=jnp.int32)[None, :]
    sc = math.pi / (2 * L)
    aa = ((k2 * n1) % (4 * L)).astype(F32) * sc
    ab = ((k2 * n2) % (4 * L)).astype(F32) * sc
    ca, sa, cb, sb = jnp.cos(aa)[:, :, None], jnp.sin(aa)[:, :, None], jnp.cos(ab)[:, None, :], jnp.sin(ab)[:, None, :]
    c = (ca * cb - sa * sb).reshape(L, L)
    s = (sa * cb + ca * sb).reshape(L, L)
    return c.astype(BF16), s.astype(BF16), c.T.astype(BF16), s.T.astype(BF16)


def _dft_tiles(L):
    return min(512, L), 512


def _dft_filter_kernel(c_ref, s_ref, a_ref, b_ref, gr_ref, gi_ref):
    gr_ref[...] = _dot(c_ref[...], a_ref[...]).astype(gr_ref.dtype)
    gi_ref[...] = _dot(s_ref[...], b_ref[...]).astype(gi_ref.dtype)


def _dft_filter(cm, sm, fa, fb, L):
    tf, tn = _dft_tiles(L)
    n = fa.shape[0]
    mat = lambda: pl.BlockSpec((tf, L), lambda k, c, s: (k, 0))
    rhs = lambda: pl.BlockSpec((None, L, tn), lambda k, c, s: (s, 0, c))
    out = pl.BlockSpec((None, tf, tn), lambda k, c, s: (s, k, c))
    return pl.pallas_call(
        _dft_filter_kernel,
        grid=(L // tf, D_MODEL // tn, n),
        in_specs=[mat(), mat(), rhs(), rhs()],
        out_specs=[out, out],
        out_shape=[jax.ShapeDtypeStruct((n, L, D_MODEL), BF16)] * 2,
        compiler_params=_cp(("parallel", "parallel", "parallel")),
        name="hyena_filter_dft",
    )(cm, sm, fa, fb)


FWD_TF = 256


def _dft_fwd_kernel(c_ref, s_ref, z_ref, gr_ref, gi_ref, yr_ref, yi_ref, *, n_blk, blk, bpb):
    c, s = c_ref[...], s_ref[...]
    for bb in range(bpb):
        zc, zs = [], []
        for j in range(n_blk):
            r = (bb * n_blk + j) * blk
            zj = z_ref[r:r + blk, :]
            zc.append(_dot(c, zj).astype(BF16))
            zs.append(_dot(s, zj).astype(BF16))
        for i in range(n_blk):
            yr = yi = None
            for j in range(n_blk):
                lag = i - j + n_blk - 1
                gr, gi = gr_ref[lag], gi_ref[lag]
                tr = gr * zc[j] + gi * zs[j]
                ti = gi * zc[j] - gr * zs[j]
                yr = tr if yr is None else yr + tr
                yi = ti if yi is None else yi + ti
            yr_ref[bb, i] = yr.astype(BF16)
            yi_ref[bb, i] = yi.astype(BF16)


def _seqs_per_step(L):
    return max(1, 2048 // L)


def _dft_fwd(cm, sm, z, z_rowblk, z_colblk, gr, gi, order, nb, L, blk):
    n_blk = L // blk
    bpb = _seqs_per_step(L)
    assert nb % bpb == 0 and z_rowblk % bpb == 0
    tf, tn = min(FWD_TF, blk), 512
    mat = lambda: pl.BlockSpec((tf, blk), lambda k, c, b: (k, 0))
    gsp = lambda: pl.BlockSpec((None, 2 * n_blk - 1, tf, tn), lambda k, c, b: (order, 0, k, c))
    out = pl.BlockSpec((bpb, n_blk, tf, tn), lambda k, c, b: (b, 0, k, c))
    return pl.pallas_call(
        functools.partial(_dft_fwd_kernel, n_blk=n_blk, blk=blk, bpb=bpb),
        grid=(blk // tf, D_MODEL // tn, nb // bpb),
        in_specs=[mat(), mat(),
                  pl.BlockSpec((bpb * L, tn), lambda k, c, b: (z_rowblk // bpb + b, z_colblk + c)),
                  gsp(), gsp()],
        out_specs=[out, out],
        out_shape=[jax.ShapeDtypeStruct((nb, n_blk, blk, D_MODEL), BF16)] * 2,
        compiler_params=_cp(("parallel", "parallel", "parallel")),
        name="hyena_dft_fwd",
    )(cm, sm, z, gr, gi)


def _dft_inv_kernel(ct_ref, st_ref, yr_ref, yi_ref, z_ref, gt_ref, sk_ref, o_ref, *, inv_len, bpb, tt):
    for bb in range(bpb):
        rows = slice(bb * tt, (bb + 1) * tt)
        y = (_dot(ct_ref[...], yr_ref[bb]) - _dot(st_ref[...], yi_ref[bb])) * inv_len
        o_ref[rows, :] = (gt_ref[rows, :].astype(F32) * (y + sk_ref[...] * z_ref[rows, :].astype(F32))).astype(BF16)


def _dft_inv(ctm, stm, yr, yi, z, z_rowblk, z_colblk, gate, g_rowblk, g_colblk, skip, nb, L):
    tt, tn = _dft_tiles(L)
    rpb = L // tt
    bpb = _seqs_per_step(L) if rpb == 1 else 1
    assert nb % bpb == 0 and z_rowblk % bpb == 0 and g_rowblk % bpb == 0
    mat = lambda: pl.BlockSpec((tt, L), lambda t, c, b: (t, 0))
    spec = lambda: pl.BlockSpec((bpb, L, tn), lambda t, c, b: (b, 0, c))
    rows = lambda blk0: (lambda t, c, b: ((blk0 + b * bpb * rpb + t) // bpb))
    return pl.pallas_call(
        functools.partial(_dft_inv_kernel, inv_len=1.0 / L, bpb=bpb, tt=tt),
        grid=(rpb, D_MODEL // tn, nb // bpb),
        in_specs=[mat(), mat(), spec(), spec(),
                  pl.BlockSpec((bpb * tt, tn), lambda t, c, b: (rows(z_rowblk)(t, c, b), z_colblk + c)),
                  pl.BlockSpec((bpb * tt, tn), lambda t, c, b: (rows(g_rowblk)(t, c, b), g_colblk + c)),
                  pl.BlockSpec((1, tn), lambda t, c, b: (0, c))],
        out_specs=pl.BlockSpec((bpb * tt, tn), lambda t, c, b: (rows(0)(t, c, b), c)),
        out_shape=jax.ShapeDtypeStruct((nb * L, D_MODEL), BF16),
        compiler_params=_cp(("parallel", "parallel", "parallel")),
        name="hyena_dft_inv",
    )(ctm, stm, yr, yi, z, gate, skip)


def _hyena_stream(u, row0, nb, L, fparams, skip):
    blk = min(HYENA_BLOCK, L)
    n_blk = L // blk
    n_lag = 2 * n_blk - 1
    cm, sm, ctm, stm = _dft_mats(blk)
    fa, fb = _hyena_filters(L, blk, *fparams)
    seg = lambda a: a.reshape(2 * n_lag, blk, D_MODEL)
    gr, gi = _dft_filter(cm, sm, seg(fa), seg(fb), blk)
    gr, gi = (a.reshape(2, n_lag, blk, D_MODEL) for a in (gr, gi))
    tt, tn = _dft_tiles(blk)
    ncb = D_MODEL // tn
    blocks = lambda a: a.reshape(nb * n_blk, blk, D_MODEL)

    yr, yi = _dft_fwd(cm, sm, u, row0 // L, 0, gr, gi, 0, nb, L, blk)
    z1 = _dft_inv(ctm, stm, blocks(yr), blocks(yi), u, row0 // tt, 0, u, row0 // tt, ncb, skip[0:1],
                  nb * n_blk, blk)
    yr, yi = _dft_fwd(cm, sm, z1, 0, 0, gr, gi, 1, nb, L, blk)
    return _dft_inv(ctm, stm, blocks(yr), blocks(yi), z1, 0, 0, u, row0 // tt, 2 * ncb, skip[1:2],
                    nb * n_blk, blk)


def _hyena_mixer(x, g, mod, w_in, b_in, conv_w, conv_b, f_w1, f_b1, f_w2, f_b2, f_freq, f_w3, decay, skip,
                 w_out, b_out):
    u0 = _nm_matmul(x, g, mod, w_in.astype(BF16), b_in.reshape(1, -1), BF16, "hyena_in_proj")
    u = _conv3(u0, conv_w, conv_b)
    fparams = (f_w1, f_b1, f_w2, f_b2, f_freq, f_w3, decay)
    zp = _hyena_stream(u, 0, BATCH, SEQ, fparams, skip)
    zs = _hyena_stream(u, TP, DEC_BATCH, DEC_SEQ, fparams, skip)
    return _resid_matmul(zp, zs, w_out.astype(BF16), b_out.reshape(1, -1), x, mod, "hyena_out_proj")


def _rope_tables():
    pos = jnp.arange(DEC_SEQ, dtype=jnp.int32)
    row = (pos // GRID_W).astype(F32)
    col = (pos % GRID_W).astype(F32)
    axis_dim = HEAD_DIM // 2
    inv_freq = ROPE_THETA ** (-jnp.arange(0, axis_dim, 2, dtype=F32) / axis_dim)
    ar = row[:, None] * inv_freq[None, :]
    ac = col[:, None] * inv_freq[None, :]
    cos = jnp.concatenate([jnp.cos(ar), jnp.cos(ar), jnp.cos(ac), jnp.cos(ac)], axis=-1)
    sin = jnp.concatenate([-jnp.sin(ar), jnp.sin(ar), -jnp.sin(ac), jnp.sin(ac)], axis=-1)
    return cos, sin


QKV_TM = 512
QKV_TN = 1024
PAIR = 2 * HEAD_DIM


def _qkv_kernel(x_ref, g_ref, sh_ref, sc_ref, w_ref, qn_ref, kn_ref, cos_ref, sin_ref,
                q_ref, k_ref, v_ref, nk_ref, nv_ref, h_scr, *, use_norm):
    tm = QKV_TM
    i, j = pl.program_id(0), pl.program_id(1)

    @pl.when(j == 0)
    def _():
        h_scr[...] = _norm_mod(x_ref[...], g_ref[...], sh_ref[...], sc_ref[...]).astype(BF16)

    quarter = HEAD_DIM // 4
    scale = HEAD_DIM ** -0.5 * LOG2E

    def head(xh, gn):
        if use_norm:
            xh = xh * lax.rsqrt(jnp.mean(xh * xh, axis=-1, keepdims=True) + NORM_EPS) * gn
        return xh

    def rope(xh):
        lane = lax.broadcasted_iota(jnp.int32, (tm, HEAD_DIM), 1)
        first = (lane % (2 * quarter)) < quarter
        partner = jnp.where(first, pltpu.roll(xh, HEAD_DIM - quarter, 1), pltpu.roll(xh, quarter, 1))
        return xh * cos_ref[...] + partner * sin_ref[...]

    def proj(c0):
        return _dot(h_scr[...], w_ref[:, c0:c0 + PAIR])

    def q_tile(latent):
        for p in range(QKV_TN // PAIR):
            acc = proj(p * PAIR)
            for t in range(2):
                xh = head(acc[:, t * HEAD_DIM:(t + 1) * HEAD_DIM], qn_ref[...])
                xh = rope(xh) if latent else xh
                c0 = p * PAIR + t * HEAD_DIM
                q_ref[:, c0:c0 + HEAD_DIM] = (xh * scale).astype(BF16)

    def kv_tile(latent):
        for p in range(KV_DIM // PAIR):
            acc = proj(p * PAIR)
            for t in range(2):
                c0 = p * PAIR + t * HEAD_DIM
                kh = head(acc[:, t * HEAD_DIM:(t + 1) * HEAD_DIM], kn_ref[...])
                if not latent:
                    nk_ref[:, c0:c0 + HEAD_DIM] = kh
                k_ref[:, c0:c0 + HEAD_DIM] = (rope(kh) if latent else kh).astype(BF16)
        for p in range(KV_DIM // PAIR):
            acc = proj(KV_DIM + p * PAIR)
            if not latent:
                nv_ref[:, p * PAIR:(p + 1) * PAIR] = acc
            v_ref[:, p * PAIR:(p + 1) * PAIR] = acc.astype(BF16)

    is_ctx = i < TP // tm
    is_q = j < D_MODEL // QKV_TN
    for latent in (False, True):
        stream = jnp.logical_not(is_ctx) if latent else is_ctx
        pl.when(jnp.logical_and(stream, is_q))(functools.partial(q_tile, latent))
        pl.when(jnp.logical_and(stream, jnp.logical_not(is_q)))(functools.partial(kv_tile, latent))


def _qkv_proj(x, g, mod, w_qkv, q_norm, k_norm, use_norm, rope):
    tm, tn = QKV_TM, QKV_TN
    n_ctx = TP // tm
    n_q = D_MODEL // tn
    tab = lambda: pl.BlockSpec((tm, HEAD_DIM), lambda i, j: (jnp.maximum(i - n_ctx, 0) % (DEC_SEQ // tm), 0))
    kv = lambda: pl.BlockSpec((tm, KV_DIM), lambda i, j: (i, 0))
    new = lambda: pl.BlockSpec((tm, KV_DIM), lambda i, j: (jnp.minimum(i, n_ctx - 1), 0))
    return pl.pallas_call(
        functools.partial(_qkv_kernel, use_norm=use_norm),
        grid=(T // tm, QKV_DIM // tn),
        in_specs=[
            pl.BlockSpec((tm, D_MODEL), lambda i, j: (i, 0)),
            pl.BlockSpec((1, D_MODEL), lambda i, j: (0, 0)),
            _mod_spec(tm, 0),
            _mod_spec(tm, 1),
            pl.BlockSpec((D_MODEL, tn), lambda i, j: (0, j)),
            pl.BlockSpec((1, HEAD_DIM), lambda i, j: (0, 0)),
            pl.BlockSpec((1, HEAD_DIM), lambda i, j: (0, 0)),
            tab(), tab(),
        ],
        out_specs=[pl.BlockSpec((tm, tn), lambda i, j: (i, jnp.minimum(j, n_q - 1))), kv(), kv(), new(), new()],
        out_shape=[jax.ShapeDtypeStruct((T, D_MODEL), BF16), jax.ShapeDtypeStruct((T, KV_DIM), BF16),
                   jax.ShapeDtypeStruct((T, KV_DIM), BF16), jax.ShapeDtypeStruct((TP, KV_DIM), F32),
                   jax.ShapeDtypeStruct((TP, KV_DIM), F32)],
        scratch_shapes=[pltpu.VMEM((tm, D_MODEL), BF16)],
        compiler_params=_cp(("arbitrary", "arbitrary")),
        name="qkv_proj",
    )(x, g, mod, mod, w_qkv, q_norm.reshape(1, HEAD_DIM), k_norm.reshape(1, HEAD_DIM), *rope)


LOG2E = math.log2(math.e)
ATTN_TQ = 256


def _attn_kernel(*refs, tq, seq_len, has_ctx, windowed, has_sink):
    it = iter(refs)
    q_ref, k_ref, v_ref = next(it), next(it), next(it)
    kc_ref, vc_ref = (next(it), next(it)) if has_ctx else (None, None)
    sink_ref = next(it) if has_sink else None
    o_ref = next(it)
    if windowed:
        i = pl.program_id(2)
        span = tq + 2 * WINDOW
        start = pl.multiple_of(jnp.clip(i * tq - WINDOW, 0, seq_len - span), WINDOW)
        qpos = i * tq + lax.broadcasted_iota(jnp.int32, (tq, 1), 0)
        kpos = start + lax.broadcasted_iota(jnp.int32, (1, span), 1)
        segs = [(k_ref, v_ref, pl.ds(start, span), jnp.abs(kpos - qpos) <= WINDOW)]
    else:
        segs = [(k_ref, v_ref, slice(None), None)]
    if has_ctx:
        segs.append((kc_ref, vc_ref, slice(None), None))
    for h in range(KV_GROUP):
        hs = slice(h * HEAD_DIM, (h + 1) * HEAD_DIM)
        qh = q_ref[:, hs]
        scores = []
        m = None
        for kr, _, rows, mask in segs:
            s = lax.dot_general(qh, kr[rows, :], (((1,), (1,)), ((), ())), preferred_element_type=F32)
            if mask is not None:
                s = jnp.where(mask, s, NEG_INF)
            scores.append(s)
            ms = jnp.max(s, axis=-1, keepdims=True)
            m = ms if m is None else jnp.maximum(m, ms)
        if has_sink:
            sk = sink_ref[pl.program_id(1) * KV_GROUP + h]
            m = jnp.maximum(m, sk)
        l = jnp.exp2(sk - m) if has_sink else jnp.zeros_like(m)
        acc = jnp.zeros((tq, HEAD_DIM), F32)
        for (_, vr, rows, _), s in zip(segs, scores):
            p = jnp.exp2(s - m)
            l = l + jnp.sum(p, axis=-1, keepdims=True)
            acc = acc + _dot(p.astype(BF16), vr[rows, :])
        o_ref[:, hs] = (acc / l).astype(BF16)


def _attention(q, row0, k, v, k_row0, n_keys, k_ctx, v_ctx, sink, nb, L, windowed):
    tq = min(ATTN_TQ, L)
    nq = L // tq
    q_blk0 = row0 // tq
    seq0 = k_row0 // n_keys
    own = lambda: pl.BlockSpec((n_keys, HEAD_DIM), lambda b, g, i: (seq0 + b, g))
    in_specs = [pl.BlockSpec((tq, KV_GROUP * HEAD_DIM), lambda b, g, i: (q_blk0 + b * nq + i, g)), own(), own()]
    args = [q, k, v]
    if k_ctx is not None:
        ctx = lambda: pl.BlockSpec((None, PAST_LEN, HEAD_DIM), lambda b, g, i: (b, 0, g))
        in_specs += [ctx(), ctx()]
        args += [k_ctx, v_ctx]
    if sink is not None:
        in_specs.append(pl.BlockSpec(memory_space=pltpu.SMEM))
        args.append(sink.astype(F32) * LOG2E)
    return pl.pallas_call(
        functools.partial(_attn_kernel, tq=tq, seq_len=L, has_ctx=k_ctx is not None, windowed=windowed,
                          has_sink=sink is not None),
        grid=(nb, N_KV_HEADS, nq),
        in_specs=in_specs,
        out_specs=pl.BlockSpec((tq, KV_GROUP * HEAD_DIM), lambda b, g, i: (b * nq + i, g)),
        out_shape=jax.ShapeDtypeStruct((nb * L, D_MODEL), BF16),
        compiler_params=_cp(("parallel", "parallel", "parallel"), 56),
        name="attention",
    )(*args)


def _attn_mixer(x, g, mod, w_qkv, q_norm, k_norm, use_norm, sink, w_o, cache_k, cache_v, windowed, rope):
    q, k, v, new_k, new_v = _qkv_proj(x, g, mod, w_qkv.astype(BF16), q_norm, k_norm, use_norm, rope)
    op = _attention(q, 0, k, v, 0, SEQ, None, None, sink, BATCH, SEQ, False)
    kc = cache_k.reshape(DEC_BATCH, PAST_LEN, KV_DIM).astype(BF16)
    vc = cache_v.reshape(DEC_BATCH, PAST_LEN, KV_DIM).astype(BF16)
    if windowed:
        osm = _attention(q, TP, k, v, TP, DEC_SEQ, kc, vc, sink, DEC_BATCH, DEC_SEQ, True)
    else:
        n_keys = DEC_SEQ + PAST_LEN
        both = lambda a, c: jnp.concatenate([a[TP:].reshape(DEC_BATCH, DEC_SEQ, KV_DIM), c], axis=1).reshape(
            DEC_BATCH * n_keys, KV_DIM)
        osm = _attention(q, TP, both(k, kc), both(v, vc), 0, n_keys, None, None, sink, DEC_BATCH, DEC_SEQ, False)
    x = _resid_matmul(op, osm, w_o.astype(BF16), jnp.zeros((1, D_MODEL), F32), x, mod, "attn_out_proj")
    shape = (BATCH, SEQ, N_KV_HEADS, HEAD_DIM)
    return x, new_k.reshape(shape), new_v.reshape(shape)


ROUTE_TILE = 512
ROUTE_ROWS = 32


def _router_kernel(x_ref, g_ref, sh_ref, sc_ref, wr_ref, br_ref, xh_ref, rt_ref, cnt_ref, carry):
    tm = ROUTE_TILE
    i = pl.program_id(0)

    @pl.when(i == 0)
    def _():
        carry[...] = jnp.zeros_like(carry)

    h = _norm_mod(x_ref[...], g_ref[...], sh_ref[...], sc_ref[...])
    xh_ref[:, :HALF_D] = _pack_bf16_pairs(h)
    logits = _dot(h.astype(BF16), wr_ref[...])
    s = _sigmoid(logits.T[:N_EXPERTS, :])
    sb = s + br_ref[...]
    u = [s[e:e + 1, :] for e in range(N_EXPERTS)]
    v = [sb[e:e + 1, :] for e in range(N_EXPERTS)]

    gscore = []
    for gq in range(N_EXPERT_GROUPS):
        m = v[4 * gq:4 * gq + 4]
        best = m[PAIR_LO[0]] + m[PAIR_HI[0]]
        for a, b in zip(PAIR_LO[1:], PAIR_HI[1:]):
            best = jnp.maximum(best, m[a] + m[b])
        gscore.append(best)
    gidx = jnp.zeros((1, tm), jnp.int32)
    gbest = gscore[0]
    for gq in range(1, N_EXPERT_GROUPS):
        upd = gscore[gq] > gbest
        gidx = jnp.where(upd, gq, gidx)
        gbest = jnp.where(upd, gscore[gq], gbest)

    def pick(rows, j):
        out = rows[j]
        for gq in range(1, N_EXPERT_GROUPS):
            out = jnp.where(gidx == gq, rows[4 * gq + j], out)
        return out

    vin = [pick(v, j) for j in range(EXPERTS_PER_GROUP)]
    uin = [pick(u, j) for j in range(EXPERTS_PER_GROUP)]
    i1 = jnp.zeros((1, tm), jnp.int32)
    m1 = vin[0]
    for j in range(1, EXPERTS_PER_GROUP):
        upd = vin[j] > m1
        i1 = jnp.where(upd, j, i1)
        m1 = jnp.where(upd, vin[j], m1)
    i2 = jnp.full((1, tm), -1, jnp.int32)
    m2 = jnp.full((1, tm), -jnp.inf, F32)
    for j in range(EXPERTS_PER_GROUP):
        upd = (i1 != j) & (vin[j] > m2)
        i2 = jnp.where(upd, j, i2)
        m2 = jnp.where(upd, vin[j], m2)

    def sel(rows, idx):
        out = rows[0]
        for j in range(1, EXPERTS_PER_GROUP):
            out = jnp.where(idx == j, rows[j], out)
        return out

    w1, w2 = sel(uin, i1), sel(uin, i2)
    wsum = w1 + w2
    w1, w2 = w1 / wsum, w2 / wsum
    first_lo = i1 < i2
    lo = jnp.where(first_lo, i1, i2)
    hi = jnp.where(first_lo, i2, i1)
    w_lo = jnp.where(first_lo, w1, w2)
    w_hi = jnp.where(first_lo, w2, w1)
    pair = jnp.where(lo == 0, hi - 1, jnp.where(lo == 1, hi + 1, 5))
    bucket = gidx * len(PAIR_LO) + pair

    onehot = (lax.broadcasted_iota(jnp.int32, (ROUTE_ROWS, tm), 0) == bucket)
    tri = (lax.broadcasted_iota(jnp.int32, (tm, tm), 0) <= lax.broadcasted_iota(jnp.int32, (tm, tm), 1))
    cum = _dot(jnp.where(onehot, 1.0, 0.0).astype(BF16), jnp.where(tri, 1.0, 0.0).astype(BF16))
    rank = jnp.sum(jnp.where(onehot, cum - 1.0 + carry[...], 0.0), axis=0, keepdims=True)
    carry[...] = carry[...] + cum[:, tm - 1:tm]
    cnt_ref[...] = jnp.broadcast_to(carry[...], (ROUTE_ROWS, LANE))

    rt_ref[...] = jnp.zeros_like(rt_ref)
    rt_ref[0:1, :] = bucket.astype(F32)
    rt_ref[1:2, :] = rank
    wt = jnp.concatenate([w_lo, w_hi, jnp.zeros((LANE - 2, tm), F32)], axis=0)
    xh_ref[:, HALF_D:] = lax.bitcast_convert_type(wt.T, jnp.uint32)


def _router(x, g, mod, w_router, b_router):
    tm = ROUTE_TILE
    wr = _pad2(w_router, D_MODEL, LANE).astype(BF16)
    return pl.pallas_call(
        _router_kernel,
        grid=(T // tm,),
        in_specs=[
            pl.BlockSpec((tm, D_MODEL), lambda i: (i, 0)),
            pl.BlockSpec((1, D_MODEL), lambda i: (0, 0)),
            _mod_spec(tm, 3),
            _mod_spec(tm, 4),
            pl.BlockSpec((D_MODEL, LANE), lambda i: (0, 0)),
            pl.BlockSpec((N_EXPERTS, 1), lambda i: (0, 0)),
        ],
        out_specs=[
            pl.BlockSpec((tm, XH_W), lambda i: (i, 0)),
            pl.BlockSpec((8, tm), lambda i: (0, i)),
            pl.BlockSpec((ROUTE_ROWS, LANE), lambda i: (0, 0)),
        ],
        out_shape=[
            jax.ShapeDtypeStruct((T, XH_W), jnp.uint32),
            jax.ShapeDtypeStruct((8, T), F32),
            jax.ShapeDtypeStruct((ROUTE_ROWS, LANE), F32),
        ],
        scratch_shapes=[pltpu.VMEM((ROUTE_ROWS, 1), F32)],
        compiler_params=_cp(("arbitrary",)),
        name="moe_router",
    )(x, g, mod, mod, wr, b_router.reshape(N_EXPERTS, 1))


DISPATCH_TILE = 256
COMBINE_AHEAD = 2


DMA_UNROLL = 32


def _invert_kernel(dest_ref, src_ref):
    def clear(s, c):
        src_ref[s] = 0
        return c

    def put(t, c):
        src_ref[dest_ref[t]] = t
        return c

    lax.fori_loop(0, T_PAD, clear, 0, unroll=DMA_UNROLL)
    lax.fori_loop(0, T, put, 0, unroll=DMA_UNROLL)


def _invert(dest):
    return pl.pallas_call(
        _invert_kernel,
        in_specs=[pl.BlockSpec(memory_space=pltpu.SMEM)],
        out_specs=pl.BlockSpec(memory_space=pltpu.SMEM),
        out_shape=jax.ShapeDtypeStruct((T_PAD,), jnp.int32),
        name="moe_invert",
    )(dest)


def _gather_rows(idx_ref, base, src_hbm, buf, sem, tm, static=False):
    def start(r, c):
        pltpu.make_async_copy(src_hbm.at[pl.ds(idx_ref[base + r], 1)], buf.at[pl.ds(r, 1)], sem).start()
        return c

    if static:
        for r in range(tm):
            start(r, 0)
    else:
        lax.fori_loop(0, tm, start, 0, unroll=DMA_UNROLL)


def _wait_rows(src_hbm, buf, sem, tm):
    pltpu.make_async_copy(src_hbm.at[pl.ds(0, tm)], buf, sem).wait()


def _expert_kernel(ea_ref, eb_ref, nv_ref, src_ref, xh_hbm, ga_ref, ua_ref, da_ref, gb_ref, ub_ref, db_ref, y_ref,
                   *scratch):
    tm = MOE_TILE
    j = pl.program_id(0)
    nv = nv_ref[j]
    *bufs, sems = scratch
    n_buf = len(bufs)

    @pl.when(j == 0)
    def _():
        for t in range(MOE_AHEAD):
            _gather_rows(src_ref, t * tm, xh_hbm, bufs[t], sems.at[t], tm)

    def run(p):
        cur, cur_sem = bufs[p], sems.at[p]
        q = (p + MOE_AHEAD) % n_buf
        ahead, ahead_sem = bufs[q], sems.at[q]

        @pl.when(jnp.logical_or(j < MOE_AHEAD, nv_ref[jnp.maximum(j - MOE_AHEAD, 0)] > 0))
        def _():
            _wait_rows(xh_hbm, cur, cur_sem, tm)

        @pl.when(nv > 0)
        def _():
            valid = lax.broadcasted_iota(jnp.int32, (tm, 1), 0) < nv
            x = jnp.where(valid, _unpack_bf16_pairs(cur[:, :HALF_D]), 0.0).astype(BF16)
            wts = jnp.where(valid, lax.bitcast_convert_type(cur[:, HALF_D:], F32), 0.0)
            for r in range(tm):
                pltpu.make_async_copy(xh_hbm.at[pl.ds(src_ref[(j + MOE_AHEAD) * tm + r], 1)],
                                      ahead.at[pl.ds(r, 1)], ahead_sem).start()

            def ffn(g_ref, u_ref, d_ref, w):
                a = _dot(x, g_ref[...])
                h = a * _sigmoid(a) * _dot(x, u_ref[...]) * w
                return _dot(h.astype(BF16), d_ref[...])

            y = ffn(ga_ref, ua_ref, da_ref, wts[:, 0:1]) + ffn(gb_ref, ub_ref, db_ref, wts[:, 1:2])
            y_ref[...] = _pack_bf16_pairs(y)

    for p in range(n_buf):
        pl.when(j % n_buf == p)(functools.partial(run, p))

    @pl.when(nv == 0)
    def _():
        y_ref[...] = jnp.zeros_like(y_ref)


def _experts(layer, tile_ea, tile_eb, tile_nv, src, xh, w_gate, w_up, w_down):
    tm = MOE_TILE
    up = lambda sel: pl.BlockSpec((None, None, D_MODEL, D_EXPERT),
                                  lambda j, ea, eb, nv, sr: (layer, (ea, eb)[sel][j], 0, 0))
    down = lambda sel: pl.BlockSpec((None, None, D_EXPERT, D_MODEL),
                                    lambda j, ea, eb, nv, sr: (layer, (ea, eb)[sel][j], 0, 0))
    return pl.pallas_call(
        _expert_kernel,
        grid_spec=pltpu.PrefetchScalarGridSpec(
            num_scalar_prefetch=4,
            grid=(MOE_TILES,),
            in_specs=[pl.BlockSpec(memory_space=pl.ANY), up(0), up(0), down(0), up(1), up(1), down(1)],
            out_specs=pl.BlockSpec((tm, HALF_D), lambda j, ea, eb, nv, sr: (j, 0)),
            scratch_shapes=[pltpu.VMEM((tm, XH_W), jnp.uint32)] * (MOE_AHEAD + 1)
            + [pltpu.SemaphoreType.DMA((MOE_AHEAD + 1,))],
        ),
        out_shape=jax.ShapeDtypeStruct((T_PAD, HALF_D), jnp.uint32),
        compiler_params=_cp(("arbitrary",), 56),
        name="moe_experts",
    )(tile_ea, tile_eb, tile_nv, src, xh, w_gate, w_up, w_down, w_gate, w_up, w_down)


def _combine_kernel(*refs, final):
    tm = DISPATCH_TILE
    if final:
        dest_ref, x_ref, gt_ref, ys_hbm, fg_ref, oc_ref, ol_ref, buf, sems = refs
    else:
        dest_ref, x_ref, gt_ref, ys_hbm, o_ref, buf, sems = refs
    i = pl.program_id(0)
    n_buf = COMBINE_AHEAD + 1
    slot = i % n_buf
    ahead = (i + COMBINE_AHEAD) % n_buf

    last = pl.num_programs(0) - 1

    @pl.when(i == 0)
    def _():
        for t in range(COMBINE_AHEAD):
            _gather_rows(dest_ref, t * tm, ys_hbm, buf.at[t], sems.at[t], tm)

    def finish():
        _wait_rows(ys_hbm, buf.at[slot], sems.at[slot], tm)
        x = x_ref[...] + gt_ref[...] * _unpack_bf16_pairs(buf[slot])
        if not final:
            o_ref[...] = x
            return
        y = x * lax.rsqrt(jnp.mean(x * x, axis=-1, keepdims=True) + NORM_EPS) * fg_ref[...]
        is_ctx = i < TP // tm

        @pl.when(is_ctx)
        def _():
            oc_ref[...] = y

        @pl.when(jnp.logical_not(is_ctx))
        def _():
            ol_ref[...] = y

    @pl.when(i + COMBINE_AHEAD <= last)
    def _():
        _gather_rows(dest_ref, (i + COMBINE_AHEAD) * tm, ys_hbm, buf.at[ahead], sems.at[ahead], tm, static=True)
        finish()

    pl.when(i + COMBINE_AHEAD > last)(finish)


def _combine(dest, x, mod, ys, final_gain=None):
    tm = DISPATCH_TILE
    final = final_gain is not None
    n_ctx = TP // tm
    in_specs = [pl.BlockSpec((tm, D_MODEL), lambda i, d: (i, 0)),
                pl.BlockSpec((None, None, 1, D_MODEL), lambda i, d: (_cond_row(i * tm), 5, 0, 0)),
                pl.BlockSpec(memory_space=pl.ANY)]
    args = [dest, x, mod, ys]
    if final:
        in_specs.append(pl.BlockSpec((1, D_MODEL), lambda i, d: (0, 0)))
        args.append(final_gain.reshape(1, D_MODEL))
        out_specs = [pl.BlockSpec((tm, D_MODEL), lambda i, d: (jnp.minimum(i, n_ctx - 1), 0)),
                     pl.BlockSpec((tm, D_MODEL), lambda i, d: (jnp.maximum(i - n_ctx, 0), 0))]
        out_shape = [jax.ShapeDtypeStruct((TP, D_MODEL), F32), jax.ShapeDtypeStruct((TS, D_MODEL), F32)]
    else:
        out_specs = pl.BlockSpec((tm, D_MODEL), lambda i, d: (i, 0))
        out_shape = jax.ShapeDtypeStruct((T, D_MODEL), F32)
    return pl.pallas_call(
        functools.partial(_combine_kernel, final=final),
        grid_spec=pltpu.PrefetchScalarGridSpec(
            num_scalar_prefetch=1,
            grid=(T // tm,),
            in_specs=in_specs,
            out_specs=out_specs,
            scratch_shapes=[pltpu.VMEM((COMBINE_AHEAD + 1, tm, HALF_D), jnp.uint32),
                            pltpu.SemaphoreType.DMA((COMBINE_AHEAD + 1,))],
        ),
        out_shape=out_shape,
        compiler_params=_cp(("arbitrary",)),
        name="moe_combine",
    )(*args)


def _lookup(table, idx):
    n = table.shape[0]
    hit = idx[:, None] == jnp.arange(n, dtype=jnp.int32)[None, :]
    return jnp.sum(jnp.where(hit, table[None, :], 0), axis=1)


def _moe_plan(rt, cnt):
    bucket = rt[0].astype(jnp.int32)
    rank = rt[1].astype(jnp.int32)
    counts = cnt[:N_BUCKETS, 0].astype(jnp.int32)
    tiles = (counts + MOE_TILE - 1) // MOE_TILE
    order = jnp.arange(N_BUCKETS, dtype=jnp.int32)
    tile_start = jnp.sum(jnp.where(order[None, :] < order[:, None], tiles[None, :], 0), axis=1)
    tile_end = tile_start + tiles
    n_used = tile_end[N_BUCKETS - 1]
    dest = _lookup(tile_start * MOE_TILE, bucket) + rank
    j = jnp.arange(MOE_TILES, dtype=jnp.int32)
    jc = jnp.minimum(j, n_used - 1)
    b = jnp.minimum(jnp.sum((jc[:, None] >= tile_end[None, :]).astype(jnp.int32), axis=1), N_BUCKETS - 1)
    nv = jnp.clip(_lookup(counts, b) - (j - _lookup(tile_start, b)) * MOE_TILE, 0, MOE_TILE)
    nv = jnp.where(j < n_used, nv, 0)
    n_pairs = len(PAIR_LO)
    ea = (b // n_pairs) * EXPERTS_PER_GROUP + _lookup(jnp.asarray(PAIR_LO, jnp.int32), b % n_pairs)
    eb = (b // n_pairs) * EXPERTS_PER_GROUP + _lookup(jnp.asarray(PAIR_HI, jnp.int32), b % n_pairs)
    return dest, ea, eb, nv


def _moe(layer, x, g, mod, w_router, b_router, w_gate, w_up, w_down, final_gain=None):
    xh, rt, cnt = _router(x, g, mod, w_router, b_router)
    dest, ea, eb, nv = _moe_plan(rt, cnt)
    ys = _experts(layer, ea, eb, nv, _invert(dest), xh, w_gate, w_up, w_down)
    return _combine(dest, x, mod, ys, final_gain)


def kernel(x_prompt, x_sample, cache_k_full, cache_v_full, cache_k_win, cache_v_win, c, c_ctx, w_mod, b_mod, norm_mix, norm_ffn, final_norm, pool_w, pool_scale, hy_w_in, hy_b_in, hy_conv_w, hy_conv_b, hy_f_w1, hy_f_b1, hy_f_w2, hy_f_b2, hy_f_freq, hy_f_w3, hy_decay, hy_skip, hy_w_out, hy_b_out, fa_w_qkv, fa_q_norm, fa_k_norm, fa_w_o, wa_w_qkv, wa_sink, wa_w_o, w_router, b_router, moe_w_gate, moe_w_up, moe_w_down):
    x = None
    cond =jnp.concatenate([c_ctx[None, :], c, jnp.zeros((N_COND - 1 - DEC_BATCH, D_MODEL), F32)], axis=0)
    mods = _adaln(cond, w_mod, b_mod).reshape(DEPTH, N_COND, 6, 1, D_MODEL)
    rope = _rope_tables()
    ones_hd = jnp.ones((HEAD_DIM,), F32)
    wg_bf, wu_bf, wd_bf = moe_w_gate.astype(BF16), moe_w_up.astype(BF16), moe_w_down.astype(BF16)
    new_kv = {}
    for layer in range(DEPTH):
        kind = layer % 4
        j = layer // 4
        mod = mods[layer]
        g_mix = norm_mix[layer].reshape(1, D_MODEL)
        if kind == 0:
            assert layer == 0, "the pooling mixer reads the two input streams, so it must be the first layer"
            x = _pool_mixer(x_prompt.reshape(TP, D_MODEL), x_sample.reshape(TS, D_MODEL), g_mix, mod,
                            pool_w[j], pool_scale[j])
        elif kind == 1:
            x = _hyena_mixer(x, g_mix, mod, hy_w_in[j], hy_b_in[j], hy_conv_w[j], hy_conv_b[j], hy_f_w1[j],
                             hy_f_b1[j], hy_f_w2[j], hy_f_b2[j], hy_f_freq[j], hy_f_w3[j], hy_decay[j],
                             hy_skip[j], hy_w_out[j], hy_b_out[j])
        elif kind == 2:
            x, nk, nv = _attn_mixer(x, g_mix, mod, fa_w_qkv[j], fa_q_norm[j], fa_k_norm[j], True, None,
                                    fa_w_o[j], cache_k_full[:, j], cache_v_full[:, j], False, rope)
            new_kv.setdefault("kf", []).append(nk)
            new_kv.setdefault("vf", []).append(nv)
        else:
            x, nk, nv = _attn_mixer(x, g_mix, mod, wa_w_qkv[j], ones_hd, ones_hd, False, wa_sink[j],
                                    wa_w_o[j], cache_k_win[:, j], cache_v_win[:, j], True, rope)
            new_kv.setdefault("kw", []).append(nk)
            new_kv.setdefault("vw", []).append(nv)
        x = _moe(layer, x, norm_ffn[layer].reshape(1, D_MODEL), mod, w_router, b_router, wg_bf, wu_bf, wd_bf,
                 final_norm if layer == DEPTH - 1 else None)
    y_prompt, y_sample = x
    y_prompt = y_prompt.reshape(BATCH, SEQ, D_MODEL)
    y_sample = y_sample.reshape(DEC_BATCH, DEC_SEQ, D_MODEL)
    return (y_prompt, y_sample, jnp.stack(new_kv["kf"], axis=1), jnp.stack(new_kv["vf"], axis=1),
            jnp.stack(new_kv["kw"], axis=1), jnp.stack(new_kv["vw"], axis=1))
```

```python
import functools
import math

import jax
import jax.numpy as jnp
import numpy as np
from jax import lax
from jax.experimental import pallas as pl
from jax.experimental.pallas import tpu as pltpu

D_MODEL = 2048
BATCH = 32
SEQ = 256
DEPTH = 4
DEC_BATCH = 4
DEC_SEQ = 4096
PAST_LEN = 512
GRID_W = 64
N_HEADS = 16
N_KV_HEADS = 4
HEAD_DIM = D_MODEL // N_HEADS
KV_GROUP = N_HEADS // N_KV_HEADS
KV_DIM = N_KV_HEADS * HEAD_DIM
QKV_DIM = (N_HEADS + 2 * N_KV_HEADS) * HEAD_DIM
ROPE_THETA = 10000.0
WINDOW = 128
POOL_WINDOWS = (2, 4, 8, 16)
POOL_GROUP = D_MODEL // len(POOL_WINDOWS)
HYENA_EMB_BANDS = 16
HYENA_FILTER_HIDDEN = 64
N_EXPERTS = 16
N_EXPERT_GROUPS = 4
EXPERTS_PER_GROUP = 4
D_EXPERT = 512
NORM_EPS = 1e-6
NEG_INF = -1e30

F32 = jnp.float32
BF16 = jnp.bfloat16

TP = BATCH * SEQ
TS = DEC_BATCH * DEC_SEQ
T = TP + TS
N_COND = 8
LANE = 128
MIB = 1024 * 1024

PAIR_LO = (0, 0, 0, 1, 1, 2)
PAIR_HI = (1, 2, 3, 2, 3, 3)
N_BUCKETS = N_EXPERT_GROUPS * len(PAIR_LO)
MOE_TILE = 256
MOE_AHEAD = 3
MOE_TILES = T // MOE_TILE + N_BUCKETS + MOE_AHEAD
T_PAD = MOE_TILES * MOE_TILE
HALF_D = D_MODEL // 2
Y_TILES = HALF_D // LANE
XH_W = HALF_D + LANE


def _cp(sem, vmem_mb=48):
    return pltpu.CompilerParams(dimension_semantics=sem, vmem_limit_bytes=vmem_mb * MIB)


def _dot(a, b):
    return jnp.dot(a, b, preferred_element_type=F32)


def _dot3(a, b):
    ah = a.astype(BF16)
    al = (a - ah.astype(F32)).astype(BF16)
    bh = b.astype(BF16)
    bl = (b - bh.astype(F32)).astype(BF16)
    return _dot(ah, bh) + (_dot(al, bh) + _dot(ah, bl))


def _sigmoid(x):
    return 1.0 / (1.0 + jnp.exp(-x))


def _pack_bf16_pairs(x):
    n = x.shape[1] // 2
    bits = lambda v: lax.bitcast_convert_type(v.astype(BF16).astype(F32), jnp.uint32)
    return (bits(x[:, :n]) >> 16) | bits(x[:, n:])


def _unpack_bf16_pairs(u):
    lo = lax.bitcast_convert_type(u << 16, F32)
    hi = lax.bitcast_convert_type(u & jnp.uint32(0xFFFF0000), F32)
    return jnp.concatenate([lo, hi], axis=1)


def _cond_row(r):
    return jnp.where(r < TP, 0, 1 + (r - TP) // DEC_SEQ)


def _mod_spec(tm, chunk, tn=D_MODEL, ncol=False):
    if ncol:
        return pl.BlockSpec((None, None, 1, tn), lambda i, j: (_cond_row(i * tm), chunk, 0, j))
    return pl.BlockSpec((None, None, 1, tn), lambda i, *_: (_cond_row(i * tm), chunk, 0, 0))


def _norm_mod(x, g, shift, scale):
    var = jnp.mean(x * x, axis=-1, keepdims=True)
    y = x * lax.rsqrt(var + NORM_EPS) * g
    return y * (1.0 + scale) + shift


def _adaln_kernel(c_ref, w_ref, b_ref, o_ref):
    c = c_ref[...]
    a = c * _sigmoid(c)
    o_ref[...] = _dot3(a, w_ref[...]) + b_ref[...]


def _adaln(cond, w_mod, b_mod):
    tn = 1024
    n = 6 * D_MODEL
    return pl.pallas_call(
        _adaln_kernel,
        grid=(DEPTH, n // tn),
        in_specs=[
            pl.BlockSpec((N_COND, D_MODEL), lambda l, j: (0, 0)),
            pl.BlockSpec((None, D_MODEL, tn), lambda l, j: (l, 0, j)),
            pl.BlockSpec((None, 1, tn), lambda l, j: (l, 0, j)),
        ],
        out_specs=pl.BlockSpec((None, N_COND, tn), lambda l, j: (l, 0, j)),
        out_shape=jax.ShapeDtypeStruct((DEPTH, N_COND, n), F32),
        compiler_params=_cp(("parallel", "parallel")),
        name="adaln",
    )(cond, w_mod, b_mod.reshape(DEPTH, 1, n))


def _nm_matmul_kernel(x_ref, g_ref, sh_ref, sc_ref, w_ref, b_ref, o_ref, h_scr):
    @pl.when(pl.program_id(1) == 0)
    def _():
        h_scr[...] = _norm_mod(x_ref[...], g_ref[...], sh_ref[...], sc_ref[...]).astype(BF16)

    o_ref[...] = (_dot(h_scr[...], w_ref[...]) + b_ref[...]).astype(o_ref.dtype)


def _nm_matmul(x, g, mod, w, b, out_dtype, name):
    tm, tn = 1024, 1024
    n = w.shape[1]
    return pl.pallas_call(
        _nm_matmul_kernel,
        grid=(T // tm, n // tn),
        in_specs=[
            pl.BlockSpec((tm, D_MODEL), lambda i, j: (i, 0)),
            pl.BlockSpec((1, D_MODEL), lambda i, j: (0, 0)),
            _mod_spec(tm, 0),
            _mod_spec(tm, 1),
            pl.BlockSpec((D_MODEL, tn), lambda i, j: (0, j)),
            pl.BlockSpec((1, tn), lambda i, j: (0, j)),
        ],
        out_specs=pl.BlockSpec((tm, tn), lambda i, j: (i, j)),
        out_shape=jax.ShapeDtypeStruct((T, n), out_dtype),
        scratch_shapes=[pltpu.VMEM((tm, D_MODEL), BF16)],
        compiler_params=_cp(("parallel", "arbitrary")),
        name=name,
    )(x, g, mod, mod, w, b)


RESID_TM = 1024


def _resid_matmul_kernel(ap_ref, as_ref, w_ref, b_ref, x_ref, gt_ref, o_ref):
    def emit(a_ref):
        o_ref[...] = x_ref[...] + gt_ref[...] * (_dot(a_ref[...], w_ref[...]) + b_ref[...])

    is_ctx = pl.program_id(0) < TP // RESID_TM
    pl.when(is_ctx)(lambda: emit(ap_ref))
    pl.when(jnp.logical_not(is_ctx))(lambda: emit(as_ref))


def _resid_matmul(a_ctx, a_lat, w, b, x, mod, name):
    tm, tn = RESID_TM, 1024
    k = a_ctx.shape[1]
    n_ctx = TP // tm
    return pl.pallas_call(
        _resid_matmul_kernel,
        grid=(T // tm, D_MODEL // tn),
        in_specs=[
            pl.BlockSpec((tm, k), lambda i, j: (jnp.minimum(i, n_ctx - 1), 0)),
            pl.BlockSpec((tm, k), lambda i, j: (jnp.maximum(i - n_ctx, 0), 0)),
            pl.BlockSpec((k, tn), lambda i, j: (0, j)),
            pl.BlockSpec((1, tn), lambda i, j: (0, j)),
            pl.BlockSpec((tm, tn), lambda i, j: (i, j)),
            _mod_spec(tm, 2, tn, ncol=True),
        ],
        out_specs=pl.BlockSpec((tm, tn), lambda i, j: (i, j)),
        out_shape=jax.ShapeDtypeStruct((T, D_MODEL), F32),
        compiler_params=_cp(("parallel", "parallel")),
        name=name,
    )(a_ctx, a_lat, w, b, x, mod)


POOL_TILE = 256
POOL_HALO = 8


def _seq_pos(r0):
    is_ctx = r0 < TP
    loc0 = jnp.where(is_ctx, r0 % SEQ, (r0 - TP) % DEC_SEQ)
    seq_len = jnp.where(is_ctx, SEQ, DEC_SEQ)
    return loc0, seq_len


def _pool_kernel(xc_ref, xcp_ref, xcn_ref, xl_ref, xlp_ref, xln_ref, *rest):
    is_ctx = pl.program_id(0) < TP // POOL_TILE
    pl.when(is_ctx)(lambda: _pool_tile(xc_ref, xcp_ref, xcn_ref, *rest))
    pl.when(jnp.logical_not(is_ctx))(lambda: _pool_tile(xl_ref, xlp_ref, xln_ref, *rest))


def _pool_tile(x_ref, xp_ref, xn_ref, g_ref, sh_ref, sc_ref, gt_ref, pw_ref, ps_ref, o_ref, hz_scr):
    tm, hl = POOL_TILE, POOL_HALO
    loc0, seq_len = _seq_pos(pl.program_id(0) * tm)
    has_prev = loc0 > 0
    has_next = loc0 + tm < seq_len
    g, sh, sc = g_ref[...], sh_ref[...], sc_ref[...]
    x = x_ref[...]
    h = _norm_mod(x, g, sh, sc)
    hz_scr[0:hl, :] = jnp.where(has_prev, _norm_mod(xp_ref[...], g, sh, sc), 0.0)
    hz_scr[hl:hl + tm, :] = h
    hz_scr[hl + tm:, :] = jnp.where(has_next, _norm_mod(xn_ref[...], g, sh, sc), 0.0)
    tl = loc0 + lax.broadcasted_iota(jnp.int32, (tm, 1), 0)
    outs = []
    for gi, w in enumerate(POOL_WINDOWS):
        cs = slice(gi * POOL_GROUP, (gi + 1) * POOL_GROUP)
        s = jnp.zeros((tm, POOL_GROUP), F32)
        for off in range(-(w // 2), w - w // 2):
            s = s + hz_scr[hl + off:hl + off + tm, cs]
        lo = jnp.maximum(tl - w // 2, 0)
        hi = jnp.minimum(tl + (w - w // 2), seq_len)
        d = s / (hi - lo).astype(F32) - h[:, cs]
        outs.append(_dot(d.astype(BF16), pw_ref[gi]))
    out = jnp.concatenate(outs, axis=1) * ps_ref[...]
    o_ref[...] = x + gt_ref[...] * out


def _pool_mixer(x_ctx, x_lat, g, mod, pool_w, pool_scale):
    tm, hl = POOL_TILE, POOL_HALO
    r = tm // hl

    def stream(first_tile, rows):
        tile = lambda i: jnp.clip(i - first_tile, 0, rows // tm - 1)
        return [pl.BlockSpec((tm, D_MODEL), lambda i: (tile(i), 0)),
                pl.BlockSpec((hl, D_MODEL), lambda i: (jnp.maximum(tile(i) * r - 1, 0), 0)),
                pl.BlockSpec((hl, D_MODEL), lambda i: (jnp.minimum((tile(i) + 1) * r, rows // hl - 1), 0))]

    return pl.pallas_call(
        _pool_kernel,
        grid=(T // tm,),
        in_specs=stream(0, TP) + stream(TP // tm, TS) + [
            pl.BlockSpec((1, D_MODEL), lambda i: (0, 0)),
            _mod_spec(tm, 0),
            _mod_spec(tm, 1),
            _mod_spec(tm, 2),
            pl.BlockSpec((len(POOL_WINDOWS), POOL_GROUP, POOL_GROUP), lambda i: (0, 0, 0)),
            pl.BlockSpec((1, D_MODEL), lambda i: (0, 0)),
        ],
        out_specs=pl.BlockSpec((tm, D_MODEL), lambda i: (i, 0)),
        out_shape=jax.ShapeDtypeStruct((T, D_MODEL), F32),
        scratch_shapes=[pltpu.VMEM((tm + 2 * hl, D_MODEL), F32)],
        compiler_params=_cp(("parallel",)),
        name="pool_mixer",
    )(x_ctx, x_ctx, x_ctx, x_lat, x_lat, x_lat, g, mod, mod, mod, pool_w.astype(BF16),
      pool_scale.reshape(1, D_MODEL))


CONV_TILE = 256
CONV_HALO = 16


def _conv3_kernel(u_ref, up_ref, un_ref, cw_ref, cb_ref, o_ref, scr):
    tm, hl = CONV_TILE, CONV_HALO
    loc0, seq_len = _seq_pos(pl.program_id(0) * tm)
    has_prev = loc0 > 0
    has_next = loc0 + tm < seq_len
    scr[0:hl, :] = jnp.where(has_prev, up_ref[...].astype(F32), 0.0)
    scr[hl:hl + tm, :] = u_ref[...].astype(F32)
    scr[hl + tm:, :] = jnp.where(has_next, un_ref[...].astype(F32), 0.0)
    out = (scr[hl - 1:hl - 1 + tm, :] * cw_ref[0:1, :] + scr[hl:hl + tm, :] * cw_ref[1:2, :]
           + scr[hl + 1:hl + 1 + tm, :] * cw_ref[2:3, :] + cb_ref[...])
    o_ref[...] = out.astype(o_ref.dtype)


def _conv3(u0, conv_w, conv_b):
    tm, hl, tc = CONV_TILE, CONV_HALO, D_MODEL
    r = tm // hl
    n = u0.shape[1]
    return pl.pallas_call(
        _conv3_kernel,
        grid=(T // tm, n // tc),
        in_specs=[
            pl.BlockSpec((tm, tc), lambda i, j: (i, j)),
            pl.BlockSpec((hl, tc), lambda i, j: (jnp.maximum(i * r - 1, 0), j)),
            pl.BlockSpec((hl, tc), lambda i, j: (jnp.minimum((i + 1) * r, T // hl - 1), j)),
            pl.BlockSpec((3, tc), lambda i, j: (0, j)),
            pl.BlockSpec((1, tc), lambda i, j: (0, j)),
        ],
        out_specs=pl.BlockSpec((tm, tc), lambda i, j: (i, j)),
        out_shape=jax.ShapeDtypeStruct((T, n), BF16),
        scratch_shapes=[pltpu.VMEM((tm + 2 * hl, tc), F32)],
        compiler_params=_cp(("parallel", "parallel")),
        name="hyena_conv3",
    )(u0, u0, u0, conv_w, conv_b.reshape(1, n))


FILT_TILE = 256


HYENA_BLOCK = 1024


T_LANE = LANE - 1


def _filter_mlp_kernel(emb_ref, w1_ref, b1_ref, w2_ref, b2_ref, fr_ref, o_ref):
    emb = emb_ref[...]
    fr = fr_ref[...]
    a = jnp.sin(fr * (_dot3(emb, w1_ref[...]) + b1_ref[...]))
    a = jnp.sin(fr * (_dot3(a, w2_ref[...]) + b2_ref[...]))
    lane = lax.broadcasted_iota(jnp.int32, a.shape, 1)
    o_ref[...] = jnp.where(lane == T_LANE, emb[:, 0:1], a)


def _filter_mlp(pos, L, f_w1, f_b1, f_w2, f_b2, f_freq):
    assert HYENA_FILTER_HIDDEN <= T_LANE
    tl = FILT_TILE
    rows = pos.shape[0]
    small = lambda: pl.BlockSpec((LANE, LANE), lambda i: (0, 0))
    vec = lambda: pl.BlockSpec((1, LANE), lambda i: (0, 0))
    return pl.pallas_call(
        _filter_mlp_kernel,
        grid=(rows // tl,),
        in_specs=[pl.BlockSpec((tl, LANE), lambda i: (i, 0)), small(), vec(), small(), vec(), vec()],
        out_specs=pl.BlockSpec((tl, LANE), lambda i: (i, 0)),
        out_shape=jax.ShapeDtypeStruct((rows, LANE), F32),
        compiler_params=_cp(("parallel",)),
        name="hyena_filter_mlp",
    )(_filter_embedding(pos, L), _pad2(f_w1, LANE, LANE), _pad2(f_b1[None], 1, LANE),
      _pad2(f_w2, LANE, LANE), _pad2(f_b2[None], 1, LANE), _pad2(f_freq[None], 1, LANE))


def _filter_kernel(h1_ref, h2_ref, w3a_ref, dca_ref, w3b_ref, dcb_ref, fa_ref, fb_ref, *, blk):
    def taps(h, w3_ref, dc_ref):
        return _dot3(h, w3_ref[...]) * jnp.exp(-h[:, T_LANE:] * jnp.abs(dc_ref[...]))

    pos = taps(h1_ref[...], w3a_ref, dca_ref)
    neg = taps(h2_ref[...], w3b_ref, dcb_ref)
    m = (pl.program_id(0) * FILT_TILE + lax.broadcasted_iota(jnp.int32, (FILT_TILE, 1), 0)) % blk
    fa_ref[...] = jnp.where(m == 0, pos, pos + neg).astype(BF16)
    fb_ref[...] = jnp.where(m == 0, 0.0, neg - pos).astype(BF16)


def _pad2(a, rows, cols):
    return jnp.pad(a, ((0, rows - a.shape[0]), (0, cols - a.shape[1])))


def _filter_positions(L, blk):
    n_blk = L // blk
    m = np.arange(blk)
    p1, p2 = [], []
    for d in range(-(n_blk - 1), n_blk):
        if d >= 1:
            p1.append(d * blk + m), p2.append(d * blk - m)
        elif d == 0:
            p1.append(m), p2.append(m)
        else:
            p1.append(-d * blk - m), p2.append(-d * blk + m)
    return np.concatenate(p1), np.concatenate(p2)


def _filter_embedding(pos, L):
    t = jnp.asarray(pos, F32) / L
    bands = jnp.linspace(1e-4, HYENA_EMB_BANDS - 1, HYENA_EMB_BANDS, dtype=F32)
    ang = (2 * math.pi) * t[:, None] * bands[None, :]
    return _pad2(jnp.concatenate([t[:, None], jnp.cos(ang), -jnp.sin(ang)], axis=-1), pos.shape[0], LANE)


def _hyena_filters(L, blk, f_w1, f_b1, f_w2, f_b2, f_freq, f_w3, decay):
    n_blk = L // blk
    p1, p2 = _filter_positions(L, blk)
    rows = p1.shape[0]
    tl = FILT_TILE
    tiles_per_lag = blk // tl
    lag = lambda i: i // tiles_per_lag - (n_blk - 1)
    col1 = lambda i, n: 2 * n + jnp.where(lag(i) >= 0, 0, 1)
    col2 = lambda i, n: 2 * n + jnp.where(lag(i) >= 1, 0, 1)
    n_tiles = rows // tl
    out = pl.BlockSpec((None, tl, D_MODEL), lambda i, n: (n, i, 0))
    w3 = _pad2(f_w3, LANE, f_w3.shape[1])
    hidden = _filter_mlp(np.concatenate([p1, p2]), L, f_w1, f_b1, f_w2, f_b2, f_freq)
    return pl.pallas_call(
        functools.partial(_filter_kernel, blk=blk),
        grid=(n_tiles, 2),
        in_specs=[
            pl.BlockSpec((tl, LANE), lambda i, n: (i, 0)),
            pl.BlockSpec((tl, LANE), lambda i, n: (n_tiles + i, 0)),
            pl.BlockSpec((LANE, D_MODEL), lambda i, n: (0, col1(i, n))),
            pl.BlockSpec((1, D_MODEL), lambda i, n: (0, col1(i, n))),
            pl.BlockSpec((LANE, D_MODEL), lambda i, n: (0, col2(i, n))),
            pl.BlockSpec((1, D_MODEL), lambda i, n: (0, col2(i, n))),
        ],
        out_specs=[out, out],
        out_shape=[jax.ShapeDtypeStruct((2, rows, D_MODEL), BF16)] * 2,
        compiler_params=_cp(("parallel", "parallel")),
        name="hyena_filters",
    )(hidden, hidden, w3, decay[None], w3, decay[None])


def _dft_mats(L):
    r = int(math.isqrt(L))
    k2 = 2 * jnp.arange(L, dtype=jnp.int32)[:, None] + 1
    n1 = r * jnp.arange(L // r, dtype=jnp.int32)[None, :]
    n2 = jnp.arange(r, dtype=jnp.int32)[None, :]
    sc = math.pi / (2 * L)
    aa = ((k2 * n1) % (4 * L)).astype(F32) * sc
    ab = ((k2 * n2) % (4 * L)).astype(F32) * sc
    ca, sa, cb, sb = jnp.cos(aa)[:, :, None], jnp.sin(aa)[:, :, None], jnp.cos(ab)[:, None, :], jnp.sin(ab)[:, None, :]
    c = (ca * cb - sa * sb).reshape(L, L)
    s = (sa * cb + ca * sb).reshape(L, L)
    return c.astype(BF16), s.astype(BF16), c.T.astype(BF16), s.T.astype(BF16)


def _dft_tiles(L):
    return min(512, L), 512


def _dft_filter_kernel(c_ref, s_ref, a_ref, b_ref, gr_ref, gi_ref):
    gr_ref[...] = _dot(c_ref[...], a_ref[...]).astype(gr_ref.dtype)
    gi_ref[...] = _dot(s_ref[...], b_ref[...]).astype(gi_ref.dtype)


def _dft_filter(cm, sm, fa, fb, L):
    tf, tn = _dft_tiles(L)
    n = fa.shape[0]
    mat = lambda: pl.BlockSpec((tf, L), lambda k, c, s: (k, 0))
    rhs = lambda: pl.BlockSpec((None, L, tn), lambda k, c, s: (s, 0, c))
    out = pl.BlockSpec((None, tf, tn), lambda k, c, s: (s, k, c))
    return pl.pallas_call(
        _dft_filter_kernel,
        grid=(L // tf, D_MODEL // tn, n),
        in_specs=[mat(), mat(), rhs(), rhs()],
        out_specs=[out, out],
        out_shape=[jax.ShapeDtypeStruct((n, L, D_MODEL), BF16)] * 2,
        compiler_params=_cp(("parallel", "parallel", "parallel")),
        name="hyena_filter_dft",
    )(cm, sm, fa, fb)


FWD_TF = 256


def _dft_fwd_kernel(c_ref, s_ref, z_ref, gr_ref, gi_ref, yr_ref, yi_ref, *, n_blk, blk, bpb):
    c, s = c_ref[...], s_ref[...]
    for bb in range(bpb):
        zc, zs = [], []
        for j in range(n_blk):
            r = (bb * n_blk + j) * blk
            zj = z_ref[r:r + blk, :]
            zc.append(_dot(c, zj).astype(BF16))
            zs.append(_dot(s, zj).astype(BF16))
        for i in range(n_blk):
            yr = yi = None
            for j in range(n_blk):
                lag = i - j + n_blk - 1
                gr, gi = gr_ref[lag], gi_ref[lag]
                tr = gr * zc[j] + gi * zs[j]
                ti = gi * zc[j] - gr * zs[j]
                yr = tr if yr is None else yr + tr
                yi = ti if yi is None else yi + ti
            yr_ref[bb, i] = yr.astype(BF16)
            yi_ref[bb, i] = yi.astype(BF16)


def _seqs_per_step(L):
    return max(1, 2048 // L)


def _dft_fwd(cm, sm, z, z_rowblk, z_colblk, gr, gi, order, nb, L, blk):
    n_blk = L // blk
    bpb = _seqs_per_step(L)
    assert nb % bpb == 0 and z_rowblk % bpb == 0
    tf, tn = min(FWD_TF, blk), 512
    mat = lambda: pl.BlockSpec((tf, blk), lambda k, c, b: (k, 0))
    gsp = lambda: pl.BlockSpec((None, 2 * n_blk - 1, tf, tn), lambda k, c, b: (order, 0, k, c))
    out = pl.BlockSpec((bpb, n_blk, tf, tn), lambda k, c, b: (b, 0, k, c))
    return pl.pallas_call(
        functools.partial(_dft_fwd_kernel, n_blk=n_blk, blk=blk, bpb=bpb),
        grid=(blk // tf, D_MODEL // tn, nb // bpb),
        in_specs=[mat(), mat(),
                  pl.BlockSpec((bpb * L, tn), lambda k, c, b: (z_rowblk // bpb + b, z_colblk + c)),
                  gsp(), gsp()],
        out_specs=[out, out],
        out_shape=[jax.ShapeDtypeStruct((nb, n_blk, blk, D_MODEL), BF16)] * 2,
        compiler_params=_cp(("parallel", "parallel", "parallel")),
        name="hyena_dft_fwd",
    )(cm, sm, z, gr, gi)


def _dft_inv_kernel(ct_ref, st_ref, yr_ref, yi_ref, z_ref, gt_ref, sk_ref, o_ref, *, inv_len, bpb, tt):
    for bb in range(bpb):
        rows = slice(bb * tt, (bb + 1) * tt)
        y = (_dot(ct_ref[...], yr_ref[bb]) - _dot(st_ref[...], yi_ref[bb])) * inv_len
        o_ref[rows, :] = (gt_ref[rows, :].astype(F32) * (y + sk_ref[...] * z_ref[rows, :].astype(F32))).astype(BF16)


def _dft_inv(ctm, stm, yr, yi, z, z_rowblk, z_colblk, gate, g_rowblk, g_colblk, skip, nb, L):
    tt, tn = _dft_tiles(L)
    rpb = L // tt
    bpb = _seqs_per_step(L) if rpb == 1 else 1
    assert nb % bpb == 0 and z_rowblk % bpb == 0 and g_rowblk % bpb == 0
    mat = lambda: pl.BlockSpec((tt, L), lambda t, c, b: (t, 0))
    spec = lambda: pl.BlockSpec((bpb, L, tn), lambda t, c, b: (b, 0, c))
    rows = lambda blk0: (lambda t, c, b: ((blk0 + b * bpb * rpb + t) // bpb))
    return pl.pallas_call(
        functools.partial(_dft_inv_kernel, inv_len=1.0 / L, bpb=bpb, tt=tt),
        grid=(rpb, D_MODEL // tn, nb // bpb),
        in_specs=[mat(), mat(), spec(), spec(),
                  pl.BlockSpec((bpb * tt, tn), lambda t, c, b: (rows(z_rowblk)(t, c, b), z_colblk + c)),
                  pl.BlockSpec((bpb * tt, tn), lambda t, c, b: (rows(g_rowblk)(t, c, b), g_colblk + c)),
                  pl.BlockSpec((1, tn), lambda t, c, b: (0, c))],
        out_specs=pl.BlockSpec((bpb * tt, tn), lambda t, c, b: (rows(0)(t, c, b), c)),
        out_shape=jax.ShapeDtypeStruct((nb * L, D_MODEL), BF16),
        compiler_params=_cp(("parallel", "parallel", "parallel")),
        name="hyena_dft_inv",
    )(ctm, stm, yr, yi, z, gate, skip)


def _hyena_stream(u, row0, nb, L, fparams, skip):
    blk = min(HYENA_BLOCK, L)
    n_blk = L // blk
    n_lag = 2 * n_blk - 1
    cm, sm, ctm, stm = _dft_mats(blk)
    fa, fb = _hyena_filters(L, blk, *fparams)
    seg = lambda a: a.reshape(2 * n_lag, blk, D_MODEL)
    gr, gi = _dft_filter(cm, sm, seg(fa), seg(fb), blk)
    gr, gi = (a.reshape(2, n_lag, blk, D_MODEL) for a in (gr, gi))
    tt, tn = _dft_tiles(blk)
    ncb = D_MODEL // tn
    blocks = lambda a: a.reshape(nb * n_blk, blk, D_MODEL)

    yr, yi = _dft_fwd(cm, sm, u, row0 // L, 0, gr, gi, 0, nb, L, blk)
    z1 = _dft_inv(ctm, stm, blocks(yr), blocks(yi), u, row0 // tt, 0, u, row0 // tt, ncb, skip[0:1],
                  nb * n_blk, blk)
    yr, yi = _dft_fwd(cm, sm, z1, 0, 0, gr, gi, 1, nb, L, blk)
    return _dft_inv(ctm, stm, blocks(yr), blocks(yi), z1, 0, 0, u, row0 // tt, 2 * ncb, skip[1:2],
                    nb * n_blk, blk)


def _hyena_mixer(x, g, mod, w_in, b_in, conv_w, conv_b, f_w1, f_b1, f_w2, f_b2, f_freq, f_w3, decay, skip,
                 w_out, b_out):
    u0 = _nm_matmul(x, g, mod, w_in.astype(BF16), b_in.reshape(1, -1), BF16, "hyena_in_proj")
    u = _conv3(u0, conv_w, conv_b)
    fparams = (f_w1, f_b1, f_w2, f_b2, f_freq, f_w3, decay)
    zp = _hyena_stream(u, 0, BATCH, SEQ, fparams, skip)
    zs = _hyena_stream(u, TP, DEC_BATCH, DEC_SEQ, fparams, skip)
    return _resid_matmul(zp, zs, w_out.astype(BF16), b_out.reshape(1, -1), x, mod, "hyena_out_proj")


def _rope_tables():
    pos = jnp.arange(DEC_SEQ, dtype=jnp.int32)
    row = (pos // GRID_W).astype(F32)
    col = (pos % GRID_W).astype(F32)
    axis_dim = HEAD_DIM // 2
    inv_freq = ROPE_THETA ** (-jnp.arange(0, axis_dim, 2, dtype=F32) / axis_dim)
    ar = row[:, None] * inv_freq[None, :]
    ac = col[:, None] * inv_freq[None, :]
    cos = jnp.concatenate([jnp.cos(ar), jnp.cos(ar), jnp.cos(ac), jnp.cos(ac)], axis=-1)
    sin = jnp.concatenate([-jnp.sin(ar), jnp.sin(ar), -jnp.sin(ac), jnp.sin(ac)], axis=-1)
    return cos, sin


QKV_TM = 512
QKV_TN = 1024
PAIR = 2 * HEAD_DIM


def _qkv_kernel(x_ref, g_ref, sh_ref, sc_ref, w_ref, qn_ref, kn_ref, cos_ref, sin_ref,
                q_ref, k_ref, v_ref, nk_ref, nv_ref, h_scr, *, use_norm):
    tm = QKV_TM
    i, j = pl.program_id(0), pl.program_id(1)

    @pl.when(j == 0)
    def _():
        h_scr[...] = _norm_mod(x_ref[...], g_ref[...], sh_ref[...], sc_ref[...]).astype(BF16)

    quarter = HEAD_DIM // 4
    scale = HEAD_DIM ** -0.5 * LOG2E

    def head(xh, gn):
        if use_norm:
            xh = xh * lax.rsqrt(jnp.mean(xh * xh, axis=-1, keepdims=True) + NORM_EPS) * gn
        return xh

    def rope(xh):
        lane = lax.broadcasted_iota(jnp.int32, (tm, HEAD_DIM), 1)
        first = (lane % (2 * quarter)) < quarter
        partner = jnp.where(first, pltpu.roll(xh, HEAD_DIM - quarter, 1), pltpu.roll(xh, quarter, 1))
        return xh * cos_ref[...] + partner * sin_ref[...]

    def proj(c0):
        return _dot(h_scr[...], w_ref[:, c0:c0 + PAIR])

    def q_tile(latent):
        for p in range(QKV_TN // PAIR):
            acc = proj(p * PAIR)
            for t in range(2):
                xh = head(acc[:, t * HEAD_DIM:(t + 1) * HEAD_DIM], qn_ref[...])
                xh = rope(xh) if latent else xh
                c0 = p * PAIR + t * HEAD_DIM
                q_ref[:, c0:c0 + HEAD_DIM] = (xh * scale).astype(BF16)

    def kv_tile(latent):
        for p in range(KV_DIM // PAIR):
            acc = proj(p * PAIR)
            for t in range(2):
                c0 = p * PAIR + t * HEAD_DIM
                kh = head(acc[:, t * HEAD_DIM:(t + 1) * HEAD_DIM], kn_ref[...])
                if not latent:
                    nk_ref[:, c0:c0 + HEAD_DIM] = kh
                k_ref[:, c0:c0 + HEAD_DIM] = (rope(kh) if latent else kh).astype(BF16)
        for p in range(KV_DIM // PAIR):
            acc = proj(KV_DIM + p * PAIR)
            if not latent:
                nv_ref[:, p * PAIR:(p + 1) * PAIR] = acc
            v_ref[:, p * PAIR:(p + 1) * PAIR] = acc.astype(BF16)

    is_ctx = i < TP // tm
    is_q = j < D_MODEL // QKV_TN
    for latent in (False, True):
        stream = jnp.logical_not(is_ctx) if latent else is_ctx
        pl.when(jnp.logical_and(stream, is_q))(functools.partial(q_tile, latent))
        pl.when(jnp.logical_and(stream, jnp.logical_not(is_q)))(functools.partial(kv_tile, latent))


def _qkv_proj(x, g, mod, w_qkv, q_norm, k_norm, use_norm, rope):
    tm, tn = QKV_TM, QKV_TN
    n_ctx = TP // tm
    n_q = D_MODEL // tn
    tab = lambda: pl.BlockSpec((tm, HEAD_DIM), lambda i, j: (jnp.maximum(i - n_ctx, 0) % (DEC_SEQ // tm), 0))
    kv = lambda: pl.BlockSpec((tm, KV_DIM), lambda i, j: (i, 0))
    new = lambda: pl.BlockSpec((tm, KV_DIM), lambda i, j: (jnp.minimum(i, n_ctx - 1), 0))
    return pl.pallas_call(
        functools.partial(_qkv_kernel, use_norm=use_norm),
        grid=(T // tm, QKV_DIM // tn),
        in_specs=[
            pl.BlockSpec((tm, D_MODEL), lambda i, j: (i, 0)),
            pl.BlockSpec((1, D_MODEL), lambda i, j: (0, 0)),
            _mod_spec(tm, 0),
            _mod_spec(tm, 1),
            pl.BlockSpec((D_MODEL, tn), lambda i, j: (0, j)),
            pl.BlockSpec((1, HEAD_DIM), lambda i, j: (0, 0)),
            pl.BlockSpec((1, HEAD_DIM), lambda i, j: (0, 0)),
            tab(), tab(),
        ],
        out_specs=[pl.BlockSpec((tm, tn), lambda i, j: (i, jnp.minimum(j, n_q - 1))), kv(), kv(), new(), new()],
        out_shape=[jax.ShapeDtypeStruct((T, D_MODEL), BF16), jax.ShapeDtypeStruct((T, KV_DIM), BF16),
                   jax.ShapeDtypeStruct((T, KV_DIM), BF16), jax.ShapeDtypeStruct((TP, KV_DIM), F32),
                   jax.ShapeDtypeStruct((TP, KV_DIM), F32)],
        scratch_shapes=[pltpu.VMEM((tm, D_MODEL), BF16)],
        compiler_params=_cp(("arbitrary", "arbitrary")),
        name="qkv_proj",
    )(x, g, mod, mod, w_qkv, q_norm.reshape(1, HEAD_DIM), k_norm.reshape(1, HEAD_DIM), *rope)


LOG2E = math.log2(math.e)
ATTN_TQ = 256


def _attn_kernel(*refs, tq, seq_len, has_ctx, windowed, has_sink):
    it = iter(refs)
    q_ref, k_ref, v_ref = next(it), next(it), next(it)
    kc_ref, vc_ref = (next(it), next(it)) if has_ctx else (None, None)
    sink_ref = next(it) if has_sink else None
    o_ref = next(it)
    if windowed:
        i = pl.program_id(2)
        span = tq + 2 * WINDOW
        start = pl.multiple_of(jnp.clip(i * tq - WINDOW, 0, seq_len - span), WINDOW)
        qpos = i * tq + lax.broadcasted_iota(jnp.int32, (tq, 1), 0)
        kpos = start + lax.broadcasted_iota(jnp.int32, (1, span), 1)
        segs = [(k_ref, v_ref, pl.ds(start, span), jnp.abs(kpos - qpos) <= WINDOW)]
    else:
        segs = [(k_ref, v_ref, slice(None), None)]
    if has_ctx:
        segs.append((kc_ref, vc_ref, slice(None), None))
    for h in range(KV_GROUP):
        hs = slice(h * HEAD_DIM, (h + 1) * HEAD_DIM)
        qh = q_ref[:, hs]
        scores = []
        m = None
        for kr, _, rows, mask in segs:
            s = lax.dot_general(qh, kr[rows, :], (((1,), (1,)), ((), ())), preferred_element_type=F32)
            if mask is not None:
                s = jnp.where(mask, s, NEG_INF)
            scores.append(s)
            ms = jnp.max(s, axis=-1, keepdims=True)
            m = ms if m is None else jnp.maximum(m, ms)
        if has_sink:
            sk = sink_ref[pl.program_id(1) * KV_GROUP + h]
            m = jnp.maximum(m, sk)
        l = jnp.exp2(sk - m) if has_sink else jnp.zeros_like(m)
        acc = jnp.zeros((tq, HEAD_DIM), F32)
        for (_, vr, rows, _), s in zip(segs, scores):
            p = jnp.exp2(s - m)
            l = l + jnp.sum(p, axis=-1, keepdims=True)
            acc = acc + _dot(p.astype(BF16), vr[rows, :])
        o_ref[:, hs] = (acc / l).astype(BF16)


def _attention(q, row0, k, v, k_row0, n_keys, k_ctx, v_ctx, sink, nb, L, windowed):
    tq = min(ATTN_TQ, L)
    nq = L // tq
    q_blk0 = row0 // tq
    seq0 = k_row0 // n_keys
    own = lambda: pl.BlockSpec((n_keys, HEAD_DIM), lambda b, g, i: (seq0 + b, g))
    in_specs = [pl.BlockSpec((tq, KV_GROUP * HEAD_DIM), lambda b, g, i: (q_blk0 + b * nq + i, g)), own(), own()]
    args = [q, k, v]
    if k_ctx is not None:
        ctx = lambda: pl.BlockSpec((None, PAST_LEN, HEAD_DIM), lambda b, g, i: (b, 0, g))
        in_specs += [ctx(), ctx()]
        args += [k_ctx, v_ctx]
    if sink is not None:
        in_specs.append(pl.BlockSpec(memory_space=pltpu.SMEM))
        args.append(sink.astype(F32) * LOG2E)
    return pl.pallas_call(
        functools.partial(_attn_kernel, tq=tq, seq_len=L, has_ctx=k_ctx is not None, windowed=windowed,
                          has_sink=sink is not None),
        grid=(nb, N_KV_HEADS, nq),
        in_specs=in_specs,
        out_specs=pl.BlockSpec((tq, KV_GROUP * HEAD_DIM), lambda b, g, i: (b * nq + i, g)),
        out_shape=jax.ShapeDtypeStruct((nb * L, D_MODEL), BF16),
        compiler_params=_cp(("parallel", "parallel", "parallel"), 56),
        name="attention",
    )(*args)


def _attn_mixer(x, g, mod, w_qkv, q_norm, k_norm, use_norm, sink, w_o, cache_k, cache_v, windowed, rope):
    q, k, v, new_k, new_v = _qkv_proj(x, g, mod, w_qkv.astype(BF16), q_norm, k_norm, use_norm, rope)
    op = _attention(q, 0, k, v, 0, SEQ, None, None, sink, BATCH, SEQ, False)
    kc = cache_k.reshape(DEC_BATCH, PAST_LEN, KV_DIM).astype(BF16)
    vc = cache_v.reshape(DEC_BATCH, PAST_LEN, KV_DIM).astype(BF16)
    if windowed:
        osm = _attention(q, TP, k, v, TP, DEC_SEQ, kc, vc, sink, DEC_BATCH, DEC_SEQ, True)
    else:
        n_keys = DEC_SEQ + PAST_LEN
        both = lambda a, c: jnp.concatenate([a[TP:].reshape(DEC_BATCH, DEC_SEQ, KV_DIM), c], axis=1).reshape(
            DEC_BATCH * n_keys, KV_DIM)
        osm = _attention(q, TP, both(k, kc), both(v, vc), 0, n_keys, None, None, sink, DEC_BATCH, DEC_SEQ, False)
    x = _resid_matmul(op, osm, w_o.astype(BF16), jnp.zeros((1, D_MODEL), F32), x, mod, "attn_out_proj")
    shape = (BATCH, SEQ, N_KV_HEADS, HEAD_DIM)
    return x, new_k.reshape(shape), new_v.reshape(shape)


ROUTE_TILE = 512
ROUTE_ROWS = 32


def _router_kernel(x_ref, g_ref, sh_ref, sc_ref, wr_ref, br_ref, xh_ref, rt_ref, cnt_ref, carry):
    tm = ROUTE_TILE
    i = pl.program_id(0)

    @pl.when(i == 0)
    def _():
        carry[...] = jnp.zeros_like(carry)

    h = _norm_mod(x_ref[...], g_ref[...], sh_ref[...], sc_ref[...])
    xh_ref[:, :HALF_D] = _pack_bf16_pairs(h)
    logits = _dot(h.astype(BF16), wr_ref[...])
    s = _sigmoid(logits.T[:N_EXPERTS, :])
    sb = s + br_ref[...]
    u = [s[e:e + 1, :] for e in range(N_EXPERTS)]
    v = [sb[e:e + 1, :] for e in range(N_EXPERTS)]

    gscore = []
    for gq in range(N_EXPERT_GROUPS):
        m = v[4 * gq:4 * gq + 4]
        best = m[PAIR_LO[0]] + m[PAIR_HI[0]]
        for a, b in zip(PAIR_LO[1:], PAIR_HI[1:]):
            best = jnp.maximum(best, m[a] + m[b])
        gscore.append(best)
    gidx = jnp.zeros((1, tm), jnp.int32)
    gbest = gscore[0]
    for gq in range(1, N_EXPERT_GROUPS):
        upd = gscore[gq] > gbest
        gidx = jnp.where(upd, gq, gidx)
        gbest = jnp.where(upd, gscore[gq], gbest)

    def pick(rows, j):
        out = rows[j]
        for gq in range(1, N_EXPERT_GROUPS):
            out = jnp.where(gidx == gq, rows[4 * gq + j], out)
        return out

    vin = [pick(v, j) for j in range(EXPERTS_PER_GROUP)]
    uin = [pick(u, j) for j in range(EXPERTS_PER_GROUP)]
    i1 = jnp.zeros((1, tm), jnp.int32)
    m1 = vin[0]
    for j in range(1, EXPERTS_PER_GROUP):
        upd = vin[j] > m1
        i1 = jnp.where(upd, j, i1)
        m1 = jnp.where(upd, vin[j], m1)
    i2 = jnp.full((1, tm), -1, jnp.int32)
    m2 = jnp.full((1, tm), -jnp.inf, F32)
    for j in range(EXPERTS_PER_GROUP):
        upd = (i1 != j) & (vin[j] > m2)
        i2 = jnp.where(upd, j, i2)
        m2 = jnp.where(upd, vin[j], m2)

    def sel(rows, idx):
        out = rows[0]
        for j in range(1, EXPERTS_PER_GROUP):
            out = jnp.where(idx == j, rows[j], out)
        return out

    w1, w2 = sel(uin, i1), sel(uin, i2)
    wsum = w1 + w2
    w1, w2 = w1 / wsum, w2 / wsum
    first_lo = i1 < i2
    lo = jnp.where(first_lo, i1, i2)
    hi = jnp.where(first_lo, i2, i1)
    w_lo = jnp.where(first_lo, w1, w2)
    w_hi = jnp.where(first_lo, w2, w1)
    pair = jnp.where(lo == 0, hi - 1, jnp.where(lo == 1, hi + 1, 5))
    bucket = gidx * len(PAIR_LO) + pair

    onehot = (lax.broadcasted_iota(jnp.int32, (ROUTE_ROWS, tm), 0) == bucket)
    tri = (lax.broadcasted_iota(jnp.int32, (tm, tm), 0) <= lax.broadcasted_iota(jnp.int32, (tm, tm), 1))
    cum = _dot(jnp.where(onehot, 1.0, 0.0).astype(BF16), jnp.where(tri, 1.0, 0.0).astype(BF16))
    rank = jnp.sum(jnp.where(onehot, cum - 1.0 + carry[...], 0.0), axis=0, keepdims=True)
    carry[...] = carry[...] + cum[:, tm - 1:tm]
    cnt_ref[...] = jnp.broadcast_to(carry[...], (ROUTE_ROWS, LANE))

    rt_ref[...] = jnp.zeros_like(rt_ref)
    rt_ref[0:1, :] = bucket.astype(F32)
    rt_ref[1:2, :] = rank
    wt = jnp.concatenate([w_lo, w_hi, jnp.zeros((LANE - 2, tm), F32)], axis=0)
    xh_ref[:, HALF_D:] = lax.bitcast_convert_type(wt.T, jnp.uint32)


def _router(x, g, mod, w_router, b_router):
    tm = ROUTE_TILE
    wr = _pad2(w_router, D_MODEL, LANE).astype(BF16)
    return pl.pallas_call(
        _router_kernel,
        grid=(T // tm,),
        in_specs=[
            pl.BlockSpec((tm, D_MODEL), lambda i: (i, 0)),
            pl.BlockSpec((1, D_MODEL), lambda i: (0, 0)),
            _mod_spec(tm, 3),
            _mod_spec(tm, 4),
            pl.BlockSpec((D_MODEL, LANE), lambda i: (0, 0)),
            pl.BlockSpec((N_EXPERTS, 1), lambda i: (0, 0)),
        ],
        out_specs=[
            pl.BlockSpec((tm, XH_W), lambda i: (i, 0)),
            pl.BlockSpec((8, tm), lambda i: (0, i)),
            pl.BlockSpec((ROUTE_ROWS, LANE), lambda i: (0, 0)),
        ],
        out_shape=[
            jax.ShapeDtypeStruct((T, XH_W), jnp.uint32),
            jax.ShapeDtypeStruct((8, T), F32),
            jax.ShapeDtypeStruct((ROUTE_ROWS, LANE), F32),
        ],
        scratch_shapes=[pltpu.VMEM((ROUTE_ROWS, 1), F32)],
        compiler_params=_cp(("arbitrary",)),
        name="moe_router",
    )(x, g, mod, mod, wr, b_router.reshape(N_EXPERTS, 1))


DISPATCH_TILE = 256


DMA_UNROLL = 32


def _invert_kernel(dest_ref, src_ref):
    def clear(s, c):
        src_ref[s] = 0
        return c

    def put(t, c):
        src_ref[dest_ref[t]] = t
        return c

    lax.fori_loop(0, T_PAD, clear, 0, unroll=DMA_UNROLL)
    lax.fori_loop(0, T, put, 0, unroll=DMA_UNROLL)


def _invert(dest):
    return pl.pallas_call(
        _invert_kernel,
        in_specs=[pl.BlockSpec(memory_space=pltpu.SMEM)],
        out_specs=pl.BlockSpec(memory_space=pltpu.SMEM),
        out_shape=jax.ShapeDtypeStruct((T_PAD,), jnp.int32),
        name="moe_invert",
    )(dest)


def _gather_rows(idx_ref, base, src_hbm, buf, sem, tm, static=False):
    def start(r, c):
        pltpu.make_async_copy(src_hbm.at[pl.ds(idx_ref[base + r], 1)], buf.at[pl.ds(r, 1)], sem).start()
        return c

    if static:
        for r in range(tm):
            start(r, 0)
    else:
        lax.fori_loop(0, tm, start, 0, unroll=DMA_UNROLL)


def _wait_rows(src_hbm, buf, sem, tm):
    pltpu.make_async_copy(src_hbm.at[pl.ds(0, tm)], buf, sem).wait()


def _expert_kernel(ea_ref, eb_ref, nv_ref, src_ref, xh_hbm, ga_ref, ua_ref, da_ref, gb_ref, ub_ref, db_ref, y_ref,
                   *scratch):
    tm = MOE_TILE
    j = pl.program_id(0)
    nv = nv_ref[j]
    *bufs, sems = scratch
    n_buf = len(bufs)

    @pl.when(j == 0)
    def _():
        for t in range(MOE_AHEAD):
            _gather_rows(src_ref, t * tm, xh_hbm, bufs[t], sems.at[t], tm)

    def run(p):
        cur, cur_sem = bufs[p], sems.at[p]
        q = (p + MOE_AHEAD) % n_buf
        ahead, ahead_sem = bufs[q], sems.at[q]

        @pl.when(jnp.logical_or(j < MOE_AHEAD, nv_ref[jnp.maximum(j - MOE_AHEAD, 0)] > 0))
        def _():
            _wait_rows(xh_hbm, cur, cur_sem, tm)

        @pl.when(nv > 0)
        def _():
            valid = lax.broadcasted_iota(jnp.int32, (tm, 1), 0) < nv
            x = jnp.where(valid, _unpack_bf16_pairs(cur[:, :HALF_D]), 0.0).astype(BF16)
            wts = jnp.where(valid, lax.bitcast_convert_type(cur[:, HALF_D:], F32), 0.0)
            for r in range(tm):
                pltpu.make_async_copy(xh_hbm.at[pl.ds(src_ref[(j + MOE_AHEAD) * tm + r], 1)],
                                      ahead.at[pl.ds(r, 1)], ahead_sem).start()

            def ffn(g_ref, u_ref, d_ref, w):
                a = _dot(x, g_ref[...])
                h = a * _sigmoid(a) * _dot(x, u_ref[...]) * w
                return _dot(h.astype(BF16), d_ref[...])

            y = ffn(ga_ref, ua_ref, da_ref, wts[:, 0:1]) + ffn(gb_ref, ub_ref, db_ref, wts[:, 1:2])
            packed = _pack_bf16_pairs(y)
            for c in range(Y_TILES):
                y_ref[:, c, :] = packed[:, c * LANE:(c + 1) * LANE]

    for p in range(n_buf):
        pl.when(j % n_buf == p)(functools.partial(run, p))

    @pl.when(nv == 0)
    def _():
        y_ref[...] = jnp.zeros_like(y_ref)


def _experts(layer, tile_ea, tile_eb, tile_nv, src, xh, w_gate, w_up, w_down):
    tm = MOE_TILE
    up = lambda sel: pl.BlockSpec((None, None, D_MODEL, D_EXPERT),
                                  lambda j, ea, eb, nv, sr: (layer, (ea, eb)[sel][j], 0, 0))
    down = lambda sel: pl.BlockSpec((None, None, D_EXPERT, D_MODEL),
                                    lambda j, ea, eb, nv, sr: (layer, (ea, eb)[sel][j], 0, 0))
    return pl.pallas_call(
        _expert_kernel,
        grid_spec=pltpu.PrefetchScalarGridSpec(
            num_scalar_prefetch=4,
            grid=(MOE_TILES,),
            in_specs=[pl.BlockSpec(memory_space=pl.ANY), up(0), up(0), down(0), up(1), up(1), down(1)],
            out_specs=pl.BlockSpec((tm, Y_TILES, LANE), lambda j, ea, eb, nv, sr: (j, 0, 0)),
            scratch_shapes=[pltpu.VMEM((tm, XH_W), jnp.uint32)] * (MOE_AHEAD + 1)
            + [pltpu.SemaphoreType.DMA((MOE_AHEAD + 1,))],
        ),
        out_shape=jax.ShapeDtypeStruct((T_PAD, Y_TILES, LANE), jnp.uint32),
        compiler_params=_cp(("arbitrary",), 56),
        name="moe_experts",
    )(tile_ea, tile_eb, tile_nv, src, xh, w_gate, w_up, w_down, w_gate, w_up, w_down)


def _combine_kernel(*refs, final):
    tm = DISPATCH_TILE
    if final:
        dest_ref, x_ref, gt_ref, ys_hbm, fg_ref, oc_ref, ol_ref, buf, sems = refs
    else:
        dest_ref, x_ref, gt_ref, ys_hbm, o_ref, buf, sems = refs
    i = pl.program_id(0)
    slot = i % 2

    last = pl.num_programs(0) - 1

    @pl.when(i == 0)
    def _():
        _gather_rows(dest_ref, 0, ys_hbm, buf.at[0], sems.at[0], tm)

    def finish():
        _wait_rows(ys_hbm, buf.at[slot], sems.at[slot], tm)
        y = jnp.concatenate([buf[slot, :, c, :] for c in range(Y_TILES)], axis=1)
        x = x_ref[...] + gt_ref[...] * _unpack_bf16_pairs(y)
        if not final:
            o_ref[...] = x
            return
        y = x * lax.rsqrt(jnp.mean(x * x, axis=-1, keepdims=True) + NORM_EPS) * fg_ref[...]
        is_ctx = i < TP // tm

        @pl.when(is_ctx)
        def _():
            oc_ref[...] = y

        @pl.when(jnp.logical_not(is_ctx))
        def _():
            ol_ref[...] = y

    @pl.when(i < last)
    def _():
        _gather_rows(dest_ref, (i + 1) * tm, ys_hbm, buf.at[1 - slot], sems.at[1 - slot], tm, static=True)
        finish()

    pl.when(i == last)(finish)


def _combine(dest, x, mod, ys, final_gain=None):
    tm = DISPATCH_TILE
    final = final_gain is not None
    n_ctx = TP // tm
    in_specs = [pl.BlockSpec((tm, D_MODEL), lambda i, d: (i, 0)),
                pl.BlockSpec((None, None, 1, D_MODEL), lambda i, d: (_cond_row(i * tm), 5, 0, 0)),
                pl.BlockSpec(memory_space=pl.ANY)]
    args = [dest, x, mod, ys]
    if final:
        in_specs.append(pl.BlockSpec((1, D_MODEL), lambda i, d: (0, 0)))
        args.append(final_gain.reshape(1, D_MODEL))
        out_specs = [pl.BlockSpec((tm, D_MODEL), lambda i, d: (jnp.minimum(i, n_ctx - 1), 0)),
                     pl.BlockSpec((tm, D_MODEL), lambda i, d: (jnp.maximum(i - n_ctx, 0), 0))]
        out_shape = [jax.ShapeDtypeStruct((TP, D_MODEL), F32), jax.ShapeDtypeStruct((TS, D_MODEL), F32)]
    else:
        out_specs = pl.BlockSpec((tm, D_MODEL), lambda i, d: (i, 0))
        out_shape = jax.ShapeDtypeStruct((T, D_MODEL), F32)
    return pl.pallas_call(
        functools.partial(_combine_kernel, final=final),
        grid_spec=pltpu.PrefetchScalarGridSpec(
            num_scalar_prefetch=1,
            grid=(T // tm,),
            in_specs=in_specs,
            out_specs=out_specs,
            scratch_shapes=[pltpu.VMEM((2, tm, Y_TILES, LANE), jnp.uint32), pltpu.SemaphoreType.DMA((2,))],
        ),
        out_shape=out_shape,
        compiler_params=_cp(("arbitrary",)),
        name="moe_combine",
    )(*args)


def _lookup(table, idx):
    n = table.shape[0]
    hit = idx[:, None] == jnp.arange(n, dtype=jnp.int32)[None, :]
    return jnp.sum(jnp.where(hit, table[None, :], 0), axis=1)


def _moe_plan(rt, cnt):
    bucket = rt[0].astype(jnp.int32)
    rank = rt[1].astype(jnp.int32)
    counts = cnt[:N_BUCKETS, 0].astype(jnp.int32)
    tiles = (counts + MOE_TILE - 1) // MOE_TILE
    order = jnp.arange(N_BUCKETS, dtype=jnp.int32)
    tile_start = jnp.sum(jnp.where(order[None, :] < order[:, None], tiles[None, :], 0), axis=1)
    tile_end = tile_start + tiles
    n_used = tile_end[N_BUCKETS - 1]
    dest = _lookup(tile_start * MOE_TILE, bucket) + rank
    j = jnp.arange(MOE_TILES, dtype=jnp.int32)
    jc = jnp.minimum(j, n_used - 1)
    b = jnp.minimum(jnp.sum((jc[:, None] >= tile_end[None, :]).astype(jnp.int32), axis=1), N_BUCKETS - 1)
    nv = jnp.clip(_lookup(counts, b) - (j - _lookup(tile_start, b)) * MOE_TILE, 0, MOE_TILE)
    nv = jnp.where(j < n_used, nv, 0)
    n_pairs = len(PAIR_LO)
    ea = (b // n_pairs) * EXPERTS_PER_GROUP + _lookup(jnp.asarray(PAIR_LO, jnp.int32), b % n_pairs)
    eb = (b // n_pairs) * EXPERTS_PER_GROUP + _lookup(jnp.asarray(PAIR_HI, jnp.int32), b % n_pairs)
    return dest, ea, eb, nv


def _moe(layer, x, g, mod, w_router, b_router, w_gate, w_up, w_down, final_gain=None):
    xh, rt, cnt = _router(x, g, mod, w_router, b_router)
    dest, ea, eb, nv = _moe_plan(rt, cnt)
    ys = _experts(layer, ea, eb, nv, _invert(dest), xh, w_gate, w_up, w_down)
    return _combine(dest, x, mod, ys, final_gain)


def kernel(x_prompt, x_sample, cache_k_full, cache_v_full, cache_k_win, cache_v_win, c, c_ctx, w_mod, b_mod, norm_mix, norm_ffn, final_norm, pool_w, pool_scale, hy_w_in, hy_b_in, hy_conv_w, hy_conv_b, hy_f_w1, hy_f_b1, hy_f_w2, hy_f_b2, hy_f_freq, hy_f_w3, hy_decay, hy_skip, hy_w_out, hy_b_out, fa_w_qkv, fa_q_norm, fa_k_norm, fa_w_o, wa_w_qkv, wa_sink, wa_w_o, w_router, b_router, moe_w_gate, moe_w_up, moe_w_down):
    x = None
    cond =jnp.concatenate([c_ctx[None, :], c, jnp.zeros((N_COND - 1 - DEC_BATCH, D_MODEL), F32)], axis=0)
    mods = _adaln(cond, w_mod, b_mod).reshape(DEPTH, N_COND, 6, 1, D_MODEL)
    rope = _rope_tables()
    ones_hd = jnp.ones((HEAD_DIM,), F32)
    wg_bf, wu_bf, wd_bf = moe_w_gate.astype(BF16), moe_w_up.astype(BF16), moe_w_down.astype(BF16)
    new_kv = {}
    for layer in range(DEPTH):
        kind = layer % 4
        j = layer // 4
        mod = mods[layer]
        g_mix = norm_mix[layer].reshape(1, D_MODEL)
        if kind == 0:
            assert layer == 0, "the pooling mixer reads the two input streams, so it must be the first layer"
            x = _pool_mixer(x_prompt.reshape(TP, D_MODEL), x_sample.reshape(TS, D_MODEL), g_mix, mod,
                            pool_w[j], pool_scale[j])
        elif kind == 1:
            x = _hyena_mixer(x, g_mix, mod, hy_w_in[j], hy_b_in[j], hy_conv_w[j], hy_conv_b[j], hy_f_w1[j],
                             hy_f_b1[j], hy_f_w2[j], hy_f_b2[j], hy_f_freq[j], hy_f_w3[j], hy_decay[j],
                             hy_skip[j], hy_w_out[j], hy_b_out[j])
        elif kind == 2:
            x, nk, nv = _attn_mixer(x, g_mix, mod, fa_w_qkv[j], fa_q_norm[j], fa_k_norm[j], True, None,
                                    fa_w_o[j], cache_k_full[:, j], cache_v_full[:, j], False, rope)
            new_kv.setdefault("kf", []).append(nk)
            new_kv.setdefault("vf", []).append(nv)
        else:
            x, nk, nv = _attn_mixer(x, g_mix, mod, wa_w_qkv[j], ones_hd, ones_hd, False, wa_sink[j],
                                    wa_w_o[j], cache_k_win[:, j], cache_v_win[:, j], True, rope)
            new_kv.setdefault("kw", []).append(nk)
            new_kv.setdefault("vw", []).append(nv)
        x = _moe(layer, x, norm_ffn[layer].reshape(1, D_MODEL), mod, w_router, b_router, wg_bf, wu_bf, wd_bf,
                 final_norm if layer == DEPTH - 1 else None)
    y_prompt, y_sample = x
    y_prompt = y_prompt.reshape(BATCH, SEQ, D_MODEL)
    y_sample = y_sample.reshape(DEC_BATCH, DEC_SEQ, D_MODEL)
    return (y_prompt, y_sample, jnp.stack(new_kv["kf"], axis=1), jnp.stack(new_kv["vf"], axis=1),
            jnp.stack(new_kv["kw"], axis=1), jnp.stack(new_kv["vw"], axis=1))
```

```python
import functools
import math

import jax
import jax.numpy as jnp
import numpy as np
from jax import lax
from jax.experimental import pallas as pl
from jax.experimental.pallas import tpu as pltpu

D_MODEL = 2048
BATCH = 32
SEQ = 256
DEPTH = 4
DEC_BATCH = 4
DEC_SEQ = 4096
PAST_LEN = 512
GRID_W = 64
N_HEADS = 16
N_KV_HEADS = 4
HEAD_DIM = D_MODEL // N_HEADS
KV_GROUP = N_HEADS // N_KV_HEADS
KV_DIM = N_KV_HEADS * HEAD_DIM
QKV_DIM = (N_HEADS + 2 * N_KV_HEADS) * HEAD_DIM
ROPE_THETA = 10000.0
WINDOW = 128
POOL_WINDOWS = (2, 4, 8, 16)
POOL_GROUP = D_MODEL // len(POOL_WINDOWS)
HYENA_EMB_BANDS = 16
HYENA_FILTER_HIDDEN = 64
N_EXPERTS = 16
N_EXPERT_GROUPS = 4
EXPERTS_PER_GROUP = 4
D_EXPERT = 512
NORM_EPS = 1e-6
NEG_INF = -1e30

F32 = jnp.float32
BF16 = jnp.bfloat16

TP = BATCH * SEQ
TS = DEC_BATCH * DEC_SEQ
T = TP + TS
N_COND = 8
LANE = 128
MIB = 1024 * 1024

PAIR_LO = (0, 0, 0, 1, 1, 2)
PAIR_HI = (1, 2, 3, 2, 3, 3)
N_BUCKETS = N_EXPERT_GROUPS * len(PAIR_LO)
MOE_TILE = 256
MOE_AHEAD = 3
MOE_TILES = T // MOE_TILE + N_BUCKETS + MOE_AHEAD
T_PAD = MOE_TILES * MOE_TILE
HALF_D = D_MODEL // 2
XH_W = HALF_D + LANE


def _cp(sem, vmem_mb=48):
    return pltpu.CompilerParams(dimension_semantics=sem, vmem_limit_bytes=vmem_mb * MIB)


def _dot(a, b):
    return jnp.dot(a, b, preferred_element_type=F32)


def _dot3(a, b):
    ah = a.astype(BF16)
    al = (a - ah.astype(F32)).astype(BF16)
    bh = b.astype(BF16)
    bl = (b - bh.astype(F32)).astype(BF16)
    return _dot(ah, bh) + (_dot(al, bh) + _dot(ah, bl))


def _sigmoid(x):
    return 1.0 / (1.0 + jnp.exp(-x))


def _pack_bf16_pairs(x):
    n = x.shape[1] // 2
    bits = lambda v: lax.bitcast_convert_type(v.astype(BF16).astype(F32), jnp.uint32)
    return (bits(x[:, :n]) >> 16) | bits(x[:, n:])


def _unpack_bf16_pairs(u):
    lo = lax.bitcast_convert_type(u << 16, F32)
    hi = lax.bitcast_convert_type(u & jnp.uint32(0xFFFF0000), F32)
    return jnp.concatenate([lo, hi], axis=1)


def _cond_row(r):
    return jnp.where(r < TP, 0, 1 + (r - TP) // DEC_SEQ)


def _mod_spec(tm, chunk, tn=D_MODEL, ncol=False):
    if ncol:
        return pl.BlockSpec((None, None, 1, tn), lambda i, j: (_cond_row(i * tm), chunk, 0, j))
    return pl.BlockSpec((None, None, 1, tn), lambda i, *_: (_cond_row(i * tm), chunk, 0, 0))


def _norm_mod(x, g, shift, scale):
    var = jnp.mean(x * x, axis=-1, keepdims=True)
    y = x * lax.rsqrt(var + NORM_EPS) * g
    return y * (1.0 + scale) + shift


def _adaln_kernel(c_ref, w_ref, b_ref, o_ref):
    c = c_ref[...]
    a = c * _sigmoid(c)
    o_ref[...] = _dot3(a, w_ref[...]) + b_ref[...]


def _adaln(cond, w_mod, b_mod):
    tn = 1024
    n = 6 * D_MODEL
    return pl.pallas_call(
        _adaln_kernel,
        grid=(DEPTH, n // tn),
        in_specs=[
            pl.BlockSpec((N_COND, D_MODEL), lambda l, j: (0, 0)),
            pl.BlockSpec((None, D_MODEL, tn), lambda l, j: (l, 0, j)),
            pl.BlockSpec((None, 1, tn), lambda l, j: (l, 0, j)),
        ],
        out_specs=pl.BlockSpec((None, N_COND, tn), lambda l, j: (l, 0, j)),
        out_shape=jax.ShapeDtypeStruct((DEPTH, N_COND, n), F32),
        compiler_params=_cp(("parallel", "parallel")),
        name="adaln",
    )(cond, w_mod, b_mod.reshape(DEPTH, 1, n))


def _nm_matmul_kernel(x_ref, g_ref, sh_ref, sc_ref, w_ref, b_ref, o_ref, h_scr):
    @pl.when(pl.program_id(1) == 0)
    def _():
        h_scr[...] = _norm_mod(x_ref[...], g_ref[...], sh_ref[...], sc_ref[...]).astype(BF16)

    o_ref[...] = (_dot(h_scr[...], w_ref[...]) + b_ref[...]).astype(o_ref.dtype)


def _nm_matmul(x, g, mod, w, b, out_dtype, name):
    tm, tn = 1024, 1024
    n = w.shape[1]
    return pl.pallas_call(
        _nm_matmul_kernel,
        grid=(T // tm, n // tn),
        in_specs=[
            pl.BlockSpec((tm, D_MODEL), lambda i, j: (i, 0)),
            pl.BlockSpec((1, D_MODEL), lambda i, j: (0, 0)),
            _mod_spec(tm, 0),
            _mod_spec(tm, 1),
            pl.BlockSpec((D_MODEL, tn), lambda i, j: (0, j)),
            pl.BlockSpec((1, tn), lambda i, j: (0, j)),
        ],
        out_specs=pl.BlockSpec((tm, tn), lambda i, j: (i, j)),
        out_shape=jax.ShapeDtypeStruct((T, n), out_dtype),
        scratch_shapes=[pltpu.VMEM((tm, D_MODEL), BF16)],
        compiler_params=_cp(("parallel", "arbitrary")),
        name=name,
    )(x, g, mod, mod, w, b)


RESID_TM = 1024


def _resid_matmul_kernel(ap_ref, as_ref, w_ref, b_ref, x_ref, gt_ref, o_ref):
    def emit(a_ref):
        o_ref[...] = x_ref[...] + gt_ref[...] * (_dot(a_ref[...], w_ref[...]) + b_ref[...])

    is_ctx = pl.program_id(0) < TP // RESID_TM
    pl.when(is_ctx)(lambda: emit(ap_ref))
    pl.when(jnp.logical_not(is_ctx))(lambda: emit(as_ref))


def _resid_matmul(a_ctx, a_lat, w, b, x, mod, name):
    tm, tn = RESID_TM, 1024
    k = a_ctx.shape[1]
    n_ctx = TP // tm
    return pl.pallas_call(
        _resid_matmul_kernel,
        grid=(T // tm, D_MODEL // tn),
        in_specs=[
            pl.BlockSpec((tm, k), lambda i, j: (jnp.minimum(i, n_ctx - 1), 0)),
            pl.BlockSpec((tm, k), lambda i, j: (jnp.maximum(i - n_ctx, 0), 0)),
            pl.BlockSpec((k, tn), lambda i, j: (0, j)),
            pl.BlockSpec((1, tn), lambda i, j: (0, j)),
            pl.BlockSpec((tm, tn), lambda i, j: (i, j)),
            _mod_spec(tm, 2, tn, ncol=True),
        ],
        out_specs=pl.BlockSpec((tm, tn), lambda i, j: (i, j)),
        out_shape=jax.ShapeDtypeStruct((T, D_MODEL), F32),
        compiler_params=_cp(("parallel", "parallel")),
        name=name,
    )(a_ctx, a_lat, w, b, x, mod)


POOL_TILE = 256
POOL_HALO = 8


def _seq_pos(r0):
    is_ctx = r0 < TP
    loc0 = jnp.where(is_ctx, r0 % SEQ, (r0 - TP) % DEC_SEQ)
    seq_len = jnp.where(is_ctx, SEQ, DEC_SEQ)
    return loc0, seq_len


def _pool_kernel(xc_ref, xcp_ref, xcn_ref, xl_ref, xlp_ref, xln_ref, *rest):
    is_ctx = pl.program_id(0) < TP // POOL_TILE
    pl.when(is_ctx)(lambda: _pool_tile(xc_ref, xcp_ref, xcn_ref, *rest))
    pl.when(jnp.logical_not(is_ctx))(lambda: _pool_tile(xl_ref, xlp_ref, xln_ref, *rest))


def _pool_tile(x_ref, xp_ref, xn_ref, g_ref, sh_ref, sc_ref, gt_ref, pw_ref, ps_ref, o_ref, hz_scr):
    tm, hl = POOL_TILE, POOL_HALO
    loc0, seq_len = _seq_pos(pl.program_id(0) * tm)
    has_prev = loc0 > 0
    has_next = loc0 + tm < seq_len
    g, sh, sc = g_ref[...], sh_ref[...], sc_ref[...]
    x = x_ref[...]
    h = _norm_mod(x, g, sh, sc)
    hz_scr[0:hl, :] = jnp.where(has_prev, _norm_mod(xp_ref[...], g, sh, sc), 0.0)
    hz_scr[hl:hl + tm, :] = h
    hz_scr[hl + tm:, :] = jnp.where(has_next, _norm_mod(xn_ref[...], g, sh, sc), 0.0)
    tl = loc0 + lax.broadcasted_iota(jnp.int32, (tm, 1), 0)
    outs = []
    for gi, w in enumerate(POOL_WINDOWS):
        cs = slice(gi * POOL_GROUP, (gi + 1) * POOL_GROUP)
        s = jnp.zeros((tm, POOL_GROUP), F32)
        for off in range(-(w // 2), w - w // 2):
            s = s + hz_scr[hl + off:hl + off + tm, cs]
        lo = jnp.maximum(tl - w // 2, 0)
        hi = jnp.minimum(tl + (w - w // 2), seq_len)
        d = s / (hi - lo).astype(F32) - h[:, cs]
        outs.append(_dot(d.astype(BF16), pw_ref[gi]))
    out = jnp.concatenate(outs, axis=1) * ps_ref[...]
    o_ref[...] = x + gt_ref[...] * out


def _pool_mixer(x_ctx, x_lat, g, mod, pool_w, pool_scale):
    tm, hl = POOL_TILE, POOL_HALO
    r = tm // hl

    def stream(first_tile, rows):
        tile = lambda i: jnp.clip(i - first_tile, 0, rows // tm - 1)
        return [pl.BlockSpec((tm, D_MODEL), lambda i: (tile(i), 0)),
                pl.BlockSpec((hl, D_MODEL), lambda i: (jnp.maximum(tile(i) * r - 1, 0), 0)),
                pl.BlockSpec((hl, D_MODEL), lambda i: (jnp.minimum((tile(i) + 1) * r, rows // hl - 1), 0))]

    return pl.pallas_call(
        _pool_kernel,
        grid=(T // tm,),
        in_specs=stream(0, TP) + stream(TP // tm, TS) + [
            pl.BlockSpec((1, D_MODEL), lambda i: (0, 0)),
            _mod_spec(tm, 0),
            _mod_spec(tm, 1),
            _mod_spec(tm, 2),
            pl.BlockSpec((len(POOL_WINDOWS), POOL_GROUP, POOL_GROUP), lambda i: (0, 0, 0)),
            pl.BlockSpec((1, D_MODEL), lambda i: (0, 0)),
        ],
        out_specs=pl.BlockSpec((tm, D_MODEL), lambda i: (i, 0)),
        out_shape=jax.ShapeDtypeStruct((T, D_MODEL), F32),
        scratch_shapes=[pltpu.VMEM((tm + 2 * hl, D_MODEL), F32)],
        compiler_params=_cp(("parallel",)),
        name="pool_mixer",
    )(x_ctx, x_ctx, x_ctx, x_lat, x_lat, x_lat, g, mod, mod, mod, pool_w.astype(BF16),
      pool_scale.reshape(1, D_MODEL))


CONV_TILE = 256
CONV_HALO = 16


def _conv3_kernel(u_ref, up_ref, un_ref, cw_ref, cb_ref, o_ref, scr):
    tm, hl = CONV_TILE, CONV_HALO
    loc0, seq_len = _seq_pos(pl.program_id(0) * tm)
    has_prev = loc0 > 0
    has_next = loc0 + tm < seq_len
    scr[0:hl, :] = jnp.where(has_prev, up_ref[...].astype(F32), 0.0)
    scr[hl:hl + tm, :] = u_ref[...].astype(F32)
    scr[hl + tm:, :] = jnp.where(has_next, un_ref[...].astype(F32), 0.0)
    out = (scr[hl - 1:hl - 1 + tm, :] * cw_ref[0:1, :] + scr[hl:hl + tm, :] * cw_ref[1:2, :]
           + scr[hl + 1:hl + 1 + tm, :] * cw_ref[2:3, :] + cb_ref[...])
    o_ref[...] = out.astype(o_ref.dtype)


def _conv3(u0, conv_w, conv_b):
    tm, hl, tc = CONV_TILE, CONV_HALO, D_MODEL
    r = tm // hl
    n = u0.shape[1]
    return pl.pallas_call(
        _conv3_kernel,
        grid=(T // tm, n // tc),
        in_specs=[
            pl.BlockSpec((tm, tc), lambda i, j: (i, j)),
            pl.BlockSpec((hl, tc), lambda i, j: (jnp.maximum(i * r - 1, 0), j)),
            pl.BlockSpec((hl, tc), lambda i, j: (jnp.minimum((i + 1) * r, T // hl - 1), j)),
            pl.BlockSpec((3, tc), lambda i, j: (0, j)),
            pl.BlockSpec((1, tc), lambda i, j: (0, j)),
        ],
        out_specs=pl.BlockSpec((tm, tc), lambda i, j: (i, j)),
        out_shape=jax.ShapeDtypeStruct((T, n), BF16),
        scratch_shapes=[pltpu.VMEM((tm + 2 * hl, tc), F32)],
        compiler_params=_cp(("parallel", "parallel")),
        name="hyena_conv3",
    )(u0, u0, u0, conv_w, conv_b.reshape(1, n))


FILT_TILE = 256


HYENA_BLOCK = 1024


T_LANE = LANE - 1


def _filter_mlp_kernel(emb_ref, w1_ref, b1_ref, w2_ref, b2_ref, fr_ref, o_ref):
    emb = emb_ref[...]
    fr = fr_ref[...]
    a = jnp.sin(fr * (_dot3(emb, w1_ref[...]) + b1_ref[...]))
    a = jnp.sin(fr * (_dot3(a, w2_ref[...]) + b2_ref[...]))
    lane = lax.broadcasted_iota(jnp.int32, a.shape, 1)
    o_ref[...] = jnp.where(lane == T_LANE, emb[:, 0:1], a)


def _filter_mlp(pos, L, f_w1, f_b1, f_w2, f_b2, f_freq):
    assert HYENA_FILTER_HIDDEN <= T_LANE
    tl = FILT_TILE
    rows = pos.shape[0]
    small = lambda: pl.BlockSpec((LANE, LANE), lambda i: (0, 0))
    vec = lambda: pl.BlockSpec((1, LANE), lambda i: (0, 0))
    return pl.pallas_call(
        _filter_mlp_kernel,
        grid=(rows // tl,),
        in_specs=[pl.BlockSpec((tl, LANE), lambda i: (i, 0)), small(), vec(), small(), vec(), vec()],
        out_specs=pl.BlockSpec((tl, LANE), lambda i: (i, 0)),
        out_shape=jax.ShapeDtypeStruct((rows, LANE), F32),
        compiler_params=_cp(("parallel",)),
        name="hyena_filter_mlp",
    )(_filter_embedding(pos, L), _pad2(f_w1, LANE, LANE), _pad2(f_b1[None], 1, LANE),
      _pad2(f_w2, LANE, LANE), _pad2(f_b2[None], 1, LANE), _pad2(f_freq[None], 1, LANE))


def _filter_kernel(h1_ref, h2_ref, w3a_ref, dca_ref, w3b_ref, dcb_ref, fa_ref, fb_ref, *, blk):
    def taps(h, w3_ref, dc_ref):
        return _dot3(h, w3_ref[...]) * jnp.exp(-h[:, T_LANE:] * jnp.abs(dc_ref[...]))

    pos = taps(h1_ref[...], w3a_ref, dca_ref)
    neg = taps(h2_ref[...], w3b_ref, dcb_ref)
    m = (pl.program_id(0) * FILT_TILE + lax.broadcasted_iota(jnp.int32, (FILT_TILE, 1), 0)) % blk
    fa_ref[...] = jnp.where(m == 0, pos, pos + neg).astype(BF16)
    fb_ref[...] = jnp.where(m == 0, 0.0, neg - pos).astype(BF16)


def _pad2(a, rows, cols):
    return jnp.pad(a, ((0, rows - a.shape[0]), (0, cols - a.shape[1])))


def _filter_positions(L, blk):
    n_blk = L // blk
    m = np.arange(blk)
    p1, p2 = [], []
    for d in range(-(n_blk - 1), n_blk):
        if d >= 1:
            p1.append(d * blk + m), p2.append(d * blk - m)
        elif d == 0:
            p1.append(m), p2.append(m)
        else:
            p1.append(-d * blk - m), p2.append(-d * blk + m)
    return np.concatenate(p1), np.concatenate(p2)


def _filter_embedding(pos, L):
    t = jnp.asarray(pos, F32) / L
    bands = jnp.linspace(1e-4, HYENA_EMB_BANDS - 1, HYENA_EMB_BANDS, dtype=F32)
    ang = (2 * math.pi) * t[:, None] * bands[None, :]
    return _pad2(jnp.concatenate([t[:, None], jnp.cos(ang), -jnp.sin(ang)], axis=-1), pos.shape[0], LANE)


def _hyena_filters(L, blk, f_w1, f_b1, f_w2, f_b2, f_freq, f_w3, decay):
    n_blk = L // blk
    p1, p2 = _filter_positions(L, blk)
    rows = p1.shape[0]
    tl = FILT_TILE
    tiles_per_lag = blk // tl
    lag = lambda i: i // tiles_per_lag - (n_blk - 1)
    col1 = lambda i, n: 2 * n + jnp.where(lag(i) >= 0, 0, 1)
    col2 = lambda i, n: 2 * n + jnp.where(lag(i) >= 1, 0, 1)
    n_tiles = rows // tl
    out = pl.BlockSpec((None, tl, D_MODEL), lambda i, n: (n, i, 0))
    w3 = _pad2(f_w3, LANE, f_w3.shape[1])
    hidden = _filter_mlp(np.concatenate([p1, p2]), L, f_w1, f_b1, f_w2, f_b2, f_freq)
    return pl.pallas_call(
        functools.partial(_filter_kernel, blk=blk),
        grid=(n_tiles, 2),
        in_specs=[
            pl.BlockSpec((tl, LANE), lambda i, n: (i, 0)),
            pl.BlockSpec((tl, LANE), lambda i, n: (n_tiles + i, 0)),
            pl.BlockSpec((LANE, D_MODEL), lambda i, n: (0, col1(i, n))),
            pl.BlockSpec((1, D_MODEL), lambda i, n: (0, col1(i, n))),
            pl.BlockSpec((LANE, D_MODEL), lambda i, n: (0, col2(i, n))),
            pl.BlockSpec((1, D_MODEL), lambda i, n: (0, col2(i, n))),
        ],
        out_specs=[out, out],
        out_shape=[jax.ShapeDtypeStruct((2, rows, D_MODEL), BF16)] * 2,
        compiler_params=_cp(("parallel", "parallel")),
        name="hyena_filters",
    )(hidden, hidden, w3, decay[None], w3, decay[None])


def _dft_mats(L):
    r = int(math.isqrt(L))
    k2 = 2 * jnp.arange(L, dtype=jnp.int32)[:, None] + 1
    n1 = r * jnp.arange(L // r, dtype=jnp.int32)[None, :]
    n2 = jnp.arange(r, dtype=jnp.int32)[None, :]
    sc = math.pi / (2 * L)
    aa = ((k2 * n1) % (4 * L)).astype(F32) * sc
    ab = ((k2 * n2) % (4 * L)).astype(F32) * sc
    ca, sa, cb, sb = jnp.cos(aa)[:, :, None], jnp.sin(aa)[:, :, None], jnp.cos(ab)[:, None, :], jnp.sin(ab)[:, None, :]
    c = (ca * cb - sa * sb).reshape(L, L)
    s = (sa * cb + ca * sb).reshape(L, L)
    return c.astype(BF16), s.astype(BF16), c.T.astype(BF16), s.T.astype(BF16)


def _dft_tiles(L):
    return min(512, L), 512


def _dft_filter_kernel(c_ref, s_ref, a_ref, b_ref, gr_ref, gi_ref):
    gr_ref[...] = _dot(c_ref[...], a_ref[...]).astype(gr_ref.dtype)
    gi_ref[...] = _dot(s_ref[...], b_ref[...]).astype(gi_ref.dtype)


def _dft_filter(cm, sm, fa, fb, L):
    tf, tn = _dft_tiles(L)
    n = fa.shape[0]
    mat = lambda: pl.BlockSpec((tf, L), lambda k, c, s: (k, 0))
    rhs = lambda: pl.BlockSpec((None, L, tn), lambda k, c, s: (s, 0, c))
    out = pl.BlockSpec((None, tf, tn), lambda k, c, s: (s, k, c))
    return pl.pallas_call(
        _dft_filter_kernel,
        grid=(L // tf, D_MODEL // tn, n),
        in_specs=[mat(), mat(), rhs(), rhs()],
        out_specs=[out, out],
        out_shape=[jax.ShapeDtypeStruct((n, L, D_MODEL), BF16)] * 2,
        compiler_params=_cp(("parallel", "parallel", "parallel")),
        name="hyena_filter_dft",
    )(cm, sm, fa, fb)


FWD_TF = 256


def _dft_fwd_kernel(c_ref, s_ref, z_ref, gr_ref, gi_ref, yr_ref, yi_ref, *, n_blk, blk, bpb):
    c, s = c_ref[...], s_ref[...]
    for bb in range(bpb):
        zc, zs = [], []
        for j in range(n_blk):
            r = (bb * n_blk + j) * blk
            zj = z_ref[r:r + blk, :]
            zc.append(_dot(c, zj).astype(BF16))
            zs.append(_dot(s, zj).astype(BF16))
        for i in range(n_blk):
            yr = yi = None
            for j in range(n_blk):
                lag = i - j + n_blk - 1
                gr, gi = gr_ref[lag], gi_ref[lag]
                tr = gr * zc[j] + gi * zs[j]
                ti = gi * zc[j] - gr * zs[j]
                yr = tr if yr is None else yr + tr
                yi = ti if yi is None else yi + ti
            yr_ref[bb, i] = yr.astype(BF16)
            yi_ref[bb, i] = yi.astype(BF16)


def _seqs_per_step(L):
    return max(1, 2048 // L)


def _dft_fwd(cm, sm, z, z_rowblk, z_colblk, gr, gi, order, nb, L, blk):
    n_blk = L // blk
    bpb = _seqs_per_step(L)
    assert nb % bpb == 0 and z_rowblk % bpb == 0
    tf, tn = min(FWD_TF, blk), 512
    mat = lambda: pl.BlockSpec((tf, blk), lambda k, c, b: (k, 0))
    gsp = lambda: pl.BlockSpec((None, 2 * n_blk - 1, tf, tn), lambda k, c, b: (order, 0, k, c))
    out = pl.BlockSpec((bpb, n_blk, tf, tn), lambda k, c, b: (b, 0, k, c))
    return pl.pallas_call(
        functools.partial(_dft_fwd_kernel, n_blk=n_blk, blk=blk, bpb=bpb),
        grid=(blk // tf, D_MODEL // tn, nb // bpb),
        in_specs=[mat(), mat(),
                  pl.BlockSpec((bpb * L, tn), lambda k, c, b: (z_rowblk // bpb + b, z_colblk + c)),
                  gsp(), gsp()],
        out_specs=[out, out],
        out_shape=[jax.ShapeDtypeStruct((nb, n_blk, blk, D_MODEL), BF16)] * 2,
        compiler_params=_cp(("parallel", "parallel", "parallel")),
        name="hyena_dft_fwd",
    )(cm, sm, z, gr, gi)


def _dft_inv_kernel(ct_ref, st_ref, yr_ref, yi_ref, z_ref, gt_ref, sk_ref, o_ref, *, inv_len, bpb, tt):
    for bb in range(bpb):
        rows = slice(bb * tt, (bb + 1) * tt)
        y = (_dot(ct_ref[...], yr_ref[bb]) - _dot(st_ref[...], yi_ref[bb])) * inv_len
        o_ref[rows, :] = (gt_ref[rows, :].astype(F32) * (y + sk_ref[...] * z_ref[rows, :].astype(F32))).astype(BF16)


def _dft_inv(ctm, stm, yr, yi, z, z_rowblk, z_colblk, gate, g_rowblk, g_colblk, skip, nb, L):
    tt, tn = _dft_tiles(L)
    rpb = L // tt
    bpb = _seqs_per_step(L) if rpb == 1 else 1
    assert nb % bpb == 0 and z_rowblk % bpb == 0 and g_rowblk % bpb == 0
    mat = lambda: pl.BlockSpec((tt, L), lambda t, c, b: (t, 0))
    spec = lambda: pl.BlockSpec((bpb, L, tn), lambda t, c, b: (b, 0, c))
    rows = lambda blk0: (lambda t, c, b: ((blk0 + b * bpb * rpb + t) // bpb))
    return pl.pallas_call(
        functools.partial(_dft_inv_kernel, inv_len=1.0 / L, bpb=bpb, tt=tt),
        grid=(rpb, D_MODEL // tn, nb // bpb),
        in_specs=[mat(), mat(), spec(), spec(),
                  pl.BlockSpec((bpb * tt, tn), lambda t, c, b: (rows(z_rowblk)(t, c, b), z_colblk + c)),
                  pl.BlockSpec((bpb * tt, tn), lambda t, c, b: (rows(g_rowblk)(t, c, b), g_colblk + c)),
                  pl.BlockSpec((1, tn), lambda t, c, b: (0, c))],
        out_specs=pl.BlockSpec((bpb * tt, tn), lambda t, c, b: (rows(0)(t, c, b), c)),
        out_shape=jax.ShapeDtypeStruct((nb * L, D_MODEL), BF16),
        compiler_params=_cp(("parallel", "parallel", "parallel")),
        name="hyena_dft_inv",
    )(ctm, stm, yr, yi, z, gate, skip)


def _hyena_stream(u, row0, nb, L, fparams, skip):
    blk = min(HYENA_BLOCK, L)
    n_blk = L // blk
    n_lag = 2 * n_blk - 1
    cm, sm, ctm, stm = _dft_mats(blk)
    fa, fb = _hyena_filters(L, blk, *fparams)
    seg = lambda a: a.reshape(2 * n_lag, blk, D_MODEL)
    gr, gi = _dft_filter(cm, sm, seg(fa), seg(fb), blk)
    gr, gi = (a.reshape(2, n_lag, blk, D_MODEL) for a in (gr, gi))
    tt, tn = _dft_tiles(blk)
    ncb = D_MODEL // tn
    blocks = lambda a: a.reshape(nb * n_blk, blk, D_MODEL)

    yr, yi = _dft_fwd(cm, sm, u, row0 // L, 0, gr, gi, 0, nb, L, blk)
    z1 = _dft_inv(ctm, stm, blocks(yr), blocks(yi), u, row0 // tt, 0, u, row0 // tt, ncb, skip[0:1],
                  nb * n_blk, blk)
    yr, yi = _dft_fwd(cm, sm, z1, 0, 0, gr, gi, 1, nb, L, blk)
    return _dft_inv(ctm, stm, blocks(yr), blocks(yi), z1, 0, 0, u, row0 // tt, 2 * ncb, skip[1:2],
                    nb * n_blk, blk)


def _hyena_mixer(x, g, mod, w_in, b_in, conv_w, conv_b, f_w1, f_b1, f_w2, f_b2, f_freq, f_w3, decay, skip,
                 w_out, b_out):
    u0 = _nm_matmul(x, g, mod, w_in.astype(BF16), b_in.reshape(1, -1), BF16, "hyena_in_proj")
    u = _conv3(u0, conv_w, conv_b)
    fparams = (f_w1, f_b1, f_w2, f_b2, f_freq, f_w3, decay)
    zp = _hyena_stream(u, 0, BATCH, SEQ, fparams, skip)
    zs = _hyena_stream(u, TP, DEC_BATCH, DEC_SEQ, fparams, skip)
    return _resid_matmul(zp, zs, w_out.astype(BF16), b_out.reshape(1, -1), x, mod, "hyena_out_proj")


def _rope_tables():
    pos = jnp.arange(DEC_SEQ, dtype=jnp.int32)
    row = (pos // GRID_W).astype(F32)
    col = (pos % GRID_W).astype(F32)
    axis_dim = HEAD_DIM // 2
    inv_freq = ROPE_THETA ** (-jnp.arange(0, axis_dim, 2, dtype=F32) / axis_dim)
    ar = row[:, None] * inv_freq[None, :]
    ac = col[:, None] * inv_freq[None, :]
    cos = jnp.concatenate([jnp.cos(ar), jnp.cos(ar), jnp.cos(ac), jnp.cos(ac)], axis=-1)
    sin = jnp.concatenate([-jnp.sin(ar), jnp.sin(ar), -jnp.sin(ac), jnp.sin(ac)], axis=-1)
    return cos, sin


QKV_TM = 512
QKV_TN = 1024
PAIR = 2 * HEAD_DIM


QKV_AHEAD = 2


def _qkv_kernel(dest_ref, x_ref, gp_ref, ys_hbm, g_ref, sh_ref, sc_ref, w_ref, qn_ref, kn_ref, cos_ref, sin_ref,
                q_ref, k_ref, v_ref, nk_ref, nv_ref, xn_ref, h_scr, ybuf, sems, *, use_norm):
    tm = QKV_TM
    i, j = pl.program_id(0), pl.program_id(1)
    n_tiles = pl.num_programs(0)
    n_buf = QKV_AHEAD + 1

    @pl.when(j == 0)
    def _():
        @pl.when(i == 0)
        def _():
            for t in range(QKV_AHEAD):
                _gather_rows(dest_ref, t * tm, ys_hbm, ybuf.at[t], sems.at[t], tm)

        slot = i % n_buf
        _wait_rows(ys_hbm, ybuf.at[slot], sems.at[slot], tm)
        x = x_ref[...] + gp_ref[...] * _unpack_bf16_pairs(ybuf[slot])
        xn_ref[...] = x
        h_scr[...] = _norm_mod(x, g_ref[...], sh_ref[...], sc_ref[...]).astype(BF16)

    def gather_ahead():
        ahead = (i + QKV_AHEAD) % n_buf
        _gather_rows(dest_ref, jnp.minimum(i + QKV_AHEAD, n_tiles - 1) * tm, ys_hbm, ybuf.at[ahead],
                     sems.at[ahead], tm, static=True)

    quarter = HEAD_DIM // 4
    scale = HEAD_DIM ** -0.5 * LOG2E

    def head(xh, gn):
        if use_norm:
            xh = xh * lax.rsqrt(jnp.mean(xh * xh, axis=-1, keepdims=True) + NORM_EPS) * gn
        return xh

    def rope(xh):
        lane = lax.broadcasted_iota(jnp.int32, (tm, HEAD_DIM), 1)
        first = (lane % (2 * quarter)) < quarter
        partner = jnp.where(first, pltpu.roll(xh, HEAD_DIM - quarter, 1), pltpu.roll(xh, quarter, 1))
        return xh * cos_ref[...] + partner * sin_ref[...]

    def proj(c0):
        return _dot(h_scr[...], w_ref[:, c0:c0 + PAIR])

    def q_tile(latent):
        for p in range(QKV_TN // PAIR):
            acc = proj(p * PAIR)
            for t in range(2):
                xh = head(acc[:, t * HEAD_DIM:(t + 1) * HEAD_DIM], qn_ref[...])
                xh = rope(xh) if latent else xh
                c0 = p * PAIR + t * HEAD_DIM
                q_ref[:, c0:c0 + HEAD_DIM] = (xh * scale).astype(BF16)

    def kv_tile(latent):
        gather_ahead()
        for p in range(KV_DIM // PAIR):
            acc = proj(p * PAIR)
            for t in range(2):
                c0 = p * PAIR + t * HEAD_DIM
                kh = head(acc[:, t * HEAD_DIM:(t + 1) * HEAD_DIM], kn_ref[...])
                if not latent:
                    nk_ref[:, c0:c0 + HEAD_DIM] = kh
                k_ref[:, c0:c0 + HEAD_DIM] = (rope(kh) if latent else kh).astype(BF16)
        for p in range(KV_DIM // PAIR):
            acc = proj(KV_DIM + p * PAIR)
            if not latent:
                nv_ref[:, p * PAIR:(p + 1) * PAIR] = acc
            v_ref[:, p * PAIR:(p + 1) * PAIR] = acc.astype(BF16)

    is_ctx = i < TP // tm
    is_q = j < D_MODEL // QKV_TN
    for latent in (False, True):
        stream = jnp.logical_not(is_ctx) if latent else is_ctx
        pl.when(jnp.logical_and(stream, is_q))(functools.partial(q_tile, latent))
        pl.when(jnp.logical_and(stream, jnp.logical_not(is_q)))(functools.partial(kv_tile, latent))

    @pl.when(jnp.logical_and(i == n_tiles - 1, j == pl.num_programs(1) - 1))
    def _():
        for t in range(1, QKV_AHEAD + 1):
            late = (i + t) % n_buf
            _wait_rows(ys_hbm, ybuf.at[late], sems.at[late], tm)


def _qkv_proj(x_prev, dest, ys, mod_prev, g, mod, w_qkv, q_norm, k_norm, use_norm, rope):
    tm, tn = QKV_TM, QKV_TN
    n_ctx = TP // tm
    n_q = D_MODEL // tn
    tab = lambda: pl.BlockSpec((tm, HEAD_DIM), lambda i, j, d: (jnp.maximum(i - n_ctx, 0) % (DEC_SEQ // tm), 0))
    kv = lambda: pl.BlockSpec((tm, KV_DIM), lambda i, j, d: (i, 0))
    new = lambda: pl.BlockSpec((tm, KV_DIM), lambda i, j, d: (jnp.minimum(i, n_ctx - 1), 0))
    row = lambda: pl.BlockSpec((tm, D_MODEL), lambda i, j, d: (i, 0))
    return pl.pallas_call(
        functools.partial(_qkv_kernel, use_norm=use_norm),
        grid_spec=pltpu.PrefetchScalarGridSpec(
            num_scalar_prefetch=1,
            grid=(T // tm, QKV_DIM // tn),
            in_specs=[
                row(),
                pl.BlockSpec((None, None, 1, D_MODEL), lambda i, j, d: (_cond_row(i * tm), 5, 0, 0)),
                pl.BlockSpec(memory_space=pl.ANY),
                pl.BlockSpec((1, D_MODEL), lambda i, j, d: (0, 0)),
                _mod_spec(tm, 0),
                _mod_spec(tm, 1),
                pl.BlockSpec((D_MODEL, tn), lambda i, j, d: (0, j)),
                pl.BlockSpec((1, HEAD_DIM), lambda i, j, d: (0, 0)),
                pl.BlockSpec((1, HEAD_DIM), lambda i, j, d: (0, 0)),
                tab(), tab(),
            ],
            out_specs=[pl.BlockSpec((tm, tn), lambda i, j, d: (i, jnp.minimum(j, n_q - 1))), kv(), kv(), new(), new(),
                       row()],
            scratch_shapes=[pltpu.VMEM((tm, D_MODEL), BF16), pltpu.VMEM((QKV_AHEAD + 1, tm, HALF_D), jnp.uint32),
                            pltpu.SemaphoreType.DMA((QKV_AHEAD + 1,))],
        ),
        out_shape=[jax.ShapeDtypeStruct((T, D_MODEL), BF16), jax.ShapeDtypeStruct((T, KV_DIM), BF16),
                   jax.ShapeDtypeStruct((T, KV_DIM), BF16), jax.ShapeDtypeStruct((TP, KV_DIM), F32),
                   jax.ShapeDtypeStruct((TP, KV_DIM), F32), jax.ShapeDtypeStruct((T, D_MODEL), F32)],
        compiler_params=_cp(("arbitrary", "arbitrary"), 56),
        name="qkv_proj",
    )(dest, x_prev, mod_prev, ys, g, mod, mod, w_qkv, q_norm.reshape(1, HEAD_DIM), k_norm.reshape(1, HEAD_DIM),
      *rope)


LOG2E = math.log2(math.e)
ATTN_TQ = 256


def _attn_kernel(*refs, tq, seq_len, has_ctx, windowed, has_sink):
    it = iter(refs)
    q_ref, k_ref, v_ref = next(it), next(it), next(it)
    kc_ref, vc_ref = (next(it), next(it)) if has_ctx else (None, None)
    sink_ref = next(it) if has_sink else None
    o_ref = next(it)
    if windowed:
        i = pl.program_id(2)
        span = tq + 2 * WINDOW
        start = pl.multiple_of(jnp.clip(i * tq - WINDOW, 0, seq_len - span), WINDOW)
        qpos = i * tq + lax.broadcasted_iota(jnp.int32, (tq, 1), 0)
        kpos = start + lax.broadcasted_iota(jnp.int32, (1, span), 1)
        segs = [(k_ref, v_ref, pl.ds(start, span), jnp.abs(kpos - qpos) <= WINDOW)]
    else:
        segs = [(k_ref, v_ref, slice(None), None)]
    if has_ctx:
        segs.append((kc_ref, vc_ref, slice(None), None))
    for h in range(KV_GROUP):
        hs = slice(h * HEAD_DIM, (h + 1) * HEAD_DIM)
        qh = q_ref[:, hs]
        scores = []
        m = None
        for kr, _, rows, mask in segs:
            s = lax.dot_general(qh, kr[rows, :], (((1,), (1,)), ((), ())), preferred_element_type=F32)
            if mask is not None:
                s = jnp.where(mask, s, NEG_INF)
            scores.append(s)
            ms = jnp.max(s, axis=-1, keepdims=True)
            m = ms if m is None else jnp.maximum(m, ms)
        if has_sink:
            sk = sink_ref[pl.program_id(1) * KV_GROUP + h]
            m = jnp.maximum(m, sk)
        l = jnp.exp2(sk - m) if has_sink else jnp.zeros_like(m)
        acc = jnp.zeros((tq, HEAD_DIM), F32)
        for (_, vr, rows, _), s in zip(segs, scores):
            p = jnp.exp2(s - m)
            l = l + jnp.sum(p, axis=-1, keepdims=True)
            acc = acc + _dot(p.astype(BF16), vr[rows, :])
        o_ref[:, hs] = (acc / l).astype(BF16)


def _attention(q, row0, k, v, k_row0, n_keys, k_ctx, v_ctx, sink, nb, L, windowed):
    tq = min(ATTN_TQ, L)
    nq = L // tq
    q_blk0 = row0 // tq
    seq0 = k_row0 // n_keys
    own = lambda: pl.BlockSpec((n_keys, HEAD_DIM), lambda b, g, i: (seq0 + b, g))
    in_specs = [pl.BlockSpec((tq, KV_GROUP * HEAD_DIM), lambda b, g, i: (q_blk0 + b * nq + i, g)), own(), own()]
    args = [q, k, v]
    if k_ctx is not None:
        ctx = lambda: pl.BlockSpec((None, PAST_LEN, HEAD_DIM), lambda b, g, i: (b, 0, g))
        in_specs += [ctx(), ctx()]
        args += [k_ctx, v_ctx]
    if sink is not None:
        in_specs.append(pl.BlockSpec(memory_space=pltpu.SMEM))
        args.append(sink.astype(F32) * LOG2E)
    return pl.pallas_call(
        functools.partial(_attn_kernel, tq=tq, seq_len=L, has_ctx=k_ctx is not None, windowed=windowed,
                          has_sink=sink is not None),
        grid=(nb, N_KV_HEADS, nq),
        in_specs=in_specs,
        out_specs=pl.BlockSpec((tq, KV_GROUP * HEAD_DIM), lambda b, g, i: (b * nq + i, g)),
        out_shape=jax.ShapeDtypeStruct((nb * L, D_MODEL), BF16),
        compiler_params=_cp(("parallel", "parallel", "parallel"), 56),
        name="attention",
    )(*args)


def _attn_mixer(pending, g, mod, w_qkv, q_norm, k_norm, use_norm, sink, w_o, cache_k, cache_v, windowed, rope):
    q, k, v, new_k, new_v, x = _qkv_proj(*pending, g, mod, w_qkv.astype(BF16), q_norm, k_norm, use_norm, rope)
    op = _attention(q, 0, k, v, 0, SEQ, None, None, sink, BATCH, SEQ, False)
    kc = cache_k.reshape(DEC_BATCH, PAST_LEN, KV_DIM).astype(BF16)
    vc = cache_v.reshape(DEC_BATCH, PAST_LEN, KV_DIM).astype(BF16)
    if windowed:
        osm = _attention(q, TP, k, v, TP, DEC_SEQ, kc, vc, sink, DEC_BATCH, DEC_SEQ, True)
    else:
        n_keys = DEC_SEQ + PAST_LEN
        both = lambda a, c: jnp.concatenate([a[TP:].reshape(DEC_BATCH, DEC_SEQ, KV_DIM), c], axis=1).reshape(
            DEC_BATCH * n_keys, KV_DIM)
        osm = _attention(q, TP, both(k, kc), both(v, vc), 0, n_keys, None, None, sink, DEC_BATCH, DEC_SEQ, False)
    x = _resid_matmul(op, osm, w_o.astype(BF16), jnp.zeros((1, D_MODEL), F32), x, mod, "attn_out_proj")
    shape = (BATCH, SEQ, N_KV_HEADS, HEAD_DIM)
    return x, new_k.reshape(shape), new_v.reshape(shape)


ROUTE_TILE = 512
ROUTE_ROWS = 32


def _router_kernel(x_ref, g_ref, sh_ref, sc_ref, wr_ref, br_ref, xh_ref, rt_ref, cnt_ref, carry):
    tm = ROUTE_TILE
    i = pl.program_id(0)

    @pl.when(i == 0)
    def _():
        carry[...] = jnp.zeros_like(carry)

    h = _norm_mod(x_ref[...], g_ref[...], sh_ref[...], sc_ref[...])
    xh_ref[:, :HALF_D] = _pack_bf16_pairs(h)
    logits = _dot(h.astype(BF16), wr_ref[...])
    s = _sigmoid(logits.T[:N_EXPERTS, :])
    sb = s + br_ref[...]
    u = [s[e:e + 1, :] for e in range(N_EXPERTS)]
    v = [sb[e:e + 1, :] for e in range(N_EXPERTS)]

    gscore = []
    for gq in range(N_EXPERT_GROUPS):
        m = v[4 * gq:4 * gq + 4]
        best = m[PAIR_LO[0]] + m[PAIR_HI[0]]
        for a, b in zip(PAIR_LO[1:], PAIR_HI[1:]):
            best = jnp.maximum(best, m[a] + m[b])
        gscore.append(best)
    gidx = jnp.zeros((1, tm), jnp.int32)
    gbest = gscore[0]
    for gq in range(1, N_EXPERT_GROUPS):
        upd = gscore[gq] > gbest
        gidx = jnp.where(upd, gq, gidx)
        gbest = jnp.where(upd, gscore[gq], gbest)

    def pick(rows, j):
        out = rows[j]
        for gq in range(1, N_EXPERT_GROUPS):
            out = jnp.where(gidx == gq, rows[4 * gq + j], out)
        return out

    vin = [pick(v, j) for j in range(EXPERTS_PER_GROUP)]
    uin = [pick(u, j) for j in range(EXPERTS_PER_GROUP)]
    i1 = jnp.zeros((1, tm), jnp.int32)
    m1 = vin[0]
    for j in range(1, EXPERTS_PER_GROUP):
        upd = vin[j] > m1
        i1 = jnp.where(upd, j, i1)
        m1 = jnp.where(upd, vin[j], m1)
    i2 = jnp.full((1, tm), -1, jnp.int32)
    m2 = jnp.full((1, tm), -jnp.inf, F32)
    for j in range(EXPERTS_PER_GROUP):
        upd = (i1 != j) & (vin[j] > m2)
        i2 = jnp.where(upd, j, i2)
        m2 = jnp.where(upd, vin[j], m2)

    def sel(rows, idx):
        out = rows[0]
        for j in range(1, EXPERTS_PER_GROUP):
            out = jnp.where(idx == j, rows[j], out)
        return out

    w1, w2 = sel(uin, i1), sel(uin, i2)
    wsum = w1 + w2
    w1, w2 = w1 / wsum, w2 / wsum
    first_lo = i1 < i2
    lo = jnp.where(first_lo, i1, i2)
    hi = jnp.where(first_lo, i2, i1)
    w_lo = jnp.where(first_lo, w1, w2)
    w_hi = jnp.where(first_lo, w2, w1)
    pair = jnp.where(lo == 0, hi - 1, jnp.where(lo == 1, hi + 1, 5))
    bucket = gidx * len(PAIR_LO) + pair

    onehot = (lax.broadcasted_iota(jnp.int32, (ROUTE_ROWS, tm), 0) == bucket)
    tri = (lax.broadcasted_iota(jnp.int32, (tm, tm), 0) <= lax.broadcasted_iota(jnp.int32, (tm, tm), 1))
    cum = _dot(jnp.where(onehot, 1.0, 0.0).astype(BF16), jnp.where(tri, 1.0, 0.0).astype(BF16))
    rank = jnp.sum(jnp.where(onehot, cum - 1.0 + carry[...], 0.0), axis=0, keepdims=True)
    carry[...] = carry[...] + cum[:, tm - 1:tm]
    cnt_ref[...] = jnp.broadcast_to(carry[...], (ROUTE_ROWS, LANE))

    rt_ref[...] = jnp.zeros_like(rt_ref)
    rt_ref[0:1, :] = bucket.astype(F32)
    rt_ref[1:2, :] = rank
    wt = jnp.concatenate([w_lo, w_hi, jnp.zeros((LANE - 2, tm), F32)], axis=0)
    xh_ref[:, HALF_D:] = lax.bitcast_convert_type(wt.T, jnp.uint32)


def _router(x, g, mod, w_router, b_router):
    tm = ROUTE_TILE
    wr = _pad2(w_router, D_MODEL, LANE).astype(BF16)
    return pl.pallas_call(
        _router_kernel,
        grid=(T // tm,),
        in_specs=[
            pl.BlockSpec((tm, D_MODEL), lambda i: (i, 0)),
            pl.BlockSpec((1, D_MODEL), lambda i: (0, 0)),
            _mod_spec(tm, 3),
            _mod_spec(tm, 4),
            pl.BlockSpec((D_MODEL, LANE), lambda i: (0, 0)),
            pl.BlockSpec((N_EXPERTS, 1), lambda i: (0, 0)),
        ],
        out_specs=[
            pl.BlockSpec((tm, XH_W), lambda i: (i, 0)),
            pl.BlockSpec((8, tm), lambda i: (0, i)),
            pl.BlockSpec((ROUTE_ROWS, LANE), lambda i: (0, 0)),
        ],
        out_shape=[
            jax.ShapeDtypeStruct((T, XH_W), jnp.uint32),
            jax.ShapeDtypeStruct((8, T), F32),
            jax.ShapeDtypeStruct((ROUTE_ROWS, LANE), F32),
        ],
        scratch_shapes=[pltpu.VMEM((ROUTE_ROWS, 1), F32)],
        compiler_params=_cp(("arbitrary",)),
        name="moe_router",
    )(x, g, mod, mod, wr, b_router.reshape(N_EXPERTS, 1))


DISPATCH_TILE = 256


DMA_UNROLL = 32


def _invert_kernel(dest_ref, src_ref):
    def clear(s, c):
        src_ref[s] = 0
        return c

    def put(t, c):
        src_ref[dest_ref[t]] = t
        return c

    lax.fori_loop(0, T_PAD, clear, 0, unroll=DMA_UNROLL)
    lax.fori_loop(0, T, put, 0, unroll=DMA_UNROLL)


def _invert(dest):
    return pl.pallas_call(
        _invert_kernel,
        in_specs=[pl.BlockSpec(memory_space=pltpu.SMEM)],
        out_specs=pl.BlockSpec(memory_space=pltpu.SMEM),
        out_shape=jax.ShapeDtypeStruct((T_PAD,), jnp.int32),
        name="moe_invert",
    )(dest)


def _gather_rows(idx_ref, base, src_hbm, buf, sem, tm, static=False):
    def start(r, c):
        pltpu.make_async_copy(src_hbm.at[pl.ds(idx_ref[base + r], 1)], buf.at[pl.ds(r, 1)], sem).start()
        return c

    if static:
        for r in range(tm):
            start(r, 0)
    else:
        lax.fori_loop(0, tm, start, 0, unroll=DMA_UNROLL)


def _wait_rows(src_hbm, buf, sem, tm):
    pltpu.make_async_copy(src_hbm.at[pl.ds(0, tm)], buf, sem).wait()


def _expert_kernel(ea_ref, eb_ref, nv_ref, src_ref, xh_hbm, ga_ref, ua_ref, da_ref, gb_ref, ub_ref, db_ref, y_ref,
                   *scratch):
    tm = MOE_TILE
    j = pl.program_id(0)
    nv = nv_ref[j]
    *bufs, sems = scratch
    n_buf = len(bufs)

    @pl.when(j == 0)
    def _():
        for t in range(MOE_AHEAD):
            _gather_rows(src_ref, t * tm, xh_hbm, bufs[t], sems.at[t], tm)

    def run(p):
        cur, cur_sem = bufs[p], sems.at[p]
        q = (p + MOE_AHEAD) % n_buf
        ahead, ahead_sem = bufs[q], sems.at[q]

        @pl.when(jnp.logical_or(j < MOE_AHEAD, nv_ref[jnp.maximum(j - MOE_AHEAD, 0)] > 0))
        def _():
            _wait_rows(xh_hbm, cur, cur_sem, tm)

        @pl.when(nv > 0)
        def _():
            valid = lax.broadcasted_iota(jnp.int32, (tm, 1), 0) < nv
            x = jnp.where(valid, _unpack_bf16_pairs(cur[:, :HALF_D]), 0.0).astype(BF16)
            wts = jnp.where(valid, lax.bitcast_convert_type(cur[:, HALF_D:], F32), 0.0)
            for r in range(tm):
                pltpu.make_async_copy(xh_hbm.at[pl.ds(src_ref[(j + MOE_AHEAD) * tm + r], 1)],
                                      ahead.at[pl.ds(r, 1)], ahead_sem).start()

            def ffn(g_ref, u_ref, d_ref, w):
                a = _dot(x, g_ref[...])
                h = a * _sigmoid(a) * _dot(x, u_ref[...]) * w
                return _dot(h.astype(BF16), d_ref[...])

            y = ffn(ga_ref, ua_ref, da_ref, wts[:, 0:1]) + ffn(gb_ref, ub_ref, db_ref, wts[:, 1:2])
            y_ref[...] = _pack_bf16_pairs(y)

    for p in range(n_buf):
        pl.when(j % n_buf == p)(functools.partial(run, p))

    @pl.when(nv == 0)
    def _():
        y_ref[...] = jnp.zeros_like(y_ref)


def _experts(layer, tile_ea, tile_eb, tile_nv, src, xh, w_gate, w_up, w_down):
    tm = MOE_TILE
    up = lambda sel: pl.BlockSpec((None, None, D_MODEL, D_EXPERT),
                                  lambda j, ea, eb, nv, sr: (layer, (ea, eb)[sel][j], 0, 0))
    down = lambda sel: pl.BlockSpec((None, None, D_EXPERT, D_MODEL),
                                    lambda j, ea, eb, nv, sr: (layer, (ea, eb)[sel][j], 0, 0))
    return pl.pallas_call(
        _expert_kernel,
        grid_spec=pltpu.PrefetchScalarGridSpec(
            num_scalar_prefetch=4,
            grid=(MOE_TILES,),
            in_specs=[pl.BlockSpec(memory_space=pl.ANY), up(0), up(0), down(0), up(1), up(1), down(1)],
            out_specs=pl.BlockSpec((tm, HALF_D), lambda j, ea, eb, nv, sr: (j, 0)),
            scratch_shapes=[pltpu.VMEM((tm, XH_W), jnp.uint32)] * (MOE_AHEAD + 1)
            + [pltpu.SemaphoreType.DMA((MOE_AHEAD + 1,))],
        ),
        out_shape=jax.ShapeDtypeStruct((T_PAD, HALF_D), jnp.uint32),
        compiler_params=_cp(("arbitrary",), 56),
        name="moe_experts",
    )(tile_ea, tile_eb, tile_nv, src, xh, w_gate, w_up, w_down, w_gate, w_up, w_down)


def _combine_kernel(*refs, final):
    tm = DISPATCH_TILE
    if final:
        dest_ref, x_ref, gt_ref, ys_hbm, fg_ref, oc_ref, ol_ref, buf, sems = refs
    else:
        dest_ref, x_ref, gt_ref, ys_hbm, o_ref, buf, sems = refs
    i = pl.program_id(0)
    slot = i % 2

    last = pl.num_programs(0) - 1

    @pl.when(i == 0)
    def _():
        _gather_rows(dest_ref, 0, ys_hbm, buf.at[0], sems.at[0], tm)

    def finish():
        _wait_rows(ys_hbm, buf.at[slot], sems.at[slot], tm)
        x = x_ref[...] + gt_ref[...] * _unpack_bf16_pairs(buf[slot])
        if not final:
            o_ref[...] = x
            return
        y = x * lax.rsqrt(jnp.mean(x * x, axis=-1, keepdims=True) + NORM_EPS) * fg_ref[...]
        is_ctx = i < TP // tm

        @pl.when(is_ctx)
        def _():
            oc_ref[...] = y

        @pl.when(jnp.logical_not(is_ctx))
        def _():
            ol_ref[...] = y

    @pl.when(i < last)
    def _():
        _gather_rows(dest_ref, (i + 1) * tm, ys_hbm, buf.at[1 - slot], sems.at[1 - slot], tm, static=True)
        finish()

    pl.when(i == last)(finish)


def _combine(dest, x, mod, ys, final_gain=None):
    tm = DISPATCH_TILE
    final = final_gain is not None
    n_ctx = TP // tm
    in_specs = [pl.BlockSpec((tm, D_MODEL), lambda i, d: (i, 0)),
                pl.BlockSpec((None, None, 1, D_MODEL), lambda i, d: (_cond_row(i * tm), 5, 0, 0)),
                pl.BlockSpec(memory_space=pl.ANY)]
    args = [dest, x, mod, ys]
    if final:
        in_specs.append(pl.BlockSpec((1, D_MODEL), lambda i, d: (0, 0)))
        args.append(final_gain.reshape(1, D_MODEL))
        out_specs = [pl.BlockSpec((tm, D_MODEL), lambda i, d: (jnp.minimum(i, n_ctx - 1), 0)),
                     pl.BlockSpec((tm, D_MODEL), lambda i, d: (jnp.maximum(i - n_ctx, 0), 0))]
        out_shape = [jax.ShapeDtypeStruct((TP, D_MODEL), F32), jax.ShapeDtypeStruct((TS, D_MODEL), F32)]
    else:
        out_specs = pl.BlockSpec((tm, D_MODEL), lambda i, d: (i, 0))
        out_shape = jax.ShapeDtypeStruct((T, D_MODEL), F32)
    return pl.pallas_call(
        functools.partial(_combine_kernel, final=final),
        grid_spec=pltpu.PrefetchScalarGridSpec(
            num_scalar_prefetch=1,
            grid=(T // tm,),
            in_specs=in_specs,
            out_specs=out_specs,
            scratch_shapes=[pltpu.VMEM((2, tm, HALF_D), jnp.uint32), pltpu.SemaphoreType.DMA((2,))],
        ),
        out_shape=out_shape,
        compiler_params=_cp(("arbitrary",)),
        name="moe_combine",
    )(*args)


def _lookup(table, idx):
    n = table.shape[0]
    hit = idx[:, None] == jnp.arange(n, dtype=jnp.int32)[None, :]
    return jnp.sum(jnp.where(hit, table[None, :], 0), axis=1)


def _moe_plan(rt, cnt):
    bucket = rt[0].astype(jnp.int32)
    rank = rt[1].astype(jnp.int32)
    counts = cnt[:N_BUCKETS, 0].astype(jnp.int32)
    tiles = (counts + MOE_TILE - 1) // MOE_TILE
    order = jnp.arange(N_BUCKETS, dtype=jnp.int32)
    tile_start = jnp.sum(jnp.where(order[None, :] < order[:, None], tiles[None, :], 0), axis=1)
    tile_end = tile_start + tiles
    n_used = tile_end[N_BUCKETS - 1]
    dest = _lookup(tile_start * MOE_TILE, bucket) + rank
    j = jnp.arange(MOE_TILES, dtype=jnp.int32)
    jc = jnp.minimum(j, n_used - 1)
    b = jnp.minimum(jnp.sum((jc[:, None] >= tile_end[None, :]).astype(jnp.int32), axis=1), N_BUCKETS - 1)
    nv = jnp.clip(_lookup(counts, b) - (j - _lookup(tile_start, b)) * MOE_TILE, 0, MOE_TILE)
    nv = jnp.where(j < n_used, nv, 0)
    n_pairs = len(PAIR_LO)
    ea = (b // n_pairs) * EXPERTS_PER_GROUP + _lookup(jnp.asarray(PAIR_LO, jnp.int32), b % n_pairs)
    eb = (b // n_pairs) * EXPERTS_PER_GROUP + _lookup(jnp.asarray(PAIR_HI, jnp.int32), b % n_pairs)
    return dest, ea, eb, nv


def _moe(layer, x, g, mod, w_router, b_router, w_gate, w_up, w_down, final_gain=None, defer=False):
    xh, rt, cnt = _router(x, g, mod, w_router, b_router)
    dest, ea, eb, nv = _moe_plan(rt, cnt)
    ys = _experts(layer, ea, eb, nv, _invert(dest), xh, w_gate, w_up, w_down)
    if defer:
        return x, dest, ys, mod
    return _combine(dest, x, mod, ys, final_gain)


def kernel(x_prompt, x_sample, cache_k_full, cache_v_full, cache_k_win, cache_v_win, c, c_ctx, w_mod, b_mod, norm_mix, norm_ffn, final_norm, pool_w, pool_scale, hy_w_in, hy_b_in, hy_conv_w, hy_conv_b, hy_f_w1, hy_f_b1, hy_f_w2, hy_f_b2, hy_f_freq, hy_f_w3, hy_decay, hy_skip, hy_w_out, hy_b_out, fa_w_qkv, fa_q_norm, fa_k_norm, fa_w_o, wa_w_qkv, wa_sink, wa_w_o, w_router, b_router, moe_w_gate, moe_w_up, moe_w_down):
    x = None
    cond =jnp.concatenate([c_ctx[None, :], c, jnp.zeros((N_COND - 1 - DEC_BATCH, D_MODEL), F32)], axis=0)
    mods = _adaln(cond, w_mod, b_mod).reshape(DEPTH, N_COND, 6, 1, D_MODEL)
    rope = _rope_tables()
    ones_hd = jnp.ones((HEAD_DIM,), F32)
    wg_bf, wu_bf, wd_bf = moe_w_gate.astype(BF16), moe_w_up.astype(BF16), moe_w_down.astype(BF16)
    new_kv = {}
    for layer in range(DEPTH):
        kind = layer % 4
        j = layer // 4
        mod = mods[layer]
        g_mix = norm_mix[layer].reshape(1, D_MODEL)
        if kind == 0:
            assert layer == 0, "the pooling mixer reads the two input streams, so it must be the first layer"
            x = _pool_mixer(x_prompt.reshape(TP, D_MODEL), x_sample.reshape(TS, D_MODEL), g_mix, mod,
                            pool_w[j], pool_scale[j])
        elif kind == 1:
            x = _hyena_mixer(x, g_mix, mod, hy_w_in[j], hy_b_in[j], hy_conv_w[j], hy_conv_b[j], hy_f_w1[j],
                             hy_f_b1[j], hy_f_w2[j], hy_f_b2[j], hy_f_freq[j], hy_f_w3[j], hy_decay[j],
                             hy_skip[j], hy_w_out[j], hy_b_out[j])
        elif kind == 2:
            x, nk, nv = _attn_mixer(x, g_mix, mod, fa_w_qkv[j], fa_q_norm[j], fa_k_norm[j], True, None,
                                    fa_w_o[j], cache_k_full[:, j], cache_v_full[:, j], False, rope)
            new_kv.setdefault("kf", []).append(nk)
            new_kv.setdefault("vf", []).append(nv)
        else:
            x, nk, nv = _attn_mixer(x, g_mix, mod, wa_w_qkv[j], ones_hd, ones_hd, False, wa_sink[j],
                                    wa_w_o[j], cache_k_win[:, j], cache_v_win[:, j], True, rope)
            new_kv.setdefault("kw", []).append(nk)
            new_kv.setdefault("vw", []).append(nv)
        defer = layer + 1 < DEPTH and (layer + 1) % 4 in (2, 3)
        x = _moe(layer, x, norm_ffn[layer].reshape(1, D_MODEL), mod, w_router, b_router, wg_bf, wu_bf, wd_bf,
                 final_norm if layer == DEPTH - 1 else None, defer)
    y_prompt, y_sample = x
    y_prompt = y_prompt.reshape(BATCH, SEQ, D_MODEL)
    y_sample = y_sample.reshape(DEC_BATCH, DEC_SEQ, D_MODEL)
    return (y_prompt, y_sample, jnp.stack(new_kv["kf"], axis=1), jnp.stack(new_kv["vf"], axis=1),
            jnp.stack(new_kv["kw"], axis=1), jnp.stack(new_kv["vw"], axis=1))
```

```python
import functools
import math

import jax
import jax.numpy as jnp
import numpy as np
from jax import lax
from jax.experimental import pallas as pl
from jax.experimental.pallas import tpu as pltpu

D_MODEL = 2048
BATCH = 32
SEQ = 256
DEPTH = 4
DEC_BATCH = 4
DEC_SEQ = 4096
PAST_LEN = 512
GRID_W = 64
N_HEADS = 16
N_KV_HEADS = 4
HEAD_DIM = D_MODEL // N_HEADS
KV_GROUP = N_HEADS // N_KV_HEADS
KV_DIM = N_KV_HEADS * HEAD_DIM
QKV_DIM = (N_HEADS + 2 * N_KV_HEADS) * HEAD_DIM
ROPE_THETA = 10000.0
WINDOW = 128
POOL_WINDOWS = (2, 4, 8, 16)
POOL_GROUP = D_MODEL // len(POOL_WINDOWS)
HYENA_EMB_BANDS = 16
HYENA_FILTER_HIDDEN = 64
N_EXPERTS = 16
N_EXPERT_GROUPS = 4
EXPERTS_PER_GROUP = 4
D_EXPERT = 512
NORM_EPS = 1e-6
NEG_INF = -1e30

F32 = jnp.float32
BF16 = jnp.bfloat16

TP = BATCH * SEQ
TS = DEC_BATCH * DEC_SEQ
T = TP + TS
N_COND = 8
LANE = 128
MIB = 1024 * 1024

PAIR_LO = (0, 0, 0, 1, 1, 2)
PAIR_HI = (1, 2, 3, 2, 3, 3)
N_BUCKETS = N_EXPERT_GROUPS * len(PAIR_LO)
MOE_TILE = 256
MOE_AHEAD = 3
MOE_TILES = T // MOE_TILE + N_BUCKETS + MOE_AHEAD
T_PAD = MOE_TILES * MOE_TILE
HALF_D = D_MODEL // 2
XH_W = HALF_D + LANE


def _cp(sem, vmem_mb=48):
    return pltpu.CompilerParams(dimension_semantics=sem, vmem_limit_bytes=vmem_mb * MIB)


def _dot(a, b):
    return jnp.dot(a, b, preferred_element_type=F32)


def _dot3(a, b):
    ah = a.astype(BF16)
    al = (a - ah.astype(F32)).astype(BF16)
    bh = b.astype(BF16)
    bl = (b - bh.astype(F32)).astype(BF16)
    return _dot(ah, bh) + (_dot(al, bh) + _dot(ah, bl))


def _sigmoid(x):
    return 1.0 / (1.0 + jnp.exp(-x))


def _pack_bf16_pairs(x):
    n = x.shape[1] // 2
    bits = lambda v: lax.bitcast_convert_type(v.astype(BF16).astype(F32), jnp.uint32)
    return (bits(x[:, :n]) >> 16) | bits(x[:, n:])


def _unpack_bf16_pairs(u):
    lo = lax.bitcast_convert_type(u << 16, F32)
    hi = lax.bitcast_convert_type(u & jnp.uint32(0xFFFF0000), F32)
    return jnp.concatenate([lo, hi], axis=1)


def _cond_row(r):
    return jnp.where(r < TP, 0, 1 + (r - TP) // DEC_SEQ)


def _mod_spec(tm, chunk, tn=D_MODEL, ncol=False):
    if ncol:
        return pl.BlockSpec((None, None, 1, tn), lambda i, j: (_cond_row(i * tm), chunk, 0, j))
    return pl.BlockSpec((None, None, 1, tn), lambda i, *_: (_cond_row(i * tm), chunk, 0, 0))


def _norm_mod(x, g, shift, scale):
    var = jnp.mean(x * x, axis=-1, keepdims=True)
    y = x * lax.rsqrt(var + NORM_EPS) * g
    return y * (1.0 + scale) + shift


def _adaln_kernel(c_ref, w_ref, b_ref, o_ref):
    c = c_ref[...]
    a = c * _sigmoid(c)
    o_ref[...] = _dot3(a, w_ref[...]) + b_ref[...]


def _adaln(cond, w_mod, b_mod):
    tn = 1024
    n = 6 * D_MODEL
    return pl.pallas_call(
        _adaln_kernel,
        grid=(DEPTH, n // tn),
        in_specs=[
            pl.BlockSpec((N_COND, D_MODEL), lambda l, j: (0, 0)),
            pl.BlockSpec((None, D_MODEL, tn), lambda l, j: (l, 0, j)),
            pl.BlockSpec((None, 1, tn), lambda l, j: (l, 0, j)),
        ],
        out_specs=pl.BlockSpec((None, N_COND, tn), lambda l, j: (l, 0, j)),
        out_shape=jax.ShapeDtypeStruct((DEPTH, N_COND, n), F32),
        compiler_params=_cp(("parallel", "parallel")),
        name="adaln",
    )(cond, w_mod, b_mod.reshape(DEPTH, 1, n))


NM_TM = 512
NM_AHEAD = 2


def _nm_matmul_kernel(dest_ref, x_ref, gp_ref, ys_hbm, g_ref, sh_ref, sc_ref, w_ref, b_ref, o_ref, xn_ref,
                      h_scr, ybuf, sems):
    tm = NM_TM
    i, j = pl.program_id(0), pl.program_id(1)
    n_tiles, n_col = pl.num_programs(0), pl.num_programs(1)
    n_buf = NM_AHEAD + 1

    @pl.when(j == 0)
    def _():
        @pl.when(i == 0)
        def _():
            for t in range(NM_AHEAD):
                _gather_rows(dest_ref, t * tm, ys_hbm, ybuf.at[t], sems.at[t], tm)

        slot = i % n_buf
        _wait_rows(ys_hbm, ybuf.at[slot], sems.at[slot], tm)
        x = x_ref[...] + gp_ref[...] * _unpack_bf16_pairs(ybuf[slot])
        xn_ref[...] = x
        h_scr[...] = _norm_mod(x, g_ref[...], sh_ref[...], sc_ref[...]).astype(BF16)

    def project():
        o_ref[...] = (_dot(h_scr[...], w_ref[...]) + b_ref[...]).astype(o_ref.dtype)

    @pl.when(j == n_col - 1)
    def _():
        ahead = (i + NM_AHEAD) % n_buf
        _gather_rows(dest_ref, jnp.minimum(i + NM_AHEAD, n_tiles - 1) * tm, ys_hbm, ybuf.at[ahead], sems.at[ahead],
                     tm, static=True)
        project()

    pl.when(j != n_col - 1)(project)

    @pl.when(jnp.logical_and(i == n_tiles - 1, j == n_col - 1))
    def _():
        for t in range(1, NM_AHEAD + 1):
            late = (i + t) % n_buf
            _wait_rows(ys_hbm, ybuf.at[late], sems.at[late], tm)


def _nm_matmul(x_prev, dest, ys, mod_prev, g, mod, w, b, out_dtype, name):
    tm, tn = NM_TM, 1024
    n = w.shape[1]
    row = lambda: pl.BlockSpec((tm, D_MODEL), lambda i, j, d: (i, 0))
    return pl.pallas_call(
        _nm_matmul_kernel,
        grid_spec=pltpu.PrefetchScalarGridSpec(
            num_scalar_prefetch=1,
            grid=(T // tm, n // tn),
            in_specs=[
                row(),
                pl.BlockSpec((None, None, 1, D_MODEL), lambda i, j, d: (_cond_row(i * tm), 5, 0, 0)),
                pl.BlockSpec(memory_space=pl.ANY),
                pl.BlockSpec((1, D_MODEL), lambda i, j, d: (0, 0)),
                _mod_spec(tm, 0),
                _mod_spec(tm, 1),
                pl.BlockSpec((D_MODEL, tn), lambda i, j, d: (0, j)),
                pl.BlockSpec((1, tn), lambda i, j, d: (0, j)),
            ],
            out_specs=[pl.BlockSpec((tm, tn), lambda i, j, d: (i, j)), row()],
            scratch_shapes=[pltpu.VMEM((tm, D_MODEL), BF16), pltpu.VMEM((NM_AHEAD + 1, tm, HALF_D), jnp.uint32),
                            pltpu.SemaphoreType.DMA((NM_AHEAD + 1,))],
        ),
        out_shape=[jax.ShapeDtypeStruct((T, n), out_dtype), jax.ShapeDtypeStruct((T, D_MODEL), F32)],
        compiler_params=_cp(("arbitrary", "arbitrary"), 56),
        name=name,
    )(dest, x_prev, mod_prev, ys, g, mod, mod, w, b)


RESID_TM = 1024


def _resid_matmul_kernel(ap_ref, as_ref, w_ref, b_ref, x_ref, gt_ref, o_ref):
    def emit(a_ref):
        o_ref[...] = x_ref[...] + gt_ref[...] * (_dot(a_ref[...], w_ref[...]) + b_ref[...])

    is_ctx = pl.program_id(0) < TP // RESID_TM
    pl.when(is_ctx)(lambda: emit(ap_ref))
    pl.when(jnp.logical_not(is_ctx))(lambda: emit(as_ref))


def _resid_matmul(a_ctx, a_lat, w, b, x, mod, name):
    tm, tn = RESID_TM, 1024
    k = a_ctx.shape[1]
    n_ctx = TP // tm
    return pl.pallas_call(
        _resid_matmul_kernel,
        grid=(T // tm, D_MODEL // tn),
        in_specs=[
            pl.BlockSpec((tm, k), lambda i, j: (jnp.minimum(i, n_ctx - 1), 0)),
            pl.BlockSpec((tm, k), lambda i, j: (jnp.maximum(i - n_ctx, 0), 0)),
            pl.BlockSpec((k, tn), lambda i, j: (0, j)),
            pl.BlockSpec((1, tn), lambda i, j: (0, j)),
            pl.BlockSpec((tm, tn), lambda i, j: (i, j)),
            _mod_spec(tm, 2, tn, ncol=True),
        ],
        out_specs=pl.BlockSpec((tm, tn), lambda i, j: (i, j)),
        out_shape=jax.ShapeDtypeStruct((T, D_MODEL), F32),
        compiler_params=_cp(("parallel", "parallel")),
        name=name,
    )(a_ctx, a_lat, w, b, x, mod)


POOL_TILE = 256
POOL_HALO = 8


def _seq_pos(r0):
    is_ctx = r0 < TP
    loc0 = jnp.where(is_ctx, r0 % SEQ, (r0 - TP) % DEC_SEQ)
    seq_len = jnp.where(is_ctx, SEQ, DEC_SEQ)
    return loc0, seq_len


def _pool_kernel(xc_ref, xcp_ref, xcn_ref, xl_ref, xlp_ref, xln_ref, *rest):
    is_ctx = pl.program_id(0) < TP // POOL_TILE
    pl.when(is_ctx)(lambda: _pool_tile(xc_ref, xcp_ref, xcn_ref, *rest))
    pl.when(jnp.logical_not(is_ctx))(lambda: _pool_tile(xl_ref, xlp_ref, xln_ref, *rest))


def _pool_tile(x_ref, xp_ref, xn_ref, g_ref, sh_ref, sc_ref, gt_ref, pw_ref, ps_ref, o_ref, hz_scr):
    tm, hl = POOL_TILE, POOL_HALO
    loc0, seq_len = _seq_pos(pl.program_id(0) * tm)
    has_prev = loc0 > 0
    has_next = loc0 + tm < seq_len
    g, sh, sc = g_ref[...], sh_ref[...], sc_ref[...]
    x = x_ref[...]
    h = _norm_mod(x, g, sh, sc)
    hz_scr[0:hl, :] = jnp.where(has_prev, _norm_mod(xp_ref[...], g, sh, sc), 0.0)
    hz_scr[hl:hl + tm, :] = h
    hz_scr[hl + tm:, :] = jnp.where(has_next, _norm_mod(xn_ref[...], g, sh, sc), 0.0)
    tl = loc0 + lax.broadcasted_iota(jnp.int32, (tm, 1), 0)
    outs = []
    for gi, w in enumerate(POOL_WINDOWS):
        cs = slice(gi * POOL_GROUP, (gi + 1) * POOL_GROUP)
        s = jnp.zeros((tm, POOL_GROUP), F32)
        for off in range(-(w // 2), w - w // 2):
            s = s + hz_scr[hl + off:hl + off + tm, cs]
        lo = jnp.maximum(tl - w // 2, 0)
        hi = jnp.minimum(tl + (w - w // 2), seq_len)
        d = s / (hi - lo).astype(F32) - h[:, cs]
        outs.append(_dot(d.astype(BF16), pw_ref[gi]))
    out = jnp.concatenate(outs, axis=1) * ps_ref[...]
    o_ref[...] = x + gt_ref[...] * out


def _pool_mixer(x_ctx, x_lat, g, mod, pool_w, pool_scale):
    tm, hl = POOL_TILE, POOL_HALO
    r = tm // hl

    def stream(first_tile, rows):
        tile = lambda i: jnp.clip(i - first_tile, 0, rows // tm - 1)
        return [pl.BlockSpec((tm, D_MODEL), lambda i: (tile(i), 0)),
                pl.BlockSpec((hl, D_MODEL), lambda i: (jnp.maximum(tile(i) * r - 1, 0), 0)),
                pl.BlockSpec((hl, D_MODEL), lambda i: (jnp.minimum((tile(i) + 1) * r, rows // hl - 1), 0))]

    return pl.pallas_call(
        _pool_kernel,
        grid=(T // tm,),
        in_specs=stream(0, TP) + stream(TP // tm, TS) + [
            pl.BlockSpec((1, D_MODEL), lambda i: (0, 0)),
            _mod_spec(tm, 0),
            _mod_spec(tm, 1),
            _mod_spec(tm, 2),
            pl.BlockSpec((len(POOL_WINDOWS), POOL_GROUP, POOL_GROUP), lambda i: (0, 0, 0)),
            pl.BlockSpec((1, D_MODEL), lambda i: (0, 0)),
        ],
        out_specs=pl.BlockSpec((tm, D_MODEL), lambda i: (i, 0)),
        out_shape=jax.ShapeDtypeStruct((T, D_MODEL), F32),
        scratch_shapes=[pltpu.VMEM((tm + 2 * hl, D_MODEL), F32)],
        compiler_params=_cp(("parallel",)),
        name="pool_mixer",
    )(x_ctx, x_ctx, x_ctx, x_lat, x_lat, x_lat, g, mod, mod, mod, pool_w.astype(BF16),
      pool_scale.reshape(1, D_MODEL))


CONV_TILE = 256
CONV_HALO = 16


def _conv3_kernel(u_ref, up_ref, un_ref, cw_ref, cb_ref, o_ref, scr):
    tm, hl = CONV_TILE, CONV_HALO
    loc0, seq_len = _seq_pos(pl.program_id(0) * tm)
    has_prev = loc0 > 0
    has_next = loc0 + tm < seq_len
    scr[0:hl, :] = jnp.where(has_prev, up_ref[...].astype(F32), 0.0)
    scr[hl:hl + tm, :] = u_ref[...].astype(F32)
    scr[hl + tm:, :] = jnp.where(has_next, un_ref[...].astype(F32), 0.0)
    out = (scr[hl - 1:hl - 1 + tm, :] * cw_ref[0:1, :] + scr[hl:hl + tm, :] * cw_ref[1:2, :]
           + scr[hl + 1:hl + 1 + tm, :] * cw_ref[2:3, :] + cb_ref[...])
    o_ref[...] = out.astype(o_ref.dtype)


def _conv3(u0, conv_w, conv_b):
    tm, hl, tc = CONV_TILE, CONV_HALO, D_MODEL
    r = tm // hl
    n = u0.shape[1]
    return pl.pallas_call(
        _conv3_kernel,
        grid=(T // tm, n // tc),
        in_specs=[
            pl.BlockSpec((tm, tc), lambda i, j: (i, j)),
            pl.BlockSpec((hl, tc), lambda i, j: (jnp.maximum(i * r - 1, 0), j)),
            pl.BlockSpec((hl, tc), lambda i, j: (jnp.minimum((i + 1) * r, T // hl - 1), j)),
            pl.BlockSpec((3, tc), lambda i, j: (0, j)),
            pl.BlockSpec((1, tc), lambda i, j: (0, j)),
        ],
        out_specs=pl.BlockSpec((tm, tc), lambda i, j: (i, j)),
        out_shape=jax.ShapeDtypeStruct((T, n), BF16),
        scratch_shapes=[pltpu.VMEM((tm + 2 * hl, tc), F32)],
        compiler_params=_cp(("parallel", "parallel")),
        name="hyena_conv3",
    )(u0, u0, u0, conv_w, conv_b.reshape(1, n))


FILT_TILE = 256


HYENA_BLOCK = 1024


T_LANE = LANE - 1


def _filter_mlp_kernel(emb_ref, w1_ref, b1_ref, w2_ref, b2_ref, fr_ref, o_ref):
    emb = emb_ref[...]
    fr = fr_ref[...]
    a = jnp.sin(fr * (_dot3(emb, w1_ref[...]) + b1_ref[...]))
    a = jnp.sin(fr * (_dot3(a, w2_ref[...]) + b2_ref[...]))
    lane = lax.broadcasted_iota(jnp.int32, a.shape, 1)
    o_ref[...] = jnp.where(lane == T_LANE, emb[:, 0:1], a)


def _filter_mlp(pos, L, f_w1, f_b1, f_w2, f_b2, f_freq):
    assert HYENA_FILTER_HIDDEN <= T_LANE
    tl = FILT_TILE
    rows = pos.shape[0]
    small = lambda: pl.BlockSpec((LANE, LANE), lambda i: (0, 0))
    vec = lambda: pl.BlockSpec((1, LANE), lambda i: (0, 0))
    return pl.pallas_call(
        _filter_mlp_kernel,
        grid=(rows // tl,),
        in_specs=[pl.BlockSpec((tl, LANE), lambda i: (i, 0)), small(), vec(), small(), vec(), vec()],
        out_specs=pl.BlockSpec((tl, LANE), lambda i: (i, 0)),
        out_shape=jax.ShapeDtypeStruct((rows, LANE), F32),
        compiler_params=_cp(("parallel",)),
        name="hyena_filter_mlp",
    )(_filter_embedding(pos, L), _pad2(f_w1, LANE, LANE), _pad2(f_b1[None], 1, LANE),
      _pad2(f_w2, LANE, LANE), _pad2(f_b2[None], 1, LANE), _pad2(f_freq[None], 1, LANE))


def _filter_kernel(h1_ref, h2_ref, w3a_ref, dca_ref, w3b_ref, dcb_ref, fa_ref, fb_ref, *, blk):
    def taps(h, w3_ref, dc_ref):
        return _dot3(h, w3_ref[...]) * jnp.exp(-h[:, T_LANE:] * jnp.abs(dc_ref[...]))

    pos = taps(h1_ref[...], w3a_ref, dca_ref)
    neg = taps(h2_ref[...], w3b_ref, dcb_ref)
    m = (pl.program_id(0) * FILT_TILE + lax.broadcasted_iota(jnp.int32, (FILT_TILE, 1), 0)) % blk
    fa_ref[...] = jnp.where(m == 0, pos, pos + neg).astype(BF16)
    fb_ref[...] = jnp.where(m == 0, 0.0, neg - pos).astype(BF16)


def _pad2(a, rows, cols):
    return jnp.pad(a, ((0, rows - a.shape[0]), (0, cols - a.shape[1])))


def _filter_positions(L, blk):
    n_blk = L // blk
    m = np.arange(blk)
    p1, p2 = [], []
    for d in range(-(n_blk - 1), n_blk):
        if d >= 1:
            p1.append(d * blk + m), p2.append(d * blk - m)
        elif d == 0:
            p1.append(m), p2.append(m)
        else:
            p1.append(-d * blk - m), p2.append(-d * blk + m)
    return np.concatenate(p1), np.concatenate(p2)


def _filter_embedding(pos, L):
    t = jnp.asarray(pos, F32) / L
    bands = jnp.linspace(1e-4, HYENA_EMB_BANDS - 1, HYENA_EMB_BANDS, dtype=F32)
    ang = (2 * math.pi) * t[:, None] * bands[None, :]
    return _pad2(jnp.concatenate([t[:, None], jnp.cos(ang), -jnp.sin(ang)], axis=-1), pos.shape[0], LANE)


def _hyena_filters(L, blk, f_w1, f_b1, f_w2, f_b2, f_freq, f_w3, decay):
    n_blk = L // blk
    p1, p2 = _filter_positions(L, blk)
    rows = p1.shape[0]
    tl = FILT_TILE
    tiles_per_lag = blk // tl
    lag = lambda i: i // tiles_per_lag - (n_blk - 1)
    col1 = lambda i, n: 2 * n + jnp.where(lag(i) >= 0, 0, 1)
    col2 = lambda i, n: 2 * n + jnp.where(lag(i) >= 1, 0, 1)
    n_tiles = rows // tl
    out = pl.BlockSpec((None, tl, D_MODEL), lambda i, n: (n, i, 0))
    w3 = _pad2(f_w3, LANE, f_w3.shape[1])
    hidden = _filter_mlp(np.concatenate([p1, p2]), L, f_w1, f_b1, f_w2, f_b2, f_freq)
    return pl.pallas_call(
        functools.partial(_filter_kernel, blk=blk),
        grid=(n_tiles, 2),
        in_specs=[
            pl.BlockSpec((tl, LANE), lambda i, n: (i, 0)),
            pl.BlockSpec((tl, LANE), lambda i, n: (n_tiles + i, 0)),
            pl.BlockSpec((LANE, D_MODEL), lambda i, n: (0, col1(i, n))),
            pl.BlockSpec((1, D_MODEL), lambda i, n: (0, col1(i, n))),
            pl.BlockSpec((LANE, D_MODEL), lambda i, n: (0, col2(i, n))),
            pl.BlockSpec((1, D_MODEL), lambda i, n: (0, col2(i, n))),
        ],
        out_specs=[out, out],
        out_shape=[jax.ShapeDtypeStruct((2, rows, D_MODEL), BF16)] * 2,
        compiler_params=_cp(("parallel", "parallel")),
        name="hyena_filters",
    )(hidden, hidden, w3, decay[None], w3, decay[None])


def _dft_mats(L):
    r = int(math.isqrt(L))
    k2 = 2 * jnp.arange(L, dtype=jnp.int32)[:, None] + 1
    n1 = r * jnp.arange(L // r, dtype=jnp.int32)[None, :]
    n2 = jnp.arange(r, dtype=jnp.int32)[None, :]
    sc = math.pi / (2 * L)
    aa = ((k2 * n1) % (4 * L)).astype(F32) * sc
    ab = ((k2 * n2) % (4 * L)).astype(F32) * sc
    ca, sa, cb, sb = jnp.cos(aa)[:, :, None], jnp.sin(aa)[:, :, None], jnp.cos(ab)[:, None, :], jnp.sin(ab)[:, None, :]
    c = (ca * cb - sa * sb).reshape(L, L)
    s = (sa * cb + ca * sb).reshape(L, L)
    return c.astype(BF16), s.astype(BF16), c.T.astype(BF16), s.T.astype(BF16)


def _dft_tiles(L):
    return min(512, L), 512


def _dft_filter_kernel(c_ref, s_ref, a_ref, b_ref, gr_ref, gi_ref):
    gr_ref[...] = _dot(c_ref[...], a_ref[...]).astype(gr_ref.dtype)
    gi_ref[...] = _dot(s_ref[...], b_ref[...]).astype(gi_ref.dtype)


def _dft_filter(cm, sm, fa, fb, L):
    tf, tn = _dft_tiles(L)
    n = fa.shape[0]
    mat = lambda: pl.BlockSpec((tf, L), lambda k, c, s: (k, 0))
    rhs = lambda: pl.BlockSpec((None, L, tn), lambda k, c, s: (s, 0, c))
    out = pl.BlockSpec((None, tf, tn), lambda k, c, s: (s, k, c))
    return pl.pallas_call(
        _dft_filter_kernel,
        grid=(L // tf, D_MODEL // tn, n),
        in_specs=[mat(), mat(), rhs(), rhs()],
        out_specs=[out, out],
        out_shape=[jax.ShapeDtypeStruct((n, L, D_MODEL), BF16)] * 2,
        compiler_params=_cp(("parallel", "parallel", "parallel")),
        name="hyena_filter_dft",
    )(cm, sm, fa, fb)


FWD_TF = 256


def _dft_fwd_kernel(c_ref, s_ref, z_ref, gr_ref, gi_ref, yr_ref, yi_ref, *, n_blk, blk, bpb):
    c, s = c_ref[...], s_ref[...]
    for bb in range(bpb):
        zc, zs = [], []
        for j in range(n_blk):
            r = (bb * n_blk + j) * blk
            zj = z_ref[r:r + blk, :]
            zc.append(_dot(c, zj).astype(BF16))
            zs.append(_dot(s, zj).astype(BF16))
        for i in range(n_blk):
            yr = yi = None
            for j in range(n_blk):
                lag = i - j + n_blk - 1
                gr, gi = gr_ref[lag], gi_ref[lag]
                tr = gr * zc[j] + gi * zs[j]
                ti = gi * zc[j] - gr * zs[j]
                yr = tr if yr is None else yr + tr
                yi = ti if yi is None else yi + ti
            yr_ref[bb, i] = yr.astype(BF16)
            yi_ref[bb, i] = yi.astype(BF16)


def _seqs_per_step(L):
    return max(1, 2048 // L)


def _dft_fwd(cm, sm, z, z_rowblk, z_colblk, gr, gi, order, nb, L, blk):
    n_blk = L // blk
    bpb = _seqs_per_step(L)
    assert nb % bpb == 0 and z_rowblk % bpb == 0
    tf, tn = min(FWD_TF, blk), 512
    mat = lambda: pl.BlockSpec((tf, blk), lambda k, c, b: (k, 0))
    gsp = lambda: pl.BlockSpec((None, 2 * n_blk - 1, tf, tn), lambda k, c, b: (order, 0, k, c))
    out = pl.BlockSpec((bpb, n_blk, tf, tn), lambda k, c, b: (b, 0, k, c))
    return pl.pallas_call(
        functools.partial(_dft_fwd_kernel, n_blk=n_blk, blk=blk, bpb=bpb),
        grid=(blk // tf, D_MODEL // tn, nb // bpb),
        in_specs=[mat(), mat(),
                  pl.BlockSpec((bpb * L, tn), lambda k, c, b: (z_rowblk // bpb + b, z_colblk + c)),
                  gsp(), gsp()],
        out_specs=[out, out],
        out_shape=[jax.ShapeDtypeStruct((nb, n_blk, blk, D_MODEL), BF16)] * 2,
        compiler_params=_cp(("parallel", "parallel", "parallel")),
        name="hyena_dft_fwd",
    )(cm, sm, z, gr, gi)


def _dft_inv_kernel(ct_ref, st_ref, yr_ref, yi_ref, z_ref, gt_ref, sk_ref, o_ref, *, inv_len, bpb, tt):
    for bb in range(bpb):
        rows = slice(bb * tt, (bb + 1) * tt)
        y = (_dot(ct_ref[...], yr_ref[bb]) - _dot(st_ref[...], yi_ref[bb])) * inv_len
        o_ref[rows, :] = (gt_ref[rows, :].astype(F32) * (y + sk_ref[...] * z_ref[rows, :].astype(F32))).astype(BF16)


def _dft_inv(ctm, stm, yr, yi, z, z_rowblk, z_colblk, gate, g_rowblk, g_colblk, skip, nb, L):
    tt, tn = _dft_tiles(L)
    rpb = L // tt
    bpb = _seqs_per_step(L) if rpb == 1 else 1
    assert nb % bpb == 0 and z_rowblk % bpb == 0 and g_rowblk % bpb == 0
    mat = lambda: pl.BlockSpec((tt, L), lambda t, c, b: (t, 0))
    spec = lambda: pl.BlockSpec((bpb, L, tn), lambda t, c, b: (b, 0, c))
    rows = lambda blk0: (lambda t, c, b: ((blk0 + b * bpb * rpb + t) // bpb))
    return pl.pallas_call(
        functools.partial(_dft_inv_kernel, inv_len=1.0 / L, bpb=bpb, tt=tt),
        grid=(rpb, D_MODEL // tn, nb // bpb),
        in_specs=[mat(), mat(), spec(), spec(),
                  pl.BlockSpec((bpb * tt, tn), lambda t, c, b: (rows(z_rowblk)(t, c, b), z_colblk + c)),
                  pl.BlockSpec((bpb * tt, tn), lambda t, c, b: (rows(g_rowblk)(t, c, b), g_colblk + c)),
                  pl.BlockSpec((1, tn), lambda t, c, b: (0, c))],
        out_specs=pl.BlockSpec((bpb * tt, tn), lambda t, c, b: (rows(0)(t, c, b), c)),
        out_shape=jax.ShapeDtypeStruct((nb * L, D_MODEL), BF16),
        compiler_params=_cp(("parallel", "parallel", "parallel")),
        name="hyena_dft_inv",
    )(ctm, stm, yr, yi, z, gate, skip)


def _hyena_stream(u, row0, nb, L, fparams, skip):
    blk = min(HYENA_BLOCK, L)
    n_blk = L // blk
    n_lag = 2 * n_blk - 1
    cm, sm, ctm, stm = _dft_mats(blk)
    fa, fb = _hyena_filters(L, blk, *fparams)
    seg = lambda a: a.reshape(2 * n_lag, blk, D_MODEL)
    gr, gi = _dft_filter(cm, sm, seg(fa), seg(fb), blk)
    gr, gi = (a.reshape(2, n_lag, blk, D_MODEL) for a in (gr, gi))
    tt, tn = _dft_tiles(blk)
    ncb = D_MODEL // tn
    blocks = lambda a: a.reshape(nb * n_blk, blk, D_MODEL)

    yr, yi = _dft_fwd(cm, sm, u, row0 // L, 0, gr, gi, 0, nb, L, blk)
    z1 = _dft_inv(ctm, stm, blocks(yr), blocks(yi), u, row0 // tt, 0, u, row0 // tt, ncb, skip[0:1],
                  nb * n_blk, blk)
    yr, yi = _dft_fwd(cm, sm, z1, 0, 0, gr, gi, 1, nb, L, blk)
    return _dft_inv(ctm, stm, blocks(yr), blocks(yi), z1, 0, 0, u, row0 // tt, 2 * ncb, skip[1:2],
                    nb * n_blk, blk)


def _hyena_mixer(pending, g, mod, w_in, b_in, conv_w, conv_b, f_w1, f_b1, f_w2, f_b2, f_freq, f_w3, decay, skip,
                 w_out, b_out):
    u0, x = _nm_matmul(*pending, g, mod, w_in.astype(BF16), b_in.reshape(1, -1), BF16, "hyena_in_proj")
    u = _conv3(u0, conv_w, conv_b)
    fparams = (f_w1, f_b1, f_w2, f_b2, f_freq, f_w3, decay)
    zp = _hyena_stream(u, 0, BATCH, SEQ, fparams, skip)
    zs = _hyena_stream(u, TP, DEC_BATCH, DEC_SEQ, fparams, skip)
    return _resid_matmul(zp, zs, w_out.astype(BF16), b_out.reshape(1, -1), x, mod, "hyena_out_proj")


def _rope_tables():
    pos = jnp.arange(DEC_SEQ, dtype=jnp.int32)
    row = (pos // GRID_W).astype(F32)
    col = (pos % GRID_W).astype(F32)
    axis_dim = HEAD_DIM // 2
    inv_freq = ROPE_THETA ** (-jnp.arange(0, axis_dim, 2, dtype=F32) / axis_dim)
    ar = row[:, None] * inv_freq[None, :]
    ac = col[:, None] * inv_freq[None, :]
    cos = jnp.concatenate([jnp.cos(ar), jnp.cos(ar), jnp.cos(ac), jnp.cos(ac)], axis=-1)
    sin = jnp.concatenate([-jnp.sin(ar), jnp.sin(ar), -jnp.sin(ac), jnp.sin(ac)], axis=-1)
    return cos, sin


QKV_TM = 512
QKV_TN = 1024
PAIR = 2 * HEAD_DIM


QKV_AHEAD = 2


def _qkv_kernel(dest_ref, x_ref, gp_ref, ys_hbm, g_ref, sh_ref, sc_ref, w_ref, qn_ref, kn_ref, cos_ref, sin_ref,
                q_ref, k_ref, v_ref, nk_ref, nv_ref, xn_ref, h_scr, ybuf, sems, *, use_norm):
    tm = QKV_TM
    i, j = pl.program_id(0), pl.program_id(1)
    n_tiles = pl.num_programs(0)
    n_buf = QKV_AHEAD + 1

    @pl.when(j == 0)
    def _():
        @pl.when(i == 0)
        def _():
            for t in range(QKV_AHEAD):
                _gather_rows(dest_ref, t * tm, ys_hbm, ybuf.at[t], sems.at[t], tm)

        slot = i % n_buf
        _wait_rows(ys_hbm, ybuf.at[slot], sems.at[slot], tm)
        x = x_ref[...] + gp_ref[...] * _unpack_bf16_pairs(ybuf[slot])
        xn_ref[...] = x
        h_scr[...] = _norm_mod(x, g_ref[...], sh_ref[...], sc_ref[...]).astype(BF16)

    def gather_ahead():
        ahead = (i + QKV_AHEAD) % n_buf
        _gather_rows(dest_ref, jnp.minimum(i + QKV_AHEAD, n_tiles - 1) * tm, ys_hbm, ybuf.at[ahead],
                     sems.at[ahead], tm, static=True)

    quarter = HEAD_DIM // 4
    scale = HEAD_DIM ** -0.5 * LOG2E

    def head(xh, gn):
        if use_norm:
            xh = xh * lax.rsqrt(jnp.mean(xh * xh, axis=-1, keepdims=True) + NORM_EPS) * gn
        return xh

    def rope(xh):
        lane = lax.broadcasted_iota(jnp.int32, (tm, HEAD_DIM), 1)
        first = (lane % (2 * quarter)) < quarter
        partner = jnp.where(first, pltpu.roll(xh, HEAD_DIM - quarter, 1), pltpu.roll(xh, quarter, 1))
        return xh * cos_ref[...] + partner * sin_ref[...]

    def proj(c0):
        return _dot(h_scr[...], w_ref[:, c0:c0 + PAIR])

    def q_tile(latent):
        for p in range(QKV_TN // PAIR):
            acc = proj(p * PAIR)
            for t in range(2):
                xh = head(acc[:, t * HEAD_DIM:(t + 1) * HEAD_DIM], qn_ref[...])
                xh = rope(xh) if latent else xh
                c0 = p * PAIR + t * HEAD_DIM
                q_ref[:, c0:c0 + HEAD_DIM] = (xh * scale).astype(BF16)

    def kv_tile(latent):
        gather_ahead()
        for p in range(KV_DIM // PAIR):
            acc = proj(p * PAIR)
            for t in range(2):
                c0 = p * PAIR + t * HEAD_DIM
                kh = head(acc[:, t * HEAD_DIM:(t + 1) * HEAD_DIM], kn_ref[...])
                if not latent:
                    nk_ref[:, c0:c0 + HEAD_DIM] = kh
                k_ref[:, c0:c0 + HEAD_DIM] = (rope(kh) if latent else kh).astype(BF16)
        for p in range(KV_DIM // PAIR):
            acc = proj(KV_DIM + p * PAIR)
            if not latent:
                nv_ref[:, p * PAIR:(p + 1) * PAIR] = acc
            v_ref[:, p * PAIR:(p + 1) * PAIR] = acc.astype(BF16)

    is_ctx = i < TP // tm
    is_q = j < D_MODEL // QKV_TN
    for latent in (False, True):
        stream = jnp.logical_not(is_ctx) if latent else is_ctx
        pl.when(jnp.logical_and(stream, is_q))(functools.partial(q_tile, latent))
        pl.when(jnp.logical_and(stream, jnp.logical_not(is_q)))(functools.partial(kv_tile, latent))

    @pl.when(jnp.logical_and(i == n_tiles - 1, j == pl.num_programs(1) - 1))
    def _():
        for t in range(1, QKV_AHEAD + 1):
            late = (i + t) % n_buf
            _wait_rows(ys_hbm, ybuf.at[late], sems.at[late], tm)


def _qkv_proj(x_prev, dest, ys, mod_prev, g, mod, w_qkv, q_norm, k_norm, use_norm, rope):
    tm, tn = QKV_TM, QKV_TN
    n_ctx = TP // tm
    n_q = D_MODEL // tn
    tab = lambda: pl.BlockSpec((tm, HEAD_DIM), lambda i, j, d: (jnp.maximum(i - n_ctx, 0) % (DEC_SEQ // tm), 0))
    kv = lambda: pl.BlockSpec((tm, KV_DIM), lambda i, j, d: (i, 0))
    new = lambda: pl.BlockSpec((tm, KV_DIM), lambda i, j, d: (jnp.minimum(i, n_ctx - 1), 0))
    row = lambda: pl.BlockSpec((tm, D_MODEL), lambda i, j, d: (i, 0))
    return pl.pallas_call(
        functools.partial(_qkv_kernel, use_norm=use_norm),
        grid_spec=pltpu.PrefetchScalarGridSpec(
            num_scalar_prefetch=1,
            grid=(T // tm, QKV_DIM // tn),
            in_specs=[
                row(),
                pl.BlockSpec((None, None, 1, D_MODEL), lambda i, j, d: (_cond_row(i * tm), 5, 0, 0)),
                pl.BlockSpec(memory_space=pl.ANY),
                pl.BlockSpec((1, D_MODEL), lambda i, j, d: (0, 0)),
                _mod_spec(tm, 0),
                _mod_spec(tm, 1),
                pl.BlockSpec((D_MODEL, tn), lambda i, j, d: (0, j)),
                pl.BlockSpec((1, HEAD_DIM), lambda i, j, d: (0, 0)),
                pl.BlockSpec((1, HEAD_DIM), lambda i, j, d: (0, 0)),
                tab(), tab(),
            ],
            out_specs=[pl.BlockSpec((tm, tn), lambda i, j, d: (i, jnp.minimum(j, n_q - 1))), kv(), kv(), new(), new(),
                       row()],
            scratch_shapes=[pltpu.VMEM((tm, D_MODEL), BF16), pltpu.VMEM((QKV_AHEAD + 1, tm, HALF_D), jnp.uint32),
                            pltpu.SemaphoreType.DMA((QKV_AHEAD + 1,))],
        ),
        out_shape=[jax.ShapeDtypeStruct((T, D_MODEL), BF16), jax.ShapeDtypeStruct((T, KV_DIM), BF16),
                   jax.ShapeDtypeStruct((T, KV_DIM), BF16), jax.ShapeDtypeStruct((TP, KV_DIM), F32),
                   jax.ShapeDtypeStruct((TP, KV_DIM), F32), jax.ShapeDtypeStruct((T, D_MODEL), F32)],
        compiler_params=_cp(("arbitrary", "arbitrary"), 56),
        name="qkv_proj",
    )(dest, x_prev, mod_prev, ys, g, mod, mod, w_qkv, q_norm.reshape(1, HEAD_DIM), k_norm.reshape(1, HEAD_DIM),
      *rope)


LOG2E = math.log2(math.e)
ATTN_TQ = 256


def _attn_kernel(*refs, tq, seq_len, has_ctx, windowed, has_sink):
    it = iter(refs)
    q_ref, k_ref, v_ref = next(it), next(it), next(it)
    kc_ref, vc_ref = (next(it), next(it)) if has_ctx else (None, None)
    sink_ref = next(it) if has_sink else None
    o_ref = next(it)
    if windowed:
        i = pl.program_id(2)
        span = tq + 2 * WINDOW
        start = pl.multiple_of(jnp.clip(i * tq - WINDOW, 0, seq_len - span), WINDOW)
        qpos = i * tq + lax.broadcasted_iota(jnp.int32, (tq, 1), 0)
        kpos = start + lax.broadcasted_iota(jnp.int32, (1, span), 1)
        segs = [(k_ref, v_ref, pl.ds(start, span), jnp.abs(kpos - qpos) <= WINDOW)]
    else:
        segs = [(k_ref, v_ref, slice(None), None)]
    if has_ctx:
        segs.append((kc_ref, vc_ref, slice(None), None))
    for h in range(KV_GROUP):
        hs = slice(h * HEAD_DIM, (h + 1) * HEAD_DIM)
        qh = q_ref[:, hs]
        scores = []
        m = None
        for kr, _, rows, mask in segs:
            s = lax.dot_general(qh, kr[rows, :], (((1,), (1,)), ((), ())), preferred_element_type=F32)
            if mask is not None:
                s = jnp.where(mask, s, NEG_INF)
            scores.append(s)
            ms = jnp.max(s, axis=-1, keepdims=True)
            m = ms if m is None else jnp.maximum(m, ms)
        if has_sink:
            sk = sink_ref[pl.program_id(1) * KV_GROUP + h]
            m = jnp.maximum(m, sk)
        l = jnp.exp2(sk - m) if has_sink else jnp.zeros_like(m)
        acc = jnp.zeros((tq, HEAD_DIM), F32)
        for (_, vr, rows, _), s in zip(segs, scores):
            p = jnp.exp2(s - m)
            l = l + jnp.sum(p, axis=-1, keepdims=True)
            acc = acc + _dot(p.astype(BF16), vr[rows, :])
        o_ref[:, hs] = (acc / l).astype(BF16)


def _attention(q, row0, k, v, k_row0, n_keys, k_ctx, v_ctx, sink, nb, L, windowed):
    tq = min(ATTN_TQ, L)
    nq = L // tq
    q_blk0 = row0 // tq
    seq0 = k_row0 // n_keys
    own = lambda: pl.BlockSpec((n_keys, HEAD_DIM), lambda b, g, i: (seq0 + b, g))
    in_specs = [pl.BlockSpec((tq, KV_GROUP * HEAD_DIM), lambda b, g, i: (q_blk0 + b * nq + i, g)), own(), own()]
    args = [q, k, v]
    if k_ctx is not None:
        ctx = lambda: pl.BlockSpec((None, PAST_LEN, HEAD_DIM), lambda b, g, i: (b, 0, g))
        in_specs += [ctx(), ctx()]
        args += [k_ctx, v_ctx]
    if sink is not None:
        in_specs.append(pl.BlockSpec(memory_space=pltpu.SMEM))
        args.append(sink.astype(F32) * LOG2E)
    return pl.pallas_call(
        functools.partial(_attn_kernel, tq=tq, seq_len=L, has_ctx=k_ctx is not None, windowed=windowed,
                          has_sink=sink is not None),
        grid=(nb, N_KV_HEADS, nq),
        in_specs=in_specs,
        out_specs=pl.BlockSpec((tq, KV_GROUP * HEAD_DIM), lambda b, g, i: (b * nq + i, g)),
        out_shape=jax.ShapeDtypeStruct((nb * L, D_MODEL), BF16),
        compiler_params=_cp(("parallel", "parallel", "parallel"), 56),
        name="attention",
    )(*args)


def _attn_mixer(pending, g, mod, w_qkv, q_norm, k_norm, use_norm, sink, w_o, cache_k, cache_v, windowed, rope):
    q, k, v, new_k, new_v, x = _qkv_proj(*pending, g, mod, w_qkv.astype(BF16), q_norm, k_norm, use_norm, rope)
    op = _attention(q, 0, k, v, 0, SEQ, None, None, sink, BATCH, SEQ, False)
    kc = cache_k.reshape(DEC_BATCH, PAST_LEN, KV_DIM).astype(BF16)
    vc = cache_v.reshape(DEC_BATCH, PAST_LEN, KV_DIM).astype(BF16)
    if windowed:
        osm = _attention(q, TP, k, v, TP, DEC_SEQ, kc, vc, sink, DEC_BATCH, DEC_SEQ, True)
    else:
        n_keys = DEC_SEQ + PAST_LEN
        both = lambda a, c: jnp.concatenate([a[TP:].reshape(DEC_BATCH, DEC_SEQ, KV_DIM), c], axis=1).reshape(
            DEC_BATCH * n_keys, KV_DIM)
        osm = _attention(q, TP, both(k, kc), both(v, vc), 0, n_keys, None, None, sink, DEC_BATCH, DEC_SEQ, False)
    x = _resid_matmul(op, osm, w_o.astype(BF16), jnp.zeros((1, D_MODEL), F32), x, mod, "attn_out_proj")
    shape = (BATCH, SEQ, N_KV_HEADS, HEAD_DIM)
    return x, new_k.reshape(shape), new_v.reshape(shape)


ROUTE_TILE = 512
ROUTE_ROWS = 32


def _router_kernel(x_ref, g_ref, sh_ref, sc_ref, wr_ref, br_ref, xh_ref, rt_ref, cnt_ref, carry):
    tm = ROUTE_TILE
    i = pl.program_id(0)

    @pl.when(i == 0)
    def _():
        carry[...] = jnp.zeros_like(carry)

    h = _norm_mod(x_ref[...], g_ref[...], sh_ref[...], sc_ref[...])
    xh_ref[:, :HALF_D] = _pack_bf16_pairs(h)
    logits = _dot(h.astype(BF16), wr_ref[...])
    s = _sigmoid(logits.T[:N_EXPERTS, :])
    sb = s + br_ref[...]
    u = [s[e:e + 1, :] for e in range(N_EXPERTS)]
    v = [sb[e:e + 1, :] for e in range(N_EXPERTS)]

    gscore = []
    for gq in range(N_EXPERT_GROUPS):
        m = v[4 * gq:4 * gq + 4]
        best = m[PAIR_LO[0]] + m[PAIR_HI[0]]
        for a, b in zip(PAIR_LO[1:], PAIR_HI[1:]):
            best = jnp.maximum(best, m[a] + m[b])
        gscore.append(best)
    gidx = jnp.zeros((1, tm), jnp.int32)
    gbest = gscore[0]
    for gq in range(1, N_EXPERT_GROUPS):
        upd = gscore[gq] > gbest
        gidx = jnp.where(upd, gq, gidx)
        gbest = jnp.where(upd, gscore[gq], gbest)

    def pick(rows, j):
        out = rows[j]
        for gq in range(1, N_EXPERT_GROUPS):
            out = jnp.where(gidx == gq, rows[4 * gq + j], out)
        return out

    vin = [pick(v, j) for j in range(EXPERTS_PER_GROUP)]
    uin = [pick(u, j) for j in range(EXPERTS_PER_GROUP)]
    i1 = jnp.zeros((1, tm), jnp.int32)
    m1 = vin[0]
    for j in range(1, EXPERTS_PER_GROUP):
        upd = vin[j] > m1
        i1 = jnp.where(upd, j, i1)
        m1 = jnp.where(upd, vin[j], m1)
    i2 = jnp.full((1, tm), -1, jnp.int32)
    m2 = jnp.full((1, tm), -jnp.inf, F32)
    for j in range(EXPERTS_PER_GROUP):
        upd = (i1 != j) & (vin[j] > m2)
        i2 = jnp.where(upd, j, i2)
        m2 = jnp.where(upd, vin[j], m2)

    def sel(rows, idx):
        out = rows[0]
        for j in range(1, EXPERTS_PER_GROUP):
            out = jnp.where(idx == j, rows[j], out)
        return out

    w1, w2 = sel(uin, i1), sel(uin, i2)
    wsum = w1 + w2
    w1, w2 = w1 / wsum, w2 / wsum
    first_lo = i1 < i2
    lo = jnp.where(first_lo, i1, i2)
    hi = jnp.where(first_lo, i2, i1)
    w_lo = jnp.where(first_lo, w1, w2)
    w_hi = jnp.where(first_lo, w2, w1)
    pair = jnp.where(lo == 0, hi - 1, jnp.where(lo == 1, hi + 1, 5))
    bucket = gidx * len(PAIR_LO) + pair

    onehot = (lax.broadcasted_iota(jnp.int32, (ROUTE_ROWS, tm), 0) == bucket)
    tri = (lax.broadcasted_iota(jnp.int32, (tm, tm), 0) <= lax.broadcasted_iota(jnp.int32, (tm, tm), 1))
    cum = _dot(jnp.where(onehot, 1.0, 0.0).astype(BF16), jnp.where(tri, 1.0, 0.0).astype(BF16))
    rank = jnp.sum(jnp.where(onehot, cum - 1.0 + carry[...], 0.0), axis=0, keepdims=True)
    carry[...] = carry[...] + cum[:, tm - 1:tm]
    cnt_ref[...] = jnp.broadcast_to(carry[...], (ROUTE_ROWS, LANE))

    rt_ref[...] = jnp.zeros_like(rt_ref)
    rt_ref[0:1, :] = bucket.astype(F32)
    rt_ref[1:2, :] = rank
    wt = jnp.concatenate([w_lo, w_hi, jnp.zeros((LANE - 2, tm), F32)], axis=0)
    xh_ref[:, HALF_D:] = lax.bitcast_convert_type(wt.T, jnp.uint32)


def _router(x, g, mod, w_router, b_router):
    tm = ROUTE_TILE
    wr = _pad2(w_router, D_MODEL, LANE).astype(BF16)
    return pl.pallas_call(
        _router_kernel,
        grid=(T // tm,),
        in_specs=[
            pl.BlockSpec((tm, D_MODEL), lambda i: (i, 0)),
            pl.BlockSpec((1, D_MODEL), lambda i: (0, 0)),
            _mod_spec(tm, 3),
            _mod_spec(tm, 4),
            pl.BlockSpec((D_MODEL, LANE), lambda i: (0, 0)),
            pl.BlockSpec((N_EXPERTS, 1), lambda i: (0, 0)),
        ],
        out_specs=[
            pl.BlockSpec((tm, XH_W), lambda i: (i, 0)),
            pl.BlockSpec((8, tm), lambda i: (0, i)),
            pl.BlockSpec((ROUTE_ROWS, LANE), lambda i: (0, 0)),
        ],
        out_shape=[
            jax.ShapeDtypeStruct((T, XH_W), jnp.uint32),
            jax.ShapeDtypeStruct((8, T), F32),
            jax.ShapeDtypeStruct((ROUTE_ROWS, LANE), F32),
        ],
        scratch_shapes=[pltpu.VMEM((ROUTE_ROWS, 1), F32)],
        compiler_params=_cp(("arbitrary",)),
        name="moe_router",
    )(x, g, mod, mod, wr, b_router.reshape(N_EXPERTS, 1))


DISPATCH_TILE = 256


DMA_UNROLL = 32


def _invert_kernel(dest_ref, src_ref):
    def clear(s, c):
        src_ref[s] = 0
        return c

    def put(t, c):
        src_ref[dest_ref[t]] = t
        return c

    lax.fori_loop(0, T_PAD, clear, 0, unroll=DMA_UNROLL)
    lax.fori_loop(0, T, put, 0, unroll=DMA_UNROLL)


def _invert(dest):
    return pl.pallas_call(
        _invert_kernel,
        in_specs=[pl.BlockSpec(memory_space=pltpu.SMEM)],
        out_specs=pl.BlockSpec(memory_space=pltpu.SMEM),
        out_shape=jax.ShapeDtypeStruct((T_PAD,), jnp.int32),
        name="moe_invert",
    )(dest)


def _gather_rows(idx_ref, base, src_hbm, buf, sem, tm, static=False):
    def start(r, c):
        pltpu.make_async_copy(src_hbm.at[pl.ds(idx_ref[base + r], 1)], buf.at[pl.ds(r, 1)], sem).start()
        return c

    if static:
        for r in range(tm):
            start(r, 0)
    else:
        lax.fori_loop(0, tm, start, 0, unroll=DMA_UNROLL)


def _wait_rows(src_hbm, buf, sem, tm):
    pltpu.make_async_copy(src_hbm.at[pl.ds(0, tm)], buf, sem).wait()


def _expert_kernel(ea_ref, eb_ref, nv_ref, src_ref, xh_hbm, ga_ref, ua_ref, da_ref, gb_ref, ub_ref, db_ref, y_ref,
                   *scratch):
    tm = MOE_TILE
    j = pl.program_id(0)
    nv = nv_ref[j]
    *bufs, sems = scratch
    n_buf = len(bufs)

    @pl.when(j == 0)
    def _():
        for t in range(MOE_AHEAD):
            _gather_rows(src_ref, t * tm, xh_hbm, bufs[t], sems.at[t], tm)

    def run(p):
        cur, cur_sem = bufs[p], sems.at[p]
        q = (p + MOE_AHEAD) % n_buf
        ahead, ahead_sem = bufs[q], sems.at[q]

        @pl.when(jnp.logical_or(j < MOE_AHEAD, nv_ref[jnp.maximum(j - MOE_AHEAD, 0)] > 0))
        def _():
            _wait_rows(xh_hbm, cur, cur_sem, tm)

        @pl.when(nv > 0)
        def _():
            valid = lax.broadcasted_iota(jnp.int32, (tm, 1), 0) < nv
            x = jnp.where(valid, _unpack_bf16_pairs(cur[:, :HALF_D]), 0.0).astype(BF16)
            wts = jnp.where(valid, lax.bitcast_convert_type(cur[:, HALF_D:], F32), 0.0)
            for r in range(tm):
                pltpu.make_async_copy(xh_hbm.at[pl.ds(src_ref[(j + MOE_AHEAD) * tm + r], 1)],
                                      ahead.at[pl.ds(r, 1)], ahead_sem).start()

            def ffn(g_ref, u_ref, d_ref, w):
                a = _dot(x, g_ref[...])
                h = a * _sigmoid(a) * _dot(x, u_ref[...]) * w
                return _dot(h.astype(BF16), d_ref[...])

            y = ffn(ga_ref, ua_ref, da_ref, wts[:, 0:1]) + ffn(gb_ref, ub_ref, db_ref, wts[:, 1:2])
            y_ref[...] = _pack_bf16_pairs(y)

    for p in range(n_buf):
        pl.when(j % n_buf == p)(functools.partial(run, p))

    @pl.when(nv == 0)
    def _():
        y_ref[...] = jnp.zeros_like(y_ref)


def _experts(layer, tile_ea, tile_eb, tile_nv, src, xh, w_gate, w_up, w_down):
    tm = MOE_TILE
    up = lambda sel: pl.BlockSpec((None, None, D_MODEL, D_EXPERT),
                                  lambda j, ea, eb, nv, sr: (layer, (ea, eb)[sel][j], 0, 0))
    down = lambda sel: pl.BlockSpec((None, None, D_EXPERT, D_MODEL),
                                    lambda j, ea, eb, nv, sr: (layer, (ea, eb)[sel][j], 0, 0))
    return pl.pallas_call(
        _expert_kernel,
        grid_spec=pltpu.PrefetchScalarGridSpec(
            num_scalar_prefetch=4,
            grid=(MOE_TILES,),
            in_specs=[pl.BlockSpec(memory_space=pl.ANY), up(0), up(0), down(0), up(1), up(1), down(1)],
            out_specs=pl.BlockSpec((tm, HALF_D), lambda j, ea, eb, nv, sr: (j, 0)),
            scratch_shapes=[pltpu.VMEM((tm, XH_W), jnp.uint32)] * (MOE_AHEAD + 1)
            + [pltpu.SemaphoreType.DMA((MOE_AHEAD + 1,))],
        ),
        out_shape=jax.ShapeDtypeStruct((T_PAD, HALF_D), jnp.uint32),
        compiler_params=_cp(("arbitrary",), 56),
        name="moe_experts",
    )(tile_ea, tile_eb, tile_nv, src, xh, w_gate, w_up, w_down, w_gate, w_up, w_down)


def _combine_kernel(*refs, final):
    tm = DISPATCH_TILE
    if final:
        dest_ref, x_ref, gt_ref, ys_hbm, fg_ref, oc_ref, ol_ref, buf, sems = refs
    else:
        dest_ref, x_ref, gt_ref, ys_hbm, o_ref, buf, sems = refs
    i = pl.program_id(0)
    slot = i % 2

    last = pl.num_programs(0) - 1

    @pl.when(i == 0)
    def _():
        _gather_rows(dest_ref, 0, ys_hbm, buf.at[0], sems.at[0], tm)

    def finish():
        _wait_rows(ys_hbm, buf.at[slot], sems.at[slot], tm)
        x = x_ref[...] + gt_ref[...] * _unpack_bf16_pairs(buf[slot])
        if not final:
            o_ref[...] = x
            return
        y = x * lax.rsqrt(jnp.mean(x * x, axis=-1, keepdims=True) + NORM_EPS) * fg_ref[...]
        is_ctx = i < TP // tm

        @pl.when(is_ctx)
        def _():
            oc_ref[...] = y

        @pl.when(jnp.logical_not(is_ctx))
        def _():
            ol_ref[...] = y

    @pl.when(i < last)
    def _():
        _gather_rows(dest_ref, (i + 1) * tm, ys_hbm, buf.at[1 - slot], sems.at[1 - slot], tm, static=True)
        finish()

    pl.when(i == last)(finish)


def _combine(dest, x, mod, ys, final_gain=None):
    tm = DISPATCH_TILE
    final = final_gain is not None
    n_ctx = TP // tm
    in_specs = [pl.BlockSpec((tm, D_MODEL), lambda i, d: (i, 0)),
                pl.BlockSpec((None, None, 1, D_MODEL), lambda i, d: (_cond_row(i * tm), 5, 0, 0)),
                pl.BlockSpec(memory_space=pl.ANY)]
    args = [dest, x, mod, ys]
    if final:
        in_specs.append(pl.BlockSpec((1, D_MODEL), lambda i, d: (0, 0)))
        args.append(final_gain.reshape(1, D_MODEL))
        out_specs = [pl.BlockSpec((tm, D_MODEL), lambda i, d: (jnp.minimum(i, n_ctx - 1), 0)),
                     pl.BlockSpec((tm, D_MODEL), lambda i, d: (jnp.maximum(i - n_ctx, 0), 0))]
        out_shape = [jax.ShapeDtypeStruct((TP, D_MODEL), F32), jax.ShapeDtypeStruct((TS, D_MODEL), F32)]
    else:
        out_specs = pl.BlockSpec((tm, D_MODEL), lambda i, d: (i, 0))
        out_shape = jax.ShapeDtypeStruct((T, D_MODEL), F32)
    return pl.pallas_call(
        functools.partial(_combine_kernel, final=final),
        grid_spec=pltpu.PrefetchScalarGridSpec(
            num_scalar_prefetch=1,
            grid=(T // tm,),
            in_specs=in_specs,
            out_specs=out_specs,
            scratch_shapes=[pltpu.VMEM((2, tm, HALF_D), jnp.uint32), pltpu.SemaphoreType.DMA((2,))],
        ),
        out_shape=out_shape,
        compiler_params=_cp(("arbitrary",)),
        name="moe_combine",
    )(*args)


def _lookup(table, idx):
    n = table.shape[0]
    hit = idx[:, None] == jnp.arange(n, dtype=jnp.int32)[None, :]
    return jnp.sum(jnp.where(hit, table[None, :], 0), axis=1)


def _moe_plan(rt, cnt):
    bucket = rt[0].astype(jnp.int32)
    rank = rt[1].astype(jnp.int32)
    counts = cnt[:N_BUCKETS, 0].astype(jnp.int32)
    tiles = (counts + MOE_TILE - 1) // MOE_TILE
    order = jnp.arange(N_BUCKETS, dtype=jnp.int32)
    tile_start = jnp.sum(jnp.where(order[None, :] < order[:, None], tiles[None, :], 0), axis=1)
    tile_end = tile_start + tiles
    n_used = tile_end[N_BUCKETS - 1]
    dest = _lookup(tile_start * MOE_TILE, bucket) + rank
    j = jnp.arange(MOE_TILES, dtype=jnp.int32)
    jc = jnp.minimum(j, n_used - 1)
    b = jnp.minimum(jnp.sum((jc[:, None] >= tile_end[None, :]).astype(jnp.int32), axis=1), N_BUCKETS - 1)
    nv = jnp.clip(_lookup(counts, b) - (j - _lookup(tile_start, b)) * MOE_TILE, 0, MOE_TILE)
    nv = jnp.where(j < n_used, nv, 0)
    n_pairs = len(PAIR_LO)
    ea = (b // n_pairs) * EXPERTS_PER_GROUP + _lookup(jnp.asarray(PAIR_LO, jnp.int32), b % n_pairs)
    eb = (b // n_pairs) * EXPERTS_PER_GROUP + _lookup(jnp.asarray(PAIR_HI, jnp.int32), b % n_pairs)
    return dest, ea, eb, nv


def _moe(layer, x, g, mod, w_router, b_router, w_gate, w_up, w_down, final_gain=None, defer=False):
    xh, rt, cnt = _router(x, g, mod, w_router, b_router)
    dest, ea, eb, nv = _moe_plan(rt, cnt)
    ys = _experts(layer, ea, eb, nv, _invert(dest), xh, w_gate, w_up, w_down)
    if defer:
        return x, dest, ys, mod
    return _combine(dest, x, mod, ys, final_gain)


def kernel(x_prompt, x_sample, cache_k_full, cache_v_full, cache_k_win, cache_v_win, c, c_ctx, w_mod, b_mod, norm_mix, norm_ffn, final_norm, pool_w, pool_scale, hy_w_in, hy_b_in, hy_conv_w, hy_conv_b, hy_f_w1, hy_f_b1, hy_f_w2, hy_f_b2, hy_f_freq, hy_f_w3, hy_decay, hy_skip, hy_w_out, hy_b_out, fa_w_qkv, fa_q_norm, fa_k_norm, fa_w_o, wa_w_qkv, wa_sink, wa_w_o, w_router, b_router, moe_w_gate, moe_w_up, moe_w_down):
    x = None
    cond =jnp.concatenate([c_ctx[None, :], c, jnp.zeros((N_COND - 1 - DEC_BATCH, D_MODEL), F32)], axis=0)
    mods = _adaln(cond, w_mod, b_mod).reshape(DEPTH, N_COND, 6, 1, D_MODEL)
    rope = _rope_tables()
    ones_hd = jnp.ones((HEAD_DIM,), F32)
    wg_bf, wu_bf, wd_bf = moe_w_gate.astype(BF16), moe_w_up.astype(BF16), moe_w_down.astype(BF16)
    new_kv = {}
    for layer in range(DEPTH):
        kind = layer % 4
        j = layer // 4
        mod = mods[layer]
        g_mix = norm_mix[layer].reshape(1, D_MODEL)
        if kind == 0:
            assert layer == 0, "the pooling mixer reads the two input streams, so it must be the first layer"
            x = _pool_mixer(x_prompt.reshape(TP, D_MODEL), x_sample.reshape(TS, D_MODEL), g_mix, mod,
                            pool_w[j], pool_scale[j])
        elif kind == 1:
            x = _hyena_mixer(x, g_mix, mod, hy_w_in[j], hy_b_in[j], hy_conv_w[j], hy_conv_b[j], hy_f_w1[j],
                             hy_f_b1[j], hy_f_w2[j], hy_f_b2[j], hy_f_freq[j], hy_f_w3[j], hy_decay[j],
                             hy_skip[j], hy_w_out[j], hy_b_out[j])
        elif kind == 2:
            x, nk, nv = _attn_mixer(x, g_mix, mod, fa_w_qkv[j], fa_q_norm[j], fa_k_norm[j], True, None,
                                    fa_w_o[j], cache_k_full[:, j], cache_v_full[:, j], False, rope)
            new_kv.setdefault("kf", []).append(nk)
            new_kv.setdefault("vf", []).append(nv)
        else:
            x, nk, nv = _attn_mixer(x, g_mix, mod, wa_w_qkv[j], ones_hd, ones_hd, False, wa_sink[j],
                                    wa_w_o[j], cache_k_win[:, j], cache_v_win[:, j], True, rope)
            new_kv.setdefault("kw", []).append(nk)
            new_kv.setdefault("vw", []).append(nv)
        defer = layer + 1 < DEPTH and (layer + 1) % 4 in (1, 2, 3)
        x = _moe(layer, x, norm_ffn[layer].reshape(1, D_MODEL), mod, w_router, b_router, wg_bf, wu_bf, wd_bf,
                 final_norm if layer == DEPTH - 1 else None, defer)
    y_prompt, y_sample = x
    y_prompt = y_prompt.reshape(BATCH, SEQ, D_MODEL)
    y_sample = y_sample.reshape(DEC_BATCH, DEC_SEQ, D_MODEL)
    return (y_prompt, y_sample, jnp.stack(new_kv["kf"], axis=1), jnp.stack(new_kv["vf"], axis=1),
            jnp.stack(new_kv["kw"], axis=1), jnp.stack(new_kv["vw"], axis=1))
```

```python
import functools
import math

import jax
import jax.numpy as jnp
import numpy as np
from jax import lax
from jax.experimental import pallas as pl
from jax.experimental.pallas import tpu as pltpu

D_MODEL = 2048
BATCH = 32
SEQ = 256
DEPTH = 4
DEC_BATCH = 4
DEC_SEQ = 4096
PAST_LEN = 512
GRID_W = 64
N_HEADS = 16
N_KV_HEADS = 4
HEAD_DIM = D_MODEL // N_HEADS
KV_GROUP = N_HEADS // N_KV_HEADS
KV_DIM = N_KV_HEADS * HEAD_DIM
QKV_DIM = (N_HEADS + 2 * N_KV_HEADS) * HEAD_DIM
ROPE_THETA = 10000.0
WINDOW = 128
POOL_WINDOWS = (2, 4, 8, 16)
POOL_GROUP = D_MODEL // len(POOL_WINDOWS)
HYENA_EMB_BANDS = 16
HYENA_FILTER_HIDDEN = 64
N_EXPERTS = 16
N_EXPERT_GROUPS = 4
EXPERTS_PER_GROUP = 4
D_EXPERT = 512
NORM_EPS = 1e-6
NEG_INF = -1e30

F32 = jnp.float32
BF16 = jnp.bfloat16

TP = BATCH * SEQ
TS = DEC_BATCH * DEC_SEQ
T = TP + TS
N_COND = 8
LANE = 128
MIB = 1024 * 1024

PAIR_LO = (0, 0, 0, 1, 1, 2)
PAIR_HI = (1, 2, 3, 2, 3, 3)
N_BUCKETS = N_EXPERT_GROUPS * len(PAIR_LO)
MOE_TILE = 256
MOE_AHEAD = 3
MOE_TILES = T // MOE_TILE + N_BUCKETS + MOE_AHEAD
T_PAD = MOE_TILES * MOE_TILE
HALF_D = D_MODEL // 2
XH_W = HALF_D + LANE


def _cp(sem, vmem_mb=48):
    return pltpu.CompilerParams(dimension_semantics=sem, vmem_limit_bytes=vmem_mb * MIB)


def _dot(a, b):
    return jnp.dot(a, b, preferred_element_type=F32)


def _dot3(a, b):
    ah = a.astype(BF16)
    al = (a - ah.astype(F32)).astype(BF16)
    bh = b.astype(BF16)
    bl = (b - bh.astype(F32)).astype(BF16)
    return _dot(ah, bh) + (_dot(al, bh) + _dot(ah, bl))


def _sigmoid(x):
    return 1.0 / (1.0 + jnp.exp(-x))


def _pack_bf16_pairs(x):
    n = x.shape[1] // 2
    bits = lambda v: lax.bitcast_convert_type(v.astype(BF16).astype(F32), jnp.uint32)
    return (bits(x[:, :n]) >> 16) | bits(x[:, n:])


def _unpack_bf16_pairs(u):
    lo = lax.bitcast_convert_type(u << 16, F32)
    hi = lax.bitcast_convert_type(u & jnp.uint32(0xFFFF0000), F32)
    return jnp.concatenate([lo, hi], axis=1)


def _cond_row(r):
    return jnp.where(r < TP, 0, 1 + (r - TP) // DEC_SEQ)


def _mod_spec(tm, chunk, tn=D_MODEL, ncol=False):
    if ncol:
        return pl.BlockSpec((None, None, 1, tn), lambda i, j: (_cond_row(i * tm), chunk, 0, j))
    return pl.BlockSpec((None, None, 1, tn), lambda i, *_: (_cond_row(i * tm), chunk, 0, 0))


def _norm_mod(x, g, shift, scale):
    var = jnp.mean(x * x, axis=-1, keepdims=True)
    y = x * lax.rsqrt(var + NORM_EPS) * g
    return y * (1.0 + scale) + shift


def _adaln_kernel(c_ref, w_ref, b_ref, o_ref):
    c = c_ref[...]
    a = c * _sigmoid(c)
    o_ref[...] = _dot3(a, w_ref[...]) + b_ref[...]


def _adaln(cond, w_mod, b_mod):
    tn = 1024
    n = 6 * D_MODEL
    return pl.pallas_call(
        _adaln_kernel,
        grid=(DEPTH, n // tn),
        in_specs=[
            pl.BlockSpec((N_COND, D_MODEL), lambda l, j: (0, 0)),
            pl.BlockSpec((None, D_MODEL, tn), lambda l, j: (l, 0, j)),
            pl.BlockSpec((None, 1, tn), lambda l, j: (l, 0, j)),
        ],
        out_specs=pl.BlockSpec((None, N_COND, tn), lambda l, j: (l, 0, j)),
        out_shape=jax.ShapeDtypeStruct((DEPTH, N_COND, n), F32),
        compiler_params=_cp(("parallel", "parallel")),
        name="adaln",
    )(cond, w_mod, b_mod.reshape(DEPTH, 1, n))


NM_TM = 512
NM_AHEAD = 2


def _nm_matmul_kernel(dest_ref, x_ref, gp_ref, ys_hbm, g_ref, sh_ref, sc_ref, w_ref, b_ref, o_ref, xn_ref,
                      h_scr, ybuf, sems, *, n_cols, tn):
    tm = NM_TM
    i, j = pl.program_id(0), pl.program_id(1)
    n_tiles = pl.num_programs(0)
    n_buf = NM_AHEAD + 1

    @pl.when(j == 0)
    def _():
        @pl.when(i == 0)
        def _():
            for t in range(NM_AHEAD):
                _gather_rows(dest_ref, t * tm, ys_hbm, ybuf.at[t], sems.at[t], tm)

        slot = i % n_buf
        _wait_rows(ys_hbm, ybuf.at[slot], sems.at[slot], tm)
        x = x_ref[...] + gp_ref[...] * _unpack_bf16_pairs(ybuf[slot])
        xn_ref[...] = x
        h_scr[...] = _norm_mod(x, g_ref[...], sh_ref[...], sc_ref[...]).astype(BF16)

    def project(c):
        o_ref[...] = (_dot(h_scr[...], w_ref[:, c * tn:(c + 1) * tn]) + b_ref[...]).astype(o_ref.dtype)

    for c in range(n_cols - 1):
        pl.when(j == c)(functools.partial(project, c))

    @pl.when(j == n_cols - 1)
    def _():
        ahead = (i + NM_AHEAD) % n_buf
        _gather_rows(dest_ref, jnp.minimum(i + NM_AHEAD, n_tiles - 1) * tm, ys_hbm, ybuf.at[ahead], sems.at[ahead],
                     tm, static=True)
        project(n_cols - 1)

    @pl.when(jnp.logical_and(i == n_tiles - 1, j == n_cols - 1))
    def _():
        for t in range(1, NM_AHEAD + 1):
            late = (i + t) % n_buf
            _wait_rows(ys_hbm, ybuf.at[late], sems.at[late], tm)


def _nm_matmul(x_prev, dest, ys, mod_prev, g, mod, w, b, out_dtype, name):
    tm, tn = NM_TM, 1024
    n = w.shape[1]
    row = lambda: pl.BlockSpec((tm, D_MODEL), lambda i, j, d: (i, 0))
    return pl.pallas_call(
        functools.partial(_nm_matmul_kernel, n_cols=n // tn, tn=tn),
        grid_spec=pltpu.PrefetchScalarGridSpec(
            num_scalar_prefetch=1,
            grid=(T // tm, n // tn),
            in_specs=[
                row(),
                pl.BlockSpec((None, None, 1, D_MODEL), lambda i, j, d: (_cond_row(i * tm), 5, 0, 0)),
                pl.BlockSpec(memory_space=pl.ANY),
                pl.BlockSpec((1, D_MODEL), lambda i, j, d: (0, 0)),
                _mod_spec(tm, 0),
                _mod_spec(tm, 1),
                pl.BlockSpec((D_MODEL, n), lambda i, j, d: (0, 0), pipeline_mode=pl.Buffered(1)),
                pl.BlockSpec((1, tn), lambda i, j, d: (0, j)),
            ],
            out_specs=[pl.BlockSpec((tm, tn), lambda i, j, d: (i, j)), row()],
            scratch_shapes=[pltpu.VMEM((tm, D_MODEL), BF16), pltpu.VMEM((NM_AHEAD + 1, tm, HALF_D), jnp.uint32),
                            pltpu.SemaphoreType.DMA((NM_AHEAD + 1,))],
        ),
        out_shape=[jax.ShapeDtypeStruct((T, n), out_dtype), jax.ShapeDtypeStruct((T, D_MODEL), F32)],
        compiler_params=_cp(("arbitrary", "arbitrary"), 56),
        name=name,
    )(dest, x_prev, mod_prev, ys, g, mod, mod, w, b)


RESID_TM = 1024


def _resid_matmul_kernel(ap_ref, as_ref, w_ref, b_ref, x_ref, gt_ref, o_ref):
    def emit(a_ref):
        o_ref[...] = x_ref[...] + gt_ref[...] * (_dot(a_ref[...], w_ref[...]) + b_ref[...])

    is_ctx = pl.program_id(0) < TP // RESID_TM
    pl.when(is_ctx)(lambda: emit(ap_ref))
    pl.when(jnp.logical_not(is_ctx))(lambda: emit(as_ref))


def _resid_matmul(a_ctx, a_lat, w, b, x, mod, name):
    tm, tn = RESID_TM, 1024
    k = a_ctx.shape[1]
    n_ctx = TP // tm
    return pl.pallas_call(
        _resid_matmul_kernel,
        grid=(T // tm, D_MODEL // tn),
        in_specs=[
            pl.BlockSpec((tm, k), lambda i, j: (jnp.minimum(i, n_ctx - 1), 0)),
            pl.BlockSpec((tm, k), lambda i, j: (jnp.maximum(i - n_ctx, 0), 0)),
            pl.BlockSpec((k, tn), lambda i, j: (0, j)),
            pl.BlockSpec((1, tn), lambda i, j: (0, j)),
            pl.BlockSpec((tm, tn), lambda i, j: (i, j)),
            _mod_spec(tm, 2, tn, ncol=True),
        ],
        out_specs=pl.BlockSpec((tm, tn), lambda i, j: (i, j)),
        out_shape=jax.ShapeDtypeStruct((T, D_MODEL), F32),
        compiler_params=_cp(("parallel", "parallel")),
        name=name,
    )(a_ctx, a_lat, w, b, x, mod)


POOL_TILE = 256
POOL_HALO = 8


def _seq_pos(r0):
    is_ctx = r0 < TP
    loc0 = jnp.where(is_ctx, r0 % SEQ, (r0 - TP) % DEC_SEQ)
    seq_len = jnp.where(is_ctx, SEQ, DEC_SEQ)
    return loc0, seq_len


def _pool_kernel(xc_ref, xcp_ref, xcn_ref, xl_ref, xlp_ref, xln_ref, *rest):
    is_ctx = pl.program_id(0) < TP // POOL_TILE
    pl.when(is_ctx)(lambda: _pool_tile(xc_ref, xcp_ref, xcn_ref, *rest))
    pl.when(jnp.logical_not(is_ctx))(lambda: _pool_tile(xl_ref, xlp_ref, xln_ref, *rest))


def _pool_tile(x_ref, xp_ref, xn_ref, g_ref, sh_ref, sc_ref, gt_ref, pw_ref, ps_ref, o_ref, hz_scr):
    tm, hl = POOL_TILE, POOL_HALO
    loc0, seq_len = _seq_pos(pl.program_id(0) * tm)
    has_prev = loc0 > 0
    has_next = loc0 + tm < seq_len
    g, sh, sc = g_ref[...], sh_ref[...], sc_ref[...]
    x = x_ref[...]
    h = _norm_mod(x, g, sh, sc)
    hz_scr[0:hl, :] = jnp.where(has_prev, _norm_mod(xp_ref[...], g, sh, sc), 0.0)
    hz_scr[hl:hl + tm, :] = h
    hz_scr[hl + tm:, :] = jnp.where(has_next, _norm_mod(xn_ref[...], g, sh, sc), 0.0)
    tl = loc0 + lax.broadcasted_iota(jnp.int32, (tm, 1), 0)
    outs = []
    for gi, w in enumerate(POOL_WINDOWS):
        cs = slice(gi * POOL_GROUP, (gi + 1) * POOL_GROUP)
        s = jnp.zeros((tm, POOL_GROUP), F32)
        for off in range(-(w // 2), w - w // 2):
            s = s + hz_scr[hl + off:hl + off + tm, cs]
        lo = jnp.maximum(tl - w // 2, 0)
        hi = jnp.minimum(tl + (w - w // 2), seq_len)
        d = s / (hi - lo).astype(F32) - h[:, cs]
        outs.append(_dot(d.astype(BF16), pw_ref[gi]))
    out = jnp.concatenate(outs, axis=1) * ps_ref[...]
    o_ref[...] = x + gt_ref[...] * out


def _pool_mixer(x_ctx, x_lat, g, mod, pool_w, pool_scale):
    tm, hl = POOL_TILE, POOL_HALO
    r = tm // hl

    def stream(first_tile, rows):
        tile = lambda i: jnp.clip(i - first_tile, 0, rows // tm - 1)
        return [pl.BlockSpec((tm, D_MODEL), lambda i: (tile(i), 0)),
                pl.BlockSpec((hl, D_MODEL), lambda i: (jnp.maximum(tile(i) * r - 1, 0), 0)),
                pl.BlockSpec((hl, D_MODEL), lambda i: (jnp.minimum((tile(i) + 1) * r, rows // hl - 1), 0))]

    return pl.pallas_call(
        _pool_kernel,
        grid=(T // tm,),
        in_specs=stream(0, TP) + stream(TP // tm, TS) + [
            pl.BlockSpec((1, D_MODEL), lambda i: (0, 0)),
            _mod_spec(tm, 0),
            _mod_spec(tm, 1),
            _mod_spec(tm, 2),
            pl.BlockSpec((len(POOL_WINDOWS), POOL_GROUP, POOL_GROUP), lambda i: (0, 0, 0)),
            pl.BlockSpec((1, D_MODEL), lambda i: (0, 0)),
        ],
        out_specs=pl.BlockSpec((tm, D_MODEL), lambda i: (i, 0)),
        out_shape=jax.ShapeDtypeStruct((T, D_MODEL), F32),
        scratch_shapes=[pltpu.VMEM((tm + 2 * hl, D_MODEL), F32)],
        compiler_params=_cp(("parallel",)),
        name="pool_mixer",
    )(x_ctx, x_ctx, x_ctx, x_lat, x_lat, x_lat, g, mod, mod, mod, pool_w.astype(BF16),
      pool_scale.reshape(1, D_MODEL))


CONV_TILE = 256
CONV_HALO = 16


def _conv3_kernel(u_ref, up_ref, un_ref, cw_ref, cb_ref, o_ref, scr):
    tm, hl = CONV_TILE, CONV_HALO
    loc0, seq_len = _seq_pos(pl.program_id(0) * tm)
    has_prev = loc0 > 0
    has_next = loc0 + tm < seq_len
    scr[0:hl, :] = jnp.where(has_prev, up_ref[...].astype(F32), 0.0)
    scr[hl:hl + tm, :] = u_ref[...].astype(F32)
    scr[hl + tm:, :] = jnp.where(has_next, un_ref[...].astype(F32), 0.0)
    out = (scr[hl - 1:hl - 1 + tm, :] * cw_ref[0:1, :] + scr[hl:hl + tm, :] * cw_ref[1:2, :]
           + scr[hl + 1:hl + 1 + tm, :] * cw_ref[2:3, :] + cb_ref[...])
    o_ref[...] = out.astype(o_ref.dtype)


def _conv3(u0, conv_w, conv_b):
    tm, hl, tc = CONV_TILE, CONV_HALO, D_MODEL
    r = tm // hl
    n = u0.shape[1]
    return pl.pallas_call(
        _conv3_kernel,
        grid=(T // tm, n // tc),
        in_specs=[
            pl.BlockSpec((tm, tc), lambda i, j: (i, j)),
            pl.BlockSpec((hl, tc), lambda i, j: (jnp.maximum(i * r - 1, 0), j)),
            pl.BlockSpec((hl, tc), lambda i, j: (jnp.minimum((i + 1) * r, T // hl - 1), j)),
            pl.BlockSpec((3, tc), lambda i, j: (0, j)),
            pl.BlockSpec((1, tc), lambda i, j: (0, j)),
        ],
        out_specs=pl.BlockSpec((tm, tc), lambda i, j: (i, j)),
        out_shape=jax.ShapeDtypeStruct((T, n), BF16),
        scratch_shapes=[pltpu.VMEM((tm + 2 * hl, tc), F32)],
        compiler_params=_cp(("parallel", "parallel")),
        name="hyena_conv3",
    )(u0, u0, u0, conv_w, conv_b.reshape(1, n))


FILT_TILE = 256


HYENA_BLOCK = 1024


T_LANE = LANE - 1


def _filter_mlp_kernel(emb_ref, w1_ref, b1_ref, w2_ref, b2_ref, fr_ref, o_ref):
    emb = emb_ref[...]
    fr = fr_ref[...]
    a = jnp.sin(fr * (_dot3(emb, w1_ref[...]) + b1_ref[...]))
    a = jnp.sin(fr * (_dot3(a, w2_ref[...]) + b2_ref[...]))
    lane = lax.broadcasted_iota(jnp.int32, a.shape, 1)
    o_ref[...] = jnp.where(lane == T_LANE, emb[:, 0:1], a)


def _filter_mlp(pos, L, f_w1, f_b1, f_w2, f_b2, f_freq):
    assert HYENA_FILTER_HIDDEN <= T_LANE
    tl = FILT_TILE
    rows = pos.shape[0]
    small = lambda: pl.BlockSpec((LANE, LANE), lambda i: (0, 0))
    vec = lambda: pl.BlockSpec((1, LANE), lambda i: (0, 0))
    return pl.pallas_call(
        _filter_mlp_kernel,
        grid=(rows // tl,),
        in_specs=[pl.BlockSpec((tl, LANE), lambda i: (i, 0)), small(), vec(), small(), vec(), vec()],
        out_specs=pl.BlockSpec((tl, LANE), lambda i: (i, 0)),
        out_shape=jax.ShapeDtypeStruct((rows, LANE), F32),
        compiler_params=_cp(("parallel",)),
        name="hyena_filter_mlp",
    )(_filter_embedding(pos, L), _pad2(f_w1, LANE, LANE), _pad2(f_b1[None], 1, LANE),
      _pad2(f_w2, LANE, LANE), _pad2(f_b2[None], 1, LANE), _pad2(f_freq[None], 1, LANE))


def _filter_kernel(h1_ref, h2_ref, w3a_ref, dca_ref, w3b_ref, dcb_ref, fa_ref, fb_ref, *, blk):
    def taps(h, w3_ref, dc_ref):
        return _dot3(h, w3_ref[...]) * jnp.exp(-h[:, T_LANE:] * jnp.abs(dc_ref[...]))

    pos = taps(h1_ref[...], w3a_ref, dca_ref)
    neg = taps(h2_ref[...], w3b_ref, dcb_ref)
    m = (pl.program_id(0) * FILT_TILE + lax.broadcasted_iota(jnp.int32, (FILT_TILE, 1), 0)) % blk
    fa_ref[...] = jnp.where(m == 0, pos, pos + neg).astype(BF16)
    fb_ref[...] = jnp.where(m == 0, 0.0, neg - pos).astype(BF16)


def _pad2(a, rows, cols):
    return jnp.pad(a, ((0, rows - a.shape[0]), (0, cols - a.shape[1])))


def _filter_positions(L, blk):
    n_blk = L // blk
    m = np.arange(blk)
    p1, p2 = [], []
    for d in range(-(n_blk - 1), n_blk):
        if d >= 1:
            p1.append(d * blk + m), p2.append(d * blk - m)
        elif d == 0:
            p1.append(m), p2.append(m)
        else:
            p1.append(-d * blk - m), p2.append(-d * blk + m)
    return np.concatenate(p1), np.concatenate(p2)


def _filter_embedding(pos, L):
    t = jnp.asarray(pos, F32) / L
    bands = jnp.linspace(1e-4, HYENA_EMB_BANDS - 1, HYENA_EMB_BANDS, dtype=F32)
    ang = (2 * math.pi) * t[:, None] * bands[None, :]
    return _pad2(jnp.concatenate([t[:, None], jnp.cos(ang), -jnp.sin(ang)], axis=-1), pos.shape[0], LANE)


def _hyena_filters(L, blk, f_w1, f_b1, f_w2, f_b2, f_freq, f_w3, decay):
    n_blk = L // blk
    p1, p2 = _filter_positions(L, blk)
    rows = p1.shape[0]
    tl = FILT_TILE
    tiles_per_lag = blk // tl
    lag = lambda i: i // tiles_per_lag - (n_blk - 1)
    col1 = lambda i, n: 2 * n + jnp.where(lag(i) >= 0, 0, 1)
    col2 = lambda i, n: 2 * n + jnp.where(lag(i) >= 1, 0, 1)
    n_tiles = rows // tl
    out = pl.BlockSpec((None, tl, D_MODEL), lambda i, n: (n, i, 0))
    w3 = _pad2(f_w3, LANE, f_w3.shape[1])
    hidden = _filter_mlp(np.concatenate([p1, p2]), L, f_w1, f_b1, f_w2, f_b2, f_freq)
    return pl.pallas_call(
        functools.partial(_filter_kernel, blk=blk),
        grid=(n_tiles, 2),
        in_specs=[
            pl.BlockSpec((tl, LANE), lambda i, n: (i, 0)),
            pl.BlockSpec((tl, LANE), lambda i, n: (n_tiles + i, 0)),
            pl.BlockSpec((LANE, D_MODEL), lambda i, n: (0, col1(i, n))),
            pl.BlockSpec((1, D_MODEL), lambda i, n: (0, col1(i, n))),
            pl.BlockSpec((LANE, D_MODEL), lambda i, n: (0, col2(i, n))),
            pl.BlockSpec((1, D_MODEL), lambda i, n: (0, col2(i, n))),
        ],
        out_specs=[out, out],
        out_shape=[jax.ShapeDtypeStruct((2, rows, D_MODEL), BF16)] * 2,
        compiler_params=_cp(("parallel", "parallel")),
        name="hyena_filters",
    )(hidden, hidden, w3, decay[None], w3, decay[None])


def _dft_mats(L):
    r = int(math.isqrt(L))
    k2 = 2 * jnp.arange(L, dtype=jnp.int32)[:, None] + 1
    n1 = r * jnp.arange(L // r, dtype=jnp.int32)[None, :]
    n2 = jnp.arange(r, dtype=jnp.int32)[None, :]
    sc = math.pi / (2 * L)
    aa = ((k2 * n1) % (4 * L)).astype(F32) * sc
    ab = ((k2 * n2) % (4 * L)).astype(F32) * sc
    ca, sa, cb, sb = jnp.cos(aa)[:, :, None], jnp.sin(aa)[:, :, None], jnp.cos(ab)[:, None, :], jnp.sin(ab)[:, None, :]
    c = (ca * cb - sa * sb).reshape(L, L)
    s = (sa * cb + ca * sb).reshape(L, L)
    return c.astype(BF16), s.astype(BF16), c.T.astype(BF16), s.T.astype(BF16)


def _dft_tiles(L):
    return min(512, L), 512


def _dft_filter_kernel(c_ref, s_ref, a_ref, b_ref, gr_ref, gi_ref):
    gr_ref[...] = _dot(c_ref[...], a_ref[...]).astype(gr_ref.dtype)
    gi_ref[...] = _dot(s_ref[...], b_ref[...]).astype(gi_ref.dtype)


def _dft_filter(cm, sm, fa, fb, L):
    tf, tn = _dft_tiles(L)
    n = fa.shape[0]
    mat = lambda: pl.BlockSpec((tf, L), lambda k, c, s: (k, 0))
    rhs = lambda: pl.BlockSpec((None, L, tn), lambda k, c, s: (s, 0, c))
    out = pl.BlockSpec((None, tf, tn), lambda k, c, s: (s, k, c))
    return pl.pallas_call(
        _dft_filter_kernel,
        grid=(L // tf, D_MODEL // tn, n),
        in_specs=[mat(), mat(), rhs(), rhs()],
        out_specs=[out, out],
        out_shape=[jax.ShapeDtypeStruct((n, L, D_MODEL), BF16)] * 2,
        compiler_params=_cp(("parallel", "parallel", "parallel")),
        name="hyena_filter_dft",
    )(cm, sm, fa, fb)


FWD_TF = 256


def _dft_fwd_kernel(c_ref, s_ref, z_ref, gr_ref, gi_ref, yr_ref, yi_ref, *, n_blk, blk, bpb):
    c, s = c_ref[...], s_ref[...]
    for bb in range(bpb):
        zc, zs = [], []
        for j in range(n_blk):
            r = (bb * n_blk + j) * blk
            zj = z_ref[r:r + blk, :]
            zc.append(_dot(c, zj).astype(BF16))
            zs.append(_dot(s, zj).astype(BF16))
        for i in range(n_blk):
            yr = yi = None
            for j in range(n_blk):
                lag = i - j + n_blk - 1
                gr, gi = gr_ref[lag], gi_ref[lag]
                tr = gr * zc[j] + gi * zs[j]
                ti = gi * zc[j] - gr * zs[j]
                yr = tr if yr is None else yr + tr
                yi = ti if yi is None else yi + ti
            yr_ref[bb, i] = yr.astype(BF16)
            yi_ref[bb, i] = yi.astype(BF16)


def _seqs_per_step(L):
    return max(1, 2048 // L)


def _dft_fwd(cm, sm, z, z_rowblk, z_colblk, gr, gi, order, nb, L, blk):
    n_blk = L // blk
    bpb = _seqs_per_step(L)
    assert nb % bpb == 0 and z_rowblk % bpb == 0
    tf, tn = min(FWD_TF, blk), 512
    mat = lambda: pl.BlockSpec((tf, blk), lambda k, c, b: (k, 0))
    gsp = lambda: pl.BlockSpec((None, 2 * n_blk - 1, tf, tn), lambda k, c, b: (order, 0, k, c))
    out = pl.BlockSpec((bpb, n_blk, tf, tn), lambda k, c, b: (b, 0, k, c))
    return pl.pallas_call(
        functools.partial(_dft_fwd_kernel, n_blk=n_blk, blk=blk, bpb=bpb),
        grid=(blk // tf, D_MODEL // tn, nb // bpb),
        in_specs=[mat(), mat(),
                  pl.BlockSpec((bpb * L, tn), lambda k, c, b: (z_rowblk // bpb + b, z_colblk + c)),
                  gsp(), gsp()],
        out_specs=[out, out],
        out_shape=[jax.ShapeDtypeStruct((nb, n_blk, blk, D_MODEL), BF16)] * 2,
        compiler_params=_cp(("parallel", "parallel", "parallel")),
        name="hyena_dft_fwd",
    )(cm, sm, z, gr, gi)


def _dft_inv_kernel(ct_ref, st_ref, yr_ref, yi_ref, z_ref, gt_ref, sk_ref, o_ref, *, inv_len, bpb, tt):
    for bb in range(bpb):
        rows = slice(bb * tt, (bb + 1) * tt)
        y = (_dot(ct_ref[...], yr_ref[bb]) - _dot(st_ref[...], yi_ref[bb])) * inv_len
        o_ref[rows, :] = (gt_ref[rows, :].astype(F32) * (y + sk_ref[...] * z_ref[rows, :].astype(F32))).astype(BF16)


def _dft_inv(ctm, stm, yr, yi, z, z_rowblk, z_colblk, gate, g_rowblk, g_colblk, skip, nb, L):
    tt, tn = _dft_tiles(L)
    rpb = L // tt
    bpb = _seqs_per_step(L) if rpb == 1 else 1
    assert nb % bpb == 0 and z_rowblk % bpb == 0 and g_rowblk % bpb == 0
    mat = lambda: pl.BlockSpec((tt, L), lambda t, c, b: (t, 0))
    spec = lambda: pl.BlockSpec((bpb, L, tn), lambda t, c, b: (b, 0, c))
    rows = lambda blk0: (lambda t, c, b: ((blk0 + b * bpb * rpb + t) // bpb))
    return pl.pallas_call(
        functools.partial(_dft_inv_kernel, inv_len=1.0 / L, bpb=bpb, tt=tt),
        grid=(rpb, D_MODEL // tn, nb // bpb),
        in_specs=[mat(), mat(), spec(), spec(),
                  pl.BlockSpec((bpb * tt, tn), lambda t, c, b: (rows(z_rowblk)(t, c, b), z_colblk + c)),
                  pl.BlockSpec((bpb * tt, tn), lambda t, c, b: (rows(g_rowblk)(t, c, b), g_colblk + c)),
                  pl.BlockSpec((1, tn), lambda t, c, b: (0, c))],
        out_specs=pl.BlockSpec((bpb * tt, tn), lambda t, c, b: (rows(0)(t, c, b), c)),
        out_shape=jax.ShapeDtypeStruct((nb * L, D_MODEL), BF16),
        compiler_params=_cp(("parallel", "parallel", "parallel")),
        name="hyena_dft_inv",
    )(ctm, stm, yr, yi, z, gate, skip)


def _hyena_stream(u, row0, nb, L, fparams, skip):
    blk = min(HYENA_BLOCK, L)
    n_blk = L // blk
    n_lag = 2 * n_blk - 1
    cm, sm, ctm, stm = _dft_mats(blk)
    fa, fb = _hyena_filters(L, blk, *fparams)
    seg = lambda a: a.reshape(2 * n_lag, blk, D_MODEL)
    gr, gi = _dft_filter(cm, sm, seg(fa), seg(fb), blk)
    gr, gi = (a.reshape(2, n_lag, blk, D_MODEL) for a in (gr, gi))
    tt, tn = _dft_tiles(blk)
    ncb = D_MODEL // tn
    blocks = lambda a: a.reshape(nb * n_blk, blk, D_MODEL)

    yr, yi = _dft_fwd(cm, sm, u, row0 // L, 0, gr, gi, 0, nb, L, blk)
    z1 = _dft_inv(ctm, stm, blocks(yr), blocks(yi), u, row0 // tt, 0, u, row0 // tt, ncb, skip[0:1],
                  nb * n_blk, blk)
    yr, yi = _dft_fwd(cm, sm, z1, 0, 0, gr, gi, 1, nb, L, blk)
    return _dft_inv(ctm, stm, blocks(yr), blocks(yi), z1, 0, 0, u, row0 // tt, 2 * ncb, skip[1:2],
                    nb * n_blk, blk)


def _hyena_mixer(pending, g, mod, w_in, b_in, conv_w, conv_b, f_w1, f_b1, f_w2, f_b2, f_freq, f_w3, decay, skip,
                 w_out, b_out):
    u0, x = _nm_matmul(*pending, g, mod, w_in.astype(BF16), b_in.reshape(1, -1), BF16, "hyena_in_proj")
    u = _conv3(u0, conv_w, conv_b)
    fparams = (f_w1, f_b1, f_w2, f_b2, f_freq, f_w3, decay)
    zp = _hyena_stream(u, 0, BATCH, SEQ, fparams, skip)
    zs = _hyena_stream(u, TP, DEC_BATCH, DEC_SEQ, fparams, skip)
    return _resid_matmul(zp, zs, w_out.astype(BF16), b_out.reshape(1, -1), x, mod, "hyena_out_proj")


def _rope_tables():
    pos = jnp.arange(DEC_SEQ, dtype=jnp.int32)
    row = (pos // GRID_W).astype(F32)
    col = (pos % GRID_W).astype(F32)
    axis_dim = HEAD_DIM // 2
    inv_freq = ROPE_THETA ** (-jnp.arange(0, axis_dim, 2, dtype=F32) / axis_dim)
    ar = row[:, None] * inv_freq[None, :]
    ac = col[:, None] * inv_freq[None, :]
    cos = jnp.concatenate([jnp.cos(ar), jnp.cos(ar), jnp.cos(ac), jnp.cos(ac)], axis=-1)
    sin = jnp.concatenate([-jnp.sin(ar), jnp.sin(ar), -jnp.sin(ac), jnp.sin(ac)], axis=-1)
    return cos, sin


QKV_TM = 512
QKV_TN = 1024
PAIR = 2 * HEAD_DIM


QKV_AHEAD = 2


def _qkv_kernel(dest_ref, x_ref, gp_ref, ys_hbm, g_ref, sh_ref, sc_ref, w_ref, qn_ref, kn_ref, cos_ref, sin_ref,
                q_ref, k_ref, v_ref, nk_ref, nv_ref, xn_ref, h_scr, ybuf, sems, *, use_norm):
    tm = QKV_TM
    i, j = pl.program_id(0), pl.program_id(1)
    n_tiles = pl.num_programs(0)
    n_buf = QKV_AHEAD + 1

    @pl.when(j == 0)
    def _():
        @pl.when(i == 0)
        def _():
            for t in range(QKV_AHEAD):
                _gather_rows(dest_ref, t * tm, ys_hbm, ybuf.at[t], sems.at[t], tm)

        slot = i % n_buf
        _wait_rows(ys_hbm, ybuf.at[slot], sems.at[slot], tm)
        x = x_ref[...] + gp_ref[...] * _unpack_bf16_pairs(ybuf[slot])
        xn_ref[...] = x
        h_scr[...] = _norm_mod(x, g_ref[...], sh_ref[...], sc_ref[...]).astype(BF16)

    def gather_ahead():
        ahead = (i + QKV_AHEAD) % n_buf
        _gather_rows(dest_ref, jnp.minimum(i + QKV_AHEAD, n_tiles - 1) * tm, ys_hbm, ybuf.at[ahead],
                     sems.at[ahead], tm, static=True)

    quarter = HEAD_DIM // 4
    scale = HEAD_DIM ** -0.5 * LOG2E

    def head(xh, gn):
        if use_norm:
            xh = xh * lax.rsqrt(jnp.mean(xh * xh, axis=-1, keepdims=True) + NORM_EPS) * gn
        return xh

    def rope(xh):
        lane = lax.broadcasted_iota(jnp.int32, (tm, HEAD_DIM), 1)
        first = (lane % (2 * quarter)) < quarter
        partner = jnp.where(first, pltpu.roll(xh, HEAD_DIM - quarter, 1), pltpu.roll(xh, quarter, 1))
        return xh * cos_ref[...] + partner * sin_ref[...]

    def proj(c0):
        return _dot(h_scr[...], w_ref[:, c0:c0 + PAIR])

    def q_tile(latent):
        for p in range(QKV_TN // PAIR):
            acc = proj(p * PAIR)
            for t in range(2):
                xh = head(acc[:, t * HEAD_DIM:(t + 1) * HEAD_DIM], qn_ref[...])
                xh = rope(xh) if latent else xh
                c0 = p * PAIR + t * HEAD_DIM
                q_ref[:, c0:c0 + HEAD_DIM] = (xh * scale).astype(BF16)

    def kv_tile(latent):
        gather_ahead()
        for p in range(KV_DIM // PAIR):
            acc = proj(p * PAIR)
            for t in range(2):
                c0 = p * PAIR + t * HEAD_DIM
                kh = head(acc[:, t * HEAD_DIM:(t + 1) * HEAD_DIM], kn_ref[...])
                if not latent:
                    nk_ref[:, c0:c0 + HEAD_DIM] = kh
                k_ref[:, c0:c0 + HEAD_DIM] = (rope(kh) if latent else kh).astype(BF16)
        for p in range(KV_DIM // PAIR):
            acc = proj(KV_DIM + p * PAIR)
            if not latent:
                nv_ref[:, p * PAIR:(p + 1) * PAIR] = acc
            v_ref[:, p * PAIR:(p + 1) * PAIR] = acc.astype(BF16)

    is_ctx = i < TP // tm
    is_q = j < D_MODEL // QKV_TN
    for latent in (False, True):
        stream = jnp.logical_not(is_ctx) if latent else is_ctx
        pl.when(jnp.logical_and(stream, is_q))(functools.partial(q_tile, latent))
        pl.when(jnp.logical_and(stream, jnp.logical_not(is_q)))(functools.partial(kv_tile, latent))

    @pl.when(jnp.logical_and(i == n_tiles - 1, j == pl.num_programs(1) - 1))
    def _():
        for t in range(1, QKV_AHEAD + 1):
            late = (i + t) % n_buf
            _wait_rows(ys_hbm, ybuf.at[late], sems.at[late], tm)


def _qkv_proj(x_prev, dest, ys, mod_prev, g, mod, w_qkv, q_norm, k_norm, use_norm, rope):
    tm, tn = QKV_TM, QKV_TN
    n_ctx = TP // tm
    n_q = D_MODEL // tn
    tab = lambda: pl.BlockSpec((tm, HEAD_DIM), lambda i, j, d: (jnp.maximum(i - n_ctx, 0) % (DEC_SEQ // tm), 0))
    kv = lambda: pl.BlockSpec((tm, KV_DIM), lambda i, j, d: (i, 0))
    new = lambda: pl.BlockSpec((tm, KV_DIM), lambda i, j, d: (jnp.minimum(i, n_ctx - 1), 0))
    row = lambda: pl.BlockSpec((tm, D_MODEL), lambda i, j, d: (i, 0))
    return pl.pallas_call(
        functools.partial(_qkv_kernel, use_norm=use_norm),
        grid_spec=pltpu.PrefetchScalarGridSpec(
            num_scalar_prefetch=1,
            grid=(T // tm, QKV_DIM // tn),
            in_specs=[
                row(),
                pl.BlockSpec((None, None, 1, D_MODEL), lambda i, j, d: (_cond_row(i * tm), 5, 0, 0)),
                pl.BlockSpec(memory_space=pl.ANY),
                pl.BlockSpec((1, D_MODEL), lambda i, j, d: (0, 0)),
                _mod_spec(tm, 0),
                _mod_spec(tm, 1),
                pl.BlockSpec((D_MODEL, tn), lambda i, j, d: (0, j)),
                pl.BlockSpec((1, HEAD_DIM), lambda i, j, d: (0, 0)),
                pl.BlockSpec((1, HEAD_DIM), lambda i, j, d: (0, 0)),
                tab(), tab(),
            ],
            out_specs=[pl.BlockSpec((tm, tn), lambda i, j, d: (i, jnp.minimum(j, n_q - 1))), kv(), kv(), new(), new(),
                       row()],
            scratch_shapes=[pltpu.VMEM((tm, D_MODEL), BF16), pltpu.VMEM((QKV_AHEAD + 1, tm, HALF_D), jnp.uint32),
                            pltpu.SemaphoreType.DMA((QKV_AHEAD + 1,))],
        ),
        out_shape=[jax.ShapeDtypeStruct((T, D_MODEL), BF16), jax.ShapeDtypeStruct((T, KV_DIM), BF16),
                   jax.ShapeDtypeStruct((T, KV_DIM), BF16), jax.ShapeDtypeStruct((TP, KV_DIM), F32),
                   jax.ShapeDtypeStruct((TP, KV_DIM), F32), jax.ShapeDtypeStruct((T, D_MODEL), F32)],
        compiler_params=_cp(("arbitrary", "arbitrary"), 56),
        name="qkv_proj",
    )(dest, x_prev, mod_prev, ys, g, mod, mod, w_qkv, q_norm.reshape(1, HEAD_DIM), k_norm.reshape(1, HEAD_DIM),
      *rope)


LOG2E = math.log2(math.e)
ATTN_TQ = 256


def _attn_kernel(*refs, tq, seq_len, has_ctx, windowed, has_sink):
    it = iter(refs)
    q_ref, k_ref, v_ref = next(it), next(it), next(it)
    kc_ref, vc_ref = (next(it), next(it)) if has_ctx else (None, None)
    sink_ref = next(it) if has_sink else None
    o_ref = next(it)
    if windowed:
        i = pl.program_id(2)
        span = tq + 2 * WINDOW
        start = pl.multiple_of(jnp.clip(i * tq - WINDOW, 0, seq_len - span), WINDOW)
        qpos = i * tq + lax.broadcasted_iota(jnp.int32, (tq, 1), 0)
        kpos = start + lax.broadcasted_iota(jnp.int32, (1, span), 1)
        segs = [(k_ref, v_ref, pl.ds(start, span), jnp.abs(kpos - qpos) <= WINDOW)]
    else:
        segs = [(k_ref, v_ref, slice(None), None)]
    if has_ctx:
        segs.append((kc_ref, vc_ref, slice(None), None))
    for h in range(KV_GROUP):
        hs = slice(h * HEAD_DIM, (h + 1) * HEAD_DIM)
        qh = q_ref[:, hs]
        scores = []
        m = None
        for kr, _, rows, mask in segs:
            s = lax.dot_general(qh, kr[rows, :], (((1,), (1,)), ((), ())), preferred_element_type=F32)
            if mask is not None:
                s = jnp.where(mask, s, NEG_INF)
            scores.append(s)
            ms = jnp.max(s, axis=-1, keepdims=True)
            m = ms if m is None else jnp.maximum(m, ms)
        if has_sink:
            sk = sink_ref[pl.program_id(1) * KV_GROUP + h]
            m = jnp.maximum(m, sk)
        l = jnp.exp2(sk - m) if has_sink else jnp.zeros_like(m)
        acc = jnp.zeros((tq, HEAD_DIM), F32)
        for (_, vr, rows, _), s in zip(segs, scores):
            p = jnp.exp2(s - m)
            l = l + jnp.sum(p, axis=-1, keepdims=True)
            acc = acc + _dot(p.astype(BF16), vr[rows, :])
        o_ref[:, hs] = (acc / l).astype(BF16)


def _attention(q, row0, k, v, k_row0, n_keys, k_ctx, v_ctx, sink, nb, L, windowed):
    tq = min(ATTN_TQ, L)
    nq = L // tq
    q_blk0 = row0 // tq
    seq0 = k_row0 // n_keys
    own = lambda: pl.BlockSpec((n_keys, HEAD_DIM), lambda b, g, i: (seq0 + b, g))
    in_specs = [pl.BlockSpec((tq, KV_GROUP * HEAD_DIM), lambda b, g, i: (q_blk0 + b * nq + i, g)), own(), own()]
    args = [q, k, v]
    if k_ctx is not None:
        ctx = lambda: pl.BlockSpec((None, PAST_LEN, HEAD_DIM), lambda b, g, i: (b, 0, g))
        in_specs += [ctx(), ctx()]
        args += [k_ctx, v_ctx]
    if sink is not None:
        in_specs.append(pl.BlockSpec(memory_space=pltpu.SMEM))
        args.append(sink.astype(F32) * LOG2E)
    return pl.pallas_call(
        functools.partial(_attn_kernel, tq=tq, seq_len=L, has_ctx=k_ctx is not None, windowed=windowed,
                          has_sink=sink is not None),
        grid=(nb, N_KV_HEADS, nq),
        in_specs=in_specs,
        out_specs=pl.BlockSpec((tq, KV_GROUP * HEAD_DIM), lambda b, g, i: (b * nq + i, g)),
        out_shape=jax.ShapeDtypeStruct((nb * L, D_MODEL), BF16),
        compiler_params=_cp(("parallel", "parallel", "parallel"), 56),
        name="attention",
    )(*args)


def _attn_mixer(pending, g, mod, w_qkv, q_norm, k_norm, use_norm, sink, w_o, cache_k, cache_v, windowed, rope):
    q, k, v, new_k, new_v, x = _qkv_proj(*pending, g, mod, w_qkv.astype(BF16), q_norm, k_norm, use_norm, rope)
    op = _attention(q, 0, k, v, 0, SEQ, None, None, sink, BATCH, SEQ, False)
    kc = cache_k.reshape(DEC_BATCH, PAST_LEN, KV_DIM).astype(BF16)
    vc = cache_v.reshape(DEC_BATCH, PAST_LEN, KV_DIM).astype(BF16)
    if windowed:
        osm = _attention(q, TP, k, v, TP, DEC_SEQ, kc, vc, sink, DEC_BATCH, DEC_SEQ, True)
    else:
        n_keys = DEC_SEQ + PAST_LEN
        both = lambda a, c: jnp.concatenate([a[TP:].reshape(DEC_BATCH, DEC_SEQ, KV_DIM), c], axis=1).reshape(
            DEC_BATCH * n_keys, KV_DIM)
        osm = _attention(q, TP, both(k, kc), both(v, vc), 0, n_keys, None, None, sink, DEC_BATCH, DEC_SEQ, False)
    x = _resid_matmul(op, osm, w_o.astype(BF16), jnp.zeros((1, D_MODEL), F32), x, mod, "attn_out_proj")
    shape = (BATCH, SEQ, N_KV_HEADS, HEAD_DIM)
    return x, new_k.reshape(shape), new_v.reshape(shape)


ROUTE_TILE = 512
ROUTE_ROWS = 32


def _router_kernel(x_ref, g_ref, sh_ref, sc_ref, wr_ref, br_ref, xh_ref, rt_ref, cnt_ref, carry):
    tm = ROUTE_TILE
    i = pl.program_id(0)

    @pl.when(i == 0)
    def _():
        carry[...] = jnp.zeros_like(carry)

    h = _norm_mod(x_ref[...], g_ref[...], sh_ref[...], sc_ref[...])
    xh_ref[:, :HALF_D] = _pack_bf16_pairs(h)
    logits = _dot(h.astype(BF16), wr_ref[...])
    s = _sigmoid(logits.T[:N_EXPERTS, :])
    sb = s + br_ref[...]
    u = [s[e:e + 1, :] for e in range(N_EXPERTS)]
    v = [sb[e:e + 1, :] for e in range(N_EXPERTS)]

    gscore = []
    for gq in range(N_EXPERT_GROUPS):
        m = v[4 * gq:4 * gq + 4]
        best = m[PAIR_LO[0]] + m[PAIR_HI[0]]
        for a, b in zip(PAIR_LO[1:], PAIR_HI[1:]):
            best = jnp.maximum(best, m[a] + m[b])
        gscore.append(best)
    gidx = jnp.zeros((1, tm), jnp.int32)
    gbest = gscore[0]
    for gq in range(1, N_EXPERT_GROUPS):
        upd = gscore[gq] > gbest
        gidx = jnp.where(upd, gq, gidx)
        gbest = jnp.where(upd, gscore[gq], gbest)

    def pick(rows, j):
        out = rows[j]
        for gq in range(1, N_EXPERT_GROUPS):
            out = jnp.where(gidx == gq, rows[4 * gq + j], out)
        return out

    vin = [pick(v, j) for j in range(EXPERTS_PER_GROUP)]
    uin = [pick(u, j) for j in range(EXPERTS_PER_GROUP)]
    i1 = jnp.zeros((1, tm), jnp.int32)
    m1 = vin[0]
    for j in range(1, EXPERTS_PER_GROUP):
        upd = vin[j] > m1
        i1 = jnp.where(upd, j, i1)
        m1 = jnp.where(upd, vin[j], m1)
    i2 = jnp.full((1, tm), -1, jnp.int32)
    m2 = jnp.full((1, tm), -jnp.inf, F32)
    for j in range(EXPERTS_PER_GROUP):
        upd = (i1 != j) & (vin[j] > m2)
        i2 = jnp.where(upd, j, i2)
        m2 = jnp.where(upd, vin[j], m2)

    def sel(rows, idx):
        out = rows[0]
        for j in range(1, EXPERTS_PER_GROUP):
            out = jnp.where(idx == j, rows[j], out)
        return out

    w1, w2 = sel(uin, i1), sel(uin, i2)
    wsum = w1 + w2
    w1, w2 = w1 / wsum, w2 / wsum
    first_lo = i1 < i2
    lo = jnp.where(first_lo, i1, i2)
    hi = jnp.where(first_lo, i2, i1)
    w_lo = jnp.where(first_lo, w1, w2)
    w_hi = jnp.where(first_lo, w2, w1)
    pair = jnp.where(lo == 0, hi - 1, jnp.where(lo == 1, hi + 1, 5))
    bucket = gidx * len(PAIR_LO) + pair

    onehot = (lax.broadcasted_iota(jnp.int32, (ROUTE_ROWS, tm), 0) == bucket)
    tri = (lax.broadcasted_iota(jnp.int32, (tm, tm), 0) <= lax.broadcasted_iota(jnp.int32, (tm, tm), 1))
    cum = _dot(jnp.where(onehot, 1.0, 0.0).astype(BF16), jnp.where(tri, 1.0, 0.0).astype(BF16))
    rank = jnp.sum(jnp.where(onehot, cum - 1.0 + carry[...], 0.0), axis=0, keepdims=True)
    carry[...] = carry[...] + cum[:, tm - 1:tm]
    cnt_ref[...] = jnp.broadcast_to(carry[...], (ROUTE_ROWS, LANE))

    rt_ref[...] = jnp.zeros_like(rt_ref)
    rt_ref[0:1, :] = bucket.astype(F32)
    rt_ref[1:2, :] = rank
    wt = jnp.concatenate([w_lo, w_hi, jnp.zeros((LANE - 2, tm), F32)], axis=0)
    xh_ref[:, HALF_D:] = lax.bitcast_convert_type(wt.T, jnp.uint32)


def _router(x, g, mod, w_router, b_router):
    tm = ROUTE_TILE
    wr = _pad2(w_router, D_MODEL, LANE).astype(BF16)
    return pl.pallas_call(
        _router_kernel,
        grid=(T // tm,),
        in_specs=[
            pl.BlockSpec((tm, D_MODEL), lambda i: (i, 0)),
            pl.BlockSpec((1, D_MODEL), lambda i: (0, 0)),
            _mod_spec(tm, 3),
            _mod_spec(tm, 4),
            pl.BlockSpec((D_MODEL, LANE), lambda i: (0, 0)),
            pl.BlockSpec((N_EXPERTS, 1), lambda i: (0, 0)),
        ],
        out_specs=[
            pl.BlockSpec((tm, XH_W), lambda i: (i, 0)),
            pl.BlockSpec((8, tm), lambda i: (0, i)),
            pl.BlockSpec((ROUTE_ROWS, LANE), lambda i: (0, 0)),
        ],
        out_shape=[
            jax.ShapeDtypeStruct((T, XH_W), jnp.uint32),
            jax.ShapeDtypeStruct((8, T), F32),
            jax.ShapeDtypeStruct((ROUTE_ROWS, LANE), F32),
        ],
        scratch_shapes=[pltpu.VMEM((ROUTE_ROWS, 1), F32)],
        compiler_params=_cp(("arbitrary",)),
        name="moe_router",
    )(x, g, mod, mod, wr, b_router.reshape(N_EXPERTS, 1))


DISPATCH_TILE = 256


DMA_UNROLL = 32


def _invert_kernel(dest_ref, src_ref):
    def clear(s, c):
        src_ref[s] = 0
        return c

    def put(t, c):
        src_ref[dest_ref[t]] = t
        return c

    lax.fori_loop(0, T_PAD, clear, 0, unroll=DMA_UNROLL)
    lax.fori_loop(0, T, put, 0, unroll=DMA_UNROLL)


def _invert(dest):
    return pl.pallas_call(
        _invert_kernel,
        in_specs=[pl.BlockSpec(memory_space=pltpu.SMEM)],
        out_specs=pl.BlockSpec(memory_space=pltpu.SMEM),
        out_shape=jax.ShapeDtypeStruct((T_PAD,), jnp.int32),
        name="moe_invert",
    )(dest)


def _gather_rows(idx_ref, base, src_hbm, buf, sem, tm, static=False):
    def start(r, c):
        pltpu.make_async_copy(src_hbm.at[pl.ds(idx_ref[base + r], 1)], buf.at[pl.ds(r, 1)], sem).start()
        return c

    if static:
        for r in range(tm):
            start(r, 0)
    else:
        lax.fori_loop(0, tm, start, 0, unroll=DMA_UNROLL)


def _wait_rows(src_hbm, buf, sem, tm):
    pltpu.make_async_copy(src_hbm.at[pl.ds(0, tm)], buf, sem).wait()


def _expert_kernel(ea_ref, eb_ref, nv_ref, src_ref, xh_hbm, ga_ref, ua_ref, da_ref, gb_ref, ub_ref, db_ref, y_ref,
                   *scratch):
    tm = MOE_TILE
    j = pl.program_id(0)
    nv = nv_ref[j]
    *bufs, sems = scratch
    n_buf = len(bufs)

    @pl.when(j == 0)
    def _():
        for t in range(MOE_AHEAD):
            _gather_rows(src_ref, t * tm, xh_hbm, bufs[t], sems.at[t], tm)

    def run(p):
        cur, cur_sem = bufs[p], sems.at[p]
        q = (p + MOE_AHEAD) % n_buf
        ahead, ahead_sem = bufs[q], sems.at[q]

        @pl.when(jnp.logical_or(j < MOE_AHEAD, nv_ref[jnp.maximum(j - MOE_AHEAD, 0)] > 0))
        def _():
            _wait_rows(xh_hbm, cur, cur_sem, tm)

        @pl.when(nv > 0)
        def _():
            valid = lax.broadcasted_iota(jnp.int32, (tm, 1), 0) < nv
            x = jnp.where(valid, _unpack_bf16_pairs(cur[:, :HALF_D]), 0.0).astype(BF16)
            wts = jnp.where(valid, lax.bitcast_convert_type(cur[:, HALF_D:], F32), 0.0)
            for r in range(tm):
                pltpu.make_async_copy(xh_hbm.at[pl.ds(src_ref[(j + MOE_AHEAD) * tm + r], 1)],
                                      ahead.at[pl.ds(r, 1)], ahead_sem).start()

            def ffn(g_ref, u_ref, d_ref, w):
                a = _dot(x, g_ref[...])
                h = a * _sigmoid(a) * _dot(x, u_ref[...]) * w
                return _dot(h.astype(BF16), d_ref[...])

            y = ffn(ga_ref, ua_ref, da_ref, wts[:, 0:1]) + ffn(gb_ref, ub_ref, db_ref, wts[:, 1:2])
            y_ref[...] = _pack_bf16_pairs(y)

    for p in range(n_buf):
        pl.when(j % n_buf == p)(functools.partial(run, p))

    @pl.when(nv == 0)
    def _():
        y_ref[...] = jnp.zeros_like(y_ref)


def _experts(layer, tile_ea, tile_eb, tile_nv, src, xh, w_gate, w_up, w_down):
    tm = MOE_TILE
    up = lambda sel: pl.BlockSpec((None, None, D_MODEL, D_EXPERT),
                                  lambda j, ea, eb, nv, sr: (layer, (ea, eb)[sel][j], 0, 0))
    down = lambda sel: pl.BlockSpec((None, None, D_EXPERT, D_MODEL),
                                    lambda j, ea, eb, nv, sr: (layer, (ea, eb)[sel][j], 0, 0))
    return pl.pallas_call(
        _expert_kernel,
        grid_spec=pltpu.PrefetchScalarGridSpec(
            num_scalar_prefetch=4,
            grid=(MOE_TILES,),
            in_specs=[pl.BlockSpec(memory_space=pl.ANY), up(0), up(0), down(0), up(1), up(1), down(1)],
            out_specs=pl.BlockSpec((tm, HALF_D), lambda j, ea, eb, nv, sr: (j, 0)),
            scratch_shapes=[pltpu.VMEM((tm, XH_W), jnp.uint32)] * (MOE_AHEAD + 1)
            + [pltpu.SemaphoreType.DMA((MOE_AHEAD + 1,))],
        ),
        out_shape=jax.ShapeDtypeStruct((T_PAD, HALF_D), jnp.uint32),
        compiler_params=_cp(("arbitrary",), 56),
        name="moe_experts",
    )(tile_ea, tile_eb, tile_nv, src, xh, w_gate, w_up, w_down, w_gate, w_up, w_down)


def _combine_kernel(*refs, final):
    tm = DISPATCH_TILE
    if final:
        dest_ref, x_ref, gt_ref, ys_hbm, fg_ref, oc_ref, ol_ref, buf, sems = refs
    else:
        dest_ref, x_ref, gt_ref, ys_hbm, o_ref, buf, sems = refs
    i = pl.program_id(0)
    slot = i % 2

    last = pl.num_programs(0) - 1

    @pl.when(i == 0)
    def _():
        _gather_rows(dest_ref, 0, ys_hbm, buf.at[0], sems.at[0], tm)

    def finish():
        _wait_rows(ys_hbm, buf.at[slot], sems.at[slot], tm)
        x = x_ref[...] + gt_ref[...] * _unpack_bf16_pairs(buf[slot])
        if not final:
            o_ref[...] = x
            return
        y = x * lax.rsqrt(jnp.mean(x * x, axis=-1, keepdims=True) + NORM_EPS) * fg_ref[...]
        is_ctx = i < TP // tm

        @pl.when(is_ctx)
        def _():
            oc_ref[...] = y

        @pl.when(jnp.logical_not(is_ctx))
        def _():
            ol_ref[...] = y

    @pl.when(i < last)
    def _():
        _gather_rows(dest_ref, (i + 1) * tm, ys_hbm, buf.at[1 - slot], sems.at[1 - slot], tm, static=True)
        finish()

    pl.when(i == last)(finish)


def _combine(dest, x, mod, ys, final_gain=None):
    tm = DISPATCH_TILE
    final = final_gain is not None
    n_ctx = TP // tm
    in_specs = [pl.BlockSpec((tm, D_MODEL), lambda i, d: (i, 0)),
                pl.BlockSpec((None, None, 1, D_MODEL), lambda i, d: (_cond_row(i * tm), 5, 0, 0)),
                pl.BlockSpec(memory_space=pl.ANY)]
    args = [dest, x, mod, ys]
    if final:
        in_specs.append(pl.BlockSpec((1, D_MODEL), lambda i, d: (0, 0)))
        args.append(final_gain.reshape(1, D_MODEL))
        out_specs = [pl.BlockSpec((tm, D_MODEL), lambda i, d: (jnp.minimum(i, n_ctx - 1), 0)),
                     pl.BlockSpec((tm, D_MODEL), lambda i, d: (jnp.maximum(i - n_ctx, 0), 0))]
        out_shape = [jax.ShapeDtypeStruct((TP, D_MODEL), F32), jax.ShapeDtypeStruct((TS, D_MODEL), F32)]
    else:
        out_specs = pl.BlockSpec((tm, D_MODEL), lambda i, d: (i, 0))
        out_shape = jax.ShapeDtypeStruct((T, D_MODEL), F32)
    return pl.pallas_call(
        functools.partial(_combine_kernel, final=final),
        grid_spec=pltpu.PrefetchScalarGridSpec(
            num_scalar_prefetch=1,
            grid=(T // tm,),
            in_specs=in_specs,
            out_specs=out_specs,
            scratch_shapes=[pltpu.VMEM((2, tm, HALF_D), jnp.uint32), pltpu.SemaphoreType.DMA((2,))],
        ),
        out_shape=out_shape,
        compiler_params=_cp(("arbitrary",)),
        name="moe_combine",
    )(*args)


def _lookup(table, idx):
    n = table.shape[0]
    hit = idx[:, None] == jnp.arange(n, dtype=jnp.int32)[None, :]
    return jnp.sum(jnp.where(hit, table[None, :], 0), axis=1)


def _moe_plan(rt, cnt):
    bucket = rt[0].astype(jnp.int32)
    rank = rt[1].astype(jnp.int32)
    counts = cnt[:N_BUCKETS, 0].astype(jnp.int32)
    tiles = (counts + MOE_TILE - 1) // MOE_TILE
    order = jnp.arange(N_BUCKETS, dtype=jnp.int32)
    tile_start = jnp.sum(jnp.where(order[None, :] < order[:, None], tiles[None, :], 0), axis=1)
    tile_end = tile_start + tiles
    n_used = tile_end[N_BUCKETS - 1]
    dest = _lookup(tile_start * MOE_TILE, bucket) + rank
    j = jnp.arange(MOE_TILES, dtype=jnp.int32)
    jc = jnp.minimum(j, n_used - 1)
    b = jnp.minimum(jnp.sum((jc[:, None] >= tile_end[None, :]).astype(jnp.int32), axis=1), N_BUCKETS - 1)
    nv = jnp.clip(_lookup(counts, b) - (j - _lookup(tile_start, b)) * MOE_TILE, 0, MOE_TILE)
    nv = jnp.where(j < n_used, nv, 0)
    n_pairs = len(PAIR_LO)
    ea = (b // n_pairs) * EXPERTS_PER_GROUP + _lookup(jnp.asarray(PAIR_LO, jnp.int32), b % n_pairs)
    eb = (b // n_pairs) * EXPERTS_PER_GROUP + _lookup(jnp.asarray(PAIR_HI, jnp.int32), b % n_pairs)
    return dest, ea, eb, nv


def _moe(layer, x, g, mod, w_router, b_router, w_gate, w_up, w_down, final_gain=None, defer=False):
    xh, rt, cnt = _router(x, g, mod, w_router, b_router)
    dest, ea, eb, nv = _moe_plan(rt, cnt)
    ys = _experts(layer, ea, eb, nv, _invert(dest), xh, w_gate, w_up, w_down)
    if defer:
        return x, dest, ys, mod
    return _combine(dest, x, mod, ys, final_gain)


def kernel(x_prompt, x_sample, cache_k_full, cache_v_full, cache_k_win, cache_v_win, c, c_ctx, w_mod, b_mod, norm_mix, norm_ffn, final_norm, pool_w, pool_scale, hy_w_in, hy_b_in, hy_conv_w, hy_conv_b, hy_f_w1, hy_f_b1, hy_f_w2, hy_f_b2, hy_f_freq, hy_f_w3, hy_decay, hy_skip, hy_w_out, hy_b_out, fa_w_qkv, fa_q_norm, fa_k_norm, fa_w_o, wa_w_qkv, wa_sink, wa_w_o, w_router, b_router, moe_w_gate, moe_w_up, moe_w_down):
    x = None
    cond =jnp.concatenate([c_ctx[None, :], c, jnp.zeros((N_COND - 1 - DEC_BATCH, D_MODEL), F32)], axis=0)
    mods = _adaln(cond, w_mod, b_mod).reshape(DEPTH, N_COND, 6, 1, D_MODEL)
    rope = _rope_tables()
    ones_hd = jnp.ones((HEAD_DIM,), F32)
    wg_bf, wu_bf, wd_bf = moe_w_gate.astype(BF16), moe_w_up.astype(BF16), moe_w_down.astype(BF16)
    new_kv = {}
    for layer in range(DEPTH):
        kind = layer % 4
        j = layer // 4
        mod = mods[layer]
        g_mix = norm_mix[layer].reshape(1, D_MODEL)
        if kind == 0:
            assert layer == 0, "the pooling mixer reads the two input streams, so it must be the first layer"
            x = _pool_mixer(x_prompt.reshape(TP, D_MODEL), x_sample.reshape(TS, D_MODEL), g_mix, mod,
                            pool_w[j], pool_scale[j])
        elif kind == 1:
            x = _hyena_mixer(x, g_mix, mod, hy_w_in[j], hy_b_in[j], hy_conv_w[j], hy_conv_b[j], hy_f_w1[j],
                             hy_f_b1[j], hy_f_w2[j], hy_f_b2[j], hy_f_freq[j], hy_f_w3[j], hy_decay[j],
                             hy_skip[j], hy_w_out[j], hy_b_out[j])
        elif kind == 2:
            x, nk, nv = _attn_mixer(x, g_mix, mod, fa_w_qkv[j], fa_q_norm[j], fa_k_norm[j], True, None,
                                    fa_w_o[j], cache_k_full[:, j], cache_v_full[:, j], False, rope)
            new_kv.setdefault("kf", []).append(nk)
            new_kv.setdefault("vf", []).append(nv)
        else:
            x, nk, nv = _attn_mixer(x, g_mix, mod, wa_w_qkv[j], ones_hd, ones_hd, False, wa_sink[j],
                                    wa_w_o[j], cache_k_win[:, j], cache_v_win[:, j], True, rope)
            new_kv.setdefault("kw", []).append(nk)
            new_kv.setdefault("vw", []).append(nv)
        defer = layer + 1 < DEPTH and (layer + 1) % 4 in (1, 2, 3)
        x = _moe(layer, x, norm_ffn[layer].reshape(1, D_MODEL), mod, w_router, b_router, wg_bf, wu_bf, wd_bf,
                 final_norm if layer == DEPTH - 1 else None, defer)
    y_prompt, y_sample = x
    y_prompt = y_prompt.reshape(BATCH, SEQ, D_MODEL)
    y_sample = y_sample.reshape(DEC_BATCH, DEC_SEQ, D_MODEL)
    return (y_prompt, y_sample, jnp.stack(new_kv["kf"], axis=1), jnp.stack(new_kv["vf"], axis=1),
            jnp.stack(new_kv["kw"], axis=1), jnp.stack(new_kv["vw"], axis=1))
```

```python
import functools
import math

import jax
import jax.numpy as jnp
import numpy as np
from jax import lax
from jax.experimental import pallas as pl
from jax.experimental.pallas import tpu as pltpu

D_MODEL = 2048
BATCH = 32
SEQ = 256
DEPTH = 4
DEC_BATCH = 4
DEC_SEQ = 4096
PAST_LEN = 512
GRID_W = 64
N_HEADS = 16
N_KV_HEADS = 4
HEAD_DIM = D_MODEL // N_HEADS
KV_GROUP = N_HEADS // N_KV_HEADS
KV_DIM = N_KV_HEADS * HEAD_DIM
QKV_DIM = (N_HEADS + 2 * N_KV_HEADS) * HEAD_DIM
ROPE_THETA = 10000.0
WINDOW = 128
POOL_WINDOWS = (2, 4, 8, 16)
POOL_GROUP = D_MODEL // len(POOL_WINDOWS)
HYENA_EMB_BANDS = 16
HYENA_FILTER_HIDDEN = 64
N_EXPERTS = 16
N_EXPERT_GROUPS = 4
EXPERTS_PER_GROUP = 4
D_EXPERT = 512
NORM_EPS = 1e-6
NEG_INF = -1e30

F32 = jnp.float32
BF16 = jnp.bfloat16

TP = BATCH * SEQ
TS = DEC_BATCH * DEC_SEQ
T = TP + TS
N_COND = 8
LANE = 128
MIB = 1024 * 1024

PAIR_LO = (0, 0, 0, 1, 1, 2)
PAIR_HI = (1, 2, 3, 2, 3, 3)
N_BUCKETS = N_EXPERT_GROUPS * len(PAIR_LO)
MOE_TILE = 256
MOE_AHEAD = 3
MOE_TILES = T // MOE_TILE + N_BUCKETS + MOE_AHEAD
T_PAD = MOE_TILES * MOE_TILE
HALF_D = D_MODEL // 2
XH_W = HALF_D + LANE


def _cp(sem, vmem_mb=48):
    return pltpu.CompilerParams(dimension_semantics=sem, vmem_limit_bytes=vmem_mb * MIB)


def _dot(a, b):
    return jnp.dot(a, b, preferred_element_type=F32)


def _dot3(a, b):
    ah = a.astype(BF16)
    al = (a - ah.astype(F32)).astype(BF16)
    bh = b.astype(BF16)
    bl = (b - bh.astype(F32)).astype(BF16)
    return _dot(ah, bh) + (_dot(al, bh) + _dot(ah, bl))


def _sigmoid(x):
    return 1.0 / (1.0 + jnp.exp(-x))


def _pack_bf16_pairs(x):
    n = x.shape[1] // 2
    bits = lambda v: lax.bitcast_convert_type(v.astype(BF16).astype(F32), jnp.uint32)
    return (bits(x[:, :n]) >> 16) | bits(x[:, n:])


def _unpack_bf16_pairs(u):
    lo = lax.bitcast_convert_type(u << 16, F32)
    hi = lax.bitcast_convert_type(u & jnp.uint32(0xFFFF0000), F32)
    return jnp.concatenate([lo, hi], axis=1)


def _cond_row(r):
    return jnp.where(r < TP, 0, 1 + (r - TP) // DEC_SEQ)


def _mod_spec(tm, chunk, tn=D_MODEL, ncol=False):
    if ncol:
        return pl.BlockSpec((None, None, 1, tn), lambda i, j: (_cond_row(i * tm), chunk, 0, j))
    return pl.BlockSpec((None, None, 1, tn), lambda i, *_: (_cond_row(i * tm), chunk, 0, 0))


def _norm_mod(x, g, shift, scale):
    var = jnp.mean(x * x, axis=-1, keepdims=True)
    y = x * lax.rsqrt(var + NORM_EPS) * g
    return y * (1.0 + scale) + shift


def _adaln_kernel(c_ref, w_ref, b_ref, o_ref):
    c = c_ref[...]
    a = c * _sigmoid(c)
    o_ref[...] = _dot3(a, w_ref[...]) + b_ref[...]


def _adaln(cond, w_mod, b_mod):
    tn = 1024
    n = 6 * D_MODEL
    return pl.pallas_call(
        _adaln_kernel,
        grid=(DEPTH, n // tn),
        in_specs=[
            pl.BlockSpec((N_COND, D_MODEL), lambda l, j: (0, 0)),
            pl.BlockSpec((None, D_MODEL, tn), lambda l, j: (l, 0, j)),
            pl.BlockSpec((None, 1, tn), lambda l, j: (l, 0, j)),
        ],
        out_specs=pl.BlockSpec((None, N_COND, tn), lambda l, j: (l, 0, j)),
        out_shape=jax.ShapeDtypeStruct((DEPTH, N_COND, n), F32),
        compiler_params=_cp(("parallel", "parallel")),
        name="adaln",
    )(cond, w_mod, b_mod.reshape(DEPTH, 1, n))


NM_TM = 512
NM_AHEAD = 2


def _nm_matmul_kernel(dest_ref, x_ref, gp_ref, ys_hbm, g_ref, sh_ref, sc_ref, w_ref, b_ref, o_ref, xn_ref,
                      h_scr, ybuf, sems, *, n_cols, tn):
    tm = NM_TM
    i, j = pl.program_id(0), pl.program_id(1)
    n_tiles = pl.num_programs(0)
    n_buf = NM_AHEAD + 1

    @pl.when(j == 0)
    def _():
        @pl.when(i == 0)
        def _():
            for t in range(NM_AHEAD):
                _gather_rows(dest_ref, t * tm, ys_hbm, ybuf.at[t], sems.at[t], tm)

        slot = i % n_buf
        _wait_rows(ys_hbm, ybuf.at[slot], sems.at[slot], tm)
        x = x_ref[...] + gp_ref[...] * _unpack_bf16_pairs(ybuf[slot])
        xn_ref[...] = x
        h_scr[...] = _norm_mod(x, g_ref[...], sh_ref[...], sc_ref[...]).astype(BF16)

    def project(c):
        o_ref[...] = (_dot(h_scr[...], w_ref[:, c * tn:(c + 1) * tn]) + b_ref[...]).astype(o_ref.dtype)

    for c in range(n_cols - 1):
        pl.when(j == c)(functools.partial(project, c))

    @pl.when(j == n_cols - 1)
    def _():
        ahead = (i + NM_AHEAD) % n_buf
        _gather_rows(dest_ref, jnp.minimum(i + NM_AHEAD, n_tiles - 1) * tm, ys_hbm, ybuf.at[ahead], sems.at[ahead],
                     tm, static=True)
        project(n_cols - 1)

    @pl.when(jnp.logical_and(i == n_tiles - 1, j == n_cols - 1))
    def _():
        for t in range(1, NM_AHEAD + 1):
            late = (i + t) % n_buf
            _wait_rows(ys_hbm, ybuf.at[late], sems.at[late], tm)


def _nm_matmul(x_prev, dest, ys, mod_prev, g, mod, w, b, out_dtype, name):
    tm, tn = NM_TM, 1024
    n = w.shape[1]
    row = lambda: pl.BlockSpec((tm, D_MODEL), lambda i, j, d: (i, 0))
    return pl.pallas_call(
        functools.partial(_nm_matmul_kernel, n_cols=n // tn, tn=tn),
        grid_spec=pltpu.PrefetchScalarGridSpec(
            num_scalar_prefetch=1,
            grid=(T // tm, n // tn),
            in_specs=[
                row(),
                pl.BlockSpec((None, None, 1, D_MODEL), lambda i, j, d: (_cond_row(i * tm), 5, 0, 0)),
                pl.BlockSpec(memory_space=pl.ANY),
                pl.BlockSpec((1, D_MODEL), lambda i, j, d: (0, 0)),
                _mod_spec(tm, 0),
                _mod_spec(tm, 1),
                pl.BlockSpec((D_MODEL, n), lambda i, j, d: (0, 0), pipeline_mode=pl.Buffered(1)),
                pl.BlockSpec((1, tn), lambda i, j, d: (0, j)),
            ],
            out_specs=[pl.BlockSpec((tm, tn), lambda i, j, d: (i, j)), row()],
            scratch_shapes=[pltpu.VMEM((tm, D_MODEL), BF16), pltpu.VMEM((NM_AHEAD + 1, tm, HALF_D), jnp.uint32),
                            pltpu.SemaphoreType.DMA((NM_AHEAD + 1,))],
        ),
        out_shape=[jax.ShapeDtypeStruct((T, n), out_dtype), jax.ShapeDtypeStruct((T, D_MODEL), F32)],
        compiler_params=_cp(("arbitrary", "arbitrary"), 56),
        name=name,
    )(dest, x_prev, mod_prev, ys, g, mod, mod, w, b)


RESID_TM = 1024


def _resid_matmul_kernel(ap_ref, as_ref, w_ref, b_ref, x_ref, gt_ref, o_ref):
    def emit(a_ref):
        o_ref[...] = x_ref[...] + gt_ref[...] * (_dot(a_ref[...], w_ref[...]) + b_ref[...])

    is_ctx = pl.program_id(0) < TP // RESID_TM
    pl.when(is_ctx)(lambda: emit(ap_ref))
    pl.when(jnp.logical_not(is_ctx))(lambda: emit(as_ref))


def _resid_matmul(a_ctx, a_lat, w, b, x, mod, name):
    tm, tn = RESID_TM, 1024
    k = a_ctx.shape[1]
    n_ctx = TP // tm
    return pl.pallas_call(
        _resid_matmul_kernel,
        grid=(T // tm, D_MODEL // tn),
        in_specs=[
            pl.BlockSpec((tm, k), lambda i, j: (jnp.minimum(i, n_ctx - 1), 0)),
            pl.BlockSpec((tm, k), lambda i, j: (jnp.maximum(i - n_ctx, 0), 0)),
            pl.BlockSpec((k, tn), lambda i, j: (0, j)),
            pl.BlockSpec((1, tn), lambda i, j: (0, j)),
            pl.BlockSpec((tm, tn), lambda i, j: (i, j)),
            _mod_spec(tm, 2, tn, ncol=True),
        ],
        out_specs=pl.BlockSpec((tm, tn), lambda i, j: (i, j)),
        out_shape=jax.ShapeDtypeStruct((T, D_MODEL), F32),
        compiler_params=_cp(("parallel", "parallel")),
        name=name,
    )(a_ctx, a_lat, w, b, x, mod)


POOL_TILE = 256
POOL_HALO = 8


def _seq_pos(r0):
    is_ctx = r0 < TP
    loc0 = jnp.where(is_ctx, r0 % SEQ, (r0 - TP) % DEC_SEQ)
    seq_len = jnp.where(is_ctx, SEQ, DEC_SEQ)
    return loc0, seq_len


def _pool_kernel(xc_ref, xcp_ref, xcn_ref, xl_ref, xlp_ref, xln_ref, *rest):
    is_ctx = pl.program_id(0) < TP // POOL_TILE
    pl.when(is_ctx)(lambda: _pool_tile(xc_ref, xcp_ref, xcn_ref, *rest))
    pl.when(jnp.logical_not(is_ctx))(lambda: _pool_tile(xl_ref, xlp_ref, xln_ref, *rest))


def _pool_tile(x_ref, xp_ref, xn_ref, g_ref, sh_ref, sc_ref, gt_ref, pw_ref, ps_ref, o_ref, hz_scr):
    tm, hl = POOL_TILE, POOL_HALO
    loc0, seq_len = _seq_pos(pl.program_id(0) * tm)
    has_prev = loc0 > 0
    has_next = loc0 + tm < seq_len
    g, sh, sc = g_ref[...], sh_ref[...], sc_ref[...]
    x = x_ref[...]
    h = _norm_mod(x, g, sh, sc)
    hz_scr[0:hl, :] = jnp.where(has_prev, _norm_mod(xp_ref[...], g, sh, sc), 0.0)
    hz_scr[hl:hl + tm, :] = h
    hz_scr[hl + tm:, :] = jnp.where(has_next, _norm_mod(xn_ref[...], g, sh, sc), 0.0)
    tl = loc0 + lax.broadcasted_iota(jnp.int32, (tm, 1), 0)
    outs = []
    for gi, w in enumerate(POOL_WINDOWS):
        cs = slice(gi * POOL_GROUP, (gi + 1) * POOL_GROUP)
        s = jnp.zeros((tm, POOL_GROUP), F32)
        for off in range(-(w // 2), w - w // 2):
            s = s + hz_scr[hl + off:hl + off + tm, cs]
        lo = jnp.maximum(tl - w // 2, 0)
        hi = jnp.minimum(tl + (w - w // 2), seq_len)
        d = s / (hi - lo).astype(F32) - h[:, cs]
        outs.append(_dot(d.astype(BF16), pw_ref[gi]))
    out = jnp.concatenate(outs, axis=1) * ps_ref[...]
    o_ref[...] = x + gt_ref[...] * out


def _pool_mixer(x_ctx, x_lat, g, mod, pool_w, pool_scale):
    tm, hl = POOL_TILE, POOL_HALO
    r = tm // hl

    def stream(first_tile, rows):
        tile = lambda i: jnp.clip(i - first_tile, 0, rows // tm - 1)
        return [pl.BlockSpec((tm, D_MODEL), lambda i: (tile(i), 0)),
                pl.BlockSpec((hl, D_MODEL), lambda i: (jnp.maximum(tile(i) * r - 1, 0), 0)),
                pl.BlockSpec((hl, D_MODEL), lambda i: (jnp.minimum((tile(i) + 1) * r, rows // hl - 1), 0))]

    return pl.pallas_call(
        _pool_kernel,
        grid=(T // tm,),
        in_specs=stream(0, TP) + stream(TP // tm, TS) + [
            pl.BlockSpec((1, D_MODEL), lambda i: (0, 0)),
            _mod_spec(tm, 0),
            _mod_spec(tm, 1),
            _mod_spec(tm, 2),
            pl.BlockSpec((len(POOL_WINDOWS), POOL_GROUP, POOL_GROUP), lambda i: (0, 0, 0)),
            pl.BlockSpec((1, D_MODEL), lambda i: (0, 0)),
        ],
        out_specs=pl.BlockSpec((tm, D_MODEL), lambda i: (i, 0)),
        out_shape=jax.ShapeDtypeStruct((T, D_MODEL), F32),
        scratch_shapes=[pltpu.VMEM((tm + 2 * hl, D_MODEL), F32)],
        compiler_params=_cp(("parallel",)),
        name="pool_mixer",
    )(x_ctx, x_ctx, x_ctx, x_lat, x_lat, x_lat, g, mod, mod, mod, pool_w.astype(BF16),
      pool_scale.reshape(1, D_MODEL))


CONV_TILE = 256
CONV_HALO = 16


def _conv3_kernel(u_ref, up_ref, un_ref, cw_ref, cb_ref, o_ref, scr):
    tm, hl = CONV_TILE, CONV_HALO
    loc0, seq_len = _seq_pos(pl.program_id(0) * tm)
    has_prev = loc0 > 0
    has_next = loc0 + tm < seq_len
    scr[0:hl, :] = jnp.where(has_prev, up_ref[...].astype(F32), 0.0)
    scr[hl:hl + tm, :] = u_ref[...].astype(F32)
    scr[hl + tm:, :] = jnp.where(has_next, un_ref[...].astype(F32), 0.0)
    out = (scr[hl - 1:hl - 1 + tm, :] * cw_ref[0:1, :] + scr[hl:hl + tm, :] * cw_ref[1:2, :]
           + scr[hl + 1:hl + 1 + tm, :] * cw_ref[2:3, :] + cb_ref[...])
    o_ref[...] = out.astype(o_ref.dtype)


def _conv3(u0, conv_w, conv_b):
    tm, hl, tc = CONV_TILE, CONV_HALO, D_MODEL
    r = tm // hl
    n = u0.shape[1]
    return pl.pallas_call(
        _conv3_kernel,
        grid=(T // tm, n // tc),
        in_specs=[
            pl.BlockSpec((tm, tc), lambda i, j: (i, j)),
            pl.BlockSpec((hl, tc), lambda i, j: (jnp.maximum(i * r - 1, 0), j)),
            pl.BlockSpec((hl, tc), lambda i, j: (jnp.minimum((i + 1) * r, T // hl - 1), j)),
            pl.BlockSpec((3, tc), lambda i, j: (0, j)),
            pl.BlockSpec((1, tc), lambda i, j: (0, j)),
        ],
        out_specs=pl.BlockSpec((tm, tc), lambda i, j: (i, j)),
        out_shape=jax.ShapeDtypeStruct((T, n), BF16),
        scratch_shapes=[pltpu.VMEM((tm + 2 * hl, tc), F32)],
        compiler_params=_cp(("parallel", "parallel")),
        name="hyena_conv3",
    )(u0, u0, u0, conv_w, conv_b.reshape(1, n))


FILT_TILE = 256


HYENA_BLOCK = 1024


T_LANE = LANE - 1


def _filter_mlp_kernel(emb_ref, w1_ref, b1_ref, w2_ref, b2_ref, fr_ref, o_ref):
    emb = emb_ref[...]
    fr = fr_ref[...]
    a = jnp.sin(fr * (_dot3(emb, w1_ref[...]) + b1_ref[...]))
    a = jnp.sin(fr * (_dot3(a, w2_ref[...]) + b2_ref[...]))
    lane = lax.broadcasted_iota(jnp.int32, a.shape, 1)
    o_ref[...] = jnp.where(lane == T_LANE, emb[:, 0:1], a)


def _filter_mlp(pos, L, f_w1, f_b1, f_w2, f_b2, f_freq):
    assert HYENA_FILTER_HIDDEN <= T_LANE
    tl = FILT_TILE
    rows = pos.shape[0]
    small = lambda: pl.BlockSpec((LANE, LANE), lambda i: (0, 0))
    vec = lambda: pl.BlockSpec((1, LANE), lambda i: (0, 0))
    return pl.pallas_call(
        _filter_mlp_kernel,
        grid=(rows // tl,),
        in_specs=[pl.BlockSpec((tl, LANE), lambda i: (i, 0)), small(), vec(), small(), vec(), vec()],
        out_specs=pl.BlockSpec((tl, LANE), lambda i: (i, 0)),
        out_shape=jax.ShapeDtypeStruct((rows, LANE), F32),
        compiler_params=_cp(("parallel",)),
        name="hyena_filter_mlp",
    )(_filter_embedding(pos, L), _pad2(f_w1, LANE, LANE), _pad2(f_b1[None], 1, LANE),
      _pad2(f_w2, LANE, LANE), _pad2(f_b2[None], 1, LANE), _pad2(f_freq[None], 1, LANE))


def _filter_kernel(h1_ref, h2_ref, w3a_ref, dca_ref, w3b_ref, dcb_ref, fa_ref, fb_ref, *, blk):
    def taps(h, w3_ref, dc_ref):
        return _dot3(h, w3_ref[...]) * jnp.exp(-h[:, T_LANE:] * jnp.abs(dc_ref[...]))

    pos = taps(h1_ref[...], w3a_ref, dca_ref)
    neg = taps(h2_ref[...], w3b_ref, dcb_ref)
    m = (pl.program_id(0) * FILT_TILE + lax.broadcasted_iota(jnp.int32, (FILT_TILE, 1), 0)) % blk
    fa_ref[...] = jnp.where(m == 0, pos, pos + neg).astype(BF16)
    fb_ref[...] = jnp.where(m == 0, 0.0, neg - pos).astype(BF16)


def _pad2(a, rows, cols):
    return jnp.pad(a, ((0, rows - a.shape[0]), (0, cols - a.shape[1])))


def _filter_positions(L, blk):
    n_blk = L // blk
    m = np.arange(blk)
    p1, p2 = [], []
    for d in range(-(n_blk - 1), n_blk):
        if d >= 1:
            p1.append(d * blk + m), p2.append(d * blk - m)
        elif d == 0:
            p1.append(m), p2.append(m)
        else:
            p1.append(-d * blk - m), p2.append(-d * blk + m)
    return np.concatenate(p1), np.concatenate(p2)


def _filter_embedding(pos, L):
    t = jnp.asarray(pos, F32) / L
    bands = jnp.linspace(1e-4, HYENA_EMB_BANDS - 1, HYENA_EMB_BANDS, dtype=F32)
    ang = (2 * math.pi) * t[:, None] * bands[None, :]
    return _pad2(jnp.concatenate([t[:, None], jnp.cos(ang), -jnp.sin(ang)], axis=-1), pos.shape[0], LANE)


def _hyena_filters(L, blk, f_w1, f_b1, f_w2, f_b2, f_freq, f_w3, decay):
    n_blk = L // blk
    p1, p2 = _filter_positions(L, blk)
    rows = p1.shape[0]
    tl = FILT_TILE
    tiles_per_lag = blk // tl
    lag = lambda i: i // tiles_per_lag - (n_blk - 1)
    col1 = lambda i, n: 2 * n + jnp.where(lag(i) >= 0, 0, 1)
    col2 = lambda i, n: 2 * n + jnp.where(lag(i) >= 1, 0, 1)
    n_tiles = rows // tl
    out = pl.BlockSpec((None, tl, D_MODEL), lambda i, n: (n, i, 0))
    w3 = _pad2(f_w3, LANE, f_w3.shape[1])
    hidden = _filter_mlp(np.concatenate([p1, p2]), L, f_w1, f_b1, f_w2, f_b2, f_freq)
    return pl.pallas_call(
        functools.partial(_filter_kernel, blk=blk),
        grid=(n_tiles, 2),
        in_specs=[
            pl.BlockSpec((tl, LANE), lambda i, n: (i, 0)),
            pl.BlockSpec((tl, LANE), lambda i, n: (n_tiles + i, 0)),
            pl.BlockSpec((LANE, D_MODEL), lambda i, n: (0, col1(i, n))),
            pl.BlockSpec((1, D_MODEL), lambda i, n: (0, col1(i, n))),
            pl.BlockSpec((LANE, D_MODEL), lambda i, n: (0, col2(i, n))),
            pl.BlockSpec((1, D_MODEL), lambda i, n: (0, col2(i, n))),
        ],
        out_specs=[out, out],
        out_shape=[jax.ShapeDtypeStruct((2, rows, D_MODEL), BF16)] * 2,
        compiler_params=_cp(("parallel", "parallel")),
        name="hyena_filters",
    )(hidden, hidden, w3, decay[None], w3, decay[None])


def _dft_mats(L):
    r = int(math.isqrt(L))
    k2 = 2 * jnp.arange(L, dtype=jnp.int32)[:, None] + 1
    n1 = r * jnp.arange(L // r, dtype=jnp.int32)[None, :]
    n2 = jnp.arange(r, dtype=jnp.int32)[None, :]
    sc = math.pi / (2 * L)
    aa = ((k2 * n1) % (4 * L)).astype(F32) * sc
    ab = ((k2 * n2) % (4 * L)).astype(F32) * sc
    ca, sa, cb, sb = jnp.cos(aa)[:, :, None], jnp.sin(aa)[:, :, None], jnp.cos(ab)[:, None, :], jnp.sin(ab)[:, None, :]
    c = (ca * cb - sa * sb).reshape(L, L)
    s = (sa * cb + ca * sb).reshape(L, L)
    return c.astype(BF16), s.astype(BF16), c.T.astype(BF16), s.T.astype(BF16)


def _dft_tiles(L):
    return min(512, L), 512


def _dft_filter_kernel(c_ref, s_ref, a_ref, b_ref, gr_ref, gi_ref):
    gr_ref[...] = _dot(c_ref[...], a_ref[...]).astype(gr_ref.dtype)
    gi_ref[...] = _dot(s_ref[...], b_ref[...]).astype(gi_ref.dtype)


def _dft_filter(cm, sm, fa, fb, L):
    tf, tn = _dft_tiles(L)
    n = fa.shape[0]
    mat = lambda: pl.BlockSpec((tf, L), lambda k, c, s: (k, 0))
    rhs = lambda: pl.BlockSpec((None, L, tn), lambda k, c, s: (s, 0, c))
    out = pl.BlockSpec((None, tf, tn), lambda k, c, s: (s, k, c))
    return pl.pallas_call(
        _dft_filter_kernel,
        grid=(L // tf, D_MODEL // tn, n),
        in_specs=[mat(), mat(), rhs(), rhs()],
        out_specs=[out, out],
        out_shape=[jax.ShapeDtypeStruct((n, L, D_MODEL), BF16)] * 2,
        compiler_params=_cp(("parallel", "parallel", "parallel")),
        name="hyena_filter_dft",
    )(cm, sm, fa, fb)


FWD_TF = 256


def _dft_fwd_kernel(c_ref, s_ref, z_ref, gr_ref, gi_ref, yr_ref, yi_ref, *, n_blk, blk, bpb):
    c, s = c_ref[...], s_ref[...]
    for bb in range(bpb):
        zc, zs = [], []
        for j in range(n_blk):
            r = (bb * n_blk + j) * blk
            zj = z_ref[r:r + blk, :]
            zc.append(_dot(c, zj).astype(BF16))
            zs.append(_dot(s, zj).astype(BF16))
        for i in range(n_blk):
            yr = yi = None
            for j in range(n_blk):
                lag = i - j + n_blk - 1
                gr, gi = gr_ref[lag], gi_ref[lag]
                tr = gr * zc[j] + gi * zs[j]
                ti = gi * zc[j] - gr * zs[j]
                yr = tr if yr is None else yr + tr
                yi = ti if yi is None else yi + ti
            yr_ref[bb, i] = yr.astype(BF16)
            yi_ref[bb, i] = yi.astype(BF16)


def _seqs_per_step(L):
    return max(1, 2048 // L)


def _dft_fwd(cm, sm, z, z_rowblk, z_colblk, gr, gi, order, nb, L, blk):
    n_blk = L // blk
    bpb = _seqs_per_step(L)
    assert nb % bpb == 0 and z_rowblk % bpb == 0
    tf, tn = min(FWD_TF, blk), 512
    mat = lambda: pl.BlockSpec((tf, blk), lambda k, c, b: (k, 0))
    gsp = lambda: pl.BlockSpec((None, 2 * n_blk - 1, tf, tn), lambda k, c, b: (order, 0, k, c))
    out = pl.BlockSpec((bpb, n_blk, tf, tn), lambda k, c, b: (b, 0, k, c))
    return pl.pallas_call(
        functools.partial(_dft_fwd_kernel, n_blk=n_blk, blk=blk, bpb=bpb),
        grid=(blk // tf, D_MODEL // tn, nb // bpb),
        in_specs=[mat(), mat(),
                  pl.BlockSpec((bpb * L, tn), lambda k, c, b: (z_rowblk // bpb + b, z_colblk + c)),
                  gsp(), gsp()],
        out_specs=[out, out],
        out_shape=[jax.ShapeDtypeStruct((nb, n_blk, blk, D_MODEL), BF16)] * 2,
        compiler_params=_cp(("parallel", "parallel", "parallel")),
        name="hyena_dft_fwd",
    )(cm, sm, z, gr, gi)


def _dft_inv_kernel(ct_ref, st_ref, yr_ref, yi_ref, z_ref, gt_ref, sk_ref, o_ref, *, inv_len, bpb, tt):
    for bb in range(bpb):
        rows = slice(bb * tt, (bb + 1) * tt)
        y = (_dot(ct_ref[...], yr_ref[bb]) - _dot(st_ref[...], yi_ref[bb])) * inv_len
        o_ref[rows, :] = (gt_ref[rows, :].astype(F32) * (y + sk_ref[...] * z_ref[rows, :].astype(F32))).astype(BF16)


def _dft_inv(ctm, stm, yr, yi, z, z_rowblk, z_colblk, gate, g_rowblk, g_colblk, skip, nb, L):
    tt, tn = _dft_tiles(L)
    rpb = L // tt
    bpb = _seqs_per_step(L) if rpb == 1 else 1
    assert nb % bpb == 0 and z_rowblk % bpb == 0 and g_rowblk % bpb == 0
    mat = lambda: pl.BlockSpec((tt, L), lambda t, c, b: (t, 0))
    spec = lambda: pl.BlockSpec((bpb, L, tn), lambda t, c, b: (b, 0, c))
    rows = lambda blk0: (lambda t, c, b: ((blk0 + b * bpb * rpb + t) // bpb))
    return pl.pallas_call(
        functools.partial(_dft_inv_kernel, inv_len=1.0 / L, bpb=bpb, tt=tt),
        grid=(rpb, D_MODEL // tn, nb // bpb),
        in_specs=[mat(), mat(), spec(), spec(),
                  pl.BlockSpec((bpb * tt, tn), lambda t, c, b: (rows(z_rowblk)(t, c, b), z_colblk + c)),
                  pl.BlockSpec((bpb * tt, tn), lambda t, c, b: (rows(g_rowblk)(t, c, b), g_colblk + c)),
                  pl.BlockSpec((1, tn), lambda t, c, b: (0, c))],
        out_specs=pl.BlockSpec((bpb * tt, tn), lambda t, c, b: (rows(0)(t, c, b), c)),
        out_shape=jax.ShapeDtypeStruct((nb * L, D_MODEL), BF16),
        compiler_params=_cp(("parallel", "parallel", "parallel")),
        name="hyena_dft_inv",
    )(ctm, stm, yr, yi, z, gate, skip)


def _hyena_stream(u, row0, nb, L, fparams, skip):
    blk = min(HYENA_BLOCK, L)
    n_blk = L // blk
    n_lag = 2 * n_blk - 1
    cm, sm, ctm, stm = _dft_mats(blk)
    fa, fb = _hyena_filters(L, blk, *fparams)
    seg = lambda a: a.reshape(2 * n_lag, blk, D_MODEL)
    gr, gi = _dft_filter(cm, sm, seg(fa), seg(fb), blk)
    gr, gi = (a.reshape(2, n_lag, blk, D_MODEL) for a in (gr, gi))
    tt, tn = _dft_tiles(blk)
    ncb = D_MODEL // tn
    blocks = lambda a: a.reshape(nb * n_blk, blk, D_MODEL)

    yr, yi = _dft_fwd(cm, sm, u, row0 // L, 0, gr, gi, 0, nb, L, blk)
    z1 = _dft_inv(ctm, stm, blocks(yr), blocks(yi), u, row0 // tt, 0, u, row0 // tt, ncb, skip[0:1],
                  nb * n_blk, blk)
    yr, yi = _dft_fwd(cm, sm, z1, 0, 0, gr, gi, 1, nb, L, blk)
    return _dft_inv(ctm, stm, blocks(yr), blocks(yi), z1, 0, 0, u, row0 // tt, 2 * ncb, skip[1:2],
                    nb * n_blk, blk)


def _hyena_mixer(pending, g, mod, w_in, b_in, conv_w, conv_b, f_w1, f_b1, f_w2, f_b2, f_freq, f_w3, decay, skip,
                 w_out, b_out):
    u0, x = _nm_matmul(*pending, g, mod, w_in.astype(BF16), b_in.reshape(1, -1), BF16, "hyena_in_proj")
    u = _conv3(u0, conv_w, conv_b)
    fparams = (f_w1, f_b1, f_w2, f_b2, f_freq, f_w3, decay)
    zp = _hyena_stream(u, 0, BATCH, SEQ, fparams, skip)
    zs = _hyena_stream(u, TP, DEC_BATCH, DEC_SEQ, fparams, skip)
    return _resid_matmul(zp, zs, w_out.astype(BF16), b_out.reshape(1, -1), x, mod, "hyena_out_proj")


def _rope_tables():
    pos = jnp.arange(DEC_SEQ, dtype=jnp.int32)
    row = (pos // GRID_W).astype(F32)
    col = (pos % GRID_W).astype(F32)
    axis_dim = HEAD_DIM // 2
    inv_freq = ROPE_THETA ** (-jnp.arange(0, axis_dim, 2, dtype=F32) / axis_dim)
    ar = row[:, None] * inv_freq[None, :]
    ac = col[:, None] * inv_freq[None, :]
    cos = jnp.concatenate([jnp.cos(ar), jnp.cos(ar), jnp.cos(ac), jnp.cos(ac)], axis=-1)
    sin = jnp.concatenate([-jnp.sin(ar), jnp.sin(ar), -jnp.sin(ac), jnp.sin(ac)], axis=-1)
    return cos, sin


QKV_TM = 512
QKV_TN = 1024
PAIR = 2 * HEAD_DIM


QKV_AHEAD = 2


def _qkv_kernel(dest_ref, x_ref, gp_ref, ys_hbm, g_ref, sh_ref, sc_ref, w_ref, qn_ref, kn_ref, cos_ref, sin_ref,
                q_ref, k_ref, v_ref, nk_ref, nv_ref, xn_ref, h_scr, ybuf, sems, *, use_norm):
    tm = QKV_TM
    i, j = pl.program_id(0), pl.program_id(1)
    n_tiles = pl.num_programs(0)
    n_buf = QKV_AHEAD + 1

    @pl.when(j == 0)
    def _():
        @pl.when(i == 0)
        def _():
            for t in range(QKV_AHEAD):
                _gather_rows(dest_ref, t * tm, ys_hbm, ybuf.at[t], sems.at[t], tm)

        slot = i % n_buf
        _wait_rows(ys_hbm, ybuf.at[slot], sems.at[slot], tm)
        x = x_ref[...] + gp_ref[...] * _unpack_bf16_pairs(ybuf[slot])
        xn_ref[...] = x
        h_scr[...] = _norm_mod(x, g_ref[...], sh_ref[...], sc_ref[...]).astype(BF16)

    def gather_ahead():
        ahead = (i + QKV_AHEAD) % n_buf
        _gather_rows(dest_ref, jnp.minimum(i + QKV_AHEAD, n_tiles - 1) * tm, ys_hbm, ybuf.at[ahead],
                     sems.at[ahead], tm, static=True)

    quarter = HEAD_DIM // 4
    scale = HEAD_DIM ** -0.5 * LOG2E

    def head(xh, gn):
        if use_norm:
            xh = xh * lax.rsqrt(jnp.mean(xh * xh, axis=-1, keepdims=True) + NORM_EPS) * gn
        return xh

    def rope(xh):
        lane = lax.broadcasted_iota(jnp.int32, (tm, HEAD_DIM), 1)
        first = (lane % (2 * quarter)) < quarter
        partner = jnp.where(first, pltpu.roll(xh, HEAD_DIM - quarter, 1), pltpu.roll(xh, quarter, 1))
        return xh * cos_ref[...] + partner * sin_ref[...]

    def proj(c0):
        return _dot(h_scr[...], w_ref[:, c0:c0 + PAIR])

    def q_tile(latent, jq):
        for p in range(QKV_TN // PAIR):
            acc = proj(jq * QKV_TN + p * PAIR)
            for t in range(2):
                xh = head(acc[:, t * HEAD_DIM:(t + 1) * HEAD_DIM], qn_ref[...])
                xh = rope(xh) if latent else xh
                c0 = p * PAIR + t * HEAD_DIM
                q_ref[:, c0:c0 + HEAD_DIM] = (xh * scale).astype(BF16)

    def kv_tile(latent):
        gather_ahead()
        for p in range(KV_DIM // PAIR):
            acc = proj(D_MODEL + p * PAIR)
            for t in range(2):
                c0 = p * PAIR + t * HEAD_DIM
                kh = head(acc[:, t * HEAD_DIM:(t + 1) * HEAD_DIM], kn_ref[...])
                if not latent:
                    nk_ref[:, c0:c0 + HEAD_DIM] = kh
                k_ref[:, c0:c0 + HEAD_DIM] = (rope(kh) if latent else kh).astype(BF16)
        for p in range(KV_DIM // PAIR):
            acc = proj(D_MODEL + KV_DIM + p * PAIR)
            if not latent:
                nv_ref[:, p * PAIR:(p + 1) * PAIR] = acc
            v_ref[:, p * PAIR:(p + 1) * PAIR] = acc.astype(BF16)

    is_ctx = i < TP // tm
    n_q = D_MODEL // QKV_TN
    for latent in (False, True):
        stream = jnp.logical_not(is_ctx) if latent else is_ctx
        for jq in range(n_q):
            pl.when(jnp.logical_and(stream, j == jq))(functools.partial(q_tile, latent, jq))
        pl.when(jnp.logical_and(stream, j == n_q))(functools.partial(kv_tile, latent))

    @pl.when(jnp.logical_and(i == n_tiles - 1, j == pl.num_programs(1) - 1))
    def _():
        for t in range(1, QKV_AHEAD + 1):
            late = (i + t) % n_buf
            _wait_rows(ys_hbm, ybuf.at[late], sems.at[late], tm)


def _qkv_proj(x_prev, dest, ys, mod_prev, g, mod, w_qkv, q_norm, k_norm, use_norm, rope):
    tm, tn = QKV_TM, QKV_TN
    n_ctx = TP // tm
    n_q = D_MODEL // tn
    tab = lambda: pl.BlockSpec((tm, HEAD_DIM), lambda i, j, d: (jnp.maximum(i - n_ctx, 0) % (DEC_SEQ // tm), 0))
    kv = lambda: pl.BlockSpec((tm, KV_DIM), lambda i, j, d: (i, 0))
    new = lambda: pl.BlockSpec((tm, KV_DIM), lambda i, j, d: (jnp.minimum(i, n_ctx - 1), 0))
    row = lambda: pl.BlockSpec((tm, D_MODEL), lambda i, j, d: (i, 0))
    return pl.pallas_call(
        functools.partial(_qkv_kernel, use_norm=use_norm),
        grid_spec=pltpu.PrefetchScalarGridSpec(
            num_scalar_prefetch=1,
            grid=(T // tm, QKV_DIM // tn),
            in_specs=[
                row(),
                pl.BlockSpec((None, None, 1, D_MODEL), lambda i, j, d: (_cond_row(i * tm), 5, 0, 0)),
                pl.BlockSpec(memory_space=pl.ANY),
                pl.BlockSpec((1, D_MODEL), lambda i, j, d: (0, 0)),
                _mod_spec(tm, 0),
                _mod_spec(tm, 1),
                pl.BlockSpec((D_MODEL, QKV_DIM), lambda i, j, d: (0, 0), pipeline_mode=pl.Buffered(1)),
                pl.BlockSpec((1, HEAD_DIM), lambda i, j, d: (0, 0)),
                pl.BlockSpec((1, HEAD_DIM), lambda i, j, d: (0, 0)),
                tab(), tab(),
            ],
            out_specs=[pl.BlockSpec((tm, tn), lambda i, j, d: (i, jnp.minimum(j, n_q - 1))), kv(), kv(), new(), new(),
                       row()],
            scratch_shapes=[pltpu.VMEM((tm, D_MODEL), BF16), pltpu.VMEM((QKV_AHEAD + 1, tm, HALF_D), jnp.uint32),
                            pltpu.SemaphoreType.DMA((QKV_AHEAD + 1,))],
        ),
        out_shape=[jax.ShapeDtypeStruct((T, D_MODEL), BF16), jax.ShapeDtypeStruct((T, KV_DIM), BF16),
                   jax.ShapeDtypeStruct((T, KV_DIM), BF16), jax.ShapeDtypeStruct((TP, KV_DIM), F32),
                   jax.ShapeDtypeStruct((TP, KV_DIM), F32), jax.ShapeDtypeStruct((T, D_MODEL), F32)],
        compiler_params=_cp(("arbitrary", "arbitrary"), 56),
        name="qkv_proj",
    )(dest, x_prev, mod_prev, ys, g, mod, mod, w_qkv, q_norm.reshape(1, HEAD_DIM), k_norm.reshape(1, HEAD_DIM),
      *rope)


LOG2E = math.log2(math.e)
ATTN_TQ = 256


def _attn_kernel(*refs, tq, seq_len, has_ctx, windowed, has_sink):
    it = iter(refs)
    q_ref, k_ref, v_ref = next(it), next(it), next(it)
    kc_ref, vc_ref = (next(it), next(it)) if has_ctx else (None, None)
    sink_ref = next(it) if has_sink else None
    o_ref = next(it)
    if windowed:
        i = pl.program_id(2)
        span = tq + 2 * WINDOW
        start = pl.multiple_of(jnp.clip(i * tq - WINDOW, 0, seq_len - span), WINDOW)
        qpos = i * tq + lax.broadcasted_iota(jnp.int32, (tq, 1), 0)
        kpos = start + lax.broadcasted_iota(jnp.int32, (1, span), 1)
        segs = [(k_ref, v_ref, pl.ds(start, span), jnp.abs(kpos - qpos) <= WINDOW)]
    else:
        segs = [(k_ref, v_ref, slice(None), None)]
    if has_ctx:
        segs.append((kc_ref, vc_ref, slice(None), None))
    for h in range(KV_GROUP):
        hs = slice(h * HEAD_DIM, (h + 1) * HEAD_DIM)
        qh = q_ref[:, hs]
        scores = []
        m = None
        for kr, _, rows, mask in segs:
            s = lax.dot_general(qh, kr[rows, :], (((1,), (1,)), ((), ())), preferred_element_type=F32)
            if mask is not None:
                s = jnp.where(mask, s, NEG_INF)
            scores.append(s)
            ms = jnp.max(s, axis=-1, keepdims=True)
            m = ms if m is None else jnp.maximum(m, ms)
        if has_sink:
            sk = sink_ref[pl.program_id(1) * KV_GROUP + h]
            m = jnp.maximum(m, sk)
        l = jnp.exp2(sk - m) if has_sink else jnp.zeros_like(m)
        acc = jnp.zeros((tq, HEAD_DIM), F32)
        for (_, vr, rows, _), s in zip(segs, scores):
            p = jnp.exp2(s - m)
            l = l + jnp.sum(p, axis=-1, keepdims=True)
            acc = acc + _dot(p.astype(BF16), vr[rows, :])
        o_ref[:, hs] = (acc / l).astype(BF16)


def _attention(q, row0, k, v, k_row0, n_keys, k_ctx, v_ctx, sink, nb, L, windowed):
    tq = min(ATTN_TQ, L)
    nq = L // tq
    q_blk0 = row0 // tq
    seq0 = k_row0 // n_keys
    own = lambda: pl.BlockSpec((n_keys, HEAD_DIM), lambda b, g, i: (seq0 + b, g))
    in_specs = [pl.BlockSpec((tq, KV_GROUP * HEAD_DIM), lambda b, g, i: (q_blk0 + b * nq + i, g)), own(), own()]
    args = [q, k, v]
    if k_ctx is not None:
        ctx = lambda: pl.BlockSpec((None, PAST_LEN, HEAD_DIM), lambda b, g, i: (b, 0, g))
        in_specs += [ctx(), ctx()]
        args += [k_ctx, v_ctx]
    if sink is not None:
        in_specs.append(pl.BlockSpec(memory_space=pltpu.SMEM))
        args.append(sink.astype(F32) * LOG2E)
    return pl.pallas_call(
        functools.partial(_attn_kernel, tq=tq, seq_len=L, has_ctx=k_ctx is not None, windowed=windowed,
                          has_sink=sink is not None),
        grid=(nb, N_KV_HEADS, nq),
        in_specs=in_specs,
        out_specs=pl.BlockSpec((tq, KV_GROUP * HEAD_DIM), lambda b, g, i: (b * nq + i, g)),
        out_shape=jax.ShapeDtypeStruct((nb * L, D_MODEL), BF16),
        compiler_params=_cp(("parallel", "parallel", "parallel"), 56),
        name="attention",
    )(*args)


def _attn_mixer(pending, g, mod, w_qkv, q_norm, k_norm, use_norm, sink, w_o, cache_k, cache_v, windowed, rope):
    q, k, v, new_k, new_v, x = _qkv_proj(*pending, g, mod, w_qkv.astype(BF16), q_norm, k_norm, use_norm, rope)
    op = _attention(q, 0, k, v, 0, SEQ, None, None, sink, BATCH, SEQ, False)
    kc = cache_k.reshape(DEC_BATCH, PAST_LEN, KV_DIM).astype(BF16)
    vc = cache_v.reshape(DEC_BATCH, PAST_LEN, KV_DIM).astype(BF16)
    if windowed:
        osm = _attention(q, TP, k, v, TP, DEC_SEQ, kc, vc, sink, DEC_BATCH, DEC_SEQ, True)
    else:
        n_keys = DEC_SEQ + PAST_LEN
        both = lambda a, c: jnp.concatenate([a[TP:].reshape(DEC_BATCH, DEC_SEQ, KV_DIM), c], axis=1).reshape(
            DEC_BATCH * n_keys, KV_DIM)
        osm = _attention(q, TP, both(k, kc), both(v, vc), 0, n_keys, None, None, sink, DEC_BATCH, DEC_SEQ, False)
    x = _resid_matmul(op, osm, w_o.astype(BF16), jnp.zeros((1, D_MODEL), F32), x, mod, "attn_out_proj")
    shape = (BATCH, SEQ, N_KV_HEADS, HEAD_DIM)
    return x, new_k.reshape(shape), new_v.reshape(shape)


ROUTE_TILE = 512
ROUTE_ROWS = 32


def _router_kernel(x_ref, g_ref, sh_ref, sc_ref, wr_ref, br_ref, xh_ref, rt_ref, cnt_ref, carry):
    tm = ROUTE_TILE
    i = pl.program_id(0)

    @pl.when(i == 0)
    def _():
        carry[...] = jnp.zeros_like(carry)

    h = _norm_mod(x_ref[...], g_ref[...], sh_ref[...], sc_ref[...])
    xh_ref[:, :HALF_D] = _pack_bf16_pairs(h)
    logits = _dot(h.astype(BF16), wr_ref[...])
    s = _sigmoid(logits.T[:N_EXPERTS, :])
    sb = s + br_ref[...]
    u = [s[e:e + 1, :] for e in range(N_EXPERTS)]
    v = [sb[e:e + 1, :] for e in range(N_EXPERTS)]

    gscore = []
    for gq in range(N_EXPERT_GROUPS):
        m = v[4 * gq:4 * gq + 4]
        best = m[PAIR_LO[0]] + m[PAIR_HI[0]]
        for a, b in zip(PAIR_LO[1:], PAIR_HI[1:]):
            best = jnp.maximum(best, m[a] + m[b])
        gscore.append(best)
    gidx = jnp.zeros((1, tm), jnp.int32)
    gbest = gscore[0]
    for gq in range(1, N_EXPERT_GROUPS):
        upd = gscore[gq] > gbest
        gidx = jnp.where(upd, gq, gidx)
        gbest = jnp.where(upd, gscore[gq], gbest)

    def pick(rows, j):
        out = rows[j]
        for gq in range(1, N_EXPERT_GROUPS):
            out = jnp.where(gidx == gq, rows[4 * gq + j], out)
        return out

    vin = [pick(v, j) for j in range(EXPERTS_PER_GROUP)]
    uin = [pick(u, j) for j in range(EXPERTS_PER_GROUP)]
    i1 = jnp.zeros((1, tm), jnp.int32)
    m1 = vin[0]
    for j in range(1, EXPERTS_PER_GROUP):
        upd = vin[j] > m1
        i1 = jnp.where(upd, j, i1)
        m1 = jnp.where(upd, vin[j], m1)
    i2 = jnp.full((1, tm), -1, jnp.int32)
    m2 = jnp.full((1, tm), -jnp.inf, F32)
    for j in range(EXPERTS_PER_GROUP):
        upd = (i1 != j) & (vin[j] > m2)
        i2 = jnp.where(upd, j, i2)
        m2 = jnp.where(upd, vin[j], m2)

    def sel(rows, idx):
        out = rows[0]
        for j in range(1, EXPERTS_PER_GROUP):
            out = jnp.where(idx == j, rows[j], out)
        return out

    w1, w2 = sel(uin, i1), sel(uin, i2)
    wsum = w1 + w2
    w1, w2 = w1 / wsum, w2 / wsum
    first_lo = i1 < i2
    lo = jnp.where(first_lo, i1, i2)
    hi = jnp.where(first_lo, i2, i1)
    w_lo = jnp.where(first_lo, w1, w2)
    w_hi = jnp.where(first_lo, w2, w1)
    pair = jnp.where(lo == 0, hi - 1, jnp.where(lo == 1, hi + 1, 5))
    bucket = gidx * len(PAIR_LO) + pair

    onehot = (lax.broadcasted_iota(jnp.int32, (ROUTE_ROWS, tm), 0) == bucket)
    tri = (lax.broadcasted_iota(jnp.int32, (tm, tm), 0) <= lax.broadcasted_iota(jnp.int32, (tm, tm), 1))
    cum = _dot(jnp.where(onehot, 1.0, 0.0).astype(BF16), jnp.where(tri, 1.0, 0.0).astype(BF16))
    rank = jnp.sum(jnp.where(onehot, cum - 1.0 + carry[...], 0.0), axis=0, keepdims=True)
    carry[...] = carry[...] + cum[:, tm - 1:tm]
    cnt_ref[...] = jnp.broadcast_to(carry[...], (ROUTE_ROWS, LANE))

    rt_ref[...] = jnp.zeros_like(rt_ref)
    rt_ref[0:1, :] = bucket.astype(F32)
    rt_ref[1:2, :] = rank
    wt = jnp.concatenate([w_lo, w_hi, jnp.zeros((LANE - 2, tm), F32)], axis=0)
    xh_ref[:, HALF_D:] = lax.bitcast_convert_type(wt.T, jnp.uint32)


def _router(x, g, mod, w_router, b_router):
    tm = ROUTE_TILE
    wr = _pad2(w_router, D_MODEL, LANE).astype(BF16)
    return pl.pallas_call(
        _router_kernel,
        grid=(T // tm,),
        in_specs=[
            pl.BlockSpec((tm, D_MODEL), lambda i: (i, 0)),
            pl.BlockSpec((1, D_MODEL), lambda i: (0, 0)),
            _mod_spec(tm, 3),
            _mod_spec(tm, 4),
            pl.BlockSpec((D_MODEL, LANE), lambda i: (0, 0)),
            pl.BlockSpec((N_EXPERTS, 1), lambda i: (0, 0)),
        ],
        out_specs=[
            pl.BlockSpec((tm, XH_W), lambda i: (i, 0)),
            pl.BlockSpec((8, tm), lambda i: (0, i)),
            pl.BlockSpec((ROUTE_ROWS, LANE), lambda i: (0, 0)),
        ],
        out_shape=[
            jax.ShapeDtypeStruct((T, XH_W), jnp.uint32),
            jax.ShapeDtypeStruct((8, T), F32),
            jax.ShapeDtypeStruct((ROUTE_ROWS, LANE), F32),
        ],
        scratch_shapes=[pltpu.VMEM((ROUTE_ROWS, 1), F32)],
        compiler_params=_cp(("arbitrary",)),
        name="moe_router",
    )(x, g, mod, mod, wr, b_router.reshape(N_EXPERTS, 1))


DISPATCH_TILE = 256


DMA_UNROLL = 32


def _invert_kernel(dest_ref, src_ref):
    def clear(s, c):
        src_ref[s] = 0
        return c

    def put(t, c):
        src_ref[dest_ref[t]] = t
        return c

    lax.fori_loop(0, T_PAD, clear, 0, unroll=DMA_UNROLL)
    lax.fori_loop(0, T, put, 0, unroll=DMA_UNROLL)


def _invert(dest):
    return pl.pallas_call(
        _invert_kernel,
        in_specs=[pl.BlockSpec(memory_space=pltpu.SMEM)],
        out_specs=pl.BlockSpec(memory_space=pltpu.SMEM),
        out_shape=jax.ShapeDtypeStruct((T_PAD,), jnp.int32),
        name="moe_invert",
    )(dest)


def _gather_rows(idx_ref, base, src_hbm, buf, sem, tm, static=False):
    def start(r, c):
        pltpu.make_async_copy(src_hbm.at[pl.ds(idx_ref[base + r], 1)], buf.at[pl.ds(r, 1)], sem).start()
        return c

    if static:
        for r in range(tm):
            start(r, 0)
    else:
        lax.fori_loop(0, tm, start, 0, unroll=DMA_UNROLL)


def _wait_rows(src_hbm, buf, sem, tm):
    pltpu.make_async_copy(src_hbm.at[pl.ds(0, tm)], buf, sem).wait()


def _expert_kernel(ea_ref, eb_ref, nv_ref, src_ref, xh_hbm, ga_ref, ua_ref, da_ref, gb_ref, ub_ref, db_ref, y_ref,
                   *scratch):
    tm = MOE_TILE
    j = pl.program_id(0)
    nv = nv_ref[j]
    *bufs, sems = scratch
    n_buf = len(bufs)

    @pl.when(j == 0)
    def _():
        for t in range(MOE_AHEAD):
            _gather_rows(src_ref, t * tm, xh_hbm, bufs[t], sems.at[t], tm)

    def run(p):
        cur, cur_sem = bufs[p], sems.at[p]
        q = (p + MOE_AHEAD) % n_buf
        ahead, ahead_sem = bufs[q], sems.at[q]

        @pl.when(jnp.logical_or(j < MOE_AHEAD, nv_ref[jnp.maximum(j - MOE_AHEAD, 0)] > 0))
        def _():
            _wait_rows(xh_hbm, cur, cur_sem, tm)

        @pl.when(nv > 0)
        def _():
            valid = lax.broadcasted_iota(jnp.int32, (tm, 1), 0) < nv
            x = jnp.where(valid, _unpack_bf16_pairs(cur[:, :HALF_D]), 0.0).astype(BF16)
            wts = jnp.where(valid, lax.bitcast_convert_type(cur[:, HALF_D:], F32), 0.0)
            for r in range(tm):
                pltpu.make_async_copy(xh_hbm.at[pl.ds(src_ref[(j + MOE_AHEAD) * tm + r], 1)],
                                      ahead.at[pl.ds(r, 1)], ahead_sem).start()

            def ffn(g_ref, u_ref, d_ref, w):
                a = _dot(x, g_ref[...])
                h = a * _sigmoid(a) * _dot(x, u_ref[...]) * w
                return _dot(h.astype(BF16), d_ref[...])

            y = ffn(ga_ref, ua_ref, da_ref, wts[:, 0:1]) + ffn(gb_ref, ub_ref, db_ref, wts[:, 1:2])
            y_ref[...] = _pack_bf16_pairs(y)

    for p in range(n_buf):
        pl.when(j % n_buf == p)(functools.partial(run, p))

    @pl.when(nv == 0)
    def _():
        y_ref[...] = jnp.zeros_like(y_ref)


def _experts(layer, tile_ea, tile_eb, tile_nv, src, xh, w_gate, w_up, w_down):
    tm = MOE_TILE
    up = lambda sel: pl.BlockSpec((None, None, D_MODEL, D_EXPERT),
                                  lambda j, ea, eb, nv, sr: (layer, (ea, eb)[sel][j], 0, 0))
    down = lambda sel: pl.BlockSpec((None, None, D_EXPERT, D_MODEL),
                                    lambda j, ea, eb, nv, sr: (layer, (ea, eb)[sel][j], 0, 0))
    return pl.pallas_call(
        _expert_kernel,
        grid_spec=pltpu.PrefetchScalarGridSpec(
            num_scalar_prefetch=4,
            grid=(MOE_TILES,),
            in_specs=[pl.BlockSpec(memory_space=pl.ANY), up(0), up(0), down(0), up(1), up(1), down(1)],
            out_specs=pl.BlockSpec((tm, HALF_D), lambda j, ea, eb, nv, sr: (j, 0)),
            scratch_shapes=[pltpu.VMEM((tm, XH_W), jnp.uint32)] * (MOE_AHEAD + 1)
            + [pltpu.SemaphoreType.DMA((MOE_AHEAD + 1,))],
        ),
        out_shape=jax.ShapeDtypeStruct((T_PAD, HALF_D), jnp.uint32),
        compiler_params=_cp(("arbitrary",), 56),
        name="moe_experts",
    )(tile_ea, tile_eb, tile_nv, src, xh, w_gate, w_up, w_down, w_gate, w_up, w_down)


def _combine_kernel(*refs, final):
    tm = DISPATCH_TILE
    if final:
        dest_ref, x_ref, gt_ref, ys_hbm, fg_ref, oc_ref, ol_ref, buf, sems = refs
    else:
        dest_ref, x_ref, gt_ref, ys_hbm, o_ref, buf, sems = refs
    i = pl.program_id(0)
    slot = i % 2

    last = pl.num_programs(0) - 1

    @pl.when(i == 0)
    def _():
        _gather_rows(dest_ref, 0, ys_hbm, buf.at[0], sems.at[0], tm)

    def finish():
        _wait_rows(ys_hbm, buf.at[slot], sems.at[slot], tm)
        x = x_ref[...] + gt_ref[...] * _unpack_bf16_pairs(buf[slot])
        if not final:
            o_ref[...] = x
            return
        y = x * lax.rsqrt(jnp.mean(x * x, axis=-1, keepdims=True) + NORM_EPS) * fg_ref[...]
        is_ctx = i < TP // tm

        @pl.when(is_ctx)
        def _():
            oc_ref[...] = y

        @pl.when(jnp.logical_not(is_ctx))
        def _():
            ol_ref[...] = y

    @pl.when(i < last)
    def _():
        _gather_rows(dest_ref, (i + 1) * tm, ys_hbm, buf.at[1 - slot], sems.at[1 - slot], tm, static=True)
        finish()

    pl.when(i == last)(finish)


def _combine(dest, x, mod, ys, final_gain=None):
    tm = DISPATCH_TILE
    final = final_gain is not None
    n_ctx = TP // tm
    in_specs = [pl.BlockSpec((tm, D_MODEL), lambda i, d: (i, 0)),
                pl.BlockSpec((None, None, 1, D_MODEL), lambda i, d: (_cond_row(i * tm), 5, 0, 0)),
                pl.BlockSpec(memory_space=pl.ANY)]
    args = [dest, x, mod, ys]
    if final:
        in_specs.append(pl.BlockSpec((1, D_MODEL), lambda i, d: (0, 0)))
        args.append(final_gain.reshape(1, D_MODEL))
        out_specs = [pl.BlockSpec((tm, D_MODEL), lambda i, d: (jnp.minimum(i, n_ctx - 1), 0)),
                     pl.BlockSpec((tm, D_MODEL), lambda i, d: (jnp.maximum(i - n_ctx, 0), 0))]
        out_shape = [jax.ShapeDtypeStruct((TP, D_MODEL), F32), jax.ShapeDtypeStruct((TS, D_MODEL), F32)]
    else:
        out_specs = pl.BlockSpec((tm, D_MODEL), lambda i, d: (i, 0))
        out_shape = jax.ShapeDtypeStruct((T, D_MODEL), F32)
    return pl.pallas_call(
        functools.partial(_combine_kernel, final=final),
        grid_spec=pltpu.PrefetchScalarGridSpec(
            num_scalar_prefetch=1,
            grid=(T // tm,),
            in_specs=in_specs,
            out_specs=out_specs,
            scratch_shapes=[pltpu.VMEM((2, tm, HALF_D), jnp.uint32), pltpu.SemaphoreType.DMA((2,))],
        ),
        out_shape=out_shape,
        compiler_params=_cp(("arbitrary",)),
        name="moe_combine",
    )(*args)


def _lookup(table, idx):
    n = table.shape[0]
    hit = idx[:, None] == jnp.arange(n, dtype=jnp.int32)[None, :]
    return jnp.sum(jnp.where(hit, table[None, :], 0), axis=1)


def _moe_plan(rt, cnt):
    bucket = rt[0].astype(jnp.int32)
    rank = rt[1].astype(jnp.int32)
    counts = cnt[:N_BUCKETS, 0].astype(jnp.int32)
    tiles = (counts + MOE_TILE - 1) // MOE_TILE
    order = jnp.arange(N_BUCKETS, dtype=jnp.int32)
    tile_start = jnp.sum(jnp.where(order[None, :] < order[:, None], tiles[None, :], 0), axis=1)
    tile_end = tile_start + tiles
    n_used = tile_end[N_BUCKETS - 1]
    dest = _lookup(tile_start * MOE_TILE, bucket) + rank
    j = jnp.arange(MOE_TILES, dtype=jnp.int32)
    jc = jnp.minimum(j, n_used - 1)
    b = jnp.minimum(jnp.sum((jc[:, None] >= tile_end[None, :]).astype(jnp.int32), axis=1), N_BUCKETS - 1)
    nv = jnp.clip(_lookup(counts, b) - (j - _lookup(tile_start, b)) * MOE_TILE, 0, MOE_TILE)
    nv = jnp.where(j < n_used, nv, 0)
    n_pairs = len(PAIR_LO)
    ea = (b // n_pairs) * EXPERTS_PER_GROUP + _lookup(jnp.asarray(PAIR_LO, jnp.int32), b % n_pairs)
    eb = (b // n_pairs) * EXPERTS_PER_GROUP + _lookup(jnp.asarray(PAIR_HI, jnp.int32), b % n_pairs)
    return dest, ea, eb, nv


def _moe(layer, x, g, mod, w_router, b_router, w_gate, w_up, w_down, final_gain=None, defer=False):
    xh, rt, cnt = _router(x, g, mod, w_router, b_router)
    dest, ea, eb, nv = _moe_plan(rt, cnt)
    ys = _experts(layer, ea, eb, nv, _invert(dest), xh, w_gate, w_up, w_down)
    if defer:
        return x, dest, ys, mod
    return _combine(dest, x, mod, ys, final_gain)


def kernel(x_prompt, x_sample, cache_k_full, cache_v_full, cache_k_win, cache_v_win, c, c_ctx, w_mod, b_mod, norm_mix, norm_ffn, final_norm, pool_w, pool_scale, hy_w_in, hy_b_in, hy_conv_w, hy_conv_b, hy_f_w1, hy_f_b1, hy_f_w2, hy_f_b2, hy_f_freq, hy_f_w3, hy_decay, hy_skip, hy_w_out, hy_b_out, fa_w_qkv, fa_q_norm, fa_k_norm, fa_w_o, wa_w_qkv, wa_sink, wa_w_o, w_router, b_router, moe_w_gate, moe_w_up, moe_w_down):
    x = None
    cond =jnp.concatenate([c_ctx[None, :], c, jnp.zeros((N_COND - 1 - DEC_BATCH, D_MODEL), F32)], axis=0)
    mods = _adaln(cond, w_mod, b_mod).reshape(DEPTH, N_COND, 6, 1, D_MODEL)
    rope = _rope_tables()
    ones_hd = jnp.ones((HEAD_DIM,), F32)
    wg_bf, wu_bf, wd_bf = moe_w_gate.astype(BF16), moe_w_up.astype(BF16), moe_w_down.astype(BF16)
    new_kv = {}
    for layer in range(DEPTH):
        kind = layer % 4
        j = layer // 4
        mod = mods[layer]
        g_mix = norm_mix[layer].reshape(1, D_MODEL)
        if kind == 0:
            assert layer == 0, "the pooling mixer reads the two input streams, so it must be the first layer"
            x = _pool_mixer(x_prompt.reshape(TP, D_MODEL), x_sample.reshape(TS, D_MODEL), g_mix, mod,
                            pool_w[j], pool_scale[j])
        elif kind == 1:
            x = _hyena_mixer(x, g_mix, mod, hy_w_in[j], hy_b_in[j], hy_conv_w[j], hy_conv_b[j], hy_f_w1[j],
                             hy_f_b1[j], hy_f_w2[j], hy_f_b2[j], hy_f_freq[j], hy_f_w3[j], hy_decay[j],
                             hy_skip[j], hy_w_out[j], hy_b_out[j])
        elif kind == 2:
            x, nk, nv = _attn_mixer(x, g_mix, mod, fa_w_qkv[j], fa_q_norm[j], fa_k_norm[j], True, None,
                                    fa_w_o[j], cache_k_full[:, j], cache_v_full[:, j], False, rope)
            new_kv.setdefault("kf", []).append(nk)
            new_kv.setdefault("vf", []).append(nv)
        else:
            x, nk, nv = _attn_mixer(x, g_mix, mod, wa_w_qkv[j], ones_hd, ones_hd, False, wa_sink[j],
                                    wa_w_o[j], cache_k_win[:, j], cache_v_win[:, j], True, rope)
            new_kv.setdefault("kw", []).append(nk)
            new_kv.setdefault("vw", []).append(nv)
        defer = layer + 1 < DEPTH and (layer + 1) % 4 in (1, 2, 3)
        x = _moe(layer, x, norm_ffn[layer].reshape(1, D_MODEL), mod, w_router, b_router, wg_bf, wu_bf, wd_bf,
                 final_norm if layer == DEPTH - 1 else None, defer)
    y_prompt, y_sample = x
    y_prompt = y_prompt.reshape(BATCH, SEQ, D_MODEL)
    y_sample = y_sample.reshape(DEC_BATCH, DEC_SEQ, D_MODEL)
    return (y_prompt, y_sample, jnp.stack(new_kv["kf"], axis=1), jnp.stack(new_kv["vf"], axis=1),
            jnp.stack(new_kv["kw"], axis=1), jnp.stack(new_kv["vw"], axis=1))
```
